```python
import jax, jax.numpy as jnp
from jax import lax
import numpy as np

D_MODEL = 1024
BATCH = 8
SEQ = 2048
DEPTH = 1
DEC_BATCH = 128
DEC_SEQ = 1
PAST_LEN = 16384
PAGE_SIZE = 128

POOL_WIDTH = D_MODEL // 2
POOL_WINDOWS = (2, 4, 8, 16)
N_POOL_GROUPS = len(POOL_WINDOWS)
POOL_GROUP = POOL_WIDTH // N_POOL_GROUPS
POOL_BUF = max(POOL_WINDOWS) - 1
CONV_WIDTH = D_MODEL - POOL_WIDTH
CONV_K = 31
CONV_BUF = CONV_K - 1
D_MIX = POOL_WIDTH + CONV_WIDTH
D_IN = POOL_WIDTH + 2 * CONV_WIDTH
N_EXPERTS = 64
N_EXPERT_GROUPS = 8
TOPK_GROUPS = 4
TOP_K = 8
D_EXPERT = D_MODEL // 4
ROUTED_SCALE = 2.5
EXPERT_BLOCK = 128
EPS = 1e-6

kernel_name = "hybrid_pool_conformer_moe_adaln_step"


def rms_norm(x, g):
    xf = x.astype(jnp.float32)
    y = xf * lax.rsqrt(jnp.mean(xf * xf, -1, keepdims=True) + EPS)
    return (y * g.astype(jnp.float32)).astype(x.dtype)


def modulate(h, shift, scale):
    return h * (1.0 + scale[:, None, :]) + shift[:, None, :]


def multi_scale_pool(u, prev, prev_valid):
    L = u.shape[1]
    full = jnp.concatenate([prev, u], axis=1)
    ff = full.astype(jnp.float32)
    cs = jnp.concatenate([jnp.zeros_like(ff[:, :1]), jnp.cumsum(ff, axis=1)], axis=1)
    end = cs[:, POOL_BUF + 1:]
    t = jnp.arange(L)
    outs = []
    for g, w in enumerate(POOL_WINDOWS):
        sl = slice(g * POOL_GROUP, (g + 1) * POOL_GROUP)
        start = cs[:, POOL_BUF + 1 - w: POOL_BUF + 1 - w + L, sl]
        cnt = jnp.minimum(w, t + 1 + prev_valid).astype(jnp.float32)
        outs.append((end[..., sl] - start) / cnt[None, :, None])
    mean = jnp.concatenate(outs, axis=-1)
    d = (mean - u.astype(jnp.float32)).astype(u.dtype)
    return d, full[:, -POOL_BUF:]


def causal_depthwise_conv(a, prev, w_dw, b_dw):
    full = jnp.concatenate([prev, a], axis=1)
    y = lax.conv_general_dilated(full, w_dw[:, None, :].astype(full.dtype), window_strides=(1,),
                                 padding='VALID', dimension_numbers=('NWC', 'WIO', 'NWC'),
                                 feature_group_count=a.shape[-1])
    return y + b_dw, full[:, -CONV_BUF:]


def route(hf, w_router, b_router):
    T = hf.shape[0]
    s = jax.nn.sigmoid(hf.astype(jnp.float32) @ w_router.astype(jnp.float32))
    sel = s + b_router.astype(jnp.float32)
    per = N_EXPERTS // N_EXPERT_GROUPS
    gscore = lax.top_k(sel.reshape(T, N_EXPERT_GROUPS, per), 2)[0].sum(-1)
    _, gidx = lax.top_k(gscore, TOPK_GROUPS)
    gmask = jax.nn.one_hot(gidx, N_EXPERT_GROUPS, dtype=jnp.float32).sum(1) > 0
    sel_m = jnp.where(jnp.repeat(gmask, per, axis=1), sel, -jnp.inf)
    _, idx = lax.top_k(sel_m, TOP_K)
    w = jnp.take_along_axis(s, idx, axis=1)
    w = w / jnp.sum(w, -1, keepdims=True) * ROUTED_SCALE
    return idx, w


def routed_experts(hf, idx, wts, w_gate, w_up, w_down):
    T, D = hf.shape
    A = T * TOP_K
    flat_e = idx.reshape(-1)
    flat_t = jnp.repeat(jnp.arange(T, dtype=jnp.int32), TOP_K)
    flat_w = wts.reshape(-1)
    order = jnp.argsort(flat_e)
    e_s, t_s, w_s = flat_e[order], flat_t[order], flat_w[order]
    counts = jnp.bincount(flat_e, length=N_EXPERTS)
    pcounts = (counts + EXPERT_BLOCK - 1) // EXPERT_BLOCK * EXPERT_BLOCK
    pends = jnp.cumsum(pcounts)
    pstart = pends - pcounts
    start = jnp.cumsum(counts) - counts
    dest = pstart[e_s] + (jnp.arange(A) - start[e_s])
    n_blocks = (A + EXPERT_BLOCK - 1) // EXPERT_BLOCK + N_EXPERTS
    P = n_blocks * EXPERT_BLOCK
    tok = jnp.full((P,), T, jnp.int32).at[dest].set(t_s)
    gate = jnp.zeros((P,), hf.dtype).at[dest].set(w_s.astype(hf.dtype))
    block_e = jnp.minimum(jnp.searchsorted(pends, jnp.arange(n_blocks) * EXPERT_BLOCK, side='right'),
                          N_EXPERTS - 1)
    xpad = jnp.concatenate([hf, jnp.zeros((1, D), hf.dtype)], axis=0)

    def one_block(args):
        tok_b, gate_b, e = args
        xb = xpad[tok_b]
        hb = jax.nn.silu(xb @ w_gate[e]) * (xb @ w_up[e])
        return (hb @ w_down[e]) * gate_b[:, None]

    out = lax.map(one_block, (tok.reshape(n_blocks, EXPERT_BLOCK), gate.reshape(n_blocks, EXPERT_BLOCK), block_e))
    return jax.ops.segment_sum(out.reshape(P, D), tok, num_segments=T + 1)[:T]


def layer(x, c, pool_prev, conv_prev, prev_valid, w_ada, b_ada, g_mix, w_in, w_pool, pool_scale,
          w_dw, b_dw, ln_g, ln_b, w_out, g_ffn, w_router, b_router, w_gate, w_up, w_down,
          ws_gate, ws_up, ws_down):
    B, L, D = x.shape
    mod = jax.nn.silu(c) @ w_ada + b_ada
    sh1, sc1, g1, sh2, sc2, g2 = jnp.split(mod, 6, axis=-1)
    h = modulate(rms_norm(x, g_mix), sh1, sc1)
    z = h @ w_in
    u = z[..., :POOL_WIDTH]
    v = z[..., POOL_WIDTH:POOL_WIDTH + CONV_WIDTH]
    gt = z[..., POOL_WIDTH + CONV_WIDTH:]
    d, new_pool = multi_scale_pool(u, pool_prev, prev_valid)
    d = d.reshape(B, L, N_POOL_GROUPS, POOL_GROUP)
    pool_out = jnp.einsum('blgc,gcd->blgd', d, w_pool).reshape(B, L, POOL_WIDTH) * pool_scale
    a = v * jax.nn.sigmoid(gt)
    yc, new_conv = causal_depthwise_conv(a, conv_prev, w_dw, b_dw)
    yf = yc.astype(jnp.float32)
    mu = jnp.mean(yf, -1, keepdims=True)
    var = jnp.mean(jnp.square(yf - mu), -1, keepdims=True)
    yn = (yf - mu) * lax.rsqrt(var + EPS) * ln_g.astype(jnp.float32) + ln_b.astype(jnp.float32)
    conv_out = jax.nn.silu(yn).astype(x.dtype)
    mix = jnp.concatenate([pool_out, conv_out], axis=-1) @ w_out
    x = x + g1[:, None, :] * mix
    h2 = modulate(rms_norm(x, g_ffn), sh2, sc2)
    hf = h2.reshape(B * L, D)
    idx, wts = route(hf, w_router, b_router)
    routed = routed_experts(hf, idx, wts, w_gate, w_up, w_down)
    shared = (jax.nn.silu(hf @ ws_gate) * (hf @ ws_up)) @ ws_down
    x = x + g2[:, None, :] * (routed + shared).reshape(B, L, D)
    return x, new_pool, new_conv


def setup_inputs(seed: int = 0) -> dict:
    key = jax.random.key(seed)
    ks = jax.random.split(key, 32)
    f32 = jnp.float32
    n = lambda k, shape, s: jax.random.normal(k, shape, f32) * s
    D = D_MODEL
    return {
        "x_prompt": n(ks[0], (BATCH, SEQ, D), 1.0),
        "x_sample": n(ks[1], (DEC_BATCH, DEC_SEQ, D), 1.0),
        "state_pool": n(ks[2], (DEPTH, DEC_BATCH, POOL_BUF, POOL_WIDTH), 1.0),
        "state_conv": n(ks[3], (DEPTH, DEC_BATCH, CONV_BUF, CONV_WIDTH), 1.0),
        "c_prompt": n(ks[4], (BATCH, D), 1.0),
        "c_sample": n(ks[5], (DEC_BATCH, D), 1.0),
        "w_ada": n(ks[6], (DEPTH, D, 6 * D), 0.5 * D ** -0.5),
        "b_ada": n(ks[7], (DEPTH, 6 * D), 0.01),
        "g_mix": 1.0 + n(ks[8], (DEPTH, D), 0.1),
        "w_in": n(ks[9], (DEPTH, D, D_IN), D ** -0.5),
        "w_pool": n(ks[10], (DEPTH, N_POOL_GROUPS, POOL_GROUP, POOL_GROUP), POOL_GROUP ** -0.5),
        "pool_scale": 1.0 + n(ks[11], (DEPTH, POOL_WIDTH), 0.1),
        "w_dw": n(ks[12], (DEPTH, CONV_K, CONV_WIDTH), CONV_K ** -0.5),
        "b_dw": n(ks[13], (DEPTH, CONV_WIDTH), 0.01),
        "ln_g": 1.0 + n(ks[14], (DEPTH, CONV_WIDTH), 0.1),
        "ln_b": n(ks[15], (DEPTH, CONV_WIDTH), 0.01),
        "w_out": n(ks[16], (DEPTH, D_MIX, D), D_MIX ** -0.5),
        "g_ffn": 1.0 + n(ks[17], (DEPTH, D), 0.1),
        "w_router": n(ks[18], (DEPTH, D, N_EXPERTS), D ** -0.5),
        "b_router": n(ks[19], (DEPTH, N_EXPERTS), 0.01),
        "w_gate": n(ks[20], (DEPTH, N_EXPERTS, D, D_EXPERT), D ** -0.5),
        "w_up": n(ks[21], (DEPTH, N_EXPERTS, D, D_EXPERT), D ** -0.5),
        "w_down": n(ks[22], (DEPTH, N_EXPERTS, D_EXPERT, D), D_EXPERT ** -0.5),
        "ws_gate": n(ks[23], (DEPTH, D, D_EXPERT), D ** -0.5),
        "ws_up": n(ks[24], (DEPTH, D, D_EXPERT), D ** -0.5),
        "ws_down": n(ks[25], (DEPTH, D_EXPERT, D), D_EXPERT ** -0.5),
        "w_ada_final": n(ks[26], (D, 2 * D), 0.5 * D ** -0.5),
        "b_ada_final": n(ks[27], (2 * D,), 0.01),
        "g_final": 1.0 + n(ks[28], (D,), 0.1),
    }


def reference(x_prompt, x_sample, state_pool, state_conv, c_prompt, c_sample, w_ada, b_ada, g_mix,
              w_in, w_pool, pool_scale, w_dw, b_dw, ln_g, ln_b, w_out, g_ffn, w_router, b_router,
              w_gate, w_up, w_down, ws_gate, ws_up, ws_down, w_ada_final, b_ada_final, g_final):
    xp, xs = x_prompt, x_sample
    Bp = xp.shape[0]
    sample_valid = min(PAST_LEN, POOL_BUF)
    pool_p, conv_p, pool_s, conv_s = [], [], [], []
    for l in range(DEPTH):
        lw = (w_ada[l], b_ada[l], g_mix[l], w_in[l], w_pool[l], pool_scale[l], w_dw[l], b_dw[l],
              ln_g[l], ln_b[l], w_out[l], g_ffn[l], w_router[l], b_router[l], w_gate[l], w_up[l],
              w_down[l], ws_gate[l], ws_up[l], ws_down[l])
        zp = jnp.zeros((Bp, POOL_BUF, POOL_WIDTH), xp.dtype)
        zc = jnp.zeros((Bp, CONV_BUF, CONV_WIDTH), xp.dtype)
        xp, npp, ncp = layer(xp, c_prompt, zp, zc, 0, *lw)
        xs, nps, ncs = layer(xs, c_sample, state_pool[l], state_conv[l], sample_valid, *lw)
        pool_p.append(npp); conv_p.append(ncp); pool_s.append(nps); conv_s.append(ncs)

    def final(x, c):
        shf, scf = jnp.split(jax.nn.silu(c) @ w_ada_final + b_ada_final, 2, axis=-1)
        return modulate(rms_norm(x, g_final), shf, scf)

    y_prompt = final(xp, c_prompt)
    y_sample = final(xs, c_sample)
    return (y_prompt, y_sample, jnp.stack(pool_p), jnp.stack(conv_p), jnp.stack(pool_s), jnp.stack(conv_s))
```

```python
import functools

import jax
import jax.numpy as jnp
from jax import lax
from jax.experimental import pallas as pl
from jax.experimental.pallas import tpu as pltpu

POOL_WINDOWS = (2, 4, 8, 16)
N_EXPERT_GROUPS = 8
TOPK_GROUPS = 4
TOP_K = 8
ROUTED_SCALE = 2.5
EPS = 1e-6

LANES = 128
SUBLANES = 8
BF16_ROWS = 16
VMEM_LIMIT = 52 * 1024 * 1024

SEQ_TILE = 512
ROW_CHUNK = 16
ROUTE_TILE = 384
EXPERT_ROWS = 256
FINAL_TILE = 512

F32 = jnp.float32
BF16 = jnp.bfloat16
NEG_INF = float("-inf")


def _sigmoid(x):
    return 1.0 / (1.0 + jnp.exp(-x))


def _silu(x):
    return x * _sigmoid(x)


def _rms_norm(x, g):
    return x * lax.rsqrt(jnp.mean(x * x, axis=-1, keepdims=True) + EPS) * g


def _dot(a, b):
    return jnp.dot(a, b, preferred_element_type=F32)


def _ada_kernel(c_ref, w_ref, b_ref, o_ref):
    s = _silu(c_ref[...])
    o_ref[...] = _dot(s.astype(BF16), w_ref[...].astype(BF16)) + b_ref[...]


def _ada(c, w, b):
    rows, d = c.shape
    n = w.shape[1]
    tn = 1024
    return pl.pallas_call(
        _ada_kernel,
        grid=(n // tn,),
        in_specs=[pl.BlockSpec((rows, d), lambda j: (0, 0)),
                  pl.BlockSpec((d, tn), lambda j: (0, j)),
                  pl.BlockSpec((1, tn), lambda j: (0, j))],
        out_specs=pl.BlockSpec((rows, tn), lambda j: (0, j)),
        out_shape=jax.ShapeDtypeStruct((rows, n), F32),
        compiler_params=pltpu.CompilerParams(dimension_semantics=("arbitrary",),
                                             vmem_limit_bytes=VMEM_LIMIT),
        name="ada",
    )(c, w, b.reshape(1, n))


def _mixer_tail(x, pool_d, conv_act, mod, w, xmid_ref, h2_ref, lgt_ref):
    sh2, sc2, g1, g2 = mod
    pw = pool_d.shape[1]
    pool_out = _dot(pool_d, w["pool"][...]) * w["pool_scale"][...]
    mix = _dot(pool_out.astype(BF16), w["out"][:pw, :]) + _dot(conv_act, w["out"][pw:, :])
    x1 = x + g1 * mix
    h2 = (_rms_norm(x1, w["g_ffn"][...]) * (1.0 + sc2) + sh2).astype(BF16)
    h2_ref[...] = h2
    lgt_ref[...] = lax.dot_general(w["router_t"][...], h2, (((1,), (1,)), ((), ())),
                                   preferred_element_type=F32)
    gu = _dot(h2, w["s_gu"][...])
    de = gu.shape[1] // 2
    hs = _silu(gu[:, :de]) * gu[:, de:]
    shared = _dot(hs.astype(BF16), w["s_down"][...])
    xmid_ref[...] = x1 + g2 * shared


_W_NAMES = ("g_mix", "in", "pool", "pool_scale", "dw", "b_dw", "ln_g", "ln_b", "out", "g_ffn",
            "router_t", "s_gu", "s_down")


def _layer_norm_silu(yc, g, b):
    mu = jnp.mean(yc, axis=-1, keepdims=True)
    yz = yc - mu
    var = jnp.mean(yz * yz, axis=-1, keepdims=True)
    return _silu(yz * lax.rsqrt(var + EPS) * g + b)


def _prompt_mixer_kernel(x_ref, mod_ref, *refs, tl, d, pw, cw, conv_k):
    nw = len(_W_NAMES)
    w = dict(zip(_W_NAMES, refs[:nw]))
    xmid_ref, h2_ref, lgt_ref, npool_ref, nconv_ref = refs[nw:nw + 5]
    ubuf, abuf, dbuf, cbuf = refs[nw + 5:]
    l = pl.program_id(1)
    pool_buf = max(POOL_WINDOWS) - 1
    uh, ah = 16, 32
    pg = pw // len(POOL_WINDOWS)

    @pl.when(l == 0)
    def _():
        ubuf[0:uh, :] = jnp.zeros((uh, pw), F32)
        abuf[0:ah, :] = jnp.zeros((ah, cw), F32)

    @pl.when(l > 0)
    def _():
        ubuf[0:uh, :] = ubuf[tl:tl + uh, :]
        abuf[0:ah, :] = abuf[tl:tl + ah, :]

    x = x_ref[0]
    sh1 = mod_ref[0, :, 0 * d:1 * d]
    sc1 = mod_ref[0, :, 1 * d:2 * d]
    g1 = mod_ref[0, :, 2 * d:3 * d]
    sh2 = mod_ref[0, :, 3 * d:4 * d]
    sc2 = mod_ref[0, :, 4 * d:5 * d]
    g2 = mod_ref[0, :, 5 * d:6 * d]

    h = _rms_norm(x, w["g_mix"][...]) * (1.0 + sc1) + sh1
    z = _dot(h.astype(BF16), w["in"][...])
    ubuf[uh:uh + tl, :] = z[:, :pw]
    abuf[ah:ah + tl, :] = z[:, pw:pw + cw] * _sigmoid(z[:, pw + cw:])

    row0 = l * tl
    for c in range(tl // ROW_CHUNK):
        r = c * ROW_CHUNK
        t = row0 + r + lax.broadcasted_iota(jnp.int32, (ROW_CHUNK, 1), 0)
        u_c = ubuf[uh + r:uh + r + ROW_CHUNK, :]
        for g, win in enumerate(POOL_WINDOWS):
            cols = slice(g * pg, (g + 1) * pg)
            acc = u_c[:, cols]
            for j in range(1, win):
                acc = acc + ubuf[uh + r - j:uh + r - j + ROW_CHUNK, cols]
            inv = 1.0 / jnp.minimum(win, t + 1).astype(F32)
            dbuf[r:r + ROW_CHUNK, cols] = (acc * inv - u_c[:, cols]).astype(BF16)
        yc = abuf[ah + r - (conv_k - 1):ah + r - (conv_k - 1) + ROW_CHUNK, :] * w["dw"][0:1, :]
        for k in range(1, conv_k):
            o = ah + r - (conv_k - 1) + k
            yc = yc + abuf[o:o + ROW_CHUNK, :] * w["dw"][k:k + 1, :]
        yc = yc + w["b_dw"][...]
        cbuf[r:r + ROW_CHUNK, :] = _layer_norm_silu(yc, w["ln_g"][...], w["ln_b"][...]).astype(BF16)

    @pl.when(l == pl.num_programs(1) - 1)
    def _():
        npool_ref[0] = ubuf[uh + tl - pool_buf:uh + tl, :]
        nconv_ref[0] = abuf[ah + tl - (conv_k - 1):ah + tl, :]

    _mixer_tail(x, dbuf[...], cbuf[...], (sh2, sc2, g1, g2), w, xmid_ref.at[0], h2_ref.at[0], lgt_ref)


def _sample_mixer_kernel(x_ref, mod_ref, sp_ref, sc_ref, *refs, d, pw, cw, conv_k):
    nw = len(_W_NAMES)
    w = dict(zip(_W_NAMES, refs[:nw]))
    xmid_ref, h2_ref, lgt_ref, u_ref, a_ref = refs[nw:nw + 5]
    pool_buf = max(POOL_WINDOWS) - 1
    pg = pw // len(POOL_WINDOWS)
    x = x_ref[...]
    sh1, sc1, g1, sh2, sc2, g2 = [mod_ref[:, i * d:(i + 1) * d] for i in range(6)]
    h = _rms_norm(x, w["g_mix"][...]) * (1.0 + sc1) + sh1
    z = _dot(h.astype(BF16), w["in"][...])
    u = z[:, :pw]
    a = z[:, pw:pw + cw] * _sigmoid(z[:, pw + cw:])
    u_ref[...] = u
    a_ref[...] = a
    ds = []
    for g, win in enumerate(POOL_WINDOWS):
        cols = slice(g * pg, (g + 1) * pg)
        acc = u[:, cols]
        for j in range(1, win):
            acc = acc + sp_ref[pool_buf - j, :, cols]
        cnt = float(min(win, 1 + pool_buf))
        ds.append(acc / cnt - u[:, cols])
    pool_d = jnp.concatenate(ds, axis=-1).astype(BF16)
    yc = a * w["dw"][conv_k - 1:conv_k, :]
    for k in range(conv_k - 1):
        yc = yc + sc_ref[k] * w["dw"][k:k + 1, :]
    yc = yc + w["b_dw"][...]
    conv_act = _layer_norm_silu(yc, w["ln_g"][...], w["ln_b"][...]).astype(BF16)
    _mixer_tail(x, pool_d, conv_act, (sh2, sc2, g1, g2), w, xmid_ref, h2_ref, lgt_ref)


def _full_spec(a):
    nd = a.ndim
    return pl.BlockSpec(a.shape, lambda *_: (0,) * nd)


def _prompt_mixer(x, mod, wl, *, conv_k):
    b, seq, d = x.shape
    pw = wl["pool"].shape[0]
    cw = wl["dw"].shape[1]
    ne = wl["router_t"].shape[0]
    tl = SEQ_TILE
    nl = seq // tl
    ws = [wl[n] for n in _W_NAMES]
    kern = functools.partial(_prompt_mixer_kernel, tl=tl, d=d, pw=pw, cw=cw, conv_k=conv_k)
    return pl.pallas_call(
        kern,
        grid=(b, nl),
        in_specs=[pl.BlockSpec((1, tl, d), lambda i, j: (i, j, 0)),
                  pl.BlockSpec((1, 1, mod.shape[-1]), lambda i, j: (i, 0, 0))]
                 + [_full_spec(a) for a in ws],
        out_specs=[pl.BlockSpec((1, tl, d), lambda i, j: (i, j, 0)),
                   pl.BlockSpec((1, tl, d), lambda i, j: (i, j, 0)),
                   pl.BlockSpec((ne, tl), lambda i, j: (0, i * nl + j)),
                   pl.BlockSpec((1, max(POOL_WINDOWS) - 1, pw), lambda i, j: (i, 0, 0)),
                   pl.BlockSpec((1, conv_k - 1, cw), lambda i, j: (i, 0, 0))],
        out_shape=[jax.ShapeDtypeStruct((b, seq, d), F32),
                   jax.ShapeDtypeStruct((b, seq, d), BF16),
                   jax.ShapeDtypeStruct((ne, b * seq), F32),
                   jax.ShapeDtypeStruct((b, max(POOL_WINDOWS) - 1, pw), F32),
                   jax.ShapeDtypeStruct((b, conv_k - 1, cw), F32)],
        scratch_shapes=[pltpu.VMEM((16 + tl, pw), F32), pltpu.VMEM((32 + tl, cw), F32),
                        pltpu.VMEM((tl, pw), BF16), pltpu.VMEM((tl, cw), BF16)],
        compiler_params=pltpu.CompilerParams(dimension_semantics=("arbitrary", "arbitrary"),
                                             vmem_limit_bytes=VMEM_LIMIT),
        name="prompt_mixer",
    )(x, mod.reshape(b, 1, -1), *ws)


def _sample_mixer(x, mod, sp_t, sc_t, wl, *, conv_k):
    rows, d = x.shape
    pw = wl["pool"].shape[0]
    cw = wl["dw"].shape[1]
    ne = wl["router_t"].shape[0]
    ws = [wl[n] for n in _W_NAMES]
    kern = functools.partial(_sample_mixer_kernel, d=d, pw=pw, cw=cw, conv_k=conv_k)
    ins = [x, mod, sp_t, sc_t] + ws
    return pl.pallas_call(
        kern,
        grid=(1,),
        in_specs=[_full_spec(a) for a in ins],
        out_specs=[pl.BlockSpec((rows, d), lambda i: (0, 0)),
                   pl.BlockSpec((rows, d), lambda i: (0, 0)),
                   pl.BlockSpec((ne, rows), lambda i: (0, 0)),
                   pl.BlockSpec((rows, pw), lambda i: (0, 0)),
                   pl.BlockSpec((rows, cw), lambda i: (0, 0))],
        out_shape=[jax.ShapeDtypeStruct((rows, d), F32),
                   jax.ShapeDtypeStruct((rows, d), BF16),
                   jax.ShapeDtypeStruct((ne, rows), F32),
                   jax.ShapeDtypeStruct((rows, pw), F32),
                   jax.ShapeDtypeStruct((rows, cw), F32)],
        compiler_params=pltpu.CompilerParams(dimension_semantics=("arbitrary",),
                                             vmem_limit_bytes=VMEM_LIMIT),
        name="sample_mixer",
    )(*ins)


def _sublane_max(x):
    return jnp.max(x, axis=0, keepdims=True)


def _route_kernel(lg_ref, b_ref, tri_ref, idx_ref, wt_ref, rank_ref, cnt_ref, carry):
    i = pl.program_id(0)
    ne, tt = lg_ref.shape
    ng = N_EXPERT_GROUPS
    per = ne // ng

    @pl.when(i == 0)
    def _():
        carry[...] = jnp.zeros_like(carry)

    s = _sigmoid(lg_ref[...])
    sel = s + b_ref[...]
    s3 = [s[p * ng:(p + 1) * ng, :] for p in range(per)]
    sel3 = [sel[p * ng:(p + 1) * ng, :] for p in range(per)]
    m1 = sel3[0]
    m2 = jnp.full_like(m1, NEG_INF)
    for p in range(1, per):
        m2 = jnp.maximum(m2, jnp.minimum(m1, sel3[p]))
        m1 = jnp.maximum(m1, sel3[p])
    gs = m1 + m2
    gi = lax.broadcasted_iota(jnp.int32, (ng, tt), 0)
    beaten = jnp.zeros((ng, tt), jnp.int32)
    for g in range(ng):
        row = gs[g:g + 1, :]
        beats = (row > gs) | ((row == gs) & (gi > g))
        beaten = beaten + beats.astype(jnp.int32)
    keep = beaten < TOPK_GROUPS
    cur = [jnp.where(keep, sel3[p], NEG_INF) for p in range(per)]
    eid = [(gi * per + p).astype(F32) for p in range(per)]
    idxs, wts, hits = [], [], []
    for _ in range(TOP_K):
        m = cur[0]
        for p in range(1, per):
            m = jnp.maximum(m, cur[p])
        m = _sublane_max(m)
        cand = jnp.where(cur[0] == m, eid[0], float(ne))
        for p in range(1, per):
            cand = jnp.minimum(cand, jnp.where(cur[p] == m, eid[p], float(ne)))
        e_sel = jnp.min(cand, axis=0, keepdims=True)
        hit = [eid[p] == e_sel for p in range(per)]
        wk = jnp.where(hit[0], s3[0], 0.0)
        for p in range(1, per):
            wk = wk + jnp.where(hit[p], s3[p], 0.0)
        wts.append(jnp.sum(wk, axis=0, keepdims=True))
        cur = [jnp.where(hit[p], NEG_INF, cur[p]) for p in range(per)]
        idxs.append(e_sel)
        hits.append(hit)
    wsum = wts[0]
    for k in range(1, TOP_K):
        wsum = wsum + wts[k]
    chosen = [functools.reduce(lambda a, b: a | b, [hits[k][p] for k in range(TOP_K)]) for p in range(per)]
    onehot = jnp.concatenate([c.astype(F32) for c in chosen], axis=0)
    before = _dot(onehot.astype(BF16), tri_ref[...]) + carry[...]
    for k in range(TOP_K):
        rk = jnp.where(hits[k][0], before[0:ng, :], 0.0)
        for p in range(1, per):
            rk = rk + jnp.where(hits[k][p], before[p * ng:(p + 1) * ng, :], 0.0)
        rank_ref[k:k + 1, :] = jnp.sum(rk, axis=0, keepdims=True).astype(jnp.int32)
        idx_ref[k:k + 1, :] = idxs[k].astype(jnp.int32)
        wt_ref[k:k + 1, :] = wts[k] / wsum * ROUTED_SCALE
    carry[...] = carry[...] + jnp.sum(onehot, axis=1, keepdims=True)
    cnt_ref[...] = carry[...]


def _route(lgt, b_perm):
    ne, t = lgt.shape
    tt = ROUTE_TILE
    tri = (jnp.arange(tt)[:, None] < jnp.arange(tt)[None, :]).astype(BF16)
    return pl.pallas_call(
        _route_kernel,
        grid=(t // tt,),
        in_specs=[pl.BlockSpec((ne, tt), lambda i: (0, i)),
                  pl.BlockSpec((ne, 1), lambda i: (0, 0)),
                  pl.BlockSpec((tt, tt), lambda i: (0, 0))],
        out_specs=[pl.BlockSpec((TOP_K, tt), lambda i: (0, i)),
                   pl.BlockSpec((TOP_K, tt), lambda i: (0, i)),
                   pl.BlockSpec((TOP_K, tt), lambda i: (0, i)),
                   pl.BlockSpec((ne, 1), lambda i: (0, 0))],
        out_shape=[jax.ShapeDtypeStruct((TOP_K, t), jnp.int32),
                   jax.ShapeDtypeStruct((TOP_K, t), F32),
                   jax.ShapeDtypeStruct((TOP_K, t), jnp.int32),
                   jax.ShapeDtypeStruct((ne, 1), F32)],
        scratch_shapes=[pltpu.VMEM((ne, 1), F32)],
        compiler_params=pltpu.CompilerParams(dimension_semantics=("arbitrary",),
                                             vmem_limit_bytes=VMEM_LIMIT),
        name="route",
    )(lgt, b_perm, tri)


def _expert_kernel(be_ref, nu_ref, xs_ref, wgu_ref, wd_ref, y_ref):
    @pl.when(pl.program_id(0) < nu_ref[0])
    def _():
        gu = _dot(xs_ref[...], wgu_ref[0])
        de = gu.shape[1] // 2
        hb = _silu(gu[:, :de]) * gu[:, de:]
        y_ref[...] = _dot(hb.astype(BF16), wd_ref[0])


def _experts(xs, w_gu, w_down, block_e, n_used):
    p, d = xs.shape
    nb = p // EXPERT_ROWS
    de2 = w_gu.shape[2]

    def row_map(b, be, nu):
        return (jnp.minimum(b, nu[0] - 1), 0)

    def w_map(b, be, nu):
        return (be[b], 0, 0)

    return pl.pallas_call(
        _expert_kernel,
        grid_spec=pltpu.PrefetchScalarGridSpec(
            num_scalar_prefetch=2,
            grid=(nb,),
            in_specs=[pl.BlockSpec((EXPERT_ROWS, d), row_map),
                      pl.BlockSpec((1, d, de2), w_map),
                      pl.BlockSpec((1, de2 // 2, d), w_map)],
            out_specs=pl.BlockSpec((EXPERT_ROWS, d), row_map)),
        out_shape=jax.ShapeDtypeStruct((p, d), F32),
        compiler_params=pltpu.CompilerParams(dimension_semantics=("arbitrary",),
                                             vmem_limit_bytes=VMEM_LIMIT),
        name="experts",
    )(block_e, n_used, xs, w_gu, w_down)


def _final_kernel(xmid_ref, routed_ref, g2_ref, shf_ref, scf_ref, gf_ref, o_ref):
    x2 = xmid_ref[...] + g2_ref[...] * routed_ref[...]
    o_ref[...] = _rms_norm(x2, gf_ref[...]) * (1.0 + scf_ref[...]) + shf_ref[...]


def _final_prompt(xmid, routed, g2, shf, scf, g_final):
    b, seq, d = xmid.shape
    tl = FINAL_TILE
    tok = pl.BlockSpec((None, tl, d), lambda i, j: (i, j, 0))
    vec = pl.BlockSpec((None, 1, d), lambda i, j: (i, 0, 0))
    return pl.pallas_call(
        _final_kernel,
        grid=(b, seq // tl),
        in_specs=[tok, tok, vec, vec, vec, pl.BlockSpec((1, d), lambda i, j: (0, 0))],
        out_specs=tok,
        out_shape=jax.ShapeDtypeStruct((b, seq, d), F32),
        compiler_params=pltpu.CompilerParams(dimension_semantics=("arbitrary", "arbitrary"),
                                             vmem_limit_bytes=VMEM_LIMIT),
        name="final_prompt",
    )(xmid, routed, g2[:, None, :], shf[:, None, :], scf[:, None, :], g_final)


def _final_sample(xmid, routed, g2, shf, scf, g_final):
    ins = [xmid, routed, g2, shf, scf, g_final]
    return pl.pallas_call(
        _final_kernel,
        grid=(1,),
        in_specs=[_full_spec(a) for a in ins],
        out_specs=_full_spec(xmid),
        out_shape=jax.ShapeDtypeStruct(xmid.shape, F32),
        compiler_params=pltpu.CompilerParams(dimension_semantics=("arbitrary",),
                                             vmem_limit_bytes=VMEM_LIMIT),
        name="final_sample",
    )(*ins)


def _block_diag_pairs(w_pool):
    g, c, _ = w_pool.shape
    eye = jnp.eye(g, dtype=w_pool.dtype)
    return (eye[:, None, :, None] * w_pool[:, :, None, :]).reshape(g * c, g * c)


def kernel(x_prompt, x_sample, state_pool, state_conv, c_prompt, c_sample, w_ada, b_ada, g_mix, w_in, w_pool, pool_scale, w_dw, b_dw, ln_g, ln_b, w_out, g_ffn, w_router, b_router, w_gate, w_up, w_down, ws_gate, ws_up, ws_down, w_ada_final, b_ada_final, g_final):
    bp, seq, d = x_prompt.shape
    bs = x_sample.shape[0]
    depth = w_ada.shape[0]
    assert depth == 1 and x_sample.shape[1] == 1
    conv_k = w_dw.shape[1]
    ne = w_router.shape[-1]
    per = ne // N_EXPERT_GROUPS
    tp = bp * seq
    t_all = tp + bs

    row_expert = (jnp.arange(ne) % N_EXPERT_GROUPS) * per + jnp.arange(ne) // N_EXPERT_GROUPS
    wl = {
        "g_mix": g_mix[0][None, :],
        "in": w_in[0].astype(BF16),
        "pool": _block_diag_pairs(w_pool[0]).astype(BF16),
        "pool_scale": pool_scale[0][None, :],
        "dw": w_dw[0],
        "b_dw": b_dw[0][None, :],
        "ln_g": ln_g[0][None, :],
        "ln_b": ln_b[0][None, :],
        "out": w_out[0].astype(BF16),
        "g_ffn": g_ffn[0][None, :],
        "router_t": w_router[0].T[row_expert].astype(BF16),
        "s_gu": jnp.concatenate([ws_gate[0], ws_up[0]], axis=1).astype(BF16),
        "s_down": ws_down[0].astype(BF16),
    }
    b_perm = b_router[0][row_expert][:, None]
    w_gu = jnp.concatenate([w_gate[0], w_up[0]], axis=2).astype(BF16)
    w_dn = w_down[0].astype(BF16)

    c_all = jnp.concatenate([c_prompt, c_sample], axis=0)
    mod = _ada(c_all, w_ada[0], b_ada[0])
    modf = _ada(c_all, w_ada_final, b_ada_final)

    xmid_p, h2_p, lgt_p, npool_p, nconv_p = _prompt_mixer(x_prompt, mod[:bp], wl, conv_k=conv_k)
    sp_t = jnp.transpose(state_pool[0], (1, 0, 2))
    sc_t = jnp.transpose(state_conv[0], (1, 0, 2))
    xmid_s, h2_s, lgt_s, u_s, a_s = _sample_mixer(x_sample[:, 0, :], mod[bp:], sp_t, sc_t, wl, conv_k=conv_k)

    lgt = jnp.concatenate([lgt_p, lgt_s], axis=1)
    idx, wts, rank, counts_perm = _route(lgt, b_perm)
    counts = jnp.zeros((ne,), jnp.int32).at[row_expert].set(counts_perm[:, 0].astype(jnp.int32))
    pcounts = (counts + EXPERT_ROWS - 1) // EXPERT_ROWS * EXPERT_ROWS
    pends = jnp.cumsum(pcounts)
    pstart = pends - pcounts
    dest = pstart[idx] + rank
    n_blocks = (t_all * TOP_K + EXPERT_ROWS - 1) // EXPERT_ROWS + ne
    n_used = (pends[-1] // EXPERT_ROWS).astype(jnp.int32)
    blk = jnp.minimum(jnp.arange(n_blocks, dtype=jnp.int32), n_used - 1)
    block_e = jnp.minimum(jnp.searchsorted(pends, blk * EXPERT_ROWS, side="right"), ne - 1).astype(jnp.int32)

    h2 = jnp.concatenate([h2_p.reshape(tp, d), h2_s, jnp.zeros((1, d), BF16)], axis=0)
    tok = jnp.full((n_blocks * EXPERT_ROWS,), t_all, jnp.int32).at[dest.reshape(-1)].set(
        jnp.tile(jnp.arange(t_all, dtype=jnp.int32), TOP_K))
    xs = h2[tok]
    y = _experts(xs, w_gu, w_dn, block_e, n_used[None])
    routed = jnp.sum(y[dest] * wts[:, :, None], axis=0)

    g2 = mod[:, 5 * d:6 * d]
    shf, scf = modf[:, :d], modf[:, d:]
    y_prompt = _final_prompt(xmid_p, routed[:tp].reshape(bp, seq, d), g2[:bp], shf[:bp], scf[:bp],
                             g_final[None, :])
    y_sample = _final_sample(xmid_s, routed[tp:], g2[bp:], shf[bp:], scf[bp:], g_final[None, :])

    new_pool_s = jnp.concatenate([state_pool[0][:, 1:], u_s[:, None, :]], axis=1)
    new_conv_s = jnp.concatenate([state_conv[0][:, 1:], a_s[:, None, :]], axis=1)
    return (y_prompt, y_sample[:, None, :], npool_p[None], nconv_p[None], new_pool_s[None], new_conv_s[None])
```

```python
import functools

import jax
import jax.numpy as jnp
from jax import lax
from jax.experimental import pallas as pl
from jax.experimental.pallas import tpu as pltpu
from jax.experimental.pallas import tpu_sc as plsc

POOL_WINDOWS = (2, 4, 8, 16)
N_EXPERT_GROUPS = 8
TOPK_GROUPS = 4
TOP_K = 8
ROUTED_SCALE = 2.5
EPS = 1e-6

LANES = 128
SUBLANES = 8
BF16_ROWS = 16
VMEM_LIMIT = 52 * 1024 * 1024

SEQ_TILE = 512
ROW_CHUNK = 16
ROUTE_TILE = 384
EXPERT_ROWS = 256
FINAL_TILE = 512

SC_CORES = 2
SC_SUBCORES = 16
SC_WORKERS = SC_CORES * SC_SUBCORES
SC_CHUNK = 88

F32 = jnp.float32
BF16 = jnp.bfloat16
I32 = jnp.int32
U32 = jnp.uint32
NEG_INF = float("-inf")
HI16 = 0xFFFF0000


def _sigmoid(x):
    return 1.0 / (1.0 + jnp.exp(-x))


def _silu(x):
    return x * _sigmoid(x)


def _rms_norm(x, g):
    return x * lax.rsqrt(jnp.mean(x * x, axis=-1, keepdims=True) + EPS) * g


def _dot(a, b):
    return jnp.dot(a, b, preferred_element_type=F32)


def _pack_bf16_pairs(x):
    w = x.shape[1] // 2
    bits = lax.bitcast_convert_type(x.astype(BF16).astype(F32), U32)
    return lax.bitcast_convert_type((bits[:, :w] >> 16) | (bits[:, w:] & U32(HI16)), I32)


def _unpack_bf16_pairs(p):
    bits = lax.bitcast_convert_type(p, U32)
    lo = lax.bitcast_convert_type(bits << 16, F32)
    hi = lax.bitcast_convert_type(bits & U32(HI16), F32)
    return jnp.concatenate([lo, hi], axis=1)


def _ada_kernel(c_ref, w_ref, b_ref, o_ref):
    s = _silu(c_ref[...])
    o_ref[...] = _dot(s.astype(BF16), w_ref[...].astype(BF16)) + b_ref[...]


def _ada(c, w, b):
    rows, d = c.shape
    n = w.shape[1]
    tn = 1024
    return pl.pallas_call(
        _ada_kernel,
        grid=(n // tn,),
        in_specs=[pl.BlockSpec((rows, d), lambda j: (0, 0)),
                  pl.BlockSpec((d, tn), lambda j: (0, j)),
                  pl.BlockSpec((1, tn), lambda j: (0, j))],
        out_specs=pl.BlockSpec((rows, tn), lambda j: (0, j)),
        out_shape=jax.ShapeDtypeStruct((rows, n), F32),
        compiler_params=pltpu.CompilerParams(dimension_semantics=("arbitrary",),
                                             vmem_limit_bytes=VMEM_LIMIT),
        name="ada",
    )(c, w, b.reshape(1, n))


def _mixer_tail(x, pool_d, conv_act, mod, w, xmid_ref, h2_ref, lgt_ref):
    sh2, sc2, g1, g2 = mod
    pw = pool_d.shape[1]
    pool_out = _dot(pool_d, w["pool"][...]) * w["pool_scale"][...]
    mix = _dot(pool_out.astype(BF16), w["out"][:pw, :]) + _dot(conv_act, w["out"][pw:, :])
    x1 = x + g1 * mix
    h2f = _rms_norm(x1, w["g_ffn"][...]) * (1.0 + sc2) + sh2
    h2_ref[...] = _pack_bf16_pairs(h2f)
    h2 = h2f.astype(BF16)
    lgt_ref[...] = lax.dot_general(w["router_t"][...], h2, (((1,), (1,)), ((), ())),
                                   preferred_element_type=F32)
    gu = _dot(h2, w["s_gu"][...])
    de = gu.shape[1] // 2
    hs = _silu(gu[:, :de]) * gu[:, de:]
    shared = _dot(hs.astype(BF16), w["s_down"][...])
    xmid_ref[...] = x1 + g2 * shared


_W_NAMES = ("g_mix", "in", "pool", "pool_scale", "dw", "b_dw", "ln_g", "ln_b", "out", "g_ffn",
            "router_t", "s_gu", "s_down")


def _layer_norm_silu(yc, g, b):
    mu = jnp.mean(yc, axis=-1, keepdims=True)
    yz = yc - mu
    var = jnp.mean(yz * yz, axis=-1, keepdims=True)
    return _silu(yz * lax.rsqrt(var + EPS) * g + b)


def _prompt_mixer_kernel(x_ref, mod_ref, *refs, tl, d, pw, cw, conv_k):
    nw = len(_W_NAMES)
    w = dict(zip(_W_NAMES, refs[:nw]))
    xmid_ref, h2_ref, lgt_ref, npool_ref, nconv_ref = refs[nw:nw + 5]
    ubuf, abuf, dbuf, cbuf = refs[nw + 5:]
    l = pl.program_id(1)
    pool_buf = max(POOL_WINDOWS) - 1
    uh, ah = 16, 32
    pg = pw // len(POOL_WINDOWS)

    @pl.when(l == 0)
    def _():
        ubuf[0:uh, :] = jnp.zeros((uh, pw), F32)
        abuf[0:ah, :] = jnp.zeros((ah, cw), F32)

    @pl.when(l > 0)
    def _():
        ubuf[0:uh, :] = ubuf[tl:tl + uh, :]
        abuf[0:ah, :] = abuf[tl:tl + ah, :]

    x = x_ref[0]
    sh1 = mod_ref[0, :, 0 * d:1 * d]
    sc1 = mod_ref[0, :, 1 * d:2 * d]
    g1 = mod_ref[0, :, 2 * d:3 * d]
    sh2 = mod_ref[0, :, 3 * d:4 * d]
    sc2 = mod_ref[0, :, 4 * d:5 * d]
    g2 = mod_ref[0, :, 5 * d:6 * d]

    h = _rms_norm(x, w["g_mix"][...]) * (1.0 + sc1) + sh1
    z = _dot(h.astype(BF16), w["in"][...])
    ubuf[uh:uh + tl, :] = z[:, :pw]
    abuf[ah:ah + tl, :] = z[:, pw:pw + cw] * _sigmoid(z[:, pw + cw:])

    row0 = l * tl
    for c in range(tl // ROW_CHUNK):
        r = c * ROW_CHUNK
        t = row0 + r + lax.broadcasted_iota(jnp.int32, (ROW_CHUNK, 1), 0)
        u_c = ubuf[uh + r:uh + r + ROW_CHUNK, :]
        for g, win in enumerate(POOL_WINDOWS):
            cols = slice(g * pg, (g + 1) * pg)
            acc = u_c[:, cols]
            for j in range(1, win):
                acc = acc + ubuf[uh + r - j:uh + r - j + ROW_CHUNK, cols]
            inv = 1.0 / jnp.minimum(win, t + 1).astype(F32)
            dbuf[r:r + ROW_CHUNK, cols] = (acc * inv - u_c[:, cols]).astype(BF16)
        yc = abuf[ah + r - (conv_k - 1):ah + r - (conv_k - 1) + ROW_CHUNK, :] * w["dw"][0:1, :]
        for k in range(1, conv_k):
            o = ah + r - (conv_k - 1) + k
            yc = yc + abuf[o:o + ROW_CHUNK, :] * w["dw"][k:k + 1, :]
        yc = yc + w["b_dw"][...]
        cbuf[r:r + ROW_CHUNK, :] = _layer_norm_silu(yc, w["ln_g"][...], w["ln_b"][...]).astype(BF16)

    @pl.when(l == pl.num_programs(1) - 1)
    def _():
        npool_ref[0] = ubuf[uh + tl - pool_buf:uh + tl, :]
        nconv_ref[0] = abuf[ah + tl - (conv_k - 1):ah + tl, :]

    _mixer_tail(x, dbuf[...], cbuf[...], (sh2, sc2, g1, g2), w, xmid_ref.at[0], h2_ref.at[0], lgt_ref)


def _sample_mixer_kernel(x_ref, mod_ref, sp_ref, sc_ref, *refs, d, pw, cw, conv_k):
    nw = len(_W_NAMES)
    w = dict(zip(_W_NAMES, refs[:nw]))
    xmid_ref, h2_ref, lgt_ref, u_ref, a_ref = refs[nw:nw + 5]
    pool_buf = max(POOL_WINDOWS) - 1
    pg = pw // len(POOL_WINDOWS)
    x = x_ref[...]
    sh1, sc1, g1, sh2, sc2, g2 = [mod_ref[:, i * d:(i + 1) * d] for i in range(6)]
    h = _rms_norm(x, w["g_mix"][...]) * (1.0 + sc1) + sh1
    z = _dot(h.astype(BF16), w["in"][...])
    u = z[:, :pw]
    a = z[:, pw:pw + cw] * _sigmoid(z[:, pw + cw:])
    u_ref[...] = u
    a_ref[...] = a
    ds = []
    for g, win in enumerate(POOL_WINDOWS):
        cols = slice(g * pg, (g + 1) * pg)
        acc = u[:, cols]
        for j in range(1, win):
            acc = acc + sp_ref[pool_buf - j, :, cols]
        cnt = float(min(win, 1 + pool_buf))
        ds.append(acc / cnt - u[:, cols])
    pool_d = jnp.concatenate(ds, axis=-1).astype(BF16)
    yc = a * w["dw"][conv_k - 1:conv_k, :]
    for k in range(conv_k - 1):
        yc = yc + sc_ref[k] * w["dw"][k:k + 1, :]
    yc = yc + w["b_dw"][...]
    conv_act = _layer_norm_silu(yc, w["ln_g"][...], w["ln_b"][...]).astype(BF16)
    _mixer_tail(x, pool_d, conv_act, (sh2, sc2, g1, g2), w, xmid_ref, h2_ref, lgt_ref)


def _full_spec(a):
    nd = a.ndim
    return pl.BlockSpec(a.shape, lambda *_: (0,) * nd)


def _prompt_mixer(x, mod, wl, *, conv_k):
    b, seq, d = x.shape
    pw = wl["pool"].shape[0]
    cw = wl["dw"].shape[1]
    ne = wl["router_t"].shape[0]
    tl = SEQ_TILE
    nl = seq // tl
    ws = [wl[n] for n in _W_NAMES]
    kern = functools.partial(_prompt_mixer_kernel, tl=tl, d=d, pw=pw, cw=cw, conv_k=conv_k)
    return pl.pallas_call(
        kern,
        grid=(b, nl),
        in_specs=[pl.BlockSpec((1, tl, d), lambda i, j: (i, j, 0)),
                  pl.BlockSpec((1, 1, mod.shape[-1]), lambda i, j: (i, 0, 0))]
                 + [_full_spec(a) for a in ws],
        out_specs=[pl.BlockSpec((1, tl, d), lambda i, j: (i, j, 0)),
                   pl.BlockSpec((1, tl, d // 2), lambda i, j: (i, j, 0)),
                   pl.BlockSpec((ne, tl), lambda i, j: (0, i * nl + j)),
                   pl.BlockSpec((1, max(POOL_WINDOWS) - 1, pw), lambda i, j: (i, 0, 0)),
                   pl.BlockSpec((1, conv_k - 1, cw), lambda i, j: (i, 0, 0))],
        out_shape=[jax.ShapeDtypeStruct((b, seq, d), F32),
                   jax.ShapeDtypeStruct((b, seq, d // 2), I32),
                   jax.ShapeDtypeStruct((ne, b * seq), F32),
                   jax.ShapeDtypeStruct((b, max(POOL_WINDOWS) - 1, pw), F32),
                   jax.ShapeDtypeStruct((b, conv_k - 1, cw), F32)],
        scratch_shapes=[pltpu.VMEM((16 + tl, pw), F32), pltpu.VMEM((32 + tl, cw), F32),
                        pltpu.VMEM((tl, pw), BF16), pltpu.VMEM((tl, cw), BF16)],
        compiler_params=pltpu.CompilerParams(dimension_semantics=("arbitrary", "arbitrary"),
                                             vmem_limit_bytes=VMEM_LIMIT),
        name="prompt_mixer",
    )(x, mod.reshape(b, 1, -1), *ws)


def _sample_mixer(x, mod, sp_t, sc_t, wl, *, conv_k):
    rows, d = x.shape
    pw = wl["pool"].shape[0]
    cw = wl["dw"].shape[1]
    ne = wl["router_t"].shape[0]
    ws = [wl[n] for n in _W_NAMES]
    kern = functools.partial(_sample_mixer_kernel, d=d, pw=pw, cw=cw, conv_k=conv_k)
    ins = [x, mod, sp_t, sc_t] + ws
    return pl.pallas_call(
        kern,
        grid=(1,),
        in_specs=[_full_spec(a) for a in ins],
        out_specs=[pl.BlockSpec((rows, d), lambda i: (0, 0)),
                   pl.BlockSpec((rows, d // 2), lambda i: (0, 0)),
                   pl.BlockSpec((ne, rows), lambda i: (0, 0)),
                   pl.BlockSpec((rows, pw), lambda i: (0, 0)),
                   pl.BlockSpec((rows, cw), lambda i: (0, 0))],
        out_shape=[jax.ShapeDtypeStruct((rows, d), F32),
                   jax.ShapeDtypeStruct((rows, d // 2), I32),
                   jax.ShapeDtypeStruct((ne, rows), F32),
                   jax.ShapeDtypeStruct((rows, pw), F32),
                   jax.ShapeDtypeStruct((rows, cw), F32)],
        compiler_params=pltpu.CompilerParams(dimension_semantics=("arbitrary",),
                                             vmem_limit_bytes=VMEM_LIMIT),
        name="sample_mixer",
    )(*ins)


def _sublane_max(x):
    return jnp.max(x, axis=0, keepdims=True)


def _route_kernel(lg_ref, b_ref, tri_ref, idx_ref, wt_ref, rank_ref, cnt_ref, carry):
    i = pl.program_id(0)
    ne, tt = lg_ref.shape
    ng = N_EXPERT_GROUPS
    per = ne // ng

    @pl.when(i == 0)
    def _():
        carry[...] = jnp.zeros_like(carry)

    s = _sigmoid(lg_ref[...])
    sel = s + b_ref[...]
    s3 = [s[p * ng:(p + 1) * ng, :] for p in range(per)]
    sel3 = [sel[p * ng:(p + 1) * ng, :] for p in range(per)]
    m1 = sel3[0]
    m2 = jnp.full_like(m1, NEG_INF)
    for p in range(1, per):
        m2 = jnp.maximum(m2, jnp.minimum(m1, sel3[p]))
        m1 = jnp.maximum(m1, sel3[p])
    gs = m1 + m2
    gi = lax.broadcasted_iota(jnp.int32, (ng, tt), 0)
    beaten = jnp.zeros((ng, tt), jnp.int32)
    for g in range(ng):
        row = gs[g:g + 1, :]
        beats = (row > gs) | ((row == gs) & (gi > g))
        beaten = beaten + beats.astype(jnp.int32)
    keep = beaten < TOPK_GROUPS
    cur = [jnp.where(keep, sel3[p], NEG_INF) for p in range(per)]
    eid = [(gi * per + p).astype(F32) for p in range(per)]
    idxs, wts, hits = [], [], []
    for _ in range(TOP_K):
        m = cur[0]
        for p in range(1, per):
            m = jnp.maximum(m, cur[p])
        m = _sublane_max(m)
        cand = jnp.where(cur[0] == m, eid[0], float(ne))
        for p in range(1, per):
            cand = jnp.minimum(cand, jnp.where(cur[p] == m, eid[p], float(ne)))
        e_sel = jnp.min(cand, axis=0, keepdims=True)
        hit = [eid[p] == e_sel for p in range(per)]
        wk = jnp.where(hit[0], s3[0], 0.0)
        for p in range(1, per):
            wk = wk + jnp.where(hit[p], s3[p], 0.0)
        wts.append(jnp.sum(wk, axis=0, keepdims=True))
        cur = [jnp.where(hit[p], NEG_INF, cur[p]) for p in range(per)]
        idxs.append(e_sel)
        hits.append(hit)
    wsum = wts[0]
    for k in range(1, TOP_K):
        wsum = wsum + wts[k]
    chosen = [functools.reduce(lambda a, b: a | b, [hits[k][p] for k in range(TOP_K)]) for p in range(per)]
    onehot = jnp.concatenate([c.astype(F32) for c in chosen], axis=0)
    before = _dot(onehot.astype(BF16), tri_ref[...]) + carry[...]
    for k in range(TOP_K):
        rk = jnp.where(hits[k][0], before[0:ng, :], 0.0)
        for p in range(1, per):
            rk = rk + jnp.where(hits[k][p], before[p * ng:(p + 1) * ng, :], 0.0)
        rank_ref[k:k + 1, :] = jnp.sum(rk, axis=0, keepdims=True).astype(jnp.int32)
        idx_ref[k:k + 1, :] = idxs[k].astype(jnp.int32)
        wt_ref[k:k + 1, :] = wts[k] / wsum * ROUTED_SCALE
    carry[...] = carry[...] + jnp.sum(onehot, axis=1, keepdims=True)
    cnt_ref[...] = carry[...]


def _route(lgt, b_perm):
    ne, t = lgt.shape
    tt = ROUTE_TILE
    tri = (jnp.arange(tt)[:, None] < jnp.arange(tt)[None, :]).astype(BF16)
    return pl.pallas_call(
        _route_kernel,
        grid=(t // tt,),
        in_specs=[pl.BlockSpec((ne, tt), lambda i: (0, i)),
                  pl.BlockSpec((ne, 1), lambda i: (0, 0)),
                  pl.BlockSpec((tt, tt), lambda i: (0, 0))],
        out_specs=[pl.BlockSpec((TOP_K, tt), lambda i: (0, i)),
                   pl.BlockSpec((TOP_K, tt), lambda i: (0, i)),
                   pl.BlockSpec((TOP_K, tt), lambda i: (0, i)),
                   pl.BlockSpec((ne, 1), lambda i: (0, 0))],
        out_shape=[jax.ShapeDtypeStruct((TOP_K, t), jnp.int32),
                   jax.ShapeDtypeStruct((TOP_K, t), F32),
                   jax.ShapeDtypeStruct((TOP_K, t), jnp.int32),
                   jax.ShapeDtypeStruct((ne, 1), F32)],
        scratch_shapes=[pltpu.VMEM((ne, 1), F32)],
        compiler_params=pltpu.CompilerParams(dimension_semantics=("arbitrary",),
                                             vmem_limit_bytes=VMEM_LIMIT),
        name="route",
    )(lgt, b_perm, tri)


def _expert_kernel(be_ref, nv_ref, nu_ref, xs_ref, wgu_ref, wd_ref, y_ref):
    b = pl.program_id(0)

    @pl.when(b < nu_ref[0])
    def _():
        rows = lax.broadcasted_iota(I32, (xs_ref.shape[0], 1), 0)
        xs = jnp.where(rows < nv_ref[b], _unpack_bf16_pairs(xs_ref[...]), 0.0).astype(BF16)
        gu = _dot(xs, wgu_ref[0])
        de = gu.shape[1] // 2
        hb = _silu(gu[:, :de]) * gu[:, de:]
        y_ref[...] = _pack_bf16_pairs(_dot(hb.astype(BF16), wd_ref[0]))


def _experts(xs, w_gu, w_down, block_e, block_valid, n_used):
    p, dw = xs.shape
    nb = p // EXPERT_ROWS
    d, de2 = w_gu.shape[1:]

    def row_map(b, be, nv, nu):
        return (jnp.minimum(b, nu[0] - 1), 0)

    def w_map(b, be, nv, nu):
        return (be[jnp.minimum(b, nu[0] - 1)], 0, 0)

    return pl.pallas_call(
        _expert_kernel,
        grid_spec=pltpu.PrefetchScalarGridSpec(
            num_scalar_prefetch=3,
            grid=(nb,),
            in_specs=[pl.BlockSpec((EXPERT_ROWS, dw), row_map),
                      pl.BlockSpec((1, d, de2), w_map),
                      pl.BlockSpec((1, de2 // 2, d), w_map)],
            out_specs=pl.BlockSpec((EXPERT_ROWS, dw), row_map)),
        out_shape=jax.ShapeDtypeStruct((p, dw), I32),
        compiler_params=pltpu.CompilerParams(dimension_semantics=("arbitrary",),
                                             vmem_limit_bytes=VMEM_LIMIT),
        name="experts",
    )(block_e, block_valid, n_used, xs, w_gu, w_down)


def _sc_worker_id():
    return lax.axis_index("s") * SC_CORES + lax.axis_index("c")


def _sc_dispatch(h2, dest, n_rows):
    w = h2.shape[1]
    n_chunks, k_top, ch = dest.shape
    per_worker = n_chunks // SC_WORKERS
    mesh = plsc.VectorSubcoreMesh(core_axis_name="c", subcore_axis_name="s")

    @functools.partial(
        pl.kernel, mesh=mesh, out_type=jax.ShapeDtypeStruct((n_rows, w), I32),
        scratch_types=[pltpu.VMEM((k_top, ch), I32), pltpu.VMEM((ch, w), I32), pltpu.SemaphoreType.DMA])
    def dispatch(h2_hbm, dest_hbm, xs_hbm, idx_v, rows_v, sem):
        wid = _sc_worker_id()

        @pl.loop(0, per_worker)
        def _(i):
            c = wid * per_worker + i
            pltpu.sync_copy(h2_hbm.at[pl.ds(c * ch, ch)], rows_v)
            pltpu.sync_copy(dest_hbm.at[c], idx_v)
            copies = [pltpu.async_copy(rows_v, xs_hbm.at[idx_v.at[k]], sem) for k in range(k_top)]
            for cp in copies:
                cp.wait()

    return dispatch(h2, dest)


def _sc_combine_gather(y, dest, t):
    w = y.shape[1]
    n_chunks, k_top, ch = dest.shape
    per_worker = n_chunks // SC_WORKERS
    mesh = plsc.VectorSubcoreMesh(core_axis_name="c", subcore_axis_name="s")

    @functools.partial(
        pl.kernel, mesh=mesh, out_type=jax.ShapeDtypeStruct((k_top, t, w), I32),
        scratch_types=[pltpu.VMEM((k_top, ch), I32), pltpu.VMEM((ch, w), I32), pltpu.SemaphoreType.DMA])
    def gather(y_hbm, dest_hbm, out_hbm, idx_v, rows_v, sem):
        wid = _sc_worker_id()

        @pl.loop(0, per_worker)
        def _(i):
            c = wid * per_worker + i
            pltpu.sync_copy(dest_hbm.at[c], idx_v)
            for k in range(k_top):
                pltpu.async_copy(y_hbm.at[idx_v.at[k]], rows_v, sem).wait()
                pltpu.sync_copy(rows_v, out_hbm.at[k, pl.ds(c * ch, ch)])

    return gather(y, dest)


def _final_kernel(xmid_ref, yg_ref, wt_ref, g2_ref, shf_ref, scf_ref, gf_ref, o_ref):
    wt = wt_ref[...]
    routed = wt[:, 0:1] * _unpack_bf16_pairs(yg_ref[0])
    for k in range(1, yg_ref.shape[0]):
        routed = routed + wt[:, k:k + 1] * _unpack_bf16_pairs(yg_ref[k])
    x2 = xmid_ref[...] + g2_ref[...] * routed
    o_ref[...] = _rms_norm(x2, gf_ref[...]) * (1.0 + scf_ref[...]) + shf_ref[...]


def _final_prompt(xmid, yg, wts_t, g2, shf, scf, g_final):
    b, seq, d = xmid.shape
    k_top, _, w = yg.shape
    tl = FINAL_TILE
    nl = seq // tl
    tok = pl.BlockSpec((None, tl, d), lambda i, j: (i, j, 0))
    vec = pl.BlockSpec((None, 1, d), lambda i, j: (i, 0, 0))
    return pl.pallas_call(
        _final_kernel,
        grid=(b, nl),
        in_specs=[tok,
                  pl.BlockSpec((k_top, tl, w), lambda i, j: (0, i * nl + j, 0)),
                  pl.BlockSpec((tl, k_top), lambda i, j: (i * nl + j, 0)),
                  vec, vec, vec, pl.BlockSpec((1, d), lambda i, j: (0, 0))],
        out_specs=tok,
        out_shape=jax.ShapeDtypeStruct((b, seq, d), F32),
        compiler_params=pltpu.CompilerParams(dimension_semantics=("arbitrary", "arbitrary"),
                                             vmem_limit_bytes=VMEM_LIMIT),
        name="final_prompt",
    )(xmid, yg, wts_t, g2[:, None, :], shf[:, None, :], scf[:, None, :], g_final)


def _final_sample(xmid, yg, wts_t, g2, shf, scf, g_final, first_row):
    rows, d = xmid.shape
    k_top, _, w = yg.shape
    blk = first_row // rows
    vec = pl.BlockSpec((rows, d), lambda i: (0, 0))
    return pl.pallas_call(
        _final_kernel,
        grid=(1,),
        in_specs=[vec,
                  pl.BlockSpec((k_top, rows, w), lambda i: (0, blk, 0)),
                  pl.BlockSpec((rows, k_top), lambda i: (blk, 0)),
                  vec, vec, vec, pl.BlockSpec((1, d), lambda i: (0, 0))],
        out_specs=vec,
        out_shape=jax.ShapeDtypeStruct(xmid.shape, F32),
        compiler_params=pltpu.CompilerParams(dimension_semantics=("arbitrary",),
                                             vmem_limit_bytes=VMEM_LIMIT),
        name="final_sample",
    )(xmid, yg, wts_t, g2, shf, scf, g_final)


def _block_diag_pairs(w_pool):
    g, c, _ = w_pool.shape
    eye = jnp.eye(g, dtype=w_pool.dtype)
    return (eye[:, None, :, None] * w_pool[:, :, None, :]).reshape(g * c, g * c)


def kernel(x_prompt, x_sample, state_pool, state_conv, c_prompt, c_sample, w_ada, b_ada, g_mix, w_in, w_pool, pool_scale, w_dw, b_dw, ln_g, ln_b, w_out, g_ffn, w_router, b_router, w_gate, w_up, w_down, ws_gate, ws_up, ws_down, w_ada_final, b_ada_final, g_final):
    bp, seq, d = x_prompt.shape
    bs = x_sample.shape[0]
    depth = w_ada.shape[0]
    assert depth == 1 and x_sample.shape[1] == 1
    conv_k = w_dw.shape[1]
    ne = w_router.shape[-1]
    per = ne // N_EXPERT_GROUPS
    tp = bp * seq
    t_all = tp + bs

    row_expert = (jnp.arange(ne) % N_EXPERT_GROUPS) * per + jnp.arange(ne) // N_EXPERT_GROUPS
    wl = {
        "g_mix": g_mix[0][None, :],
        "in": w_in[0].astype(BF16),
        "pool": _block_diag_pairs(w_pool[0]).astype(BF16),
        "pool_scale": pool_scale[0][None, :],
        "dw": w_dw[0],
        "b_dw": b_dw[0][None, :],
        "ln_g": ln_g[0][None, :],
        "ln_b": ln_b[0][None, :],
        "out": w_out[0].astype(BF16),
        "g_ffn": g_ffn[0][None, :],
        "router_t": w_router[0].T[row_expert].astype(BF16),
        "s_gu": jnp.concatenate([ws_gate[0], ws_up[0]], axis=1).astype(BF16),
        "s_down": ws_down[0].astype(BF16),
    }
    b_perm = b_router[0][row_expert][:, None]
    w_gu = jnp.concatenate([w_gate[0], w_up[0]], axis=2).astype(BF16)
    w_dn = w_down[0].astype(BF16)

    c_all = jnp.concatenate([c_prompt, c_sample], axis=0)
    mod = _ada(c_all, w_ada[0], b_ada[0])
    modf = _ada(c_all, w_ada_final, b_ada_final)

    xmid_p, h2_p, lgt_p, npool_p, nconv_p = _prompt_mixer(x_prompt, mod[:bp], wl, conv_k=conv_k)
    sp_t = jnp.transpose(state_pool[0], (1, 0, 2))
    sc_t = jnp.transpose(state_conv[0], (1, 0, 2))
    xmid_s, h2_s, lgt_s, u_s, a_s = _sample_mixer(x_sample[:, 0, :], mod[bp:], sp_t, sc_t, wl, conv_k=conv_k)

    lgt = jnp.concatenate([lgt_p, lgt_s], axis=1)
    idx, wts, rank, counts_perm = _route(lgt, b_perm)
    counts = jnp.zeros((ne,), I32).at[row_expert].set(counts_perm[:, 0].astype(I32))
    pcounts = (counts + EXPERT_ROWS - 1) // EXPERT_ROWS * EXPERT_ROWS
    pends = jnp.cumsum(pcounts)
    pstart = pends - pcounts
    n_blocks = (t_all * TOP_K + EXPERT_ROWS - 1) // EXPERT_ROWS + ne
    n_rows = (n_blocks + 1) * EXPERT_ROWS
    n_used = (pends[-1] // EXPERT_ROWS).astype(I32)
    blk_row = jnp.minimum(jnp.arange(n_blocks + 1, dtype=I32), n_used - 1) * EXPERT_ROWS
    block_e = jnp.minimum(jnp.sum(pends[None, :] <= blk_row[:, None], axis=1), ne - 1).astype(I32)
    block_valid = jnp.clip(counts[block_e] - (blk_row - pstart[block_e]), 0, EXPERT_ROWS).astype(I32)

    grain = SC_WORKERS * SC_CHUNK
    t_pad = (t_all + grain - 1) // grain * grain
    dest = pstart[idx] + rank
    dest = jnp.pad(dest, ((0, 0), (0, t_pad - t_all)), constant_values=n_rows - 1)
    dest = dest.reshape(TOP_K, t_pad // SC_CHUNK, SC_CHUNK).transpose(1, 0, 2)
    wts_t = jnp.pad(wts, ((0, 0), (0, t_pad - t_all))).T

    h2 = jnp.concatenate([h2_p.reshape(tp, d // 2), h2_s, jnp.zeros((t_pad - t_all, d // 2), I32)], axis=0)
    xs = _sc_dispatch(h2, dest, n_rows)
    y = _experts(xs, w_gu, w_dn, block_e, block_valid, n_used[None])
    yg = _sc_combine_gather(y, dest, t_pad)

    g2 = mod[:, 5 * d:6 * d]
    shf, scf = modf[:, :d], modf[:, d:]
    y_prompt = _final_prompt(xmid_p, yg, wts_t, g2[:bp], shf[:bp], scf[:bp], g_final[None, :])
    y_sample = _final_sample(xmid_s, yg, wts_t, g2[bp:], shf[bp:], scf[bp:], g_final[None, :], tp)

    new_pool_s = jnp.concatenate([state_pool[0][:, 1:], u_s[:, None, :]], axis=1)
    new_conv_s = jnp.concatenate([state_conv[0][:, 1:], a_s[:, None, :]], axis=1)
    return (y_prompt, y_sample[:, None, :], npool_p[None], nconv_p[None], new_pool_s[None], new_conv_s[None])
```

```python
import functools
import math

import jax
import jax.numpy as jnp
from jax import lax
from jax.experimental import pallas as pl
from jax.experimental.pallas import tpu as pltpu
from jax.experimental.pallas import tpu_sc as plsc

POOL_WINDOWS = (2, 4, 8, 16)
N_EXPERT_GROUPS = 8
TOPK_GROUPS = 4
TOP_K = 8
ROUTED_SCALE = 2.5
EPS = 1e-6

LANES = 128
SUBLANES = 8
BF16_ROWS = 16
VMEM_LIMIT = 52 * 1024 * 1024

SEQ_TILE = 512
ROW_CHUNK = 16
ROUTE_TILE = 384
EXPERT_ROWS = 256
FINAL_TILE = 512
DEST_TILE = 2816

SC_CORES = 2
SC_SUBCORES = 16
SC_WORKERS = SC_CORES * SC_SUBCORES
SC_DISPATCH_CHUNK = 88
SC_COMBINE_CHUNK = 24

F32 = jnp.float32
BF16 = jnp.bfloat16
I32 = jnp.int32
U32 = jnp.uint32
NEG_INF = float("-inf")
HI16 = 0xFFFF0000


def _sigmoid(x):
    return 1.0 / (1.0 + jnp.exp(-x))


def _silu(x):
    return x * _sigmoid(x)


def _rms_norm(x, g):
    return x * lax.rsqrt(jnp.mean(x * x, axis=-1, keepdims=True) + EPS) * g


def _dot(a, b):
    return jnp.dot(a, b, preferred_element_type=F32)


def _pack_bf16_pairs(x):
    w = x.shape[1] // 2
    bits = lax.bitcast_convert_type(x.astype(BF16).astype(F32), U32)
    return lax.bitcast_convert_type((bits[:, :w] >> 16) | (bits[:, w:] & U32(HI16)), I32)


def _unpack_bf16_pairs(p):
    bits = lax.bitcast_convert_type(p, U32)
    lo = lax.bitcast_convert_type(bits << 16, F32)
    hi = lax.bitcast_convert_type(bits & U32(HI16), F32)
    return jnp.concatenate([lo, hi], axis=1)


def _ada_kernel(c_ref, w_ref, b_ref, o_ref):
    s = _silu(c_ref[...])
    o_ref[...] = _dot(s.astype(BF16), w_ref[...].astype(BF16)) + b_ref[...]


def _ada(c, w, b):
    rows, d = c.shape
    n = w.shape[1]
    tn = 1024
    return pl.pallas_call(
        _ada_kernel,
        grid=(n // tn,),
        in_specs=[pl.BlockSpec((rows, d), lambda j: (0, 0)),
                  pl.BlockSpec((d, tn), lambda j: (0, j)),
                  pl.BlockSpec((1, tn), lambda j: (0, j))],
        out_specs=pl.BlockSpec((rows, tn), lambda j: (0, j)),
        out_shape=jax.ShapeDtypeStruct((rows, n), F32),
        compiler_params=pltpu.CompilerParams(dimension_semantics=("arbitrary",),
                                             vmem_limit_bytes=VMEM_LIMIT),
        name="ada",
    )(c, w, b.reshape(1, n))


def _mixer_tail(x, pool_d, conv_act, mod, w, xmid_ref, h2_ref, lgt_ref):
    sh2, sc2, g1, g2 = mod
    pw = pool_d.shape[1]
    pool_out = _dot(pool_d, w["pool"][...]) * w["pool_scale"][...]
    mix = _dot(pool_out.astype(BF16), w["out"][:pw, :]) + _dot(conv_act, w["out"][pw:, :])
    x1 = x + g1 * mix
    h2f = _rms_norm(x1, w["g_ffn"][...]) * (1.0 + sc2) + sh2
    h2_ref[...] = _pack_bf16_pairs(h2f)
    h2 = h2f.astype(BF16)
    lgt_ref[...] = lax.dot_general(w["router_t"][...], h2, (((1,), (1,)), ((), ())),
                                   preferred_element_type=F32)
    gu = _dot(h2, w["s_gu"][...])
    de = gu.shape[1] // 2
    hs = _silu(gu[:, :de]) * gu[:, de:]
    shared = _dot(hs.astype(BF16), w["s_down"][...])
    xmid_ref[...] = x1 + g2 * shared


_W_NAMES = ("g_mix", "in", "pool", "pool_scale", "dw", "b_dw", "ln_g", "ln_b", "out", "g_ffn",
            "router_t", "s_gu", "s_down")


def _layer_norm_silu(yc, g, b):
    mu = jnp.mean(yc, axis=-1, keepdims=True)
    yz = yc - mu
    var = jnp.mean(yz * yz, axis=-1, keepdims=True)
    return _silu(yz * lax.rsqrt(var + EPS) * g + b)


def _prompt_mixer_kernel(x_ref, mod_ref, *refs, tl, d, pw, cw, conv_k):
    nw = len(_W_NAMES)
    w = dict(zip(_W_NAMES, refs[:nw]))
    xmid_ref, h2_ref, lgt_ref, npool_ref, nconv_ref = refs[nw + 2:nw + 7]
    ubuf, abuf, dbuf, cbuf = refs[nw + 7:]
    l = pl.program_id(1)
    pool_buf = max(POOL_WINDOWS) - 1
    uh, ah = 16, 32
    pg = pw // len(POOL_WINDOWS)

    @pl.when(l == 0)
    def _():
        ubuf[0:uh, :] = jnp.zeros((uh, pw), F32)
        abuf[0:ah, :] = jnp.zeros((ah, cw), F32)

    @pl.when(l > 0)
    def _():
        ubuf[0:uh, :] = ubuf[tl:tl + uh, :]
        abuf[0:ah, :] = abuf[tl:tl + ah, :]

    x = x_ref[0]
    sh1 = mod_ref[0, :, 0 * d:1 * d]
    sc1 = mod_ref[0, :, 1 * d:2 * d]
    g1 = mod_ref[0, :, 2 * d:3 * d]
    sh2 = mod_ref[0, :, 3 * d:4 * d]
    sc2 = mod_ref[0, :, 4 * d:5 * d]
    g2 = mod_ref[0, :, 5 * d:6 * d]

    h = _rms_norm(x, w["g_mix"][...]) * (1.0 + sc1) + sh1
    z = _dot(h.astype(BF16), w["in"][...])
    ubuf[uh:uh + tl, :] = z[:, :pw]
    abuf[ah:ah + tl, :] = z[:, pw:pw + cw] * _sigmoid(z[:, pw + cw:])

    row0 = l * tl
    for c in range(tl // ROW_CHUNK):
        r = c * ROW_CHUNK
        t = row0 + r + lax.broadcasted_iota(jnp.int32, (ROW_CHUNK, 1), 0)
        u_c = ubuf[uh + r:uh + r + ROW_CHUNK, :]
        for g, win in enumerate(POOL_WINDOWS):
            cols = slice(g * pg, (g + 1) * pg)
            acc = u_c[:, cols]
            for j in range(1, win):
                acc = acc + ubuf[uh + r - j:uh + r - j + ROW_CHUNK, cols]
            inv = 1.0 / jnp.minimum(win, t + 1).astype(F32)
            dbuf[r:r + ROW_CHUNK, cols] = (acc * inv - u_c[:, cols]).astype(BF16)
        yc = abuf[ah + r - (conv_k - 1):ah + r - (conv_k - 1) + ROW_CHUNK, :] * w["dw"][0:1, :]
        for k in range(1, conv_k):
            o = ah + r - (conv_k - 1) + k
            yc = yc + abuf[o:o + ROW_CHUNK, :] * w["dw"][k:k + 1, :]
        yc = yc + w["b_dw"][...]
        cbuf[r:r + ROW_CHUNK, :] = _layer_norm_silu(yc, w["ln_g"][...], w["ln_b"][...]).astype(BF16)

    @pl.when(l == pl.num_programs(1) - 1)
    def _():
        npool_ref[0] = ubuf[uh + tl - pool_buf:uh + tl, :]
        nconv_ref[0] = abuf[ah + tl - (conv_k - 1):ah + tl, :]

    _mixer_tail(x, dbuf[...], cbuf[...], (sh2, sc2, g1, g2), w, xmid_ref.at[0], h2_ref, lgt_ref)


def _sample_mixer_kernel(x_ref, mod_ref, sp_ref, sc_ref, *refs, d, pw, cw, conv_k):
    nw = len(_W_NAMES)
    w = dict(zip(_W_NAMES, refs[:nw]))
    xmid_ref, h2_ref, lgt_ref, u_ref, a_ref = refs[nw:nw + 5]
    pool_buf = max(POOL_WINDOWS) - 1
    pg = pw // len(POOL_WINDOWS)
    x = x_ref[...]
    sh1, sc1, g1, sh2, sc2, g2 = [mod_ref[:, i * d:(i + 1) * d] for i in range(6)]
    h = _rms_norm(x, w["g_mix"][...]) * (1.0 + sc1) + sh1
    z = _dot(h.astype(BF16), w["in"][...])
    u = z[:, :pw]
    a = z[:, pw:pw + cw] * _sigmoid(z[:, pw + cw:])
    u_ref[...] = u
    a_ref[...] = a
    ds = []
    for g, win in enumerate(POOL_WINDOWS):
        cols = slice(g * pg, (g + 1) * pg)
        acc = u[:, cols]
        for j in range(1, win):
            acc = acc + sp_ref[pool_buf - j, :, cols]
        cnt = float(min(win, 1 + pool_buf))
        ds.append(acc / cnt - u[:, cols])
    pool_d = jnp.concatenate(ds, axis=-1).astype(BF16)
    yc = a * w["dw"][conv_k - 1:conv_k, :]
    for k in range(conv_k - 1):
        yc = yc + sc_ref[k] * w["dw"][k:k + 1, :]
    yc = yc + w["b_dw"][...]
    conv_act = _layer_norm_silu(yc, w["ln_g"][...], w["ln_b"][...]).astype(BF16)
    rows = x.shape[0]
    h2_ref[...] = jnp.zeros_like(h2_ref)
    lgt_ref[...] = jnp.zeros_like(lgt_ref)
    _mixer_tail(x, pool_d, conv_act, (sh2, sc2, g1, g2), w, xmid_ref,
                h2_ref.at[pl.ds(0, rows), :], lgt_ref.at[:, pl.ds(0, rows)])


def _full_spec(a):
    nd = a.ndim
    return pl.BlockSpec(a.shape, lambda *_: (0,) * nd)


def _prompt_mixer(x, mod, wl, h2_all, lgt_all, *, conv_k):
    b, seq, d = x.shape
    pw = wl["pool"].shape[0]
    cw = wl["dw"].shape[1]
    ne = wl["router_t"].shape[0]
    tl = SEQ_TILE
    nl = seq // tl
    ws = [wl[n] for n in _W_NAMES]
    kern = functools.partial(_prompt_mixer_kernel, tl=tl, d=d, pw=pw, cw=cw, conv_k=conv_k)
    return pl.pallas_call(
        kern,
        grid=(b, nl),
        in_specs=[pl.BlockSpec((1, tl, d), lambda i, j: (i, j, 0)),
                  pl.BlockSpec((1, 1, mod.shape[-1]), lambda i, j: (i, 0, 0))]
                 + [_full_spec(a) for a in ws]
                 + [pl.BlockSpec(memory_space=pl.ANY), pl.BlockSpec(memory_space=pl.ANY)],
        out_specs=[pl.BlockSpec((1, tl, d), lambda i, j: (i, j, 0)),
                   pl.BlockSpec((tl, d // 2), lambda i, j: (i * nl + j, 0)),
                   pl.BlockSpec((ne, tl), lambda i, j: (0, i * nl + j)),
                   pl.BlockSpec((1, max(POOL_WINDOWS) - 1, pw), lambda i, j: (i, 0, 0)),
                   pl.BlockSpec((1, conv_k - 1, cw), lambda i, j: (i, 0, 0))],
        out_shape=[jax.ShapeDtypeStruct((b, seq, d), F32),
                   jax.ShapeDtypeStruct(h2_all.shape, I32),
                   jax.ShapeDtypeStruct(lgt_all.shape, F32),
                   jax.ShapeDtypeStruct((b, max(POOL_WINDOWS) - 1, pw), F32),
                   jax.ShapeDtypeStruct((b, conv_k - 1, cw), F32)],
        scratch_shapes=[pltpu.VMEM((16 + tl, pw), F32), pltpu.VMEM((32 + tl, cw), F32),
                        pltpu.VMEM((tl, pw), BF16), pltpu.VMEM((tl, cw), BF16)],
        compiler_params=pltpu.CompilerParams(dimension_semantics=("arbitrary", "arbitrary"),
                                             vmem_limit_bytes=VMEM_LIMIT),
        input_output_aliases={2 + len(ws): 1, 3 + len(ws): 2},
        name="prompt_mixer",
    )(x, mod.reshape(b, 1, -1), *ws, h2_all, lgt_all)


def _sample_mixer(x, mod, sp_t, sc_t, wl, t_prompt, t_pad, *, conv_k):
    rows, d = x.shape
    tail = t_pad - t_prompt
    assert t_prompt % tail == 0 and rows <= tail
    pw = wl["pool"].shape[0]
    cw = wl["dw"].shape[1]
    ne = wl["router_t"].shape[0]
    ws = [wl[n] for n in _W_NAMES]
    kern = functools.partial(_sample_mixer_kernel, d=d, pw=pw, cw=cw, conv_k=conv_k)
    ins = [x, mod, sp_t, sc_t] + ws
    return pl.pallas_call(
        kern,
        grid=(1,),
        in_specs=[_full_spec(a) for a in ins],
        out_specs=[pl.BlockSpec((rows, d), lambda i: (0, 0)),
                   pl.BlockSpec((tail, d // 2), lambda i: (t_prompt // tail, 0)),
                   pl.BlockSpec((ne, tail), lambda i: (0, t_prompt // tail)),
                   pl.BlockSpec((rows, pw), lambda i: (0, 0)),
                   pl.BlockSpec((rows, cw), lambda i: (0, 0))],
        out_shape=[jax.ShapeDtypeStruct((rows, d), F32),
                   jax.ShapeDtypeStruct((t_pad, d // 2), I32),
                   jax.ShapeDtypeStruct((ne, t_pad), F32),
                   jax.ShapeDtypeStruct((rows, pw), F32),
                   jax.ShapeDtypeStruct((rows, cw), F32)],
        compiler_params=pltpu.CompilerParams(dimension_semantics=("arbitrary",),
                                             vmem_limit_bytes=VMEM_LIMIT),
        name="sample_mixer",
    )(*ins)


def _sublane_max(x):
    return jnp.max(x, axis=0, keepdims=True)


def _route_kernel(lg_ref, b_ref, tri_ref, idx_ref, wt_ref, rank_ref, cnt_ref, carry, *, t_valid):
    i = pl.program_id(0)
    ne, tt = lg_ref.shape
    ng = N_EXPERT_GROUPS
    per = ne // ng

    @pl.when(i == 0)
    def _():
        carry[...] = jnp.zeros_like(carry)

    s = _sigmoid(lg_ref[...])
    sel = s + b_ref[...]
    s3 = [s[p * ng:(p + 1) * ng, :] for p in range(per)]
    sel3 = [sel[p * ng:(p + 1) * ng, :] for p in range(per)]
    m1 = sel3[0]
    m2 = jnp.full_like(m1, NEG_INF)
    for p in range(1, per):
        m2 = jnp.maximum(m2, jnp.minimum(m1, sel3[p]))
        m1 = jnp.maximum(m1, sel3[p])
    gs = m1 + m2
    gi = lax.broadcasted_iota(jnp.int32, (ng, tt), 0)
    beaten = jnp.zeros((ng, tt), jnp.int32)
    for g in range(ng):
        row = gs[g:g + 1, :]
        beats = (row > gs) | ((row == gs) & (gi > g))
        beaten = beaten + beats.astype(jnp.int32)
    keep = beaten < TOPK_GROUPS
    cur = [jnp.where(keep, sel3[p], NEG_INF) for p in range(per)]
    eid = [(gi * per + p).astype(F32) for p in range(per)]
    idxs, wts, hits = [], [], []
    for _ in range(TOP_K):
        m = cur[0]
        for p in range(1, per):
            m = jnp.maximum(m, cur[p])
        m = _sublane_max(m)
        cand = jnp.where(cur[0] == m, eid[0], float(ne))
        for p in range(1, per):
            cand = jnp.minimum(cand, jnp.where(cur[p] == m, eid[p], float(ne)))
        e_sel = jnp.min(cand, axis=0, keepdims=True)
        hit = [eid[p] == e_sel for p in range(per)]
        wk = jnp.where(hit[0], s3[0], 0.0)
        for p in range(1, per):
            wk = wk + jnp.where(hit[p], s3[p], 0.0)
        wts.append(jnp.sum(wk, axis=0, keepdims=True))
        cur = [jnp.where(hit[p], NEG_INF, cur[p]) for p in range(per)]
        idxs.append(e_sel)
        hits.append(hit)
    wsum = wts[0]
    for k in range(1, TOP_K):
        wsum = wsum + wts[k]
    chosen = [functools.reduce(lambda a, b: a | b, [hits[k][p] for k in range(TOP_K)]) for p in range(per)]
    real = (i * tt + lax.broadcasted_iota(I32, (1, tt), 1)) < t_valid
    onehot = jnp.concatenate([(c & real).astype(F32) for c in chosen], axis=0)
    before = _dot(onehot.astype(BF16), tri_ref[...]) + carry[...]
    for k in range(TOP_K):
        rk = jnp.where(hits[k][0], before[0:ng, :], 0.0)
        for p in range(1, per):
            rk = rk + jnp.where(hits[k][p], before[p * ng:(p + 1) * ng, :], 0.0)
        rank_ref[k:k + 1, :] = jnp.sum(rk, axis=0, keepdims=True).astype(jnp.int32)
        idx_ref[k:k + 1, :] = idxs[k].astype(jnp.int32)
        wt_ref[k:k + 1, :] = wts[k] / wsum * ROUTED_SCALE
    carry[...] = carry[...] + jnp.sum(onehot, axis=1, keepdims=True)
    cnt_ref[...] = carry[...]


def _route(lgt, b_perm, t_valid):
    ne, t = lgt.shape
    tt = ROUTE_TILE
    tri = (jnp.arange(tt)[:, None] < jnp.arange(tt)[None, :]).astype(BF16)
    return pl.pallas_call(
        functools.partial(_route_kernel, t_valid=t_valid),
        grid=(t // tt,),
        in_specs=[pl.BlockSpec((ne, tt), lambda i: (0, i)),
                  pl.BlockSpec((ne, 1), lambda i: (0, 0)),
                  pl.BlockSpec((tt, tt), lambda i: (0, 0))],
        out_specs=[pl.BlockSpec((TOP_K, tt), lambda i: (0, i)),
                   pl.BlockSpec((TOP_K, tt), lambda i: (0, i)),
                   pl.BlockSpec((TOP_K, tt), lambda i: (0, i)),
                   pl.BlockSpec((ne, 1), lambda i: (0, 0))],
        out_shape=[jax.ShapeDtypeStruct((TOP_K, t), jnp.int32),
                   jax.ShapeDtypeStruct((TOP_K, t), F32),
                   jax.ShapeDtypeStruct((TOP_K, t), jnp.int32),
                   jax.ShapeDtypeStruct((ne, 1), F32)],
        scratch_shapes=[pltpu.VMEM((ne, 1), F32)],
        compiler_params=pltpu.CompilerParams(dimension_semantics=("arbitrary",),
                                             vmem_limit_bytes=VMEM_LIMIT),
        name="route",
    )(lgt, b_perm, tri)


def _dest_kernel(pstart_ref, idx_ref, rank_ref, dest_ref, *, t_valid, spare_row):
    idx = idx_ref[...]
    tile = idx.shape[1]

    def add_start(e, acc):
        return acc + jnp.where(idx == e, pstart_ref[e], 0)

    dest = lax.fori_loop(0, pstart_ref.shape[0], add_start, rank_ref[...])
    tok = pl.program_id(0) * tile + lax.broadcasted_iota(I32, idx.shape, 1)
    dest_ref[...] = jnp.where(tok < t_valid, dest, spare_row)


def _dest_rows(pstart, idx, rank, t_valid, spare_row):
    k_top, t = idx.shape
    tile = DEST_TILE
    spec = pl.BlockSpec((k_top, tile), lambda i, ps: (0, i))
    return pl.pallas_call(
        functools.partial(_dest_kernel, t_valid=t_valid, spare_row=spare_row),
        grid_spec=pltpu.PrefetchScalarGridSpec(num_scalar_prefetch=1, grid=(t // tile,),
                                               in_specs=[spec, spec], out_specs=spec),
        out_shape=jax.ShapeDtypeStruct((k_top, t), I32),
        compiler_params=pltpu.CompilerParams(dimension_semantics=("arbitrary",),
                                             vmem_limit_bytes=VMEM_LIMIT),
        name="dest_rows",
    )(pstart, idx, rank)


def _expert_kernel(first_ref, cnt_ref, xs_hbm, wg_ref, wu_ref, wd_ref, y_hbm,
                   xbuf, ybuf, wgu_bf, wd_bf, in_sem, out_sem):
    e = pl.program_id(0)
    ne = first_ref.shape[0] - 1
    rows = xbuf.shape[1]
    de = wg_ref.shape[2]
    b0, b1 = first_ref[e], first_ref[e + 1]
    n_total = first_ref[ne]

    def in_copy(g, slot):
        return pltpu.make_async_copy(xs_hbm.at[pl.ds(g * rows, rows)], xbuf.at[slot], in_sem.at[slot])

    def out_copy(g, slot):
        return pltpu.make_async_copy(ybuf.at[slot], y_hbm.at[pl.ds(g * rows, rows)], out_sem.at[slot])

    @pl.when((e == 0) & (n_total > 0))
    def _():
        in_copy(0, 0).start()

    wgu_bf[:, :de] = wg_ref[0].astype(BF16)
    wgu_bf[:, de:] = wu_ref[0].astype(BF16)
    wd_bf[...] = wd_ref[0].astype(BF16)

    def block(g, carry):
        slot = g % 2

        @pl.when(g + 1 < n_total)
        def _():
            in_copy(g + 1, 1 - slot).start()

        in_copy(g, slot).wait()

        @pl.when(g >= 2)
        def _():
            out_copy(g - 2, slot).wait()

        valid = cnt_ref[e] - (g - b0) * rows
        rid = lax.broadcasted_iota(I32, (rows, 1), 0)
        xs = jnp.where(rid < valid, _unpack_bf16_pairs(xbuf[slot]), 0.0).astype(BF16)
        gu = _dot(xs, wgu_bf[...])
        hb = _silu(gu[:, :de]) * gu[:, de:]
        ybuf[slot] = _pack_bf16_pairs(_dot(hb.astype(BF16), wd_bf[...]))
        out_copy(g, slot).start()
        return carry

    lax.fori_loop(b0, b1, block, 0)

    @pl.when(e == ne - 1)
    def _():
        @pl.when(n_total >= 2)
        def _():
            out_copy(n_total - 2, n_total % 2).wait()

        @pl.when(n_total >= 1)
        def _():
            out_copy(n_total - 1, (n_total - 1) % 2).wait()


def _experts(xs, w_gate, w_up, w_down, first_block, counts):
    p, dw = xs.shape
    ne, d, de = w_gate.shape
    w_in_spec = pl.BlockSpec((1, d, de), lambda e, fb, cn: (e, 0, 0))
    return pl.pallas_call(
        _expert_kernel,
        grid_spec=pltpu.PrefetchScalarGridSpec(
            num_scalar_prefetch=2,
            grid=(ne,),
            in_specs=[pl.BlockSpec(memory_space=pl.ANY), w_in_spec, w_in_spec,
                      pl.BlockSpec((1, de, d), lambda e, fb, cn: (e, 0, 0))],
            out_specs=pl.BlockSpec(memory_space=pl.ANY),
            scratch_shapes=[pltpu.VMEM((2, EXPERT_ROWS, dw), I32), pltpu.VMEM((2, EXPERT_ROWS, dw), I32),
                            pltpu.VMEM((d, 2 * de), BF16), pltpu.VMEM((de, d), BF16),
                            pltpu.SemaphoreType.DMA((2,)), pltpu.SemaphoreType.DMA((2,))]),
        out_shape=jax.ShapeDtypeStruct((p, dw), I32),
        compiler_params=pltpu.CompilerParams(dimension_semantics=("arbitrary",),
                                             vmem_limit_bytes=VMEM_LIMIT),
        name="experts",
    )(first_block, counts, xs, w_gate, w_up, w_down)


def _sc_worker_id():
    return lax.axis_index("s") * SC_CORES + lax.axis_index("c")


def _sc_start(copies):
    for cp in copies:
        cp.start()


def _sc_wait(copies):
    for cp in copies:
        cp.wait()


def _sc_dispatch(h2, dest, n_rows):
    w = h2.shape[1]
    n_chunks, k_top, ch = dest.shape
    per_worker = n_chunks // SC_WORKERS
    assert per_worker % 2 == 0 and per_worker * SC_WORKERS == n_chunks
    n_pairs = per_worker // 2
    mesh = plsc.VectorSubcoreMesh(core_axis_name="c", subcore_axis_name="s")

    @functools.partial(
        pl.kernel, mesh=mesh, out_type=jax.ShapeDtypeStruct((n_rows, w), I32),
        scratch_types=[pltpu.VMEM((2, k_top, ch), I32), pltpu.VMEM((2, ch, w), I32),
                       pltpu.SemaphoreType.DMA((2,)), pltpu.SemaphoreType.DMA((2,))])
    def dispatch(h2_hbm, dest_hbm, xs_hbm, idx_v, rows_v, load_sem, store_sem):
        first = _sc_worker_id() * per_worker

        def loads(c, slot):
            return (pltpu.make_async_copy(h2_hbm.at[pl.ds(c * ch, ch)], rows_v.at[slot], load_sem.at[slot]),
                    pltpu.make_async_copy(dest_hbm.at[c], idx_v.at[slot], load_sem.at[slot]))

        def scatters(slot):
            return [pltpu.make_async_copy(rows_v.at[slot], xs_hbm.at[idx_v.at[slot, k]], store_sem.at[slot])
                    for k in range(k_top)]

        _sc_start(loads(first, 0))
        _sc_start(loads(first + 1, 1))

        @pl.loop(0, n_pairs)
        def _(j):
            c = first + 2 * j
            for slot in range(2):
                _sc_wait(loads(c + slot, slot))
                _sc_start(scatters(slot))

            @pl.when(j + 1 < n_pairs)
            def _():
                for slot in range(2):
                    _sc_wait(scatters(slot))
                    _sc_start(loads(c + 2 + slot, slot))

        for slot in range(2):
            _sc_wait(scatters(slot))

    return dispatch(h2, dest)


def _sc_combine_gather(y, dest, t):
    w = y.shape[1]
    n_chunks, k_top, ch = dest.shape
    per_worker = n_chunks // SC_WORKERS
    assert per_worker * SC_WORKERS == n_chunks
    mesh = plsc.VectorSubcoreMesh(core_axis_name="c", subcore_axis_name="s")

    @functools.partial(
        pl.kernel, mesh=mesh, out_type=jax.ShapeDtypeStruct((k_top, t, w), I32),
        scratch_types=[pltpu.VMEM((per_worker, k_top, ch), I32), pltpu.VMEM((k_top, ch, w), I32),
                       pltpu.SemaphoreType.DMA((k_top,)), pltpu.SemaphoreType.DMA((k_top,))])
    def gather(y_hbm, dest_hbm, out_hbm, idx_v, rows_v, load_sem, store_sem):
        first = _sc_worker_id() * per_worker
        pltpu.sync_copy(dest_hbm.at[pl.ds(first, per_worker)], idx_v)

        def fetch(i, k):
            return pltpu.make_async_copy(y_hbm.at[idx_v.at[i, k]], rows_v.at[k], load_sem.at[k])

        def put(i, k):
            return pltpu.make_async_copy(rows_v.at[k], out_hbm.at[k, pl.ds((first + i) * ch, ch)], store_sem.at[k])

        @pl.loop(0, per_worker)
        def _(i):
            for k in range(k_top):
                @pl.when(i > 0)
                def _():
                    put(i - 1, k).wait()

                fetch(i, k).start()
            for k in range(k_top):
                fetch(i, k).wait()
                put(i, k).start()

        for k in range(k_top):
            put(per_worker - 1, k).wait()

    return gather(y, dest)


def _final_kernel(xmid_ref, yg_ref, wt_ref, g2_ref, shf_ref, scf_ref, gf_ref, o_ref):
    wt = wt_ref[...]
    routed = wt[:, 0:1] * _unpack_bf16_pairs(yg_ref[0])
    for k in range(1, yg_ref.shape[0]):
        routed = routed + wt[:, k:k + 1] * _unpack_bf16_pairs(yg_ref[k])
    x2 = xmid_ref[...] + g2_ref[...] * routed
    o_ref[...] = _rms_norm(x2, gf_ref[...]) * (1.0 + scf_ref[...]) + shf_ref[...]


def _final_prompt(xmid, yg, wts_t, g2, shf, scf, g_final):
    b, seq, d = xmid.shape
    k_top, _, w = yg.shape
    tl = FINAL_TILE
    nl = seq // tl
    tok = pl.BlockSpec((None, tl, d), lambda i, j: (i, j, 0))
    vec = pl.BlockSpec((None, 1, d), lambda i, j: (i, 0, 0))
    return pl.pallas_call(
        _final_kernel,
        grid=(b, nl),
        in_specs=[tok,
                  pl.BlockSpec((k_top, tl, w), lambda i, j: (0, i * nl + j, 0)),
                  pl.BlockSpec((tl, k_top), lambda i, j: (i * nl + j, 0)),
                  vec, vec, vec, pl.BlockSpec((1, d), lambda i, j: (0, 0))],
        out_specs=tok,
        out_shape=jax.ShapeDtypeStruct((b, seq, d), F32),
        compiler_params=pltpu.CompilerParams(dimension_semantics=("arbitrary", "arbitrary"),
                                             vmem_limit_bytes=VMEM_LIMIT),
        name="final_prompt",
    )(xmid, yg, wts_t, g2[:, None, :], shf[:, None, :], scf[:, None, :], g_final)


def _final_sample(xmid, yg, wts_t, g2, shf, scf, g_final, first_row):
    rows, d = xmid.shape
    k_top, _, w = yg.shape
    blk = first_row // rows
    vec = pl.BlockSpec((rows, d), lambda i: (0, 0))
    return pl.pallas_call(
        _final_kernel,
        grid=(1,),
        in_specs=[vec,
                  pl.BlockSpec((k_top, rows, w), lambda i: (0, blk, 0)),
                  pl.BlockSpec((rows, k_top), lambda i: (blk, 0)),
                  vec, vec, vec, pl.BlockSpec((1, d), lambda i: (0, 0))],
        out_specs=vec,
        out_shape=jax.ShapeDtypeStruct(xmid.shape, F32),
        compiler_params=pltpu.CompilerParams(dimension_semantics=("arbitrary",),
                                             vmem_limit_bytes=VMEM_LIMIT),
        name="final_sample",
    )(xmid, yg, wts_t, g2, shf, scf, g_final)


def _block_diag_pairs(w_pool):
    g, c, _ = w_pool.shape
    eye = jnp.eye(g, dtype=w_pool.dtype)
    return (eye[:, None, :, None] * w_pool[:, :, None, :]).reshape(g * c, g * c)


def kernel(x_prompt, x_sample, state_pool, state_conv, c_prompt, c_sample, w_ada, b_ada, g_mix, w_in, w_pool, pool_scale, w_dw, b_dw, ln_g, ln_b, w_out, g_ffn, w_router, b_router, w_gate, w_up, w_down, ws_gate, ws_up, ws_down, w_ada_final, b_ada_final, g_final):
    bp, seq, d = x_prompt.shape
    bs = x_sample.shape[0]
    depth = w_ada.shape[0]
    assert depth == 1 and x_sample.shape[1] == 1
    conv_k = w_dw.shape[1]
    ne = w_router.shape[-1]
    per = ne // N_EXPERT_GROUPS
    tp = bp * seq
    t_all = tp + bs

    row_expert = (jnp.arange(ne) % N_EXPERT_GROUPS) * per + jnp.arange(ne) // N_EXPERT_GROUPS
    wl = {
        "g_mix": g_mix[0][None, :],
        "in": w_in[0].astype(BF16),
        "pool": _block_diag_pairs(w_pool[0]).astype(BF16),
        "pool_scale": pool_scale[0][None, :],
        "dw": w_dw[0],
        "b_dw": b_dw[0][None, :],
        "ln_g": ln_g[0][None, :],
        "ln_b": ln_b[0][None, :],
        "out": w_out[0].astype(BF16),
        "g_ffn": g_ffn[0][None, :],
        "router_t": w_router[0].T[row_expert].astype(BF16),
        "s_gu": jnp.concatenate([ws_gate[0], ws_up[0]], axis=1).astype(BF16),
        "s_down": ws_down[0].astype(BF16),
    }
    b_perm = b_router[0][row_expert][:, None]

    grain = SC_WORKERS * SC_DISPATCH_CHUNK * SC_COMBINE_CHUNK // math.gcd(SC_DISPATCH_CHUNK, SC_COMBINE_CHUNK)
    t_pad = (t_all + grain - 1) // grain * grain
    assert t_pad % ROUTE_TILE == 0 and t_pad % DEST_TILE == 0

    c_all = jnp.concatenate([c_prompt, c_sample], axis=0)
    mod = _ada(c_all, w_ada[0], b_ada[0])
    modf = _ada(c_all, w_ada_final, b_ada_final)

    sp_t = jnp.transpose(state_pool[0], (1, 0, 2))
    sc_t = jnp.transpose(state_conv[0], (1, 0, 2))
    xmid_s, h2, lgt, u_s, a_s = _sample_mixer(x_sample[:, 0, :], mod[bp:], sp_t, sc_t, wl, tp, t_pad,
                                              conv_k=conv_k)
    xmid_p, h2, lgt, npool_p, nconv_p = _prompt_mixer(x_prompt, mod[:bp], wl, h2, lgt, conv_k=conv_k)

    idx, wts, rank, counts_perm = _route(lgt, b_perm, t_all)
    counts = jnp.zeros((ne,), I32).at[row_expert].set(counts_perm[:, 0].astype(I32))
    nblk = (counts + EXPERT_ROWS - 1) // EXPERT_ROWS
    first_block = jnp.concatenate([jnp.zeros((1,), I32), jnp.cumsum(nblk).astype(I32)])
    n_blocks = (t_all * TOP_K + EXPERT_ROWS - 1) // EXPERT_ROWS + ne
    n_rows = (n_blocks + 1) * EXPERT_ROWS
    dest = _dest_rows(first_block[:ne] * EXPERT_ROWS, idx, rank, t_all, n_rows - 1)

    def chunked(ch):
        return dest.reshape(TOP_K, t_pad // ch, ch).transpose(1, 0, 2)

    xs = _sc_dispatch(h2, chunked(SC_DISPATCH_CHUNK), n_rows)
    y = _experts(xs, w_gate[0], w_up[0], w_down[0], first_block, counts)
    yg = _sc_combine_gather(y, chunked(SC_COMBINE_CHUNK), t_pad)

    g2 = mod[:, 5 * d:6 * d]
    shf, scf = modf[:, :d], modf[:, d:]
    wts_t = wts.T
    y_prompt = _final_prompt(xmid_p, yg, wts_t, g2[:bp], shf[:bp], scf[:bp], g_final[None, :])
    y_sample = _final_sample(xmid_s, yg, wts_t, g2[bp:], shf[bp:], scf[bp:], g_final[None, :], tp)

    new_pool_s = jnp.concatenate([state_pool[0][:, 1:], u_s[:, None, :]], axis=1)
    new_conv_s = jnp.concatenate([state_conv[0][:, 1:], a_s[:, None, :]], axis=1)
    return (y_prompt, y_sample[:, None, :], npool_p[None], nconv_p[None], new_pool_s[None], new_conv_s[None])
```

```python
import functools
import math

import jax
import jax.numpy as jnp
from jax import lax
from jax.experimental import pallas as pl
from jax.experimental.pallas import tpu as pltpu
from jax.experimental.pallas import tpu_sc as plsc

POOL_WINDOWS = (2, 4, 8, 16)
N_EXPERT_GROUPS = 8
TOPK_GROUPS = 4
TOP_K = 8
ROUTED_SCALE = 2.5
EPS = 1e-6

LANES = 128
SUBLANES = 8
BF16_ROWS = 16
VMEM_LIMIT = 52 * 1024 * 1024

SEQ_TILE = 512
ROW_CHUNK = 16
ROUTE_TILE = 384
EXPERT_ROWS = 256
EXPERT_BUFFERS = 4
FINAL_TILE = 512
DEST_TILE = 2816

SC_CORES = 2
SC_SUBCORES = 16
SC_WORKERS = SC_CORES * SC_SUBCORES
SC_CORE_SHARE = (2, 1)
SC_DISPATCH_CHUNK = 88
SC_COMBINE_CHUNK = 24

F32 = jnp.float32
BF16 = jnp.bfloat16
I32 = jnp.int32
U32 = jnp.uint32
NEG_INF = float("-inf")
HI16 = 0xFFFF0000


def _sigmoid(x):
    return 1.0 / (1.0 + jnp.exp(-x))


def _silu(x):
    return x * _sigmoid(x)


def _rms_norm(x, g):
    return x * lax.rsqrt(jnp.mean(x * x, axis=-1, keepdims=True) + EPS) * g


def _dot(a, b):
    return jnp.dot(a, b, preferred_element_type=F32)


def _pack_bf16_pairs(x):
    w = x.shape[1] // 2
    bits = lax.bitcast_convert_type(x.astype(BF16).astype(F32), U32)
    return lax.bitcast_convert_type((bits[:, :w] >> 16) | (bits[:, w:] & U32(HI16)), I32)


def _unpack_bf16_pairs(p):
    bits = lax.bitcast_convert_type(p, U32)
    lo = lax.bitcast_convert_type(bits << 16, F32)
    hi = lax.bitcast_convert_type(bits & U32(HI16), F32)
    return jnp.concatenate([lo, hi], axis=1)


def _ada_kernel(c_ref, w_ref, b_ref, o_ref):
    s = _silu(c_ref[...])
    o_ref[...] = _dot(s.astype(BF16), w_ref[...].astype(BF16)) + b_ref[...]


def _ada(c, w, b):
    rows, d = c.shape
    n = w.shape[1]
    tn = 1024
    return pl.pallas_call(
        _ada_kernel,
        grid=(n // tn,),
        in_specs=[pl.BlockSpec((rows, d), lambda j: (0, 0)),
                  pl.BlockSpec((d, tn), lambda j: (0, j)),
                  pl.BlockSpec((1, tn), lambda j: (0, j))],
        out_specs=pl.BlockSpec((rows, tn), lambda j: (0, j)),
        out_shape=jax.ShapeDtypeStruct((rows, n), F32),
        compiler_params=pltpu.CompilerParams(dimension_semantics=("arbitrary",),
                                             vmem_limit_bytes=VMEM_LIMIT),
        name="ada",
    )(c, w, b.reshape(1, n))


def _mixer_tail(x, pool_d, conv_act, mod, w, xmid_ref, h2_ref, lgt_ref):
    sh2, sc2, g1, g2 = mod
    pw = pool_d.shape[1]
    pool_out = _dot(pool_d, w["pool"][...]) * w["pool_scale"][...]
    mix = _dot(pool_out.astype(BF16), w["out"][:pw, :]) + _dot(conv_act, w["out"][pw:, :])
    x1 = x + g1 * mix
    h2f = _rms_norm(x1, w["g_ffn"][...]) * (1.0 + sc2) + sh2
    h2_ref[...] = _pack_bf16_pairs(h2f)
    h2 = h2f.astype(BF16)
    lgt_ref[...] = lax.dot_general(w["router_t"][...], h2, (((1,), (1,)), ((), ())),
                                   preferred_element_type=F32)
    gu = _dot(h2, w["s_gu"][...])
    de = gu.shape[1] // 2
    hs = _silu(gu[:, :de]) * gu[:, de:]
    shared = _dot(hs.astype(BF16), w["s_down"][...])
    xmid_ref[...] = x1 + g2 * shared


_W_NAMES = ("g_mix", "in", "pool", "pool_scale", "dw", "b_dw", "ln_g", "ln_b", "out", "g_ffn",
            "router_t", "s_gu", "s_down")


def _layer_norm_silu(yc, g, b):
    mu = jnp.mean(yc, axis=-1, keepdims=True)
    yz = yc - mu
    var = jnp.mean(yz * yz, axis=-1, keepdims=True)
    return _silu(yz * lax.rsqrt(var + EPS) * g + b)


U_HALO = 32
A_HALO = 32


def _prompt_mixer_kernel(x_ref, mod_ref, *refs, tl, d, pw, cw, conv_k):
    nw = len(_W_NAMES)
    w = dict(zip(_W_NAMES, refs[:nw]))
    xmid_ref, h2_ref, lgt_ref, npool_ref, nconv_ref = refs[nw + 2:nw + 7]
    ubuf, s2buf, s4buf, s8buf, abuf, ashift, dbuf, cbuf = refs[nw + 7:]
    l = pl.program_id(1)
    pool_buf = max(POOL_WINDOWS) - 1
    uh, ah = U_HALO, A_HALO
    pg = pw // len(POOL_WINDOWS)
    nu, na = uh + tl, ah + tl

    @pl.when(l == 0)
    def _():
        ubuf[0:uh, :] = jnp.zeros((uh, pw), F32)
        abuf[0:ah, :] = jnp.zeros((ah, cw), F32)

    @pl.when(l > 0)
    def _():
        ubuf[0:uh, :] = ubuf[tl:tl + uh, :]
        abuf[0:ah, :] = abuf[tl:tl + ah, :]

    x = x_ref[0]
    sh1 = mod_ref[0, :, 0 * d:1 * d]
    sc1 = mod_ref[0, :, 1 * d:2 * d]
    g1 = mod_ref[0, :, 2 * d:3 * d]
    sh2 = mod_ref[0, :, 3 * d:4 * d]
    sc2 = mod_ref[0, :, 4 * d:5 * d]
    g2 = mod_ref[0, :, 5 * d:6 * d]

    h = _rms_norm(x, w["g_mix"][...]) * (1.0 + sc1) + sh1
    z = _dot(h.astype(BF16), w["in"][...])
    u = z[:, :pw]
    ubuf[uh:nu, :] = u
    abuf[ah:na, :] = z[:, pw:pw + cw] * _sigmoid(z[:, pw + cw:])

    s2buf[8:nu, :] = ubuf[8:nu, :] + ubuf[7:nu - 1, :]
    s4buf[16:nu, :] = s2buf[16:nu, pg:] + s2buf[14:nu - 2, pg:]
    s8buf[24:nu, :] = s4buf[24:nu, pg:] + s4buf[20:nu - 4, pg:]
    t = l * tl + lax.broadcasted_iota(I32, (tl, 1), 0)
    sums = (s2buf[uh:nu, 0:pg], s4buf[uh:nu, 0:pg], s8buf[uh:nu, 0:pg],
            s8buf[uh:nu, pg:2 * pg] + s8buf[uh - 8:nu - 8, pg:2 * pg])
    for g, win in enumerate(POOL_WINDOWS):
        cols = slice(g * pg, (g + 1) * pg)
        inv = 1.0 / jnp.minimum(win, t + 1).astype(F32)
        dbuf[:, cols] = (sums[g] * inv - u[:, cols]).astype(BF16)

    for r in range(1, SUBLANES):
        ashift[r - 1, 8:na, :] = abuf[8 - r:na - r, :]

    for c in range(tl // ROW_CHUNK):
        o = ah + c * ROW_CHUNK
        yc = abuf[o:o + ROW_CHUNK, :] * w["dw"][conv_k - 1:conv_k, :]
        for back in range(1, conv_k):
            q, r = divmod(back, SUBLANES)
            src = abuf if r == 0 else ashift.at[r - 1]
            yc = yc + src[o - q * SUBLANES:o - q * SUBLANES + ROW_CHUNK, :] * w["dw"][conv_k - 1 - back:conv_k - back, :]
        yc = yc + w["b_dw"][...]
        r0 = c * ROW_CHUNK
        cbuf[r0:r0 + ROW_CHUNK, :] = _layer_norm_silu(yc, w["ln_g"][...], w["ln_b"][...]).astype(BF16)

    @pl.when(l == pl.num_programs(1) - 1)
    def _():
        npool_ref[0] = ubuf[nu - pool_buf:nu, :]
        nconv_ref[0] = abuf[na - (conv_k - 1):na, :]

    _mixer_tail(x, dbuf[...], cbuf[...], (sh2, sc2, g1, g2), w, xmid_ref.at[0], h2_ref, lgt_ref)


def _sample_mixer_kernel(x_ref, mod_ref, sp_ref, sc_ref, *refs, d, pw, cw, conv_k):
    nw = len(_W_NAMES)
    w = dict(zip(_W_NAMES, refs[:nw]))
    xmid_ref, h2_ref, lgt_ref, u_ref, a_ref = refs[nw:nw + 5]
    pool_buf = max(POOL_WINDOWS) - 1
    pg = pw // len(POOL_WINDOWS)
    x = x_ref[...]
    sh1, sc1, g1, sh2, sc2, g2 = [mod_ref[:, i * d:(i + 1) * d] for i in range(6)]
    h = _rms_norm(x, w["g_mix"][...]) * (1.0 + sc1) + sh1
    z = _dot(h.astype(BF16), w["in"][...])
    u = z[:, :pw]
    a = z[:, pw:pw + cw] * _sigmoid(z[:, pw + cw:])
    u_ref[...] = u
    a_ref[...] = a
    ds = []
    for g, win in enumerate(POOL_WINDOWS):
        cols = slice(g * pg, (g + 1) * pg)
        acc = u[:, cols]
        for j in range(1, win):
            acc = acc + sp_ref[pool_buf - j, :, cols]
        cnt = float(min(win, 1 + pool_buf))
        ds.append(acc / cnt - u[:, cols])
    pool_d = jnp.concatenate(ds, axis=-1).astype(BF16)
    yc = a * w["dw"][conv_k - 1:conv_k, :]
    for k in range(conv_k - 1):
        yc = yc + sc_ref[k] * w["dw"][k:k + 1, :]
    yc = yc + w["b_dw"][...]
    conv_act = _layer_norm_silu(yc, w["ln_g"][...], w["ln_b"][...]).astype(BF16)
    rows = x.shape[0]
    h2_ref[...] = jnp.zeros_like(h2_ref)
    lgt_ref[...] = jnp.zeros_like(lgt_ref)
    _mixer_tail(x, pool_d, conv_act, (sh2, sc2, g1, g2), w, xmid_ref,
                h2_ref.at[pl.ds(0, rows), :], lgt_ref.at[:, pl.ds(0, rows)])


def _full_spec(a):
    nd = a.ndim
    return pl.BlockSpec(a.shape, lambda *_: (0,) * nd)


def _prompt_mixer(x, mod, wl, h2_all, lgt_all, *, conv_k):
    b, seq, d = x.shape
    pw = wl["pool"].shape[0]
    cw = wl["dw"].shape[1]
    ne = wl["router_t"].shape[0]
    tl = SEQ_TILE
    nl = seq // tl
    pg = pw // len(POOL_WINDOWS)
    assert POOL_WINDOWS == (2, 4, 8, 16) and conv_k - 1 <= A_HALO
    ws = [wl[n] for n in _W_NAMES]
    kern = functools.partial(_prompt_mixer_kernel, tl=tl, d=d, pw=pw, cw=cw, conv_k=conv_k)
    return pl.pallas_call(
        kern,
        grid=(b, nl),
        in_specs=[pl.BlockSpec((1, tl, d), lambda i, j: (i, j, 0)),
                  pl.BlockSpec((1, 1, mod.shape[-1]), lambda i, j: (i, 0, 0))]
                 + [_full_spec(a) for a in ws]
                 + [pl.BlockSpec(memory_space=pl.ANY), pl.BlockSpec(memory_space=pl.ANY)],
        out_specs=[pl.BlockSpec((1, tl, d), lambda i, j: (i, j, 0)),
                   pl.BlockSpec((tl, d // 2), lambda i, j: (i * nl + j, 0)),
                   pl.BlockSpec((ne, tl), lambda i, j: (0, i * nl + j)),
                   pl.BlockSpec((1, max(POOL_WINDOWS) - 1, pw), lambda i, j: (i, 0, 0)),
                   pl.BlockSpec((1, conv_k - 1, cw), lambda i, j: (i, 0, 0))],
        out_shape=[jax.ShapeDtypeStruct((b, seq, d), F32),
                   jax.ShapeDtypeStruct(h2_all.shape, I32),
                   jax.ShapeDtypeStruct(lgt_all.shape, F32),
                   jax.ShapeDtypeStruct((b, max(POOL_WINDOWS) - 1, pw), F32),
                   jax.ShapeDtypeStruct((b, conv_k - 1, cw), F32)],
        scratch_shapes=[pltpu.VMEM((U_HALO + tl, pw), F32), pltpu.VMEM((U_HALO + tl, pw), F32),
                        pltpu.VMEM((U_HALO + tl, pw - pg), F32), pltpu.VMEM((U_HALO + tl, pw - 2 * pg), F32),
                        pltpu.VMEM((A_HALO + tl, cw), F32), pltpu.VMEM((SUBLANES - 1, A_HALO + tl, cw), F32),
                        pltpu.VMEM((tl, pw), BF16), pltpu.VMEM((tl, cw), BF16)],
        compiler_params=pltpu.CompilerParams(dimension_semantics=("arbitrary", "arbitrary"),
                                             vmem_limit_bytes=VMEM_LIMIT),
        input_output_aliases={2 + len(ws): 1, 3 + len(ws): 2},
        name="prompt_mixer",
    )(x, mod.reshape(b, 1, -1), *ws, h2_all, lgt_all)


def _sample_mixer(x, mod, sp_t, sc_t, wl, t_prompt, t_pad, *, conv_k):
    rows, d = x.shape
    tail = t_pad - t_prompt
    assert t_prompt % tail == 0 and rows <= tail
    pw = wl["pool"].shape[0]
    cw = wl["dw"].shape[1]
    ne = wl["router_t"].shape[0]
    ws = [wl[n] for n in _W_NAMES]
    kern = functools.partial(_sample_mixer_kernel, d=d, pw=pw, cw=cw, conv_k=conv_k)
    ins = [x, mod, sp_t, sc_t] + ws
    return pl.pallas_call(
        kern,
        grid=(1,),
        in_specs=[_full_spec(a) for a in ins],
        out_specs=[pl.BlockSpec((rows, d), lambda i: (0, 0)),
                   pl.BlockSpec((tail, d // 2), lambda i: (t_prompt // tail, 0)),
                   pl.BlockSpec((ne, tail), lambda i: (0, t_prompt // tail)),
                   pl.BlockSpec((rows, pw), lambda i: (0, 0)),
                   pl.BlockSpec((rows, cw), lambda i: (0, 0))],
        out_shape=[jax.ShapeDtypeStruct((rows, d), F32),
                   jax.ShapeDtypeStruct((t_pad, d // 2), I32),
                   jax.ShapeDtypeStruct((ne, t_pad), F32),
                   jax.ShapeDtypeStruct((rows, pw), F32),
                   jax.ShapeDtypeStruct((rows, cw), F32)],
        compiler_params=pltpu.CompilerParams(dimension_semantics=("arbitrary",),
                                             vmem_limit_bytes=VMEM_LIMIT),
        name="sample_mixer",
    )(*ins)


def _sublane_max(x):
    return jnp.max(x, axis=0, keepdims=True)


def _route_kernel(lg_ref, b_ref, tri_ref, idx_ref, wt_ref, rank_ref, cnt_ref, carry, *, t_valid):
    i = pl.program_id(0)
    ne, tt = lg_ref.shape
    ng = N_EXPERT_GROUPS
    per = ne // ng

    @pl.when(i == 0)
    def _():
        carry[...] = jnp.zeros_like(carry)

    s = _sigmoid(lg_ref[...])
    sel = s + b_ref[...]
    s3 = [s[p * ng:(p + 1) * ng, :] for p in range(per)]
    sel3 = [sel[p * ng:(p + 1) * ng, :] for p in range(per)]
    m1 = sel3[0]
    m2 = jnp.full_like(m1, NEG_INF)
    for p in range(1, per):
        m2 = jnp.maximum(m2, jnp.minimum(m1, sel3[p]))
        m1 = jnp.maximum(m1, sel3[p])
    gs = m1 + m2
    gi = lax.broadcasted_iota(jnp.int32, (ng, tt), 0)
    beaten = jnp.zeros((ng, tt), jnp.int32)
    for g in range(ng):
        row = gs[g:g + 1, :]
        beats = (row > gs) | ((row == gs) & (gi > g))
        beaten = beaten + beats.astype(jnp.int32)
    keep = beaten < TOPK_GROUPS
    cur = [jnp.where(keep, sel3[p], NEG_INF) for p in range(per)]
    eid = [(gi * per + p).astype(F32) for p in range(per)]
    idxs, wts, hits = [], [], []
    for _ in range(TOP_K):
        m = cur[0]
        for p in range(1, per):
            m = jnp.maximum(m, cur[p])
        m = _sublane_max(m)
        cand = jnp.where(cur[0] == m, eid[0], float(ne))
        for p in range(1, per):
            cand = jnp.minimum(cand, jnp.where(cur[p] == m, eid[p], float(ne)))
        e_sel = jnp.min(cand, axis=0, keepdims=True)
        hit = [eid[p] == e_sel for p in range(per)]
        wk = jnp.where(hit[0], s3[0], 0.0)
        for p in range(1, per):
            wk = wk + jnp.where(hit[p], s3[p], 0.0)
        wts.append(jnp.sum(wk, axis=0, keepdims=True))
        cur = [jnp.where(hit[p], NEG_INF, cur[p]) for p in range(per)]
        idxs.append(e_sel)
        hits.append(hit)
    wsum = wts[0]
    for k in range(1, TOP_K):
        wsum = wsum + wts[k]
    chosen = [functools.reduce(lambda a, b: a | b, [hits[k][p] for k in range(TOP_K)]) for p in range(per)]
    real = (i * tt + lax.broadcasted_iota(I32, (1, tt), 1)) < t_valid
    onehot = jnp.concatenate([(c & real).astype(F32) for c in chosen], axis=0)
    before = _dot(onehot.astype(BF16), tri_ref[...]) + carry[...]
    for k in range(TOP_K):
        rk = jnp.where(hits[k][0], before[0:ng, :], 0.0)
        for p in range(1, per):
            rk = rk + jnp.where(hits[k][p], before[p * ng:(p + 1) * ng, :], 0.0)
        rank_ref[k:k + 1, :] = jnp.sum(rk, axis=0, keepdims=True).astype(jnp.int32)
        idx_ref[k:k + 1, :] = idxs[k].astype(jnp.int32)
        wt_ref[k:k + 1, :] = wts[k] / wsum * ROUTED_SCALE
    carry[...] = carry[...] + jnp.sum(onehot, axis=1, keepdims=True)
    cnt_ref[...] = carry[...]


def _route(lgt, b_perm, t_valid):
    ne, t = lgt.shape
    tt = ROUTE_TILE
    tri = (jnp.arange(tt)[:, None] < jnp.arange(tt)[None, :]).astype(BF16)
    return pl.pallas_call(
        functools.partial(_route_kernel, t_valid=t_valid),
        grid=(t // tt,),
        in_specs=[pl.BlockSpec((ne, tt), lambda i: (0, i)),
                  pl.BlockSpec((ne, 1), lambda i: (0, 0)),
                  pl.BlockSpec((tt, tt), lambda i: (0, 0))],
        out_specs=[pl.BlockSpec((TOP_K, tt), lambda i: (0, i)),
                   pl.BlockSpec((TOP_K, tt), lambda i: (0, i)),
                   pl.BlockSpec((TOP_K, tt), lambda i: (0, i)),
                   pl.BlockSpec((ne, 1), lambda i: (0, 0))],
        out_shape=[jax.ShapeDtypeStruct((TOP_K, t), jnp.int32),
                   jax.ShapeDtypeStruct((TOP_K, t), F32),
                   jax.ShapeDtypeStruct((TOP_K, t), jnp.int32),
                   jax.ShapeDtypeStruct((ne, 1), F32)],
        scratch_shapes=[pltpu.VMEM((ne, 1), F32)],
        compiler_params=pltpu.CompilerParams(dimension_semantics=("arbitrary",),
                                             vmem_limit_bytes=VMEM_LIMIT),
        name="route",
    )(lgt, b_perm, tri)


def _dest_kernel(pstart_ref, idx_ref, rank_ref, dest_ref, *, t_valid, spare_row):
    idx = idx_ref[...]
    tile = idx.shape[1]

    def add_start(e, acc):
        return acc + jnp.where(idx == e, pstart_ref[e], 0)

    dest = lax.fori_loop(0, pstart_ref.shape[0], add_start, rank_ref[...])
    tok = pl.program_id(0) * tile + lax.broadcasted_iota(I32, idx.shape, 1)
    dest_ref[...] = jnp.where(tok < t_valid, dest, spare_row)


def _dest_rows(pstart, idx, rank, t_valid, spare_row):
    k_top, t = idx.shape
    tile = DEST_TILE
    spec = pl.BlockSpec((k_top, tile), lambda i, ps: (0, i))
    return pl.pallas_call(
        functools.partial(_dest_kernel, t_valid=t_valid, spare_row=spare_row),
        grid_spec=pltpu.PrefetchScalarGridSpec(num_scalar_prefetch=1, grid=(t // tile,),
                                               in_specs=[spec, spec], out_specs=spec),
        out_shape=jax.ShapeDtypeStruct((k_top, t), I32),
        compiler_params=pltpu.CompilerParams(dimension_semantics=("arbitrary",),
                                             vmem_limit_bytes=VMEM_LIMIT),
        name="dest_rows",
    )(pstart, idx, rank)


def _expert_kernel(first_ref, cnt_ref, xs_hbm, wg_ref, wu_ref, wd_ref, y_hbm,
                   xbuf, ybuf, wgu_bf, wd_bf, in_sem, out_sem):
    e = pl.program_id(0)
    ne = first_ref.shape[0] - 1
    nbuf, rows = xbuf.shape[:2]
    de = wg_ref.shape[2]
    b0, b1 = first_ref[e], first_ref[e + 1]
    n_total = first_ref[ne]

    def in_copy(g):
        return pltpu.make_async_copy(xs_hbm.at[pl.ds(g * rows, rows)], xbuf.at[g % nbuf], in_sem.at[g % nbuf])

    def out_copy(g):
        return pltpu.make_async_copy(ybuf.at[g % nbuf], y_hbm.at[pl.ds(g * rows, rows)], out_sem.at[g % nbuf])

    @pl.when(e == 0)
    def _():
        for g in range(nbuf - 1):
            @pl.when(g < n_total)
            def _():
                in_copy(g).start()

    wgu_bf[:, :de] = wg_ref[0].astype(BF16)
    wgu_bf[:, de:] = wu_ref[0].astype(BF16)
    wd_bf[...] = wd_ref[0].astype(BF16)

    def block(g, carry):
        slot = g % nbuf

        @pl.when(g + nbuf - 1 < n_total)
        def _():
            in_copy(g + nbuf - 1).start()

        in_copy(g).wait()

        @pl.when(g >= nbuf)
        def _():
            out_copy(g - nbuf).wait()

        valid = cnt_ref[e] - (g - b0) * rows
        half = rows // 2
        for r0 in (0, half):
            rid = r0 + lax.broadcasted_iota(I32, (half, 1), 0)
            xs = jnp.where(rid < valid, _unpack_bf16_pairs(xbuf[slot, r0:r0 + half, :]), 0.0).astype(BF16)
            gu = _dot(xs, wgu_bf[...])
            hb = _silu(gu[:, :de]) * gu[:, de:]
            ybuf[slot, r0:r0 + half, :] = _pack_bf16_pairs(_dot(hb.astype(BF16), wd_bf[...]))
        out_copy(g).start()
        return carry

    lax.fori_loop(b0, b1, block, 0)

    @pl.when(e == ne - 1)
    def _():
        for back in range(nbuf, 0, -1):
            @pl.when(n_total >= back)
            def _():
                out_copy(n_total - back).wait()


def _experts(xs, w_gate, w_up, w_down, first_block, counts):
    p, dw = xs.shape
    ne, d, de = w_gate.shape
    w_in_spec = pl.BlockSpec((1, d, de), lambda e, fb, cn: (e, 0, 0))
    return pl.pallas_call(
        _expert_kernel,
        grid_spec=pltpu.PrefetchScalarGridSpec(
            num_scalar_prefetch=2,
            grid=(ne,),
            in_specs=[pl.BlockSpec(memory_space=pl.ANY), w_in_spec, w_in_spec,
                      pl.BlockSpec((1, de, d), lambda e, fb, cn: (e, 0, 0))],
            out_specs=pl.BlockSpec(memory_space=pl.ANY),
            scratch_shapes=[pltpu.VMEM((EXPERT_BUFFERS, EXPERT_ROWS, dw), I32),
                            pltpu.VMEM((EXPERT_BUFFERS, EXPERT_ROWS, dw), I32),
                            pltpu.VMEM((d, 2 * de), BF16), pltpu.VMEM((de, d), BF16),
                            pltpu.SemaphoreType.DMA((EXPERT_BUFFERS,)), pltpu.SemaphoreType.DMA((EXPERT_BUFFERS,))]),
        out_shape=jax.ShapeDtypeStruct((p, dw), I32),
        compiler_params=pltpu.CompilerParams(dimension_semantics=("arbitrary",),
                                             vmem_limit_bytes=VMEM_LIMIT),
        name="experts",
    )(first_block, counts, xs, w_gate, w_up, w_down)


def _sc_share(n_chunks, even):
    per_pair = n_chunks // SC_SUBCORES
    assert per_pair * SC_SUBCORES == n_chunks
    n0 = per_pair * SC_CORE_SHARE[0] // sum(SC_CORE_SHARE)
    if even:
        n0 += n0 % 2
        assert (per_pair - n0) % 2 == 0
    n1 = per_pair - n0
    assert n0 >= n1 >= 2
    core = lax.axis_index("c")
    first = lax.axis_index("s") * per_pair + core * n0
    return n0, n1, core, first, jnp.where(core == 0, n0, n1)


def _sc_start(copies):
    for cp in copies:
        cp.start()


def _sc_wait(copies):
    for cp in copies:
        cp.wait()


def _sc_dispatch(h2, dest, n_rows):
    w = h2.shape[1]
    n_chunks, k_top, ch = dest.shape
    mesh = plsc.VectorSubcoreMesh(core_axis_name="c", subcore_axis_name="s")

    @functools.partial(
        pl.kernel, mesh=mesh, out_type=jax.ShapeDtypeStruct((n_rows, w), I32),
        scratch_types=[pltpu.VMEM((2, k_top, ch), I32), pltpu.VMEM((2, ch, w), I32),
                       pltpu.SemaphoreType.DMA((2,)), pltpu.SemaphoreType.DMA((2,))])
    def dispatch(h2_hbm, dest_hbm, xs_hbm, idx_v, rows_v, load_sem, store_sem):
        _, _, _, first, count = _sc_share(n_chunks, even=True)
        n_pairs = count // 2

        def loads(c, slot):
            return (pltpu.make_async_copy(h2_hbm.at[pl.ds(c * ch, ch)], rows_v.at[slot], load_sem.at[slot]),
                    pltpu.make_async_copy(dest_hbm.at[c], idx_v.at[slot], load_sem.at[slot]))

        def scatters(slot):
            return [pltpu.make_async_copy(rows_v.at[slot], xs_hbm.at[idx_v.at[slot, k]], store_sem.at[slot])
                    for k in range(k_top)]

        _sc_start(loads(first, 0))
        _sc_start(loads(first + 1, 1))

        @pl.loop(0, n_pairs)
        def _(j):
            c = first + 2 * j
            for slot in range(2):
                _sc_wait(loads(c + slot, slot))
                _sc_start(scatters(slot))

            @pl.when(j + 1 < n_pairs)
            def _():
                for slot in range(2):
                    _sc_wait(scatters(slot))
                    _sc_start(loads(c + 2 + slot, slot))

        for slot in range(2):
            _sc_wait(scatters(slot))

    return dispatch(h2, dest)


def _sc_combine_gather(y, dest, t):
    w = y.shape[1]
    n_chunks, k_top, ch = dest.shape
    n_max = _sc_share_max(n_chunks)
    mesh = plsc.VectorSubcoreMesh(core_axis_name="c", subcore_axis_name="s")

    @functools.partial(
        pl.kernel, mesh=mesh, out_type=jax.ShapeDtypeStruct((k_top, t, w), I32),
        scratch_types=[pltpu.VMEM((n_max, k_top, ch), I32), pltpu.VMEM((k_top, ch, w), I32),
                       pltpu.SemaphoreType.DMA((k_top,)), pltpu.SemaphoreType.DMA((k_top,))])
    def gather(y_hbm, dest_hbm, out_hbm, idx_v, rows_v, load_sem, store_sem):
        n0, n1, core, first, count = _sc_share(n_chunks, even=False)
        skip = core * (n0 - n1)
        pltpu.sync_copy(dest_hbm.at[pl.ds(first - skip, n0)], idx_v)

        def fetch(i, k):
            return pltpu.make_async_copy(y_hbm.at[idx_v.at[skip + i, k]], rows_v.at[k], load_sem.at[k])

        def put(i, k):
            return pltpu.make_async_copy(rows_v.at[k], out_hbm.at[k, pl.ds((first + i) * ch, ch)], store_sem.at[k])

        @pl.loop(0, count)
        def _(i):
            for k in range(k_top):
                @pl.when(i > 0)
                def _():
                    put(i - 1, k).wait()

                fetch(i, k).start()
            for k in range(k_top):
                fetch(i, k).wait()
                put(i, k).start()

        for k in range(k_top):
            put(count - 1, k).wait()

    return gather(y, dest)


def _sc_share_max(n_chunks):
    per_pair = n_chunks // SC_SUBCORES
    return per_pair * SC_CORE_SHARE[0] // sum(SC_CORE_SHARE)


def _final_kernel(xmid_ref, yg_ref, wt_ref, g2_ref, shf_ref, scf_ref, gf_ref, o_ref):
    wt = wt_ref[...]
    routed = wt[:, 0:1] * _unpack_bf16_pairs(yg_ref[0])
    for k in range(1, yg_ref.shape[0]):
        routed = routed + wt[:, k:k + 1] * _unpack_bf16_pairs(yg_ref[k])
    x2 = xmid_ref[...] + g2_ref[...] * routed
    o_ref[...] = _rms_norm(x2, gf_ref[...]) * (1.0 + scf_ref[...]) + shf_ref[...]


def _final_prompt(xmid, yg, wts_t, g2, shf, scf, g_final):
    b, seq, d = xmid.shape
    k_top, _, w = yg.shape
    tl = FINAL_TILE
    nl = seq // tl
    tok = pl.BlockSpec((None, tl, d), lambda i, j: (i, j, 0))
    vec = pl.BlockSpec((None, 1, d), lambda i, j: (i, 0, 0))
    return pl.pallas_call(
        _final_kernel,
        grid=(b, nl),
        in_specs=[tok,
                  pl.BlockSpec((k_top, tl, w), lambda i, j: (0, i * nl + j, 0)),
                  pl.BlockSpec((tl, k_top), lambda i, j: (i * nl + j, 0)),
                  vec, vec, vec, pl.BlockSpec((1, d), lambda i, j: (0, 0))],
        out_specs=tok,
        out_shape=jax.ShapeDtypeStruct((b, seq, d), F32),
        compiler_params=pltpu.CompilerParams(dimension_semantics=("arbitrary", "arbitrary"),
                                             vmem_limit_bytes=VMEM_LIMIT),
        name="final_prompt",
    )(xmid, yg, wts_t, g2[:, None, :], shf[:, None, :], scf[:, None, :], g_final)


def _final_sample(xmid, yg, wts_t, g2, shf, scf, g_final, first_row):
    rows, d = xmid.shape
    k_top, _, w = yg.shape
    blk = first_row // rows
    vec = pl.BlockSpec((rows, d), lambda i: (0, 0))
    return pl.pallas_call(
        _final_kernel,
        grid=(1,),
        in_specs=[vec,
                  pl.BlockSpec((k_top, rows, w), lambda i: (0, blk, 0)),
                  pl.BlockSpec((rows, k_top), lambda i: (blk, 0)),
                  vec, vec, vec, pl.BlockSpec((1, d), lambda i: (0, 0))],
        out_specs=vec,
        out_shape=jax.ShapeDtypeStruct(xmid.shape, F32),
        compiler_params=pltpu.CompilerParams(dimension_semantics=("arbitrary",),
                                             vmem_limit_bytes=VMEM_LIMIT),
        name="final_sample",
    )(xmid, yg, wts_t, g2, shf, scf, g_final)


def _block_diag_pairs(w_pool):
    g, c, _ = w_pool.shape
    eye = jnp.eye(g, dtype=w_pool.dtype)
    return (eye[:, None, :, None] * w_pool[:, :, None, :]).reshape(g * c, g * c)


def kernel(x_prompt, x_sample, state_pool, state_conv, c_prompt, c_sample, w_ada, b_ada, g_mix, w_in, w_pool, pool_scale, w_dw, b_dw, ln_g, ln_b, w_out, g_ffn, w_router, b_router, w_gate, w_up, w_down, ws_gate, ws_up, ws_down, w_ada_final, b_ada_final, g_final):
    bp, seq, d = x_prompt.shape
    bs = x_sample.shape[0]
    depth = w_ada.shape[0]
    assert depth == 1 and x_sample.shape[1] == 1
    conv_k = w_dw.shape[1]
    ne = w_router.shape[-1]
    per = ne // N_EXPERT_GROUPS
    tp = bp * seq
    t_all = tp + bs

    row_expert = (jnp.arange(ne) % N_EXPERT_GROUPS) * per + jnp.arange(ne) // N_EXPERT_GROUPS
    wl = {
        "g_mix": g_mix[0][None, :],
        "in": w_in[0].astype(BF16),
        "pool": _block_diag_pairs(w_pool[0]).astype(BF16),
        "pool_scale": pool_scale[0][None, :],
        "dw": w_dw[0],
        "b_dw": b_dw[0][None, :],
        "ln_g": ln_g[0][None, :],
        "ln_b": ln_b[0][None, :],
        "out": w_out[0].astype(BF16),
        "g_ffn": g_ffn[0][None, :],
        "router_t": w_router[0].T[row_expert].astype(BF16),
        "s_gu": jnp.concatenate([ws_gate[0], ws_up[0]], axis=1).astype(BF16),
        "s_down": ws_down[0].astype(BF16),
    }
    b_perm = b_router[0][row_expert][:, None]

    grain = SC_WORKERS * SC_DISPATCH_CHUNK * SC_COMBINE_CHUNK // math.gcd(SC_DISPATCH_CHUNK, SC_COMBINE_CHUNK)
    t_pad = (t_all + grain - 1) // grain * grain
    assert t_pad % ROUTE_TILE == 0 and t_pad % DEST_TILE == 0

    c_all = jnp.concatenate([c_prompt, c_sample], axis=0)
    mod = _ada(c_all, w_ada[0], b_ada[0])
    modf = _ada(c_all, w_ada_final, b_ada_final)

    sp_t = jnp.transpose(state_pool[0], (1, 0, 2))
    sc_t = jnp.transpose(state_conv[0], (1, 0, 2))
    xmid_s, h2, lgt, u_s, a_s = _sample_mixer(x_sample[:, 0, :], mod[bp:], sp_t, sc_t, wl, tp, t_pad,
                                              conv_k=conv_k)
    xmid_p, h2, lgt, npool_p, nconv_p = _prompt_mixer(x_prompt, mod[:bp], wl, h2, lgt, conv_k=conv_k)

    idx, wts, rank, counts_perm = _route(lgt, b_perm, t_all)
    counts = jnp.zeros((ne,), I32).at[row_expert].set(counts_perm[:, 0].astype(I32))
    nblk = (counts + EXPERT_ROWS - 1) // EXPERT_ROWS
    first_block = jnp.concatenate([jnp.zeros((1,), I32), jnp.cumsum(nblk).astype(I32)])
    n_blocks = (t_all * TOP_K + EXPERT_ROWS - 1) // EXPERT_ROWS + ne
    n_rows = (n_blocks + 1) * EXPERT_ROWS
    dest = _dest_rows(first_block[:ne] * EXPERT_ROWS, idx, rank, t_all, n_rows - 1)

    def chunked(ch):
        return dest.reshape(TOP_K, t_pad // ch, ch).transpose(1, 0, 2)

    xs = _sc_dispatch(h2, chunked(SC_DISPATCH_CHUNK), n_rows)
    y = _experts(xs, w_gate[0], w_up[0], w_down[0], first_block, counts)
    yg = _sc_combine_gather(y, chunked(SC_COMBINE_CHUNK), t_pad)

    g2 = mod[:, 5 * d:6 * d]
    shf, scf = modf[:, :d], modf[:, d:]
    wts_t = wts.T
    y_prompt = _final_prompt(xmid_p, yg, wts_t, g2[:bp], shf[:bp], scf[:bp], g_final[None, :])
    y_sample = _final_sample(xmid_s, yg, wts_t, g2[bp:], shf[bp:], scf[bp:], g_final[None, :], tp)

    new_pool_s = jnp.concatenate([state_pool[0][:, 1:], u_s[:, None, :]], axis=1)
    new_conv_s = jnp.concatenate([state_conv[0][:, 1:], a_s[:, None, :]], axis=1)
    return (y_prompt, y_sample[:, None, :], npool_p[None], nconv_p[None], new_pool_s[None], new_conv_s[None])
```

```python
import functools
import math

import jax
import jax.numpy as jnp
from jax import lax
from jax.experimental import pallas as pl
from jax.experimental.pallas import tpu as pltpu
from jax.experimental.pallas import tpu_sc as plsc

POOL_WINDOWS = (2, 4, 8, 16)
N_EXPERT_GROUPS = 8
TOPK_GROUPS = 4
TOP_K = 8
ROUTED_SCALE = 2.5
EPS = 1e-6

LANES = 128
SUBLANES = 8
BF16_ROWS = 16
VMEM_LIMIT = 52 * 1024 * 1024

SEQ_TILE = 512
ROW_CHUNK = 16
ROUTE_TILE = 384
EXPERT_ROWS = 512
N_PARTS = 2
EXPERT_BUFFERS = 4
FINAL_TILE = 512
DEST_TILE = 2816

SC_CORES = 2
SC_SUBCORES = 16
SC_WORKERS = SC_CORES * SC_SUBCORES
SC_CORE_SHARE = (2, 1)
SC_DISPATCH_CHUNK = 88
SC_COMBINE_CHUNK = 24

F32 = jnp.float32
BF16 = jnp.bfloat16
I32 = jnp.int32
U32 = jnp.uint32
NEG_INF = float("-inf")
HI16 = 0xFFFF0000


def _sigmoid(x):
    return 1.0 / (1.0 + jnp.exp(-x))


def _silu(x):
    return x * _sigmoid(x)


def _rms_norm(x, g):
    return x * lax.rsqrt(jnp.mean(x * x, axis=-1, keepdims=True) + EPS) * g


def _dot(a, b):
    return jnp.dot(a, b, preferred_element_type=F32)


def _pack_bf16_pairs(x):
    w = x.shape[1] // 2
    bits = lax.bitcast_convert_type(x.astype(BF16).astype(F32), U32)
    return lax.bitcast_convert_type((bits[:, :w] >> 16) | (bits[:, w:] & U32(HI16)), I32)


def _unpack_bf16_pairs(p):
    bits = lax.bitcast_convert_type(p, U32)
    lo = lax.bitcast_convert_type(bits << 16, F32)
    hi = lax.bitcast_convert_type(bits & U32(HI16), F32)
    return jnp.concatenate([lo, hi], axis=1)


def _ada_kernel(c_ref, w_ref, b_ref, o_ref):
    s = _silu(c_ref[...])
    o_ref[...] = _dot(s.astype(BF16), w_ref[...].astype(BF16)) + b_ref[...]


def _ada(c, w, b):
    rows, d = c.shape
    n = w.shape[1]
    tn = 1024
    return pl.pallas_call(
        _ada_kernel,
        grid=(n // tn,),
        in_specs=[pl.BlockSpec((rows, d), lambda j: (0, 0)),
                  pl.BlockSpec((d, tn), lambda j: (0, j)),
                  pl.BlockSpec((1, tn), lambda j: (0, j))],
        out_specs=pl.BlockSpec((rows, tn), lambda j: (0, j)),
        out_shape=jax.ShapeDtypeStruct((rows, n), F32),
        compiler_params=pltpu.CompilerParams(dimension_semantics=("arbitrary",),
                                             vmem_limit_bytes=VMEM_LIMIT),
        name="ada",
    )(c, w, b.reshape(1, n))


def _mixer_tail(x, pool_d, conv_act, mod, w, xmid_ref, h2_ref, lgt_ref):
    sh2, sc2, g1, g2 = mod
    pw = pool_d.shape[1]
    pool_out = _dot(pool_d, w["pool"][...]) * w["pool_scale"][...]
    mix = _dot(pool_out.astype(BF16), w["out"][:pw, :]) + _dot(conv_act, w["out"][pw:, :])
    x1 = x + g1 * mix
    h2f = _rms_norm(x1, w["g_ffn"][...]) * (1.0 + sc2) + sh2
    h2_ref[...] = _pack_bf16_pairs(h2f)
    h2 = h2f.astype(BF16)
    lgt_ref[...] = lax.dot_general(w["router_t"][...], h2, (((1,), (1,)), ((), ())),
                                   preferred_element_type=F32)
    gu = _dot(h2, w["s_gu"][...])
    de = gu.shape[1] // 2
    hs = _silu(gu[:, :de]) * gu[:, de:]
    shared = _dot(hs.astype(BF16), w["s_down"][...])
    xmid_ref[...] = x1 + g2 * shared


_W_NAMES = ("g_mix", "in", "pool", "pool_scale", "dw", "b_dw", "ln_g", "ln_b", "out", "g_ffn",
            "router_t", "s_gu", "s_down")


def _layer_norm_silu(yc, g, b):
    mu = jnp.mean(yc, axis=-1, keepdims=True)
    yz = yc - mu
    var = jnp.mean(yz * yz, axis=-1, keepdims=True)
    return _silu(yz * lax.rsqrt(var + EPS) * g + b)


U_HALO = 32
A_HALO = 32


def _prompt_mixer_kernel(x_ref, mod_ref, *refs, tl, d, pw, cw, conv_k):
    nw = len(_W_NAMES)
    w = dict(zip(_W_NAMES, refs[:nw]))
    xmid_ref, h2_ref, lgt_ref, npool_ref, nconv_ref = refs[nw + 2:nw + 7]
    ubuf, s2buf, s4buf, s8buf, abuf, ashift, dbuf, cbuf = refs[nw + 7:]
    l = pl.program_id(1)
    pool_buf = max(POOL_WINDOWS) - 1
    uh, ah = U_HALO, A_HALO
    pg = pw // len(POOL_WINDOWS)
    nu, na = uh + tl, ah + tl

    @pl.when(l == 0)
    def _():
        ubuf[0:uh, :] = jnp.zeros((uh, pw), F32)
        abuf[0:ah, :] = jnp.zeros((ah, cw), F32)

    @pl.when(l > 0)
    def _():
        ubuf[0:uh, :] = ubuf[tl:tl + uh, :]
        abuf[0:ah, :] = abuf[tl:tl + ah, :]

    x = x_ref[0]
    sh1 = mod_ref[0, :, 0 * d:1 * d]
    sc1 = mod_ref[0, :, 1 * d:2 * d]
    g1 = mod_ref[0, :, 2 * d:3 * d]
    sh2 = mod_ref[0, :, 3 * d:4 * d]
    sc2 = mod_ref[0, :, 4 * d:5 * d]
    g2 = mod_ref[0, :, 5 * d:6 * d]

    h = _rms_norm(x, w["g_mix"][...]) * (1.0 + sc1) + sh1
    z = _dot(h.astype(BF16), w["in"][...])
    u = z[:, :pw]
    ubuf[uh:nu, :] = u
    abuf[ah:na, :] = z[:, pw:pw + cw] * _sigmoid(z[:, pw + cw:])

    s2buf[8:nu, :] = ubuf[8:nu, :] + ubuf[7:nu - 1, :]
    s4buf[16:nu, :] = s2buf[16:nu, pg:] + s2buf[14:nu - 2, pg:]
    s8buf[24:nu, :] = s4buf[24:nu, pg:] + s4buf[20:nu - 4, pg:]
    t = l * tl + lax.broadcasted_iota(I32, (tl, 1), 0)
    sums = (s2buf[uh:nu, 0:pg], s4buf[uh:nu, 0:pg], s8buf[uh:nu, 0:pg],
            s8buf[uh:nu, pg:2 * pg] + s8buf[uh - 8:nu - 8, pg:2 * pg])
    for g, win in enumerate(POOL_WINDOWS):
        cols = slice(g * pg, (g + 1) * pg)
        inv = 1.0 / jnp.minimum(win, t + 1).astype(F32)
        dbuf[:, cols] = (sums[g] * inv - u[:, cols]).astype(BF16)

    for r in range(1, SUBLANES):
        ashift[r - 1, 8:na, :] = abuf[8 - r:na - r, :]

    for c in range(tl // ROW_CHUNK):
        o = ah + c * ROW_CHUNK
        yc = abuf[o:o + ROW_CHUNK, :] * w["dw"][conv_k - 1:conv_k, :]
        for back in range(1, conv_k):
            q, r = divmod(back, SUBLANES)
            src = abuf if r == 0 else ashift.at[r - 1]
            yc = yc + src[o - q * SUBLANES:o - q * SUBLANES + ROW_CHUNK, :] * w["dw"][conv_k - 1 - back:conv_k - back, :]
        yc = yc + w["b_dw"][...]
        r0 = c * ROW_CHUNK
        cbuf[r0:r0 + ROW_CHUNK, :] = _layer_norm_silu(yc, w["ln_g"][...], w["ln_b"][...]).astype(BF16)

    @pl.when(l == pl.num_programs(1) - 1)
    def _():
        npool_ref[0] = ubuf[nu - pool_buf:nu, :]
        nconv_ref[0] = abuf[na - (conv_k - 1):na, :]

    _mixer_tail(x, dbuf[...], cbuf[...], (sh2, sc2, g1, g2), w, xmid_ref.at[0], h2_ref, lgt_ref)


def _sample_mixer_kernel(x_ref, mod_ref, sp_ref, sc_ref, *refs, d, pw, cw, conv_k):
    nw = len(_W_NAMES)
    w = dict(zip(_W_NAMES, refs[:nw]))
    xmid_ref, u_ref, a_ref = refs[nw:nw + 3]
    tails = refs[nw + 3:-2]
    h2_tmp, lgt_tmp = refs[-2:]
    pool_buf = max(POOL_WINDOWS) - 1
    pg = pw // len(POOL_WINDOWS)
    x = x_ref[...]
    sh1, sc1, g1, sh2, sc2, g2 = [mod_ref[:, i * d:(i + 1) * d] for i in range(6)]
    h = _rms_norm(x, w["g_mix"][...]) * (1.0 + sc1) + sh1
    z = _dot(h.astype(BF16), w["in"][...])
    u = z[:, :pw]
    a = z[:, pw:pw + cw] * _sigmoid(z[:, pw + cw:])
    u_ref[...] = u
    a_ref[...] = a
    ds = []
    for g, win in enumerate(POOL_WINDOWS):
        cols = slice(g * pg, (g + 1) * pg)
        acc = u[:, cols]
        for j in range(1, win):
            acc = acc + sp_ref[pool_buf - j, :, cols]
        cnt = float(min(win, 1 + pool_buf))
        ds.append(acc / cnt - u[:, cols])
    pool_d = jnp.concatenate(ds, axis=-1).astype(BF16)
    yc = a * w["dw"][conv_k - 1:conv_k, :]
    for k in range(conv_k - 1):
        yc = yc + sc_ref[k] * w["dw"][k:k + 1, :]
    yc = yc + w["b_dw"][...]
    conv_act = _layer_norm_silu(yc, w["ln_g"][...], w["ln_b"][...]).astype(BF16)
    _mixer_tail(x, pool_d, conv_act, (sh2, sc2, g1, g2), w, xmid_ref, h2_tmp, lgt_tmp)
    n_parts = len(tails) // 2
    share = x.shape[0] // n_parts
    for p in range(n_parts):
        h2_ref, lgt_ref = tails[2 * p], tails[2 * p + 1]
        h2_ref[...] = jnp.zeros_like(h2_ref)
        lgt_ref[...] = jnp.zeros_like(lgt_ref)
        h2_ref[0:share, :] = h2_tmp[p * share:(p + 1) * share, :]
        lgt_ref[:, 0:share] = lgt_tmp[:, p * share:(p + 1) * share]


def _full_spec(a):
    nd = a.ndim
    return pl.BlockSpec(a.shape, lambda *_: (0,) * nd)


def _prompt_mixer(x, mod, wl, h2_all, lgt_all, b0, b, *, conv_k):
    _, seq, d = x.shape
    pw = wl["pool"].shape[0]
    cw = wl["dw"].shape[1]
    ne = wl["router_t"].shape[0]
    tl = SEQ_TILE
    nl = seq // tl
    pg = pw // len(POOL_WINDOWS)
    assert POOL_WINDOWS == (2, 4, 8, 16) and conv_k - 1 <= A_HALO
    ws = [wl[n] for n in _W_NAMES]
    kern = functools.partial(_prompt_mixer_kernel, tl=tl, d=d, pw=pw, cw=cw, conv_k=conv_k)
    return pl.pallas_call(
        kern,
        grid=(b, nl),
        in_specs=[pl.BlockSpec((1, tl, d), lambda i, j: (i + b0, j, 0)),
                  pl.BlockSpec((1, 1, mod.shape[-1]), lambda i, j: (i + b0, 0, 0))]
                 + [_full_spec(a) for a in ws]
                 + [pl.BlockSpec(memory_space=pl.ANY), pl.BlockSpec(memory_space=pl.ANY)],
        out_specs=[pl.BlockSpec((1, tl, d), lambda i, j: (i, j, 0)),
                   pl.BlockSpec((tl, d // 2), lambda i, j: (i * nl + j, 0)),
                   pl.BlockSpec((ne, tl), lambda i, j: (0, i * nl + j)),
                   pl.BlockSpec((1, max(POOL_WINDOWS) - 1, pw), lambda i, j: (i, 0, 0)),
                   pl.BlockSpec((1, conv_k - 1, cw), lambda i, j: (i, 0, 0))],
        out_shape=[jax.ShapeDtypeStruct((b, seq, d), F32),
                   jax.ShapeDtypeStruct(h2_all.shape, I32),
                   jax.ShapeDtypeStruct(lgt_all.shape, F32),
                   jax.ShapeDtypeStruct((b, max(POOL_WINDOWS) - 1, pw), F32),
                   jax.ShapeDtypeStruct((b, conv_k - 1, cw), F32)],
        scratch_shapes=[pltpu.VMEM((U_HALO + tl, pw), F32), pltpu.VMEM((U_HALO + tl, pw), F32),
                        pltpu.VMEM((U_HALO + tl, pw - pg), F32), pltpu.VMEM((U_HALO + tl, pw - 2 * pg), F32),
                        pltpu.VMEM((A_HALO + tl, cw), F32), pltpu.VMEM((SUBLANES - 1, A_HALO + tl, cw), F32),
                        pltpu.VMEM((tl, pw), BF16), pltpu.VMEM((tl, cw), BF16)],
        compiler_params=pltpu.CompilerParams(dimension_semantics=("arbitrary", "arbitrary"),
                                             vmem_limit_bytes=VMEM_LIMIT),
        input_output_aliases={2 + len(ws): 1, 3 + len(ws): 2},
        name="prompt_mixer",
    )(x, mod[:, None, :], *ws, h2_all, lgt_all)


def _sample_mixer(x, mod, sp_t, sc_t, wl, t_prompt, t_pad, *, conv_k):
    rows, d = x.shape
    tail = t_pad - t_prompt
    assert t_prompt % tail == 0 and rows % N_PARTS == 0 and rows // N_PARTS <= tail
    pw = wl["pool"].shape[0]
    cw = wl["dw"].shape[1]
    ne = wl["router_t"].shape[0]
    ws = [wl[n] for n in _W_NAMES]
    kern = functools.partial(_sample_mixer_kernel, d=d, pw=pw, cw=cw, conv_k=conv_k)
    ins = [x, mod, sp_t, sc_t] + ws
    tail_specs = [pl.BlockSpec((tail, d // 2), lambda i: (t_prompt // tail, 0)),
                  pl.BlockSpec((ne, tail), lambda i: (0, t_prompt // tail))] * N_PARTS
    tail_shapes = [jax.ShapeDtypeStruct((t_pad, d // 2), I32), jax.ShapeDtypeStruct((ne, t_pad), F32)] * N_PARTS
    outs = pl.pallas_call(
        kern,
        grid=(1,),
        in_specs=[_full_spec(a) for a in ins],
        out_specs=[pl.BlockSpec((rows, d), lambda i: (0, 0)),
                   pl.BlockSpec((rows, pw), lambda i: (0, 0)),
                   pl.BlockSpec((rows, cw), lambda i: (0, 0))] + tail_specs,
        out_shape=[jax.ShapeDtypeStruct((rows, d), F32),
                   jax.ShapeDtypeStruct((rows, pw), F32),
                   jax.ShapeDtypeStruct((rows, cw), F32)] + tail_shapes,
        scratch_shapes=[pltpu.VMEM((rows, d // 2), I32), pltpu.VMEM((ne, rows), F32)],
        compiler_params=pltpu.CompilerParams(dimension_semantics=("arbitrary",),
                                             vmem_limit_bytes=VMEM_LIMIT),
        name="sample_mixer",
    )(*ins)
    return outs[0], outs[1], outs[2], [(outs[3 + 2 * p], outs[4 + 2 * p]) for p in range(N_PARTS)]


def _sublane_max(x):
    return jnp.max(x, axis=0, keepdims=True)


def _route_kernel(lg_ref, b_ref, tri_ref, idx_ref, wt_ref, rank_ref, cnt_ref, carry, *, t_valid):
    i = pl.program_id(0)
    ne, tt = lg_ref.shape
    ng = N_EXPERT_GROUPS
    per = ne // ng

    @pl.when(i == 0)
    def _():
        carry[...] = jnp.zeros_like(carry)

    s = _sigmoid(lg_ref[...])
    sel = s + b_ref[...]
    s3 = [s[p * ng:(p + 1) * ng, :] for p in range(per)]
    sel3 = [sel[p * ng:(p + 1) * ng, :] for p in range(per)]
    m1 = sel3[0]
    m2 = jnp.full_like(m1, NEG_INF)
    for p in range(1, per):
        m2 = jnp.maximum(m2, jnp.minimum(m1, sel3[p]))
        m1 = jnp.maximum(m1, sel3[p])
    gs = m1 + m2
    gi = lax.broadcasted_iota(jnp.int32, (ng, tt), 0)
    beaten = jnp.zeros((ng, tt), jnp.int32)
    for g in range(ng):
        row = gs[g:g + 1, :]
        beats = (row > gs) | ((row == gs) & (gi > g))
        beaten = beaten + beats.astype(jnp.int32)
    keep = beaten < TOPK_GROUPS
    cur = [jnp.where(keep, sel3[p], NEG_INF) for p in range(per)]
    eid = [(gi * per + p).astype(F32) for p in range(per)]
    idxs, wts, hits = [], [], []
    for _ in range(TOP_K):
        m = cur[0]
        for p in range(1, per):
            m = jnp.maximum(m, cur[p])
        m = _sublane_max(m)
        cand = jnp.where(cur[0] == m, eid[0], float(ne))
        for p in range(1, per):
            cand = jnp.minimum(cand, jnp.where(cur[p] == m, eid[p], float(ne)))
        e_sel = jnp.min(cand, axis=0, keepdims=True)
        hit = [eid[p] == e_sel for p in range(per)]
        wk = jnp.where(hit[0], s3[0], 0.0)
        for p in range(1, per):
            wk = wk + jnp.where(hit[p], s3[p], 0.0)
        wts.append(jnp.sum(wk, axis=0, keepdims=True))
        cur = [jnp.where(hit[p], NEG_INF, cur[p]) for p in range(per)]
        idxs.append(e_sel)
        hits.append(hit)
    wsum = wts[0]
    for k in range(1, TOP_K):
        wsum = wsum + wts[k]
    chosen = [functools.reduce(lambda a, b: a | b, [hits[k][p] for k in range(TOP_K)]) for p in range(per)]
    real = (i * tt + lax.broadcasted_iota(I32, (1, tt), 1)) < t_valid
    onehot = jnp.concatenate([(c & real).astype(F32) for c in chosen], axis=0)
    before = _dot(onehot.astype(BF16), tri_ref[...]) + carry[...]
    for k in range(TOP_K):
        rk = jnp.where(hits[k][0], before[0:ng, :], 0.0)
        for p in range(1, per):
            rk = rk + jnp.where(hits[k][p], before[p * ng:(p + 1) * ng, :], 0.0)
        rank_ref[k:k + 1, :] = jnp.sum(rk, axis=0, keepdims=True).astype(jnp.int32)
        idx_ref[k:k + 1, :] = idxs[k].astype(jnp.int32)
        wt_ref[k:k + 1, :] = wts[k] / wsum * ROUTED_SCALE
    carry[...] = carry[...] + jnp.sum(onehot, axis=1, keepdims=True)
    cnt_ref[...] = carry[...]


def _route(lgt, b_perm, t_valid):
    ne, t = lgt.shape
    tt = ROUTE_TILE
    tri = (jnp.arange(tt)[:, None] < jnp.arange(tt)[None, :]).astype(BF16)
    return pl.pallas_call(
        functools.partial(_route_kernel, t_valid=t_valid),
        grid=(t // tt,),
        in_specs=[pl.BlockSpec((ne, tt), lambda i: (0, i)),
                  pl.BlockSpec((ne, 1), lambda i: (0, 0)),
                  pl.BlockSpec((tt, tt), lambda i: (0, 0))],
        out_specs=[pl.BlockSpec((TOP_K, tt), lambda i: (0, i)),
                   pl.BlockSpec((TOP_K, tt), lambda i: (0, i)),
                   pl.BlockSpec((TOP_K, tt), lambda i: (0, i)),
                   pl.BlockSpec((ne, 1), lambda i: (0, 0))],
        out_shape=[jax.ShapeDtypeStruct((TOP_K, t), jnp.int32),
                   jax.ShapeDtypeStruct((TOP_K, t), F32),
                   jax.ShapeDtypeStruct((TOP_K, t), jnp.int32),
                   jax.ShapeDtypeStruct((ne, 1), F32)],
        scratch_shapes=[pltpu.VMEM((ne, 1), F32)],
        compiler_params=pltpu.CompilerParams(dimension_semantics=("arbitrary",),
                                             vmem_limit_bytes=VMEM_LIMIT),
        name="route",
    )(lgt, b_perm, tri)


def _dest_kernel(pstart_ref, idx_ref, rank_ref, dest_ref, *, t_valid, spare_row):
    idx = idx_ref[...]
    tile = idx.shape[1]

    def add_start(e, acc):
        return acc + jnp.where(idx == e, pstart_ref[e], 0)

    dest = lax.fori_loop(0, pstart_ref.shape[0], add_start, rank_ref[...])
    tok = pl.program_id(0) * tile + lax.broadcasted_iota(I32, idx.shape, 1)
    dest_ref[...] = jnp.where(tok < t_valid, dest, spare_row)


def _dest_rows(pstart, idx, rank, t_valid, spare_row):
    k_top, t = idx.shape
    tile = DEST_TILE
    spec = pl.BlockSpec((k_top, tile), lambda i, ps: (0, i))
    return pl.pallas_call(
        functools.partial(_dest_kernel, t_valid=t_valid, spare_row=spare_row),
        grid_spec=pltpu.PrefetchScalarGridSpec(num_scalar_prefetch=1, grid=(t // tile,),
                                               in_specs=[spec, spec], out_specs=spec),
        out_shape=jax.ShapeDtypeStruct((k_top, t), I32),
        compiler_params=pltpu.CompilerParams(dimension_semantics=("arbitrary",),
                                             vmem_limit_bytes=VMEM_LIMIT),
        name="dest_rows",
    )(pstart, idx, rank)


def _expert_kernel(first_ref, cnt_ref, xs_hbm, wg_ref, wu_ref, wd_ref, y_hbm,
                   xbuf, ybuf, wgu_bf, wd_bf, in_sem, out_sem):
    e = pl.program_id(0)
    ne = first_ref.shape[0] - 1
    nbuf, rows = xbuf.shape[:2]
    de = wg_ref.shape[2]
    b0, b1 = first_ref[e], first_ref[e + 1]
    n_total = first_ref[ne]

    def in_copy(g):
        return pltpu.make_async_copy(xs_hbm.at[pl.ds(g * rows, rows)], xbuf.at[g % nbuf], in_sem.at[g % nbuf])

    def out_copy(g):
        return pltpu.make_async_copy(ybuf.at[g % nbuf], y_hbm.at[pl.ds(g * rows, rows)], out_sem.at[g % nbuf])

    @pl.when(e == 0)
    def _():
        for g in range(nbuf - 1):
            @pl.when(g < n_total)
            def _():
                in_copy(g).start()

    wgu_bf[:, :de] = wg_ref[0].astype(BF16)
    wgu_bf[:, de:] = wu_ref[0].astype(BF16)
    wd_bf[...] = wd_ref[0].astype(BF16)

    def block(g, carry):
        slot = g % nbuf

        @pl.when(g + nbuf - 1 < n_total)
        def _():
            in_copy(g + nbuf - 1).start()

        in_copy(g).wait()

        @pl.when(g >= nbuf)
        def _():
            out_copy(g - nbuf).wait()

        valid = cnt_ref[e] - (g - b0) * rows
        rid = lax.broadcasted_iota(I32, (rows, 1), 0)
        xs = jnp.where(rid < valid, _unpack_bf16_pairs(xbuf[slot]), 0.0).astype(BF16)
        gu = _dot(xs, wgu_bf[...])
        hb = _silu(gu[:, :de]) * gu[:, de:]
        ybuf[slot] = _pack_bf16_pairs(_dot(hb.astype(BF16), wd_bf[...]))
        out_copy(g).start()
        return carry

    lax.fori_loop(b0, b1, block, 0)

    @pl.when(e == ne - 1)
    def _():
        for back in range(nbuf, 0, -1):
            @pl.when(n_total >= back)
            def _():
                out_copy(n_total - back).wait()


def _experts(xs, w_gate, w_up, w_down, first_block, counts):
    p, dw = xs.shape
    ne, d, de = w_gate.shape
    w_in_spec = pl.BlockSpec((1, d, de), lambda e, fb, cn: (e, 0, 0))
    return pl.pallas_call(
        _expert_kernel,
        grid_spec=pltpu.PrefetchScalarGridSpec(
            num_scalar_prefetch=2,
            grid=(ne,),
            in_specs=[pl.BlockSpec(memory_space=pl.ANY), w_in_spec, w_in_spec,
                      pl.BlockSpec((1, de, d), lambda e, fb, cn: (e, 0, 0))],
            out_specs=pl.BlockSpec(memory_space=pl.ANY),
            scratch_shapes=[pltpu.VMEM((EXPERT_BUFFERS, EXPERT_ROWS, dw), I32),
                            pltpu.VMEM((EXPERT_BUFFERS, EXPERT_ROWS, dw), I32),
                            pltpu.VMEM((d, 2 * de), BF16), pltpu.VMEM((de, d), BF16),
                            pltpu.SemaphoreType.DMA((EXPERT_BUFFERS,)), pltpu.SemaphoreType.DMA((EXPERT_BUFFERS,))]),
        out_shape=jax.ShapeDtypeStruct((p, dw), I32),
        compiler_params=pltpu.CompilerParams(dimension_semantics=("arbitrary",),
                                             vmem_limit_bytes=VMEM_LIMIT),
        name="experts",
    )(first_block, counts, xs, w_gate, w_up, w_down)


def _sc_share(n_chunks, even):
    per_pair = n_chunks // SC_SUBCORES
    assert per_pair * SC_SUBCORES == n_chunks
    n0 = per_pair * SC_CORE_SHARE[0] // sum(SC_CORE_SHARE)
    if even:
        n0 += n0 % 2
        assert (per_pair - n0) % 2 == 0
    n1 = per_pair - n0
    assert n0 >= n1 >= 2
    core = lax.axis_index("c")
    first = lax.axis_index("s") * per_pair + core * n0
    return n0, n1, core, first, jnp.where(core == 0, n0, n1)


def _sc_start(copies):
    for cp in copies:
        cp.start()


def _sc_wait(copies):
    for cp in copies:
        cp.wait()


def _sc_dispatch(h2, dest, n_rows):
    w = h2.shape[1]
    n_chunks, k_top, ch = dest.shape
    mesh = plsc.VectorSubcoreMesh(core_axis_name="c", subcore_axis_name="s")

    @functools.partial(
        pl.kernel, mesh=mesh, out_type=jax.ShapeDtypeStruct((n_rows, w), I32),
        scratch_types=[pltpu.VMEM((2, k_top, ch), I32), pltpu.VMEM((2, ch, w), I32),
                       pltpu.SemaphoreType.DMA((2,)), pltpu.SemaphoreType.DMA((2,))])
    def dispatch(h2_hbm, dest_hbm, xs_hbm, idx_v, rows_v, load_sem, store_sem):
        _, _, _, first, count = _sc_share(n_chunks, even=True)
        n_pairs = count // 2

        def loads(c, slot):
            return (pltpu.make_async_copy(h2_hbm.at[pl.ds(c * ch, ch)], rows_v.at[slot], load_sem.at[slot]),
                    pltpu.make_async_copy(dest_hbm.at[c], idx_v.at[slot], load_sem.at[slot]))

        def scatters(slot):
            return [pltpu.make_async_copy(rows_v.at[slot], xs_hbm.at[idx_v.at[slot, k]], store_sem.at[slot])
                    for k in range(k_top)]

        _sc_start(loads(first, 0))
        _sc_start(loads(first + 1, 1))

        @pl.loop(0, n_pairs)
        def _(j):
            c = first + 2 * j
            for slot in range(2):
                _sc_wait(loads(c + slot, slot))
                _sc_start(scatters(slot))

            @pl.when(j + 1 < n_pairs)
            def _():
                for slot in range(2):
                    _sc_wait(scatters(slot))
                    _sc_start(loads(c + 2 + slot, slot))

        for slot in range(2):
            _sc_wait(scatters(slot))

    return dispatch(h2, dest)


def _sc_combine_gather(y, dest, t):
    w = y.shape[1]
    n_chunks, k_top, ch = dest.shape
    n_max = _sc_share_max(n_chunks)
    mesh = plsc.VectorSubcoreMesh(core_axis_name="c", subcore_axis_name="s")

    @functools.partial(
        pl.kernel, mesh=mesh, out_type=jax.ShapeDtypeStruct((k_top, t, w), I32),
        scratch_types=[pltpu.VMEM((n_max, k_top, ch), I32), pltpu.VMEM((k_top, ch, w), I32),
                       pltpu.SemaphoreType.DMA((k_top,)), pltpu.SemaphoreType.DMA((k_top,))])
    def gather(y_hbm, dest_hbm, out_hbm, idx_v, rows_v, load_sem, store_sem):
        n0, n1, core, first, count = _sc_share(n_chunks, even=False)
        skip = core * (n0 - n1)
        pltpu.sync_copy(dest_hbm.at[pl.ds(first - skip, n0)], idx_v)

        def fetch(i, k):
            return pltpu.make_async_copy(y_hbm.at[idx_v.at[skip + i, k]], rows_v.at[k], load_sem.at[k])

        def put(i, k):
            return pltpu.make_async_copy(rows_v.at[k], out_hbm.at[k, pl.ds((first + i) * ch, ch)], store_sem.at[k])

        @pl.loop(0, count)
        def _(i):
            for k in range(k_top):
                @pl.when(i > 0)
                def _():
                    put(i - 1, k).wait()

                fetch(i, k).start()
            for k in range(k_top):
                fetch(i, k).wait()
                put(i, k).start()

        for k in range(k_top):
            put(count - 1, k).wait()

    return gather(y, dest)


def _sc_share_max(n_chunks):
    per_pair = n_chunks // SC_SUBCORES
    return per_pair * SC_CORE_SHARE[0] // sum(SC_CORE_SHARE)


def _final_kernel(xmid_ref, yg_ref, wt_ref, g2_ref, shf_ref, scf_ref, gf_ref, *rest):
    o_ref = rest[-1]
    wt = wt_ref[...]
    routed = wt[:, 0:1] * _unpack_bf16_pairs(yg_ref[0])
    for k in range(1, yg_ref.shape[0]):
        routed = routed + wt[:, k:k + 1] * _unpack_bf16_pairs(yg_ref[k])
    x2 = xmid_ref[...] + g2_ref[...] * routed
    o_ref[...] = _rms_norm(x2, gf_ref[...]) * (1.0 + scf_ref[...]) + shf_ref[...]


def _final_prompt(xmid, yg, wts_t, g2, shf, scf, g_final, b0, n_batch, out_prev):
    b, seq, d = xmid.shape
    k_top, _, w = yg.shape
    tl = FINAL_TILE
    nl = seq // tl
    vec = pl.BlockSpec((None, 1, d), lambda i, j: (i + b0, 0, 0))
    in_specs = [pl.BlockSpec((None, tl, d), lambda i, j: (i, j, 0)),
                pl.BlockSpec((k_top, tl, w), lambda i, j: (0, i * nl + j, 0)),
                pl.BlockSpec((tl, k_top), lambda i, j: (i * nl + j, 0)),
                vec, vec, vec, pl.BlockSpec((1, d), lambda i, j: (0, 0))]
    args = [xmid, yg, wts_t, g2[:, None, :], shf[:, None, :], scf[:, None, :], g_final]
    aliases = {}
    if out_prev is not None:
        in_specs.append(pl.BlockSpec(memory_space=pl.ANY))
        args.append(out_prev)
        aliases = {7: 0}
    return pl.pallas_call(
        _final_kernel,
        grid=(b, nl),
        in_specs=in_specs,
        out_specs=pl.BlockSpec((None, tl, d), lambda i, j: (i + b0, j, 0)),
        out_shape=jax.ShapeDtypeStruct((n_batch, seq, d), F32),
        compiler_params=pltpu.CompilerParams(dimension_semantics=("arbitrary", "arbitrary"),
                                             vmem_limit_bytes=VMEM_LIMIT),
        input_output_aliases=aliases,
        name="final_prompt",
    )(*args)


def _final_sample(xmid, yg, wts_t, g2, shf, scf, g_final, first_row, after):
    rows, d = xmid.shape
    k_top, _, w = yg.shape
    blk = first_row // rows
    vec = pl.BlockSpec((rows, d), lambda i: (0, 0))
    return pl.pallas_call(
        _final_kernel,
        grid=(1,),
        in_specs=[vec,
                  pl.BlockSpec((k_top, rows, w), lambda i: (0, blk, 0)),
                  pl.BlockSpec((rows, k_top), lambda i: (blk, 0)),
                  vec, vec, vec, pl.BlockSpec((1, d), lambda i: (0, 0)), pl.BlockSpec(memory_space=pl.ANY)],
        out_specs=vec,
        out_shape=jax.ShapeDtypeStruct(xmid.shape, F32),
        compiler_params=pltpu.CompilerParams(dimension_semantics=("arbitrary",),
                                             vmem_limit_bytes=VMEM_LIMIT),
        name="final_sample",
    )(xmid, yg, wts_t, g2, shf, scf, g_final, after)


def _block_diag_pairs(w_pool):
    g, c, _ = w_pool.shape
    eye = jnp.eye(g, dtype=w_pool.dtype)
    return (eye[:, None, :, None] * w_pool[:, :, None, :]).reshape(g * c, g * c)


def kernel(x_prompt, x_sample, state_pool, state_conv, c_prompt, c_sample, w_ada, b_ada, g_mix, w_in, w_pool, pool_scale, w_dw, b_dw, ln_g, ln_b, w_out, g_ffn, w_router, b_router, w_gate, w_up, w_down, ws_gate, ws_up, ws_down, w_ada_final, b_ada_final, g_final):
    bp, seq, d = x_prompt.shape
    bs = x_sample.shape[0]
    depth = w_ada.shape[0]
    assert depth == 1 and x_sample.shape[1] == 1
    conv_k = w_dw.shape[1]
    ne = w_router.shape[-1]
    per = ne // N_EXPERT_GROUPS
    tp = bp * seq
    t_all = tp + bs

    row_expert = (jnp.arange(ne) % N_EXPERT_GROUPS) * per + jnp.arange(ne) // N_EXPERT_GROUPS
    wl = {
        "g_mix": g_mix[0][None, :],
        "in": w_in[0].astype(BF16),
        "pool": _block_diag_pairs(w_pool[0]).astype(BF16),
        "pool_scale": pool_scale[0][None, :],
        "dw": w_dw[0],
        "b_dw": b_dw[0][None, :],
        "ln_g": ln_g[0][None, :],
        "ln_b": ln_b[0][None, :],
        "out": w_out[0].astype(BF16),
        "g_ffn": g_ffn[0][None, :],
        "router_t": w_router[0].T[row_expert].astype(BF16),
        "s_gu": jnp.concatenate([ws_gate[0], ws_up[0]], axis=1).astype(BF16),
        "s_down": ws_down[0].astype(BF16),
    }
    b_perm = b_router[0][row_expert][:, None]

    assert bp % N_PARTS == 0 and bs % N_PARTS == 0
    bpp, bsp = bp // N_PARTS, bs // N_PARTS
    tpp = bpp * seq
    t_part = tpp + bsp
    grain = SC_WORKERS * SC_DISPATCH_CHUNK * SC_COMBINE_CHUNK // math.gcd(SC_DISPATCH_CHUNK, SC_COMBINE_CHUNK)
    t_pad = (t_part + grain - 1) // grain * grain
    assert t_pad % ROUTE_TILE == 0 and t_pad % DEST_TILE == 0 and tpp % bsp == 0

    c_all = jnp.concatenate([c_prompt, c_sample], axis=0)
    mod = _ada(c_all, w_ada[0], b_ada[0])
    modf = _ada(c_all, w_ada_final, b_ada_final)
    g2 = mod[:, 5 * d:6 * d]
    shf, scf = modf[:, :d], modf[:, d:]

    sp_t = jnp.transpose(state_pool[0], (1, 0, 2))
    sc_t = jnp.transpose(state_conv[0], (1, 0, 2))
    xmid_s, u_s, a_s, tails = _sample_mixer(x_sample[:, 0, :], mod[bp:], sp_t, sc_t, wl, tpp, t_pad, conv_k=conv_k)

    n_blocks = (t_part * TOP_K + EXPERT_ROWS - 1) // EXPERT_ROWS + ne
    n_rows = (n_blocks + 1) * EXPERT_ROWS

    y_prompt, y_samples, npools, nconvs = None, [], [], []
    for p in range(N_PARTS):
        h2, lgt = tails[p]
        xmid_p, h2, lgt, npool_p, nconv_p = _prompt_mixer(x_prompt, mod[:bp], wl, h2, lgt, p * bpp, bpp,
                                                          conv_k=conv_k)
        idx, wts, rank, counts_perm = _route(lgt, b_perm, t_part)
        counts = jnp.zeros((ne,), I32).at[row_expert].set(counts_perm[:, 0].astype(I32))
        nblk = (counts + EXPERT_ROWS - 1) // EXPERT_ROWS
        first_block = jnp.concatenate([jnp.zeros((1,), I32), jnp.cumsum(nblk).astype(I32)])
        dest = _dest_rows(first_block[:ne] * EXPERT_ROWS, idx, rank, t_part, n_rows - 1)

        def chunked(ch):
            return dest.reshape(TOP_K, t_pad // ch, ch).transpose(1, 0, 2)

        xs = _sc_dispatch(h2, chunked(SC_DISPATCH_CHUNK), n_rows)
        y = _experts(xs, w_gate[0], w_up[0], w_down[0], first_block, counts)
        yg = _sc_combine_gather(y, chunked(SC_COMBINE_CHUNK), t_pad)

        wts_t = wts.T
        rs = slice(bp + p * bsp, bp + (p + 1) * bsp)
        y_samples.append(_final_sample(xmid_s[p * bsp:(p + 1) * bsp], yg, wts_t, g2[rs], shf[rs], scf[rs],
                                       g_final[None, :], tpp, g_final if y_prompt is None else y_prompt))
        y_prompt = _final_prompt(xmid_p, yg, wts_t, g2[:bp], shf[:bp], scf[:bp], g_final[None, :],
                                 p * bpp, bp, y_prompt)
        npools.append(npool_p)
        nconvs.append(nconv_p)
    y_sample = jnp.concatenate(y_samples, axis=0)
    npool_p = jnp.concatenate(npools, axis=0)
    nconv_p = jnp.concatenate(nconvs, axis=0)

    new_pool_s = jnp.concatenate([state_pool[0][:, 1:], u_s[:, None, :]], axis=1)
    new_conv_s = jnp.concatenate([state_conv[0][:, 1:], a_s[:, None, :]], axis=1)
    return (y_prompt, y_sample[:, None, :], npool_p[None], nconv_p[None], new_pool_s[None], new_conv_s[None])
```

```python
import functools
import math

import jax
import jax.numpy as jnp
from jax import lax
from jax.experimental import pallas as pl
from jax.experimental.pallas import tpu as pltpu
from jax.experimental.pallas import tpu_sc as plsc

POOL_WINDOWS = (2, 4, 8, 16)
N_EXPERT_GROUPS = 8
TOPK_GROUPS = 4
TOP_K = 8
ROUTED_SCALE = 2.5
EPS = 1e-6

LANES = 128
SUBLANES = 8
BF16_ROWS = 16
VMEM_LIMIT = 52 * 1024 * 1024

SEQ_TILE = 512
ROW_CHUNK = 16
ROUTE_TILE = 384
EXPERT_ROWS = 512
N_PARTS = 2
EXPERT_BUFFERS = 4
FINAL_TILE = 512
DEST_TILE = 2816

SC_CORES = 2
SC_SUBCORES = 16
SC_WORKERS = SC_CORES * SC_SUBCORES
SC_DISPATCH_CHUNK = 88
SC_COMBINE_CHUNK = 24

F32 = jnp.float32
BF16 = jnp.bfloat16
I32 = jnp.int32
U32 = jnp.uint32
NEG_INF = float("-inf")
HI16 = 0xFFFF0000


def _sigmoid(x):
    return 1.0 / (1.0 + jnp.exp(-x))


def _silu(x):
    return x * _sigmoid(x)


def _rms_norm(x, g):
    return x * lax.rsqrt(jnp.mean(x * x, axis=-1, keepdims=True) + EPS) * g


def _dot(a, b):
    return jnp.dot(a, b, preferred_element_type=F32)


def _pack_bf16_pairs(x):
    w = x.shape[1] // 2
    bits = lax.bitcast_convert_type(x.astype(BF16).astype(F32), U32)
    return lax.bitcast_convert_type((bits[:, :w] >> 16) | (bits[:, w:] & U32(HI16)), I32)


def _unpack_bf16_pairs(p):
    bits = lax.bitcast_convert_type(p, U32)
    lo = lax.bitcast_convert_type(bits << 16, F32)
    hi = lax.bitcast_convert_type(bits & U32(HI16), F32)
    return jnp.concatenate([lo, hi], axis=1)


def _ada_kernel(c_ref, w_ref, b_ref, o_ref):
    s = _silu(c_ref[...])
    o_ref[...] = _dot(s.astype(BF16), w_ref[...].astype(BF16)) + b_ref[...]


def _ada(c, w, b):
    rows, d = c.shape
    n = w.shape[1]
    tn = 1024
    return pl.pallas_call(
        _ada_kernel,
        grid=(n // tn,),
        in_specs=[pl.BlockSpec((rows, d), lambda j: (0, 0)),
                  pl.BlockSpec((d, tn), lambda j: (0, j)),
                  pl.BlockSpec((1, tn), lambda j: (0, j))],
        out_specs=pl.BlockSpec((rows, tn), lambda j: (0, j)),
        out_shape=jax.ShapeDtypeStruct((rows, n), F32),
        compiler_params=pltpu.CompilerParams(dimension_semantics=("arbitrary",),
                                             vmem_limit_bytes=VMEM_LIMIT),
        name="ada",
    )(c, w, b.reshape(1, n))


def _mixer_tail(x, pool_d, conv_act, mod, w, xmid_ref, h2_ref, lgt_ref):
    sh2, sc2, g1, g2 = mod
    pw = pool_d.shape[1]
    pool_out = _dot(pool_d, w["pool"][...]) * w["pool_scale"][...]
    mix = _dot(pool_out.astype(BF16), w["out"][:pw, :]) + _dot(conv_act, w["out"][pw:, :])
    x1 = x + g1 * mix
    h2f = _rms_norm(x1, w["g_ffn"][...]) * (1.0 + sc2) + sh2
    h2_ref[...] = _pack_bf16_pairs(h2f)
    h2 = h2f.astype(BF16)
    lgt_ref[...] = lax.dot_general(w["router_t"][...], h2, (((1,), (1,)), ((), ())),
                                   preferred_element_type=F32)
    gu = _dot(h2, w["s_gu"][...])
    de = gu.shape[1] // 2
    hs = _silu(gu[:, :de]) * gu[:, de:]
    shared = _dot(hs.astype(BF16), w["s_down"][...])
    xmid_ref[...] = x1 + g2 * shared


_W_NAMES = ("g_mix", "in", "pool", "pool_scale", "dw", "b_dw", "ln_g", "ln_b", "out", "g_ffn",
            "router_t", "s_gu", "s_down")


def _layer_norm_silu(yc, g, b):
    mu = jnp.mean(yc, axis=-1, keepdims=True)
    yz = yc - mu
    var = jnp.mean(yz * yz, axis=-1, keepdims=True)
    return _silu(yz * lax.rsqrt(var + EPS) * g + b)


U_HALO = 32
A_HALO = 32


def _prompt_mixer_kernel(x_ref, mod_ref, *refs, tl, d, pw, cw, conv_k):
    nw = len(_W_NAMES)
    w = dict(zip(_W_NAMES, refs[:nw]))
    xmid_ref, h2_ref, lgt_ref, npool_ref, nconv_ref = refs[nw + 2:nw + 7]
    ubuf, s2buf, s4buf, s8buf, abuf, ashift, dbuf, cbuf = refs[nw + 7:]
    l = pl.program_id(1)
    pool_buf = max(POOL_WINDOWS) - 1
    uh, ah = U_HALO, A_HALO
    pg = pw // len(POOL_WINDOWS)
    nu, na = uh + tl, ah + tl

    @pl.when(l == 0)
    def _():
        ubuf[0:uh, :] = jnp.zeros((uh, pw), F32)
        abuf[0:ah, :] = jnp.zeros((ah, cw), F32)

    @pl.when(l > 0)
    def _():
        ubuf[0:uh, :] = ubuf[tl:tl + uh, :]
        abuf[0:ah, :] = abuf[tl:tl + ah, :]

    x = x_ref[0]
    sh1 = mod_ref[0, :, 0 * d:1 * d]
    sc1 = mod_ref[0, :, 1 * d:2 * d]
    g1 = mod_ref[0, :, 2 * d:3 * d]
    sh2 = mod_ref[0, :, 3 * d:4 * d]
    sc2 = mod_ref[0, :, 4 * d:5 * d]
    g2 = mod_ref[0, :, 5 * d:6 * d]

    h = _rms_norm(x, w["g_mix"][...]) * (1.0 + sc1) + sh1
    z = _dot(h.astype(BF16), w["in"][...])
    u = z[:, :pw]
    ubuf[uh:nu, :] = u
    abuf[ah:na, :] = z[:, pw:pw + cw] * _sigmoid(z[:, pw + cw:])

    s2buf[8:nu, :] = ubuf[8:nu, :] + ubuf[7:nu - 1, :]
    s4buf[16:nu, :] = s2buf[16:nu, pg:] + s2buf[14:nu - 2, pg:]
    s8buf[24:nu, :] = s4buf[24:nu, pg:] + s4buf[20:nu - 4, pg:]
    t = l * tl + lax.broadcasted_iota(I32, (tl, 1), 0)
    sums = (s2buf[uh:nu, 0:pg], s4buf[uh:nu, 0:pg], s8buf[uh:nu, 0:pg],
            s8buf[uh:nu, pg:2 * pg] + s8buf[uh - 8:nu - 8, pg:2 * pg])
    for g, win in enumerate(POOL_WINDOWS):
        cols = slice(g * pg, (g + 1) * pg)
        inv = 1.0 / jnp.minimum(win, t + 1).astype(F32)
        dbuf[:, cols] = (sums[g] * inv - u[:, cols]).astype(BF16)

    for r in range(1, SUBLANES):
        ashift[r - 1, 8:na, :] = abuf[8 - r:na - r, :]

    for c in range(tl // ROW_CHUNK):
        o = ah + c * ROW_CHUNK
        yc = abuf[o:o + ROW_CHUNK, :] * w["dw"][conv_k - 1:conv_k, :]
        for back in range(1, conv_k):
            q, r = divmod(back, SUBLANES)
            src = abuf if r == 0 else ashift.at[r - 1]
            yc = yc + src[o - q * SUBLANES:o - q * SUBLANES + ROW_CHUNK, :] * w["dw"][conv_k - 1 - back:conv_k - back, :]
        yc = yc + w["b_dw"][...]
        r0 = c * ROW_CHUNK
        cbuf[r0:r0 + ROW_CHUNK, :] = _layer_norm_silu(yc, w["ln_g"][...], w["ln_b"][...]).astype(BF16)

    @pl.when(l == pl.num_programs(1) - 1)
    def _():
        npool_ref[0] = ubuf[nu - pool_buf:nu, :]
        nconv_ref[0] = abuf[na - (conv_k - 1):na, :]

    _mixer_tail(x, dbuf[...], cbuf[...], (sh2, sc2, g1, g2), w, xmid_ref.at[0], h2_ref, lgt_ref)


def _sample_mixer_kernel(x_ref, mod_ref, sp_ref, sc_ref, *refs, d, pw, cw, conv_k):
    nw = len(_W_NAMES)
    w = dict(zip(_W_NAMES, refs[:nw]))
    xmid_ref, u_ref, a_ref = refs[nw:nw + 3]
    tails = refs[nw + 3:-2]
    h2_tmp, lgt_tmp = refs[-2:]
    pool_buf = max(POOL_WINDOWS) - 1
    pg = pw // len(POOL_WINDOWS)
    x = x_ref[...]
    sh1, sc1, g1, sh2, sc2, g2 = [mod_ref[:, i * d:(i + 1) * d] for i in range(6)]
    h = _rms_norm(x, w["g_mix"][...]) * (1.0 + sc1) + sh1
    z = _dot(h.astype(BF16), w["in"][...])
    u = z[:, :pw]
    a = z[:, pw:pw + cw] * _sigmoid(z[:, pw + cw:])
    u_ref[...] = u
    a_ref[...] = a
    ds = []
    for g, win in enumerate(POOL_WINDOWS):
        cols = slice(g * pg, (g + 1) * pg)
        acc = u[:, cols]
        for j in range(1, win):
            acc = acc + sp_ref[pool_buf - j, :, cols]
        cnt = float(min(win, 1 + pool_buf))
        ds.append(acc / cnt - u[:, cols])
    pool_d = jnp.concatenate(ds, axis=-1).astype(BF16)
    yc = a * w["dw"][conv_k - 1:conv_k, :]
    for k in range(conv_k - 1):
        yc = yc + sc_ref[k] * w["dw"][k:k + 1, :]
    yc = yc + w["b_dw"][...]
    conv_act = _layer_norm_silu(yc, w["ln_g"][...], w["ln_b"][...]).astype(BF16)
    _mixer_tail(x, pool_d, conv_act, (sh2, sc2, g1, g2), w, xmid_ref, h2_tmp, lgt_tmp)
    n_parts = len(tails) // 2
    share = x.shape[0] // n_parts
    for p in range(n_parts):
        h2_ref, lgt_ref = tails[2 * p], tails[2 * p + 1]
        h2_ref[...] = jnp.zeros_like(h2_ref)
        lgt_ref[...] = jnp.zeros_like(lgt_ref)
        h2_ref[0:share, :] = h2_tmp[p * share:(p + 1) * share, :]
        lgt_ref[:, 0:share] = lgt_tmp[:, p * share:(p + 1) * share]


def _full_spec(a):
    nd = a.ndim
    return pl.BlockSpec(a.shape, lambda *_: (0,) * nd)


def _prompt_mixer(x, mod, wl, h2_all, lgt_all, b0, b, *, conv_k):
    _, seq, d = x.shape
    pw = wl["pool"].shape[0]
    cw = wl["dw"].shape[1]
    ne = wl["router_t"].shape[0]
    tl = SEQ_TILE
    nl = seq // tl
    pg = pw // len(POOL_WINDOWS)
    assert POOL_WINDOWS == (2, 4, 8, 16) and conv_k - 1 <= A_HALO
    ws = [wl[n] for n in _W_NAMES]
    kern = functools.partial(_prompt_mixer_kernel, tl=tl, d=d, pw=pw, cw=cw, conv_k=conv_k)
    return pl.pallas_call(
        kern,
        grid=(b, nl),
        in_specs=[pl.BlockSpec((1, tl, d), lambda i, j: (i + b0, j, 0)),
                  pl.BlockSpec((1, 1, mod.shape[-1]), lambda i, j: (i + b0, 0, 0))]
                 + [_full_spec(a) for a in ws]
                 + [pl.BlockSpec(memory_space=pl.ANY), pl.BlockSpec(memory_space=pl.ANY)],
        out_specs=[pl.BlockSpec((1, tl, d), lambda i, j: (i, j, 0)),
                   pl.BlockSpec((tl, d // 2), lambda i, j: (i * nl + j, 0)),
                   pl.BlockSpec((ne, tl), lambda i, j: (0, i * nl + j)),
                   pl.BlockSpec((1, max(POOL_WINDOWS) - 1, pw), lambda i, j: (i, 0, 0)),
                   pl.BlockSpec((1, conv_k - 1, cw), lambda i, j: (i, 0, 0))],
        out_shape=[jax.ShapeDtypeStruct((b, seq, d), F32),
                   jax.ShapeDtypeStruct(h2_all.shape, I32),
                   jax.ShapeDtypeStruct(lgt_all.shape, F32),
                   jax.ShapeDtypeStruct((b, max(POOL_WINDOWS) - 1, pw), F32),
                   jax.ShapeDtypeStruct((b, conv_k - 1, cw), F32)],
        scratch_shapes=[pltpu.VMEM((U_HALO + tl, pw), F32), pltpu.VMEM((U_HALO + tl, pw), F32),
                        pltpu.VMEM((U_HALO + tl, pw - pg), F32), pltpu.VMEM((U_HALO + tl, pw - 2 * pg), F32),
                        pltpu.VMEM((A_HALO + tl, cw), F32), pltpu.VMEM((SUBLANES - 1, A_HALO + tl, cw), F32),
                        pltpu.VMEM((tl, pw), BF16), pltpu.VMEM((tl, cw), BF16)],
        compiler_params=pltpu.CompilerParams(dimension_semantics=("arbitrary", "arbitrary"),
                                             vmem_limit_bytes=VMEM_LIMIT),
        input_output_aliases={2 + len(ws): 1, 3 + len(ws): 2},
        name="prompt_mixer",
    )(x, mod[:, None, :], *ws, h2_all, lgt_all)


def _sample_mixer(x, mod, sp_t, sc_t, wl, t_prompt, t_pad, *, conv_k):
    rows, d = x.shape
    tail = t_pad - t_prompt
    assert t_prompt % tail == 0 and rows % N_PARTS == 0 and rows // N_PARTS <= tail
    pw = wl["pool"].shape[0]
    cw = wl["dw"].shape[1]
    ne = wl["router_t"].shape[0]
    ws = [wl[n] for n in _W_NAMES]
    kern = functools.partial(_sample_mixer_kernel, d=d, pw=pw, cw=cw, conv_k=conv_k)
    ins = [x, mod, sp_t, sc_t] + ws
    tail_specs = [pl.BlockSpec((tail, d // 2), lambda i: (t_prompt // tail, 0)),
                  pl.BlockSpec((ne, tail), lambda i: (0, t_prompt // tail))] * N_PARTS
    tail_shapes = [jax.ShapeDtypeStruct((t_pad, d // 2), I32), jax.ShapeDtypeStruct((ne, t_pad), F32)] * N_PARTS
    outs = pl.pallas_call(
        kern,
        grid=(1,),
        in_specs=[_full_spec(a) for a in ins],
        out_specs=[pl.BlockSpec((rows, d), lambda i: (0, 0)),
                   pl.BlockSpec((rows, pw), lambda i: (0, 0)),
                   pl.BlockSpec((rows, cw), lambda i: (0, 0))] + tail_specs,
        out_shape=[jax.ShapeDtypeStruct((rows, d), F32),
                   jax.ShapeDtypeStruct((rows, pw), F32),
                   jax.ShapeDtypeStruct((rows, cw), F32)] + tail_shapes,
        scratch_shapes=[pltpu.VMEM((rows, d // 2), I32), pltpu.VMEM((ne, rows), F32)],
        compiler_params=pltpu.CompilerParams(dimension_semantics=("arbitrary",),
                                             vmem_limit_bytes=VMEM_LIMIT),
        name="sample_mixer",
    )(*ins)
    return outs[0], outs[1], outs[2], [(outs[3 + 2 * p], outs[4 + 2 * p]) for p in range(N_PARTS)]


def _sublane_max(x):
    return jnp.max(x, axis=0, keepdims=True)


def _route_kernel(lg_ref, b_ref, tri_ref, idx_ref, wt_ref, rank_ref, cnt_ref, carry, *, t_valid):
    i = pl.program_id(0)
    ne, tt = lg_ref.shape
    ng = N_EXPERT_GROUPS
    per = ne // ng

    @pl.when(i == 0)
    def _():
        carry[...] = jnp.zeros_like(carry)

    s = _sigmoid(lg_ref[...])
    sel = s + b_ref[...]
    s3 = [s[p * ng:(p + 1) * ng, :] for p in range(per)]
    sel3 = [sel[p * ng:(p + 1) * ng, :] for p in range(per)]
    m1 = sel3[0]
    m2 = jnp.full_like(m1, NEG_INF)
    for p in range(1, per):
        m2 = jnp.maximum(m2, jnp.minimum(m1, sel3[p]))
        m1 = jnp.maximum(m1, sel3[p])
    gs = m1 + m2
    gi = lax.broadcasted_iota(jnp.int32, (ng, tt), 0)
    beaten = jnp.zeros((ng, tt), jnp.int32)
    for g in range(ng):
        row = gs[g:g + 1, :]
        beats = (row > gs) | ((row == gs) & (gi > g))
        beaten = beaten + beats.astype(jnp.int32)
    keep = beaten < TOPK_GROUPS
    cur = [jnp.where(keep, sel3[p], NEG_INF) for p in range(per)]
    eid = [(gi * per + p).astype(F32) for p in range(per)]
    idxs, wts, hits = [], [], []
    for _ in range(TOP_K):
        m = cur[0]
        for p in range(1, per):
            m = jnp.maximum(m, cur[p])
        m = _sublane_max(m)
        cand = jnp.where(cur[0] == m, eid[0], float(ne))
        for p in range(1, per):
            cand = jnp.minimum(cand, jnp.where(cur[p] == m, eid[p], float(ne)))
        e_sel = jnp.min(cand, axis=0, keepdims=True)
        hit = [eid[p] == e_sel for p in range(per)]
        wk = jnp.where(hit[0], s3[0], 0.0)
        for p in range(1, per):
            wk = wk + jnp.where(hit[p], s3[p], 0.0)
        wts.append(jnp.sum(wk, axis=0, keepdims=True))
        cur = [jnp.where(hit[p], NEG_INF, cur[p]) for p in range(per)]
        idxs.append(e_sel)
        hits.append(hit)
    wsum = wts[0]
    for k in range(1, TOP_K):
        wsum = wsum + wts[k]
    chosen = [functools.reduce(lambda a, b: a | b, [hits[k][p] for k in range(TOP_K)]) for p in range(per)]
    real = (i * tt + lax.broadcasted_iota(I32, (1, tt), 1)) < t_valid
    onehot = jnp.concatenate([(c & real).astype(F32) for c in chosen], axis=0)
    before = _dot(onehot.astype(BF16), tri_ref[...]) + carry[...]
    for k in range(TOP_K):
        rk = jnp.where(hits[k][0], before[0:ng, :], 0.0)
        for p in range(1, per):
            rk = rk + jnp.where(hits[k][p], before[p * ng:(p + 1) * ng, :], 0.0)
        rank_ref[k:k + 1, :] = jnp.sum(rk, axis=0, keepdims=True).astype(jnp.int32)
        idx_ref[k:k + 1, :] = idxs[k].astype(jnp.int32)
        wt_ref[k:k + 1, :] = wts[k] / wsum * ROUTED_SCALE
    carry[...] = carry[...] + jnp.sum(onehot, axis=1, keepdims=True)
    cnt_ref[...] = carry[...]


def _route(lgt, b_perm, t_valid):
    ne, t = lgt.shape
    tt = ROUTE_TILE
    tri = (jnp.arange(tt)[:, None] < jnp.arange(tt)[None, :]).astype(BF16)
    return pl.pallas_call(
        functools.partial(_route_kernel, t_valid=t_valid),
        grid=(t // tt,),
        in_specs=[pl.BlockSpec((ne, tt), lambda i: (0, i)),
                  pl.BlockSpec((ne, 1), lambda i: (0, 0)),
                  pl.BlockSpec((tt, tt), lambda i: (0, 0))],
        out_specs=[pl.BlockSpec((TOP_K, tt), lambda i: (0, i)),
                   pl.BlockSpec((TOP_K, tt), lambda i: (0, i)),
                   pl.BlockSpec((TOP_K, tt), lambda i: (0, i)),
                   pl.BlockSpec((ne, 1), lambda i: (0, 0))],
        out_shape=[jax.ShapeDtypeStruct((TOP_K, t), jnp.int32),
                   jax.ShapeDtypeStruct((TOP_K, t), F32),
                   jax.ShapeDtypeStruct((TOP_K, t), jnp.int32),
                   jax.ShapeDtypeStruct((ne, 1), F32)],
        scratch_shapes=[pltpu.VMEM((ne, 1), F32)],
        compiler_params=pltpu.CompilerParams(dimension_semantics=("arbitrary",),
                                             vmem_limit_bytes=VMEM_LIMIT),
        name="route",
    )(lgt, b_perm, tri)


def _dest_kernel(pstart_ref, idx_ref, rank_ref, dest_ref, *, t_valid, spare_row):
    idx = idx_ref[...]
    k_top, tile = idx.shape

    def add_start(e, acc):
        return acc + jnp.where(idx == e, pstart_ref[e], 0)

    dest = lax.fori_loop(0, pstart_ref.shape[0], add_start, rank_ref[...])
    tok = pl.program_id(0) * tile + lax.broadcasted_iota(I32, idx.shape, 1)
    spare = spare_row + (tok - t_valid) * k_top + lax.broadcasted_iota(I32, idx.shape, 0)
    dest_ref[...] = jnp.where(tok < t_valid, dest, spare)


def _dest_rows(pstart, idx, rank, t_valid, spare_row):
    k_top, t = idx.shape
    tile = DEST_TILE
    spec = pl.BlockSpec((k_top, tile), lambda i, ps: (0, i))
    return pl.pallas_call(
        functools.partial(_dest_kernel, t_valid=t_valid, spare_row=spare_row),
        grid_spec=pltpu.PrefetchScalarGridSpec(num_scalar_prefetch=1, grid=(t // tile,),
                                               in_specs=[spec, spec], out_specs=spec),
        out_shape=jax.ShapeDtypeStruct((k_top, t), I32),
        compiler_params=pltpu.CompilerParams(dimension_semantics=("arbitrary",),
                                             vmem_limit_bytes=VMEM_LIMIT),
        name="dest_rows",
    )(pstart, idx, rank)


def _expert_kernel(first_ref, cnt_ref, xs_hbm, wg_ref, wu_ref, wd_ref, y_hbm,
                   xbuf, ybuf, wgu_bf, wd_bf, in_sem, out_sem):
    e = pl.program_id(0)
    ne = first_ref.shape[0] - 1
    nbuf, rows = xbuf.shape[:2]
    de = wg_ref.shape[2]
    b0, b1 = first_ref[e], first_ref[e + 1]
    n_total = first_ref[ne]

    def in_copy(g):
        return pltpu.make_async_copy(xs_hbm.at[pl.ds(g * rows, rows)], xbuf.at[g % nbuf], in_sem.at[g % nbuf])

    def out_copy(g):
        return pltpu.make_async_copy(ybuf.at[g % nbuf], y_hbm.at[pl.ds(g * rows, rows)], out_sem.at[g % nbuf])

    @pl.when(e == 0)
    def _():
        for g in range(nbuf - 1):
            @pl.when(g < n_total)
            def _():
                in_copy(g).start()

    wgu_bf[:, :de] = wg_ref[0].astype(BF16)
    wgu_bf[:, de:] = wu_ref[0].astype(BF16)
    wd_bf[...] = wd_ref[0].astype(BF16)

    def block(g, carry):
        slot = g % nbuf

        @pl.when(g + nbuf - 1 < n_total)
        def _():
            in_copy(g + nbuf - 1).start()

        in_copy(g).wait()

        @pl.when(g >= nbuf)
        def _():
            out_copy(g - nbuf).wait()

        valid = cnt_ref[e] - (g - b0) * rows
        rid = lax.broadcasted_iota(I32, (rows, 1), 0)
        xs = jnp.where(rid < valid, _unpack_bf16_pairs(xbuf[slot]), 0.0).astype(BF16)
        gu = _dot(xs, wgu_bf[...])
        hb = _silu(gu[:, :de]) * gu[:, de:]
        ybuf[slot] = _pack_bf16_pairs(_dot(hb.astype(BF16), wd_bf[...]))
        out_copy(g).start()
        return carry

    lax.fori_loop(b0, b1, block, 0)

    @pl.when(e == ne - 1)
    def _():
        for back in range(nbuf, 0, -1):
            @pl.when(n_total >= back)
            def _():
                out_copy(n_total - back).wait()


def _experts(xs, w_gate, w_up, w_down, first_block, counts):
    p, dw = xs.shape
    ne, d, de = w_gate.shape
    w_in_spec = pl.BlockSpec((1, d, de), lambda e, fb, cn: (e, 0, 0))
    return pl.pallas_call(
        _expert_kernel,
        grid_spec=pltpu.PrefetchScalarGridSpec(
            num_scalar_prefetch=2,
            grid=(ne,),
            in_specs=[pl.BlockSpec(memory_space=pl.ANY), w_in_spec, w_in_spec,
                      pl.BlockSpec((1, de, d), lambda e, fb, cn: (e, 0, 0))],
            out_specs=pl.BlockSpec(memory_space=pl.ANY),
            scratch_shapes=[pltpu.VMEM((EXPERT_BUFFERS, EXPERT_ROWS, dw), I32),
                            pltpu.VMEM((EXPERT_BUFFERS, EXPERT_ROWS, dw), I32),
                            pltpu.VMEM((d, 2 * de), BF16), pltpu.VMEM((de, d), BF16),
                            pltpu.SemaphoreType.DMA((EXPERT_BUFFERS,)), pltpu.SemaphoreType.DMA((EXPERT_BUFFERS,))]),
        out_shape=jax.ShapeDtypeStruct((p, dw), I32),
        compiler_params=pltpu.CompilerParams(dimension_semantics=("arbitrary",),
                                             vmem_limit_bytes=VMEM_LIMIT),
        name="experts",
    )(first_block, counts, xs, w_gate, w_up, w_down)


def _sc_first_chunk(n_chunks):
    per_worker = n_chunks // SC_WORKERS
    assert per_worker * SC_WORKERS == n_chunks
    return per_worker, (lax.axis_index("s") * SC_CORES + lax.axis_index("c")) * per_worker


def _sc_start(copies):
    for cp in copies:
        cp.start()


def _sc_wait(copies):
    for cp in copies:
        cp.wait()


def _sc_dispatch(h2, dest, n_rows):
    w = h2.shape[1]
    n_chunks, k_top, ch = dest.shape
    mesh = plsc.VectorSubcoreMesh(core_axis_name="c", subcore_axis_name="s")

    @functools.partial(
        pl.kernel, mesh=mesh, out_type=jax.ShapeDtypeStruct((n_rows, w), I32),
        scratch_types=[pltpu.VMEM((2, k_top, ch), I32), pltpu.VMEM((2, ch, w), I32),
                       pltpu.SemaphoreType.DMA((2,)), pltpu.SemaphoreType.DMA((2,))])
    def dispatch(h2_hbm, dest_hbm, xs_hbm, idx_v, rows_v, load_sem, store_sem):
        per_worker, first = _sc_first_chunk(n_chunks)

        def loads(i):
            slot = i % 2
            return (pltpu.make_async_copy(h2_hbm.at[pl.ds((first + i) * ch, ch)], rows_v.at[slot], load_sem.at[slot]),
                    pltpu.make_async_copy(dest_hbm.at[first + i], idx_v.at[slot], load_sem.at[slot]))

        def scatters(i):
            slot = i % 2
            return [pltpu.make_async_copy(rows_v.at[slot], xs_hbm.at[idx_v.at[slot, k]], store_sem.at[slot])
                    for k in range(k_top)]

        for i in range(min(2, per_worker)):
            _sc_start(loads(i))
        for i in range(per_worker):
            _sc_wait(loads(i))
            _sc_start(scatters(i))
            if 1 <= i < per_worker - 1:
                _sc_wait(scatters(i - 1))
                _sc_start(loads(i + 1))
        for i in range(max(per_worker - 2, 0), per_worker):
            _sc_wait(scatters(i))

    return dispatch(h2, dest)


def _sc_combine_gather(y, dest, t):
    w = y.shape[1]
    n_chunks, k_top, ch = dest.shape
    n_local = n_chunks // SC_WORKERS
    mesh = plsc.VectorSubcoreMesh(core_axis_name="c", subcore_axis_name="s")

    @functools.partial(
        pl.kernel, mesh=mesh, out_type=jax.ShapeDtypeStruct((k_top, t, w), I32),
        scratch_types=[pltpu.VMEM((n_local, k_top, ch), I32), pltpu.VMEM((k_top, ch, w), I32),
                       pltpu.SemaphoreType.DMA((k_top,)), pltpu.SemaphoreType.DMA((k_top,))])
    def gather(y_hbm, dest_hbm, out_hbm, idx_v, rows_v, load_sem, store_sem):
        per_worker, first = _sc_first_chunk(n_chunks)
        pltpu.sync_copy(dest_hbm.at[pl.ds(first, per_worker)], idx_v)

        def fetch(i, k):
            return pltpu.make_async_copy(y_hbm.at[idx_v.at[i, k]], rows_v.at[k], load_sem.at[k])

        def put(i, k):
            return pltpu.make_async_copy(rows_v.at[k], out_hbm.at[k, pl.ds((first + i) * ch, ch)], store_sem.at[k])

        @pl.loop(0, per_worker)
        def _(i):
            for k in range(k_top):
                @pl.when(i > 0)
                def _():
                    put(i - 1, k).wait()

                fetch(i, k).start()
            for k in range(k_top):
                fetch(i, k).wait()
                put(i, k).start()

        for k in range(k_top):
            put(per_worker - 1, k).wait()

    return gather(y, dest)


def _final_kernel(xmid_ref, yg_ref, wt_ref, g2_ref, shf_ref, scf_ref, gf_ref, *rest):
    o_ref = rest[-1]
    wt = wt_ref[...]
    routed = wt[:, 0:1] * _unpack_bf16_pairs(yg_ref[0])
    for k in range(1, yg_ref.shape[0]):
        routed = routed + wt[:, k:k + 1] * _unpack_bf16_pairs(yg_ref[k])
    x2 = xmid_ref[...] + g2_ref[...] * routed
    o_ref[...] = _rms_norm(x2, gf_ref[...]) * (1.0 + scf_ref[...]) + shf_ref[...]


def _final_prompt(xmid, yg, wts_t, g2, shf, scf, g_final, b0, n_batch, out_prev):
    b, seq, d = xmid.shape
    k_top, _, w = yg.shape
    tl = FINAL_TILE
    nl = seq // tl
    vec = pl.BlockSpec((None, 1, d), lambda i, j: (i + b0, 0, 0))
    in_specs = [pl.BlockSpec((None, tl, d), lambda i, j: (i, j, 0)),
                pl.BlockSpec((k_top, tl, w), lambda i, j: (0, i * nl + j, 0)),
                pl.BlockSpec((tl, k_top), lambda i, j: (i * nl + j, 0)),
                vec, vec, vec, pl.BlockSpec((1, d), lambda i, j: (0, 0))]
    args = [xmid, yg, wts_t, g2[:, None, :], shf[:, None, :], scf[:, None, :], g_final]
    aliases = {}
    if out_prev is not None:
        in_specs.append(pl.BlockSpec(memory_space=pl.ANY))
        args.append(out_prev)
        aliases = {7: 0}
    return pl.pallas_call(
        _final_kernel,
        grid=(b, nl),
        in_specs=in_specs,
        out_specs=pl.BlockSpec((None, tl, d), lambda i, j: (i + b0, j, 0)),
        out_shape=jax.ShapeDtypeStruct((n_batch, seq, d), F32),
        compiler_params=pltpu.CompilerParams(dimension_semantics=("arbitrary", "arbitrary"),
                                             vmem_limit_bytes=VMEM_LIMIT),
        input_output_aliases=aliases,
        name="final_prompt",
    )(*args)


def _final_sample(xmid, yg, wts_t, g2, shf, scf, g_final, first_row, after):
    rows, d = xmid.shape
    k_top, _, w = yg.shape
    blk = first_row // rows
    vec = pl.BlockSpec((rows, d), lambda i: (0, 0))
    return pl.pallas_call(
        _final_kernel,
        grid=(1,),
        in_specs=[vec,
                  pl.BlockSpec((k_top, rows, w), lambda i: (0, blk, 0)),
                  pl.BlockSpec((rows, k_top), lambda i: (blk, 0)),
                  vec, vec, vec, pl.BlockSpec((1, d), lambda i: (0, 0)), pl.BlockSpec(memory_space=pl.ANY)],
        out_specs=vec,
        out_shape=jax.ShapeDtypeStruct(xmid.shape, F32),
        compiler_params=pltpu.CompilerParams(dimension_semantics=("arbitrary",),
                                             vmem_limit_bytes=VMEM_LIMIT),
        name="final_sample",
    )(xmid, yg, wts_t, g2, shf, scf, g_final, after)


def _block_diag_pairs(w_pool):
    g, c, _ = w_pool.shape
    eye = jnp.eye(g, dtype=w_pool.dtype)
    return (eye[:, None, :, None] * w_pool[:, :, None, :]).reshape(g * c, g * c)


def kernel(x_prompt, x_sample, state_pool, state_conv, c_prompt, c_sample, w_ada, b_ada, g_mix, w_in, w_pool, pool_scale, w_dw, b_dw, ln_g, ln_b, w_out, g_ffn, w_router, b_router, w_gate, w_up, w_down, ws_gate, ws_up, ws_down, w_ada_final, b_ada_final, g_final):
    bp, seq, d = x_prompt.shape
    bs = x_sample.shape[0]
    depth = w_ada.shape[0]
    assert depth == 1 and x_sample.shape[1] == 1
    conv_k = w_dw.shape[1]
    ne = w_router.shape[-1]
    per = ne // N_EXPERT_GROUPS
    tp = bp * seq
    t_all = tp + bs

    row_expert = (jnp.arange(ne) % N_EXPERT_GROUPS) * per + jnp.arange(ne) // N_EXPERT_GROUPS
    wl = {
        "g_mix": g_mix[0][None, :],
        "in": w_in[0].astype(BF16),
        "pool": _block_diag_pairs(w_pool[0]).astype(BF16),
        "pool_scale": pool_scale[0][None, :],
        "dw": w_dw[0],
        "b_dw": b_dw[0][None, :],
        "ln_g": ln_g[0][None, :],
        "ln_b": ln_b[0][None, :],
        "out": w_out[0].astype(BF16),
        "g_ffn": g_ffn[0][None, :],
        "router_t": w_router[0].T[row_expert].astype(BF16),
        "s_gu": jnp.concatenate([ws_gate[0], ws_up[0]], axis=1).astype(BF16),
        "s_down": ws_down[0].astype(BF16),
    }
    b_perm = b_router[0][row_expert][:, None]

    assert bp % N_PARTS == 0 and bs % N_PARTS == 0
    bpp, bsp = bp // N_PARTS, bs // N_PARTS
    tpp = bpp * seq
    t_part = tpp + bsp
    grain = SC_WORKERS * SC_DISPATCH_CHUNK * SC_COMBINE_CHUNK // math.gcd(SC_DISPATCH_CHUNK, SC_COMBINE_CHUNK)
    t_pad = (t_part + grain - 1) // grain * grain
    assert t_pad % ROUTE_TILE == 0 and t_pad % DEST_TILE == 0 and tpp % bsp == 0

    c_all = jnp.concatenate([c_prompt, c_sample], axis=0)
    mod = _ada(c_all, w_ada[0], b_ada[0])
    modf = _ada(c_all, w_ada_final, b_ada_final)
    g2 = mod[:, 5 * d:6 * d]
    shf, scf = modf[:, :d], modf[:, d:]

    sp_t = jnp.transpose(state_pool[0], (1, 0, 2))
    sc_t = jnp.transpose(state_conv[0], (1, 0, 2))
    xmid_s, u_s, a_s, tails = _sample_mixer(x_sample[:, 0, :], mod[bp:], sp_t, sc_t, wl, tpp, t_pad, conv_k=conv_k)

    n_blocks = (t_part * TOP_K + EXPERT_ROWS - 1) // EXPERT_ROWS + ne
    n_spare = -(-(t_pad - t_part) * TOP_K // EXPERT_ROWS)
    n_rows = (n_blocks + n_spare) * EXPERT_ROWS

    y_prompt, y_samples, npools, nconvs = None, [], [], []
    for p in range(N_PARTS):
        h2, lgt = tails[p]
        xmid_p, h2, lgt, npool_p, nconv_p = _prompt_mixer(x_prompt, mod[:bp], wl, h2, lgt, p * bpp, bpp,
                                                          conv_k=conv_k)
        idx, wts, rank, counts_perm = _route(lgt, b_perm, t_part)
        counts = jnp.zeros((ne,), I32).at[row_expert].set(counts_perm[:, 0].astype(I32))
        nblk = (counts + EXPERT_ROWS - 1) // EXPERT_ROWS
        first_block = jnp.concatenate([jnp.zeros((1,), I32), jnp.cumsum(nblk).astype(I32)])
        dest = _dest_rows(first_block[:ne] * EXPERT_ROWS, idx, rank, t_part, n_blocks * EXPERT_ROWS)

        def chunked(ch):
            return dest.reshape(TOP_K, t_pad // ch, ch).transpose(1, 0, 2)

        xs = _sc_dispatch(h2, chunked(SC_DISPATCH_CHUNK), n_rows)
        y = _experts(xs, w_gate[0], w_up[0], w_down[0], first_block, counts)
        yg = _sc_combine_gather(y, chunked(SC_COMBINE_CHUNK), t_pad)

        wts_t = wts.T
        rs = slice(bp + p * bsp, bp + (p + 1) * bsp)
        y_samples.append(_final_sample(xmid_s[p * bsp:(p + 1) * bsp], yg, wts_t, g2[rs], shf[rs], scf[rs],
                                       g_final[None, :], tpp, g_final if y_prompt is None else y_prompt))
        y_prompt = _final_prompt(xmid_p, yg, wts_t, g2[:bp], shf[:bp], scf[:bp], g_final[None, :],
                                 p * bpp, bp, y_prompt)
        npools.append(npool_p)
        nconvs.append(nconv_p)
    y_sample = jnp.concatenate(y_samples, axis=0)
    npool_p = jnp.concatenate(npools, axis=0)
    nconv_p = jnp.concatenate(nconvs, axis=0)

    new_pool_s = jnp.concatenate([state_pool[0][:, 1:], u_s[:, None, :]], axis=1)
    new_conv_s = jnp.concatenate([state_conv[0][:, 1:], a_s[:, None, :]], axis=1)
    return (y_prompt, y_sample[:, None, :], npool_p[None], nconv_p[None], new_pool_s[None], new_conv_s[None])
```

```python
import functools
import math

import jax
import jax.numpy as jnp
from jax import lax
from jax.experimental import pallas as pl
from jax.experimental.pallas import tpu as pltpu
from jax.experimental.pallas import tpu_sc as plsc

POOL_WINDOWS = (2, 4, 8, 16)
N_EXPERT_GROUPS = 8
TOPK_GROUPS = 4
TOP_K = 8
ROUTED_SCALE = 2.5
EPS = 1e-6

LANES = 128
SUBLANES = 8
BF16_ROWS = 16
VMEM_LIMIT = 52 * 1024 * 1024

SEQ_TILE = 512
ROW_CHUNK = 16
ROUTE_TILE = 384
EXPERT_ROWS = 512
N_PARTS = 2
EXPERT_BUFFERS = 4
FINAL_TILE = 512
DEST_TILE = 2816

SC_CORES = 2
SC_SUBCORES = 16
SC_WORKERS = SC_CORES * SC_SUBCORES
SC_DISPATCH_CHUNK = 88
SC_COMBINE_CHUNK = 24

F32 = jnp.float32
BF16 = jnp.bfloat16
I32 = jnp.int32
U32 = jnp.uint32
NEG_INF = float("-inf")
HI16 = 0xFFFF0000


def _sigmoid(x):
    return 1.0 / (1.0 + jnp.exp(-x))


def _silu(x):
    return x * _sigmoid(x)


def _rms_norm(x, g):
    return x * lax.rsqrt(jnp.mean(x * x, axis=-1, keepdims=True) + EPS) * g


def _dot(a, b):
    return jnp.dot(a, b, preferred_element_type=F32)


def _pack_bf16_pairs(x):
    w = x.shape[1] // 2
    bits = lax.bitcast_convert_type(x.astype(BF16).astype(F32), U32)
    return lax.bitcast_convert_type((bits[:, :w] >> 16) | (bits[:, w:] & U32(HI16)), I32)


def _unpack_bf16_pairs(p):
    bits = lax.bitcast_convert_type(p, U32)
    lo = lax.bitcast_convert_type(bits << 16, F32)
    hi = lax.bitcast_convert_type(bits & U32(HI16), F32)
    return jnp.concatenate([lo, hi], axis=1)


def _ada_kernel(c_ref, w_ref, b_ref, o_ref):
    s = _silu(c_ref[...])
    o_ref[...] = _dot(s.astype(BF16), w_ref[...].astype(BF16)) + b_ref[...]


def _ada(c, w, b):
    rows, d = c.shape
    n = w.shape[1]
    tn = 1024
    return pl.pallas_call(
        _ada_kernel,
        grid=(n // tn,),
        in_specs=[pl.BlockSpec((rows, d), lambda j: (0, 0)),
                  pl.BlockSpec((d, tn), lambda j: (0, j)),
                  pl.BlockSpec((1, tn), lambda j: (0, j))],
        out_specs=pl.BlockSpec((rows, tn), lambda j: (0, j)),
        out_shape=jax.ShapeDtypeStruct((rows, n), F32),
        compiler_params=pltpu.CompilerParams(dimension_semantics=("arbitrary",),
                                             vmem_limit_bytes=VMEM_LIMIT),
        name="ada",
    )(c, w, b.reshape(1, n))


def _mixer_tail(x, pool_d, conv_act, mod, w, xmid_ref, h2_ref, lgt_ref):
    sh2, sc2, g1, g2 = mod
    pw = pool_d.shape[1]
    pool_out = _dot(pool_d, w["pool"][...]) * w["pool_scale"][...]
    mix = _dot(pool_out.astype(BF16), w["out"][:pw, :]) + _dot(conv_act, w["out"][pw:, :])
    x1 = x + g1 * mix
    h2f = _rms_norm(x1, w["g_ffn"][...]) * (1.0 + sc2) + sh2
    h2_ref[...] = _pack_bf16_pairs(h2f)
    h2 = h2f.astype(BF16)
    lgt_ref[...] = lax.dot_general(w["router_t"][...], h2, (((1,), (1,)), ((), ())),
                                   preferred_element_type=F32)
    gu = _dot(h2, w["s_gu"][...])
    de = gu.shape[1] // 2
    hs = _silu(gu[:, :de]) * gu[:, de:]
    shared = _dot(hs.astype(BF16), w["s_down"][...])
    xmid_ref[...] = x1 + g2 * shared


_W_NAMES = ("g_mix", "in", "pool", "pool_scale", "dw", "b_dw", "ln_g", "ln_b", "out", "g_ffn",
            "router_t", "s_gu", "s_down")


def _layer_norm_silu(yc, g, b):
    mu = jnp.mean(yc, axis=-1, keepdims=True)
    yz = yc - mu
    var = jnp.mean(yz * yz, axis=-1, keepdims=True)
    return _silu(yz * lax.rsqrt(var + EPS) * g + b)


U_HALO = 32
A_HALO = 32


def _prompt_mixer_kernel(x_ref, mod_ref, *refs, tl, d, pw, cw, conv_k):
    nw = len(_W_NAMES)
    w = dict(zip(_W_NAMES, refs[:nw]))
    wg_ref, wu_ref, wd_ref = refs[nw + 2:nw + 5]
    xmid_ref, h2_ref, lgt_ref, npool_ref, nconv_ref, wgu_bf_ref, wd_bf_ref = refs[-15:-8]
    ubuf, s2buf, s4buf, s8buf, abuf, ashift, dbuf, cbuf = refs[-8:]
    de = wg_ref.shape[2]
    wgu_bf_ref[:, :, :de] = wg_ref[...].astype(BF16)
    wgu_bf_ref[:, :, de:] = wu_ref[...].astype(BF16)
    wd_bf_ref[...] = wd_ref[...].astype(BF16)
    l = pl.program_id(1)
    pool_buf = max(POOL_WINDOWS) - 1
    uh, ah = U_HALO, A_HALO
    pg = pw // len(POOL_WINDOWS)
    nu, na = uh + tl, ah + tl

    @pl.when(l == 0)
    def _():
        ubuf[0:uh, :] = jnp.zeros((uh, pw), F32)
        abuf[0:ah, :] = jnp.zeros((ah, cw), F32)

    @pl.when(l > 0)
    def _():
        ubuf[0:uh, :] = ubuf[tl:tl + uh, :]
        abuf[0:ah, :] = abuf[tl:tl + ah, :]

    x = x_ref[0]
    sh1 = mod_ref[0, :, 0 * d:1 * d]
    sc1 = mod_ref[0, :, 1 * d:2 * d]
    g1 = mod_ref[0, :, 2 * d:3 * d]
    sh2 = mod_ref[0, :, 3 * d:4 * d]
    sc2 = mod_ref[0, :, 4 * d:5 * d]
    g2 = mod_ref[0, :, 5 * d:6 * d]

    h = _rms_norm(x, w["g_mix"][...]) * (1.0 + sc1) + sh1
    z = _dot(h.astype(BF16), w["in"][...])
    u = z[:, :pw]
    ubuf[uh:nu, :] = u
    abuf[ah:na, :] = z[:, pw:pw + cw] * _sigmoid(z[:, pw + cw:])

    s2buf[8:nu, :] = ubuf[8:nu, :] + ubuf[7:nu - 1, :]
    s4buf[16:nu, :] = s2buf[16:nu, pg:] + s2buf[14:nu - 2, pg:]
    s8buf[24:nu, :] = s4buf[24:nu, pg:] + s4buf[20:nu - 4, pg:]
    t = l * tl + lax.broadcasted_iota(I32, (tl, 1), 0)
    sums = (s2buf[uh:nu, 0:pg], s4buf[uh:nu, 0:pg], s8buf[uh:nu, 0:pg],
            s8buf[uh:nu, pg:2 * pg] + s8buf[uh - 8:nu - 8, pg:2 * pg])
    for g, win in enumerate(POOL_WINDOWS):
        cols = slice(g * pg, (g + 1) * pg)
        inv = 1.0 / jnp.minimum(win, t + 1).astype(F32)
        dbuf[:, cols] = (sums[g] * inv - u[:, cols]).astype(BF16)

    for r in range(1, SUBLANES):
        ashift[r - 1, 8:na, :] = abuf[8 - r:na - r, :]

    for c in range(tl // ROW_CHUNK):
        o = ah + c * ROW_CHUNK
        yc = abuf[o:o + ROW_CHUNK, :] * w["dw"][conv_k - 1:conv_k, :]
        for back in range(1, conv_k):
            q, r = divmod(back, SUBLANES)
            src = abuf if r == 0 else ashift.at[r - 1]
            yc = yc + src[o - q * SUBLANES:o - q * SUBLANES + ROW_CHUNK, :] * w["dw"][conv_k - 1 - back:conv_k - back, :]
        yc = yc + w["b_dw"][...]
        r0 = c * ROW_CHUNK
        cbuf[r0:r0 + ROW_CHUNK, :] = _layer_norm_silu(yc, w["ln_g"][...], w["ln_b"][...]).astype(BF16)

    @pl.when(l == pl.num_programs(1) - 1)
    def _():
        npool_ref[0] = ubuf[nu - pool_buf:nu, :]
        nconv_ref[0] = abuf[na - (conv_k - 1):na, :]

    _mixer_tail(x, dbuf[...], cbuf[...], (sh2, sc2, g1, g2), w, xmid_ref.at[0], h2_ref, lgt_ref)


def _sample_mixer_kernel(x_ref, mod_ref, sp_ref, sc_ref, *refs, d, pw, cw, conv_k):
    nw = len(_W_NAMES)
    w = dict(zip(_W_NAMES, refs[:nw]))
    xmid_ref, u_ref, a_ref = refs[nw:nw + 3]
    tails = refs[nw + 3:-2]
    h2_tmp, lgt_tmp = refs[-2:]
    pool_buf = max(POOL_WINDOWS) - 1
    pg = pw // len(POOL_WINDOWS)
    x = x_ref[...]
    sh1, sc1, g1, sh2, sc2, g2 = [mod_ref[:, i * d:(i + 1) * d] for i in range(6)]
    h = _rms_norm(x, w["g_mix"][...]) * (1.0 + sc1) + sh1
    z = _dot(h.astype(BF16), w["in"][...])
    u = z[:, :pw]
    a = z[:, pw:pw + cw] * _sigmoid(z[:, pw + cw:])
    u_ref[...] = u
    a_ref[...] = a
    ds = []
    for g, win in enumerate(POOL_WINDOWS):
        cols = slice(g * pg, (g + 1) * pg)
        acc = u[:, cols]
        for j in range(1, win):
            acc = acc + sp_ref[pool_buf - j, :, cols]
        cnt = float(min(win, 1 + pool_buf))
        ds.append(acc / cnt - u[:, cols])
    pool_d = jnp.concatenate(ds, axis=-1).astype(BF16)
    yc = a * w["dw"][conv_k - 1:conv_k, :]
    for k in range(conv_k - 1):
        yc = yc + sc_ref[k] * w["dw"][k:k + 1, :]
    yc = yc + w["b_dw"][...]
    conv_act = _layer_norm_silu(yc, w["ln_g"][...], w["ln_b"][...]).astype(BF16)
    _mixer_tail(x, pool_d, conv_act, (sh2, sc2, g1, g2), w, xmid_ref, h2_tmp, lgt_tmp)
    n_parts = len(tails) // 2
    share = x.shape[0] // n_parts
    for p in range(n_parts):
        h2_ref, lgt_ref = tails[2 * p], tails[2 * p + 1]
        h2_ref[...] = jnp.zeros_like(h2_ref)
        lgt_ref[...] = jnp.zeros_like(lgt_ref)
        h2_ref[0:share, :] = h2_tmp[p * share:(p + 1) * share, :]
        lgt_ref[:, 0:share] = lgt_tmp[:, p * share:(p + 1) * share]


def _full_spec(a):
    nd = a.ndim
    return pl.BlockSpec(a.shape, lambda *_: (0,) * nd)


def _prompt_mixer(x, mod, wl, h2_all, lgt_all, b0, b, w_experts, w_bf_prev, *, conv_k):
    n_batch, seq, d = x.shape
    w_gate, w_up, w_down = w_experts
    n_exp, _, de = w_gate.shape
    pw = wl["pool"].shape[0]
    cw = wl["dw"].shape[1]
    ne = wl["router_t"].shape[0]
    tl = SEQ_TILE
    nl = seq // tl
    pg = pw // len(POOL_WINDOWS)
    assert POOL_WINDOWS == (2, 4, 8, 16) and conv_k - 1 <= A_HALO
    ws = [wl[n] for n in _W_NAMES]
    kern = functools.partial(_prompt_mixer_kernel, tl=tl, d=d, pw=pw, cw=cw, conv_k=conv_k)
    ecs = n_exp // (n_batch * nl)
    assert ecs * n_batch * nl == n_exp

    def w_map(i, j):
        return ((i + b0) * nl + j, 0, 0)

    ins = [x, mod[:, None, :], *ws, h2_all, lgt_all, w_gate, w_up, w_down]
    in_specs = ([pl.BlockSpec((1, tl, d), lambda i, j: (i + b0, j, 0)),
                 pl.BlockSpec((1, 1, mod.shape[-1]), lambda i, j: (i + b0, 0, 0))]
                + [_full_spec(a) for a in ws]
                + [pl.BlockSpec(memory_space=pl.ANY), pl.BlockSpec(memory_space=pl.ANY),
                   pl.BlockSpec((ecs, d, de), w_map), pl.BlockSpec((ecs, d, de), w_map),
                   pl.BlockSpec((ecs, de, d), w_map)])
    aliases = {2 + len(ws): 1, 3 + len(ws): 2}
    if w_bf_prev is not None:
        aliases.update({len(ins): 5, len(ins) + 1: 6})
        ins += list(w_bf_prev)
        in_specs += [pl.BlockSpec(memory_space=pl.ANY), pl.BlockSpec(memory_space=pl.ANY)]
    return pl.pallas_call(
        kern,
        grid=(b, nl),
        in_specs=in_specs,
        out_specs=[pl.BlockSpec((1, tl, d), lambda i, j: (i, j, 0)),
                   pl.BlockSpec((tl, d // 2), lambda i, j: (i * nl + j, 0)),
                   pl.BlockSpec((ne, tl), lambda i, j: (0, i * nl + j)),
                   pl.BlockSpec((1, max(POOL_WINDOWS) - 1, pw), lambda i, j: (i, 0, 0)),
                   pl.BlockSpec((1, conv_k - 1, cw), lambda i, j: (i, 0, 0)),
                   pl.BlockSpec((ecs, d, 2 * de), w_map), pl.BlockSpec((ecs, de, d), w_map)],
        out_shape=[jax.ShapeDtypeStruct((b, seq, d), F32),
                   jax.ShapeDtypeStruct(h2_all.shape, I32),
                   jax.ShapeDtypeStruct(lgt_all.shape, F32),
                   jax.ShapeDtypeStruct((b, max(POOL_WINDOWS) - 1, pw), F32),
                   jax.ShapeDtypeStruct((b, conv_k - 1, cw), F32),
                   jax.ShapeDtypeStruct((n_exp, d, 2 * de), BF16),
                   jax.ShapeDtypeStruct((n_exp, de, d), BF16)],
        scratch_shapes=[pltpu.VMEM((U_HALO + tl, pw), F32), pltpu.VMEM((U_HALO + tl, pw), F32),
                        pltpu.VMEM((U_HALO + tl, pw - pg), F32), pltpu.VMEM((U_HALO + tl, pw - 2 * pg), F32),
                        pltpu.VMEM((A_HALO + tl, cw), F32), pltpu.VMEM((SUBLANES - 1, A_HALO + tl, cw), F32),
                        pltpu.VMEM((tl, pw), BF16), pltpu.VMEM((tl, cw), BF16)],
        compiler_params=pltpu.CompilerParams(dimension_semantics=("arbitrary", "arbitrary"),
                                             vmem_limit_bytes=VMEM_LIMIT),
        input_output_aliases=aliases,
        name="prompt_mixer",
    )(*ins)


def _sample_mixer(x, mod, sp_t, sc_t, wl, t_prompt, t_pad, *, conv_k):
    rows, d = x.shape
    tail = t_pad - t_prompt
    assert t_prompt % tail == 0 and rows % N_PARTS == 0 and rows // N_PARTS <= tail
    pw = wl["pool"].shape[0]
    cw = wl["dw"].shape[1]
    ne = wl["router_t"].shape[0]
    ws = [wl[n] for n in _W_NAMES]
    kern = functools.partial(_sample_mixer_kernel, d=d, pw=pw, cw=cw, conv_k=conv_k)
    ins = [x, mod, sp_t, sc_t] + ws
    tail_specs = [pl.BlockSpec((tail, d // 2), lambda i: (t_prompt // tail, 0)),
                  pl.BlockSpec((ne, tail), lambda i: (0, t_prompt // tail))] * N_PARTS
    tail_shapes = [jax.ShapeDtypeStruct((t_pad, d // 2), I32), jax.ShapeDtypeStruct((ne, t_pad), F32)] * N_PARTS
    outs = pl.pallas_call(
        kern,
        grid=(1,),
        in_specs=[_full_spec(a) for a in ins],
        out_specs=[pl.BlockSpec((rows, d), lambda i: (0, 0)),
                   pl.BlockSpec((rows, pw), lambda i: (0, 0)),
                   pl.BlockSpec((rows, cw), lambda i: (0, 0))] + tail_specs,
        out_shape=[jax.ShapeDtypeStruct((rows, d), F32),
                   jax.ShapeDtypeStruct((rows, pw), F32),
                   jax.ShapeDtypeStruct((rows, cw), F32)] + tail_shapes,
        scratch_shapes=[pltpu.VMEM((rows, d // 2), I32), pltpu.VMEM((ne, rows), F32)],
        compiler_params=pltpu.CompilerParams(dimension_semantics=("arbitrary",),
                                             vmem_limit_bytes=VMEM_LIMIT),
        name="sample_mixer",
    )(*ins)
    return outs[0], outs[1], outs[2], [(outs[3 + 2 * p], outs[4 + 2 * p]) for p in range(N_PARTS)]


def _sublane_max(x):
    return jnp.max(x, axis=0, keepdims=True)


def _route_kernel(lg_ref, b_ref, tri_ref, idx_ref, wt_ref, rank_ref, cnt_ref, carry, *, t_valid):
    i = pl.program_id(0)
    ne, tt = lg_ref.shape
    ng = N_EXPERT_GROUPS
    per = ne // ng

    @pl.when(i == 0)
    def _():
        carry[...] = jnp.zeros_like(carry)

    s = _sigmoid(lg_ref[...])
    sel = s + b_ref[...]
    s3 = [s[p * ng:(p + 1) * ng, :] for p in range(per)]
    sel3 = [sel[p * ng:(p + 1) * ng, :] for p in range(per)]
    m1 = sel3[0]
    m2 = jnp.full_like(m1, NEG_INF)
    for p in range(1, per):
        m2 = jnp.maximum(m2, jnp.minimum(m1, sel3[p]))
        m1 = jnp.maximum(m1, sel3[p])
    gs = m1 + m2
    gi = lax.broadcasted_iota(jnp.int32, (ng, tt), 0)
    beaten = jnp.zeros((ng, tt), jnp.int32)
    for g in range(ng):
        row = gs[g:g + 1, :]
        beats = (row > gs) | ((row == gs) & (gi > g))
        beaten = beaten + beats.astype(jnp.int32)
    keep = beaten < TOPK_GROUPS
    cur = [jnp.where(keep, sel3[p], NEG_INF) for p in range(per)]
    eid = [(gi * per + p).astype(F32) for p in range(per)]
    idxs, wts, hits = [], [], []
    for _ in range(TOP_K):
        m = cur[0]
        for p in range(1, per):
            m = jnp.maximum(m, cur[p])
        m = _sublane_max(m)
        cand = jnp.where(cur[0] == m, eid[0], float(ne))
        for p in range(1, per):
            cand = jnp.minimum(cand, jnp.where(cur[p] == m, eid[p], float(ne)))
        e_sel = jnp.min(cand, axis=0, keepdims=True)
        hit = [eid[p] == e_sel for p in range(per)]
        wk = jnp.where(hit[0], s3[0], 0.0)
        for p in range(1, per):
            wk = wk + jnp.where(hit[p], s3[p], 0.0)
        wts.append(jnp.sum(wk, axis=0, keepdims=True))
        cur = [jnp.where(hit[p], NEG_INF, cur[p]) for p in range(per)]
        idxs.append(e_sel)
        hits.append(hit)
    wsum = wts[0]
    for k in range(1, TOP_K):
        wsum = wsum + wts[k]
    chosen = [functools.reduce(lambda a, b: a | b, [hits[k][p] for k in range(TOP_K)]) for p in range(per)]
    real = (i * tt + lax.broadcasted_iota(I32, (1, tt), 1)) < t_valid
    onehot = jnp.concatenate([(c & real).astype(F32) for c in chosen], axis=0)
    before = _dot(onehot.astype(BF16), tri_ref[...]) + carry[...]
    for k in range(TOP_K):
        rk = jnp.where(hits[k][0], before[0:ng, :], 0.0)
        for p in range(1, per):
            rk = rk + jnp.where(hits[k][p], before[p * ng:(p + 1) * ng, :], 0.0)
        rank_ref[k:k + 1, :] = jnp.sum(rk, axis=0, keepdims=True).astype(jnp.int32)
        idx_ref[k:k + 1, :] = idxs[k].astype(jnp.int32)
        wt_ref[k:k + 1, :] = wts[k] / wsum * ROUTED_SCALE
    carry[...] = carry[...] + jnp.sum(onehot, axis=1, keepdims=True)
    cnt_ref[...] = carry[...]


def _route(lgt, b_perm, t_valid):
    ne, t = lgt.shape
    tt = ROUTE_TILE
    tri = (jnp.arange(tt)[:, None] < jnp.arange(tt)[None, :]).astype(BF16)
    return pl.pallas_call(
        functools.partial(_route_kernel, t_valid=t_valid),
        grid=(t // tt,),
        in_specs=[pl.BlockSpec((ne, tt), lambda i: (0, i)),
                  pl.BlockSpec((ne, 1), lambda i: (0, 0)),
                  pl.BlockSpec((tt, tt), lambda i: (0, 0))],
        out_specs=[pl.BlockSpec((TOP_K, tt), lambda i: (0, i)),
                   pl.BlockSpec((TOP_K, tt), lambda i: (0, i)),
                   pl.BlockSpec((TOP_K, tt), lambda i: (0, i)),
                   pl.BlockSpec((ne, 1), lambda i: (0, 0))],
        out_shape=[jax.ShapeDtypeStruct((TOP_K, t), jnp.int32),
                   jax.ShapeDtypeStruct((TOP_K, t), F32),
                   jax.ShapeDtypeStruct((TOP_K, t), jnp.int32),
                   jax.ShapeDtypeStruct((ne, 1), F32)],
        scratch_shapes=[pltpu.VMEM((ne, 1), F32)],
        compiler_params=pltpu.CompilerParams(dimension_semantics=("arbitrary",),
                                             vmem_limit_bytes=VMEM_LIMIT),
        name="route",
    )(lgt, b_perm, tri)


def _dest_kernel(pstart_ref, idx_ref, rank_ref, dest_ref, *, t_valid, spare_row):
    idx = idx_ref[...]
    k_top, tile = idx.shape

    def add_start(e, acc):
        return acc + jnp.where(idx == e, pstart_ref[e], 0)

    dest = lax.fori_loop(0, pstart_ref.shape[0], add_start, rank_ref[...])
    tok = pl.program_id(0) * tile + lax.broadcasted_iota(I32, idx.shape, 1)
    spare = spare_row + (tok - t_valid) * k_top + lax.broadcasted_iota(I32, idx.shape, 0)
    dest_ref[...] = jnp.where(tok < t_valid, dest, spare)


def _dest_rows(pstart, idx, rank, t_valid, spare_row):
    k_top, t = idx.shape
    tile = DEST_TILE
    spec = pl.BlockSpec((k_top, tile), lambda i, ps: (0, i))
    return pl.pallas_call(
        functools.partial(_dest_kernel, t_valid=t_valid, spare_row=spare_row),
        grid_spec=pltpu.PrefetchScalarGridSpec(num_scalar_prefetch=1, grid=(t // tile,),
                                               in_specs=[spec, spec], out_specs=spec),
        out_shape=jax.ShapeDtypeStruct((k_top, t), I32),
        compiler_params=pltpu.CompilerParams(dimension_semantics=("arbitrary",),
                                             vmem_limit_bytes=VMEM_LIMIT),
        name="dest_rows",
    )(pstart, idx, rank)


def _expert_kernel(first_ref, cnt_ref, xs_hbm, wgu_ref, wd_ref, y_hbm, xbuf, ybuf, in_sem, out_sem):
    e = pl.program_id(0)
    ne = first_ref.shape[0] - 1
    nbuf, rows = xbuf.shape[:2]
    de = wd_ref.shape[1]
    b0, b1 = first_ref[e], first_ref[e + 1]
    n_total = first_ref[ne]

    def in_copy(g):
        return pltpu.make_async_copy(xs_hbm.at[pl.ds(g * rows, rows)], xbuf.at[g % nbuf], in_sem.at[g % nbuf])

    def out_copy(g):
        return pltpu.make_async_copy(ybuf.at[g % nbuf], y_hbm.at[pl.ds(g * rows, rows)], out_sem.at[g % nbuf])

    @pl.when(e == 0)
    def _():
        for g in range(nbuf - 1):
            @pl.when(g < n_total)
            def _():
                in_copy(g).start()

    def block(g, carry):
        slot = g % nbuf

        @pl.when(g + nbuf - 1 < n_total)
        def _():
            in_copy(g + nbuf - 1).start()

        in_copy(g).wait()

        @pl.when(g >= nbuf)
        def _():
            out_copy(g - nbuf).wait()

        valid = cnt_ref[e] - (g - b0) * rows
        rid = lax.broadcasted_iota(I32, (rows, 1), 0)
        xs = jnp.where(rid < valid, _unpack_bf16_pairs(xbuf[slot]), 0.0).astype(BF16)
        gu = _dot(xs, wgu_ref[0])
        hb = _silu(gu[:, :de]) * gu[:, de:]
        ybuf[slot] = _pack_bf16_pairs(_dot(hb.astype(BF16), wd_ref[0]))
        out_copy(g).start()
        return carry

    lax.fori_loop(b0, b1, block, 0)

    @pl.when(e == ne - 1)
    def _():
        for back in range(nbuf, 0, -1):
            @pl.when(n_total >= back)
            def _():
                out_copy(n_total - back).wait()


def _experts(xs, w_gu, w_down, first_block, counts):
    p, dw = xs.shape
    ne, de, d = w_down.shape
    return pl.pallas_call(
        _expert_kernel,
        grid_spec=pltpu.PrefetchScalarGridSpec(
            num_scalar_prefetch=2,
            grid=(ne,),
            in_specs=[pl.BlockSpec(memory_space=pl.ANY),
                      pl.BlockSpec((1, d, 2 * de), lambda e, fb, cn: (e, 0, 0)),
                      pl.BlockSpec((1, de, d), lambda e, fb, cn: (e, 0, 0))],
            out_specs=pl.BlockSpec(memory_space=pl.ANY),
            scratch_shapes=[pltpu.VMEM((EXPERT_BUFFERS, EXPERT_ROWS, dw), I32),
                            pltpu.VMEM((EXPERT_BUFFERS, EXPERT_ROWS, dw), I32),
                            pltpu.SemaphoreType.DMA((EXPERT_BUFFERS,)), pltpu.SemaphoreType.DMA((EXPERT_BUFFERS,))]),
        out_shape=jax.ShapeDtypeStruct((p, dw), I32),
        compiler_params=pltpu.CompilerParams(dimension_semantics=("arbitrary",),
                                             vmem_limit_bytes=VMEM_LIMIT),
        name="experts",
    )(first_block, counts, xs, w_gu, w_down)


def _sc_first_chunk(n_chunks):
    per_worker = n_chunks // SC_WORKERS
    assert per_worker * SC_WORKERS == n_chunks
    return per_worker, (lax.axis_index("s") * SC_CORES + lax.axis_index("c")) * per_worker


def _sc_start(copies):
    for cp in copies:
        cp.start()


def _sc_wait(copies):
    for cp in copies:
        cp.wait()


def _sc_dispatch(h2, dest, n_rows):
    w = h2.shape[1]
    n_chunks, k_top, ch = dest.shape
    mesh = plsc.VectorSubcoreMesh(core_axis_name="c", subcore_axis_name="s")

    @functools.partial(
        pl.kernel, mesh=mesh, out_type=jax.ShapeDtypeStruct((n_rows, w), I32),
        scratch_types=[pltpu.VMEM((2, k_top, ch), I32), pltpu.VMEM((2, ch, w), I32),
                       pltpu.SemaphoreType.DMA((2,)), pltpu.SemaphoreType.DMA((2,))])
    def dispatch(h2_hbm, dest_hbm, xs_hbm, idx_v, rows_v, load_sem, store_sem):
        per_worker, first = _sc_first_chunk(n_chunks)

        def loads(i):
            slot = i % 2
            return (pltpu.make_async_copy(h2_hbm.at[pl.ds((first + i) * ch, ch)], rows_v.at[slot], load_sem.at[slot]),
                    pltpu.make_async_copy(dest_hbm.at[first + i], idx_v.at[slot], load_sem.at[slot]))

        def scatters(i):
            slot = i % 2
            return [pltpu.make_async_copy(rows_v.at[slot], xs_hbm.at[idx_v.at[slot, k]], store_sem.at[slot])
                    for k in range(k_top)]

        for i in range(min(2, per_worker)):
            _sc_start(loads(i))
        for i in range(per_worker):
            _sc_wait(loads(i))
            _sc_start(scatters(i))
            if 1 <= i < per_worker - 1:
                _sc_wait(scatters(i - 1))
                _sc_start(loads(i + 1))
        for i in range(max(per_worker - 2, 0), per_worker):
            _sc_wait(scatters(i))

    return dispatch(h2, dest)


def _sc_combine_gather(y, dest, t):
    w = y.shape[1]
    n_chunks, k_top, ch = dest.shape
    n_local = n_chunks // SC_WORKERS
    mesh = plsc.VectorSubcoreMesh(core_axis_name="c", subcore_axis_name="s")

    @functools.partial(
        pl.kernel, mesh=mesh, out_type=jax.ShapeDtypeStruct((k_top, t, w), I32),
        scratch_types=[pltpu.VMEM((n_local, k_top, ch), I32), pltpu.VMEM((k_top, ch, w), I32),
                       pltpu.SemaphoreType.DMA((k_top,)), pltpu.SemaphoreType.DMA((k_top,))])
    def gather(y_hbm, dest_hbm, out_hbm, idx_v, rows_v, load_sem, store_sem):
        per_worker, first = _sc_first_chunk(n_chunks)
        pltpu.sync_copy(dest_hbm.at[pl.ds(first, per_worker)], idx_v)

        def fetch(i, k):
            return pltpu.make_async_copy(y_hbm.at[idx_v.at[i, k]], rows_v.at[k], load_sem.at[k])

        def put(i, k):
            return pltpu.make_async_copy(rows_v.at[k], out_hbm.at[k, pl.ds((first + i) * ch, ch)], store_sem.at[k])

        @pl.loop(0, per_worker)
        def _(i):
            for k in range(k_top):
                @pl.when(i > 0)
                def _():
                    put(i - 1, k).wait()

                fetch(i, k).start()
            for k in range(k_top):
                fetch(i, k).wait()
                put(i, k).start()

        for k in range(k_top):
            put(per_worker - 1, k).wait()

    return gather(y, dest)


def _final_kernel(xmid_ref, yg_ref, wt_ref, g2_ref, shf_ref, scf_ref, gf_ref, *rest):
    o_ref = rest[-1]
    wt = wt_ref[...].T[:xmid_ref.shape[0]]
    routed = wt[:, 0:1] * _unpack_bf16_pairs(yg_ref[0])
    for k in range(1, yg_ref.shape[0]):
        routed = routed + wt[:, k:k + 1] * _unpack_bf16_pairs(yg_ref[k])
    x2 = xmid_ref[...] + g2_ref[...] * routed
    o_ref[...] = _rms_norm(x2, gf_ref[...]) * (1.0 + scf_ref[...]) + shf_ref[...]


def _final_prompt(xmid, yg, wts_t, g2, shf, scf, g_final, b0, n_batch, out_prev):
    b, seq, d = xmid.shape
    k_top, _, w = yg.shape
    tl = FINAL_TILE
    nl = seq // tl
    vec = pl.BlockSpec((None, 1, d), lambda i, j: (i + b0, 0, 0))
    in_specs = [pl.BlockSpec((None, tl, d), lambda i, j: (i, j, 0)),
                pl.BlockSpec((k_top, tl, w), lambda i, j: (0, i * nl + j, 0)),
                pl.BlockSpec((k_top, tl), lambda i, j: (0, i * nl + j)),
                vec, vec, vec, pl.BlockSpec((1, d), lambda i, j: (0, 0))]
    args = [xmid, yg, wts_t, g2[:, None, :], shf[:, None, :], scf[:, None, :], g_final]
    aliases = {}
    if out_prev is not None:
        in_specs.append(pl.BlockSpec(memory_space=pl.ANY))
        args.append(out_prev)
        aliases = {7: 0}
    return pl.pallas_call(
        _final_kernel,
        grid=(b, nl),
        in_specs=in_specs,
        out_specs=pl.BlockSpec((None, tl, d), lambda i, j: (i + b0, j, 0)),
        out_shape=jax.ShapeDtypeStruct((n_batch, seq, d), F32),
        compiler_params=pltpu.CompilerParams(dimension_semantics=("arbitrary", "arbitrary"),
                                             vmem_limit_bytes=VMEM_LIMIT),
        input_output_aliases=aliases,
        name="final_prompt",
    )(*args)


def _final_sample(xmid, yg, wts_t, g2, shf, scf, g_final, first_row, after):
    rows, d = xmid.shape
    k_top, _, w = yg.shape
    blk = first_row // rows
    vec = pl.BlockSpec((rows, d), lambda i: (0, 0))
    return pl.pallas_call(
        _final_kernel,
        grid=(1,),
        in_specs=[vec,
                  pl.BlockSpec((k_top, rows, w), lambda i: (0, blk, 0)),
                  pl.BlockSpec((k_top, 128), lambda i: (0, first_row // 128)),
                  vec, vec, vec, pl.BlockSpec((1, d), lambda i: (0, 0)), pl.BlockSpec(memory_space=pl.ANY)],
        out_specs=vec,
        out_shape=jax.ShapeDtypeStruct(xmid.shape, F32),
        compiler_params=pltpu.CompilerParams(dimension_semantics=("arbitrary",),
                                             vmem_limit_bytes=VMEM_LIMIT),
        name="final_sample",
    )(xmid, yg, wts_t, g2, shf, scf, g_final, after)


def _block_diag_pairs(w_pool):
    g, c, _ = w_pool.shape
    eye = jnp.eye(g, dtype=w_pool.dtype)
    return (eye[:, None, :, None] * w_pool[:, :, None, :]).reshape(g * c, g * c)


def kernel(x_prompt, x_sample, state_pool, state_conv, c_prompt, c_sample, w_ada, b_ada, g_mix, w_in, w_pool, pool_scale, w_dw, b_dw, ln_g, ln_b, w_out, g_ffn, w_router, b_router, w_gate, w_up, w_down, ws_gate, ws_up, ws_down, w_ada_final, b_ada_final, g_final):
    bp, seq, d = x_prompt.shape
    bs = x_sample.shape[0]
    depth = w_ada.shape[0]
    assert depth == 1 and x_sample.shape[1] == 1
    conv_k = w_dw.shape[1]
    ne = w_router.shape[-1]
    per = ne // N_EXPERT_GROUPS
    tp = bp * seq
    t_all = tp + bs

    row_expert = (jnp.arange(ne) % N_EXPERT_GROUPS) * per + jnp.arange(ne) // N_EXPERT_GROUPS
    wl = {
        "g_mix": g_mix[0][None, :],
        "in": w_in[0].astype(BF16),
        "pool": _block_diag_pairs(w_pool[0]).astype(BF16),
        "pool_scale": pool_scale[0][None, :],
        "dw": w_dw[0],
        "b_dw": b_dw[0][None, :],
        "ln_g": ln_g[0][None, :],
        "ln_b": ln_b[0][None, :],
        "out": w_out[0].astype(BF16),
        "g_ffn": g_ffn[0][None, :],
        "router_t": w_router[0].T[row_expert].astype(BF16),
        "s_gu": jnp.concatenate([ws_gate[0], ws_up[0]], axis=1).astype(BF16),
        "s_down": ws_down[0].astype(BF16),
    }
    b_perm = b_router[0][row_expert][:, None]

    assert bp % N_PARTS == 0 and bs % N_PARTS == 0
    bpp, bsp = bp // N_PARTS, bs // N_PARTS
    tpp = bpp * seq
    t_part = tpp + bsp
    grain = SC_WORKERS * SC_DISPATCH_CHUNK * SC_COMBINE_CHUNK // math.gcd(SC_DISPATCH_CHUNK, SC_COMBINE_CHUNK)
    t_pad = (t_part + grain - 1) // grain * grain
    assert t_pad % ROUTE_TILE == 0 and t_pad % DEST_TILE == 0 and tpp % bsp == 0

    c_all = jnp.concatenate([c_prompt, c_sample], axis=0)
    mod = _ada(c_all, w_ada[0], b_ada[0])
    modf = _ada(c_all, w_ada_final, b_ada_final)
    g2 = mod[:, 5 * d:6 * d]
    shf, scf = modf[:, :d], modf[:, d:]

    sp_t = jnp.transpose(state_pool[0], (1, 0, 2))
    sc_t = jnp.transpose(state_conv[0], (1, 0, 2))
    xmid_s, u_s, a_s, tails = _sample_mixer(x_sample[:, 0, :], mod[bp:], sp_t, sc_t, wl, tpp, t_pad, conv_k=conv_k)

    n_blocks = (t_part * TOP_K + EXPERT_ROWS - 1) // EXPERT_ROWS + ne
    n_spare = -(-(t_pad - t_part) * TOP_K // EXPERT_ROWS)
    n_rows = (n_blocks + n_spare) * EXPERT_ROWS

    y_prompt, y_samples, npools, nconvs, w_bf = None, [], [], [], None
    mixed = []
    for p in range(N_PARTS):
        h2, lgt = tails[p]
        xmid_p, h2, lgt, npool_p, nconv_p, *w_bf = _prompt_mixer(
            x_prompt, mod[:bp], wl, h2, lgt, p * bpp, bpp, (w_gate[0], w_up[0], w_down[0]), w_bf, conv_k=conv_k)
        mixed.append((xmid_p, h2, lgt))
        npools.append(npool_p)
        nconvs.append(nconv_p)
    for p in range(N_PARTS):
        xmid_p, h2, lgt = mixed[p]
        idx, wts, rank, counts_perm = _route(lgt, b_perm, t_part)
        counts = jnp.zeros((ne,), I32).at[row_expert].set(counts_perm[:, 0].astype(I32))
        nblk = (counts + EXPERT_ROWS - 1) // EXPERT_ROWS
        first_block = jnp.concatenate([jnp.zeros((1,), I32), jnp.cumsum(nblk).astype(I32)])
        dest = _dest_rows(first_block[:ne] * EXPERT_ROWS, idx, rank, t_part, n_blocks * EXPERT_ROWS)

        def chunked(ch):
            return dest.reshape(TOP_K, t_pad // ch, ch).transpose(1, 0, 2)

        xs = _sc_dispatch(h2, chunked(SC_DISPATCH_CHUNK), n_rows)
        y = _experts(xs, w_bf[0], w_bf[1], first_block, counts)
        yg = _sc_combine_gather(y, chunked(SC_COMBINE_CHUNK), t_pad)

        wts_t = wts
        rs = slice(bp + p * bsp, bp + (p + 1) * bsp)
        y_samples.append(_final_sample(xmid_s[p * bsp:(p + 1) * bsp], yg, wts_t, g2[rs], shf[rs], scf[rs],
                                       g_final[None, :], tpp, g_final if y_prompt is None else y_prompt))
        y_prompt = _final_prompt(xmid_p, yg, wts_t, g2[:bp], shf[:bp], scf[:bp], g_final[None, :],
                                 p * bpp, bp, y_prompt)
    y_sample = jnp.concatenate(y_samples, axis=0)
    npool_p = jnp.concatenate(npools, axis=0)
    nconv_p = jnp.concatenate(nconvs, axis=0)

    new_pool_s = jnp.concatenate([state_pool[0][:, 1:], u_s[:, None, :]], axis=1)
    new_conv_s = jnp.concatenate([state_conv[0][:, 1:], a_s[:, None, :]], axis=1)
    return (y_prompt, y_sample[:, None, :], npool_p[None], nconv_p[None], new_pool_s[None], new_conv_s[None])
```

```python
import functools
import math

import jax
import jax.numpy as jnp
from jax import lax
from jax.experimental import pallas as pl
from jax.experimental.pallas import tpu as pltpu
from jax.experimental.pallas import tpu_sc as plsc

POOL_WINDOWS = (2, 4, 8, 16)
N_EXPERT_GROUPS = 8
TOPK_GROUPS = 4
TOP_K = 8
ROUTED_SCALE = 2.5
EPS = 1e-6

LANES = 128
SUBLANES = 8
BF16_ROWS = 16
VMEM_LIMIT = 52 * 1024 * 1024

SEQ_TILE = 512
ROW_CHUNK = 16
ROUTE_TILE = 384
EXPERT_ROWS = 512
N_PARTS = 2
EXPERT_BUFFERS = 6
FINAL_TILE = 512
DEST_TILE = 2816

SC_CORES = 2
SC_SUBCORES = 16
SC_WORKERS = SC_CORES * SC_SUBCORES
SC_DISPATCH_CHUNK = 88
SC_COMBINE_CHUNK = 24

F32 = jnp.float32
BF16 = jnp.bfloat16
I32 = jnp.int32
U32 = jnp.uint32
NEG_INF = float("-inf")
HI16 = 0xFFFF0000


def _sigmoid(x):
    return 1.0 / (1.0 + jnp.exp(-x))


def _silu(x):
    return x * _sigmoid(x)


def _rms_norm(x, g):
    return x * lax.rsqrt(jnp.mean(x * x, axis=-1, keepdims=True) + EPS) * g


def _dot(a, b):
    return jnp.dot(a, b, preferred_element_type=F32)


def _pack_bf16_pairs(x):
    w = x.shape[1] // 2
    bits = lax.bitcast_convert_type(x.astype(BF16).astype(F32), U32)
    return lax.bitcast_convert_type((bits[:, :w] >> 16) | (bits[:, w:] & U32(HI16)), I32)


def _unpack_bf16_pairs(p):
    bits = lax.bitcast_convert_type(p, U32)
    lo = lax.bitcast_convert_type(bits << 16, F32)
    hi = lax.bitcast_convert_type(bits & U32(HI16), F32)
    return jnp.concatenate([lo, hi], axis=1)


def _ada_kernel(c_ref, w_ref, b_ref, o_ref):
    s = _silu(c_ref[...])
    o_ref[...] = _dot(s.astype(BF16), w_ref[...].astype(BF16)) + b_ref[...]


def _ada(c, w, b):
    rows, d = c.shape
    n = w.shape[1]
    tn = 1024
    return pl.pallas_call(
        _ada_kernel,
        grid=(n // tn,),
        in_specs=[pl.BlockSpec((rows, d), lambda j: (0, 0)),
                  pl.BlockSpec((d, tn), lambda j: (0, j)),
                  pl.BlockSpec((1, tn), lambda j: (0, j))],
        out_specs=pl.BlockSpec((rows, tn), lambda j: (0, j)),
        out_shape=jax.ShapeDtypeStruct((rows, n), F32),
        compiler_params=pltpu.CompilerParams(dimension_semantics=("arbitrary",),
                                             vmem_limit_bytes=VMEM_LIMIT),
        name="ada",
    )(c, w, b.reshape(1, n))


def _mixer_tail(x, pool_d, conv_act, mod, w, xmid_ref, h2_ref, lgt_ref):
    sh2, sc2, g1, g2 = mod
    pw = pool_d.shape[1]
    pool_out = _dot(pool_d, w["pool"][...]) * w["pool_scale"][...]
    mix = _dot(pool_out.astype(BF16), w["out"][:pw, :]) + _dot(conv_act, w["out"][pw:, :])
    x1 = x + g1 * mix
    h2f = _rms_norm(x1, w["g_ffn"][...]) * (1.0 + sc2) + sh2
    h2_ref[...] = _pack_bf16_pairs(h2f)
    h2 = h2f.astype(BF16)
    lgt_ref[...] = lax.dot_general(w["router_t"][...], h2, (((1,), (1,)), ((), ())),
                                   preferred_element_type=F32)
    gu = _dot(h2, w["s_gu"][...])
    de = gu.shape[1] // 2
    hs = _silu(gu[:, :de]) * gu[:, de:]
    shared = _dot(hs.astype(BF16), w["s_down"][...])
    xmid_ref[...] = x1 + g2 * shared


_W_NAMES = ("g_mix", "in", "pool", "pool_scale", "dw", "b_dw", "ln_g", "ln_b", "out", "g_ffn",
            "router_t", "s_gu", "s_down")


def _layer_norm_silu(yc, g, b):
    mu = jnp.mean(yc, axis=-1, keepdims=True)
    yz = yc - mu
    var = jnp.mean(yz * yz, axis=-1, keepdims=True)
    return _silu(yz * lax.rsqrt(var + EPS) * g + b)


U_HALO = 32
A_HALO = 32


def _prompt_mixer_kernel(x_ref, mod_ref, *refs, tl, d, pw, cw, conv_k):
    nw = len(_W_NAMES)
    w = dict(zip(_W_NAMES, refs[:nw]))
    wg_ref, wu_ref, wd_ref = refs[nw + 2:nw + 5]
    xmid_ref, h2_ref, lgt_ref, npool_ref, nconv_ref, wgu_bf_ref, wd_bf_ref = refs[-16:-9]
    ubuf, s2buf, s4buf, s8buf, abuf, ashift, dbuf, cbuf, rowb = refs[-9:]
    de = wg_ref.shape[2]
    wgu_bf_ref[:, :, :de] = wg_ref[...].astype(BF16)
    wgu_bf_ref[:, :, de:] = wu_ref[...].astype(BF16)
    wd_bf_ref[...] = wd_ref[...].astype(BF16)
    l = pl.program_id(1)
    pool_buf = max(POOL_WINDOWS) - 1
    uh, ah = U_HALO, A_HALO
    pg = pw // len(POOL_WINDOWS)
    nu, na = uh + tl, ah + tl

    @pl.when(l == 0)
    def _():
        ubuf[0:uh, :] = jnp.zeros((uh, pw), F32)
        abuf[0:ah, :] = jnp.zeros((ah, cw), F32)

    @pl.when(l > 0)
    def _():
        ubuf[0:uh, :] = ubuf[tl:tl + uh, :]
        abuf[0:ah, :] = abuf[tl:tl + ah, :]

    x = x_ref[0]
    sh1 = mod_ref[0, :, 0 * d:1 * d]
    sc1 = mod_ref[0, :, 1 * d:2 * d]
    g1 = mod_ref[0, :, 2 * d:3 * d]
    sh2 = mod_ref[0, :, 3 * d:4 * d]
    sc2 = mod_ref[0, :, 4 * d:5 * d]
    g2 = mod_ref[0, :, 5 * d:6 * d]

    h = _rms_norm(x, w["g_mix"][...]) * (1.0 + sc1) + sh1
    z = _dot(h.astype(BF16), w["in"][...])
    u = z[:, :pw]
    ubuf[uh:nu, :] = u
    abuf[ah:na, :] = z[:, pw:pw + cw] * _sigmoid(z[:, pw + cw:])

    s2buf[8:nu, :] = ubuf[8:nu, :] + ubuf[7:nu - 1, :]
    s4buf[16:nu, :] = s2buf[16:nu, pg:] + s2buf[14:nu - 2, pg:]
    s8buf[24:nu, :] = s4buf[24:nu, pg:] + s4buf[20:nu - 4, pg:]
    t = l * tl + lax.broadcasted_iota(I32, (tl, 1), 0)
    sums = (s2buf[uh:nu, 0:pg], s4buf[uh:nu, 0:pg], s8buf[uh:nu, 0:pg],
            s8buf[uh:nu, pg:2 * pg] + s8buf[uh - 8:nu - 8, pg:2 * pg])
    for g, win in enumerate(POOL_WINDOWS):
        cols = slice(g * pg, (g + 1) * pg)
        inv = 1.0 / jnp.minimum(win, t + 1).astype(F32)
        dbuf[:, cols] = (sums[g] * inv - u[:, cols]).astype(BF16)

    for r in range(1, SUBLANES):
        ashift[r - 1, 8:na, :] = abuf[8 - r:na - r, :]

    for k in range(conv_k):
        rowb[k] = jnp.broadcast_to(w["dw"][k:k + 1, :], (SUBLANES, cw))
    rowb[conv_k] = jnp.broadcast_to(w["b_dw"][...], (SUBLANES, cw))

    for c in range(tl // ROW_CHUNK):
        groups = []
        for o in range(ah + c * ROW_CHUNK, ah + (c + 1) * ROW_CHUNK, SUBLANES):
            yc = abuf[o:o + SUBLANES, :] * rowb[conv_k - 1]
            for back in range(1, conv_k):
                q, r = divmod(back, SUBLANES)
                src = abuf if r == 0 else ashift.at[r - 1]
                yc = yc + src[o - q * SUBLANES:o - (q - 1) * SUBLANES, :] * rowb[conv_k - 1 - back]
            groups.append(yc + rowb[conv_k])
        r0 = c * ROW_CHUNK
        cbuf[r0:r0 + ROW_CHUNK, :] = _layer_norm_silu(
            jnp.concatenate(groups, axis=0), w["ln_g"][...], w["ln_b"][...]).astype(BF16)

    @pl.when(l == pl.num_programs(1) - 1)
    def _():
        npool_ref[0] = ubuf[nu - pool_buf:nu, :]
        nconv_ref[0] = abuf[na - (conv_k - 1):na, :]

    _mixer_tail(x, dbuf[...], cbuf[...], (sh2, sc2, g1, g2), w, xmid_ref.at[0], h2_ref, lgt_ref)


def _sample_mixer_kernel(x_ref, mod_ref, sp_ref, sc_ref, *refs, d, pw, cw, conv_k):
    nw = len(_W_NAMES)
    w = dict(zip(_W_NAMES, refs[:nw]))
    xmid_ref, npool_ref, nconv_ref = refs[nw:nw + 3]
    tails = refs[nw + 3:-2]
    h2_tmp, lgt_tmp = refs[-2:]
    pool_buf = max(POOL_WINDOWS) - 1
    pg = pw // len(POOL_WINDOWS)
    x = x_ref[...]
    sh1, sc1, g1, sh2, sc2, g2 = [mod_ref[:, i * d:(i + 1) * d] for i in range(6)]
    h = _rms_norm(x, w["g_mix"][...]) * (1.0 + sc1) + sh1
    z = _dot(h.astype(BF16), w["in"][...])
    u = z[:, :pw]
    a = z[:, pw:pw + cw] * _sigmoid(z[:, pw + cw:])
    npool_ref[:, 0:pool_buf - 1, :] = sp_ref[:, 1:pool_buf, :]
    npool_ref[:, pool_buf - 1, :] = u
    nconv_ref[:, 0:conv_k - 2, :] = sc_ref[:, 1:conv_k - 1, :]
    nconv_ref[:, conv_k - 2, :] = a
    ds = []
    for g, win in enumerate(POOL_WINDOWS):
        cols = slice(g * pg, (g + 1) * pg)
        acc = u[:, cols]
        for j in range(1, win):
            acc = acc + sp_ref[:, pool_buf - j, cols]
        cnt = float(min(win, 1 + pool_buf))
        ds.append(acc / cnt - u[:, cols])
    pool_d = jnp.concatenate(ds, axis=-1).astype(BF16)
    yc = a * w["dw"][conv_k - 1:conv_k, :]
    for k in range(conv_k - 1):
        yc = yc + sc_ref[:, k, :] * w["dw"][k:k + 1, :]
    yc = yc + w["b_dw"][...]
    conv_act = _layer_norm_silu(yc, w["ln_g"][...], w["ln_b"][...]).astype(BF16)
    _mixer_tail(x, pool_d, conv_act, (sh2, sc2, g1, g2), w, xmid_ref, h2_tmp, lgt_tmp)
    n_parts = len(tails) // 2
    share = x.shape[0] // n_parts
    for p in range(n_parts):
        h2_ref, lgt_ref = tails[2 * p], tails[2 * p + 1]
        h2_ref[...] = jnp.zeros_like(h2_ref)
        lgt_ref[...] = jnp.zeros_like(lgt_ref)
        h2_ref[0:share, :] = h2_tmp[p * share:(p + 1) * share, :]
        lgt_ref[:, 0:share] = lgt_tmp[:, p * share:(p + 1) * share]


def _full_spec(a):
    nd = a.ndim
    return pl.BlockSpec(a.shape, lambda *_: (0,) * nd)


def _prompt_mixer(x, mod, wl, h2_all, lgt_all, b0, b, w_experts, w_bf_prev, *, conv_k):
    n_batch, seq, d = x.shape
    w_gate, w_up, w_down = w_experts
    n_exp, _, de = w_gate.shape
    pw = wl["pool"].shape[0]
    cw = wl["dw"].shape[1]
    ne = wl["router_t"].shape[0]
    tl = SEQ_TILE
    nl = seq // tl
    pg = pw // len(POOL_WINDOWS)
    assert POOL_WINDOWS == (2, 4, 8, 16) and conv_k - 1 <= A_HALO
    ws = [wl[n] for n in _W_NAMES]
    kern = functools.partial(_prompt_mixer_kernel, tl=tl, d=d, pw=pw, cw=cw, conv_k=conv_k)
    ecs = n_exp // (n_batch * nl)
    assert ecs * n_batch * nl == n_exp

    def w_map(i, j):
        return ((i + b0) * nl + j, 0, 0)

    ins = [x, mod[:, None, :], *ws, h2_all, lgt_all, w_gate, w_up, w_down]
    in_specs = ([pl.BlockSpec((1, tl, d), lambda i, j: (i + b0, j, 0)),
                 pl.BlockSpec((1, 1, mod.shape[-1]), lambda i, j: (i + b0, 0, 0))]
                + [_full_spec(a) for a in ws]
                + [pl.BlockSpec(memory_space=pl.ANY), pl.BlockSpec(memory_space=pl.ANY),
                   pl.BlockSpec((ecs, d, de), w_map), pl.BlockSpec((ecs, d, de), w_map),
                   pl.BlockSpec((ecs, de, d), w_map)])
    aliases = {2 + len(ws): 1, 3 + len(ws): 2}
    if w_bf_prev is not None:
        aliases.update({len(ins): 5, len(ins) + 1: 6})
        ins += list(w_bf_prev)
        in_specs += [pl.BlockSpec(memory_space=pl.ANY), pl.BlockSpec(memory_space=pl.ANY)]
    return pl.pallas_call(
        kern,
        grid=(b, nl),
        in_specs=in_specs,
        out_specs=[pl.BlockSpec((1, tl, d), lambda i, j: (i, j, 0)),
                   pl.BlockSpec((tl, d // 2), lambda i, j: (i * nl + j, 0)),
                   pl.BlockSpec((ne, tl), lambda i, j: (0, i * nl + j)),
                   pl.BlockSpec((1, max(POOL_WINDOWS) - 1, pw), lambda i, j: (i, 0, 0)),
                   pl.BlockSpec((1, conv_k - 1, cw), lambda i, j: (i, 0, 0)),
                   pl.BlockSpec((ecs, d, 2 * de), w_map), pl.BlockSpec((ecs, de, d), w_map)],
        out_shape=[jax.ShapeDtypeStruct((b, seq, d), F32),
                   jax.ShapeDtypeStruct(h2_all.shape, I32),
                   jax.ShapeDtypeStruct(lgt_all.shape, F32),
                   jax.ShapeDtypeStruct((b, max(POOL_WINDOWS) - 1, pw), F32),
                   jax.ShapeDtypeStruct((b, conv_k - 1, cw), F32),
                   jax.ShapeDtypeStruct((n_exp, d, 2 * de), BF16),
                   jax.ShapeDtypeStruct((n_exp, de, d), BF16)],
        scratch_shapes=[pltpu.VMEM((U_HALO + tl, pw), F32), pltpu.VMEM((U_HALO + tl, pw), F32),
                        pltpu.VMEM((U_HALO + tl, pw - pg), F32), pltpu.VMEM((U_HALO + tl, pw - 2 * pg), F32),
                        pltpu.VMEM((A_HALO + tl, cw), F32), pltpu.VMEM((SUBLANES - 1, A_HALO + tl, cw), F32),
                        pltpu.VMEM((tl, pw), BF16), pltpu.VMEM((tl, cw), BF16),
                        pltpu.VMEM((conv_k + 1, SUBLANES, cw), F32)],
        compiler_params=pltpu.CompilerParams(dimension_semantics=("arbitrary", "arbitrary"),
                                             vmem_limit_bytes=VMEM_LIMIT),
        input_output_aliases=aliases,
        name="prompt_mixer",
    )(*ins)


def _sample_mixer(x, mod, sp, sc, wl, t_prompt, t_pad, *, conv_k):
    rows, d = x.shape
    tail = t_pad - t_prompt
    assert t_prompt % tail == 0 and rows % N_PARTS == 0 and rows // N_PARTS <= tail
    pw = wl["pool"].shape[0]
    cw = wl["dw"].shape[1]
    ne = wl["router_t"].shape[0]
    ws = [wl[n] for n in _W_NAMES]
    kern = functools.partial(_sample_mixer_kernel, d=d, pw=pw, cw=cw, conv_k=conv_k)
    ins = [x, mod, sp, sc] + ws
    tail_specs = [pl.BlockSpec((tail, d // 2), lambda i: (t_prompt // tail, 0)),
                  pl.BlockSpec((ne, tail), lambda i: (0, t_prompt // tail))] * N_PARTS
    tail_shapes = [jax.ShapeDtypeStruct((t_pad, d // 2), I32), jax.ShapeDtypeStruct((ne, t_pad), F32)] * N_PARTS
    outs = pl.pallas_call(
        kern,
        grid=(1,),
        in_specs=[_full_spec(a) for a in ins],
        out_specs=[pl.BlockSpec((rows, d), lambda i: (0, 0)), _full_spec(sp), _full_spec(sc)] + tail_specs,
        out_shape=[jax.ShapeDtypeStruct((rows, d), F32),
                   jax.ShapeDtypeStruct(sp.shape, F32),
                   jax.ShapeDtypeStruct(sc.shape, F32)] + tail_shapes,
        scratch_shapes=[pltpu.VMEM((rows, d // 2), I32), pltpu.VMEM((ne, rows), F32)],
        compiler_params=pltpu.CompilerParams(dimension_semantics=("arbitrary",),
                                             vmem_limit_bytes=VMEM_LIMIT),
        name="sample_mixer",
    )(*ins)
    return outs[0], outs[1], outs[2], [(outs[3 + 2 * p], outs[4 + 2 * p]) for p in range(N_PARTS)]


def _sublane_max(x):
    return jnp.max(x, axis=0, keepdims=True)


def _route_kernel(lg_ref, b_ref, tri_ref, idx_ref, wt_ref, rank_ref, cnt_ref, carry, *, t_valid):
    i = pl.program_id(0)
    ne, tt = lg_ref.shape
    ng = N_EXPERT_GROUPS
    per = ne // ng

    @pl.when(i == 0)
    def _():
        carry[...] = jnp.zeros_like(carry)

    s = _sigmoid(lg_ref[...])
    sel = s + b_ref[...]
    s3 = [s[p * ng:(p + 1) * ng, :] for p in range(per)]
    sel3 = [sel[p * ng:(p + 1) * ng, :] for p in range(per)]
    m1 = sel3[0]
    m2 = jnp.full_like(m1, NEG_INF)
    for p in range(1, per):
        m2 = jnp.maximum(m2, jnp.minimum(m1, sel3[p]))
        m1 = jnp.maximum(m1, sel3[p])
    gs = m1 + m2
    gi = lax.broadcasted_iota(jnp.int32, (ng, tt), 0)
    beaten = jnp.zeros((ng, tt), jnp.int32)
    for g in range(ng):
        row = gs[g:g + 1, :]
        beats = (row > gs) | ((row == gs) & (gi > g))
        beaten = beaten + beats.astype(jnp.int32)
    keep = beaten < TOPK_GROUPS
    cur = [jnp.where(keep, sel3[p], NEG_INF) for p in range(per)]
    eid = [(gi * per + p).astype(F32) for p in range(per)]
    idxs, wts, hits = [], [], []
    for _ in range(TOP_K):
        m = cur[0]
        for p in range(1, per):
            m = jnp.maximum(m, cur[p])
        m = _sublane_max(m)
        cand = jnp.where(cur[0] == m, eid[0], float(ne))
        for p in range(1, per):
            cand = jnp.minimum(cand, jnp.where(cur[p] == m, eid[p], float(ne)))
        e_sel = jnp.min(cand, axis=0, keepdims=True)
        hit = [eid[p] == e_sel for p in range(per)]
        wk = jnp.where(hit[0], s3[0], 0.0)
        for p in range(1, per):
            wk = wk + jnp.where(hit[p], s3[p], 0.0)
        wts.append(jnp.sum(wk, axis=0, keepdims=True))
        cur = [jnp.where(hit[p], NEG_INF, cur[p]) for p in range(per)]
        idxs.append(e_sel)
        hits.append(hit)
    wsum = wts[0]
    for k in range(1, TOP_K):
        wsum = wsum + wts[k]
    chosen = [functools.reduce(lambda a, b: a | b, [hits[k][p] for k in range(TOP_K)]) for p in range(per)]
    real = (i * tt + lax.broadcasted_iota(I32, (1, tt), 1)) < t_valid
    onehot = jnp.concatenate([(c & real).astype(F32) for c in chosen], axis=0)
    before = _dot(onehot.astype(BF16), tri_ref[...]) + carry[...]
    for k in range(TOP_K):
        rk = jnp.where(hits[k][0], before[0:ng, :], 0.0)
        for p in range(1, per):
            rk = rk + jnp.where(hits[k][p], before[p * ng:(p + 1) * ng, :], 0.0)
        rank_ref[k:k + 1, :] = jnp.sum(rk, axis=0, keepdims=True).astype(jnp.int32)
        idx_ref[k:k + 1, :] = idxs[k].astype(jnp.int32)
        wt_ref[k:k + 1, :] = wts[k] / wsum * ROUTED_SCALE
    carry[...] = carry[...] + jnp.sum(onehot, axis=1, keepdims=True)
    cnt_ref[...] = carry[...]


def _route(lgt, b_perm, t_valid):
    ne, t = lgt.shape
    tt = ROUTE_TILE
    tri = (jnp.arange(tt)[:, None] < jnp.arange(tt)[None, :]).astype(BF16)
    return pl.pallas_call(
        functools.partial(_route_kernel, t_valid=t_valid),
        grid=(t // tt,),
        in_specs=[pl.BlockSpec((ne, tt), lambda i: (0, i)),
                  pl.BlockSpec((ne, 1), lambda i: (0, 0)),
                  pl.BlockSpec((tt, tt), lambda i: (0, 0))],
        out_specs=[pl.BlockSpec((TOP_K, tt), lambda i: (0, i)),
                   pl.BlockSpec((TOP_K, tt), lambda i: (0, i)),
                   pl.BlockSpec((TOP_K, tt), lambda i: (0, i)),
                   pl.BlockSpec((ne, 1), lambda i: (0, 0))],
        out_shape=[jax.ShapeDtypeStruct((TOP_K, t), jnp.int32),
                   jax.ShapeDtypeStruct((TOP_K, t), F32),
                   jax.ShapeDtypeStruct((TOP_K, t), jnp.int32),
                   jax.ShapeDtypeStruct((ne, 1), F32)],
        scratch_shapes=[pltpu.VMEM((ne, 1), F32)],
        compiler_params=pltpu.CompilerParams(dimension_semantics=("arbitrary",),
                                             vmem_limit_bytes=VMEM_LIMIT),
        name="route",
    )(lgt, b_perm, tri)


def _dest_kernel(pstart_ref, idx_ref, rank_ref, dest_ref, *, t_valid, spare_row):
    idx = idx_ref[...]
    k_top, tile = idx.shape

    def add_start(e, acc):
        return acc + jnp.where(idx == e, pstart_ref[e], 0)

    dest = lax.fori_loop(0, pstart_ref.shape[0], add_start, rank_ref[...])
    tok = pl.program_id(0) * tile + lax.broadcasted_iota(I32, idx.shape, 1)
    spare = spare_row + (tok - t_valid) * k_top + lax.broadcasted_iota(I32, idx.shape, 0)
    dest_ref[...] = jnp.where(tok < t_valid, dest, spare)


def _dest_rows(pstart, idx, rank, t_valid, spare_row):
    k_top, t = idx.shape
    tile = DEST_TILE
    spec = pl.BlockSpec((k_top, tile), lambda i, ps: (0, i))
    return pl.pallas_call(
        functools.partial(_dest_kernel, t_valid=t_valid, spare_row=spare_row),
        grid_spec=pltpu.PrefetchScalarGridSpec(num_scalar_prefetch=1, grid=(t // tile,),
                                               in_specs=[spec, spec], out_specs=spec),
        out_shape=jax.ShapeDtypeStruct((k_top, t), I32),
        compiler_params=pltpu.CompilerParams(dimension_semantics=("arbitrary",),
                                             vmem_limit_bytes=VMEM_LIMIT),
        name="dest_rows",
    )(pstart, idx, rank)


def _expert_kernel(first_ref, cnt_ref, xs_hbm, wgu_ref, wd_ref, y_hbm, xbuf, ybuf, in_sem, out_sem):
    e = pl.program_id(0)
    ne = first_ref.shape[0] - 1
    nbuf, rows = xbuf.shape[:2]
    de = wd_ref.shape[1]
    b0, b1 = first_ref[e], first_ref[e + 1]
    n_total = first_ref[ne]

    def in_copy(g):
        return pltpu.make_async_copy(xs_hbm.at[pl.ds(g * rows, rows)], xbuf.at[g % nbuf], in_sem.at[g % nbuf])

    def out_copy(g):
        return pltpu.make_async_copy(ybuf.at[g % nbuf], y_hbm.at[pl.ds(g * rows, rows)], out_sem.at[g % nbuf])

    @pl.when(e == 0)
    def _():
        for g in range(nbuf - 1):
            @pl.when(g < n_total)
            def _():
                in_copy(g).start()

    def block(g, carry):
        slot = g % nbuf

        @pl.when(g + nbuf - 1 < n_total)
        def _():
            in_copy(g + nbuf - 1).start()

        in_copy(g).wait()

        @pl.when(g >= nbuf)
        def _():
            out_copy(g - nbuf).wait()

        valid = cnt_ref[e] - (g - b0) * rows
        rid = lax.broadcasted_iota(I32, (rows, 1), 0)
        xs = jnp.where(rid < valid, _unpack_bf16_pairs(xbuf[slot]), 0.0).astype(BF16)
        gu = _dot(xs, wgu_ref[0])
        hb = _silu(gu[:, :de]) * gu[:, de:]
        ybuf[slot] = _pack_bf16_pairs(_dot(hb.astype(BF16), wd_ref[0]))
        out_copy(g).start()
        return carry

    lax.fori_loop(b0, b1, block, 0)

    @pl.when(e == ne - 1)
    def _():
        for back in range(nbuf, 0, -1):
            @pl.when(n_total >= back)
            def _():
                out_copy(n_total - back).wait()


def _experts(xs, w_gu, w_down, first_block, counts):
    p, dw = xs.shape
    ne, de, d = w_down.shape
    return pl.pallas_call(
        _expert_kernel,
        grid_spec=pltpu.PrefetchScalarGridSpec(
            num_scalar_prefetch=2,
            grid=(ne,),
            in_specs=[pl.BlockSpec(memory_space=pl.ANY),
                      pl.BlockSpec((1, d, 2 * de), lambda e, fb, cn: (e, 0, 0)),
                      pl.BlockSpec((1, de, d), lambda e, fb, cn: (e, 0, 0))],
            out_specs=pl.BlockSpec(memory_space=pl.ANY),
            scratch_shapes=[pltpu.VMEM((EXPERT_BUFFERS, EXPERT_ROWS, dw), I32),
                            pltpu.VMEM((EXPERT_BUFFERS, EXPERT_ROWS, dw), I32),
                            pltpu.SemaphoreType.DMA((EXPERT_BUFFERS,)), pltpu.SemaphoreType.DMA((EXPERT_BUFFERS,))]),
        out_shape=jax.ShapeDtypeStruct((p, dw), I32),
        compiler_params=pltpu.CompilerParams(dimension_semantics=("arbitrary",),
                                             vmem_limit_bytes=VMEM_LIMIT),
        name="experts",
    )(first_block, counts, xs, w_gu, w_down)


def _sc_first_chunk(n_chunks):
    per_worker = n_chunks // SC_WORKERS
    assert per_worker * SC_WORKERS == n_chunks
    return per_worker, (lax.axis_index("s") * SC_CORES + lax.axis_index("c")) * per_worker


def _sc_start(copies):
    for cp in copies:
        cp.start()


def _sc_wait(copies):
    for cp in copies:
        cp.wait()


def _sc_dispatch(h2, dest, n_rows):
    w = h2.shape[1]
    n_chunks, k_top, ch = dest.shape
    mesh = plsc.VectorSubcoreMesh(core_axis_name="c", subcore_axis_name="s")

    @functools.partial(
        pl.kernel, mesh=mesh, out_type=jax.ShapeDtypeStruct((n_rows, w), I32),
        scratch_types=[pltpu.VMEM((2, k_top, ch), I32), pltpu.VMEM((2, ch, w), I32),
                       pltpu.SemaphoreType.DMA((2,)), pltpu.SemaphoreType.DMA((2,))])
    def dispatch(h2_hbm, dest_hbm, xs_hbm, idx_v, rows_v, load_sem, store_sem):
        per_worker, first = _sc_first_chunk(n_chunks)

        def loads(i):
            slot = i % 2
            return (pltpu.make_async_copy(h2_hbm.at[pl.ds((first + i) * ch, ch)], rows_v.at[slot], load_sem.at[slot]),
                    pltpu.make_async_copy(dest_hbm.at[first + i], idx_v.at[slot], load_sem.at[slot]))

        def scatters(i):
            slot = i % 2
            return [pltpu.make_async_copy(rows_v.at[slot], xs_hbm.at[idx_v.at[slot, k]], store_sem.at[slot])
                    for k in range(k_top)]

        for i in range(min(2, per_worker)):
            _sc_start(loads(i))
        for i in range(per_worker):
            _sc_wait(loads(i))
            _sc_start(scatters(i))
            if 1 <= i < per_worker - 1:
                _sc_wait(scatters(i - 1))
                _sc_start(loads(i + 1))
        for i in range(max(per_worker - 2, 0), per_worker):
            _sc_wait(scatters(i))

    return dispatch(h2, dest)


def _sc_combine_gather(y, dest, t):
    w = y.shape[1]
    n_chunks, k_top, ch = dest.shape
    n_local = n_chunks // SC_WORKERS
    mesh = plsc.VectorSubcoreMesh(core_axis_name="c", subcore_axis_name="s")

    @functools.partial(
        pl.kernel, mesh=mesh, out_type=jax.ShapeDtypeStruct((k_top, t, w), I32),
        scratch_types=[pltpu.VMEM((n_local, k_top, ch), I32), pltpu.VMEM((k_top, ch, w), I32),
                       pltpu.SemaphoreType.DMA((k_top,)), pltpu.SemaphoreType.DMA((k_top,))])
    def gather(y_hbm, dest_hbm, out_hbm, idx_v, rows_v, load_sem, store_sem):
        per_worker, first = _sc_first_chunk(n_chunks)
        pltpu.sync_copy(dest_hbm.at[pl.ds(first, per_worker)], idx_v)

        def fetch(i, k):
            return pltpu.make_async_copy(y_hbm.at[idx_v.at[i, k]], rows_v.at[k], load_sem.at[k])

        def put(i, k):
            return pltpu.make_async_copy(rows_v.at[k], out_hbm.at[k, pl.ds((first + i) * ch, ch)], store_sem.at[k])

        @pl.loop(0, per_worker)
        def _(i):
            for k in range(k_top):
                @pl.when(i > 0)
                def _():
                    put(i - 1, k).wait()

                fetch(i, k).start()
            for k in range(k_top):
                fetch(i, k).wait()
                put(i, k).start()

        for k in range(k_top):
            put(per_worker - 1, k).wait()

    return gather(y, dest)


def _final_kernel(xmid_ref, yg_ref, wt_ref, g2_ref, shf_ref, scf_ref, gf_ref, *rest):
    o_ref = rest[-1]
    wt = wt_ref[...].T[:xmid_ref.shape[0]]
    routed = wt[:, 0:1] * _unpack_bf16_pairs(yg_ref[0])
    for k in range(1, yg_ref.shape[0]):
        routed = routed + wt[:, k:k + 1] * _unpack_bf16_pairs(yg_ref[k])
    x2 = xmid_ref[...] + g2_ref[...] * routed
    o_ref[...] = _rms_norm(x2, gf_ref[...]) * (1.0 + scf_ref[...]) + shf_ref[...]


def _final_prompt(xmid, yg, wts_t, g2, shf, scf, g_final, b0, n_batch, out_prev):
    b, seq, d = xmid.shape
    k_top, _, w = yg.shape
    tl = FINAL_TILE
    nl = seq // tl
    vec = pl.BlockSpec((None, 1, d), lambda i, j: (i + b0, 0, 0))
    in_specs = [pl.BlockSpec((None, tl, d), lambda i, j: (i, j, 0)),
                pl.BlockSpec((k_top, tl, w), lambda i, j: (0, i * nl + j, 0)),
                pl.BlockSpec((k_top, tl), lambda i, j: (0, i * nl + j)),
                vec, vec, vec, pl.BlockSpec((1, d), lambda i, j: (0, 0))]
    args = [xmid, yg, wts_t, g2[:, None, :], shf[:, None, :], scf[:, None, :], g_final]
    aliases = {}
    if out_prev is not None:
        in_specs.append(pl.BlockSpec(memory_space=pl.ANY))
        args.append(out_prev)
        aliases = {7: 0}
    return pl.pallas_call(
        _final_kernel,
        grid=(b, nl),
        in_specs=in_specs,
        out_specs=pl.BlockSpec((None, tl, d), lambda i, j: (i + b0, j, 0)),
        out_shape=jax.ShapeDtypeStruct((n_batch, seq, d), F32),
        compiler_params=pltpu.CompilerParams(dimension_semantics=("arbitrary", "arbitrary"),
                                             vmem_limit_bytes=VMEM_LIMIT),
        input_output_aliases=aliases,
        name="final_prompt",
    )(*args)


def _final_sample(xmid, yg, wts_t, g2, shf, scf, g_final, first_row, after):
    rows, d = xmid.shape
    k_top, _, w = yg.shape
    blk = first_row // rows
    vec = pl.BlockSpec((rows, d), lambda i: (0, 0))
    return pl.pallas_call(
        _final_kernel,
        grid=(1,),
        in_specs=[vec,
                  pl.BlockSpec((k_top, rows, w), lambda i: (0, blk, 0)),
                  pl.BlockSpec((k_top, 128), lambda i: (0, first_row // 128)),
                  vec, vec, vec, pl.BlockSpec((1, d), lambda i: (0, 0)), pl.BlockSpec(memory_space=pl.ANY)],
        out_specs=vec,
        out_shape=jax.ShapeDtypeStruct(xmid.shape, F32),
        compiler_params=pltpu.CompilerParams(dimension_semantics=("arbitrary",),
                                             vmem_limit_bytes=VMEM_LIMIT),
        name="final_sample",
    )(xmid, yg, wts_t, g2, shf, scf, g_final, after)


def _block_diag_pairs(w_pool):
    g, c, _ = w_pool.shape
    eye = jnp.eye(g, dtype=w_pool.dtype)
    return (eye[:, None, :, None] * w_pool[:, :, None, :]).reshape(g * c, g * c)


def kernel(x_prompt, x_sample, state_pool, state_conv, c_prompt, c_sample, w_ada, b_ada, g_mix, w_in, w_pool, pool_scale, w_dw, b_dw, ln_g, ln_b, w_out, g_ffn, w_router, b_router, w_gate, w_up, w_down, ws_gate, ws_up, ws_down, w_ada_final, b_ada_final, g_final):
    bp, seq, d = x_prompt.shape
    bs = x_sample.shape[0]
    depth = w_ada.shape[0]
    assert depth == 1 and x_sample.shape[1] == 1
    conv_k = w_dw.shape[1]
    ne = w_router.shape[-1]
    per = ne // N_EXPERT_GROUPS
    tp = bp * seq
    t_all = tp + bs

    row_expert = (jnp.arange(ne) % N_EXPERT_GROUPS) * per + jnp.arange(ne) // N_EXPERT_GROUPS
    wl = {
        "g_mix": g_mix[0][None, :],
        "in": w_in[0].astype(BF16),
        "pool": _block_diag_pairs(w_pool[0]).astype(BF16),
        "pool_scale": pool_scale[0][None, :],
        "dw": w_dw[0],
        "b_dw": b_dw[0][None, :],
        "ln_g": ln_g[0][None, :],
        "ln_b": ln_b[0][None, :],
        "out": w_out[0].astype(BF16),
        "g_ffn": g_ffn[0][None, :],
        "router_t": w_router[0].T[row_expert].astype(BF16),
        "s_gu": jnp.concatenate([ws_gate[0], ws_up[0]], axis=1).astype(BF16),
        "s_down": ws_down[0].astype(BF16),
    }
    b_perm = b_router[0][row_expert][:, None]

    assert bp % N_PARTS == 0 and bs % N_PARTS == 0
    bpp, bsp = bp // N_PARTS, bs // N_PARTS
    tpp = bpp * seq
    t_part = tpp + bsp
    grain = SC_WORKERS * SC_DISPATCH_CHUNK * SC_COMBINE_CHUNK // math.gcd(SC_DISPATCH_CHUNK, SC_COMBINE_CHUNK)
    t_pad = (t_part + grain - 1) // grain * grain
    assert t_pad % ROUTE_TILE == 0 and t_pad % DEST_TILE == 0 and tpp % bsp == 0

    c_all = jnp.concatenate([c_prompt, c_sample], axis=0)
    mod = _ada(c_all, w_ada[0], b_ada[0])
    modf = _ada(c_all, w_ada_final, b_ada_final)
    g2 = mod[:, 5 * d:6 * d]
    shf, scf = modf[:, :d], modf[:, d:]

    xmid_s, new_pool_s, new_conv_s, tails = _sample_mixer(x_sample[:, 0, :], mod[bp:], state_pool[0], state_conv[0],
                                                          wl, tpp, t_pad, conv_k=conv_k)

    n_blocks = (t_part * TOP_K + EXPERT_ROWS - 1) // EXPERT_ROWS + ne
    n_spare = -(-(t_pad - t_part) * TOP_K // EXPERT_ROWS)
    n_rows = (n_blocks + n_spare) * EXPERT_ROWS

    y_prompt, y_samples, npools, nconvs, w_bf = None, [], [], [], None
    mixed = []
    for p in range(N_PARTS):
        h2, lgt = tails[p]
        xmid_p, h2, lgt, npool_p, nconv_p, *w_bf = _prompt_mixer(
            x_prompt, mod[:bp], wl, h2, lgt, p * bpp, bpp, (w_gate[0], w_up[0], w_down[0]), w_bf, conv_k=conv_k)
        mixed.append((xmid_p, h2, lgt))
        npools.append(npool_p)
        nconvs.append(nconv_p)
    for p in range(N_PARTS):
        xmid_p, h2, lgt = mixed[p]
        idx, wts, rank, counts_perm = _route(lgt, b_perm, t_part)
        counts = jnp.zeros((ne,), I32).at[row_expert].set(counts_perm[:, 0].astype(I32))
        nblk = (counts + EXPERT_ROWS - 1) // EXPERT_ROWS
        first_block = jnp.concatenate([jnp.zeros((1,), I32), jnp.cumsum(nblk).astype(I32)])
        dest = _dest_rows(first_block[:ne] * EXPERT_ROWS, idx, rank, t_part, n_blocks * EXPERT_ROWS)

        def chunked(ch):
            return dest.reshape(TOP_K, t_pad // ch, ch).transpose(1, 0, 2)

        xs = _sc_dispatch(h2, chunked(SC_DISPATCH_CHUNK), n_rows)
        y = _experts(xs, w_bf[0], w_bf[1], first_block, counts)
        yg = _sc_combine_gather(y, chunked(SC_COMBINE_CHUNK), t_pad)

        wts_t = wts
        rs = slice(bp + p * bsp, bp + (p + 1) * bsp)
        y_samples.append(_final_sample(xmid_s[p * bsp:(p + 1) * bsp], yg, wts_t, g2[rs], shf[rs], scf[rs],
                                       g_final[None, :], tpp, g_final if y_prompt is None else y_prompt))
        y_prompt = _final_prompt(xmid_p, yg, wts_t, g2[:bp], shf[:bp], scf[:bp], g_final[None, :],
                                 p * bpp, bp, y_prompt)
    y_sample = jnp.concatenate(y_samples, axis=0)
    npool_p = jnp.concatenate(npools, axis=0)
    nconv_p = jnp.concatenate(nconvs, axis=0)

    return (y_prompt, y_sample[:, None, :], npool_p[None], nconv_p[None], new_pool_s[None], new_conv_s[None])
```

```python
import functools
import math

import jax
import jax.numpy as jnp
from jax import lax
from jax.experimental import pallas as pl
from jax.experimental.pallas import tpu as pltpu
from jax.experimental.pallas import tpu_sc as plsc

POOL_WINDOWS = (2, 4, 8, 16)
N_EXPERT_GROUPS = 8
TOPK_GROUPS = 4
TOP_K = 8
ROUTED_SCALE = 2.5
EPS = 1e-6

LANES = 128
SUBLANES = 8
BF16_ROWS = 16
VMEM_LIMIT = 52 * 1024 * 1024

SEQ_TILE = 512
ROW_CHUNK = 16
ROUTE_TILE = 384
EXPERT_ROWS = 512
N_PARTS = 2
EXPERT_BUFFERS = 6
FINAL_TILE = 512
DEST_TILE = 2816

SC_CORES = 2
SC_SUBCORES = 16
SC_WORKERS = SC_CORES * SC_SUBCORES
SC_DISPATCH_CHUNK = 88
SC_COMBINE_CHUNK = 24

F32 = jnp.float32
BF16 = jnp.bfloat16
I32 = jnp.int32
U32 = jnp.uint32
NEG_INF = float("-inf")
HI16 = 0xFFFF0000


def _sigmoid(x):
    return 1.0 / (1.0 + jnp.exp(-x))


def _silu(x):
    return x * _sigmoid(x)


def _rms_norm(x, g):
    return x * lax.rsqrt(jnp.mean(x * x, axis=-1, keepdims=True) + EPS) * g


def _dot(a, b):
    return jnp.dot(a, b, preferred_element_type=F32)


def _pack_bf16_pairs(x):
    w = x.shape[1] // 2
    bits = lax.bitcast_convert_type(x.astype(BF16).astype(F32), U32)
    return lax.bitcast_convert_type((bits[:, :w] >> 16) | (bits[:, w:] & U32(HI16)), I32)


def _unpack_bf16_pairs(p):
    bits = lax.bitcast_convert_type(p, U32)
    lo = lax.bitcast_convert_type(bits << 16, F32)
    hi = lax.bitcast_convert_type(bits & U32(HI16), F32)
    return jnp.concatenate([lo, hi], axis=1)


def _ada_kernel(c_ref, w_ref, b_ref, o_ref):
    s = _silu(c_ref[...])
    o_ref[...] = _dot(s.astype(BF16), w_ref[...].astype(BF16)) + b_ref[...]


def _ada(c, w, b):
    rows, d = c.shape
    n = w.shape[1]
    tn = 1024
    return pl.pallas_call(
        _ada_kernel,
        grid=(n // tn,),
        in_specs=[pl.BlockSpec((rows, d), lambda j: (0, 0)),
                  pl.BlockSpec((d, tn), lambda j: (0, j)),
                  pl.BlockSpec((1, tn), lambda j: (0, j))],
        out_specs=pl.BlockSpec((rows, tn), lambda j: (0, j)),
        out_shape=jax.ShapeDtypeStruct((rows, n), F32),
        compiler_params=pltpu.CompilerParams(dimension_semantics=("arbitrary",),
                                             vmem_limit_bytes=VMEM_LIMIT),
        name="ada",
    )(c, w, b.reshape(1, n))


def _mixer_tail(x, pool_d, conv_act, mod, w, xmid_ref, h2_ref, lgt_ref):
    sh2, sc2, g1, g2 = mod
    pw = pool_d.shape[1]
    pool_out = _dot(pool_d, w["pool"][...]) * w["pool_scale"][...]
    mix = _dot(pool_out.astype(BF16), w["out"][:pw, :]) + _dot(conv_act, w["out"][pw:, :])
    x1 = x + g1 * mix
    h2f = _rms_norm(x1, w["g_ffn"][...]) * (1.0 + sc2) + sh2
    h2_ref[...] = _pack_bf16_pairs(h2f)
    h2 = h2f.astype(BF16)
    lgt_ref[...] = lax.dot_general(w["router_t"][...], h2, (((1,), (1,)), ((), ())),
                                   preferred_element_type=F32)
    gu = _dot(h2, w["s_gu"][...])
    de = gu.shape[1] // 2
    hs = _silu(gu[:, :de]) * gu[:, de:]
    shared = _dot(hs.astype(BF16), w["s_down"][...])
    xmid_ref[...] = x1 + g2 * shared


_W_NAMES = ("g_mix", "in", "pool", "pool_scale", "dw", "b_dw", "ln_g", "ln_b", "out", "g_ffn",
            "router_t", "s_gu", "s_down")


def _layer_norm_silu(yc, g, b):
    mu = jnp.mean(yc, axis=-1, keepdims=True)
    yz = yc - mu
    var = jnp.mean(yz * yz, axis=-1, keepdims=True)
    return _silu(yz * lax.rsqrt(var + EPS) * g + b)


U_HALO = 32
A_HALO = 32


def _prompt_mixer_kernel(x_ref, mod_ref, *refs, tl, d, pw, cw, conv_k):
    nw = len(_W_NAMES)
    w = dict(zip(_W_NAMES, refs[:nw]))
    wg_ref, wu_ref, wd_ref = refs[nw + 2:nw + 5]
    xmid_ref, h2_ref, lgt_ref, npool_ref, nconv_ref, wgu_bf_ref, wd_bf_ref = refs[-16:-9]
    ubuf, s2buf, s4buf, s8buf, abuf, ashift, dbuf, cbuf, rowb = refs[-9:]
    de = wg_ref.shape[2]
    wgu_bf_ref[:, :, :de] = wg_ref[...].astype(BF16)
    wgu_bf_ref[:, :, de:] = wu_ref[...].astype(BF16)
    wd_bf_ref[...] = wd_ref[...].astype(BF16)
    l = pl.program_id(1)
    pool_buf = max(POOL_WINDOWS) - 1
    uh, ah = U_HALO, A_HALO
    pg = pw // len(POOL_WINDOWS)
    nu, na = uh + tl, ah + tl

    @pl.when(l == 0)
    def _():
        ubuf[0:uh, :] = jnp.zeros((uh, pw), F32)
        abuf[0:ah, :] = jnp.zeros((ah, cw), F32)

    @pl.when(l > 0)
    def _():
        ubuf[0:uh, :] = ubuf[tl:tl + uh, :]
        abuf[0:ah, :] = abuf[tl:tl + ah, :]

    x = x_ref[0]
    sh1 = mod_ref[0, :, 0 * d:1 * d]
    sc1 = mod_ref[0, :, 1 * d:2 * d]
    g1 = mod_ref[0, :, 2 * d:3 * d]
    sh2 = mod_ref[0, :, 3 * d:4 * d]
    sc2 = mod_ref[0, :, 4 * d:5 * d]
    g2 = mod_ref[0, :, 5 * d:6 * d]

    h = _rms_norm(x, w["g_mix"][...]) * (1.0 + sc1) + sh1
    z = _dot(h.astype(BF16), w["in"][...])
    u = z[:, :pw]
    ubuf[uh:nu, :] = u
    abuf[ah:na, :] = z[:, pw:pw + cw] * _sigmoid(z[:, pw + cw:])

    s2buf[8:nu, :] = ubuf[8:nu, :] + ubuf[7:nu - 1, :]
    s4buf[16:nu, :] = s2buf[16:nu, pg:] + s2buf[14:nu - 2, pg:]
    s8buf[24:nu, :] = s4buf[24:nu, pg:] + s4buf[20:nu - 4, pg:]
    t = l * tl + lax.broadcasted_iota(I32, (tl, 1), 0)
    sums = (s2buf[uh:nu, 0:pg], s4buf[uh:nu, 0:pg], s8buf[uh:nu, 0:pg],
            s8buf[uh:nu, pg:2 * pg] + s8buf[uh - 8:nu - 8, pg:2 * pg])
    for g, win in enumerate(POOL_WINDOWS):
        cols = slice(g * pg, (g + 1) * pg)
        inv = 1.0 / jnp.minimum(win, t + 1).astype(F32)
        dbuf[:, cols] = (sums[g] * inv - u[:, cols]).astype(BF16)

    for r in range(1, SUBLANES):
        ashift[r - 1, 8:na, :] = abuf[8 - r:na - r, :]

    for k in range(conv_k):
        rowb[k] = jnp.broadcast_to(w["dw"][k:k + 1, :], (SUBLANES, cw))
    rowb[conv_k] = jnp.broadcast_to(w["b_dw"][...], (SUBLANES, cw))

    for c in range(tl // ROW_CHUNK):
        groups = []
        for o in range(ah + c * ROW_CHUNK, ah + (c + 1) * ROW_CHUNK, SUBLANES):
            yc = abuf[o:o + SUBLANES, :] * rowb[conv_k - 1]
            for back in range(1, conv_k):
                q, r = divmod(back, SUBLANES)
                src = abuf if r == 0 else ashift.at[r - 1]
                yc = yc + src[o - q * SUBLANES:o - (q - 1) * SUBLANES, :] * rowb[conv_k - 1 - back]
            groups.append(yc + rowb[conv_k])
        r0 = c * ROW_CHUNK
        cbuf[r0:r0 + ROW_CHUNK, :] = _layer_norm_silu(
            jnp.concatenate(groups, axis=0), w["ln_g"][...], w["ln_b"][...]).astype(BF16)

    @pl.when(l == pl.num_programs(1) - 1)
    def _():
        npool_ref[0] = ubuf[nu - pool_buf:nu, :]
        nconv_ref[0] = abuf[na - (conv_k - 1):na, :]

    _mixer_tail(x, dbuf[...], cbuf[...], (sh2, sc2, g1, g2), w, xmid_ref.at[0], h2_ref, lgt_ref)


def _sample_mixer_kernel(x_ref, mod_ref, sp_ref, sc_ref, *refs, d, pw, cw, conv_k):
    nw = len(_W_NAMES)
    w = dict(zip(_W_NAMES, refs[:nw]))
    xmid_ref, npool_ref, nconv_ref = refs[nw:nw + 3]
    tails = refs[nw + 3:-2]
    h2_tmp, lgt_tmp = refs[-2:]
    pool_buf = max(POOL_WINDOWS) - 1
    pg = pw // len(POOL_WINDOWS)
    x = x_ref[...]
    sh1, sc1, g1, sh2, sc2, g2 = [mod_ref[:, i * d:(i + 1) * d] for i in range(6)]
    h = _rms_norm(x, w["g_mix"][...]) * (1.0 + sc1) + sh1
    z = _dot(h.astype(BF16), w["in"][...])
    u = z[:, :pw]
    a = z[:, pw:pw + cw] * _sigmoid(z[:, pw + cw:])
    npool_ref[0:pool_buf - 1] = sp_ref[1:pool_buf]
    npool_ref[pool_buf - 1] = u
    nconv_ref[0:conv_k - 2] = sc_ref[1:conv_k - 1]
    nconv_ref[conv_k - 2] = a
    ds = []
    for g, win in enumerate(POOL_WINDOWS):
        cols = slice(g * pg, (g + 1) * pg)
        acc = u[:, cols]
        for j in range(1, win):
            acc = acc + sp_ref[pool_buf - j, :, cols]
        cnt = float(min(win, 1 + pool_buf))
        ds.append(acc / cnt - u[:, cols])
    pool_d = jnp.concatenate(ds, axis=-1).astype(BF16)
    yc = a * w["dw"][conv_k - 1:conv_k, :]
    for k in range(conv_k - 1):
        yc = yc + sc_ref[k] * w["dw"][k:k + 1, :]
    yc = yc + w["b_dw"][...]
    conv_act = _layer_norm_silu(yc, w["ln_g"][...], w["ln_b"][...]).astype(BF16)
    _mixer_tail(x, pool_d, conv_act, (sh2, sc2, g1, g2), w, xmid_ref, h2_tmp, lgt_tmp)
    n_parts = len(tails) // 2
    share = x.shape[0] // n_parts
    for p in range(n_parts):
        h2_ref, lgt_ref = tails[2 * p], tails[2 * p + 1]
        h2_ref[...] = jnp.zeros_like(h2_ref)
        lgt_ref[...] = jnp.zeros_like(lgt_ref)
        h2_ref[0:share, :] = h2_tmp[p * share:(p + 1) * share, :]
        lgt_ref[:, 0:share] = lgt_tmp[:, p * share:(p + 1) * share]


def _full_spec(a):
    nd = a.ndim
    return pl.BlockSpec(a.shape, lambda *_: (0,) * nd)


def _prompt_mixer(x, mod3, mod_row0, wl, h2_all, lgt_all, b0, b, w_experts, w_bf_prev, *, conv_k):
    n_batch, seq, d = x.shape
    w_gate, w_up, w_down = w_experts
    n_exp, _, de = w_gate.shape
    pw = wl["pool"].shape[0]
    cw = wl["dw"].shape[1]
    ne = wl["router_t"].shape[0]
    tl = SEQ_TILE
    nl = seq // tl
    pg = pw // len(POOL_WINDOWS)
    assert POOL_WINDOWS == (2, 4, 8, 16) and conv_k - 1 <= A_HALO
    ws = [wl[n] for n in _W_NAMES]
    kern = functools.partial(_prompt_mixer_kernel, tl=tl, d=d, pw=pw, cw=cw, conv_k=conv_k)
    ecs = n_exp // (n_batch * nl)
    assert ecs * n_batch * nl == n_exp

    def w_map(i, j):
        return ((i + b0) * nl + j, 0, 0)

    ins = [x, mod3, *ws, h2_all, lgt_all, w_gate, w_up, w_down]
    in_specs = ([pl.BlockSpec((1, tl, d), lambda i, j: (i + b0, j, 0)),
                 pl.BlockSpec((1, 1, mod3.shape[-1]), lambda i, j: (i + b0 + mod_row0, 0, 0))]
                + [_full_spec(a) for a in ws]
                + [pl.BlockSpec(memory_space=pl.ANY), pl.BlockSpec(memory_space=pl.ANY),
                   pl.BlockSpec((ecs, d, de), w_map), pl.BlockSpec((ecs, d, de), w_map),
                   pl.BlockSpec((ecs, de, d), w_map)])
    aliases = {2 + len(ws): 1, 3 + len(ws): 2}
    if w_bf_prev is not None:
        aliases.update({len(ins): 5, len(ins) + 1: 6})
        ins += list(w_bf_prev)
        in_specs += [pl.BlockSpec(memory_space=pl.ANY), pl.BlockSpec(memory_space=pl.ANY)]
    return pl.pallas_call(
        kern,
        grid=(b, nl),
        in_specs=in_specs,
        out_specs=[pl.BlockSpec((1, tl, d), lambda i, j: (i, j, 0)),
                   pl.BlockSpec((tl, d // 2), lambda i, j: (i * nl + j, 0)),
                   pl.BlockSpec((ne, tl), lambda i, j: (0, i * nl + j)),
                   pl.BlockSpec((1, max(POOL_WINDOWS) - 1, pw), lambda i, j: (i, 0, 0)),
                   pl.BlockSpec((1, conv_k - 1, cw), lambda i, j: (i, 0, 0)),
                   pl.BlockSpec((ecs, d, 2 * de), w_map), pl.BlockSpec((ecs, de, d), w_map)],
        out_shape=[jax.ShapeDtypeStruct((b, seq, d), F32),
                   jax.ShapeDtypeStruct(h2_all.shape, I32),
                   jax.ShapeDtypeStruct(lgt_all.shape, F32),
                   jax.ShapeDtypeStruct((b, max(POOL_WINDOWS) - 1, pw), F32),
                   jax.ShapeDtypeStruct((b, conv_k - 1, cw), F32),
                   jax.ShapeDtypeStruct((n_exp, d, 2 * de), BF16),
                   jax.ShapeDtypeStruct((n_exp, de, d), BF16)],
        scratch_shapes=[pltpu.VMEM((U_HALO + tl, pw), F32), pltpu.VMEM((U_HALO + tl, pw), F32),
                        pltpu.VMEM((U_HALO + tl, pw - pg), F32), pltpu.VMEM((U_HALO + tl, pw - 2 * pg), F32),
                        pltpu.VMEM((A_HALO + tl, cw), F32), pltpu.VMEM((SUBLANES - 1, A_HALO + tl, cw), F32),
                        pltpu.VMEM((tl, pw), BF16), pltpu.VMEM((tl, cw), BF16),
                        pltpu.VMEM((conv_k + 1, SUBLANES, cw), F32)],
        compiler_params=pltpu.CompilerParams(dimension_semantics=("arbitrary", "arbitrary"),
                                             vmem_limit_bytes=VMEM_LIMIT),
        input_output_aliases=aliases,
        name="prompt_mixer",
    )(*ins)


def _sample_mixer(x, mod, sp, sc, wl, t_prompt, t_pad, *, conv_k):
    rows, d = x.shape
    tail = t_pad - t_prompt
    assert t_prompt % tail == 0 and rows % N_PARTS == 0 and rows // N_PARTS <= tail
    pw = wl["pool"].shape[0]
    cw = wl["dw"].shape[1]
    ne = wl["router_t"].shape[0]
    ws = [wl[n] for n in _W_NAMES]
    kern = functools.partial(_sample_mixer_kernel, d=d, pw=pw, cw=cw, conv_k=conv_k)
    ins = [x, mod, sp, sc] + ws
    tail_specs = [pl.BlockSpec((tail, d // 2), lambda i: (t_prompt // tail, 0)),
                  pl.BlockSpec((ne, tail), lambda i: (0, t_prompt // tail))] * N_PARTS
    tail_shapes = [jax.ShapeDtypeStruct((t_pad, d // 2), I32), jax.ShapeDtypeStruct((ne, t_pad), F32)] * N_PARTS
    outs = pl.pallas_call(
        kern,
        grid=(1,),
        in_specs=[_full_spec(x), pl.BlockSpec((rows, mod.shape[1]), lambda i: (0, 0))]
                 + [_full_spec(a) for a in ins[2:]],
        out_specs=[pl.BlockSpec((rows, d), lambda i: (0, 0)), _full_spec(sp), _full_spec(sc)] + tail_specs,
        out_shape=[jax.ShapeDtypeStruct((rows, d), F32),
                   jax.ShapeDtypeStruct(sp.shape, F32),
                   jax.ShapeDtypeStruct(sc.shape, F32)] + tail_shapes,
        scratch_shapes=[pltpu.VMEM((rows, d // 2), I32), pltpu.VMEM((ne, rows), F32)],
        compiler_params=pltpu.CompilerParams(dimension_semantics=("arbitrary",),
                                             vmem_limit_bytes=VMEM_LIMIT),
        name="sample_mixer",
    )(*ins)
    return outs[0], outs[1], outs[2], [(outs[3 + 2 * p], outs[4 + 2 * p]) for p in range(N_PARTS)]


def _sublane_max(x):
    return jnp.max(x, axis=0, keepdims=True)


def _route_kernel(lg_ref, b_ref, tri_ref, idx_ref, wt_ref, rank_ref, cnt_ref, carry, *, t_valid):
    i = pl.program_id(0)
    ne, tt = lg_ref.shape
    ng = N_EXPERT_GROUPS
    per = ne // ng

    @pl.when(i == 0)
    def _():
        carry[...] = jnp.zeros_like(carry)

    s = _sigmoid(lg_ref[...])
    sel = s + b_ref[...]
    s3 = [s[p * ng:(p + 1) * ng, :] for p in range(per)]
    sel3 = [sel[p * ng:(p + 1) * ng, :] for p in range(per)]
    m1 = sel3[0]
    m2 = jnp.full_like(m1, NEG_INF)
    for p in range(1, per):
        m2 = jnp.maximum(m2, jnp.minimum(m1, sel3[p]))
        m1 = jnp.maximum(m1, sel3[p])
    gs = m1 + m2
    gi = lax.broadcasted_iota(jnp.int32, (ng, tt), 0)
    beaten = jnp.zeros((ng, tt), jnp.int32)
    for g in range(ng):
        row = gs[g:g + 1, :]
        beats = (row > gs) | ((row == gs) & (gi > g))
        beaten = beaten + beats.astype(jnp.int32)
    keep = beaten < TOPK_GROUPS
    cur = [jnp.where(keep, sel3[p], NEG_INF) for p in range(per)]
    eid = [(gi * per + p).astype(F32) for p in range(per)]
    idxs, wts, hits = [], [], []
    for _ in range(TOP_K):
        m = cur[0]
        for p in range(1, per):
            m = jnp.maximum(m, cur[p])
        m = _sublane_max(m)
        cand = jnp.where(cur[0] == m, eid[0], float(ne))
        for p in range(1, per):
            cand = jnp.minimum(cand, jnp.where(cur[p] == m, eid[p], float(ne)))
        e_sel = jnp.min(cand, axis=0, keepdims=True)
        hit = [eid[p] == e_sel for p in range(per)]
        wk = jnp.where(hit[0], s3[0], 0.0)
        for p in range(1, per):
            wk = wk + jnp.where(hit[p], s3[p], 0.0)
        wts.append(jnp.sum(wk, axis=0, keepdims=True))
        cur = [jnp.where(hit[p], NEG_INF, cur[p]) for p in range(per)]
        idxs.append(e_sel)
        hits.append(hit)
    wsum = wts[0]
    for k in range(1, TOP_K):
        wsum = wsum + wts[k]
    chosen = [functools.reduce(lambda a, b: a | b, [hits[k][p] for k in range(TOP_K)]) for p in range(per)]
    real = (i * tt + lax.broadcasted_iota(I32, (1, tt), 1)) < t_valid
    onehot = jnp.concatenate([(c & real).astype(F32) for c in chosen], axis=0)
    before = _dot(onehot.astype(BF16), tri_ref[...]) + carry[...]
    for k in range(TOP_K):
        rk = jnp.where(hits[k][0], before[0:ng, :], 0.0)
        for p in range(1, per):
            rk = rk + jnp.where(hits[k][p], before[p * ng:(p + 1) * ng, :], 0.0)
        rank_ref[k:k + 1, :] = jnp.sum(rk, axis=0, keepdims=True).astype(jnp.int32)
        idx_ref[k:k + 1, :] = idxs[k].astype(jnp.int32)
        wt_ref[k:k + 1, :] = wts[k] / wsum * ROUTED_SCALE
    carry[...] = carry[...] + jnp.sum(onehot, axis=1, keepdims=True)
    cnt_ref[...] = carry[...]


def _route(lgt, b_perm, t_valid):
    ne, t = lgt.shape
    tt = ROUTE_TILE
    tri = (jnp.arange(tt)[:, None] < jnp.arange(tt)[None, :]).astype(BF16)
    return pl.pallas_call(
        functools.partial(_route_kernel, t_valid=t_valid),
        grid=(t // tt,),
        in_specs=[pl.BlockSpec((ne, tt), lambda i: (0, i)),
                  pl.BlockSpec((ne, 1), lambda i: (0, 0)),
                  pl.BlockSpec((tt, tt), lambda i: (0, 0))],
        out_specs=[pl.BlockSpec((TOP_K, tt), lambda i: (0, i)),
                   pl.BlockSpec((TOP_K, tt), lambda i: (0, i)),
                   pl.BlockSpec((TOP_K, tt), lambda i: (0, i)),
                   pl.BlockSpec((ne, 1), lambda i: (0, 0))],
        out_shape=[jax.ShapeDtypeStruct((TOP_K, t), jnp.int32),
                   jax.ShapeDtypeStruct((TOP_K, t), F32),
                   jax.ShapeDtypeStruct((TOP_K, t), jnp.int32),
                   jax.ShapeDtypeStruct((ne, 1), F32)],
        scratch_shapes=[pltpu.VMEM((ne, 1), F32)],
        compiler_params=pltpu.CompilerParams(dimension_semantics=("arbitrary",),
                                             vmem_limit_bytes=VMEM_LIMIT),
        name="route",
    )(lgt, b_perm, tri)


def _dest_kernel(pstart_ref, idx_ref, rank_ref, dest_ref, *, t_valid, spare_row):
    idx = idx_ref[...]
    k_top, tile = idx.shape

    def add_start(e, acc):
        return acc + jnp.where(idx == e, pstart_ref[e], 0)

    dest = lax.fori_loop(0, pstart_ref.shape[0], add_start, rank_ref[...])
    tok = pl.program_id(0) * tile + lax.broadcasted_iota(I32, idx.shape, 1)
    spare = spare_row + (tok - t_valid) * k_top + lax.broadcasted_iota(I32, idx.shape, 0)
    dest_ref[...] = jnp.where(tok < t_valid, dest, spare)


def _dest_rows(pstart, idx, rank, t_valid, spare_row):
    k_top, t = idx.shape
    tile = DEST_TILE
    spec = pl.BlockSpec((k_top, tile), lambda i, ps: (0, i))
    return pl.pallas_call(
        functools.partial(_dest_kernel, t_valid=t_valid, spare_row=spare_row),
        grid_spec=pltpu.PrefetchScalarGridSpec(num_scalar_prefetch=1, grid=(t // tile,),
                                               in_specs=[spec, spec], out_specs=spec),
        out_shape=jax.ShapeDtypeStruct((k_top, t), I32),
        compiler_params=pltpu.CompilerParams(dimension_semantics=("arbitrary",),
                                             vmem_limit_bytes=VMEM_LIMIT),
        name="dest_rows",
    )(pstart, idx, rank)


def _expert_kernel(first_ref, cnt_ref, xs_hbm, wgu_ref, wd_ref, y_hbm, xbuf, ybuf, in_sem, out_sem):
    e = pl.program_id(0)
    ne = first_ref.shape[0] - 1
    nbuf, rows = xbuf.shape[:2]
    de = wd_ref.shape[1]
    b0, b1 = first_ref[e], first_ref[e + 1]
    n_total = first_ref[ne]

    def in_copy(g):
        return pltpu.make_async_copy(xs_hbm.at[pl.ds(g * rows, rows)], xbuf.at[g % nbuf], in_sem.at[g % nbuf])

    def out_copy(g):
        return pltpu.make_async_copy(ybuf.at[g % nbuf], y_hbm.at[pl.ds(g * rows, rows)], out_sem.at[g % nbuf])

    @pl.when(e == 0)
    def _():
        for g in range(nbuf - 1):
            @pl.when(g < n_total)
            def _():
                in_copy(g).start()

    def block(g, carry):
        slot = g % nbuf

        @pl.when(g + nbuf - 1 < n_total)
        def _():
            in_copy(g + nbuf - 1).start()

        in_copy(g).wait()

        @pl.when(g >= nbuf)
        def _():
            out_copy(g - nbuf).wait()

        valid = cnt_ref[e] - (g - b0) * rows
        rid = lax.broadcasted_iota(I32, (rows, 1), 0)
        xs = jnp.where(rid < valid, _unpack_bf16_pairs(xbuf[slot]), 0.0).astype(BF16)
        gu = _dot(xs, wgu_ref[0])
        hb = _silu(gu[:, :de]) * gu[:, de:]
        ybuf[slot] = _pack_bf16_pairs(_dot(hb.astype(BF16), wd_ref[0]))
        out_copy(g).start()
        return carry

    lax.fori_loop(b0, b1, block, 0)

    @pl.when(e == ne - 1)
    def _():
        for back in range(nbuf, 0, -1):
            @pl.when(n_total >= back)
            def _():
                out_copy(n_total - back).wait()


def _experts(xs, w_gu, w_down, first_block, counts):
    p, dw = xs.shape
    ne, de, d = w_down.shape
    return pl.pallas_call(
        _expert_kernel,
        grid_spec=pltpu.PrefetchScalarGridSpec(
            num_scalar_prefetch=2,
            grid=(ne,),
            in_specs=[pl.BlockSpec(memory_space=pl.ANY),
                      pl.BlockSpec((1, d, 2 * de), lambda e, fb, cn: (e, 0, 0)),
                      pl.BlockSpec((1, de, d), lambda e, fb, cn: (e, 0, 0))],
            out_specs=pl.BlockSpec(memory_space=pl.ANY),
            scratch_shapes=[pltpu.VMEM((EXPERT_BUFFERS, EXPERT_ROWS, dw), I32),
                            pltpu.VMEM((EXPERT_BUFFERS, EXPERT_ROWS, dw), I32),
                            pltpu.SemaphoreType.DMA((EXPERT_BUFFERS,)), pltpu.SemaphoreType.DMA((EXPERT_BUFFERS,))]),
        out_shape=jax.ShapeDtypeStruct((p, dw), I32),
        compiler_params=pltpu.CompilerParams(dimension_semantics=("arbitrary",),
                                             vmem_limit_bytes=VMEM_LIMIT),
        name="experts",
    )(first_block, counts, xs, w_gu, w_down)


def _sc_first_chunk(n_chunks):
    per_worker = n_chunks // SC_WORKERS
    assert per_worker * SC_WORKERS == n_chunks
    return per_worker, (lax.axis_index("s") * SC_CORES + lax.axis_index("c")) * per_worker


def _sc_start(copies):
    for cp in copies:
        cp.start()


def _sc_wait(copies):
    for cp in copies:
        cp.wait()


def _sc_dispatch(h2, dest, n_rows):
    w = h2.shape[1]
    n_chunks, k_top, ch = dest.shape
    mesh = plsc.VectorSubcoreMesh(core_axis_name="c", subcore_axis_name="s")

    @functools.partial(
        pl.kernel, mesh=mesh, out_type=jax.ShapeDtypeStruct((n_rows, w), I32),
        scratch_types=[pltpu.VMEM((2, k_top, ch), I32), pltpu.VMEM((2, ch, w), I32),
                       pltpu.SemaphoreType.DMA((2,)), pltpu.SemaphoreType.DMA((2,))])
    def dispatch(h2_hbm, dest_hbm, xs_hbm, idx_v, rows_v, load_sem, store_sem):
        per_worker, first = _sc_first_chunk(n_chunks)

        def loads(i):
            slot = i % 2
            return (pltpu.make_async_copy(h2_hbm.at[pl.ds((first + i) * ch, ch)], rows_v.at[slot], load_sem.at[slot]),
                    pltpu.make_async_copy(dest_hbm.at[first + i], idx_v.at[slot], load_sem.at[slot]))

        def scatters(i):
            slot = i % 2
            return [pltpu.make_async_copy(rows_v.at[slot], xs_hbm.at[idx_v.at[slot, k]], store_sem.at[slot])
                    for k in range(k_top)]

        for i in range(min(2, per_worker)):
            _sc_start(loads(i))
        for i in range(per_worker):
            _sc_wait(loads(i))
            _sc_start(scatters(i))
            if 1 <= i < per_worker - 1:
                _sc_wait(scatters(i - 1))
                _sc_start(loads(i + 1))
        for i in range(max(per_worker - 2, 0), per_worker):
            _sc_wait(scatters(i))

    return dispatch(h2, dest)


def _sc_combine_gather(y, dest, t):
    w = y.shape[1]
    n_chunks, k_top, ch = dest.shape
    n_local = n_chunks // SC_WORKERS
    mesh = plsc.VectorSubcoreMesh(core_axis_name="c", subcore_axis_name="s")

    @functools.partial(
        pl.kernel, mesh=mesh, out_type=jax.ShapeDtypeStruct((k_top, t, w), I32),
        scratch_types=[pltpu.VMEM((n_local, k_top, ch), I32), pltpu.VMEM((k_top, ch, w), I32),
                       pltpu.SemaphoreType.DMA((k_top,)), pltpu.SemaphoreType.DMA((k_top,))])
    def gather(y_hbm, dest_hbm, out_hbm, idx_v, rows_v, load_sem, store_sem):
        per_worker, first = _sc_first_chunk(n_chunks)
        pltpu.sync_copy(dest_hbm.at[pl.ds(first, per_worker)], idx_v)

        def fetch(i, k):
            return pltpu.make_async_copy(y_hbm.at[idx_v.at[i, k]], rows_v.at[k], load_sem.at[k])

        def put(i, k):
            return pltpu.make_async_copy(rows_v.at[k], out_hbm.at[k, pl.ds((first + i) * ch, ch)], store_sem.at[k])

        @pl.loop(0, per_worker)
        def _(i):
            for k in range(k_top):
                @pl.when(i > 0)
                def _():
                    put(i - 1, k).wait()

                fetch(i, k).start()
            for k in range(k_top):
                fetch(i, k).wait()
                put(i, k).start()

        for k in range(k_top):
            put(per_worker - 1, k).wait()

    return gather(y, dest)


def _final_kernel(xmid_ref, yg_ref, wt_ref, g2_ref, shf_ref, scf_ref, gf_ref, *rest):
    o_ref = rest[-1]
    wt = wt_ref[...].T[:xmid_ref.shape[0]]
    routed = wt[:, 0:1] * _unpack_bf16_pairs(yg_ref[0])
    for k in range(1, yg_ref.shape[0]):
        routed = routed + wt[:, k:k + 1] * _unpack_bf16_pairs(yg_ref[k])
    x2 = xmid_ref[...] + g2_ref[...] * routed
    o_ref[...] = _rms_norm(x2, gf_ref[...]) * (1.0 + scf_ref[...]) + shf_ref[...]


def _final_prompt(xmid, yg, wts_t, mod3, modf3, mod_row0, g_final, b0, n_batch, out_prev):
    b, seq, d = xmid.shape
    k_top, _, w = yg.shape
    tl = FINAL_TILE
    nl = seq // tl

    def vec(col):
        return pl.BlockSpec((None, 1, d), lambda i, j: (i + b0 + mod_row0, 0, col))

    in_specs = [pl.BlockSpec((None, tl, d), lambda i, j: (i, j, 0)),
                pl.BlockSpec((k_top, tl, w), lambda i, j: (0, i * nl + j, 0)),
                pl.BlockSpec((k_top, tl), lambda i, j: (0, i * nl + j)),
                vec(5), vec(0), vec(1), pl.BlockSpec((1, d), lambda i, j: (0, 0))]
    args = [xmid, yg, wts_t, mod3, modf3, modf3, g_final]
    aliases = {}
    if out_prev is not None:
        in_specs.append(pl.BlockSpec(memory_space=pl.ANY))
        args.append(out_prev)
        aliases = {7: 0}
    return pl.pallas_call(
        _final_kernel,
        grid=(b, nl),
        in_specs=in_specs,
        out_specs=pl.BlockSpec((None, tl, d), lambda i, j: (i + b0, j, 0)),
        out_shape=jax.ShapeDtypeStruct((n_batch, seq, d), F32),
        compiler_params=pltpu.CompilerParams(dimension_semantics=("arbitrary", "arbitrary"),
                                             vmem_limit_bytes=VMEM_LIMIT),
        input_output_aliases=aliases,
        name="final_prompt",
    )(*args)


def _final_sample(xmid_all, yg, wts_t, mod, modf, g_final, part, rows, first_row, after):
    d = xmid_all.shape[1]
    k_top, _, w = yg.shape

    def vec(col):
        return pl.BlockSpec((rows, d), lambda i: (part, col))

    return pl.pallas_call(
        _final_kernel,
        grid=(1,),
        in_specs=[vec(0),
                  pl.BlockSpec((k_top, rows, w), lambda i: (0, first_row // rows, 0)),
                  pl.BlockSpec((k_top, 128), lambda i: (0, first_row // 128)),
                  vec(5), vec(0), vec(1), pl.BlockSpec((1, d), lambda i: (0, 0)), pl.BlockSpec(memory_space=pl.ANY)],
        out_specs=pl.BlockSpec((rows, d), lambda i: (0, 0)),
        out_shape=jax.ShapeDtypeStruct((rows, d), F32),
        compiler_params=pltpu.CompilerParams(dimension_semantics=("arbitrary",),
                                             vmem_limit_bytes=VMEM_LIMIT),
        name="final_sample",
    )(xmid_all, yg, wts_t, mod, modf, modf, g_final, after)


def _block_diag_pairs(w_pool):
    g, c, _ = w_pool.shape
    eye = jnp.eye(g, dtype=w_pool.dtype)
    return (eye[:, None, :, None] * w_pool[:, :, None, :]).reshape(g * c, g * c)


def kernel(x_prompt, x_sample, state_pool, state_conv, c_prompt, c_sample, w_ada, b_ada, g_mix, w_in, w_pool, pool_scale, w_dw, b_dw, ln_g, ln_b, w_out, g_ffn, w_router, b_router, w_gate, w_up, w_down, ws_gate, ws_up, ws_down, w_ada_final, b_ada_final, g_final):
    bp, seq, d = x_prompt.shape
    bs = x_sample.shape[0]
    depth = w_ada.shape[0]
    assert depth == 1 and x_sample.shape[1] == 1
    conv_k = w_dw.shape[1]
    ne = w_router.shape[-1]
    per = ne // N_EXPERT_GROUPS
    tp = bp * seq
    t_all = tp + bs

    row_expert = (jnp.arange(ne) % N_EXPERT_GROUPS) * per + jnp.arange(ne) // N_EXPERT_GROUPS
    wl = {
        "g_mix": g_mix[0][None, :],
        "in": w_in[0].astype(BF16),
        "pool": _block_diag_pairs(w_pool[0]).astype(BF16),
        "pool_scale": pool_scale[0][None, :],
        "dw": w_dw[0],
        "b_dw": b_dw[0][None, :],
        "ln_g": ln_g[0][None, :],
        "ln_b": ln_b[0][None, :],
        "out": w_out[0].astype(BF16),
        "g_ffn": g_ffn[0][None, :],
        "router_t": w_router[0].T[row_expert].astype(BF16),
        "s_gu": jnp.concatenate([ws_gate[0], ws_up[0]], axis=1).astype(BF16),
        "s_down": ws_down[0].astype(BF16),
    }
    b_perm = b_router[0][row_expert][:, None]

    assert bp % N_PARTS == 0 and bs % N_PARTS == 0
    bpp, bsp = bp // N_PARTS, bs // N_PARTS
    tpp = bpp * seq
    t_part = tpp + bsp
    grain = SC_WORKERS * SC_DISPATCH_CHUNK * SC_COMBINE_CHUNK // math.gcd(SC_DISPATCH_CHUNK, SC_COMBINE_CHUNK)
    t_pad = (t_part + grain - 1) // grain * grain
    assert t_pad % ROUTE_TILE == 0 and t_pad % DEST_TILE == 0 and tpp % bsp == 0

    c_all = jnp.concatenate([c_sample, c_prompt], axis=0)
    mod = _ada(c_all, w_ada[0], b_ada[0])
    modf = _ada(c_all, w_ada_final, b_ada_final)
    mod3, modf3 = mod[:, None, :], modf[:, None, :]

    xmid_s, new_pool_t, new_conv_t, tails = _sample_mixer(
        x_sample.reshape(bs, d), mod, jnp.transpose(state_pool[0], (1, 0, 2)),
        jnp.transpose(state_conv[0], (1, 0, 2)), wl, tpp, t_pad, conv_k=conv_k)
    new_pool_s = jnp.transpose(new_pool_t, (1, 0, 2))
    new_conv_s = jnp.transpose(new_conv_t, (1, 0, 2))
    w_experts = tuple(a.reshape(a.shape[1:]) for a in (w_gate, w_up, w_down))

    n_blocks = (t_part * TOP_K + EXPERT_ROWS - 1) // EXPERT_ROWS + ne
    n_spare = -(-(t_pad - t_part) * TOP_K // EXPERT_ROWS)
    n_rows = (n_blocks + n_spare) * EXPERT_ROWS

    y_prompt, y_samples, npools, nconvs, w_bf = None, [], [], [], None
    mixed = []
    for p in range(N_PARTS):
        h2, lgt = tails[p]
        xmid_p, h2, lgt, npool_p, nconv_p, *w_bf = _prompt_mixer(
            x_prompt, mod3, bs, wl, h2, lgt, p * bpp, bpp, w_experts, w_bf, conv_k=conv_k)
        mixed.append((xmid_p, h2, lgt))
        npools.append(npool_p)
        nconvs.append(nconv_p)
    for p in range(N_PARTS):
        xmid_p, h2, lgt = mixed[p]
        idx, wts, rank, counts_perm = _route(lgt, b_perm, t_part)
        counts = jnp.zeros((ne,), I32).at[row_expert].set(counts_perm[:, 0].astype(I32))
        nblk = (counts + EXPERT_ROWS - 1) // EXPERT_ROWS
        first_block = jnp.concatenate([jnp.zeros((1,), I32), jnp.cumsum(nblk).astype(I32)])
        dest = _dest_rows(first_block[:ne] * EXPERT_ROWS, idx, rank, t_part, n_blocks * EXPERT_ROWS)

        def chunked(ch):
            return dest.reshape(TOP_K, t_pad // ch, ch).transpose(1, 0, 2)

        xs = _sc_dispatch(h2, chunked(SC_DISPATCH_CHUNK), n_rows)
        y = _experts(xs, w_bf[0], w_bf[1], first_block, counts)
        yg = _sc_combine_gather(y, chunked(SC_COMBINE_CHUNK), t_pad)

        y_samples.append(_final_sample(xmid_s, yg, wts, mod, modf, g_final[None, :], p, bsp, tpp,
                                       g_final if y_prompt is None else y_prompt))
        y_prompt = _final_prompt(xmid_p, yg, wts, mod3, modf3, bs, g_final[None, :], p * bpp, bp, y_prompt)
    y_sample = jnp.concatenate(y_samples, axis=0)
    npool_p = jnp.concatenate(npools, axis=0)
    nconv_p = jnp.concatenate(nconvs, axis=0)

    return (y_prompt, y_sample[:, None, :], npool_p[None], nconv_p[None], new_pool_s[None], new_conv_s[None])
```

```python
import functools
import math

import jax
import jax.numpy as jnp
from jax import lax
from jax.experimental import pallas as pl
from jax.experimental.pallas import tpu as pltpu
from jax.experimental.pallas import tpu_sc as plsc

POOL_WINDOWS = (2, 4, 8, 16)
N_EXPERT_GROUPS = 8
TOPK_GROUPS = 4
TOP_K = 8
ROUTED_SCALE = 2.5
EPS = 1e-6

LANES = 128
SUBLANES = 8
BF16_ROWS = 16
VMEM_LIMIT = 56 * 1024 * 1024

SEQ_TILE = 512
ROW_CHUNK = 32
MIXER_SUB_TILES = 2
ROUTE_TILE = 384
EXPERT_ROWS = 512
N_PARTS = 2
EXPERT_BUFFERS = 6
FINAL_TILE = 512
DEST_TILE = 2816

SC_CORES = 2
SC_SUBCORES = 16
SC_WORKERS = SC_CORES * SC_SUBCORES
SC_DISPATCH_CHUNK = 88
SC_COMBINE_CHUNK = 24

F32 = jnp.float32
BF16 = jnp.bfloat16
I32 = jnp.int32
U32 = jnp.uint32
NEG_INF = float("-inf")
HI16 = 0xFFFF0000


def _sigmoid(x):
    return 1.0 / (1.0 + jnp.exp(-x))


def _silu(x):
    return x * _sigmoid(x)


def _rms_norm(x, g):
    return x * lax.rsqrt(jnp.mean(x * x, axis=-1, keepdims=True) + EPS) * g


def _dot(a, b):
    return jnp.dot(a, b, preferred_element_type=F32)


def _pack_bf16_pairs(x):
    w = x.shape[1] // 2
    bits = lax.bitcast_convert_type(x.astype(BF16).astype(F32), U32)
    return lax.bitcast_convert_type((bits[:, :w] >> 16) | (bits[:, w:] & U32(HI16)), I32)


def _unpack_bf16_pairs(p):
    bits = lax.bitcast_convert_type(p, U32)
    lo = lax.bitcast_convert_type(bits << 16, F32)
    hi = lax.bitcast_convert_type(bits & U32(HI16), F32)
    return jnp.concatenate([lo, hi], axis=1)


def _ada_kernel(c_ref, w_ref, b_ref, o_ref):
    s = _silu(c_ref[...])
    o_ref[...] = _dot(s.astype(BF16), w_ref[...].astype(BF16)) + b_ref[...]


def _ada(c, w, b):
    rows, d = c.shape
    n = w.shape[1]
    tn = 1024
    return pl.pallas_call(
        _ada_kernel,
        grid=(n // tn,),
        in_specs=[pl.BlockSpec((rows, d), lambda j: (0, 0)),
                  pl.BlockSpec((d, tn), lambda j: (0, j)),
                  pl.BlockSpec((1, tn), lambda j: (0, j))],
        out_specs=pl.BlockSpec((rows, tn), lambda j: (0, j)),
        out_shape=jax.ShapeDtypeStruct((rows, n), F32),
        compiler_params=pltpu.CompilerParams(dimension_semantics=("arbitrary",),
                                             vmem_limit_bytes=VMEM_LIMIT),
        name="ada",
    )(c, w, b.reshape(1, n))


def _mixer_tail(x, pool_d, conv_act, mod, w, xmid_ref, h2_ref, lgt_ref):
    sh2, sc2, g1, g2 = mod
    pw = pool_d.shape[1]
    pool_out = _dot(pool_d, w["pool"][...]) * w["pool_scale"][...]
    mix = _dot(pool_out.astype(BF16), w["out"][:pw, :]) + _dot(conv_act, w["out"][pw:, :])
    x1 = x + g1 * mix
    h2f = _rms_norm(x1, w["g_ffn"][...]) * (1.0 + sc2) + sh2
    h2_ref[...] = _pack_bf16_pairs(h2f)
    h2 = h2f.astype(BF16)
    lgt_ref[...] = lax.dot_general(w["router_t"][...], h2, (((1,), (1,)), ((), ())),
                                   preferred_element_type=F32)
    gu = _dot(h2, w["s_gu"][...])
    de = gu.shape[1] // 2
    hs = _silu(gu[:, :de]) * gu[:, de:]
    shared = _dot(hs.astype(BF16), w["s_down"][...])
    xmid_ref[...] = x1 + g2 * shared


_W_NAMES = ("g_mix", "in", "pool", "pool_scale", "dw", "b_dw", "ln_g", "ln_b", "out", "g_ffn",
            "router_t", "s_gu", "s_down")


def _layer_norm_silu(yc, g, b):
    mu = jnp.mean(yc, axis=-1, keepdims=True)
    yz = yc - mu
    var = jnp.mean(yz * yz, axis=-1, keepdims=True)
    return _silu(yz * lax.rsqrt(var + EPS) * g + b)


U_HALO = 32
A_HALO = 32


def _prompt_mixer_kernel(x_ref, mod_ref, *refs, tl, d, pw, cw, conv_k, cast):
    nw = len(_W_NAMES)
    w = dict(zip(_W_NAMES, refs[:nw]))
    ubuf, s2buf, s4buf, s8buf, abuf, ashift, dbuf, cbuf, rowb = refs[-9:]
    outs = refs[-9 - (7 if cast else 5):-9]
    xmid_ref, h2_ref, lgt_ref, npool_ref, nconv_ref = outs[:5]
    if cast:
        wg_ref, wu_ref, wd_ref = refs[nw + 2:nw + 5]
        wgu_bf_ref, wd_bf_ref = outs[5:]
        de = wg_ref.shape[2]
        wgu_bf_ref[:, :, :de] = wg_ref[...].astype(BF16)
        wgu_bf_ref[:, :, de:] = wu_ref[...].astype(BF16)
        wd_bf_ref[...] = wd_ref[...].astype(BF16)
    l = pl.program_id(1)
    pool_buf = max(POOL_WINDOWS) - 1
    uh, ah = U_HALO, A_HALO
    pg = pw // len(POOL_WINDOWS)
    nu, na = uh + tl, ah + tl

    @pl.when(l == 0)
    def _():
        ubuf[0:uh, :] = jnp.zeros((uh, pw), F32)
        abuf[0:ah, :] = jnp.zeros((ah, cw), F32)

    @pl.when(l > 0)
    def _():
        ubuf[0:uh, :] = ubuf[tl:tl + uh, :]
        abuf[0:ah, :] = abuf[tl:tl + ah, :]

    sh1 = mod_ref[0, :, 0 * d:1 * d]
    sc1 = mod_ref[0, :, 1 * d:2 * d]
    g1 = mod_ref[0, :, 2 * d:3 * d]
    sh2 = mod_ref[0, :, 3 * d:4 * d]
    sc2 = mod_ref[0, :, 4 * d:5 * d]
    g2 = mod_ref[0, :, 5 * d:6 * d]

    for k in range(conv_k):
        rowb[k] = jnp.broadcast_to(w["dw"][k:k + 1, :], (SUBLANES, cw))
    rowb[conv_k] = jnp.broadcast_to(w["b_dw"][...], (SUBLANES, cw))

    def stage_in(r0, r1):
        h = _rms_norm(x_ref[0, r0:r1, :], w["g_mix"][...]) * (1.0 + sc1) + sh1
        z = _dot(h.astype(BF16), w["in"][...])
        ubuf[uh + r0:uh + r1, :] = z[:, :pw]
        abuf[ah + r0:ah + r1, :] = z[:, pw:pw + cw] * _sigmoid(z[:, pw + cw:])

    def stage_mix(r0, r1):
        first = r0 == 0
        ue = uh + r1
        b2, b4, b8 = (8, 16, 24) if first else (uh + r0,) * 3
        s2buf[b2:ue, :] = ubuf[b2:ue, :] + ubuf[b2 - 1:ue - 1, :]
        s4buf[b4:ue, :] = s2buf[b4:ue, pg:] + s2buf[b4 - 2:ue - 2, pg:]
        s8buf[b8:ue, :] = s4buf[b8:ue, pg:] + s4buf[b8 - 4:ue - 4, pg:]
        us = uh + r0
        t = l * tl + r0 + lax.broadcasted_iota(I32, (r1 - r0, 1), 0)
        sums = (s2buf[us:ue, 0:pg], s4buf[us:ue, 0:pg], s8buf[us:ue, 0:pg],
                s8buf[us:ue, pg:2 * pg] + s8buf[us - 8:ue - 8, pg:2 * pg])
        for g, win in enumerate(POOL_WINDOWS):
            cols = slice(g * pg, (g + 1) * pg)
            inv = 1.0 / jnp.minimum(win, t + 1).astype(F32)
            dbuf[r0:r1, cols] = (sums[g] * inv - ubuf[us:ue, cols]).astype(BF16)

        ab, ae = (8 if first else ah + r0), ah + r1
        for r in range(1, SUBLANES):
            ashift[r - 1, ab:ae, :] = abuf[ab - r:ae - r, :]

        for c in range(r0 // ROW_CHUNK, r1 // ROW_CHUNK):
            starts = range(ah + c * ROW_CHUNK, ah + (c + 1) * ROW_CHUNK, SUBLANES)
            groups = [abuf[o:o + SUBLANES, :] * rowb[conv_k - 1] for o in starts]
            for back in range(1, conv_k):
                q, r = divmod(back, SUBLANES)
                src = abuf if r == 0 else ashift.at[r - 1]
                tap = rowb[conv_k - 1 - back]
                groups = [yc + src[o - q * SUBLANES:o - (q - 1) * SUBLANES, :] * tap for yc, o in zip(groups, starts)]
            groups = [yc + rowb[conv_k] for yc in groups]
            c0 = c * ROW_CHUNK
            cbuf[c0:c0 + ROW_CHUNK, :] = _layer_norm_silu(
                jnp.concatenate(groups, axis=0), w["ln_g"][...], w["ln_b"][...]).astype(BF16)

    def stage_out(r0, r1):
        n = r1 - r0
        _mixer_tail(x_ref[0, r0:r1, :], dbuf[r0:r1, :], cbuf[r0:r1, :], (sh2, sc2, g1, g2), w,
                    xmid_ref.at[0, pl.ds(r0, n), :], h2_ref.at[pl.ds(r0, n), :], lgt_ref.at[:, pl.ds(r0, n)])

    sub = tl // MIXER_SUB_TILES
    ranges = [(i * sub, (i + 1) * sub) for i in range(MIXER_SUB_TILES)]
    stage_in(*ranges[0])
    for i, rng in enumerate(ranges):
        if i + 1 < len(ranges):
            stage_in(*ranges[i + 1])
        stage_mix(*rng)
        stage_out(*rng)

    @pl.when(l == pl.num_programs(1) - 1)
    def _():
        npool_ref[0] = ubuf[nu - pool_buf:nu, :]
        nconv_ref[0] = abuf[na - (conv_k - 1):na, :]


def _sample_mixer_kernel(x_ref, mod_ref, sp_ref, sc_ref, *refs, d, pw, cw, conv_k):
    nw = len(_W_NAMES)
    w = dict(zip(_W_NAMES, refs[:nw]))
    xmid_ref, npool_ref, nconv_ref = refs[nw:nw + 3]
    tails = refs[nw + 3:-2]
    h2_tmp, lgt_tmp = refs[-2:]
    pool_buf = max(POOL_WINDOWS) - 1
    pg = pw // len(POOL_WINDOWS)
    x = x_ref[...]
    sh1, sc1, g1, sh2, sc2, g2 = [mod_ref[:, i * d:(i + 1) * d] for i in range(6)]
    h = _rms_norm(x, w["g_mix"][...]) * (1.0 + sc1) + sh1
    z = _dot(h.astype(BF16), w["in"][...])
    u = z[:, :pw]
    a = z[:, pw:pw + cw] * _sigmoid(z[:, pw + cw:])
    npool_ref[0:pool_buf - 1] = sp_ref[1:pool_buf]
    npool_ref[pool_buf - 1] = u
    nconv_ref[0:conv_k - 2] = sc_ref[1:conv_k - 1]
    nconv_ref[conv_k - 2] = a
    ds = []
    for g, win in enumerate(POOL_WINDOWS):
        cols = slice(g * pg, (g + 1) * pg)
        acc = u[:, cols]
        for j in range(1, win):
            acc = acc + sp_ref[pool_buf - j, :, cols]
        cnt = float(min(win, 1 + pool_buf))
        ds.append(acc / cnt - u[:, cols])
    pool_d = jnp.concatenate(ds, axis=-1).astype(BF16)
    yc = a * w["dw"][conv_k - 1:conv_k, :]
    for k in range(conv_k - 1):
        yc = yc + sc_ref[k] * w["dw"][k:k + 1, :]
    yc = yc + w["b_dw"][...]
    conv_act = _layer_norm_silu(yc, w["ln_g"][...], w["ln_b"][...]).astype(BF16)
    _mixer_tail(x, pool_d, conv_act, (sh2, sc2, g1, g2), w, xmid_ref, h2_tmp, lgt_tmp)
    n_parts = len(tails) // 2
    share = x.shape[0] // n_parts
    for p in range(n_parts):
        h2_ref, lgt_ref = tails[2 * p], tails[2 * p + 1]
        h2_ref[...] = jnp.zeros_like(h2_ref)
        lgt_ref[...] = jnp.zeros_like(lgt_ref)
        h2_ref[0:share, :] = h2_tmp[p * share:(p + 1) * share, :]
        lgt_ref[:, 0:share] = lgt_tmp[:, p * share:(p + 1) * share]


def _full_spec(a):
    nd = a.ndim
    return pl.BlockSpec(a.shape, lambda *_: (0,) * nd)


def _prompt_mixer(x, mod3, mod_row0, wl, h2_all, lgt_all, b0, b, tok_b0, w_experts, cast, w_bf_prev, after, *,
                  conv_k):
    n_batch, seq, d = x.shape
    w_gate, w_up, w_down = w_experts
    n_exp, _, de = w_gate.shape
    pw = wl["pool"].shape[0]
    cw = wl["dw"].shape[1]
    ne = wl["router_t"].shape[0]
    tl = SEQ_TILE
    nl = seq // tl
    pg = pw // len(POOL_WINDOWS)
    assert POOL_WINDOWS == (2, 4, 8, 16) and conv_k - 1 <= A_HALO
    ws = [wl[n] for n in _W_NAMES]
    kern = functools.partial(_prompt_mixer_kernel, tl=tl, d=d, pw=pw, cw=cw, conv_k=conv_k, cast=cast is not None)

    def tok_map(i, j):
        return ((i + b0 - tok_b0) * nl + j, 0)

    ins = [x, mod3, *ws, h2_all, lgt_all]
    in_specs = ([pl.BlockSpec((1, tl, d), lambda i, j: (i + b0, j, 0)),
                 pl.BlockSpec((1, 1, mod3.shape[-1]), lambda i, j: (i + b0 + mod_row0, 0, 0))]
                + [_full_spec(a) for a in ws]
                + [pl.BlockSpec(memory_space=pl.ANY), pl.BlockSpec(memory_space=pl.ANY)])
    aliases = {2 + len(ws): 1, 3 + len(ws): 2}
    out_specs = [pl.BlockSpec((1, tl, d), lambda i, j: (i, j, 0)),
                 pl.BlockSpec((tl, d // 2), tok_map),
                 pl.BlockSpec((ne, tl), lambda i, j: tok_map(i, j)[::-1]),
                 pl.BlockSpec((1, max(POOL_WINDOWS) - 1, pw), lambda i, j: (i, 0, 0)),
                 pl.BlockSpec((1, conv_k - 1, cw), lambda i, j: (i, 0, 0))]
    out_shape = [jax.ShapeDtypeStruct((b, seq, d), F32),
                 jax.ShapeDtypeStruct(h2_all.shape, I32),
                 jax.ShapeDtypeStruct(lgt_all.shape, F32),
                 jax.ShapeDtypeStruct((b, max(POOL_WINDOWS) - 1, pw), F32),
                 jax.ShapeDtypeStruct((b, conv_k - 1, cw), F32)]
    if cast is not None:
        e0, ecs = cast
        assert e0 % ecs == 0 and e0 + b * nl * ecs <= n_exp

        def w_map(i, j):
            return (e0 // ecs + i * nl + j, 0, 0)

        ins += [w_gate, w_up, w_down]
        in_specs += [pl.BlockSpec((ecs, d, de), w_map), pl.BlockSpec((ecs, d, de), w_map),
                     pl.BlockSpec((ecs, de, d), w_map)]
        out_specs += [pl.BlockSpec((ecs, d, 2 * de), w_map), pl.BlockSpec((ecs, de, d), w_map)]
        out_shape += [jax.ShapeDtypeStruct((n_exp, d, 2 * de), BF16), jax.ShapeDtypeStruct((n_exp, de, d), BF16)]
        if w_bf_prev is not None:
            aliases.update({len(ins): 5, len(ins) + 1: 6})
            ins += list(w_bf_prev)
            in_specs += [pl.BlockSpec(memory_space=pl.ANY), pl.BlockSpec(memory_space=pl.ANY)]
    if after is not None:
        ins.append(after)
        in_specs.append(pl.BlockSpec(memory_space=pl.ANY))
    return pl.pallas_call(
        kern,
        grid=(b, nl),
        in_specs=in_specs,
        out_specs=out_specs,
        out_shape=out_shape,
        scratch_shapes=[pltpu.VMEM((U_HALO + tl, pw), F32), pltpu.VMEM((U_HALO + tl, pw), F32),
                        pltpu.VMEM((U_HALO + tl, pw - pg), F32), pltpu.VMEM((U_HALO + tl, pw - 2 * pg), F32),
                        pltpu.VMEM((A_HALO + tl, cw), F32), pltpu.VMEM((SUBLANES - 1, A_HALO + tl, cw), F32),
                        pltpu.VMEM((tl, pw), BF16), pltpu.VMEM((tl, cw), BF16),
                        pltpu.VMEM((conv_k + 1, SUBLANES, cw), F32)],
        compiler_params=pltpu.CompilerParams(dimension_semantics=("arbitrary", "arbitrary"),
                                             vmem_limit_bytes=VMEM_LIMIT),
        input_output_aliases=aliases,
        name="prompt_mixer",
    )(*ins)


def _sample_mixer(x, mod, sp, sc, wl, t_prompt, t_pad, *, conv_k):
    rows, d = x.shape
    tail = t_pad - t_prompt
    assert t_prompt % tail == 0 and rows % N_PARTS == 0 and rows // N_PARTS <= tail
    pw = wl["pool"].shape[0]
    cw = wl["dw"].shape[1]
    ne = wl["router_t"].shape[0]
    ws = [wl[n] for n in _W_NAMES]
    kern = functools.partial(_sample_mixer_kernel, d=d, pw=pw, cw=cw, conv_k=conv_k)
    ins = [x, mod, sp, sc] + ws
    tail_specs = [pl.BlockSpec((tail, d // 2), lambda i: (t_prompt // tail, 0)),
                  pl.BlockSpec((ne, tail), lambda i: (0, t_prompt // tail))] * N_PARTS
    tail_shapes = [jax.ShapeDtypeStruct((t_pad, d // 2), I32), jax.ShapeDtypeStruct((ne, t_pad), F32)] * N_PARTS
    outs = pl.pallas_call(
        kern,
        grid=(1,),
        in_specs=[_full_spec(x), pl.BlockSpec((rows, mod.shape[1]), lambda i: (0, 0))]
                 + [_full_spec(a) for a in ins[2:]],
        out_specs=[pl.BlockSpec((rows, d), lambda i: (0, 0)), _full_spec(sp), _full_spec(sc)] + tail_specs,
        out_shape=[jax.ShapeDtypeStruct((rows, d), F32),
                   jax.ShapeDtypeStruct(sp.shape, F32),
                   jax.ShapeDtypeStruct(sc.shape, F32)] + tail_shapes,
        scratch_shapes=[pltpu.VMEM((rows, d // 2), I32), pltpu.VMEM((ne, rows), F32)],
        compiler_params=pltpu.CompilerParams(dimension_semantics=("arbitrary",),
                                             vmem_limit_bytes=VMEM_LIMIT),
        name="sample_mixer",
    )(*ins)
    return outs[0], outs[1], outs[2], [(outs[3 + 2 * p], outs[4 + 2 * p]) for p in range(N_PARTS)]


def _sublane_max(x):
    return jnp.max(x, axis=0, keepdims=True)


def _route_kernel(lg_ref, b_ref, tri_ref, idx_ref, wt_ref, rank_ref, cnt_ref, carry, *, t_valid):
    i = pl.program_id(0)
    ne, tt = lg_ref.shape
    ng = N_EXPERT_GROUPS
    per = ne // ng

    @pl.when(i == 0)
    def _():
        carry[...] = jnp.zeros_like(carry)

    s = _sigmoid(lg_ref[...])
    sel = s + b_ref[...]
    s3 = [s[p * ng:(p + 1) * ng, :] for p in range(per)]
    sel3 = [sel[p * ng:(p + 1) * ng, :] for p in range(per)]
    m1 = sel3[0]
    m2 = jnp.full_like(m1, NEG_INF)
    for p in range(1, per):
        m2 = jnp.maximum(m2, jnp.minimum(m1, sel3[p]))
        m1 = jnp.maximum(m1, sel3[p])
    gs = m1 + m2
    gi = lax.broadcasted_iota(jnp.int32, (ng, tt), 0)
    beaten = jnp.zeros((ng, tt), jnp.int32)
    for g in range(ng):
        row = gs[g:g + 1, :]
        beats = (row > gs) | ((row == gs) & (gi > g))
        beaten = beaten + beats.astype(jnp.int32)
    keep = beaten < TOPK_GROUPS
    cur = [jnp.where(keep, sel3[p], NEG_INF) for p in range(per)]
    eid = [(gi * per + p).astype(F32) for p in range(per)]
    idxs, wts, hits = [], [], []
    for _ in range(TOP_K):
        m = cur[0]
        for p in range(1, per):
            m = jnp.maximum(m, cur[p])
        m = _sublane_max(m)
        cand = jnp.where(cur[0] == m, eid[0], float(ne))
        for p in range(1, per):
            cand = jnp.minimum(cand, jnp.where(cur[p] == m, eid[p], float(ne)))
        e_sel = jnp.min(cand, axis=0, keepdims=True)
        hit = [eid[p] == e_sel for p in range(per)]
        wk = jnp.where(hit[0], s3[0], 0.0)
        for p in range(1, per):
            wk = wk + jnp.where(hit[p], s3[p], 0.0)
        wts.append(jnp.sum(wk, axis=0, keepdims=True))
        cur = [jnp.where(hit[p], NEG_INF, cur[p]) for p in range(per)]
        idxs.append(e_sel)
        hits.append(hit)
    wsum = wts[0]
    for k in range(1, TOP_K):
        wsum = wsum + wts[k]
    chosen = [functools.reduce(lambda a, b: a | b, [hits[k][p] for k in range(TOP_K)]) for p in range(per)]
    real = (i * tt + lax.broadcasted_iota(I32, (1, tt), 1)) < t_valid
    onehot = jnp.concatenate([(c & real).astype(F32) for c in chosen], axis=0)
    before = _dot(onehot.astype(BF16), tri_ref[...]) + carry[...]
    for k in range(TOP_K):
        rk = jnp.where(hits[k][0], before[0:ng, :], 0.0)
        for p in range(1, per):
            rk = rk + jnp.where(hits[k][p], before[p * ng:(p + 1) * ng, :], 0.0)
        rank_ref[k:k + 1, :] = jnp.sum(rk, axis=0, keepdims=True).astype(jnp.int32)
        idx_ref[k:k + 1, :] = idxs[k].astype(jnp.int32)
        wt_ref[k:k + 1, :] = wts[k] / wsum * ROUTED_SCALE
    carry[...] = carry[...] + jnp.sum(onehot, axis=1, keepdims=True)
    cnt_ref[...] = carry[...]


def _route(lgt, b_perm, t_valid):
    ne, t = lgt.shape
    tt = ROUTE_TILE
    tri = (jnp.arange(tt)[:, None] < jnp.arange(tt)[None, :]).astype(BF16)
    return pl.pallas_call(
        functools.partial(_route_kernel, t_valid=t_valid),
        grid=(t // tt,),
        in_specs=[pl.BlockSpec((ne, tt), lambda i: (0, i)),
                  pl.BlockSpec((ne, 1), lambda i: (0, 0)),
                  pl.BlockSpec((tt, tt), lambda i: (0, 0))],
        out_specs=[pl.BlockSpec((TOP_K, tt), lambda i: (0, i)),
                   pl.BlockSpec((TOP_K, tt), lambda i: (0, i)),
                   pl.BlockSpec((TOP_K, tt), lambda i: (0, i)),
                   pl.BlockSpec((ne, 1), lambda i: (0, 0))],
        out_shape=[jax.ShapeDtypeStruct((TOP_K, t), jnp.int32),
                   jax.ShapeDtypeStruct((TOP_K, t), F32),
                   jax.ShapeDtypeStruct((TOP_K, t), jnp.int32),
                   jax.ShapeDtypeStruct((ne, 1), F32)],
        scratch_shapes=[pltpu.VMEM((ne, 1), F32)],
        compiler_params=pltpu.CompilerParams(dimension_semantics=("arbitrary",),
                                             vmem_limit_bytes=VMEM_LIMIT),
        name="route",
    )(lgt, b_perm, tri)


def _dest_kernel(pstart_ref, idx_ref, rank_ref, dest_ref, *, t_valid, spare_row):
    idx = idx_ref[...]
    k_top, tile = idx.shape

    def add_start(e, acc):
        return acc + jnp.where(idx == e, pstart_ref[e], 0)

    dest = lax.fori_loop(0, pstart_ref.shape[0], add_start, rank_ref[...])
    tok = pl.program_id(0) * tile + lax.broadcasted_iota(I32, idx.shape, 1)
    spare = spare_row + (tok - t_valid) * k_top + lax.broadcasted_iota(I32, idx.shape, 0)
    dest_ref[...] = jnp.where(tok < t_valid, dest, spare)


def _dest_rows(pstart, idx, rank, t_valid, spare_row):
    k_top, t = idx.shape
    tile = DEST_TILE
    spec = pl.BlockSpec((k_top, tile), lambda i, ps: (0, i))
    return pl.pallas_call(
        functools.partial(_dest_kernel, t_valid=t_valid, spare_row=spare_row),
        grid_spec=pltpu.PrefetchScalarGridSpec(num_scalar_prefetch=1, grid=(t // tile,),
                                               in_specs=[spec, spec], out_specs=spec),
        out_shape=jax.ShapeDtypeStruct((k_top, t), I32),
        compiler_params=pltpu.CompilerParams(dimension_semantics=("arbitrary",),
                                             vmem_limit_bytes=VMEM_LIMIT),
        name="dest_rows",
    )(pstart, idx, rank)


def _expert_kernel(first_ref, cnt_ref, xs_hbm, wgu_ref, wd_ref, *rest):
    y_hbm, xbuf, ybuf, in_sem, out_sem = rest[-5:]
    e = pl.program_id(0)
    ne = first_ref.shape[0] - 1
    nbuf, rows = xbuf.shape[:2]
    de = wd_ref.shape[1]
    b0, b1 = first_ref[e], first_ref[e + 1]
    n_total = first_ref[ne]

    def in_copy(g):
        return pltpu.make_async_copy(xs_hbm.at[pl.ds(g * rows, rows)], xbuf.at[g % nbuf], in_sem.at[g % nbuf])

    def out_copy(g):
        return pltpu.make_async_copy(ybuf.at[g % nbuf], y_hbm.at[pl.ds(g * rows, rows)], out_sem.at[g % nbuf])

    @pl.when(e == 0)
    def _():
        for g in range(nbuf - 1):
            @pl.when(g < n_total)
            def _():
                in_copy(g).start()

    def block(g, carry):
        slot = g % nbuf

        @pl.when(g + nbuf - 1 < n_total)
        def _():
            in_copy(g + nbuf - 1).start()

        in_copy(g).wait()

        @pl.when(g >= nbuf)
        def _():
            out_copy(g - nbuf).wait()

        valid = cnt_ref[e] - (g - b0) * rows
        rid = lax.broadcasted_iota(I32, (rows, 1), 0)
        xs = jnp.where(rid < valid, _unpack_bf16_pairs(xbuf[slot]), 0.0).astype(BF16)
        gu = _dot(xs, wgu_ref[0])
        hb = _silu(gu[:, :de]) * gu[:, de:]
        ybuf[slot] = _pack_bf16_pairs(_dot(hb.astype(BF16), wd_ref[0]))
        out_copy(g).start()
        return carry

    lax.fori_loop(b0, b1, block, 0)

    @pl.when(e == ne - 1)
    def _():
        for back in range(nbuf, 0, -1):
            @pl.when(n_total >= back)
            def _():
                out_copy(n_total - back).wait()


def _experts(xs, w_gu, w_down, first_block, counts, after):
    p, dw = xs.shape
    ne, de, d = w_down.shape
    extra = [] if after is None else [after]
    return pl.pallas_call(
        _expert_kernel,
        grid_spec=pltpu.PrefetchScalarGridSpec(
            num_scalar_prefetch=2,
            grid=(ne,),
            in_specs=[pl.BlockSpec(memory_space=pl.ANY),
                      pl.BlockSpec((1, d, 2 * de), lambda e, fb, cn: (e, 0, 0)),
                      pl.BlockSpec((1, de, d), lambda e, fb, cn: (e, 0, 0))]
                     + [pl.BlockSpec(memory_space=pl.ANY)] * len(extra),
            out_specs=pl.BlockSpec(memory_space=pl.ANY),
            scratch_shapes=[pltpu.VMEM((EXPERT_BUFFERS, EXPERT_ROWS, dw), I32),
                            pltpu.VMEM((EXPERT_BUFFERS, EXPERT_ROWS, dw), I32),
                            pltpu.SemaphoreType.DMA((EXPERT_BUFFERS,)), pltpu.SemaphoreType.DMA((EXPERT_BUFFERS,))]),
        out_shape=jax.ShapeDtypeStruct((p, dw), I32),
        compiler_params=pltpu.CompilerParams(dimension_semantics=("arbitrary",),
                                             vmem_limit_bytes=VMEM_LIMIT),
        name="experts",
    )(first_block, counts, xs, w_gu, w_down, *extra)


def _sc_first_chunk(n_chunks):
    per_worker = n_chunks // SC_WORKERS
    assert per_worker * SC_WORKERS == n_chunks
    return per_worker, (lax.axis_index("s") * SC_CORES + lax.axis_index("c")) * per_worker


def _sc_start(copies):
    for cp in copies:
        cp.start()


def _sc_wait(copies):
    for cp in copies:
        cp.wait()


def _sc_dispatch(h2, dest, n_rows):
    w = h2.shape[1]
    n_chunks, k_top, ch = dest.shape
    mesh = plsc.VectorSubcoreMesh(core_axis_name="c", subcore_axis_name="s")

    @functools.partial(
        pl.kernel, mesh=mesh, out_type=jax.ShapeDtypeStruct((n_rows, w), I32),
        scratch_types=[pltpu.VMEM((2, k_top, ch), I32), pltpu.VMEM((2, ch, w), I32),
                       pltpu.SemaphoreType.DMA((2,)), pltpu.SemaphoreType.DMA((2,))])
    def dispatch(h2_hbm, dest_hbm, xs_hbm, idx_v, rows_v, load_sem, store_sem):
        per_worker, first = _sc_first_chunk(n_chunks)

        def loads(i):
            slot = i % 2
            return (pltpu.make_async_copy(h2_hbm.at[pl.ds((first + i) * ch, ch)], rows_v.at[slot], load_sem.at[slot]),
                    pltpu.make_async_copy(dest_hbm.at[first + i], idx_v.at[slot], load_sem.at[slot]))

        def scatters(i):
            slot = i % 2
            return [pltpu.make_async_copy(rows_v.at[slot], xs_hbm.at[idx_v.at[slot, k]], store_sem.at[slot])
                    for k in range(k_top)]

        for i in range(min(2, per_worker)):
            _sc_start(loads(i))
        for i in range(per_worker):
            _sc_wait(loads(i))
            _sc_start(scatters(i))
            if 1 <= i < per_worker - 1:
                _sc_wait(scatters(i - 1))
                _sc_start(loads(i + 1))
        for i in range(max(per_worker - 2, 0), per_worker):
            _sc_wait(scatters(i))

    return dispatch(h2, dest)


def _sc_combine_gather(y, dest, t):
    w = y.shape[1]
    n_chunks, k_top, ch = dest.shape
    n_local = n_chunks // SC_WORKERS
    mesh = plsc.VectorSubcoreMesh(core_axis_name="c", subcore_axis_name="s")

    @functools.partial(
        pl.kernel, mesh=mesh, out_type=jax.ShapeDtypeStruct((k_top, t, w), I32),
        scratch_types=[pltpu.VMEM((n_local, k_top, ch), I32), pltpu.VMEM((k_top, ch, w), I32),
                       pltpu.SemaphoreType.DMA((k_top,)), pltpu.SemaphoreType.DMA((k_top,))])
    def gather(y_hbm, dest_hbm, out_hbm, idx_v, rows_v, load_sem, store_sem):
        per_worker, first = _sc_first_chunk(n_chunks)
        pltpu.sync_copy(dest_hbm.at[pl.ds(first, per_worker)], idx_v)

        def fetch(i, k):
            return pltpu.make_async_copy(y_hbm.at[idx_v.at[i, k]], rows_v.at[k], load_sem.at[k])

        def put(i, k):
            return pltpu.make_async_copy(rows_v.at[k], out_hbm.at[k, pl.ds((first + i) * ch, ch)], store_sem.at[k])

        @pl.loop(0, per_worker)
        def _(i):
            for k in range(k_top):
                @pl.when(i > 0)
                def _():
                    put(i - 1, k).wait()

                fetch(i, k).start()
            for k in range(k_top):
                fetch(i, k).wait()
                put(i, k).start()

        for k in range(k_top):
            put(per_worker - 1, k).wait()

    return gather(y, dest)


def _final_kernel(xmid_ref, yg_ref, wt_ref, g2_ref, shf_ref, scf_ref, gf_ref, *rest):
    o_ref = rest[-1]
    wt = wt_ref[...].T[:xmid_ref.shape[0]]
    routed = wt[:, 0:1] * _unpack_bf16_pairs(yg_ref[0])
    for k in range(1, yg_ref.shape[0]):
        routed = routed + wt[:, k:k + 1] * _unpack_bf16_pairs(yg_ref[k])
    x2 = xmid_ref[...] + g2_ref[...] * routed
    o_ref[...] = _rms_norm(x2, gf_ref[...]) * (1.0 + scf_ref[...]) + shf_ref[...]


def _final_prompt(xmid, yg, wts_t, mod3, modf3, mod_row0, g_final, b0, tok_b0, n_batch, out_prev):
    b, seq, d = xmid.shape
    k_top, _, w = yg.shape
    tl = FINAL_TILE
    nl = seq // tl

    def vec(col):
        return pl.BlockSpec((None, 1, d), lambda i, j: (i + b0 + mod_row0, 0, col))

    in_specs = [pl.BlockSpec((None, tl, d), lambda i, j: (i, j, 0)),
                pl.BlockSpec((k_top, tl, w), lambda i, j: (0, (i + b0 - tok_b0) * nl + j, 0)),
                pl.BlockSpec((k_top, tl), lambda i, j: (0, (i + b0 - tok_b0) * nl + j)),
                vec(5), vec(0), vec(1), pl.BlockSpec((1, d), lambda i, j: (0, 0))]
    args = [xmid, yg, wts_t, mod3, modf3, modf3, g_final]
    aliases = {}
    if out_prev is not None:
        in_specs.append(pl.BlockSpec(memory_space=pl.ANY))
        args.append(out_prev)
        aliases = {7: 0}
    return pl.pallas_call(
        _final_kernel,
        grid=(b, nl),
        in_specs=in_specs,
        out_specs=pl.BlockSpec((None, tl, d), lambda i, j: (i + b0, j, 0)),
        out_shape=jax.ShapeDtypeStruct((n_batch, seq, d), F32),
        compiler_params=pltpu.CompilerParams(dimension_semantics=("arbitrary", "arbitrary"),
                                             vmem_limit_bytes=VMEM_LIMIT),
        input_output_aliases=aliases,
        name="final_prompt",
    )(*args)


def _final_sample(xmid_all, yg, wts_t, mod, modf, g_final, part, rows, first_row, after):
    d = xmid_all.shape[1]
    k_top, _, w = yg.shape

    def vec(col):
        return pl.BlockSpec((rows, d), lambda i: (part, col))

    return pl.pallas_call(
        _final_kernel,
        grid=(1,),
        in_specs=[vec(0),
                  pl.BlockSpec((k_top, rows, w), lambda i: (0, first_row // rows, 0)),
                  pl.BlockSpec((k_top, 128), lambda i: (0, first_row // 128)),
                  vec(5), vec(0), vec(1), pl.BlockSpec((1, d), lambda i: (0, 0)), pl.BlockSpec(memory_space=pl.ANY)],
        out_specs=pl.BlockSpec((rows, d), lambda i: (0, 0)),
        out_shape=jax.ShapeDtypeStruct((rows, d), F32),
        compiler_params=pltpu.CompilerParams(dimension_semantics=("arbitrary",),
                                             vmem_limit_bytes=VMEM_LIMIT),
        name="final_sample",
    )(xmid_all, yg, wts_t, mod, modf, modf, g_final, after)


def _block_diag_pairs(w_pool):
    g, c, _ = w_pool.shape
    eye = jnp.eye(g, dtype=w_pool.dtype)
    return (eye[:, None, :, None] * w_pool[:, :, None, :]).reshape(g * c, g * c)


def kernel(x_prompt, x_sample, state_pool, state_conv, c_prompt, c_sample, w_ada, b_ada, g_mix, w_in, w_pool, pool_scale, w_dw, b_dw, ln_g, ln_b, w_out, g_ffn, w_router, b_router, w_gate, w_up, w_down, ws_gate, ws_up, ws_down, w_ada_final, b_ada_final, g_final):
    bp, seq, d = x_prompt.shape
    bs = x_sample.shape[0]
    depth = w_ada.shape[0]
    assert depth == 1 and x_sample.shape[1] == 1
    conv_k = w_dw.shape[1]
    ne = w_router.shape[-1]
    per = ne // N_EXPERT_GROUPS
    tp = bp * seq
    t_all = tp + bs

    row_expert = (jnp.arange(ne) % N_EXPERT_GROUPS) * per + jnp.arange(ne) // N_EXPERT_GROUPS
    wl = {
        "g_mix": g_mix[0][None, :],
        "in": w_in[0].astype(BF16),
        "pool": _block_diag_pairs(w_pool[0]).astype(BF16),
        "pool_scale": pool_scale[0][None, :],
        "dw": w_dw[0],
        "b_dw": b_dw[0][None, :],
        "ln_g": ln_g[0][None, :],
        "ln_b": ln_b[0][None, :],
        "out": w_out[0].astype(BF16),
        "g_ffn": g_ffn[0][None, :],
        "router_t": w_router[0].T[row_expert].astype(BF16),
        "s_gu": jnp.concatenate([ws_gate[0], ws_up[0]], axis=1).astype(BF16),
        "s_down": ws_down[0].astype(BF16),
    }
    b_perm = b_router[0][row_expert][:, None]

    assert bp % N_PARTS == 0 and bs % N_PARTS == 0
    bpp, bsp = bp // N_PARTS, bs // N_PARTS
    tpp = bpp * seq
    t_part = tpp + bsp
    grain = SC_WORKERS * SC_DISPATCH_CHUNK * SC_COMBINE_CHUNK // math.gcd(SC_DISPATCH_CHUNK, SC_COMBINE_CHUNK)
    t_pad = (t_part + grain - 1) // grain * grain
    assert t_pad % ROUTE_TILE == 0 and t_pad % DEST_TILE == 0 and tpp % bsp == 0

    c_all = jnp.concatenate([c_sample, c_prompt], axis=0)
    mod = _ada(c_all, w_ada[0], b_ada[0])
    modf = _ada(c_all, w_ada_final, b_ada_final)
    mod3, modf3 = mod[:, None, :], modf[:, None, :]

    xmid_s, new_pool_t, new_conv_t, tails = _sample_mixer(
        x_sample.reshape(bs, d), mod, jnp.transpose(state_pool[0], (1, 0, 2)),
        jnp.transpose(state_conv[0], (1, 0, 2)), wl, tpp, t_pad, conv_k=conv_k)
    new_pool_s = jnp.transpose(new_pool_t, (1, 0, 2))
    new_conv_s = jnp.transpose(new_conv_t, (1, 0, 2))
    w_experts = tuple(a.reshape(a.shape[1:]) for a in (w_gate, w_up, w_down))

    n_blocks = (t_part * TOP_K + EXPERT_ROWS - 1) // EXPERT_ROWS + ne
    n_spare = -(-(t_pad - t_part) * TOP_K // EXPERT_ROWS)
    n_rows = (n_blocks + n_spare) * EXPERT_ROWS

    def routed_layout(lgt):
        idx, wts, rank, counts_perm = _route(lgt, b_perm, t_part)
        counts = jnp.zeros((ne,), I32).at[row_expert].set(counts_perm[:, 0].astype(I32))
        nblk = (counts + EXPERT_ROWS - 1) // EXPERT_ROWS
        first_block = jnp.concatenate([jnp.zeros((1,), I32), jnp.cumsum(nblk).astype(I32)])
        dest = _dest_rows(first_block[:ne] * EXPERT_ROWS, idx, rank, t_part, n_blocks * EXPERT_ROWS)

        def chunked(ch):
            return dest.reshape(TOP_K, t_pad // ch, ch).transpose(1, 0, 2)

        return wts, first_block, counts, chunked(SC_DISPATCH_CHUNK), chunked(SC_COMBINE_CHUNK)

    assert N_PARTS == 2 and bpp % 2 == 0
    half = bpp // 2
    n_steps = seq // SEQ_TILE * bpp
    mix = functools.partial(_prompt_mixer, x_prompt, mod3, bs, wl, conv_k=conv_k)
    per_step_a = 3
    n_first = per_step_a * n_steps
    per_step_b = (ne - n_first) // (n_steps // 2)
    assert 0 < n_first < ne and n_first + per_step_b * (n_steps // 2) == ne and n_first % per_step_b == 0
    xmid_a, h2_a, lgt_a, npool_a, nconv_a, *w_bf = mix(*tails[0], 0, bpp, 0, w_experts, (0, per_step_a), None, None)
    wts_a, first_a, counts_a, dest_a_d, dest_a_c = routed_layout(lgt_a)
    xs_a = _sc_dispatch(h2_a, dest_a_d, n_rows)
    xmid_b1, h2_b, lgt_b, npool_b1, nconv_b1, *w_bf = mix(*tails[1], bpp, half, bpp, w_experts,
                                                         (n_first, per_step_b), w_bf, None)
    y_a = _experts(xs_a, w_bf[0], w_bf[1], first_a, counts_a, None)
    yg_a = _sc_combine_gather(y_a, dest_a_c, t_pad)
    xmid_b2, h2_b, lgt_b, npool_b2, nconv_b2 = mix(h2_b, lgt_b, bpp + half, half, bpp, w_experts, None, None, y_a)
    wts_b, first_b, counts_b, dest_b_d, dest_b_c = routed_layout(lgt_b)
    xs_b = _sc_dispatch(h2_b, dest_b_d, n_rows)
    ys_a = _final_sample(xmid_s, yg_a, wts_a, mod, modf, g_final[None, :], 0, bsp, tpp, g_final)
    y_prompt = _final_prompt(xmid_a, yg_a, wts_a, mod3, modf3, bs, g_final[None, :], 0, 0, bp, None)
    y_b = _experts(xs_b, w_bf[0], w_bf[1], first_b, counts_b, y_prompt)
    yg_b = _sc_combine_gather(y_b, dest_b_c, t_pad)
    ys_b = _final_sample(xmid_s, yg_b, wts_b, mod, modf, g_final[None, :], 1, bsp, tpp, y_prompt)
    y_prompt = _final_prompt(xmid_b1, yg_b, wts_b, mod3, modf3, bs, g_final[None, :], bpp, bpp, bp, y_prompt)
    y_prompt = _final_prompt(xmid_b2, yg_b, wts_b, mod3, modf3, bs, g_final[None, :], bpp + half, bpp, bp, y_prompt)
    y_samples = [ys_a, ys_b]
    npools = [npool_a, npool_b1, npool_b2]
    nconvs = [nconv_a, nconv_b1, nconv_b2]
    y_sample = jnp.concatenate(y_samples, axis=0)
    npool_p = jnp.concatenate(npools, axis=0)
    nconv_p = jnp.concatenate(nconvs, axis=0)

    return (y_prompt, y_sample[:, None, :], npool_p[None], nconv_p[None], new_pool_s[None], new_conv_s[None])
```

```python
import functools
import math

import jax
import jax.numpy as jnp
from jax import lax
from jax.experimental import pallas as pl
from jax.experimental.pallas import tpu as pltpu
from jax.experimental.pallas import tpu_sc as plsc

POOL_WINDOWS = (2, 4, 8, 16)
N_EXPERT_GROUPS = 8
TOPK_GROUPS = 4
TOP_K = 8
ROUTED_SCALE = 2.5
EPS = 1e-6

LANES = 128
SUBLANES = 8
BF16_ROWS = 16
VMEM_LIMIT = 52 * 1024 * 1024

SEQ_TILE = 512
ROW_CHUNK = 32
MIXER_SUB_TILES = 2
ROUTE_TILE = 384
EXPERT_ROWS = 512
N_PARTS = 2
EXPERT_BUFFERS = 6
FINAL_TILE = 512
DEST_TILE = 2816

SC_CORES = 2
SC_SUBCORES = 16
SC_WORKERS = SC_CORES * SC_SUBCORES
SC_DISPATCH_CHUNK = 88
SC_COMBINE_CHUNK = 24

F32 = jnp.float32
BF16 = jnp.bfloat16
I32 = jnp.int32
U32 = jnp.uint32
NEG_INF = float("-inf")
HI16 = 0xFFFF0000


def _sigmoid(x):
    return 1.0 / (1.0 + jnp.exp(-x))


def _silu(x):
    return x * _sigmoid(x)


def _rms_norm(x, g):
    return x * lax.rsqrt(jnp.mean(x * x, axis=-1, keepdims=True) + EPS) * g


def _dot(a, b):
    return jnp.dot(a, b, preferred_element_type=F32)


def _pack_bf16_pairs(x):
    w = x.shape[1] // 2
    bits = lax.bitcast_convert_type(x.astype(BF16).astype(F32), U32)
    return lax.bitcast_convert_type((bits[:, :w] >> 16) | (bits[:, w:] & U32(HI16)), I32)


def _unpack_bf16_pairs(p):
    bits = lax.bitcast_convert_type(p, U32)
    lo = lax.bitcast_convert_type(bits << 16, F32)
    hi = lax.bitcast_convert_type(bits & U32(HI16), F32)
    return jnp.concatenate([lo, hi], axis=1)


def _ada_kernel(c_ref, w_ref, b_ref, o_ref):
    s = _silu(c_ref[...])
    o_ref[...] = _dot(s.astype(BF16), w_ref[...].astype(BF16)) + b_ref[...]


def _ada(c, w, b):
    rows, d = c.shape
    n = w.shape[1]
    tn = 1024
    return pl.pallas_call(
        _ada_kernel,
        grid=(n // tn,),
        in_specs=[pl.BlockSpec((rows, d), lambda j: (0, 0)),
                  pl.BlockSpec((d, tn), lambda j: (0, j)),
                  pl.BlockSpec((1, tn), lambda j: (0, j))],
        out_specs=pl.BlockSpec((rows, tn), lambda j: (0, j)),
        out_shape=jax.ShapeDtypeStruct((rows, n), F32),
        compiler_params=pltpu.CompilerParams(dimension_semantics=("arbitrary",),
                                             vmem_limit_bytes=VMEM_LIMIT),
        name="ada",
    )(c, w, b.reshape(1, n))


def _mixer_tail(x, pool_d, conv_act, mod, w, xmid_ref, h2_ref, lgt_ref):
    sh2, sc2, g1, g2 = mod
    pw = pool_d.shape[1]
    pool_out = _dot(pool_d, w["pool"][...]) * w["pool_scale"][...]
    mix = _dot(pool_out.astype(BF16), w["out"][:pw, :]) + _dot(conv_act, w["out"][pw:, :])
    x1 = x + g1 * mix
    h2f = _rms_norm(x1, w["g_ffn"][...]) * (1.0 + sc2) + sh2
    h2_ref[...] = _pack_bf16_pairs(h2f)
    h2 = h2f.astype(BF16)
    lgt_ref[...] = lax.dot_general(w["router_t"][...], h2, (((1,), (1,)), ((), ())),
                                   preferred_element_type=F32)
    gu = _dot(h2, w["s_gu"][...])
    de = gu.shape[1] // 2
    hs = _silu(gu[:, :de]) * gu[:, de:]
    shared = _dot(hs.astype(BF16), w["s_down"][...])
    xmid_ref[...] = x1 + g2 * shared


_W_NAMES = ("g_mix", "in", "pool", "pool_scale", "dw", "b_dw", "ln_g", "ln_b", "out", "g_ffn",
            "router_t", "s_gu", "s_down")


def _layer_norm_silu(yc, g, b):
    mu = jnp.mean(yc, axis=-1, keepdims=True)
    yz = yc - mu
    var = jnp.mean(yz * yz, axis=-1, keepdims=True)
    return _silu(yz * lax.rsqrt(var + EPS) * g + b)


U_HALO = 32
A_HALO = 32


def _prompt_mixer_kernel(x_ref, mod_ref, *refs, tl, d, pw, cw, conv_k, cast):
    nw = len(_W_NAMES)
    w = dict(zip(_W_NAMES, refs[:nw]))
    ubuf, s2buf, s4buf, s8buf, abuf, ashift, dbuf, cbuf, rowb = refs[-9:]
    outs = refs[-9 - (7 if cast else 5):-9]
    xmid_ref, h2_ref, lgt_ref, npool_ref, nconv_ref = outs[:5]
    if cast:
        wg_ref, wu_ref, wd_ref = refs[nw + 2:nw + 5]
        wgu_bf_ref, wd_bf_ref = outs[5:]
        de = wg_ref.shape[2]
        wgu_bf_ref[:, :, :de] = wg_ref[...].astype(BF16)
        wgu_bf_ref[:, :, de:] = wu_ref[...].astype(BF16)
        wd_bf_ref[...] = wd_ref[...].astype(BF16)
    l = pl.program_id(1)
    pool_buf = max(POOL_WINDOWS) - 1
    uh, ah = U_HALO, A_HALO
    pg = pw // len(POOL_WINDOWS)
    nu, na = uh + tl, ah + tl

    @pl.when(l == 0)
    def _():
        ubuf[0:uh, :] = jnp.zeros((uh, pw), F32)
        abuf[0:ah, :] = jnp.zeros((ah, cw), F32)

    @pl.when(l > 0)
    def _():
        ubuf[0:uh, :] = ubuf[tl:tl + uh, :]
        abuf[0:ah, :] = abuf[tl:tl + ah, :]

    sh1 = mod_ref[0, :, 0 * d:1 * d]
    sc1 = mod_ref[0, :, 1 * d:2 * d]
    g1 = mod_ref[0, :, 2 * d:3 * d]
    sh2 = mod_ref[0, :, 3 * d:4 * d]
    sc2 = mod_ref[0, :, 4 * d:5 * d]
    g2 = mod_ref[0, :, 5 * d:6 * d]

    for k in range(conv_k):
        rowb[k] = jnp.broadcast_to(w["dw"][k:k + 1, :], (SUBLANES, cw))
    rowb[conv_k] = jnp.broadcast_to(w["b_dw"][...], (SUBLANES, cw))

    def stage_in(r0, r1):
        h = _rms_norm(x_ref[0, r0:r1, :], w["g_mix"][...]) * (1.0 + sc1) + sh1
        z = _dot(h.astype(BF16), w["in"][...])
        ubuf[uh + r0:uh + r1, :] = z[:, :pw]
        abuf[ah + r0:ah + r1, :] = z[:, pw:pw + cw] * _sigmoid(z[:, pw + cw:])

    def stage_mix(r0, r1):
        first = r0 == 0
        ue = uh + r1
        b2, b4, b8 = (8, 16, 24) if first else (uh + r0,) * 3
        s2buf[b2:ue, :] = ubuf[b2:ue, :] + ubuf[b2 - 1:ue - 1, :]
        s4buf[b4:ue, :] = s2buf[b4:ue, pg:] + s2buf[b4 - 2:ue - 2, pg:]
        s8buf[b8:ue, :] = s4buf[b8:ue, pg:] + s4buf[b8 - 4:ue - 4, pg:]
        us = uh + r0
        t = l * tl + r0 + lax.broadcasted_iota(I32, (r1 - r0, 1), 0)
        sums = (s2buf[us:ue, 0:pg], s4buf[us:ue, 0:pg], s8buf[us:ue, 0:pg],
                s8buf[us:ue, pg:2 * pg] + s8buf[us - 8:ue - 8, pg:2 * pg])
        for g, win in enumerate(POOL_WINDOWS):
            cols = slice(g * pg, (g + 1) * pg)
            inv = 1.0 / jnp.minimum(win, t + 1).astype(F32)
            dbuf[r0:r1, cols] = (sums[g] * inv - ubuf[us:ue, cols]).astype(BF16)

        ab, ae = (8 if first else ah + r0), ah + r1
        for r in range(1, SUBLANES):
            ashift[r - 1, ab:ae, :] = abuf[ab - r:ae - r, :]

        for c in range(r0 // ROW_CHUNK, r1 // ROW_CHUNK):
            starts = range(ah + c * ROW_CHUNK, ah + (c + 1) * ROW_CHUNK, SUBLANES)
            groups = [abuf[o:o + SUBLANES, :] * rowb[conv_k - 1] for o in starts]
            for back in range(1, conv_k):
                q, r = divmod(back, SUBLANES)
                src = abuf if r == 0 else ashift.at[r - 1]
                tap = rowb[conv_k - 1 - back]
                groups = [yc + src[o - q * SUBLANES:o - (q - 1) * SUBLANES, :] * tap for yc, o in zip(groups, starts)]
            groups = [yc + rowb[conv_k] for yc in groups]
            c0 = c * ROW_CHUNK
            cbuf[c0:c0 + ROW_CHUNK, :] = _layer_norm_silu(
                jnp.concatenate(groups, axis=0), w["ln_g"][...], w["ln_b"][...]).astype(BF16)

    def stage_out(r0, r1):
        n = r1 - r0
        _mixer_tail(x_ref[0, r0:r1, :], dbuf[r0:r1, :], cbuf[r0:r1, :], (sh2, sc2, g1, g2), w,
                    xmid_ref.at[0, pl.ds(r0, n), :], h2_ref.at[pl.ds(r0, n), :], lgt_ref.at[:, pl.ds(r0, n)])

    sub = tl // MIXER_SUB_TILES
    ranges = [(i * sub, (i + 1) * sub) for i in range(MIXER_SUB_TILES)]
    stage_in(*ranges[0])
    for i, rng in enumerate(ranges):
        if i + 1 < len(ranges):
            stage_in(*ranges[i + 1])
        stage_mix(*rng)
        stage_out(*rng)

    @pl.when(l == pl.num_programs(1) - 1)
    def _():
        npool_ref[0] = ubuf[nu - pool_buf:nu, :]
        nconv_ref[0] = abuf[na - (conv_k - 1):na, :]


def _sample_mixer_kernel(x_ref, mod_ref, sp_ref, sc_ref, *refs, d, pw, cw, conv_k):
    nw = len(_W_NAMES)
    w = dict(zip(_W_NAMES, refs[:nw]))
    xmid_ref, npool_ref, nconv_ref = refs[nw:nw + 3]
    tails = refs[nw + 3:-2]
    h2_tmp, lgt_tmp = refs[-2:]
    pool_buf = max(POOL_WINDOWS) - 1
    pg = pw // len(POOL_WINDOWS)
    x = x_ref[...]
    sh1, sc1, g1, sh2, sc2, g2 = [mod_ref[:, i * d:(i + 1) * d] for i in range(6)]
    h = _rms_norm(x, w["g_mix"][...]) * (1.0 + sc1) + sh1
    z = _dot(h.astype(BF16), w["in"][...])
    u = z[:, :pw]
    a = z[:, pw:pw + cw] * _sigmoid(z[:, pw + cw:])
    npool_ref[0:pool_buf - 1] = sp_ref[1:pool_buf]
    npool_ref[pool_buf - 1] = u
    nconv_ref[0:conv_k - 2] = sc_ref[1:conv_k - 1]
    nconv_ref[conv_k - 2] = a
    ds = []
    for g, win in enumerate(POOL_WINDOWS):
        cols = slice(g * pg, (g + 1) * pg)
        acc = u[:, cols]
        for j in range(1, win):
            acc = acc + sp_ref[pool_buf - j, :, cols]
        cnt = float(min(win, 1 + pool_buf))
        ds.append(acc / cnt - u[:, cols])
    pool_d = jnp.concatenate(ds, axis=-1).astype(BF16)
    yc = a * w["dw"][conv_k - 1:conv_k, :]
    for k in range(conv_k - 1):
        yc = yc + sc_ref[k] * w["dw"][k:k + 1, :]
    yc = yc + w["b_dw"][...]
    conv_act = _layer_norm_silu(yc, w["ln_g"][...], w["ln_b"][...]).astype(BF16)
    _mixer_tail(x, pool_d, conv_act, (sh2, sc2, g1, g2), w, xmid_ref, h2_tmp, lgt_tmp)
    n_parts = len(tails) // 2
    share = x.shape[0] // n_parts
    for p in range(n_parts):
        h2_ref, lgt_ref = tails[2 * p], tails[2 * p + 1]
        h2_ref[...] = jnp.zeros_like(h2_ref)
        lgt_ref[...] = jnp.zeros_like(lgt_ref)
        h2_ref[0:share, :] = h2_tmp[p * share:(p + 1) * share, :]
        lgt_ref[:, 0:share] = lgt_tmp[:, p * share:(p + 1) * share]


def _full_spec(a):
    nd = a.ndim
    return pl.BlockSpec(a.shape, lambda *_: (0,) * nd)


def _prompt_mixer(x, mod3, mod_row0, wl, h2_all, lgt_all, b0, b, tok_b0, w_experts, cast, w_bf_prev, after, *,
                  conv_k):
    n_batch, seq, d = x.shape
    w_gate, w_up, w_down = w_experts
    n_exp, _, de = w_gate.shape
    pw = wl["pool"].shape[0]
    cw = wl["dw"].shape[1]
    ne = wl["router_t"].shape[0]
    tl = SEQ_TILE
    nl = seq // tl
    pg = pw // len(POOL_WINDOWS)
    assert POOL_WINDOWS == (2, 4, 8, 16) and conv_k - 1 <= A_HALO
    ws = [wl[n] for n in _W_NAMES]
    kern = functools.partial(_prompt_mixer_kernel, tl=tl, d=d, pw=pw, cw=cw, conv_k=conv_k, cast=cast is not None)

    def tok_map(i, j):
        return ((i + b0 - tok_b0) * nl + j, 0)

    ins = [x, mod3, *ws, h2_all, lgt_all]
    in_specs = ([pl.BlockSpec((1, tl, d), lambda i, j: (i + b0, j, 0)),
                 pl.BlockSpec((1, 1, mod3.shape[-1]), lambda i, j: (i + b0 + mod_row0, 0, 0))]
                + [_full_spec(a) for a in ws]
                + [pl.BlockSpec(memory_space=pl.ANY), pl.BlockSpec(memory_space=pl.ANY)])
    aliases = {2 + len(ws): 1, 3 + len(ws): 2}
    out_specs = [pl.BlockSpec((1, tl, d), lambda i, j: (i, j, 0)),
                 pl.BlockSpec((tl, d // 2), tok_map),
                 pl.BlockSpec((ne, tl), lambda i, j: tok_map(i, j)[::-1]),
                 pl.BlockSpec((1, max(POOL_WINDOWS) - 1, pw), lambda i, j: (i, 0, 0)),
                 pl.BlockSpec((1, conv_k - 1, cw), lambda i, j: (i, 0, 0))]
    out_shape = [jax.ShapeDtypeStruct((b, seq, d), F32),
                 jax.ShapeDtypeStruct(h2_all.shape, I32),
                 jax.ShapeDtypeStruct(lgt_all.shape, F32),
                 jax.ShapeDtypeStruct((b, max(POOL_WINDOWS) - 1, pw), F32),
                 jax.ShapeDtypeStruct((b, conv_k - 1, cw), F32)]
    if cast is not None:
        e0, ecs = cast
        assert e0 % ecs == 0 and e0 + b * nl * ecs <= n_exp

        def w_map(i, j):
            return (e0 // ecs + i * nl + j, 0, 0)

        ins += [w_gate, w_up, w_down]
        in_specs += [pl.BlockSpec((ecs, d, de), w_map), pl.BlockSpec((ecs, d, de), w_map),
                     pl.BlockSpec((ecs, de, d), w_map)]
        out_specs += [pl.BlockSpec((ecs, d, 2 * de), w_map), pl.BlockSpec((ecs, de, d), w_map)]
        out_shape += [jax.ShapeDtypeStruct((n_exp, d, 2 * de), BF16), jax.ShapeDtypeStruct((n_exp, de, d), BF16)]
        if w_bf_prev is not None:
            aliases.update({len(ins): 5, len(ins) + 1: 6})
            ins += list(w_bf_prev)
            in_specs += [pl.BlockSpec(memory_space=pl.ANY), pl.BlockSpec(memory_space=pl.ANY)]
    if after is not None:
        ins.append(after)
        in_specs.append(pl.BlockSpec(memory_space=pl.ANY))
    return pl.pallas_call(
        kern,
        grid=(b, nl),
        in_specs=in_specs,
        out_specs=out_specs,
        out_shape=out_shape,
        scratch_shapes=[pltpu.VMEM((U_HALO + tl, pw), F32), pltpu.VMEM((U_HALO + tl, pw), F32),
                        pltpu.VMEM((U_HALO + tl, pw - pg), F32), pltpu.VMEM((U_HALO + tl, pw - 2 * pg), F32),
                        pltpu.VMEM((A_HALO + tl, cw), F32), pltpu.VMEM((SUBLANES - 1, A_HALO + tl, cw), F32),
                        pltpu.VMEM((tl, pw), BF16), pltpu.VMEM((tl, cw), BF16),
                        pltpu.VMEM((conv_k + 1, SUBLANES, cw), F32)],
        compiler_params=pltpu.CompilerParams(dimension_semantics=("arbitrary", "arbitrary"),
                                             vmem_limit_bytes=VMEM_LIMIT),
        input_output_aliases=aliases,
        name="prompt_mixer",
    )(*ins)


def _sample_mixer(x, mod, sp, sc, wl, t_prompt, t_pad, *, conv_k):
    rows, d = x.shape
    tail = t_pad - t_prompt
    assert t_prompt % tail == 0 and rows % N_PARTS == 0 and rows // N_PARTS <= tail
    pw = wl["pool"].shape[0]
    cw = wl["dw"].shape[1]
    ne = wl["router_t"].shape[0]
    ws = [wl[n] for n in _W_NAMES]
    kern = functools.partial(_sample_mixer_kernel, d=d, pw=pw, cw=cw, conv_k=conv_k)
    ins = [x, mod, sp, sc] + ws
    tail_specs = [pl.BlockSpec((tail, d // 2), lambda i: (t_prompt // tail, 0)),
                  pl.BlockSpec((ne, tail), lambda i: (0, t_prompt // tail))] * N_PARTS
    tail_shapes = [jax.ShapeDtypeStruct((t_pad, d // 2), I32), jax.ShapeDtypeStruct((ne, t_pad), F32)] * N_PARTS
    outs = pl.pallas_call(
        kern,
        grid=(1,),
        in_specs=[_full_spec(x), pl.BlockSpec((rows, mod.shape[1]), lambda i: (0, 0))]
                 + [_full_spec(a) for a in ins[2:]],
        out_specs=[pl.BlockSpec((rows, d), lambda i: (0, 0)), _full_spec(sp), _full_spec(sc)] + tail_specs,
        out_shape=[jax.ShapeDtypeStruct((rows, d), F32),
                   jax.ShapeDtypeStruct(sp.shape, F32),
                   jax.ShapeDtypeStruct(sc.shape, F32)] + tail_shapes,
        scratch_shapes=[pltpu.VMEM((rows, d // 2), I32), pltpu.VMEM((ne, rows), F32)],
        compiler_params=pltpu.CompilerParams(dimension_semantics=("arbitrary",),
                                             vmem_limit_bytes=VMEM_LIMIT),
        name="sample_mixer",
    )(*ins)
    return outs[0], outs[1], outs[2], [(outs[3 + 2 * p], outs[4 + 2 * p]) for p in range(N_PARTS)]


def _sublane_max(x):
    return jnp.max(x, axis=0, keepdims=True)


def _route_kernel(lg_ref, b_ref, tri_ref, idx_ref, wt_ref, rank_ref, cnt_ref, carry, *, t_valid):
    i = pl.program_id(0)
    ne, tt = lg_ref.shape
    ng = N_EXPERT_GROUPS
    per = ne // ng

    @pl.when(i == 0)
    def _():
        carry[...] = jnp.zeros_like(carry)

    s = _sigmoid(lg_ref[...])
    sel = s + b_ref[...]
    s3 = [s[p * ng:(p + 1) * ng, :] for p in range(per)]
    sel3 = [sel[p * ng:(p + 1) * ng, :] for p in range(per)]
    m1 = sel3[0]
    m2 = jnp.full_like(m1, NEG_INF)
    for p in range(1, per):
        m2 = jnp.maximum(m2, jnp.minimum(m1, sel3[p]))
        m1 = jnp.maximum(m1, sel3[p])
    gs = m1 + m2
    gi = lax.broadcasted_iota(jnp.int32, (ng, tt), 0)
    beaten = jnp.zeros((ng, tt), jnp.int32)
    for g in range(ng):
        row = gs[g:g + 1, :]
        beats = (row > gs) | ((row == gs) & (gi > g))
        beaten = beaten + beats.astype(jnp.int32)
    keep = beaten < TOPK_GROUPS
    cur = [jnp.where(keep, sel3[p], NEG_INF) for p in range(per)]
    eid = [(gi * per + p).astype(F32) for p in range(per)]
    idxs, wts, hits = [], [], []
    for _ in range(TOP_K):
        m = cur[0]
        for p in range(1, per):
            m = jnp.maximum(m, cur[p])
        m = _sublane_max(m)
        cand = jnp.where(cur[0] == m, eid[0], float(ne))
        for p in range(1, per):
            cand = jnp.minimum(cand, jnp.where(cur[p] == m, eid[p], float(ne)))
        e_sel = jnp.min(cand, axis=0, keepdims=True)
        hit = [eid[p] == e_sel for p in range(per)]
        wk = jnp.where(hit[0], s3[0], 0.0)
        for p in range(1, per):
            wk = wk + jnp.where(hit[p], s3[p], 0.0)
        wts.append(jnp.sum(wk, axis=0, keepdims=True))
        cur = [jnp.where(hit[p], NEG_INF, cur[p]) for p in range(per)]
        idxs.append(e_sel)
        hits.append(hit)
    wsum = wts[0]
    for k in range(1, TOP_K):
        wsum = wsum + wts[k]
    chosen = [functools.reduce(lambda a, b: a | b, [hits[k][p] for k in range(TOP_K)]) for p in range(per)]
    real = (i * tt + lax.broadcasted_iota(I32, (1, tt), 1)) < t_valid
    onehot = jnp.concatenate([(c & real).astype(F32) for c in chosen], axis=0)
    before = _dot(onehot.astype(BF16), tri_ref[...]) + carry[...]
    for k in range(TOP_K):
        rk = jnp.where(hits[k][0], before[0:ng, :], 0.0)
        for p in range(1, per):
            rk = rk + jnp.where(hits[k][p], before[p * ng:(p + 1) * ng, :], 0.0)
        rank_ref[k:k + 1, :] = jnp.sum(rk, axis=0, keepdims=True).astype(jnp.int32)
        idx_ref[k:k + 1, :] = idxs[k].astype(jnp.int32)
        wt_ref[k:k + 1, :] = wts[k] / wsum * ROUTED_SCALE
    carry[...] = carry[...] + jnp.sum(onehot, axis=1, keepdims=True)
    cnt_ref[...] = carry[...]


def _route(lgt, b_perm, t_valid):
    ne, t = lgt.shape
    tt = ROUTE_TILE
    tri = (jnp.arange(tt)[:, None] < jnp.arange(tt)[None, :]).astype(BF16)
    return pl.pallas_call(
        functools.partial(_route_kernel, t_valid=t_valid),
        grid=(t // tt,),
        in_specs=[pl.BlockSpec((ne, tt), lambda i: (0, i)),
                  pl.BlockSpec((ne, 1), lambda i: (0, 0)),
                  pl.BlockSpec((tt, tt), lambda i: (0, 0))],
        out_specs=[pl.BlockSpec((TOP_K, tt), lambda i: (0, i)),
                   pl.BlockSpec((TOP_K, tt), lambda i: (0, i)),
                   pl.BlockSpec((TOP_K, tt), lambda i: (0, i)),
                   pl.BlockSpec((ne, 1), lambda i: (0, 0))],
        out_shape=[jax.ShapeDtypeStruct((TOP_K, t), jnp.int32),
                   jax.ShapeDtypeStruct((TOP_K, t), F32),
                   jax.ShapeDtypeStruct((TOP_K, t), jnp.int32),
                   jax.ShapeDtypeStruct((ne, 1), F32)],
        scratch_shapes=[pltpu.VMEM((ne, 1), F32)],
        compiler_params=pltpu.CompilerParams(dimension_semantics=("arbitrary",),
                                             vmem_limit_bytes=VMEM_LIMIT),
        name="route",
    )(lgt, b_perm, tri)


def _dest_kernel(pstart_ref, idx_ref, rank_ref, dest_ref, *, t_valid, spare_row):
    idx = idx_ref[...]
    k_top, tile = idx.shape

    def add_start(e, acc):
        return acc + jnp.where(idx == e, pstart_ref[e], 0)

    dest = lax.fori_loop(0, pstart_ref.shape[0], add_start, rank_ref[...])
    tok = pl.program_id(0) * tile + lax.broadcasted_iota(I32, idx.shape, 1)
    spare = spare_row + (tok - t_valid) * k_top + lax.broadcasted_iota(I32, idx.shape, 0)
    dest_ref[...] = jnp.where(tok < t_valid, dest, spare)


def _dest_rows(pstart, idx, rank, t_valid, spare_row):
    k_top, t = idx.shape
    tile = DEST_TILE
    spec = pl.BlockSpec((k_top, tile), lambda i, ps: (0, i))
    return pl.pallas_call(
        functools.partial(_dest_kernel, t_valid=t_valid, spare_row=spare_row),
        grid_spec=pltpu.PrefetchScalarGridSpec(num_scalar_prefetch=1, grid=(t // tile,),
                                               in_specs=[spec, spec], out_specs=spec),
        out_shape=jax.ShapeDtypeStruct((k_top, t), I32),
        compiler_params=pltpu.CompilerParams(dimension_semantics=("arbitrary",),
                                             vmem_limit_bytes=VMEM_LIMIT),
        name="dest_rows",
    )(pstart, idx, rank)


def _expert_kernel(first_ref, cnt_ref, xs_hbm, wgu_ref, wd_ref, *rest):
    y_hbm, xbuf, ybuf, in_sem, out_sem = rest[-5:]
    e = pl.program_id(0)
    ne = first_ref.shape[0] - 1
    nbuf, rows = xbuf.shape[:2]
    de = wd_ref.shape[1]
    b0, b1 = first_ref[e], first_ref[e + 1]
    n_total = first_ref[ne]

    def in_copy(g):
        return pltpu.make_async_copy(xs_hbm.at[pl.ds(g * rows, rows)], xbuf.at[g % nbuf], in_sem.at[g % nbuf])

    def out_copy(g):
        return pltpu.make_async_copy(ybuf.at[g % nbuf], y_hbm.at[pl.ds(g * rows, rows)], out_sem.at[g % nbuf])

    @pl.when(e == 0)
    def _():
        for g in range(nbuf - 1):
            @pl.when(g < n_total)
            def _():
                in_copy(g).start()

    def block(g, carry):
        slot = g % nbuf

        @pl.when(g + nbuf - 1 < n_total)
        def _():
            in_copy(g + nbuf - 1).start()

        in_copy(g).wait()

        @pl.when(g >= nbuf)
        def _():
            out_copy(g - nbuf).wait()

        valid = cnt_ref[e] - (g - b0) * rows
        rid = lax.broadcasted_iota(I32, (rows, 1), 0)
        xs = jnp.where(rid < valid, _unpack_bf16_pairs(xbuf[slot]), 0.0).astype(BF16)
        gu = _dot(xs, wgu_ref[0])
        hb = _silu(gu[:, :de]) * gu[:, de:]
        ybuf[slot] = _pack_bf16_pairs(_dot(hb.astype(BF16), wd_ref[0]))
        out_copy(g).start()
        return carry

    lax.fori_loop(b0, b1, block, 0)

    @pl.when(e == ne - 1)
    def _():
        for back in range(nbuf, 0, -1):
            @pl.when(n_total >= back)
            def _():
                out_copy(n_total - back).wait()


def _experts(xs, w_gu, w_down, first_block, counts, after):
    p, dw = xs.shape
    ne, de, d = w_down.shape
    extra = [] if after is None else [after]
    return pl.pallas_call(
        _expert_kernel,
        grid_spec=pltpu.PrefetchScalarGridSpec(
            num_scalar_prefetch=2,
            grid=(ne,),
            in_specs=[pl.BlockSpec(memory_space=pl.ANY),
                      pl.BlockSpec((1, d, 2 * de), lambda e, fb, cn: (e, 0, 0)),
                      pl.BlockSpec((1, de, d), lambda e, fb, cn: (e, 0, 0))]
                     + [pl.BlockSpec(memory_space=pl.ANY)] * len(extra),
            out_specs=pl.BlockSpec(memory_space=pl.ANY),
            scratch_shapes=[pltpu.VMEM((EXPERT_BUFFERS, EXPERT_ROWS, dw), I32),
                            pltpu.VMEM((EXPERT_BUFFERS, EXPERT_ROWS, dw), I32),
                            pltpu.SemaphoreType.DMA((EXPERT_BUFFERS,)), pltpu.SemaphoreType.DMA((EXPERT_BUFFERS,))]),
        out_shape=jax.ShapeDtypeStruct((p, dw), I32),
        compiler_params=pltpu.CompilerParams(dimension_semantics=("arbitrary",),
                                             vmem_limit_bytes=VMEM_LIMIT),
        name="experts",
    )(first_block, counts, xs, w_gu, w_down, *extra)


def _sc_first_chunk(n_chunks):
    per_worker = n_chunks // SC_WORKERS
    assert per_worker * SC_WORKERS == n_chunks
    return per_worker, (lax.axis_index("s") * SC_CORES + lax.axis_index("c")) * per_worker


def _sc_start(copies):
    for cp in copies:
        cp.start()


def _sc_wait(copies):
    for cp in copies:
        cp.wait()


def _sc_dispatch(h2, dest, n_rows):
    w = h2.shape[1]
    n_chunks, k_top, ch = dest.shape
    mesh = plsc.VectorSubcoreMesh(core_axis_name="c", subcore_axis_name="s")

    @functools.partial(
        pl.kernel, mesh=mesh, out_type=jax.ShapeDtypeStruct((n_rows, w), I32),
        scratch_types=[pltpu.VMEM((2, k_top, ch), I32), pltpu.VMEM((2, ch, w), I32),
                       pltpu.SemaphoreType.DMA((2,)), pltpu.SemaphoreType.DMA((2,))])
    def dispatch(h2_hbm, dest_hbm, xs_hbm, idx_v, rows_v, load_sem, store_sem):
        per_worker, first = _sc_first_chunk(n_chunks)

        def loads(i):
            slot = i % 2
            return (pltpu.make_async_copy(h2_hbm.at[pl.ds((first + i) * ch, ch)], rows_v.at[slot], load_sem.at[slot]),
                    pltpu.make_async_copy(dest_hbm.at[first + i], idx_v.at[slot], load_sem.at[slot]))

        def scatters(i):
            slot = i % 2
            return [pltpu.make_async_copy(rows_v.at[slot], xs_hbm.at[idx_v.at[slot, k]], store_sem.at[slot])
                    for k in range(k_top)]

        for i in range(min(2, per_worker)):
            _sc_start(loads(i))
        for i in range(per_worker):
            _sc_wait(loads(i))
            _sc_start(scatters(i))
            if 1 <= i < per_worker - 1:
                _sc_wait(scatters(i - 1))
                _sc_start(loads(i + 1))
        for i in range(max(per_worker - 2, 0), per_worker):
            _sc_wait(scatters(i))

    return dispatch(h2, dest)


def _sc_combine_gather(y, dest, t):
    w = y.shape[1]
    n_chunks, k_top, ch = dest.shape
    n_local = n_chunks // SC_WORKERS
    mesh = plsc.VectorSubcoreMesh(core_axis_name="c", subcore_axis_name="s")

    @functools.partial(
        pl.kernel, mesh=mesh, out_type=jax.ShapeDtypeStruct((k_top, t, w), I32),
        scratch_types=[pltpu.VMEM((n_local, k_top, ch), I32), pltpu.VMEM((k_top, ch, w), I32),
                       pltpu.SemaphoreType.DMA((k_top,)), pltpu.SemaphoreType.DMA((k_top,))])
    def gather(y_hbm, dest_hbm, out_hbm, idx_v, rows_v, load_sem, store_sem):
        per_worker, first = _sc_first_chunk(n_chunks)
        pltpu.sync_copy(dest_hbm.at[pl.ds(first, per_worker)], idx_v)

        def fetch(i, k):
            return pltpu.make_async_copy(y_hbm.at[idx_v.at[i, k]], rows_v.at[k], load_sem.at[k])

        def put(i, k):
            return pltpu.make_async_copy(rows_v.at[k], out_hbm.at[k, pl.ds((first + i) * ch, ch)], store_sem.at[k])

        @pl.loop(0, per_worker)
        def _(i):
            for k in range(k_top):
                @pl.when(i > 0)
                def _():
                    put(i - 1, k).wait()

                fetch(i, k).start()
            for k in range(k_top):
                fetch(i, k).wait()
                put(i, k).start()

        for k in range(k_top):
            put(per_worker - 1, k).wait()

    return gather(y, dest)


def _final_kernel(xmid_ref, yg_ref, wt_ref, g2_ref, shf_ref, scf_ref, gf_ref, *rest):
    o_ref = rest[-1]
    wt = wt_ref[...].T[:xmid_ref.shape[0]]
    routed = wt[:, 0:1] * _unpack_bf16_pairs(yg_ref[0])
    for k in range(1, yg_ref.shape[0]):
        routed = routed + wt[:, k:k + 1] * _unpack_bf16_pairs(yg_ref[k])
    x2 = xmid_ref[...] + g2_ref[...] * routed
    o_ref[...] = _rms_norm(x2, gf_ref[...]) * (1.0 + scf_ref[...]) + shf_ref[...]


def _final_prompt(xmid, yg, wts_t, mod3, modf3, mod_row0, g_final, b0, tok_b0, n_batch, out_prev):
    b, seq, d = xmid.shape
    k_top, _, w = yg.shape
    tl = FINAL_TILE
    nl = seq // tl

    def vec(col):
        return pl.BlockSpec((None, 1, d), lambda i, j: (i + b0 + mod_row0, 0, col))

    in_specs = [pl.BlockSpec((None, tl, d), lambda i, j: (i, j, 0)),
                pl.BlockSpec((k_top, tl, w), lambda i, j: (0, (i + b0 - tok_b0) * nl + j, 0)),
                pl.BlockSpec((k_top, tl), lambda i, j: (0, (i + b0 - tok_b0) * nl + j)),
                vec(5), vec(0), vec(1), pl.BlockSpec((1, d), lambda i, j: (0, 0))]
    args = [xmid, yg, wts_t, mod3, modf3, modf3, g_final]
    aliases = {}
    if out_prev is not None:
        in_specs.append(pl.BlockSpec(memory_space=pl.ANY))
        args.append(out_prev)
        aliases = {7: 0}
    return pl.pallas_call(
        _final_kernel,
        grid=(b, nl),
        in_specs=in_specs,
        out_specs=pl.BlockSpec((None, tl, d), lambda i, j: (i + b0, j, 0)),
        out_shape=jax.ShapeDtypeStruct((n_batch, seq, d), F32),
        compiler_params=pltpu.CompilerParams(dimension_semantics=("arbitrary", "arbitrary"),
                                             vmem_limit_bytes=VMEM_LIMIT),
        input_output_aliases=aliases,
        name="final_prompt",
    )(*args)


def _final_sample(xmid_all, yg, wts_t, mod, modf, g_final, part, rows, first_row, after):
    d = xmid_all.shape[1]
    k_top, _, w = yg.shape

    def vec(col):
        return pl.BlockSpec((rows, d), lambda i: (part, col))

    return pl.pallas_call(
        _final_kernel,
        grid=(1,),
        in_specs=[vec(0),
                  pl.BlockSpec((k_top, rows, w), lambda i: (0, first_row // rows, 0)),
                  pl.BlockSpec((k_top, 128), lambda i: (0, first_row // 128)),
                  vec(5), vec(0), vec(1), pl.BlockSpec((1, d), lambda i: (0, 0)), pl.BlockSpec(memory_space=pl.ANY)],
        out_specs=pl.BlockSpec((rows, d), lambda i: (0, 0)),
        out_shape=jax.ShapeDtypeStruct((rows, d), F32),
        compiler_params=pltpu.CompilerParams(dimension_semantics=("arbitrary",),
                                             vmem_limit_bytes=VMEM_LIMIT),
        name="final_sample",
    )(xmid_all, yg, wts_t, mod, modf, modf, g_final, after)


def _block_diag_pairs(w_pool):
    g, c, _ = w_pool.shape
    eye = jnp.eye(g, dtype=w_pool.dtype)
    return (eye[:, None, :, None] * w_pool[:, :, None, :]).reshape(g * c, g * c)


def kernel(x_prompt, x_sample, state_pool, state_conv, c_prompt, c_sample, w_ada, b_ada, g_mix, w_in, w_pool, pool_scale, w_dw, b_dw, ln_g, ln_b, w_out, g_ffn, w_router, b_router, w_gate, w_up, w_down, ws_gate, ws_up, ws_down, w_ada_final, b_ada_final, g_final):
    bp, seq, d = x_prompt.shape
    bs = x_sample.shape[0]
    depth = w_ada.shape[0]
    assert depth == 1 and x_sample.shape[1] == 1
    conv_k = w_dw.shape[1]
    ne = w_router.shape[-1]
    per = ne // N_EXPERT_GROUPS
    tp = bp * seq
    t_all = tp + bs

    row_expert = (jnp.arange(ne) % N_EXPERT_GROUPS) * per + jnp.arange(ne) // N_EXPERT_GROUPS
    wl = {
        "g_mix": g_mix[0][None, :],
        "in": w_in[0].astype(BF16),
        "pool": _block_diag_pairs(w_pool[0]).astype(BF16),
        "pool_scale": pool_scale[0][None, :],
        "dw": w_dw[0],
        "b_dw": b_dw[0][None, :],
        "ln_g": ln_g[0][None, :],
        "ln_b": ln_b[0][None, :],
        "out": w_out[0].astype(BF16),
        "g_ffn": g_ffn[0][None, :],
        "router_t": w_router[0].T[row_expert].astype(BF16),
        "s_gu": jnp.concatenate([ws_gate[0], ws_up[0]], axis=1).astype(BF16),
        "s_down": ws_down[0].astype(BF16),
    }
    b_perm = b_router[0][row_expert][:, None]

    assert bp % N_PARTS == 0 and bs % N_PARTS == 0
    bpp, bsp = bp // N_PARTS, bs // N_PARTS
    tpp = bpp * seq
    t_part = tpp + bsp
    grain = SC_WORKERS * SC_DISPATCH_CHUNK * SC_COMBINE_CHUNK // math.gcd(SC_DISPATCH_CHUNK, SC_COMBINE_CHUNK)
    t_pad = (t_part + grain - 1) // grain * grain
    assert t_pad % ROUTE_TILE == 0 and t_pad % DEST_TILE == 0 and tpp % bsp == 0

    c_all = jnp.concatenate([c_sample, c_prompt], axis=0)
    mod = _ada(c_all, w_ada[0], b_ada[0])
    modf = _ada(c_all, w_ada_final, b_ada_final)
    mod3, modf3 = mod[:, None, :], modf[:, None, :]

    xmid_s, new_pool_t, new_conv_t, tails = _sample_mixer(
        x_sample.reshape(bs, d), mod, jnp.transpose(state_pool[0], (1, 0, 2)),
        jnp.transpose(state_conv[0], (1, 0, 2)), wl, tpp, t_pad, conv_k=conv_k)
    new_pool_s = jnp.transpose(new_pool_t, (1, 0, 2))
    new_conv_s = jnp.transpose(new_conv_t, (1, 0, 2))
    w_experts = tuple(a.reshape(a.shape[1:]) for a in (w_gate, w_up, w_down))

    n_blocks = (t_part * TOP_K + EXPERT_ROWS - 1) // EXPERT_ROWS + ne
    n_spare = -(-(t_pad - t_part) * TOP_K // EXPERT_ROWS)
    n_rows = (n_blocks + n_spare) * EXPERT_ROWS

    def routed_layout(lgt):
        idx, wts, rank, counts_perm = _route(lgt, b_perm, t_part)
        counts = jnp.zeros((ne,), I32).at[row_expert].set(counts_perm[:, 0].astype(I32))
        nblk = (counts + EXPERT_ROWS - 1) // EXPERT_ROWS
        first_block = jnp.concatenate([jnp.zeros((1,), I32), jnp.cumsum(nblk).astype(I32)])
        dest = _dest_rows(first_block[:ne] * EXPERT_ROWS, idx, rank, t_part, n_blocks * EXPERT_ROWS)

        def chunked(ch):
            return dest.reshape(TOP_K, t_pad // ch, ch).transpose(1, 0, 2)

        return wts, first_block, counts, chunked(SC_DISPATCH_CHUNK), chunked(SC_COMBINE_CHUNK)

    n_steps = seq // SEQ_TILE * bp
    per_step = ne // n_steps
    assert per_step * n_steps == ne
    mix = functools.partial(_prompt_mixer, x_prompt, mod3, bs, wl, conv_k=conv_k)
    y_prompt, y_samples, npools, nconvs, w_bf, mixed = None, [], [], [], None, []
    for p in range(N_PARTS):
        xmid_p, h2, lgt, npool_p, nconv_p, *w_bf = mix(*tails[p], p * bpp, bpp, p * bpp, w_experts,
                                                       (p * ne // N_PARTS, per_step), w_bf, None)
        mixed.append((xmid_p, h2, lgt))
        npools.append(npool_p)
        nconvs.append(nconv_p)
    for p in range(N_PARTS):
        xmid_p, h2, lgt = mixed[p]
        wts, first_block, counts, dest_d, dest_c = routed_layout(lgt)
        xs = _sc_dispatch(h2, dest_d, n_rows)
        y = _experts(xs, w_bf[0], w_bf[1], first_block, counts, None)
        yg = _sc_combine_gather(y, dest_c, t_pad)
        y_samples.append(_final_sample(xmid_s, yg, wts, mod, modf, g_final[None, :], p, bsp, tpp,
                                       g_final if y_prompt is None else y_prompt))
        y_prompt = _final_prompt(xmid_p, yg, wts, mod3, modf3, bs, g_final[None, :], p * bpp, p * bpp, bp, y_prompt)
    y_sample = jnp.concatenate(y_samples, axis=0)
    npool_p = jnp.concatenate(npools, axis=0)
    nconv_p = jnp.concatenate(nconvs, axis=0)

    return (y_prompt, y_sample[:, None, :], npool_p[None], nconv_p[None], new_pool_s[None], new_conv_s[None])
```

```python
import functools
import math

import jax
import jax.numpy as jnp
from jax import lax
from jax.experimental import pallas as pl
from jax.experimental.pallas import tpu as pltpu
from jax.experimental.pallas import tpu_sc as plsc

POOL_WINDOWS = (2, 4, 8, 16)
N_EXPERT_GROUPS = 8
TOPK_GROUPS = 4
TOP_K = 8
ROUTED_SCALE = 2.5
EPS = 1e-6

LANES = 128
SUBLANES = 8
BF16_ROWS = 16
VMEM_LIMIT = 52 * 1024 * 1024

SEQ_TILE = 512
ROW_CHUNK = 16
ROUTE_TILE = 384
EXPERT_ROWS = 256
N_PARTS = 2
EXPERT_BUFFERS = 6
FINAL_TILE = 512
DEST_TILE = 2816

SC_CORES = 2
SC_SUBCORES = 16
SC_WORKERS = SC_CORES * SC_SUBCORES
SC_DISPATCH_CHUNK = 88
SC_COMBINE_CHUNK = 24

F32 = jnp.float32
BF16 = jnp.bfloat16
I32 = jnp.int32
U32 = jnp.uint32
NEG_INF = float("-inf")
HI16 = 0xFFFF0000


def _sigmoid(x):
    return 1.0 / (1.0 + jnp.exp(-x))


def _silu(x):
    return x * _sigmoid(x)


def _rms_norm(x, g):
    return x * lax.rsqrt(jnp.mean(x * x, axis=-1, keepdims=True) + EPS) * g


def _dot(a, b):
    return jnp.dot(a, b, preferred_element_type=F32)


def _pack_bf16_pairs(x):
    w = x.shape[1] // 2
    bits = lax.bitcast_convert_type(x.astype(BF16).astype(F32), U32)
    return lax.bitcast_convert_type((bits[:, :w] >> 16) | (bits[:, w:] & U32(HI16)), I32)


def _unpack_bf16_pairs(p):
    bits = lax.bitcast_convert_type(p, U32)
    lo = lax.bitcast_convert_type(bits << 16, F32)
    hi = lax.bitcast_convert_type(bits & U32(HI16), F32)
    return jnp.concatenate([lo, hi], axis=1)


def _ada_kernel(c_ref, w_ref, b_ref, o_ref):
    s = _silu(c_ref[...])
    o_ref[...] = _dot(s.astype(BF16), w_ref[...].astype(BF16)) + b_ref[...]


def _ada(c, w, b):
    rows, d = c.shape
    n = w.shape[1]
    tn = 1024
    return pl.pallas_call(
        _ada_kernel,
        grid=(n // tn,),
        in_specs=[pl.BlockSpec((rows, d), lambda j: (0, 0)),
                  pl.BlockSpec((d, tn), lambda j: (0, j)),
                  pl.BlockSpec((1, tn), lambda j: (0, j))],
        out_specs=pl.BlockSpec((rows, tn), lambda j: (0, j)),
        out_shape=jax.ShapeDtypeStruct((rows, n), F32),
        compiler_params=pltpu.CompilerParams(dimension_semantics=("arbitrary",),
                                             vmem_limit_bytes=VMEM_LIMIT),
        name="ada",
    )(c, w, b.reshape(1, n))


def _mixer_tail(x, pool_d, conv_act, mod, w, xmid_ref, h2_ref, lgt_ref):
    sh2, sc2, g1, g2 = mod
    pw = pool_d.shape[1]
    pool_out = _dot(pool_d, w["pool"][...]) * w["pool_scale"][...]
    mix = _dot(pool_out.astype(BF16), w["out"][:pw, :]) + _dot(conv_act, w["out"][pw:, :])
    x1 = x + g1 * mix
    h2f = _rms_norm(x1, w["g_ffn"][...]) * (1.0 + sc2) + sh2
    h2_ref[...] = _pack_bf16_pairs(h2f)
    h2 = h2f.astype(BF16)
    lgt_ref[...] = lax.dot_general(w["router_t"][...], h2, (((1,), (1,)), ((), ())),
                                   preferred_element_type=F32)
    gu = _dot(h2, w["s_gu"][...])
    de = gu.shape[1] // 2
    hs = _silu(gu[:, :de]) * gu[:, de:]
    shared = _dot(hs.astype(BF16), w["s_down"][...])
    xmid_ref[...] = x1 + g2 * shared


_W_NAMES = ("g_mix", "in", "pool", "pool_scale", "dw", "b_dw", "ln_g", "ln_b", "out", "g_ffn",
            "router_t", "s_gu", "s_down")


def _layer_norm_silu(yc, g, b):
    mu = jnp.mean(yc, axis=-1, keepdims=True)
    yz = yc - mu
    var = jnp.mean(yz * yz, axis=-1, keepdims=True)
    return _silu(yz * lax.rsqrt(var + EPS) * g + b)


U_HALO = 32
A_HALO = 32


def _prompt_mixer_kernel(x_ref, mod_ref, *refs, tl, d, pw, cw, conv_k):
    nw = len(_W_NAMES)
    w = dict(zip(_W_NAMES, refs[:nw]))
    wg_ref, wu_ref, wd_ref = refs[nw + 2:nw + 5]
    xmid_ref, h2_ref, lgt_ref, npool_ref, nconv_ref, wgu_bf_ref, wd_bf_ref = refs[-16:-9]
    ubuf, s2buf, s4buf, s8buf, abuf, ashift, dbuf, cbuf, rowb = refs[-9:]
    de = wg_ref.shape[2]
    wgu_bf_ref[:, :, :de] = wg_ref[...].astype(BF16)
    wgu_bf_ref[:, :, de:] = wu_ref[...].astype(BF16)
    wd_bf_ref[...] = wd_ref[...].astype(BF16)
    l = pl.program_id(1)
    pool_buf = max(POOL_WINDOWS) - 1
    uh, ah = U_HALO, A_HALO
    pg = pw // len(POOL_WINDOWS)
    nu, na = uh + tl, ah + tl

    @pl.when(l == 0)
    def _():
        ubuf[0:uh, :] = jnp.zeros((uh, pw), F32)
        abuf[0:ah, :] = jnp.zeros((ah, cw), F32)

    @pl.when(l > 0)
    def _():
        ubuf[0:uh, :] = ubuf[tl:tl + uh, :]
        abuf[0:ah, :] = abuf[tl:tl + ah, :]

    x = x_ref[0]
    sh1 = mod_ref[0, :, 0 * d:1 * d]
    sc1 = mod_ref[0, :, 1 * d:2 * d]
    g1 = mod_ref[0, :, 2 * d:3 * d]
    sh2 = mod_ref[0, :, 3 * d:4 * d]
    sc2 = mod_ref[0, :, 4 * d:5 * d]
    g2 = mod_ref[0, :, 5 * d:6 * d]

    h = _rms_norm(x, w["g_mix"][...]) * (1.0 + sc1) + sh1
    z = _dot(h.astype(BF16), w["in"][...])
    u = z[:, :pw]
    ubuf[uh:nu, :] = u
    abuf[ah:na, :] = z[:, pw:pw + cw] * _sigmoid(z[:, pw + cw:])

    s2buf[8:nu, :] = ubuf[8:nu, :] + ubuf[7:nu - 1, :]
    s4buf[16:nu, :] = s2buf[16:nu, pg:] + s2buf[14:nu - 2, pg:]
    s8buf[24:nu, :] = s4buf[24:nu, pg:] + s4buf[20:nu - 4, pg:]
    t = l * tl + lax.broadcasted_iota(I32, (tl, 1), 0)
    sums = (s2buf[uh:nu, 0:pg], s4buf[uh:nu, 0:pg], s8buf[uh:nu, 0:pg],
            s8buf[uh:nu, pg:2 * pg] + s8buf[uh - 8:nu - 8, pg:2 * pg])
    for g, win in enumerate(POOL_WINDOWS):
        cols = slice(g * pg, (g + 1) * pg)
        inv = 1.0 / jnp.minimum(win, t + 1).astype(F32)
        dbuf[:, cols] = (sums[g] * inv - u[:, cols]).astype(BF16)

    for r in range(1, SUBLANES):
        ashift[r - 1, 8:na, :] = abuf[8 - r:na - r, :]

    for k in range(conv_k):
        rowb[k] = jnp.broadcast_to(w["dw"][k:k + 1, :], (SUBLANES, cw))
    rowb[conv_k] = jnp.broadcast_to(w["b_dw"][...], (SUBLANES, cw))

    for c in range(tl // ROW_CHUNK):
        groups = []
        for o in range(ah + c * ROW_CHUNK, ah + (c + 1) * ROW_CHUNK, SUBLANES):
            yc = abuf[o:o + SUBLANES, :] * rowb[conv_k - 1]
            for back in range(1, conv_k):
                q, r = divmod(back, SUBLANES)
                src = abuf if r == 0 else ashift.at[r - 1]
                yc = yc + src[o - q * SUBLANES:o - (q - 1) * SUBLANES, :] * rowb[conv_k - 1 - back]
            groups.append(yc + rowb[conv_k])
        r0 = c * ROW_CHUNK
        cbuf[r0:r0 + ROW_CHUNK, :] = _layer_norm_silu(
            jnp.concatenate(groups, axis=0), w["ln_g"][...], w["ln_b"][...]).astype(BF16)

    @pl.when(l == pl.num_programs(1) - 1)
    def _():
        npool_ref[0] = ubuf[nu - pool_buf:nu, :]
        nconv_ref[0] = abuf[na - (conv_k - 1):na, :]

    _mixer_tail(x, dbuf[...], cbuf[...], (sh2, sc2, g1, g2), w, xmid_ref.at[0], h2_ref, lgt_ref)


def _sample_mixer_kernel(x_ref, mod_ref, sp_ref, sc_ref, *refs, d, pw, cw, conv_k):
    nw = len(_W_NAMES)
    w = dict(zip(_W_NAMES, refs[:nw]))
    xmid_ref, npool_ref, nconv_ref = refs[nw:nw + 3]
    tails = refs[nw + 3:-2]
    h2_tmp, lgt_tmp = refs[-2:]
    pool_buf = max(POOL_WINDOWS) - 1
    pg = pw // len(POOL_WINDOWS)
    x = x_ref[...]
    sh1, sc1, g1, sh2, sc2, g2 = [mod_ref[:, i * d:(i + 1) * d] for i in range(6)]
    h = _rms_norm(x, w["g_mix"][...]) * (1.0 + sc1) + sh1
    z = _dot(h.astype(BF16), w["in"][...])
    u = z[:, :pw]
    a = z[:, pw:pw + cw] * _sigmoid(z[:, pw + cw:])
    npool_ref[0:pool_buf - 1] = sp_ref[1:pool_buf]
    npool_ref[pool_buf - 1] = u
    nconv_ref[0:conv_k - 2] = sc_ref[1:conv_k - 1]
    nconv_ref[conv_k - 2] = a
    ds = []
    for g, win in enumerate(POOL_WINDOWS):
        cols = slice(g * pg, (g + 1) * pg)
        acc = u[:, cols]
        for j in range(1, win):
            acc = acc + sp_ref[pool_buf - j, :, cols]
        cnt = float(min(win, 1 + pool_buf))
        ds.append(acc / cnt - u[:, cols])
    pool_d = jnp.concatenate(ds, axis=-1).astype(BF16)
    yc = a * w["dw"][conv_k - 1:conv_k, :]
    for k in range(conv_k - 1):
        yc = yc + sc_ref[k] * w["dw"][k:k + 1, :]
    yc = yc + w["b_dw"][...]
    conv_act = _layer_norm_silu(yc, w["ln_g"][...], w["ln_b"][...]).astype(BF16)
    _mixer_tail(x, pool_d, conv_act, (sh2, sc2, g1, g2), w, xmid_ref, h2_tmp, lgt_tmp)
    n_parts = len(tails) // 2
    share = x.shape[0] // n_parts
    for p in range(n_parts):
        h2_ref, lgt_ref = tails[2 * p], tails[2 * p + 1]
        h2_ref[...] = jnp.zeros_like(h2_ref)
        lgt_ref[...] = jnp.zeros_like(lgt_ref)
        h2_ref[0:share, :] = h2_tmp[p * share:(p + 1) * share, :]
        lgt_ref[:, 0:share] = lgt_tmp[:, p * share:(p + 1) * share]


def _full_spec(a):
    nd = a.ndim
    return pl.BlockSpec(a.shape, lambda *_: (0,) * nd)


def _prompt_mixer(x, mod3, mod_row0, wl, h2_all, lgt_all, b0, b, w_experts, w_bf_prev, *, conv_k):
    n_batch, seq, d = x.shape
    w_gate, w_up, w_down = w_experts
    n_exp, _, de = w_gate.shape
    pw = wl["pool"].shape[0]
    cw = wl["dw"].shape[1]
    ne = wl["router_t"].shape[0]
    tl = SEQ_TILE
    nl = seq // tl
    pg = pw // len(POOL_WINDOWS)
    assert POOL_WINDOWS == (2, 4, 8, 16) and conv_k - 1 <= A_HALO
    ws = [wl[n] for n in _W_NAMES]
    kern = functools.partial(_prompt_mixer_kernel, tl=tl, d=d, pw=pw, cw=cw, conv_k=conv_k)
    ecs = n_exp // (n_batch * nl)
    assert ecs * n_batch * nl == n_exp

    def w_map(i, j):
        return ((i + b0) * nl + j, 0, 0)

    ins = [x, mod3, *ws, h2_all, lgt_all, w_gate, w_up, w_down]
    in_specs = ([pl.BlockSpec((1, tl, d), lambda i, j: (i + b0, j, 0)),
                 pl.BlockSpec((1, 1, mod3.shape[-1]), lambda i, j: (i + b0 + mod_row0, 0, 0))]
                + [_full_spec(a) for a in ws]
                + [pl.BlockSpec(memory_space=pl.ANY), pl.BlockSpec(memory_space=pl.ANY),
                   pl.BlockSpec((ecs, d, de), w_map), pl.BlockSpec((ecs, d, de), w_map),
                   pl.BlockSpec((ecs, de, d), w_map)])
    aliases = {2 + len(ws): 1, 3 + len(ws): 2}
    if w_bf_prev is not None:
        aliases.update({len(ins): 5, len(ins) + 1: 6})
        ins += list(w_bf_prev)
        in_specs += [pl.BlockSpec(memory_space=pl.ANY), pl.BlockSpec(memory_space=pl.ANY)]
    return pl.pallas_call(
        kern,
        grid=(b, nl),
        in_specs=in_specs,
        out_specs=[pl.BlockSpec((1, tl, d), lambda i, j: (i, j, 0)),
                   pl.BlockSpec((tl, d // 2), lambda i, j: (i * nl + j, 0)),
                   pl.BlockSpec((ne, tl), lambda i, j: (0, i * nl + j)),
                   pl.BlockSpec((1, max(POOL_WINDOWS) - 1, pw), lambda i, j: (i, 0, 0)),
                   pl.BlockSpec((1, conv_k - 1, cw), lambda i, j: (i, 0, 0)),
                   pl.BlockSpec((ecs, d, 2 * de), w_map), pl.BlockSpec((ecs, de, d), w_map)],
        out_shape=[jax.ShapeDtypeStruct((b, seq, d), F32),
                   jax.ShapeDtypeStruct(h2_all.shape, I32),
                   jax.ShapeDtypeStruct(lgt_all.shape, F32),
                   jax.ShapeDtypeStruct((b, max(POOL_WINDOWS) - 1, pw), F32),
                   jax.ShapeDtypeStruct((b, conv_k - 1, cw), F32),
                   jax.ShapeDtypeStruct((n_exp, d, 2 * de), BF16),
                   jax.ShapeDtypeStruct((n_exp, de, d), BF16)],
        scratch_shapes=[pltpu.VMEM((U_HALO + tl, pw), F32), pltpu.VMEM((U_HALO + tl, pw), F32),
                        pltpu.VMEM((U_HALO + tl, pw - pg), F32), pltpu.VMEM((U_HALO + tl, pw - 2 * pg), F32),
                        pltpu.VMEM((A_HALO + tl, cw), F32), pltpu.VMEM((SUBLANES - 1, A_HALO + tl, cw), F32),
                        pltpu.VMEM((tl, pw), BF16), pltpu.VMEM((tl, cw), BF16),
                        pltpu.VMEM((conv_k + 1, SUBLANES, cw), F32)],
        compiler_params=pltpu.CompilerParams(dimension_semantics=("arbitrary", "arbitrary"),
                                             vmem_limit_bytes=VMEM_LIMIT),
        input_output_aliases=aliases,
        name="prompt_mixer",
    )(*ins)


def _sample_mixer(x, mod, sp, sc, wl, t_prompt, t_pad, *, conv_k):
    rows, d = x.shape
    tail = t_pad - t_prompt
    assert t_prompt % tail == 0 and rows % N_PARTS == 0 and rows // N_PARTS <= tail
    pw = wl["pool"].shape[0]
    cw = wl["dw"].shape[1]
    ne = wl["router_t"].shape[0]
    ws = [wl[n] for n in _W_NAMES]
    kern = functools.partial(_sample_mixer_kernel, d=d, pw=pw, cw=cw, conv_k=conv_k)
    ins = [x, mod, sp, sc] + ws
    tail_specs = [pl.BlockSpec((tail, d // 2), lambda i: (t_prompt // tail, 0)),
                  pl.BlockSpec((ne, tail), lambda i: (0, t_prompt // tail))] * N_PARTS
    tail_shapes = [jax.ShapeDtypeStruct((t_pad, d // 2), I32), jax.ShapeDtypeStruct((ne, t_pad), F32)] * N_PARTS
    outs = pl.pallas_call(
        kern,
        grid=(1,),
        in_specs=[_full_spec(x), pl.BlockSpec((rows, mod.shape[1]), lambda i: (0, 0))]
                 + [_full_spec(a) for a in ins[2:]],
        out_specs=[pl.BlockSpec((rows, d), lambda i: (0, 0)), _full_spec(sp), _full_spec(sc)] + tail_specs,
        out_shape=[jax.ShapeDtypeStruct((rows, d), F32),
                   jax.ShapeDtypeStruct(sp.shape, F32),
                   jax.ShapeDtypeStruct(sc.shape, F32)] + tail_shapes,
        scratch_shapes=[pltpu.VMEM((rows, d // 2), I32), pltpu.VMEM((ne, rows), F32)],
        compiler_params=pltpu.CompilerParams(dimension_semantics=("arbitrary",),
                                             vmem_limit_bytes=VMEM_LIMIT),
        name="sample_mixer",
    )(*ins)
    return outs[0], outs[1], outs[2], [(outs[3 + 2 * p], outs[4 + 2 * p]) for p in range(N_PARTS)]


def _sublane_max(x):
    return jnp.max(x, axis=0, keepdims=True)


def _route_kernel(lg_ref, b_ref, tri_ref, idx_ref, wt_ref, rank_ref, cnt_ref, carry, *, t_valid):
    i = pl.program_id(0)
    ne, tt = lg_ref.shape
    ng = N_EXPERT_GROUPS
    per = ne // ng

    @pl.when(i == 0)
    def _():
        carry[...] = jnp.zeros_like(carry)

    s = _sigmoid(lg_ref[...])
    sel = s + b_ref[...]
    s3 = [s[p * ng:(p + 1) * ng, :] for p in range(per)]
    sel3 = [sel[p * ng:(p + 1) * ng, :] for p in range(per)]
    m1 = sel3[0]
    m2 = jnp.full_like(m1, NEG_INF)
    for p in range(1, per):
        m2 = jnp.maximum(m2, jnp.minimum(m1, sel3[p]))
        m1 = jnp.maximum(m1, sel3[p])
    gs = m1 + m2
    gi = lax.broadcasted_iota(jnp.int32, (ng, tt), 0)
    beaten = jnp.zeros((ng, tt), jnp.int32)
    for g in range(ng):
        row = gs[g:g + 1, :]
        beats = (row > gs) | ((row == gs) & (gi > g))
        beaten = beaten + beats.astype(jnp.int32)
    keep = beaten < TOPK_GROUPS
    cur = [jnp.where(keep, sel3[p], NEG_INF) for p in range(per)]
    eid = [(gi * per + p).astype(F32) for p in range(per)]
    idxs, wts, hits = [], [], []
    for _ in range(TOP_K):
        m = cur[0]
        for p in range(1, per):
            m = jnp.maximum(m, cur[p])
        m = _sublane_max(m)
        cand = jnp.where(cur[0] == m, eid[0], float(ne))
        for p in range(1, per):
            cand = jnp.minimum(cand, jnp.where(cur[p] == m, eid[p], float(ne)))
        e_sel = jnp.min(cand, axis=0, keepdims=True)
        hit = [eid[p] == e_sel for p in range(per)]
        wk = jnp.where(hit[0], s3[0], 0.0)
        for p in range(1, per):
            wk = wk + jnp.where(hit[p], s3[p], 0.0)
        wts.append(jnp.sum(wk, axis=0, keepdims=True))
        cur = [jnp.where(hit[p], NEG_INF, cur[p]) for p in range(per)]
        idxs.append(e_sel)
        hits.append(hit)
    wsum = wts[0]
    for k in range(1, TOP_K):
        wsum = wsum + wts[k]
    chosen = [functools.reduce(lambda a, b: a | b, [hits[k][p] for k in range(TOP_K)]) for p in range(per)]
    real = (i * tt + lax.broadcasted_iota(I32, (1, tt), 1)) < t_valid
    onehot = jnp.concatenate([(c & real).astype(F32) for c in chosen], axis=0)
    before = _dot(onehot.astype(BF16), tri_ref[...]) + carry[...]
    for k in range(TOP_K):
        rk = jnp.where(hits[k][0], before[0:ng, :], 0.0)
        for p in range(1, per):
            rk = rk + jnp.where(hits[k][p], before[p * ng:(p + 1) * ng, :], 0.0)
        rank_ref[k:k + 1, :] = jnp.sum(rk, axis=0, keepdims=True).astype(jnp.int32)
        idx_ref[k:k + 1, :] = idxs[k].astype(jnp.int32)
        wt_ref[k:k + 1, :] = wts[k] / wsum * ROUTED_SCALE
    carry[...] = carry[...] + jnp.sum(onehot, axis=1, keepdims=True)
    cnt_ref[...] = carry[...]


def _route(lgt, b_perm, t_valid):
    ne, t = lgt.shape
    tt = ROUTE_TILE
    tri = (jnp.arange(tt)[:, None] < jnp.arange(tt)[None, :]).astype(BF16)
    return pl.pallas_call(
        functools.partial(_route_kernel, t_valid=t_valid),
        grid=(t // tt,),
        in_specs=[pl.BlockSpec((ne, tt), lambda i: (0, i)),
                  pl.BlockSpec((ne, 1), lambda i: (0, 0)),
                  pl.BlockSpec((tt, tt), lambda i: (0, 0))],
        out_specs=[pl.BlockSpec((TOP_K, tt), lambda i: (0, i)),
                   pl.BlockSpec((TOP_K, tt), lambda i: (0, i)),
                   pl.BlockSpec((TOP_K, tt), lambda i: (0, i)),
                   pl.BlockSpec((ne, 1), lambda i: (0, 0))],
        out_shape=[jax.ShapeDtypeStruct((TOP_K, t), jnp.int32),
                   jax.ShapeDtypeStruct((TOP_K, t), F32),
                   jax.ShapeDtypeStruct((TOP_K, t), jnp.int32),
                   jax.ShapeDtypeStruct((ne, 1), F32)],
        scratch_shapes=[pltpu.VMEM((ne, 1), F32)],
        compiler_params=pltpu.CompilerParams(dimension_semantics=("arbitrary",),
                                             vmem_limit_bytes=VMEM_LIMIT),
        name="route",
    )(lgt, b_perm, tri)


def _dest_kernel(pstart_ref, idx_ref, rank_ref, dest_ref, *, t_valid, spare_row):
    idx = idx_ref[...]
    k_top, tile = idx.shape

    def add_start(e, acc):
        return acc + jnp.where(idx == e, pstart_ref[e], 0)

    dest = lax.fori_loop(0, pstart_ref.shape[0], add_start, rank_ref[...])
    tok = pl.program_id(0) * tile + lax.broadcasted_iota(I32, idx.shape, 1)
    spare = spare_row + (tok - t_valid) * k_top + lax.broadcasted_iota(I32, idx.shape, 0)
    dest_ref[...] = jnp.where(tok < t_valid, dest, spare)


def _dest_rows(pstart, idx, rank, t_valid, spare_row):
    k_top, t = idx.shape
    tile = DEST_TILE
    spec = pl.BlockSpec((k_top, tile), lambda i, ps: (0, i))
    return pl.pallas_call(
        functools.partial(_dest_kernel, t_valid=t_valid, spare_row=spare_row),
        grid_spec=pltpu.PrefetchScalarGridSpec(num_scalar_prefetch=1, grid=(t // tile,),
                                               in_specs=[spec, spec], out_specs=spec),
        out_shape=jax.ShapeDtypeStruct((k_top, t), I32),
        compiler_params=pltpu.CompilerParams(dimension_semantics=("arbitrary",),
                                             vmem_limit_bytes=VMEM_LIMIT),
        name="dest_rows",
    )(pstart, idx, rank)


def _expert_kernel(first_ref, cnt_ref, xs_hbm, wgu_ref, wd_ref, y_hbm, xbuf, ybuf, in_sem, out_sem):
    e = pl.program_id(0)
    ne = first_ref.shape[0] - 1
    nbuf, rows = xbuf.shape[:2]
    de = wd_ref.shape[1]
    b0, b1 = first_ref[e], first_ref[e + 1]
    n_total = first_ref[ne]

    def in_copy(g):
        return pltpu.make_async_copy(xs_hbm.at[pl.ds(g * rows, rows)], xbuf.at[g % nbuf], in_sem.at[g % nbuf])

    def out_copy(g):
        return pltpu.make_async_copy(ybuf.at[g % nbuf], y_hbm.at[pl.ds(g * rows, rows)], out_sem.at[g % nbuf])

    @pl.when(e == 0)
    def _():
        for g in range(nbuf - 1):
            @pl.when(g < n_total)
            def _():
                in_copy(g).start()

    def block(g, carry):
        slot = g % nbuf

        @pl.when(g + nbuf - 1 < n_total)
        def _():
            in_copy(g + nbuf - 1).start()

        in_copy(g).wait()

        @pl.when(g >= nbuf)
        def _():
            out_copy(g - nbuf).wait()

        valid = cnt_ref[e] - (g - b0) * rows
        rid = lax.broadcasted_iota(I32, (rows, 1), 0)
        xs = jnp.where(rid < valid, _unpack_bf16_pairs(xbuf[slot]), 0.0).astype(BF16)
        gu = _dot(xs, wgu_ref[0])
        hb = _silu(gu[:, :de]) * gu[:, de:]
        ybuf[slot] = _pack_bf16_pairs(_dot(hb.astype(BF16), wd_ref[0]))
        out_copy(g).start()
        return carry

    lax.fori_loop(b0, b1, block, 0)

    @pl.when(e == ne - 1)
    def _():
        for back in range(nbuf, 0, -1):
            @pl.when(n_total >= back)
            def _():
                out_copy(n_total - back).wait()


def _experts(xs, w_gu, w_down, first_block, counts):
    p, dw = xs.shape
    ne, de, d = w_down.shape
    return pl.pallas_call(
        _expert_kernel,
        grid_spec=pltpu.PrefetchScalarGridSpec(
            num_scalar_prefetch=2,
            grid=(ne,),
            in_specs=[pl.BlockSpec(memory_space=pl.ANY),
                      pl.BlockSpec((1, d, 2 * de), lambda e, fb, cn: (e, 0, 0)),
                      pl.BlockSpec((1, de, d), lambda e, fb, cn: (e, 0, 0))],
            out_specs=pl.BlockSpec(memory_space=pl.ANY),
            scratch_shapes=[pltpu.VMEM((EXPERT_BUFFERS, EXPERT_ROWS, dw), I32),
                            pltpu.VMEM((EXPERT_BUFFERS, EXPERT_ROWS, dw), I32),
                            pltpu.SemaphoreType.DMA((EXPERT_BUFFERS,)), pltpu.SemaphoreType.DMA((EXPERT_BUFFERS,))]),
        out_shape=jax.ShapeDtypeStruct((p, dw), I32),
        compiler_params=pltpu.CompilerParams(dimension_semantics=("arbitrary",),
                                             vmem_limit_bytes=VMEM_LIMIT),
        name="experts",
    )(first_block, counts, xs, w_gu, w_down)


def _sc_first_chunk(n_chunks):
    per_worker = n_chunks // SC_WORKERS
    assert per_worker * SC_WORKERS == n_chunks
    return per_worker, (lax.axis_index("s") * SC_CORES + lax.axis_index("c")) * per_worker


def _sc_start(copies):
    for cp in copies:
        cp.start()


def _sc_wait(copies):
    for cp in copies:
        cp.wait()


def _sc_dispatch(h2, dest, n_rows):
    w = h2.shape[1]
    n_chunks, k_top, ch = dest.shape
    mesh = plsc.VectorSubcoreMesh(core_axis_name="c", subcore_axis_name="s")

    @functools.partial(
        pl.kernel, mesh=mesh, out_type=jax.ShapeDtypeStruct((n_rows, w), I32),
        scratch_types=[pltpu.VMEM((2, k_top, ch), I32), pltpu.VMEM((2, ch, w), I32),
                       pltpu.SemaphoreType.DMA((2,)), pltpu.SemaphoreType.DMA((2,))])
    def dispatch(h2_hbm, dest_hbm, xs_hbm, idx_v, rows_v, load_sem, store_sem):
        per_worker, first = _sc_first_chunk(n_chunks)

        def loads(i):
            slot = i % 2
            return (pltpu.make_async_copy(h2_hbm.at[pl.ds((first + i) * ch, ch)], rows_v.at[slot], load_sem.at[slot]),
                    pltpu.make_async_copy(dest_hbm.at[first + i], idx_v.at[slot], load_sem.at[slot]))

        def scatters(i):
            slot = i % 2
            return [pltpu.make_async_copy(rows_v.at[slot], xs_hbm.at[idx_v.at[slot, k]], store_sem.at[slot])
                    for k in range(k_top)]

        for i in range(min(2, per_worker)):
            _sc_start(loads(i))
        for i in range(per_worker):
            _sc_wait(loads(i))
            _sc_start(scatters(i))
            if 1 <= i < per_worker - 1:
                _sc_wait(scatters(i - 1))
                _sc_start(loads(i + 1))
        for i in range(max(per_worker - 2, 0), per_worker):
            _sc_wait(scatters(i))

    return dispatch(h2, dest)


def _sc_combine_gather(y, dest, t):
    w = y.shape[1]
    n_chunks, k_top, ch = dest.shape
    n_local = n_chunks // SC_WORKERS
    mesh = plsc.VectorSubcoreMesh(core_axis_name="c", subcore_axis_name="s")

    @functools.partial(
        pl.kernel, mesh=mesh, out_type=jax.ShapeDtypeStruct((k_top, t, w), I32),
        scratch_types=[pltpu.VMEM((n_local, k_top, ch), I32), pltpu.VMEM((k_top, ch, w), I32),
                       pltpu.SemaphoreType.DMA((k_top,)), pltpu.SemaphoreType.DMA((k_top,))])
    def gather(y_hbm, dest_hbm, out_hbm, idx_v, rows_v, load_sem, store_sem):
        per_worker, first = _sc_first_chunk(n_chunks)
        pltpu.sync_copy(dest_hbm.at[pl.ds(first, per_worker)], idx_v)

        def fetch(i, k):
            return pltpu.make_async_copy(y_hbm.at[idx_v.at[i, k]], rows_v.at[k], load_sem.at[k])

        def put(i, k):
            return pltpu.make_async_copy(rows_v.at[k], out_hbm.at[k, pl.ds((first + i) * ch, ch)], store_sem.at[k])

        @pl.loop(0, per_worker)
        def _(i):
            for k in range(k_top):
                @pl.when(i > 0)
                def _():
                    put(i - 1, k).wait()

                fetch(i, k).start()
            for k in range(k_top):
                fetch(i, k).wait()
                put(i, k).start()

        for k in range(k_top):
            put(per_worker - 1, k).wait()

    return gather(y, dest)


def _final_kernel(xmid_ref, yg_ref, wt_ref, g2_ref, shf_ref, scf_ref, gf_ref, *rest):
    o_ref = rest[-1]
    wt = wt_ref[...].T[:xmid_ref.shape[0]]
    routed = wt[:, 0:1] * _unpack_bf16_pairs(yg_ref[0])
    for k in range(1, yg_ref.shape[0]):
        routed = routed + wt[:, k:k + 1] * _unpack_bf16_pairs(yg_ref[k])
    x2 = xmid_ref[...] + g2_ref[...] * routed
    o_ref[...] = _rms_norm(x2, gf_ref[...]) * (1.0 + scf_ref[...]) + shf_ref[...]


def _final_prompt(xmid, yg, wts_t, mod3, modf3, mod_row0, g_final, b0, n_batch, out_prev):
    b, seq, d = xmid.shape
    k_top, _, w = yg.shape
    tl = FINAL_TILE
    nl = seq // tl

    def vec(col):
        return pl.BlockSpec((None, 1, d), lambda i, j: (i + b0 + mod_row0, 0, col))

    in_specs = [pl.BlockSpec((None, tl, d), lambda i, j: (i, j, 0)),
                pl.BlockSpec((k_top, tl, w), lambda i, j: (0, i * nl + j, 0)),
                pl.BlockSpec((k_top, tl), lambda i, j: (0, i * nl + j)),
                vec(5), vec(0), vec(1), pl.BlockSpec((1, d), lambda i, j: (0, 0))]
    args = [xmid, yg, wts_t, mod3, modf3, modf3, g_final]
    aliases = {}
    if out_prev is not None:
        in_specs.append(pl.BlockSpec(memory_space=pl.ANY))
        args.append(out_prev)
        aliases = {7: 0}
    return pl.pallas_call(
        _final_kernel,
        grid=(b, nl),
        in_specs=in_specs,
        out_specs=pl.BlockSpec((None, tl, d), lambda i, j: (i + b0, j, 0)),
        out_shape=jax.ShapeDtypeStruct((n_batch, seq, d), F32),
        compiler_params=pltpu.CompilerParams(dimension_semantics=("arbitrary", "arbitrary"),
                                             vmem_limit_bytes=VMEM_LIMIT),
        input_output_aliases=aliases,
        name="final_prompt",
    )(*args)


def _final_sample(xmid_all, yg, wts_t, mod, modf, g_final, part, rows, first_row, after):
    d = xmid_all.shape[1]
    k_top, _, w = yg.shape

    def vec(col):
        return pl.BlockSpec((rows, d), lambda i: (part, col))

    return pl.pallas_call(
        _final_kernel,
        grid=(1,),
        in_specs=[vec(0),
                  pl.BlockSpec((k_top, rows, w), lambda i: (0, first_row // rows, 0)),
                  pl.BlockSpec((k_top, 128), lambda i: (0, first_row // 128)),
                  vec(5), vec(0), vec(1), pl.BlockSpec((1, d), lambda i: (0, 0)), pl.BlockSpec(memory_space=pl.ANY)],
        out_specs=pl.BlockSpec((rows, d), lambda i: (0, 0)),
        out_shape=jax.ShapeDtypeStruct((rows, d), F32),
        compiler_params=pltpu.CompilerParams(dimension_semantics=("arbitrary",),
                                             vmem_limit_bytes=VMEM_LIMIT),
        name="final_sample",
    )(xmid_all, yg, wts_t, mod, modf, modf, g_final, after)


def _block_diag_pairs(w_pool):
    g, c, _ = w_pool.shape
    eye = jnp.eye(g, dtype=w_pool.dtype)
    return (eye[:, None, :, None] * w_pool[:, :, None, :]).reshape(g * c, g * c)


def kernel(x_prompt, x_sample, state_pool, state_conv, c_prompt, c_sample, w_ada, b_ada, g_mix, w_in, w_pool, pool_scale, w_dw, b_dw, ln_g, ln_b, w_out, g_ffn, w_router, b_router, w_gate, w_up, w_down, ws_gate, ws_up, ws_down, w_ada_final, b_ada_final, g_final):
    bp, seq, d = x_prompt.shape
    bs = x_sample.shape[0]
    depth = w_ada.shape[0]
    assert depth == 1 and x_sample.shape[1] == 1
    conv_k = w_dw.shape[1]
    ne = w_router.shape[-1]
    per = ne // N_EXPERT_GROUPS
    tp = bp * seq
    t_all = tp + bs

    row_expert = (jnp.arange(ne) % N_EXPERT_GROUPS) * per + jnp.arange(ne) // N_EXPERT_GROUPS
    wl = {
        "g_mix": g_mix[0][None, :],
        "in": w_in[0].astype(BF16),
        "pool": _block_diag_pairs(w_pool[0]).astype(BF16),
        "pool_scale": pool_scale[0][None, :],
        "dw": w_dw[0],
        "b_dw": b_dw[0][None, :],
        "ln_g": ln_g[0][None, :],
        "ln_b": ln_b[0][None, :],
        "out": w_out[0].astype(BF16),
        "g_ffn": g_ffn[0][None, :],
        "router_t": w_router[0].T[row_expert].astype(BF16),
        "s_gu": jnp.concatenate([ws_gate[0], ws_up[0]], axis=1).astype(BF16),
        "s_down": ws_down[0].astype(BF16),
    }
    b_perm = b_router[0][row_expert][:, None]

    assert bp % N_PARTS == 0 and bs % N_PARTS == 0
    bpp, bsp = bp // N_PARTS, bs // N_PARTS
    tpp = bpp * seq
    t_part = tpp + bsp
    grain = SC_WORKERS * SC_DISPATCH_CHUNK * SC_COMBINE_CHUNK // math.gcd(SC_DISPATCH_CHUNK, SC_COMBINE_CHUNK)
    t_pad = (t_part + grain - 1) // grain * grain
    assert t_pad % ROUTE_TILE == 0 and t_pad % DEST_TILE == 0 and tpp % bsp == 0

    c_all = jnp.concatenate([c_sample, c_prompt], axis=0)
    mod = _ada(c_all, w_ada[0], b_ada[0])
    modf = _ada(c_all, w_ada_final, b_ada_final)
    mod3, modf3 = mod[:, None, :], modf[:, None, :]

    xmid_s, new_pool_t, new_conv_t, tails = _sample_mixer(
        x_sample.reshape(bs, d), mod, jnp.transpose(state_pool[0], (1, 0, 2)),
        jnp.transpose(state_conv[0], (1, 0, 2)), wl, tpp, t_pad, conv_k=conv_k)
    new_pool_s = jnp.transpose(new_pool_t, (1, 0, 2))
    new_conv_s = jnp.transpose(new_conv_t, (1, 0, 2))
    w_experts = tuple(a.reshape(a.shape[1:]) for a in (w_gate, w_up, w_down))

    n_blocks = (t_part * TOP_K + EXPERT_ROWS - 1) // EXPERT_ROWS + ne
    n_spare = -(-(t_pad - t_part) * TOP_K // EXPERT_ROWS)
    n_rows = (n_blocks + n_spare) * EXPERT_ROWS

    y_prompt, y_samples, npools, nconvs, w_bf = None, [], [], [], None
    mixed = []
    for p in range(N_PARTS):
        h2, lgt = tails[p]
        xmid_p, h2, lgt, npool_p, nconv_p, *w_bf = _prompt_mixer(
            x_prompt, mod3, bs, wl, h2, lgt, p * bpp, bpp, w_experts, w_bf, conv_k=conv_k)
        mixed.append((xmid_p, h2, lgt))
        npools.append(npool_p)
        nconvs.append(nconv_p)
    for p in range(N_PARTS):
        xmid_p, h2, lgt = mixed[p]
        idx, wts, rank, counts_perm = _route(lgt, b_perm, t_part)
        counts = jnp.zeros((ne,), I32).at[row_expert].set(counts_perm[:, 0].astype(I32))
        nblk = (counts + EXPERT_ROWS - 1) // EXPERT_ROWS
        first_block = jnp.concatenate([jnp.zeros((1,), I32), jnp.cumsum(nblk).astype(I32)])
        dest = _dest_rows(first_block[:ne] * EXPERT_ROWS, idx, rank, t_part, n_blocks * EXPERT_ROWS)

        def chunked(ch):
            return dest.reshape(TOP_K, t_pad // ch, ch).transpose(1, 0, 2)

        xs = _sc_dispatch(h2, chunked(SC_DISPATCH_CHUNK), n_rows)
        y = _experts(xs, w_bf[0], w_bf[1], first_block, counts)
        yg = _sc_combine_gather(y, chunked(SC_COMBINE_CHUNK), t_pad)

        y_samples.append(_final_sample(xmid_s, yg, wts, mod, modf, g_final[None, :], p, bsp, tpp,
                                       g_final if y_prompt is None else y_prompt))
        y_prompt = _final_prompt(xmid_p, yg, wts, mod3, modf3, bs, g_final[None, :], p * bpp, bp, y_prompt)
    y_sample = jnp.concatenate(y_samples, axis=0)
    npool_p = jnp.concatenate(npools, axis=0)
    nconv_p = jnp.concatenate(nconvs, axis=0)

    return (y_prompt, y_sample[:, None, :], npool_p[None], nconv_p[None], new_pool_s[None], new_conv_s[None])
```

```python
import functools
import math

import jax
import jax.numpy as jnp
from jax import lax
from jax.experimental import pallas as pl
from jax.experimental.pallas import tpu as pltpu
from jax.experimental.pallas import tpu_sc as plsc

POOL_WINDOWS = (2, 4, 8, 16)
N_EXPERT_GROUPS = 8
TOPK_GROUPS = 4
TOP_K = 8
ROUTED_SCALE = 2.5
EPS = 1e-6

LANES = 128
SUBLANES = 8
BF16_ROWS = 16
VMEM_LIMIT = 52 * 1024 * 1024

SEQ_TILE = 512
ROW_CHUNK = 16
ROUTE_TILE = 1408
EXPERT_ROWS = 512
N_PARTS = 2
EXPERT_BUFFERS = 6
FINAL_TILE = 512
DEST_TILE = 2816

SC_CORES = 2
SC_SUBCORES = 16
SC_WORKERS = SC_CORES * SC_SUBCORES
SC_DISPATCH_CHUNK = 88
SC_COMBINE_CHUNK = 24

F32 = jnp.float32
BF16 = jnp.bfloat16
I32 = jnp.int32
U32 = jnp.uint32
NEG_INF = float("-inf")
HI16 = 0xFFFF0000


def _sigmoid(x):
    return 1.0 / (1.0 + jnp.exp(-x))


def _silu(x):
    return x * _sigmoid(x)


def _rms_norm(x, g):
    return x * lax.rsqrt(jnp.mean(x * x, axis=-1, keepdims=True) + EPS) * g


def _dot(a, b):
    return jnp.dot(a, b, preferred_element_type=F32)


def _pack_bf16_pairs(x):
    w = x.shape[1] // 2
    bits = lax.bitcast_convert_type(x.astype(BF16).astype(F32), U32)
    return lax.bitcast_convert_type((bits[:, :w] >> 16) | (bits[:, w:] & U32(HI16)), I32)


def _unpack_bf16_pairs(p):
    bits = lax.bitcast_convert_type(p, U32)
    lo = lax.bitcast_convert_type(bits << 16, F32)
    hi = lax.bitcast_convert_type(bits & U32(HI16), F32)
    return jnp.concatenate([lo, hi], axis=1)


def _ada_kernel(c_ref, w_ref, b_ref, o_ref):
    s = _silu(c_ref[...])
    o_ref[...] = _dot(s.astype(BF16), w_ref[...].astype(BF16)) + b_ref[...]


def _ada(c, w, b):
    rows, d = c.shape
    n = w.shape[1]
    tn = 1024
    return pl.pallas_call(
        _ada_kernel,
        grid=(n // tn,),
        in_specs=[pl.BlockSpec((rows, d), lambda j: (0, 0)),
                  pl.BlockSpec((d, tn), lambda j: (0, j)),
                  pl.BlockSpec((1, tn), lambda j: (0, j))],
        out_specs=pl.BlockSpec((rows, tn), lambda j: (0, j)),
        out_shape=jax.ShapeDtypeStruct((rows, n), F32),
        compiler_params=pltpu.CompilerParams(dimension_semantics=("arbitrary",),
                                             vmem_limit_bytes=VMEM_LIMIT),
        name="ada",
    )(c, w, b.reshape(1, n))


def _mixer_tail(x, pool_d, conv_act, mod, w, xmid_ref, h2_ref, lgt_ref):
    sh2, sc2, g1, g2 = mod
    pw = pool_d.shape[1]
    pool_out = _dot(pool_d, w["pool"][...]) * w["pool_scale"][...]
    mix = _dot(pool_out.astype(BF16), w["out"][:pw, :]) + _dot(conv_act, w["out"][pw:, :])
    x1 = x + g1 * mix
    h2f = _rms_norm(x1, w["g_ffn"][...]) * (1.0 + sc2) + sh2
    h2_ref[...] = _pack_bf16_pairs(h2f)
    h2 = h2f.astype(BF16)
    lgt_ref[...] = lax.dot_general(w["router_t"][...], h2, (((1,), (1,)), ((), ())),
                                   preferred_element_type=F32)
    gu = _dot(h2, w["s_gu"][...])
    de = gu.shape[1] // 2
    hs = _silu(gu[:, :de]) * gu[:, de:]
    shared = _dot(hs.astype(BF16), w["s_down"][...])
    xmid_ref[...] = x1 + g2 * shared


_W_NAMES = ("g_mix", "in", "pool", "pool_scale", "dw", "b_dw", "ln_g", "ln_b", "out", "g_ffn",
            "router_t", "s_gu", "s_down")


def _layer_norm_silu(yc, g, b):
    mu = jnp.mean(yc, axis=-1, keepdims=True)
    yz = yc - mu
    var = jnp.mean(yz * yz, axis=-1, keepdims=True)
    return _silu(yz * lax.rsqrt(var + EPS) * g + b)


U_HALO = 32
A_HALO = 32


def _prompt_mixer_kernel(x_ref, mod_ref, *refs, tl, d, pw, cw, conv_k, mod_row0):
    nw = len(_W_NAMES)
    w = dict(zip(_W_NAMES, refs[:nw]))
    wg_ref, wu_ref, wd_ref = refs[nw + 2:nw + 5]
    xmid_ref, h2_ref, lgt_ref, npool_ref, nconv_ref, wgu_bf_ref, wd_bf_ref = refs[-16:-9]
    ubuf, s2buf, s4buf, s8buf, abuf, ashift, dbuf, cbuf, rowb = refs[-9:]
    de = wg_ref.shape[2]
    wgu_bf_ref[:, :, :de] = wg_ref[...].astype(BF16)
    wgu_bf_ref[:, :, de:] = wu_ref[...].astype(BF16)
    wd_bf_ref[...] = wd_ref[...].astype(BF16)
    l = pl.program_id(1)
    pool_buf = max(POOL_WINDOWS) - 1
    uh, ah = U_HALO, A_HALO
    pg = pw // len(POOL_WINDOWS)
    nu, na = uh + tl, ah + tl

    @pl.when(l == 0)
    def _():
        ubuf[0:uh, :] = jnp.zeros((uh, pw), F32)
        abuf[0:ah, :] = jnp.zeros((ah, cw), F32)

    @pl.when(l > 0)
    def _():
        ubuf[0:uh, :] = ubuf[tl:tl + uh, :]
        abuf[0:ah, :] = abuf[tl:tl + ah, :]

    x = x_ref[0]
    row = pl.ds((mod_row0 + pl.program_id(0)) % SUBLANES, 1)
    sh1, sc1, g1, sh2, sc2, g2 = [mod_ref[row, i * d:(i + 1) * d] for i in range(6)]

    h = _rms_norm(x, w["g_mix"][...]) * (1.0 + sc1) + sh1
    z = _dot(h.astype(BF16), w["in"][...])
    u = z[:, :pw]
    ubuf[uh:nu, :] = u
    abuf[ah:na, :] = z[:, pw:pw + cw] * _sigmoid(z[:, pw + cw:])

    s2buf[8:nu, :] = ubuf[8:nu, :] + ubuf[7:nu - 1, :]
    s4buf[16:nu, :] = s2buf[16:nu, pg:] + s2buf[14:nu - 2, pg:]
    s8buf[24:nu, :] = s4buf[24:nu, pg:] + s4buf[20:nu - 4, pg:]
    t = l * tl + lax.broadcasted_iota(I32, (tl, 1), 0)
    sums = (s2buf[uh:nu, 0:pg], s4buf[uh:nu, 0:pg], s8buf[uh:nu, 0:pg],
            s8buf[uh:nu, pg:2 * pg] + s8buf[uh - 8:nu - 8, pg:2 * pg])
    for g, win in enumerate(POOL_WINDOWS):
        cols = slice(g * pg, (g + 1) * pg)
        inv = 1.0 / jnp.minimum(win, t + 1).astype(F32)
        dbuf[:, cols] = (sums[g] * inv - u[:, cols]).astype(BF16)

    for r in range(1, SUBLANES):
        ashift[r - 1, 8:na, :] = abuf[8 - r:na - r, :]

    for k in range(conv_k):
        rowb[k] = jnp.broadcast_to(w["dw"][k:k + 1, :], (SUBLANES, cw))
    rowb[conv_k] = jnp.broadcast_to(w["b_dw"][...], (SUBLANES, cw))

    for c in range(tl // ROW_CHUNK):
        groups = []
        for o in range(ah + c * ROW_CHUNK, ah + (c + 1) * ROW_CHUNK, SUBLANES):
            yc = abuf[o:o + SUBLANES, :] * rowb[conv_k - 1]
            for back in range(1, conv_k):
                q, r = divmod(back, SUBLANES)
                src = abuf if r == 0 else ashift.at[r - 1]
                yc = yc + src[o - q * SUBLANES:o - (q - 1) * SUBLANES, :] * rowb[conv_k - 1 - back]
            groups.append(yc + rowb[conv_k])
        r0 = c * ROW_CHUNK
        cbuf[r0:r0 + ROW_CHUNK, :] = _layer_norm_silu(
            jnp.concatenate(groups, axis=0), w["ln_g"][...], w["ln_b"][...]).astype(BF16)

    @pl.when(l == pl.num_programs(1) - 1)
    def _():
        npool_ref[0] = ubuf[nu - pool_buf:nu, :]
        nconv_ref[0] = abuf[na - (conv_k - 1):na, :]

    _mixer_tail(x, dbuf[...], cbuf[...], (sh2, sc2, g1, g2), w, xmid_ref.at[0], h2_ref, lgt_ref)


def _sample_mixer_kernel(x_ref, mod_ref, sp_ref, sc_ref, *refs, d, pw, cw, conv_k):
    nw = len(_W_NAMES)
    w = dict(zip(_W_NAMES, refs[:nw]))
    xmid_ref, npool_ref, nconv_ref = refs[nw:nw + 3]
    tails = refs[nw + 3:-2]
    h2_tmp, lgt_tmp = refs[-2:]
    pool_buf = max(POOL_WINDOWS) - 1
    pg = pw // len(POOL_WINDOWS)
    x = x_ref[...]
    sh1, sc1, g1, sh2, sc2, g2 = [mod_ref[:, i * d:(i + 1) * d] for i in range(6)]
    h = _rms_norm(x, w["g_mix"][...]) * (1.0 + sc1) + sh1
    z = _dot(h.astype(BF16), w["in"][...])
    u = z[:, :pw]
    a = z[:, pw:pw + cw] * _sigmoid(z[:, pw + cw:])
    npool_ref[0:pool_buf - 1] = sp_ref[1:pool_buf]
    npool_ref[pool_buf - 1] = u
    nconv_ref[0:conv_k - 2] = sc_ref[1:conv_k - 1]
    nconv_ref[conv_k - 2] = a
    ds = []
    for g, win in enumerate(POOL_WINDOWS):
        cols = slice(g * pg, (g + 1) * pg)
        acc = u[:, cols]
        for j in range(1, win):
            acc = acc + sp_ref[pool_buf - j, :, cols]
        cnt = float(min(win, 1 + pool_buf))
        ds.append(acc / cnt - u[:, cols])
    pool_d = jnp.concatenate(ds, axis=-1).astype(BF16)
    yc = a * w["dw"][conv_k - 1:conv_k, :]
    for k in range(conv_k - 1):
        yc = yc + sc_ref[k] * w["dw"][k:k + 1, :]
    yc = yc + w["b_dw"][...]
    conv_act = _layer_norm_silu(yc, w["ln_g"][...], w["ln_b"][...]).astype(BF16)
    _mixer_tail(x, pool_d, conv_act, (sh2, sc2, g1, g2), w, xmid_ref, h2_tmp, lgt_tmp)
    n_parts = len(tails) // 2
    share = x.shape[0] // n_parts
    for p in range(n_parts):
        h2_ref, lgt_ref = tails[2 * p], tails[2 * p + 1]
        h2_ref[...] = jnp.zeros_like(h2_ref)
        lgt_ref[...] = jnp.zeros_like(lgt_ref)
        h2_ref[0:share, :] = h2_tmp[p * share:(p + 1) * share, :]
        lgt_ref[:, 0:share] = lgt_tmp[:, p * share:(p + 1) * share]


def _full_spec(a):
    nd = a.ndim
    return pl.BlockSpec(a.shape, lambda *_: (0,) * nd)


def _prompt_mixer(x, mod, mod_row0, wl, h2_all, lgt_all, b0, b, w_experts, w_bf_prev, *, conv_k):
    n_batch, seq, d = x.shape
    w_gate, w_up, w_down = w_experts
    n_exp, _, de = w_gate.shape
    pw = wl["pool"].shape[0]
    cw = wl["dw"].shape[1]
    ne = wl["router_t"].shape[0]
    tl = SEQ_TILE
    nl = seq // tl
    pg = pw // len(POOL_WINDOWS)
    assert POOL_WINDOWS == (2, 4, 8, 16) and conv_k - 1 <= A_HALO
    ws = [wl[n] for n in _W_NAMES]
    kern = functools.partial(_prompt_mixer_kernel, tl=tl, d=d, pw=pw, cw=cw, conv_k=conv_k, mod_row0=mod_row0 + b0)
    ecs = n_exp // (n_batch * nl)
    assert ecs * n_batch * nl == n_exp

    def w_map(i, j):
        return ((i + b0) * nl + j, 0, 0)

    ins = [x, mod, *ws, h2_all, lgt_all, w_gate, w_up, w_down]
    in_specs = ([pl.BlockSpec((1, tl, d), lambda i, j: (i + b0, j, 0)),
                 pl.BlockSpec((SUBLANES, mod.shape[-1]), lambda i, j: ((i + b0 + mod_row0) // SUBLANES, 0))]
                + [_full_spec(a) for a in ws]
                + [pl.BlockSpec(memory_space=pl.ANY), pl.BlockSpec(memory_space=pl.ANY),
                   pl.BlockSpec((ecs, d, de), w_map), pl.BlockSpec((ecs, d, de), w_map),
                   pl.BlockSpec((ecs, de, d), w_map)])
    aliases = {2 + len(ws): 1, 3 + len(ws): 2}
    if w_bf_prev is not None:
        aliases.update({len(ins): 5, len(ins) + 1: 6})
        ins += list(w_bf_prev)
        in_specs += [pl.BlockSpec(memory_space=pl.ANY), pl.BlockSpec(memory_space=pl.ANY)]
    return pl.pallas_call(
        kern,
        grid=(b, nl),
        in_specs=in_specs,
        out_specs=[pl.BlockSpec((1, tl, d), lambda i, j: (i, j, 0)),
                   pl.BlockSpec((tl, d // 2), lambda i, j: (i * nl + j, 0)),
                   pl.BlockSpec((ne, tl), lambda i, j: (0, i * nl + j)),
                   pl.BlockSpec((1, max(POOL_WINDOWS) - 1, pw), lambda i, j: (i, 0, 0)),
                   pl.BlockSpec((1, conv_k - 1, cw), lambda i, j: (i, 0, 0)),
                   pl.BlockSpec((ecs, d, 2 * de), w_map), pl.BlockSpec((ecs, de, d), w_map)],
        out_shape=[jax.ShapeDtypeStruct((b, seq, d), F32),
                   jax.ShapeDtypeStruct(h2_all.shape, I32),
                   jax.ShapeDtypeStruct(lgt_all.shape, F32),
                   jax.ShapeDtypeStruct((b, max(POOL_WINDOWS) - 1, pw), F32),
                   jax.ShapeDtypeStruct((b, conv_k - 1, cw), F32),
                   jax.ShapeDtypeStruct((n_exp, d, 2 * de), BF16),
                   jax.ShapeDtypeStruct((n_exp, de, d), BF16)],
        scratch_shapes=[pltpu.VMEM((U_HALO + tl, pw), F32), pltpu.VMEM((U_HALO + tl, pw), F32),
                        pltpu.VMEM((U_HALO + tl, pw - pg), F32), pltpu.VMEM((U_HALO + tl, pw - 2 * pg), F32),
                        pltpu.VMEM((A_HALO + tl, cw), F32), pltpu.VMEM((SUBLANES - 1, A_HALO + tl, cw), F32),
                        pltpu.VMEM((tl, pw), BF16), pltpu.VMEM((tl, cw), BF16),
                        pltpu.VMEM((conv_k + 1, SUBLANES, cw), F32)],
        compiler_params=pltpu.CompilerParams(dimension_semantics=("arbitrary", "arbitrary"),
                                             vmem_limit_bytes=VMEM_LIMIT),
        input_output_aliases=aliases,
        name="prompt_mixer",
    )(*ins)


def _sample_mixer(x, mod, sp, sc, wl, t_prompt, t_pad, *, conv_k):
    rows, d = x.shape
    tail = t_pad - t_prompt
    assert t_prompt % tail == 0 and rows % N_PARTS == 0 and rows // N_PARTS <= tail
    pw = wl["pool"].shape[0]
    cw = wl["dw"].shape[1]
    ne = wl["router_t"].shape[0]
    ws = [wl[n] for n in _W_NAMES]
    kern = functools.partial(_sample_mixer_kernel, d=d, pw=pw, cw=cw, conv_k=conv_k)
    ins = [x, mod, sp, sc] + ws
    tail_specs = [pl.BlockSpec((tail, d // 2), lambda i: (t_prompt // tail, 0)),
                  pl.BlockSpec((ne, tail), lambda i: (0, t_prompt // tail))] * N_PARTS
    tail_shapes = [jax.ShapeDtypeStruct((t_pad, d // 2), I32), jax.ShapeDtypeStruct((ne, t_pad), F32)] * N_PARTS
    outs = pl.pallas_call(
        kern,
        grid=(1,),
        in_specs=[_full_spec(x), pl.BlockSpec((rows, mod.shape[1]), lambda i: (0, 0))]
                 + [_full_spec(a) for a in ins[2:]],
        out_specs=[pl.BlockSpec((rows, d), lambda i: (0, 0)), _full_spec(sp), _full_spec(sc)] + tail_specs,
        out_shape=[jax.ShapeDtypeStruct((rows, d), F32),
                   jax.ShapeDtypeStruct(sp.shape, F32),
                   jax.ShapeDtypeStruct(sc.shape, F32)] + tail_shapes,
        scratch_shapes=[pltpu.VMEM((rows, d // 2), I32), pltpu.VMEM((ne, rows), F32)],
        compiler_params=pltpu.CompilerParams(dimension_semantics=("arbitrary",),
                                             vmem_limit_bytes=VMEM_LIMIT),
        name="sample_mixer",
    )(*ins)
    return outs[0], outs[1], outs[2], [(outs[3 + 2 * p], outs[4 + 2 * p]) for p in range(N_PARTS)]


def _sublane_max(x):
    return jnp.max(x, axis=0, keepdims=True)


def _route_kernel(lg_ref, b_ref, tri_ref, idx_ref, wt_ref, rank_ref, cnt_ref, carry, *, t_valid):
    i = pl.program_id(0)
    ne, tt = lg_ref.shape
    ng = N_EXPERT_GROUPS
    per = ne // ng

    @pl.when(i == 0)
    def _():
        carry[...] = jnp.zeros_like(carry)

    s = _sigmoid(lg_ref[...])
    sel = s + b_ref[...]
    s3 = [s[p * ng:(p + 1) * ng, :] for p in range(per)]
    sel3 = [sel[p * ng:(p + 1) * ng, :] for p in range(per)]
    m1 = sel3[0]
    m2 = jnp.full_like(m1, NEG_INF)
    for p in range(1, per):
        m2 = jnp.maximum(m2, jnp.minimum(m1, sel3[p]))
        m1 = jnp.maximum(m1, sel3[p])
    gs = m1 + m2
    gi = lax.broadcasted_iota(jnp.int32, (ng, tt), 0)
    beaten = jnp.zeros((ng, tt), jnp.int32)
    for g in range(ng):
        row = gs[g:g + 1, :]
        beats = (row > gs) | ((row == gs) & (gi > g))
        beaten = beaten + beats.astype(jnp.int32)
    keep = beaten < TOPK_GROUPS
    cur = [jnp.where(keep, sel3[p], NEG_INF) for p in range(per)]
    eid = [(gi * per + p).astype(F32) for p in range(per)]
    idxs, wts, hits = [], [], []
    for _ in range(TOP_K):
        m = cur[0]
        for p in range(1, per):
            m = jnp.maximum(m, cur[p])
        m = _sublane_max(m)
        cand = jnp.where(cur[0] == m, eid[0], float(ne))
        for p in range(1, per):
            cand = jnp.minimum(cand, jnp.where(cur[p] == m, eid[p], float(ne)))
        e_sel = jnp.min(cand, axis=0, keepdims=True)
        hit = [eid[p] == e_sel for p in range(per)]
        wk = jnp.where(hit[0], s3[0], 0.0)
        for p in range(1, per):
            wk = wk + jnp.where(hit[p], s3[p], 0.0)
        wts.append(jnp.sum(wk, axis=0, keepdims=True))
        cur = [jnp.where(hit[p], NEG_INF, cur[p]) for p in range(per)]
        idxs.append(e_sel)
        hits.append(hit)
    wsum = wts[0]
    for k in range(1, TOP_K):
        wsum = wsum + wts[k]
    chosen = [functools.reduce(lambda a, b: a | b, [hits[k][p] for k in range(TOP_K)]) for p in range(per)]
    real = (i * tt + lax.broadcasted_iota(I32, (1, tt), 1)) < t_valid
    onehot = jnp.concatenate([(c & real).astype(F32) for c in chosen], axis=0)
    before = _dot(onehot.astype(BF16), tri_ref[...]) + carry[...]
    for k in range(TOP_K):
        rk = jnp.where(hits[k][0], before[0:ng, :], 0.0)
        for p in range(1, per):
            rk = rk + jnp.where(hits[k][p], before[p * ng:(p + 1) * ng, :], 0.0)
        rank_ref[k:k + 1, :] = jnp.sum(rk, axis=0, keepdims=True).astype(jnp.int32)
        idx_ref[k:k + 1, :] = idxs[k].astype(jnp.int32)
        wt_ref[k:k + 1, :] = wts[k] / wsum * ROUTED_SCALE
    carry[...] = carry[...] + jnp.sum(onehot, axis=1, keepdims=True)
    cnt_ref[...] = carry[...]


def _route(lgt, b_perm, t_valid):
    ne, t = lgt.shape
    tt = ROUTE_TILE
    tri = (jnp.arange(tt)[:, None] < jnp.arange(tt)[None, :]).astype(BF16)
    return pl.pallas_call(
        functools.partial(_route_kernel, t_valid=t_valid),
        grid=(t // tt,),
        in_specs=[pl.BlockSpec((ne, tt), lambda i: (0, i)),
                  pl.BlockSpec((ne, 1), lambda i: (0, 0)),
                  pl.BlockSpec((tt, tt), lambda i: (0, 0))],
        out_specs=[pl.BlockSpec((TOP_K, tt), lambda i: (0, i)),
                   pl.BlockSpec((TOP_K, tt), lambda i: (0, i)),
                   pl.BlockSpec((TOP_K, tt), lambda i: (0, i)),
                   pl.BlockSpec((ne, 1), lambda i: (0, 0))],
        out_shape=[jax.ShapeDtypeStruct((TOP_K, t), jnp.int32),
                   jax.ShapeDtypeStruct((TOP_K, t), F32),
                   jax.ShapeDtypeStruct((TOP_K, t), jnp.int32),
                   jax.ShapeDtypeStruct((ne, 1), F32)],
        scratch_shapes=[pltpu.VMEM((ne, 1), F32)],
        compiler_params=pltpu.CompilerParams(dimension_semantics=("arbitrary",),
                                             vmem_limit_bytes=VMEM_LIMIT),
        name="route",
    )(lgt, b_perm, tri)


def _dest_kernel(pstart_ref, idx_ref, rank_ref, dest_ref, *, t_valid, spare_row):
    idx = idx_ref[...]
    k_top, tile = idx.shape

    def add_start(e, acc):
        return acc + jnp.where(idx == e, pstart_ref[e], 0)

    dest = lax.fori_loop(0, pstart_ref.shape[0], add_start, rank_ref[...])
    tok = pl.program_id(0) * tile + lax.broadcasted_iota(I32, idx.shape, 1)
    spare = spare_row + (tok - t_valid) * k_top + lax.broadcasted_iota(I32, idx.shape, 0)
    dest_ref[...] = jnp.where(tok < t_valid, dest, spare)


def _dest_rows(pstart, idx, rank, t_valid, spare_row):
    k_top, t = idx.shape
    tile = DEST_TILE
    spec = pl.BlockSpec((k_top, tile), lambda i, ps: (0, i))
    return pl.pallas_call(
        functools.partial(_dest_kernel, t_valid=t_valid, spare_row=spare_row),
        grid_spec=pltpu.PrefetchScalarGridSpec(num_scalar_prefetch=1, grid=(t // tile,),
                                               in_specs=[spec, spec], out_specs=spec),
        out_shape=jax.ShapeDtypeStruct((k_top, t), I32),
        compiler_params=pltpu.CompilerParams(dimension_semantics=("arbitrary",),
                                             vmem_limit_bytes=VMEM_LIMIT),
        name="dest_rows",
    )(pstart, idx, rank)


def _expert_kernel(first_ref, cnt_ref, xs_hbm, wgu_ref, wd_ref, y_hbm, xbuf, ybuf, in_sem, out_sem):
    e = pl.program_id(0)
    ne = first_ref.shape[0] - 1
    nbuf, rows = xbuf.shape[:2]
    de = wd_ref.shape[1]
    b0, b1 = first_ref[e], first_ref[e + 1]
    n_total = first_ref[ne]

    def in_copy(g):
        return pltpu.make_async_copy(xs_hbm.at[pl.ds(g * rows, rows)], xbuf.at[g % nbuf], in_sem.at[g % nbuf])

    def out_copy(g):
        return pltpu.make_async_copy(ybuf.at[g % nbuf], y_hbm.at[pl.ds(g * rows, rows)], out_sem.at[g % nbuf])

    @pl.when(e == 0)
    def _():
        for g in range(nbuf - 1):
            @pl.when(g < n_total)
            def _():
                in_copy(g).start()

    def block(g, carry):
        slot = g % nbuf

        @pl.when(g + nbuf - 1 < n_total)
        def _():
            in_copy(g + nbuf - 1).start()

        in_copy(g).wait()

        @pl.when(g >= nbuf)
        def _():
            out_copy(g - nbuf).wait()

        valid = cnt_ref[e] - (g - b0) * rows
        rid = lax.broadcasted_iota(I32, (rows, 1), 0)
        xs = jnp.where(rid < valid, _unpack_bf16_pairs(xbuf[slot]), 0.0).astype(BF16)
        gu = _dot(xs, wgu_ref[0])
        hb = _silu(gu[:, :de]) * gu[:, de:]
        ybuf[slot] = _pack_bf16_pairs(_dot(hb.astype(BF16), wd_ref[0]))
        out_copy(g).start()
        return carry

    lax.fori_loop(b0, b1, block, 0)

    @pl.when(e == ne - 1)
    def _():
        for back in range(nbuf, 0, -1):
            @pl.when(n_total >= back)
            def _():
                out_copy(n_total - back).wait()


def _experts(xs, w_gu, w_down, first_block, counts):
    p, dw = xs.shape
    ne, de, d = w_down.shape
    return pl.pallas_call(
        _expert_kernel,
        grid_spec=pltpu.PrefetchScalarGridSpec(
            num_scalar_prefetch=2,
            grid=(ne,),
            in_specs=[pl.BlockSpec(memory_space=pl.ANY),
                      pl.BlockSpec((1, d, 2 * de), lambda e, fb, cn: (e, 0, 0)),
                      pl.BlockSpec((1, de, d), lambda e, fb, cn: (e, 0, 0))],
            out_specs=pl.BlockSpec(memory_space=pl.ANY),
            scratch_shapes=[pltpu.VMEM((EXPERT_BUFFERS, EXPERT_ROWS, dw), I32),
                            pltpu.VMEM((EXPERT_BUFFERS, EXPERT_ROWS, dw), I32),
                            pltpu.SemaphoreType.DMA((EXPERT_BUFFERS,)), pltpu.SemaphoreType.DMA((EXPERT_BUFFERS,))]),
        out_shape=jax.ShapeDtypeStruct((p, dw), I32),
        compiler_params=pltpu.CompilerParams(dimension_semantics=("arbitrary",),
                                             vmem_limit_bytes=VMEM_LIMIT),
        name="experts",
    )(first_block, counts, xs, w_gu, w_down)


def _sc_first_chunk(n_chunks):
    per_worker = n_chunks // SC_WORKERS
    assert per_worker * SC_WORKERS == n_chunks
    return per_worker, (lax.axis_index("s") * SC_CORES + lax.axis_index("c")) * per_worker


def _sc_start(copies):
    for cp in copies:
        cp.start()


def _sc_wait(copies):
    for cp in copies:
        cp.wait()


def _sc_dispatch(h2, dest, n_rows):
    w = h2.shape[1]
    n_chunks, k_top, ch = dest.shape
    mesh = plsc.VectorSubcoreMesh(core_axis_name="c", subcore_axis_name="s")

    @functools.partial(
        pl.kernel, mesh=mesh, out_type=jax.ShapeDtypeStruct((n_rows, w), I32),
        scratch_types=[pltpu.VMEM((2, k_top, ch), I32), pltpu.VMEM((2, ch, w), I32),
                       pltpu.SemaphoreType.DMA((2,)), pltpu.SemaphoreType.DMA((2,))])
    def dispatch(h2_hbm, dest_hbm, xs_hbm, idx_v, rows_v, load_sem, store_sem):
        per_worker, first = _sc_first_chunk(n_chunks)

        def loads(i):
            slot = i % 2
            return (pltpu.make_async_copy(h2_hbm.at[pl.ds((first + i) * ch, ch)], rows_v.at[slot], load_sem.at[slot]),
                    pltpu.make_async_copy(dest_hbm.at[first + i], idx_v.at[slot], load_sem.at[slot]))

        def scatters(i):
            slot = i % 2
            return [pltpu.make_async_copy(rows_v.at[slot], xs_hbm.at[idx_v.at[slot, k]], store_sem.at[slot])
                    for k in range(k_top)]

        for i in range(min(2, per_worker)):
            _sc_start(loads(i))
        for i in range(per_worker):
            _sc_wait(loads(i))
            _sc_start(scatters(i))
            if 1 <= i < per_worker - 1:
                _sc_wait(scatters(i - 1))
                _sc_start(loads(i + 1))
        for i in range(max(per_worker - 2, 0), per_worker):
            _sc_wait(scatters(i))

    return dispatch(h2, dest)


def _sc_combine_gather(y, dest, t):
    w = y.shape[1]
    n_chunks, k_top, ch = dest.shape
    n_local = n_chunks // SC_WORKERS
    mesh = plsc.VectorSubcoreMesh(core_axis_name="c", subcore_axis_name="s")

    @functools.partial(
        pl.kernel, mesh=mesh, out_type=jax.ShapeDtypeStruct((k_top, t, w), I32),
        scratch_types=[pltpu.VMEM((n_local, k_top, ch), I32), pltpu.VMEM((k_top, ch, w), I32),
                       pltpu.SemaphoreType.DMA((k_top,)), pltpu.SemaphoreType.DMA((k_top,))])
    def gather(y_hbm, dest_hbm, out_hbm, idx_v, rows_v, load_sem, store_sem):
        per_worker, first = _sc_first_chunk(n_chunks)
        pltpu.sync_copy(dest_hbm.at[pl.ds(first, per_worker)], idx_v)

        def fetch(i, k):
            return pltpu.make_async_copy(y_hbm.at[idx_v.at[i, k]], rows_v.at[k], load_sem.at[k])

        def put(i, k):
            return pltpu.make_async_copy(rows_v.at[k], out_hbm.at[k, pl.ds((first + i) * ch, ch)], store_sem.at[k])

        @pl.loop(0, per_worker)
        def _(i):
            for k in range(k_top):
                @pl.when(i > 0)
                def _():
                    put(i - 1, k).wait()

                fetch(i, k).start()
            for k in range(k_top):
                fetch(i, k).wait()
                put(i, k).start()

        for k in range(k_top):
            put(per_worker - 1, k).wait()

    return gather(y, dest)


def _final_kernel(xmid_ref, yg_ref, wt_ref, g2_ref, shf_ref, scf_ref, gf_ref, *rest, vec_row0=None):
    o_ref = rest[-1]
    if vec_row0 is not None:
        row = pl.ds((vec_row0 + pl.program_id(0)) % SUBLANES, 1)
        g2_ref, shf_ref, scf_ref = g2_ref.at[row, :], shf_ref.at[row, :], scf_ref.at[row, :]
    wt = wt_ref[...].T[:xmid_ref.shape[0]]
    routed = wt[:, 0:1] * _unpack_bf16_pairs(yg_ref[0])
    for k in range(1, yg_ref.shape[0]):
        routed = routed + wt[:, k:k + 1] * _unpack_bf16_pairs(yg_ref[k])
    x2 = xmid_ref[...] + g2_ref[...] * routed
    o_ref[...] = _rms_norm(x2, gf_ref[...]) * (1.0 + scf_ref[...]) + shf_ref[...]


def _final_prompt(xmid, yg, wts_t, mod, modf, mod_row0, g_final, b0, n_batch, out_prev):
    b, seq, d = xmid.shape
    k_top, _, w = yg.shape
    tl = FINAL_TILE
    nl = seq // tl

    def vec(col):
        return pl.BlockSpec((SUBLANES, d), lambda i, j: ((i + b0 + mod_row0) // SUBLANES, col))

    in_specs = [pl.BlockSpec((None, tl, d), lambda i, j: (i, j, 0)),
                pl.BlockSpec((k_top, tl, w), lambda i, j: (0, i * nl + j, 0)),
                pl.BlockSpec((k_top, tl), lambda i, j: (0, i * nl + j)),
                vec(5), vec(0), vec(1), pl.BlockSpec((1, d), lambda i, j: (0, 0))]
    args = [xmid, yg, wts_t, mod, modf, modf, g_final]
    aliases = {}
    if out_prev is not None:
        in_specs.append(pl.BlockSpec(memory_space=pl.ANY))
        args.append(out_prev)
        aliases = {7: 0}
    return pl.pallas_call(
        functools.partial(_final_kernel, vec_row0=b0 + mod_row0),
        grid=(b, nl),
        in_specs=in_specs,
        out_specs=pl.BlockSpec((None, tl, d), lambda i, j: (i + b0, j, 0)),
        out_shape=jax.ShapeDtypeStruct((n_batch, seq, d), F32),
        compiler_params=pltpu.CompilerParams(dimension_semantics=("arbitrary", "arbitrary"),
                                             vmem_limit_bytes=VMEM_LIMIT),
        input_output_aliases=aliases,
        name="final_prompt",
    )(*args)


def _final_sample(xmid_all, yg, wts_t, mod, modf, g_final, part, rows, first_row, after):
    d = xmid_all.shape[1]
    k_top, _, w = yg.shape

    def vec(col):
        return pl.BlockSpec((rows, d), lambda i: (part, col))

    return pl.pallas_call(
        _final_kernel,
        grid=(1,),
        in_specs=[vec(0),
                  pl.BlockSpec((k_top, rows, w), lambda i: (0, first_row // rows, 0)),
                  pl.BlockSpec((k_top, 128), lambda i: (0, first_row // 128)),
                  vec(5), vec(0), vec(1), pl.BlockSpec((1, d), lambda i: (0, 0)), pl.BlockSpec(memory_space=pl.ANY)],
        out_specs=pl.BlockSpec((rows, d), lambda i: (0, 0)),
        out_shape=jax.ShapeDtypeStruct((rows, d), F32),
        compiler_params=pltpu.CompilerParams(dimension_semantics=("arbitrary",),
                                             vmem_limit_bytes=VMEM_LIMIT),
        name="final_sample",
    )(xmid_all, yg, wts_t, mod, modf, modf, g_final, after)


def _block_diag_pairs(w_pool):
    g, c, _ = w_pool.shape
    eye = jnp.eye(g, dtype=w_pool.dtype)
    return (eye[:, None, :, None] * w_pool[:, :, None, :]).reshape(g * c, g * c)


def kernel(x_prompt, x_sample, state_pool, state_conv, c_prompt, c_sample, w_ada, b_ada, g_mix, w_in, w_pool, pool_scale, w_dw, b_dw, ln_g, ln_b, w_out, g_ffn, w_router, b_router, w_gate, w_up, w_down, ws_gate, ws_up, ws_down, w_ada_final, b_ada_final, g_final):
    bp, seq, d = x_prompt.shape
    bs = x_sample.shape[0]
    depth = w_ada.shape[0]
    assert depth == 1 and x_sample.shape[1] == 1
    conv_k = w_dw.shape[1]
    ne = w_router.shape[-1]
    per = ne // N_EXPERT_GROUPS
    tp = bp * seq
    t_all = tp + bs

    row_expert = (jnp.arange(ne) % N_EXPERT_GROUPS) * per + jnp.arange(ne) // N_EXPERT_GROUPS
    wl = {
        "g_mix": g_mix[0][None, :],
        "in": w_in[0].astype(BF16),
        "pool": _block_diag_pairs(w_pool[0]).astype(BF16),
        "pool_scale": pool_scale[0][None, :],
        "dw": w_dw[0],
        "b_dw": b_dw[0][None, :],
        "ln_g": ln_g[0][None, :],
        "ln_b": ln_b[0][None, :],
        "out": w_out[0].astype(BF16),
        "g_ffn": g_ffn[0][None, :],
        "router_t": w_router[0].T[row_expert].astype(BF16),
        "s_gu": jnp.concatenate([ws_gate[0], ws_up[0]], axis=1).astype(BF16),
        "s_down": ws_down[0].astype(BF16),
    }
    b_perm = b_router[0][row_expert][:, None]

    assert bp % N_PARTS == 0 and bs % N_PARTS == 0
    bpp, bsp = bp // N_PARTS, bs // N_PARTS
    tpp = bpp * seq
    t_part = tpp + bsp
    grain = SC_WORKERS * SC_DISPATCH_CHUNK * SC_COMBINE_CHUNK // math.gcd(SC_DISPATCH_CHUNK, SC_COMBINE_CHUNK)
    t_pad = (t_part + grain - 1) // grain * grain
    assert t_pad % ROUTE_TILE == 0 and t_pad % DEST_TILE == 0 and tpp % bsp == 0

    c_all = jnp.concatenate([c_sample, c_prompt], axis=0)
    mod = _ada(c_all, w_ada[0], b_ada[0])
    modf = _ada(c_all, w_ada_final, b_ada_final)

    xmid_s, new_pool_t, new_conv_t, tails = _sample_mixer(
        x_sample.reshape(bs, d), mod, jnp.transpose(state_pool[0], (1, 0, 2)),
        jnp.transpose(state_conv[0], (1, 0, 2)), wl, tpp, t_pad, conv_k=conv_k)
    new_pool_s = jnp.transpose(new_pool_t, (1, 0, 2))
    new_conv_s = jnp.transpose(new_conv_t, (1, 0, 2))
    w_experts = tuple(a.reshape(a.shape[1:]) for a in (w_gate, w_up, w_down))

    n_blocks = (t_part * TOP_K + EXPERT_ROWS - 1) // EXPERT_ROWS + ne
    n_spare = -(-(t_pad - t_part) * TOP_K // EXPERT_ROWS)
    n_rows = (n_blocks + n_spare) * EXPERT_ROWS

    y_prompt, y_samples, npools, nconvs, w_bf = None, [], [], [], None
    mixed = []
    for p in range(N_PARTS):
        h2, lgt = tails[p]
        xmid_p, h2, lgt, npool_p, nconv_p, *w_bf = _prompt_mixer(
            x_prompt, mod, bs, wl, h2, lgt, p * bpp, bpp, w_experts, w_bf, conv_k=conv_k)
        mixed.append((xmid_p, h2, lgt))
        npools.append(npool_p)
        nconvs.append(nconv_p)
    for p in range(N_PARTS):
        xmid_p, h2, lgt = mixed[p]
        idx, wts, rank, counts_perm = _route(lgt, b_perm, t_part)
        counts = jnp.zeros((ne,), I32).at[row_expert].set(counts_perm[:, 0].astype(I32))
        nblk = (counts + EXPERT_ROWS - 1) // EXPERT_ROWS
        first_block = jnp.concatenate([jnp.zeros((1,), I32), jnp.cumsum(nblk).astype(I32)])
        dest = _dest_rows(first_block[:ne] * EXPERT_ROWS, idx, rank, t_part, n_blocks * EXPERT_ROWS)

        def chunked(ch):
            return dest.reshape(TOP_K, t_pad // ch, ch).transpose(1, 0, 2)

        xs = _sc_dispatch(h2, chunked(SC_DISPATCH_CHUNK), n_rows)
        y = _experts(xs, w_bf[0], w_bf[1], first_block, counts)
        yg = _sc_combine_gather(y, chunked(SC_COMBINE_CHUNK), t_pad)

        y_samples.append(_final_sample(xmid_s, yg, wts, mod, modf, g_final[None, :], p, bsp, tpp,
                                       g_final if y_prompt is None else y_prompt))
        y_prompt = _final_prompt(xmid_p, yg, wts, mod, modf, bs, g_final[None, :], p * bpp, bp, y_prompt)
    y_sample = jnp.concatenate(y_samples, axis=0)
    npool_p = jnp.concatenate(npools, axis=0)
    nconv_p = jnp.concatenate(nconvs, axis=0)

    return (y_prompt, y_sample[:, None, :], npool_p[None], nconv_p[None], new_pool_s[None], new_conv_s[None])
```

```python
import functools
import math

import jax
import jax.numpy as jnp
from jax import lax
from jax.experimental import pallas as pl
from jax.experimental.pallas import tpu as pltpu
from jax.experimental.pallas import tpu_sc as plsc

POOL_WINDOWS = (2, 4, 8, 16)
N_EXPERT_GROUPS = 8
TOPK_GROUPS = 4
TOP_K = 8
ROUTED_SCALE = 2.5
EPS = 1e-6

LANES = 128
SUBLANES = 8
VMEM_LIMIT = 52 * 1024 * 1024

ADA_TILE = 1024
SEQ_TILE = 512
ROW_CHUNK = 16
ROUTE_TILE = 1408
EXPERT_ROWS = 512
N_PARTS = 2
EXPERT_BUFFERS = 6
FINAL_TILE = 512
DEST_TILE = 2816

SC_CORES = 2
SC_SUBCORES = 16
SC_WORKERS = SC_CORES * SC_SUBCORES
SC_DISPATCH_CHUNK = 88
SC_COMBINE_CHUNK = 24

F32 = jnp.float32
BF16 = jnp.bfloat16
I32 = jnp.int32
U32 = jnp.uint32
NEG_INF = float("-inf")
HI16 = 0xFFFF0000


def _sigmoid(x):
    return 1.0 / (1.0 + jnp.exp(-x))


def _silu(x):
    return x * _sigmoid(x)


def _rms_norm(x, g):
    return x * lax.rsqrt(jnp.mean(x * x, axis=-1, keepdims=True) + EPS) * g


def _dot(a, b):
    return jnp.dot(a, b, preferred_element_type=F32)


def _pack_bf16_pairs(x):
    w = x.shape[1] // 2
    bits = lax.bitcast_convert_type(x.astype(BF16).astype(F32), U32)
    return lax.bitcast_convert_type((bits[:, :w] >> 16) | (bits[:, w:] & U32(HI16)), I32)


def _unpack_bf16_pairs(p):
    bits = lax.bitcast_convert_type(p, U32)
    lo = lax.bitcast_convert_type(bits << 16, F32)
    hi = lax.bitcast_convert_type(bits & U32(HI16), F32)
    return jnp.concatenate([lo, hi], axis=1)


def _ada_kernel(c_ref, w_ref, b_ref, o_ref):
    s = _silu(c_ref[...])
    o_ref[...] = _dot(s.astype(BF16), w_ref[...].astype(BF16)) + b_ref[...]


def _ada(c, w, b):
    rows, d = c.shape
    n = w.shape[1]
    tn = ADA_TILE
    return pl.pallas_call(
        _ada_kernel,
        grid=(n // tn,),
        in_specs=[pl.BlockSpec((rows, d), lambda j: (0, 0)),
                  pl.BlockSpec((d, tn), lambda j: (0, j)),
                  pl.BlockSpec((1, tn), lambda j: (0, j))],
        out_specs=pl.BlockSpec((rows, tn), lambda j: (0, j)),
        out_shape=jax.ShapeDtypeStruct((rows, n), F32),
        compiler_params=pltpu.CompilerParams(dimension_semantics=("arbitrary",),
                                             vmem_limit_bytes=VMEM_LIMIT),
        name="ada",
    )(c, w, b.reshape(1, n))


def _mixer_tail(x, pool_d, conv_act, mod, w, xmid_ref, h2_ref, lgt_ref):
    sh2, sc2, g1, g2 = mod
    pw = pool_d.shape[1]
    pool_out = _dot(pool_d, w["pool"][...]) * w["pool_scale"][...]
    mix = _dot(pool_out.astype(BF16), w["out"][:pw, :]) + _dot(conv_act, w["out"][pw:, :])
    x1 = x + g1 * mix
    h2f = _rms_norm(x1, w["g_ffn"][...]) * (1.0 + sc2) + sh2
    h2_ref[...] = _pack_bf16_pairs(h2f)
    h2 = h2f.astype(BF16)
    lgt_ref[...] = lax.dot_general(w["router_t"][...], h2, (((1,), (1,)), ((), ())),
                                   preferred_element_type=F32)
    gu = _dot(h2, w["s_gu"][...])
    de = gu.shape[1] // 2
    hs = _silu(gu[:, :de]) * gu[:, de:]
    shared = _dot(hs.astype(BF16), w["s_down"][...])
    xmid_ref[...] = x1 + g2 * shared


_W_NAMES = ("g_mix", "in", "pool", "pool_scale", "dw", "b_dw", "ln_g", "ln_b", "out", "g_ffn",
            "router_t", "s_gu", "s_down")


def _layer_norm_silu(yc, g, b):
    mu = jnp.mean(yc, axis=-1, keepdims=True)
    yz = yc - mu
    var = jnp.mean(yz * yz, axis=-1, keepdims=True)
    return _silu(yz * lax.rsqrt(var + EPS) * g + b)


U_HALO = 32
A_HALO = 32


def _prompt_mixer_kernel(x_ref, mod_ref, *refs, tl, d, pw, cw, conv_k, mod_row0):
    nw = len(_W_NAMES)
    w = dict(zip(_W_NAMES, refs[:nw]))
    wg_ref, wu_ref, wd_ref = refs[nw + 2:nw + 5]
    xmid_ref, h2_ref, lgt_ref, npool_ref, nconv_ref, wgu_bf_ref, wd_bf_ref = refs[-16:-9]
    ubuf, s2buf, s4buf, s8buf, abuf, ashift, dbuf, cbuf, rowb = refs[-9:]
    de = wg_ref.shape[2]
    wgu_bf_ref[:, :, :de] = wg_ref[...].astype(BF16)
    wgu_bf_ref[:, :, de:] = wu_ref[...].astype(BF16)
    wd_bf_ref[...] = wd_ref[...].astype(BF16)
    l = pl.program_id(1)
    pool_buf = max(POOL_WINDOWS) - 1
    uh, ah = U_HALO, A_HALO
    pg = pw // len(POOL_WINDOWS)
    nu, na = uh + tl, ah + tl

    @pl.when(l == 0)
    def _():
        ubuf[0:uh, :] = jnp.zeros((uh, pw), F32)
        abuf[0:ah, :] = jnp.zeros((ah, cw), F32)

    @pl.when(l > 0)
    def _():
        ubuf[0:uh, :] = ubuf[tl:tl + uh, :]
        abuf[0:ah, :] = abuf[tl:tl + ah, :]

    x = x_ref[0]
    row = pl.ds((mod_row0 + pl.program_id(0)) % SUBLANES, 1)
    sh1, sc1, g1, sh2, sc2, g2 = [mod_ref[row, i * d:(i + 1) * d] for i in range(6)]

    h = _rms_norm(x, w["g_mix"][...]) * (1.0 + sc1) + sh1
    z = _dot(h.astype(BF16), w["in"][...])
    u = z[:, :pw]
    ubuf[uh:nu, :] = u
    abuf[ah:na, :] = z[:, pw:pw + cw] * _sigmoid(z[:, pw + cw:])

    s2buf[8:nu, :] = ubuf[8:nu, :] + ubuf[7:nu - 1, :]
    s4buf[16:nu, :] = s2buf[16:nu, pg:] + s2buf[14:nu - 2, pg:]
    s8buf[24:nu, :] = s4buf[24:nu, pg:] + s4buf[20:nu - 4, pg:]
    t = l * tl + lax.broadcasted_iota(I32, (tl, 1), 0)
    sums = (s2buf[uh:nu, 0:pg], s4buf[uh:nu, 0:pg], s8buf[uh:nu, 0:pg],
            s8buf[uh:nu, pg:2 * pg] + s8buf[uh - 8:nu - 8, pg:2 * pg])
    for g, win in enumerate(POOL_WINDOWS):
        cols = slice(g * pg, (g + 1) * pg)
        inv = 1.0 / jnp.minimum(win, t + 1).astype(F32)
        dbuf[:, cols] = (sums[g] * inv - u[:, cols]).astype(BF16)

    for r in range(1, SUBLANES):
        ashift[r - 1, 8:na, :] = abuf[8 - r:na - r, :]

    for k in range(conv_k):
        rowb[k] = jnp.broadcast_to(w["dw"][k:k + 1, :], (SUBLANES, cw))
    rowb[conv_k] = jnp.broadcast_to(w["b_dw"][...], (SUBLANES, cw))

    for c in range(tl // ROW_CHUNK):
        groups = []
        for o in range(ah + c * ROW_CHUNK, ah + (c + 1) * ROW_CHUNK, SUBLANES):
            yc = abuf[o:o + SUBLANES, :] * rowb[conv_k - 1]
            for back in range(1, conv_k):
                q, r = divmod(back, SUBLANES)
                src = abuf if r == 0 else ashift.at[r - 1]
                yc = yc + src[o - q * SUBLANES:o - (q - 1) * SUBLANES, :] * rowb[conv_k - 1 - back]
            groups.append(yc + rowb[conv_k])
        r0 = c * ROW_CHUNK
        cbuf[r0:r0 + ROW_CHUNK, :] = _layer_norm_silu(
            jnp.concatenate(groups, axis=0), w["ln_g"][...], w["ln_b"][...]).astype(BF16)

    @pl.when(l == pl.num_programs(1) - 1)
    def _():
        npool_ref[0] = ubuf[nu - pool_buf:nu, :]
        nconv_ref[0] = abuf[na - (conv_k - 1):na, :]

    _mixer_tail(x, dbuf[...], cbuf[...], (sh2, sc2, g1, g2), w, xmid_ref.at[0], h2_ref, lgt_ref)


def _sample_mixer_kernel(x_ref, mod_ref, sp_ref, sc_ref, *refs, d, pw, cw, conv_k):
    nw = len(_W_NAMES)
    w = dict(zip(_W_NAMES, refs[:nw]))
    xmid_ref, npool_ref, nconv_ref = refs[nw:nw + 3]
    tails = refs[nw + 3:-2]
    h2_tmp, lgt_tmp = refs[-2:]
    pool_buf = max(POOL_WINDOWS) - 1
    pg = pw // len(POOL_WINDOWS)
    x = x_ref[...]
    sh1, sc1, g1, sh2, sc2, g2 = [mod_ref[:, i * d:(i + 1) * d] for i in range(6)]
    h = _rms_norm(x, w["g_mix"][...]) * (1.0 + sc1) + sh1
    z = _dot(h.astype(BF16), w["in"][...])
    u = z[:, :pw]
    a = z[:, pw:pw + cw] * _sigmoid(z[:, pw + cw:])
    npool_ref[0:pool_buf - 1] = sp_ref[1:pool_buf]
    npool_ref[pool_buf - 1] = u
    nconv_ref[0:conv_k - 2] = sc_ref[1:conv_k - 1]
    nconv_ref[conv_k - 2] = a
    ds = []
    for g, win in enumerate(POOL_WINDOWS):
        cols = slice(g * pg, (g + 1) * pg)
        acc = u[:, cols]
        for j in range(1, win):
            acc = acc + sp_ref[pool_buf - j, :, cols]
        cnt = float(min(win, 1 + pool_buf))
        ds.append(acc / cnt - u[:, cols])
    pool_d = jnp.concatenate(ds, axis=-1).astype(BF16)
    yc = a * w["dw"][conv_k - 1:conv_k, :]
    for k in range(conv_k - 1):
        yc = yc + sc_ref[k] * w["dw"][k:k + 1, :]
    yc = yc + w["b_dw"][...]
    conv_act = _layer_norm_silu(yc, w["ln_g"][...], w["ln_b"][...]).astype(BF16)
    _mixer_tail(x, pool_d, conv_act, (sh2, sc2, g1, g2), w, xmid_ref, h2_tmp, lgt_tmp)
    n_parts = len(tails) // 2
    share = x.shape[0] // n_parts
    for p in range(n_parts):
        h2_ref, lgt_ref = tails[2 * p], tails[2 * p + 1]
        h2_ref[...] = jnp.zeros_like(h2_ref)
        lgt_ref[...] = jnp.zeros_like(lgt_ref)
        h2_ref[0:share, :] = h2_tmp[p * share:(p + 1) * share, :]
        lgt_ref[:, 0:share] = lgt_tmp[:, p * share:(p + 1) * share]


def _full_spec(a):
    nd = a.ndim
    return pl.BlockSpec(a.shape, lambda *_: (0,) * nd)


def _prompt_mixer(x, mod, mod_row0, wl, h2_all, lgt_all, b0, b, w_experts, w_bf_prev, *, conv_k):
    n_batch, seq, d = x.shape
    w_gate, w_up, w_down = w_experts
    n_exp, _, de = w_gate.shape
    pw = wl["pool"].shape[0]
    cw = wl["dw"].shape[1]
    ne = wl["router_t"].shape[0]
    tl = SEQ_TILE
    nl = seq // tl
    pg = pw // len(POOL_WINDOWS)
    assert POOL_WINDOWS == (2, 4, 8, 16) and conv_k - 1 <= A_HALO
    ws = [wl[n] for n in _W_NAMES]
    kern = functools.partial(_prompt_mixer_kernel, tl=tl, d=d, pw=pw, cw=cw, conv_k=conv_k, mod_row0=mod_row0 + b0)
    ecs = n_exp // (n_batch * nl)
    assert ecs * n_batch * nl == n_exp

    def w_map(i, j):
        return ((i + b0) * nl + j, 0, 0)

    ins = [x, mod, *ws, h2_all, lgt_all, w_gate, w_up, w_down]
    in_specs = ([pl.BlockSpec((1, tl, d), lambda i, j: (i + b0, j, 0)),
                 pl.BlockSpec((SUBLANES, mod.shape[-1]), lambda i, j: ((i + b0 + mod_row0) // SUBLANES, 0))]
                + [_full_spec(a) for a in ws]
                + [pl.BlockSpec(memory_space=pl.ANY), pl.BlockSpec(memory_space=pl.ANY),
                   pl.BlockSpec((ecs, d, de), w_map), pl.BlockSpec((ecs, d, de), w_map),
                   pl.BlockSpec((ecs, de, d), w_map)])
    aliases = {2 + len(ws): 1, 3 + len(ws): 2}
    if w_bf_prev is not None:
        aliases.update({len(ins): 5, len(ins) + 1: 6})
        ins += list(w_bf_prev)
        in_specs += [pl.BlockSpec(memory_space=pl.ANY), pl.BlockSpec(memory_space=pl.ANY)]
    return pl.pallas_call(
        kern,
        grid=(b, nl),
        in_specs=in_specs,
        out_specs=[pl.BlockSpec((1, tl, d), lambda i, j: (i, j, 0)),
                   pl.BlockSpec((tl, d // 2), lambda i, j: (i * nl + j, 0)),
                   pl.BlockSpec((ne, tl), lambda i, j: (0, i * nl + j)),
                   pl.BlockSpec((1, max(POOL_WINDOWS) - 1, pw), lambda i, j: (i, 0, 0)),
                   pl.BlockSpec((1, conv_k - 1, cw), lambda i, j: (i, 0, 0)),
                   pl.BlockSpec((ecs, d, 2 * de), w_map), pl.BlockSpec((ecs, de, d), w_map)],
        out_shape=[jax.ShapeDtypeStruct((b, seq, d), F32),
                   jax.ShapeDtypeStruct(h2_all.shape, I32),
                   jax.ShapeDtypeStruct(lgt_all.shape, F32),
                   jax.ShapeDtypeStruct((b, max(POOL_WINDOWS) - 1, pw), F32),
                   jax.ShapeDtypeStruct((b, conv_k - 1, cw), F32),
                   jax.ShapeDtypeStruct((n_exp, d, 2 * de), BF16),
                   jax.ShapeDtypeStruct((n_exp, de, d), BF16)],
        scratch_shapes=[pltpu.VMEM((U_HALO + tl, pw), F32), pltpu.VMEM((U_HALO + tl, pw), F32),
                        pltpu.VMEM((U_HALO + tl, pw - pg), F32), pltpu.VMEM((U_HALO + tl, pw - 2 * pg), F32),
                        pltpu.VMEM((A_HALO + tl, cw), F32), pltpu.VMEM((SUBLANES - 1, A_HALO + tl, cw), F32),
                        pltpu.VMEM((tl, pw), BF16), pltpu.VMEM((tl, cw), BF16),
                        pltpu.VMEM((conv_k + 1, SUBLANES, cw), F32)],
        compiler_params=pltpu.CompilerParams(dimension_semantics=("arbitrary", "arbitrary"),
                                             vmem_limit_bytes=VMEM_LIMIT),
        input_output_aliases=aliases,
        name="prompt_mixer",
    )(*ins)


def _sample_mixer(x, mod, sp, sc, wl, t_prompt, t_pad, *, conv_k):
    rows, d = x.shape
    tail = t_pad - t_prompt
    assert t_prompt % tail == 0 and rows % N_PARTS == 0 and rows // N_PARTS <= tail
    pw = wl["pool"].shape[0]
    cw = wl["dw"].shape[1]
    ne = wl["router_t"].shape[0]
    ws = [wl[n] for n in _W_NAMES]
    kern = functools.partial(_sample_mixer_kernel, d=d, pw=pw, cw=cw, conv_k=conv_k)
    ins = [x, mod, sp, sc] + ws
    tail_specs = [pl.BlockSpec((tail, d // 2), lambda i: (t_prompt // tail, 0)),
                  pl.BlockSpec((ne, tail), lambda i: (0, t_prompt // tail))] * N_PARTS
    tail_shapes = [jax.ShapeDtypeStruct((t_pad, d // 2), I32), jax.ShapeDtypeStruct((ne, t_pad), F32)] * N_PARTS
    outs = pl.pallas_call(
        kern,
        grid=(1,),
        in_specs=[_full_spec(x), pl.BlockSpec((rows, mod.shape[1]), lambda i: (0, 0))]
                 + [_full_spec(a) for a in ins[2:]],
        out_specs=[pl.BlockSpec((rows, d), lambda i: (0, 0)), _full_spec(sp), _full_spec(sc)] + tail_specs,
        out_shape=[jax.ShapeDtypeStruct((rows, d), F32),
                   jax.ShapeDtypeStruct(sp.shape, F32),
                   jax.ShapeDtypeStruct(sc.shape, F32)] + tail_shapes,
        scratch_shapes=[pltpu.VMEM((rows, d // 2), I32), pltpu.VMEM((ne, rows), F32)],
        compiler_params=pltpu.CompilerParams(dimension_semantics=("arbitrary",),
                                             vmem_limit_bytes=VMEM_LIMIT),
        name="sample_mixer",
    )(*ins)
    return outs[0], outs[1], outs[2], [(outs[3 + 2 * p], outs[4 + 2 * p]) for p in range(N_PARTS)]


def _sublane_max(x):
    return jnp.max(x, axis=0, keepdims=True)


def _route_kernel(lg_ref, b_ref, tri_ref, idx_ref, wt_ref, rank_ref, cnt_ref, carry, *, t_valid):
    i = pl.program_id(0)
    ne, tt = lg_ref.shape
    ng = N_EXPERT_GROUPS
    per = ne // ng

    @pl.when(i == 0)
    def _():
        carry[...] = jnp.zeros_like(carry)

    s = _sigmoid(lg_ref[...])
    sel = s + b_ref[...]
    s3 = [s[p * ng:(p + 1) * ng, :] for p in range(per)]
    sel3 = [sel[p * ng:(p + 1) * ng, :] for p in range(per)]
    m1 = sel3[0]
    m2 = jnp.full_like(m1, NEG_INF)
    for p in range(1, per):
        m2 = jnp.maximum(m2, jnp.minimum(m1, sel3[p]))
        m1 = jnp.maximum(m1, sel3[p])
    gs = m1 + m2
    gi = lax.broadcasted_iota(jnp.int32, (ng, tt), 0)
    beaten = jnp.zeros((ng, tt), jnp.int32)
    for g in range(ng):
        row = gs[g:g + 1, :]
        beats = (row > gs) | ((row == gs) & (gi > g))
        beaten = beaten + beats.astype(jnp.int32)
    keep = beaten < TOPK_GROUPS
    cur = [jnp.where(keep, sel3[p], NEG_INF) for p in range(per)]
    eid = [(gi * per + p).astype(F32) for p in range(per)]
    idxs, wts, hits = [], [], []
    for _ in range(TOP_K):
        m = cur[0]
        for p in range(1, per):
            m = jnp.maximum(m, cur[p])
        m = _sublane_max(m)
        cand = jnp.where(cur[0] == m, eid[0], float(ne))
        for p in range(1, per):
            cand = jnp.minimum(cand, jnp.where(cur[p] == m, eid[p], float(ne)))
        e_sel = jnp.min(cand, axis=0, keepdims=True)
        hit = [eid[p] == e_sel for p in range(per)]
        wk = jnp.where(hit[0], s3[0], 0.0)
        for p in range(1, per):
            wk = wk + jnp.where(hit[p], s3[p], 0.0)
        wts.append(jnp.sum(wk, axis=0, keepdims=True))
        cur = [jnp.where(hit[p], NEG_INF, cur[p]) for p in range(per)]
        idxs.append(e_sel)
        hits.append(hit)
    wsum = wts[0]
    for k in range(1, TOP_K):
        wsum = wsum + wts[k]
    chosen = [functools.reduce(lambda a, b: a | b, [hits[k][p] for k in range(TOP_K)]) for p in range(per)]
    real = (i * tt + lax.broadcasted_iota(I32, (1, tt), 1)) < t_valid
    onehot = jnp.concatenate([(c & real).astype(F32) for c in chosen], axis=0)
    before = _dot(onehot.astype(BF16), tri_ref[...]) + carry[...]
    for k in range(TOP_K):
        rk = jnp.where(hits[k][0], before[0:ng, :], 0.0)
        for p in range(1, per):
            rk = rk + jnp.where(hits[k][p], before[p * ng:(p + 1) * ng, :], 0.0)
        rank_ref[k:k + 1, :] = jnp.sum(rk, axis=0, keepdims=True).astype(jnp.int32)
        idx_ref[k:k + 1, :] = idxs[k].astype(jnp.int32)
        wt_ref[k:k + 1, :] = wts[k] / wsum * ROUTED_SCALE
    carry[...] = carry[...] + jnp.sum(onehot, axis=1, keepdims=True)
    cnt_ref[...] = carry[...]


def _route(lgt, b_perm, t_valid):
    ne, t = lgt.shape
    tt = ROUTE_TILE
    tri = (jnp.arange(tt)[:, None] < jnp.arange(tt)[None, :]).astype(BF16)
    return pl.pallas_call(
        functools.partial(_route_kernel, t_valid=t_valid),
        grid=(t // tt,),
        in_specs=[pl.BlockSpec((ne, tt), lambda i: (0, i)),
                  pl.BlockSpec((ne, 1), lambda i: (0, 0)),
                  pl.BlockSpec((tt, tt), lambda i: (0, 0))],
        out_specs=[pl.BlockSpec((TOP_K, tt), lambda i: (0, i)),
                   pl.BlockSpec((TOP_K, tt), lambda i: (0, i)),
                   pl.BlockSpec((TOP_K, tt), lambda i: (0, i)),
                   pl.BlockSpec((ne, 1), lambda i: (0, 0))],
        out_shape=[jax.ShapeDtypeStruct((TOP_K, t), jnp.int32),
                   jax.ShapeDtypeStruct((TOP_K, t), F32),
                   jax.ShapeDtypeStruct((TOP_K, t), jnp.int32),
                   jax.ShapeDtypeStruct((ne, 1), F32)],
        scratch_shapes=[pltpu.VMEM((ne, 1), F32)],
        compiler_params=pltpu.CompilerParams(dimension_semantics=("arbitrary",),
                                             vmem_limit_bytes=VMEM_LIMIT),
        name="route",
    )(lgt, b_perm, tri)


def _dest_kernel(pstart_ref, idx_ref, rank_ref, dest_ref, *, t_valid, spare_row):
    idx = idx_ref[...]
    k_top, tile = idx.shape

    def add_start(e, acc):
        return acc + jnp.where(idx == e, pstart_ref[e], 0)

    dest = lax.fori_loop(0, pstart_ref.shape[0], add_start, rank_ref[...])
    tok = pl.program_id(0) * tile + lax.broadcasted_iota(I32, idx.shape, 1)
    spare = spare_row + (tok - t_valid) * k_top + lax.broadcasted_iota(I32, idx.shape, 0)
    dest_ref[...] = jnp.where(tok < t_valid, dest, spare)


def _dest_rows(pstart, idx, rank, t_valid, spare_row):
    k_top, t = idx.shape
    tile = DEST_TILE
    spec = pl.BlockSpec((k_top, tile), lambda i, ps: (0, i))
    return pl.pallas_call(
        functools.partial(_dest_kernel, t_valid=t_valid, spare_row=spare_row),
        grid_spec=pltpu.PrefetchScalarGridSpec(num_scalar_prefetch=1, grid=(t // tile,),
                                               in_specs=[spec, spec], out_specs=spec),
        out_shape=jax.ShapeDtypeStruct((k_top, t), I32),
        compiler_params=pltpu.CompilerParams(dimension_semantics=("arbitrary",),
                                             vmem_limit_bytes=VMEM_LIMIT),
        name="dest_rows",
    )(pstart, idx, rank)


def _expert_kernel(first_ref, cnt_ref, xs_hbm, wgu_ref, wd_ref, y_hbm, xbuf, ybuf, in_sem, out_sem):
    e = pl.program_id(0)
    ne = first_ref.shape[0] - 1
    nbuf, rows = xbuf.shape[:2]
    de = wd_ref.shape[1]
    b0, b1 = first_ref[e], first_ref[e + 1]
    n_total = first_ref[ne]

    def in_copy(g):
        return pltpu.make_async_copy(xs_hbm.at[pl.ds(g * rows, rows)], xbuf.at[g % nbuf], in_sem.at[g % nbuf])

    def out_copy(g):
        return pltpu.make_async_copy(ybuf.at[g % nbuf], y_hbm.at[pl.ds(g * rows, rows)], out_sem.at[g % nbuf])

    @pl.when(e == 0)
    def _():
        for g in range(nbuf - 1):
            @pl.when(g < n_total)
            def _():
                in_copy(g).start()

    def block(g, carry):
        slot = g % nbuf

        @pl.when(g + nbuf - 1 < n_total)
        def _():
            in_copy(g + nbuf - 1).start()

        in_copy(g).wait()

        @pl.when(g >= nbuf)
        def _():
            out_copy(g - nbuf).wait()

        valid = cnt_ref[e] - (g - b0) * rows
        rid = lax.broadcasted_iota(I32, (rows, 1), 0)
        xs = jnp.where(rid < valid, _unpack_bf16_pairs(xbuf[slot]), 0.0).astype(BF16)
        gu = _dot(xs, wgu_ref[0])
        hb = _silu(gu[:, :de]) * gu[:, de:]
        ybuf[slot] = _pack_bf16_pairs(_dot(hb.astype(BF16), wd_ref[0]))
        out_copy(g).start()
        return carry

    lax.fori_loop(b0, b1, block, 0)

    @pl.when(e == ne - 1)
    def _():
        for back in range(nbuf, 0, -1):
            @pl.when(n_total >= back)
            def _():
                out_copy(n_total - back).wait()


def _experts(xs, w_gu, w_down, first_block, counts):
    p, dw = xs.shape
    ne, de, d = w_down.shape
    return pl.pallas_call(
        _expert_kernel,
        grid_spec=pltpu.PrefetchScalarGridSpec(
            num_scalar_prefetch=2,
            grid=(ne,),
            in_specs=[pl.BlockSpec(memory_space=pl.ANY),
                      pl.BlockSpec((1, d, 2 * de), lambda e, fb, cn: (e, 0, 0)),
                      pl.BlockSpec((1, de, d), lambda e, fb, cn: (e, 0, 0))],
            out_specs=pl.BlockSpec(memory_space=pl.ANY),
            scratch_shapes=[pltpu.VMEM((EXPERT_BUFFERS, EXPERT_ROWS, dw), I32),
                            pltpu.VMEM((EXPERT_BUFFERS, EXPERT_ROWS, dw), I32),
                            pltpu.SemaphoreType.DMA((EXPERT_BUFFERS,)), pltpu.SemaphoreType.DMA((EXPERT_BUFFERS,))]),
        out_shape=jax.ShapeDtypeStruct((p, dw), I32),
        compiler_params=pltpu.CompilerParams(dimension_semantics=("arbitrary",),
                                             vmem_limit_bytes=VMEM_LIMIT),
        name="experts",
    )(first_block, counts, xs, w_gu, w_down)


def _sc_first_chunk(n_chunks):
    per_worker = n_chunks // SC_WORKERS
    assert per_worker * SC_WORKERS == n_chunks
    return per_worker, (lax.axis_index("s") * SC_CORES + lax.axis_index("c")) * per_worker


def _sc_start(copies):
    for cp in copies:
        cp.start()


def _sc_wait(copies):
    for cp in copies:
        cp.wait()


def _sc_dispatch(h2, dest, n_rows):
    w = h2.shape[1]
    n_chunks, k_top, ch = dest.shape
    mesh = plsc.VectorSubcoreMesh(core_axis_name="c", subcore_axis_name="s")

    @functools.partial(
        pl.kernel, mesh=mesh, out_type=jax.ShapeDtypeStruct((n_rows, w), I32),
        scratch_types=[pltpu.VMEM((2, k_top, ch), I32), pltpu.VMEM((2, ch, w), I32),
                       pltpu.SemaphoreType.DMA((2,)), pltpu.SemaphoreType.DMA((2,))])
    def dispatch(h2_hbm, dest_hbm, xs_hbm, idx_v, rows_v, load_sem, store_sem):
        per_worker, first = _sc_first_chunk(n_chunks)

        def loads(i):
            slot = i % 2
            return (pltpu.make_async_copy(h2_hbm.at[pl.ds((first + i) * ch, ch)], rows_v.at[slot], load_sem.at[slot]),
                    pltpu.make_async_copy(dest_hbm.at[first + i], idx_v.at[slot], load_sem.at[slot]))

        def scatters(i):
            slot = i % 2
            return [pltpu.make_async_copy(rows_v.at[slot], xs_hbm.at[idx_v.at[slot, k]], store_sem.at[slot])
                    for k in range(k_top)]

        for i in range(min(2, per_worker)):
            _sc_start(loads(i))
        for i in range(per_worker):
            _sc_wait(loads(i))
            _sc_start(scatters(i))
            if 1 <= i < per_worker - 1:
                _sc_wait(scatters(i - 1))
                _sc_start(loads(i + 1))
        for i in range(max(per_worker - 2, 0), per_worker):
            _sc_wait(scatters(i))

    return dispatch(h2, dest)


def _sc_combine_gather(y, dest, t):
    w = y.shape[1]
    n_chunks, k_top, ch = dest.shape
    n_local = n_chunks // SC_WORKERS
    mesh = plsc.VectorSubcoreMesh(core_axis_name="c", subcore_axis_name="s")

    @functools.partial(
        pl.kernel, mesh=mesh, out_type=jax.ShapeDtypeStruct((k_top, t, w), I32),
        scratch_types=[pltpu.VMEM((n_local, k_top, ch), I32), pltpu.VMEM((k_top, ch, w), I32),
                       pltpu.SemaphoreType.DMA((k_top,)), pltpu.SemaphoreType.DMA((k_top,))])
    def gather(y_hbm, dest_hbm, out_hbm, idx_v, rows_v, load_sem, store_sem):
        per_worker, first = _sc_first_chunk(n_chunks)
        pltpu.sync_copy(dest_hbm.at[pl.ds(first, per_worker)], idx_v)

        def fetch(i, k):
            return pltpu.make_async_copy(y_hbm.at[idx_v.at[i, k]], rows_v.at[k], load_sem.at[k])

        def put(i, k):
            return pltpu.make_async_copy(rows_v.at[k], out_hbm.at[k, pl.ds((first + i) * ch, ch)], store_sem.at[k])

        @pl.loop(0, per_worker)
        def _(i):
            for k in range(k_top):
                @pl.when(i > 0)
                def _():
                    put(i - 1, k).wait()

                fetch(i, k).start()
            for k in range(k_top):
                fetch(i, k).wait()
                put(i, k).start()

        for k in range(k_top):
            put(per_worker - 1, k).wait()

    return gather(y, dest)


def _final_kernel(xmid_ref, yg_ref, wt_ref, g2_ref, shf_ref, scf_ref, gf_ref, *rest, vec_row0=None):
    o_ref = rest[-1]
    if vec_row0 is not None:
        row = pl.ds((vec_row0 + pl.program_id(0)) % SUBLANES, 1)
        g2_ref, shf_ref, scf_ref = g2_ref.at[row, :], shf_ref.at[row, :], scf_ref.at[row, :]
    wt = wt_ref[...].T[:xmid_ref.shape[0]]
    routed = wt[:, 0:1] * _unpack_bf16_pairs(yg_ref[0])
    for k in range(1, yg_ref.shape[0]):
        routed = routed + wt[:, k:k + 1] * _unpack_bf16_pairs(yg_ref[k])
    x2 = xmid_ref[...] + g2_ref[...] * routed
    o_ref[...] = _rms_norm(x2, gf_ref[...]) * (1.0 + scf_ref[...]) + shf_ref[...]


def _final_prompt(xmid, yg, wts_t, mod, modf, mod_row0, g_final, b0, n_batch, out_prev):
    b, seq, d = xmid.shape
    k_top, _, w = yg.shape
    tl = FINAL_TILE
    nl = seq // tl

    def vec(col):
        return pl.BlockSpec((SUBLANES, d), lambda i, j: ((i + b0 + mod_row0) // SUBLANES, col))

    in_specs = [pl.BlockSpec((None, tl, d), lambda i, j: (i, j, 0)),
                pl.BlockSpec((k_top, tl, w), lambda i, j: (0, i * nl + j, 0)),
                pl.BlockSpec((k_top, tl), lambda i, j: (0, i * nl + j)),
                vec(5), vec(0), vec(1), pl.BlockSpec((1, d), lambda i, j: (0, 0))]
    args = [xmid, yg, wts_t, mod, modf, modf, g_final]
    aliases = {}
    if out_prev is not None:
        in_specs.append(pl.BlockSpec(memory_space=pl.ANY))
        args.append(out_prev)
        aliases = {7: 0}
    return pl.pallas_call(
        functools.partial(_final_kernel, vec_row0=b0 + mod_row0),
        grid=(b, nl),
        in_specs=in_specs,
        out_specs=pl.BlockSpec((None, tl, d), lambda i, j: (i + b0, j, 0)),
        out_shape=jax.ShapeDtypeStruct((n_batch, seq, d), F32),
        compiler_params=pltpu.CompilerParams(dimension_semantics=("arbitrary", "arbitrary"),
                                             vmem_limit_bytes=VMEM_LIMIT),
        input_output_aliases=aliases,
        name="final_prompt",
    )(*args)


def _final_sample(xmid_all, yg, wts_t, mod, modf, g_final, part, rows, first_row, after):
    d = xmid_all.shape[1]
    k_top, _, w = yg.shape

    def vec(col):
        return pl.BlockSpec((rows, d), lambda i: (part, col))

    return pl.pallas_call(
        _final_kernel,
        grid=(1,),
        in_specs=[vec(0),
                  pl.BlockSpec((k_top, rows, w), lambda i: (0, first_row // rows, 0)),
                  pl.BlockSpec((k_top, LANES), lambda i: (0, first_row // LANES)),
                  vec(5), vec(0), vec(1), pl.BlockSpec((1, d), lambda i: (0, 0)), pl.BlockSpec(memory_space=pl.ANY)],
        out_specs=pl.BlockSpec((rows, d), lambda i: (0, 0)),
        out_shape=jax.ShapeDtypeStruct((rows, d), F32),
        compiler_params=pltpu.CompilerParams(dimension_semantics=("arbitrary",),
                                             vmem_limit_bytes=VMEM_LIMIT),
        name="final_sample",
    )(xmid_all, yg, wts_t, mod, modf, modf, g_final, after)


def _block_diag_pairs(w_pool):
    g, c, _ = w_pool.shape
    eye = jnp.eye(g, dtype=w_pool.dtype)
    return (eye[:, None, :, None] * w_pool[:, :, None, :]).reshape(g * c, g * c)


def kernel(x_prompt, x_sample, state_pool, state_conv, c_prompt, c_sample, w_ada, b_ada, g_mix, w_in, w_pool, pool_scale, w_dw, b_dw, ln_g, ln_b, w_out, g_ffn, w_router, b_router, w_gate, w_up, w_down, ws_gate, ws_up, ws_down, w_ada_final, b_ada_final, g_final):
    bp, seq, d = x_prompt.shape
    bs = x_sample.shape[0]
    depth = w_ada.shape[0]
    assert depth == 1 and x_sample.shape[1] == 1
    conv_k = w_dw.shape[1]
    ne = w_router.shape[-1]
    per = ne // N_EXPERT_GROUPS

    row_expert = (jnp.arange(ne) % N_EXPERT_GROUPS) * per + jnp.arange(ne) // N_EXPERT_GROUPS
    wl = {
        "g_mix": g_mix[0][None, :],
        "in": w_in[0].astype(BF16),
        "pool": _block_diag_pairs(w_pool[0]).astype(BF16),
        "pool_scale": pool_scale[0][None, :],
        "dw": w_dw[0],
        "b_dw": b_dw[0][None, :],
        "ln_g": ln_g[0][None, :],
        "ln_b": ln_b[0][None, :],
        "out": w_out[0].astype(BF16),
        "g_ffn": g_ffn[0][None, :],
        "router_t": w_router[0].T[row_expert].astype(BF16),
        "s_gu": jnp.concatenate([ws_gate[0], ws_up[0]], axis=1).astype(BF16),
        "s_down": ws_down[0].astype(BF16),
    }
    b_perm = b_router[0][row_expert][:, None]

    assert bp % N_PARTS == 0 and bs % N_PARTS == 0
    bpp, bsp = bp // N_PARTS, bs // N_PARTS
    tpp = bpp * seq
    t_part = tpp + bsp
    grain = SC_WORKERS * SC_DISPATCH_CHUNK * SC_COMBINE_CHUNK // math.gcd(SC_DISPATCH_CHUNK, SC_COMBINE_CHUNK)
    t_pad = (t_part + grain - 1) // grain * grain
    assert t_pad % ROUTE_TILE == 0 and t_pad % DEST_TILE == 0 and tpp % bsp == 0

    c_all = jnp.concatenate([c_sample, c_prompt], axis=0)
    mod = _ada(c_all, w_ada[0], b_ada[0])
    modf = _ada(c_all, w_ada_final, b_ada_final)

    xmid_s, new_pool_t, new_conv_t, tails = _sample_mixer(
        x_sample.reshape(bs, d), mod, jnp.transpose(state_pool[0], (1, 0, 2)),
        jnp.transpose(state_conv[0], (1, 0, 2)), wl, tpp, t_pad, conv_k=conv_k)
    new_pool_s = jnp.transpose(new_pool_t, (1, 0, 2))
    new_conv_s = jnp.transpose(new_conv_t, (1, 0, 2))
    w_experts = tuple(a.reshape(a.shape[1:]) for a in (w_gate, w_up, w_down))

    n_blocks = (t_part * TOP_K + EXPERT_ROWS - 1) // EXPERT_ROWS + ne
    n_spare = -(-(t_pad - t_part) * TOP_K // EXPERT_ROWS)
    n_rows = (n_blocks + n_spare) * EXPERT_ROWS

    y_prompt, y_samples, npools, nconvs, w_bf = None, [], [], [], None
    mixed = []
    for p in range(N_PARTS):
        h2, lgt = tails[p]
        xmid_p, h2, lgt, npool_p, nconv_p, *w_bf = _prompt_mixer(
            x_prompt, mod, bs, wl, h2, lgt, p * bpp, bpp, w_experts, w_bf, conv_k=conv_k)
        mixed.append((xmid_p, h2, lgt))
        npools.append(npool_p)
        nconvs.append(nconv_p)
    for p in range(N_PARTS):
        xmid_p, h2, lgt = mixed[p]
        idx, wts, rank, counts_perm = _route(lgt, b_perm, t_part)
        counts = jnp.zeros((ne,), I32).at[row_expert].set(counts_perm[:, 0].astype(I32))
        nblk = (counts + EXPERT_ROWS - 1) // EXPERT_ROWS
        first_block = jnp.concatenate([jnp.zeros((1,), I32), jnp.cumsum(nblk).astype(I32)])
        dest = _dest_rows(first_block[:ne] * EXPERT_ROWS, idx, rank, t_part, n_blocks * EXPERT_ROWS)

        def chunked(ch):
            return dest.reshape(TOP_K, t_pad // ch, ch).transpose(1, 0, 2)

        xs = _sc_dispatch(h2, chunked(SC_DISPATCH_CHUNK), n_rows)
        y = _experts(xs, w_bf[0], w_bf[1], first_block, counts)
        yg = _sc_combine_gather(y, chunked(SC_COMBINE_CHUNK), t_pad)

        y_samples.append(_final_sample(xmid_s, yg, wts, mod, modf, g_final[None, :], p, bsp, tpp,
                                       g_final if y_prompt is None else y_prompt))
        y_prompt = _final_prompt(xmid_p, yg, wts, mod, modf, bs, g_final[None, :], p * bpp, bp, y_prompt)
    y_sample = jnp.concatenate(y_samples, axis=0)
    npool_p = jnp.concatenate(npools, axis=0)
    nconv_p = jnp.concatenate(nconvs, axis=0)

    return (y_prompt, y_sample[:, None, :], npool_p[None], nconv_p[None], new_pool_s[None], new_conv_s[None])
```

```python
import functools
import math

import jax
import jax.numpy as jnp
from jax import lax
from jax.experimental import pallas as pl
from jax.experimental.pallas import tpu as pltpu
from jax.experimental.pallas import tpu_sc as plsc

POOL_WINDOWS = (2, 4, 8, 16)
N_EXPERT_GROUPS = 8
TOPK_GROUPS = 4
TOP_K = 8
ROUTED_SCALE = 2.5
EPS = 1e-6

LANES = 128
SUBLANES = 8
VMEM_LIMIT = 52 * 1024 * 1024

ADA_TILE = 1024
SEQ_TILE = 512
ROW_CHUNK = 16
ROUTE_TILE = 1408
EXPERT_ROWS = 512
N_PARTS = 2
EXPERT_BUFFERS = 6
EXPERTS_PER_STEP = 2
FINAL_TILE = 512
DEST_TILE = 2816

SC_CORES = 2
SC_SUBCORES = 16
SC_WORKERS = SC_CORES * SC_SUBCORES
SC_DISPATCH_CHUNK = 88
SC_COMBINE_CHUNK = 24

F32 = jnp.float32
BF16 = jnp.bfloat16
I32 = jnp.int32
U32 = jnp.uint32
NEG_INF = float("-inf")
HI16 = 0xFFFF0000


def _sigmoid(x):
    return 1.0 / (1.0 + jnp.exp(-x))


def _silu(x):
    return x * _sigmoid(x)


def _rms_norm(x, g):
    return x * lax.rsqrt(jnp.mean(x * x, axis=-1, keepdims=True) + EPS) * g


def _dot(a, b):
    return jnp.dot(a, b, preferred_element_type=F32)


def _pack_bf16_pairs(x):
    w = x.shape[1] // 2
    bits = lax.bitcast_convert_type(x.astype(BF16).astype(F32), U32)
    return lax.bitcast_convert_type((bits[:, :w] >> 16) | (bits[:, w:] & U32(HI16)), I32)


def _unpack_bf16_pairs(p):
    bits = lax.bitcast_convert_type(p, U32)
    lo = lax.bitcast_convert_type(bits << 16, F32)
    hi = lax.bitcast_convert_type(bits & U32(HI16), F32)
    return jnp.concatenate([lo, hi], axis=1)


def _ada_kernel(c_ref, w_ref, b_ref, o_ref):
    s = _silu(c_ref[...])
    o_ref[...] = _dot(s.astype(BF16), w_ref[...].astype(BF16)) + b_ref[...]


def _ada(c, w, b):
    rows, d = c.shape
    n = w.shape[1]
    tn = ADA_TILE
    return pl.pallas_call(
        _ada_kernel,
        grid=(n // tn,),
        in_specs=[pl.BlockSpec((rows, d), lambda j: (0, 0)),
                  pl.BlockSpec((d, tn), lambda j: (0, j)),
                  pl.BlockSpec((1, tn), lambda j: (0, j))],
        out_specs=pl.BlockSpec((rows, tn), lambda j: (0, j)),
        out_shape=jax.ShapeDtypeStruct((rows, n), F32),
        compiler_params=pltpu.CompilerParams(dimension_semantics=("arbitrary",),
                                             vmem_limit_bytes=VMEM_LIMIT),
        name="ada",
    )(c, w, b.reshape(1, n))


def _mixer_tail(x, pool_d, conv_act, mod, w, xmid_ref, h2_ref, lgt_ref):
    sh2, sc2, g1, g2 = mod
    pw = pool_d.shape[1]
    pool_out = _dot(pool_d, w["pool"][...]) * w["pool_scale"][...]
    mix = _dot(pool_out.astype(BF16), w["out"][:pw, :]) + _dot(conv_act, w["out"][pw:, :])
    x1 = x + g1 * mix
    h2f = _rms_norm(x1, w["g_ffn"][...]) * (1.0 + sc2) + sh2
    h2_ref[...] = _pack_bf16_pairs(h2f)
    h2 = h2f.astype(BF16)
    lgt_ref[...] = lax.dot_general(w["router_t"][...], h2, (((1,), (1,)), ((), ())),
                                   preferred_element_type=F32)
    gu = _dot(h2, w["s_gu"][...])
    de = gu.shape[1] // 2
    hs = _silu(gu[:, :de]) * gu[:, de:]
    shared = _dot(hs.astype(BF16), w["s_down"][...])
    xmid_ref[...] = x1 + g2 * shared


_W_NAMES = ("g_mix", "in", "pool", "pool_scale", "dw", "b_dw", "ln_g", "ln_b", "out", "g_ffn",
            "router_t", "s_gu", "s_down")


def _layer_norm_silu(yc, g, b):
    mu = jnp.mean(yc, axis=-1, keepdims=True)
    yz = yc - mu
    var = jnp.mean(yz * yz, axis=-1, keepdims=True)
    return _silu(yz * lax.rsqrt(var + EPS) * g + b)


U_HALO = 32
A_HALO = 32


def _prompt_mixer_kernel(x_ref, mod_ref, *refs, tl, d, pw, cw, conv_k, mod_row0):
    nw = len(_W_NAMES)
    w = dict(zip(_W_NAMES, refs[:nw]))
    wg_ref, wu_ref, wd_ref = refs[nw + 2:nw + 5]
    xmid_ref, h2_ref, lgt_ref, npool_ref, nconv_ref, wgu_bf_ref, wd_bf_ref = refs[-16:-9]
    ubuf, s2buf, s4buf, s8buf, abuf, ashift, dbuf, cbuf, rowb = refs[-9:]
    de = wg_ref.shape[2]
    wgu_bf_ref[:, :, :de] = wg_ref[...].astype(BF16)
    wgu_bf_ref[:, :, de:] = wu_ref[...].astype(BF16)
    wd_bf_ref[...] = wd_ref[...].astype(BF16)
    l = pl.program_id(1)
    pool_buf = max(POOL_WINDOWS) - 1
    uh, ah = U_HALO, A_HALO
    pg = pw // len(POOL_WINDOWS)
    nu, na = uh + tl, ah + tl

    @pl.when(l == 0)
    def _():
        ubuf[0:uh, :] = jnp.zeros((uh, pw), F32)
        abuf[0:ah, :] = jnp.zeros((ah, cw), F32)

    @pl.when(l > 0)
    def _():
        ubuf[0:uh, :] = ubuf[tl:tl + uh, :]
        abuf[0:ah, :] = abuf[tl:tl + ah, :]

    x = x_ref[0]
    row = pl.ds((mod_row0 + pl.program_id(0)) % SUBLANES, 1)
    sh1, sc1, g1, sh2, sc2, g2 = [mod_ref[row, i * d:(i + 1) * d] for i in range(6)]

    h = _rms_norm(x, w["g_mix"][...]) * (1.0 + sc1) + sh1
    z = _dot(h.astype(BF16), w["in"][...])
    u = z[:, :pw]
    ubuf[uh:nu, :] = u
    abuf[ah:na, :] = z[:, pw:pw + cw] * _sigmoid(z[:, pw + cw:])

    s2buf[8:nu, :] = ubuf[8:nu, :] + ubuf[7:nu - 1, :]
    s4buf[16:nu, :] = s2buf[16:nu, pg:] + s2buf[14:nu - 2, pg:]
    s8buf[24:nu, :] = s4buf[24:nu, pg:] + s4buf[20:nu - 4, pg:]
    t = l * tl + lax.broadcasted_iota(I32, (tl, 1), 0)
    sums = (s2buf[uh:nu, 0:pg], s4buf[uh:nu, 0:pg], s8buf[uh:nu, 0:pg],
            s8buf[uh:nu, pg:2 * pg] + s8buf[uh - 8:nu - 8, pg:2 * pg])
    for g, win in enumerate(POOL_WINDOWS):
        cols = slice(g * pg, (g + 1) * pg)
        inv = 1.0 / jnp.minimum(win, t + 1).astype(F32)
        dbuf[:, cols] = (sums[g] * inv - u[:, cols]).astype(BF16)

    for r in range(1, SUBLANES):
        ashift[r - 1, 8:na, :] = abuf[8 - r:na - r, :]

    for k in range(conv_k):
        rowb[k] = jnp.broadcast_to(w["dw"][k:k + 1, :], (SUBLANES, cw))
    rowb[conv_k] = jnp.broadcast_to(w["b_dw"][...], (SUBLANES, cw))

    for c in range(tl // ROW_CHUNK):
        groups = []
        for o in range(ah + c * ROW_CHUNK, ah + (c + 1) * ROW_CHUNK, SUBLANES):
            yc = abuf[o:o + SUBLANES, :] * rowb[conv_k - 1]
            for back in range(1, conv_k):
                q, r = divmod(back, SUBLANES)
                src = abuf if r == 0 else ashift.at[r - 1]
                yc = yc + src[o - q * SUBLANES:o - (q - 1) * SUBLANES, :] * rowb[conv_k - 1 - back]
            groups.append(yc + rowb[conv_k])
        r0 = c * ROW_CHUNK
        cbuf[r0:r0 + ROW_CHUNK, :] = _layer_norm_silu(
            jnp.concatenate(groups, axis=0), w["ln_g"][...], w["ln_b"][...]).astype(BF16)

    @pl.when(l == pl.num_programs(1) - 1)
    def _():
        npool_ref[0] = ubuf[nu - pool_buf:nu, :]
        nconv_ref[0] = abuf[na - (conv_k - 1):na, :]

    _mixer_tail(x, dbuf[...], cbuf[...], (sh2, sc2, g1, g2), w, xmid_ref.at[0], h2_ref, lgt_ref)


def _sample_mixer_kernel(x_ref, mod_ref, sp_ref, sc_ref, *refs, d, pw, cw, conv_k):
    nw = len(_W_NAMES)
    w = dict(zip(_W_NAMES, refs[:nw]))
    xmid_ref, npool_ref, nconv_ref = refs[nw:nw + 3]
    tails = refs[nw + 3:-2]
    h2_tmp, lgt_tmp = refs[-2:]
    pool_buf = max(POOL_WINDOWS) - 1
    pg = pw // len(POOL_WINDOWS)
    x = x_ref[...]
    sh1, sc1, g1, sh2, sc2, g2 = [mod_ref[:, i * d:(i + 1) * d] for i in range(6)]
    h = _rms_norm(x, w["g_mix"][...]) * (1.0 + sc1) + sh1
    z = _dot(h.astype(BF16), w["in"][...])
    u = z[:, :pw]
    a = z[:, pw:pw + cw] * _sigmoid(z[:, pw + cw:])
    npool_ref[0:pool_buf - 1] = sp_ref[1:pool_buf]
    npool_ref[pool_buf - 1] = u
    nconv_ref[0:conv_k - 2] = sc_ref[1:conv_k - 1]
    nconv_ref[conv_k - 2] = a
    ds = []
    for g, win in enumerate(POOL_WINDOWS):
        cols = slice(g * pg, (g + 1) * pg)
        acc = u[:, cols]
        for j in range(1, win):
            acc = acc + sp_ref[pool_buf - j, :, cols]
        cnt = float(min(win, 1 + pool_buf))
        ds.append(acc / cnt - u[:, cols])
    pool_d = jnp.concatenate(ds, axis=-1).astype(BF16)
    yc = a * w["dw"][conv_k - 1:conv_k, :]
    for k in range(conv_k - 1):
        yc = yc + sc_ref[k] * w["dw"][k:k + 1, :]
    yc = yc + w["b_dw"][...]
    conv_act = _layer_norm_silu(yc, w["ln_g"][...], w["ln_b"][...]).astype(BF16)
    _mixer_tail(x, pool_d, conv_act, (sh2, sc2, g1, g2), w, xmid_ref, h2_tmp, lgt_tmp)
    n_parts = len(tails) // 2
    share = x.shape[0] // n_parts
    for p in range(n_parts):
        h2_ref, lgt_ref = tails[2 * p], tails[2 * p + 1]
        h2_ref[...] = jnp.zeros_like(h2_ref)
        lgt_ref[...] = jnp.zeros_like(lgt_ref)
        h2_ref[0:share, :] = h2_tmp[p * share:(p + 1) * share, :]
        lgt_ref[:, 0:share] = lgt_tmp[:, p * share:(p + 1) * share]


def _full_spec(a):
    nd = a.ndim
    return pl.BlockSpec(a.shape, lambda *_: (0,) * nd)


def _prompt_mixer(x, mod, mod_row0, wl, h2_all, lgt_all, b0, b, w_experts, w_bf_prev, *, conv_k):
    n_batch, seq, d = x.shape
    w_gate, w_up, w_down = w_experts
    n_exp, _, de = w_gate.shape
    pw = wl["pool"].shape[0]
    cw = wl["dw"].shape[1]
    ne = wl["router_t"].shape[0]
    tl = SEQ_TILE
    nl = seq // tl
    pg = pw // len(POOL_WINDOWS)
    assert POOL_WINDOWS == (2, 4, 8, 16) and conv_k - 1 <= A_HALO
    ws = [wl[n] for n in _W_NAMES]
    kern = functools.partial(_prompt_mixer_kernel, tl=tl, d=d, pw=pw, cw=cw, conv_k=conv_k, mod_row0=mod_row0 + b0)
    ecs = n_exp // (n_batch * nl)
    assert ecs * n_batch * nl == n_exp

    def w_map(i, j):
        return ((i + b0) * nl + j, 0, 0)

    ins = [x, mod, *ws, h2_all, lgt_all, w_gate, w_up, w_down]
    in_specs = ([pl.BlockSpec((1, tl, d), lambda i, j: (i + b0, j, 0)),
                 pl.BlockSpec((SUBLANES, mod.shape[-1]), lambda i, j: ((i + b0 + mod_row0) // SUBLANES, 0))]
                + [_full_spec(a) for a in ws]
                + [pl.BlockSpec(memory_space=pl.ANY), pl.BlockSpec(memory_space=pl.ANY),
                   pl.BlockSpec((ecs, d, de), w_map), pl.BlockSpec((ecs, d, de), w_map),
                   pl.BlockSpec((ecs, de, d), w_map)])
    aliases = {2 + len(ws): 1, 3 + len(ws): 2}
    if w_bf_prev is not None:
        aliases.update({len(ins): 5, len(ins) + 1: 6})
        ins += list(w_bf_prev)
        in_specs += [pl.BlockSpec(memory_space=pl.ANY), pl.BlockSpec(memory_space=pl.ANY)]
    return pl.pallas_call(
        kern,
        grid=(b, nl),
        in_specs=in_specs,
        out_specs=[pl.BlockSpec((1, tl, d), lambda i, j: (i, j, 0)),
                   pl.BlockSpec((tl, d // 2), lambda i, j: (i * nl + j, 0)),
                   pl.BlockSpec((ne, tl), lambda i, j: (0, i * nl + j)),
                   pl.BlockSpec((1, max(POOL_WINDOWS) - 1, pw), lambda i, j: (i, 0, 0)),
                   pl.BlockSpec((1, conv_k - 1, cw), lambda i, j: (i, 0, 0)),
                   pl.BlockSpec((ecs, d, 2 * de), w_map), pl.BlockSpec((ecs, de, d), w_map)],
        out_shape=[jax.ShapeDtypeStruct((b, seq, d), F32),
                   jax.ShapeDtypeStruct(h2_all.shape, I32),
                   jax.ShapeDtypeStruct(lgt_all.shape, F32),
                   jax.ShapeDtypeStruct((b, max(POOL_WINDOWS) - 1, pw), F32),
                   jax.ShapeDtypeStruct((b, conv_k - 1, cw), F32),
                   jax.ShapeDtypeStruct((n_exp, d, 2 * de), BF16),
                   jax.ShapeDtypeStruct((n_exp, de, d), BF16)],
        scratch_shapes=[pltpu.VMEM((U_HALO + tl, pw), F32), pltpu.VMEM((U_HALO + tl, pw), F32),
                        pltpu.VMEM((U_HALO + tl, pw - pg), F32), pltpu.VMEM((U_HALO + tl, pw - 2 * pg), F32),
                        pltpu.VMEM((A_HALO + tl, cw), F32), pltpu.VMEM((SUBLANES - 1, A_HALO + tl, cw), F32),
                        pltpu.VMEM((tl, pw), BF16), pltpu.VMEM((tl, cw), BF16),
                        pltpu.VMEM((conv_k + 1, SUBLANES, cw), F32)],
        compiler_params=pltpu.CompilerParams(dimension_semantics=("arbitrary", "arbitrary"),
                                             vmem_limit_bytes=VMEM_LIMIT),
        input_output_aliases=aliases,
        name="prompt_mixer",
    )(*ins)


def _sample_mixer(x, mod, sp, sc, wl, t_prompt, t_pad, *, conv_k):
    rows, d = x.shape
    tail = t_pad - t_prompt
    assert t_prompt % tail == 0 and rows % N_PARTS == 0 and rows // N_PARTS <= tail
    pw = wl["pool"].shape[0]
    cw = wl["dw"].shape[1]
    ne = wl["router_t"].shape[0]
    ws = [wl[n] for n in _W_NAMES]
    kern = functools.partial(_sample_mixer_kernel, d=d, pw=pw, cw=cw, conv_k=conv_k)
    ins = [x, mod, sp, sc] + ws
    tail_specs = [pl.BlockSpec((tail, d // 2), lambda i: (t_prompt // tail, 0)),
                  pl.BlockSpec((ne, tail), lambda i: (0, t_prompt // tail))] * N_PARTS
    tail_shapes = [jax.ShapeDtypeStruct((t_pad, d // 2), I32), jax.ShapeDtypeStruct((ne, t_pad), F32)] * N_PARTS
    outs = pl.pallas_call(
        kern,
        grid=(1,),
        in_specs=[_full_spec(x), pl.BlockSpec((rows, mod.shape[1]), lambda i: (0, 0))]
                 + [_full_spec(a) for a in ins[2:]],
        out_specs=[pl.BlockSpec((rows, d), lambda i: (0, 0)), _full_spec(sp), _full_spec(sc)] + tail_specs,
        out_shape=[jax.ShapeDtypeStruct((rows, d), F32),
                   jax.ShapeDtypeStruct(sp.shape, F32),
                   jax.ShapeDtypeStruct(sc.shape, F32)] + tail_shapes,
        scratch_shapes=[pltpu.VMEM((rows, d // 2), I32), pltpu.VMEM((ne, rows), F32)],
        compiler_params=pltpu.CompilerParams(dimension_semantics=("arbitrary",),
                                             vmem_limit_bytes=VMEM_LIMIT),
        name="sample_mixer",
    )(*ins)
    return outs[0], outs[1], outs[2], [(outs[3 + 2 * p], outs[4 + 2 * p]) for p in range(N_PARTS)]


def _sublane_max(x):
    return jnp.max(x, axis=0, keepdims=True)


def _route_kernel(lg_ref, b_ref, tri_ref, idx_ref, wt_ref, rank_ref, cnt_ref, carry, *, t_valid):
    i = pl.program_id(0)
    ne, tt = lg_ref.shape
    ng = N_EXPERT_GROUPS
    per = ne // ng

    @pl.when(i == 0)
    def _():
        carry[...] = jnp.zeros_like(carry)

    s = _sigmoid(lg_ref[...])
    sel = s + b_ref[...]
    s3 = [s[p * ng:(p + 1) * ng, :] for p in range(per)]
    sel3 = [sel[p * ng:(p + 1) * ng, :] for p in range(per)]
    m1 = sel3[0]
    m2 = jnp.full_like(m1, NEG_INF)
    for p in range(1, per):
        m2 = jnp.maximum(m2, jnp.minimum(m1, sel3[p]))
        m1 = jnp.maximum(m1, sel3[p])
    gs = m1 + m2
    gi = lax.broadcasted_iota(jnp.int32, (ng, tt), 0)
    beaten = jnp.zeros((ng, tt), jnp.int32)
    for g in range(ng):
        row = gs[g:g + 1, :]
        beats = (row > gs) | ((row == gs) & (gi > g))
        beaten = beaten + beats.astype(jnp.int32)
    keep = beaten < TOPK_GROUPS
    cur = [jnp.where(keep, sel3[p], NEG_INF) for p in range(per)]
    eid = [(gi * per + p).astype(F32) for p in range(per)]
    idxs, wts, hits = [], [], []
    for _ in range(TOP_K):
        m = cur[0]
        for p in range(1, per):
            m = jnp.maximum(m, cur[p])
        m = _sublane_max(m)
        cand = jnp.where(cur[0] == m, eid[0], float(ne))
        for p in range(1, per):
            cand = jnp.minimum(cand, jnp.where(cur[p] == m, eid[p], float(ne)))
        e_sel = jnp.min(cand, axis=0, keepdims=True)
        hit = [eid[p] == e_sel for p in range(per)]
        wk = jnp.where(hit[0], s3[0], 0.0)
        for p in range(1, per):
            wk = wk + jnp.where(hit[p], s3[p], 0.0)
        wts.append(jnp.sum(wk, axis=0, keepdims=True))
        cur = [jnp.where(hit[p], NEG_INF, cur[p]) for p in range(per)]
        idxs.append(e_sel)
        hits.append(hit)
    wsum = wts[0]
    for k in range(1, TOP_K):
        wsum = wsum + wts[k]
    chosen = [functools.reduce(lambda a, b: a | b, [hits[k][p] for k in range(TOP_K)]) for p in range(per)]
    real = (i * tt + lax.broadcasted_iota(I32, (1, tt), 1)) < t_valid
    onehot = jnp.concatenate([(c & real).astype(F32) for c in chosen], axis=0)
    before = _dot(onehot.astype(BF16), tri_ref[...]) + carry[...]
    for k in range(TOP_K):
        rk = jnp.where(hits[k][0], before[0:ng, :], 0.0)
        for p in range(1, per):
            rk = rk + jnp.where(hits[k][p], before[p * ng:(p + 1) * ng, :], 0.0)
        rank_ref[k:k + 1, :] = jnp.sum(rk, axis=0, keepdims=True).astype(jnp.int32)
        idx_ref[k:k + 1, :] = idxs[k].astype(jnp.int32)
        wt_ref[k:k + 1, :] = wts[k] / wsum * ROUTED_SCALE
    carry[...] = carry[...] + jnp.sum(onehot, axis=1, keepdims=True)
    cnt_ref[...] = carry[...]


def _route(lgt, b_perm, t_valid):
    ne, t = lgt.shape
    tt = ROUTE_TILE
    tri = (jnp.arange(tt)[:, None] < jnp.arange(tt)[None, :]).astype(BF16)
    return pl.pallas_call(
        functools.partial(_route_kernel, t_valid=t_valid),
        grid=(t // tt,),
        in_specs=[pl.BlockSpec((ne, tt), lambda i: (0, i)),
                  pl.BlockSpec((ne, 1), lambda i: (0, 0)),
                  pl.BlockSpec((tt, tt), lambda i: (0, 0))],
        out_specs=[pl.BlockSpec((TOP_K, tt), lambda i: (0, i)),
                   pl.BlockSpec((TOP_K, tt), lambda i: (0, i)),
                   pl.BlockSpec((TOP_K, tt), lambda i: (0, i)),
                   pl.BlockSpec((ne, 1), lambda i: (0, 0))],
        out_shape=[jax.ShapeDtypeStruct((TOP_K, t), jnp.int32),
                   jax.ShapeDtypeStruct((TOP_K, t), F32),
                   jax.ShapeDtypeStruct((TOP_K, t), jnp.int32),
                   jax.ShapeDtypeStruct((ne, 1), F32)],
        scratch_shapes=[pltpu.VMEM((ne, 1), F32)],
        compiler_params=pltpu.CompilerParams(dimension_semantics=("arbitrary",),
                                             vmem_limit_bytes=VMEM_LIMIT),
        name="route",
    )(lgt, b_perm, tri)


def _dest_kernel(pstart_ref, idx_ref, rank_ref, dest_ref, *, t_valid, spare_row):
    idx = idx_ref[...]
    k_top, tile = idx.shape

    def add_start(e, acc):
        return acc + jnp.where(idx == e, pstart_ref[e], 0)

    dest = lax.fori_loop(0, pstart_ref.shape[0], add_start, rank_ref[...])
    tok = pl.program_id(0) * tile + lax.broadcasted_iota(I32, idx.shape, 1)
    spare = spare_row + (tok - t_valid) * k_top + lax.broadcasted_iota(I32, idx.shape, 0)
    dest_ref[...] = jnp.where(tok < t_valid, dest, spare)


def _dest_rows(pstart, idx, rank, t_valid, spare_row):
    k_top, t = idx.shape
    tile = DEST_TILE
    spec = pl.BlockSpec((k_top, tile), lambda i, ps: (0, i))
    return pl.pallas_call(
        functools.partial(_dest_kernel, t_valid=t_valid, spare_row=spare_row),
        grid_spec=pltpu.PrefetchScalarGridSpec(num_scalar_prefetch=1, grid=(t // tile,),
                                               in_specs=[spec, spec], out_specs=spec),
        out_shape=jax.ShapeDtypeStruct((k_top, t), I32),
        compiler_params=pltpu.CompilerParams(dimension_semantics=("arbitrary",),
                                             vmem_limit_bytes=VMEM_LIMIT),
        name="dest_rows",
    )(pstart, idx, rank)


def _expert_kernel(first_ref, cnt_ref, xs_hbm, wgu_ref, wd_ref, y_hbm, xbuf, ybuf, in_sem, out_sem):
    step = pl.program_id(0)
    ne = first_ref.shape[0] - 1
    nbuf, rows = xbuf.shape[:2]
    per_step, de = wd_ref.shape[:2]
    n_total = first_ref[ne]

    def in_copy(g):
        return pltpu.make_async_copy(xs_hbm.at[pl.ds(g * rows, rows)], xbuf.at[g % nbuf], in_sem.at[g % nbuf])

    def out_copy(g):
        return pltpu.make_async_copy(ybuf.at[g % nbuf], y_hbm.at[pl.ds(g * rows, rows)], out_sem.at[g % nbuf])

    @pl.when(step == 0)
    def _():
        for g in range(nbuf - 1):
            @pl.when(g < n_total)
            def _():
                in_copy(g).start()

    def block(g, carry, *, j, e, b0):
        slot = g % nbuf

        @pl.when(g + nbuf - 1 < n_total)
        def _():
            in_copy(g + nbuf - 1).start()

        in_copy(g).wait()

        @pl.when(g >= nbuf)
        def _():
            out_copy(g - nbuf).wait()

        valid = cnt_ref[e] - (g - b0) * rows
        rid = lax.broadcasted_iota(I32, (rows, 1), 0)
        xs = jnp.where(rid < valid, _unpack_bf16_pairs(xbuf[slot]), 0.0).astype(BF16)
        gu = _dot(xs, wgu_ref[j])
        hb = _silu(gu[:, :de]) * gu[:, de:]
        ybuf[slot] = _pack_bf16_pairs(_dot(hb.astype(BF16), wd_ref[j]))
        out_copy(g).start()
        return carry

    for j in range(per_step):
        e = step * per_step + j
        b0 = first_ref[e]
        lax.fori_loop(b0, first_ref[e + 1], functools.partial(block, j=j, e=e, b0=b0), 0)

    @pl.when(step == pl.num_programs(0) - 1)
    def _():
        for back in range(nbuf, 0, -1):
            @pl.when(n_total >= back)
            def _():
                out_copy(n_total - back).wait()


def _experts(xs, w_gu, w_down, first_block, counts):
    p, dw = xs.shape
    ne, de, d = w_down.shape
    return pl.pallas_call(
        _expert_kernel,
        grid_spec=pltpu.PrefetchScalarGridSpec(
            num_scalar_prefetch=2,
            grid=(ne // EXPERTS_PER_STEP,),
            in_specs=[pl.BlockSpec(memory_space=pl.ANY),
                      pl.BlockSpec((EXPERTS_PER_STEP, d, 2 * de), lambda e, fb, cn: (e, 0, 0)),
                      pl.BlockSpec((EXPERTS_PER_STEP, de, d), lambda e, fb, cn: (e, 0, 0))],
            out_specs=pl.BlockSpec(memory_space=pl.ANY),
            scratch_shapes=[pltpu.VMEM((EXPERT_BUFFERS, EXPERT_ROWS, dw), I32),
                            pltpu.VMEM((EXPERT_BUFFERS, EXPERT_ROWS, dw), I32),
                            pltpu.SemaphoreType.DMA((EXPERT_BUFFERS,)), pltpu.SemaphoreType.DMA((EXPERT_BUFFERS,))]),
        out_shape=jax.ShapeDtypeStruct((p, dw), I32),
        compiler_params=pltpu.CompilerParams(dimension_semantics=("arbitrary",),
                                             vmem_limit_bytes=VMEM_LIMIT),
        name="experts",
    )(first_block, counts, xs, w_gu, w_down)


def _sc_first_chunk(n_chunks):
    per_worker = n_chunks // SC_WORKERS
    assert per_worker * SC_WORKERS == n_chunks
    return per_worker, (lax.axis_index("s") * SC_CORES + lax.axis_index("c")) * per_worker


def _sc_start(copies):
    for cp in copies:
        cp.start()


def _sc_wait(copies):
    for cp in copies:
        cp.wait()


def _sc_dispatch(h2, dest, n_rows):
    w = h2.shape[1]
    n_chunks, k_top, ch = dest.shape
    mesh = plsc.VectorSubcoreMesh(core_axis_name="c", subcore_axis_name="s")

    @functools.partial(
        pl.kernel, mesh=mesh, out_type=jax.ShapeDtypeStruct((n_rows, w), I32),
        scratch_types=[pltpu.VMEM((2, k_top, ch), I32), pltpu.VMEM((2, ch, w), I32),
                       pltpu.SemaphoreType.DMA((2,)), pltpu.SemaphoreType.DMA((2,))])
    def dispatch(h2_hbm, dest_hbm, xs_hbm, idx_v, rows_v, load_sem, store_sem):
        per_worker, first = _sc_first_chunk(n_chunks)

        def loads(i):
            slot = i % 2
            return (pltpu.make_async_copy(h2_hbm.at[pl.ds((first + i) * ch, ch)], rows_v.at[slot], load_sem.at[slot]),
                    pltpu.make_async_copy(dest_hbm.at[first + i], idx_v.at[slot], load_sem.at[slot]))

        def scatters(i):
            slot = i % 2
            return [pltpu.make_async_copy(rows_v.at[slot], xs_hbm.at[idx_v.at[slot, k]], store_sem.at[slot])
                    for k in range(k_top)]

        for i in range(min(2, per_worker)):
            _sc_start(loads(i))
        for i in range(per_worker):
            _sc_wait(loads(i))
            _sc_start(scatters(i))
            if 1 <= i < per_worker - 1:
                _sc_wait(scatters(i - 1))
                _sc_start(loads(i + 1))
        for i in range(max(per_worker - 2, 0), per_worker):
            _sc_wait(scatters(i))

    return dispatch(h2, dest)


def _sc_combine_gather(y, dest, t):
    w = y.shape[1]
    n_chunks, k_top, ch = dest.shape
    n_local = n_chunks // SC_WORKERS
    mesh = plsc.VectorSubcoreMesh(core_axis_name="c", subcore_axis_name="s")

    @functools.partial(
        pl.kernel, mesh=mesh, out_type=jax.ShapeDtypeStruct((k_top, t, w), I32),
        scratch_types=[pltpu.VMEM((n_local, k_top, ch), I32), pltpu.VMEM((k_top, ch, w), I32),
                       pltpu.SemaphoreType.DMA((k_top,)), pltpu.SemaphoreType.DMA((k_top,))])
    def gather(y_hbm, dest_hbm, out_hbm, idx_v, rows_v, load_sem, store_sem):
        per_worker, first = _sc_first_chunk(n_chunks)
        pltpu.sync_copy(dest_hbm.at[pl.ds(first, per_worker)], idx_v)

        def fetch(i, k):
            return pltpu.make_async_copy(y_hbm.at[idx_v.at[i, k]], rows_v.at[k], load_sem.at[k])

        def put(i, k):
            return pltpu.make_async_copy(rows_v.at[k], out_hbm.at[k, pl.ds((first + i) * ch, ch)], store_sem.at[k])

        @pl.loop(0, per_worker)
        def _(i):
            for k in range(k_top):
                @pl.when(i > 0)
                def _():
                    put(i - 1, k).wait()

                fetch(i, k).start()
            for k in range(k_top):
                fetch(i, k).wait()
                put(i, k).start()

        for k in range(k_top):
            put(per_worker - 1, k).wait()

    return gather(y, dest)


def _final_kernel(xmid_ref, yg_ref, wt_ref, g2_ref, shf_ref, scf_ref, gf_ref, *rest, vec_row0=None):
    o_ref = rest[-1]
    if vec_row0 is not None:
        row = pl.ds((vec_row0 + pl.program_id(0)) % SUBLANES, 1)
        g2_ref, shf_ref, scf_ref = g2_ref.at[row, :], shf_ref.at[row, :], scf_ref.at[row, :]
    wt = wt_ref[...].T[:xmid_ref.shape[0]]
    routed = wt[:, 0:1] * _unpack_bf16_pairs(yg_ref[0])
    for k in range(1, yg_ref.shape[0]):
        routed = routed + wt[:, k:k + 1] * _unpack_bf16_pairs(yg_ref[k])
    x2 = xmid_ref[...] + g2_ref[...] * routed
    o_ref[...] = _rms_norm(x2, gf_ref[...]) * (1.0 + scf_ref[...]) + shf_ref[...]


def _final_prompt(xmid, yg, wts_t, mod, modf, mod_row0, g_final, b0, n_batch, out_prev):
    b, seq, d = xmid.shape
    k_top, _, w = yg.shape
    tl = FINAL_TILE
    nl = seq // tl

    def vec(col):
        return pl.BlockSpec((SUBLANES, d), lambda i, j: ((i + b0 + mod_row0) // SUBLANES, col))

    in_specs = [pl.BlockSpec((None, tl, d), lambda i, j: (i, j, 0)),
                pl.BlockSpec((k_top, tl, w), lambda i, j: (0, i * nl + j, 0)),
                pl.BlockSpec((k_top, tl), lambda i, j: (0, i * nl + j)),
                vec(5), vec(0), vec(1), pl.BlockSpec((1, d), lambda i, j: (0, 0))]
    args = [xmid, yg, wts_t, mod, modf, modf, g_final]
    aliases = {}
    if out_prev is not None:
        in_specs.append(pl.BlockSpec(memory_space=pl.ANY))
        args.append(out_prev)
        aliases = {7: 0}
    return pl.pallas_call(
        functools.partial(_final_kernel, vec_row0=b0 + mod_row0),
        grid=(b, nl),
        in_specs=in_specs,
        out_specs=pl.BlockSpec((None, tl, d), lambda i, j: (i + b0, j, 0)),
        out_shape=jax.ShapeDtypeStruct((n_batch, seq, d), F32),
        compiler_params=pltpu.CompilerParams(dimension_semantics=("arbitrary", "arbitrary"),
                                             vmem_limit_bytes=VMEM_LIMIT),
        input_output_aliases=aliases,
        name="final_prompt",
    )(*args)


def _final_sample(xmid_all, yg, wts_t, mod, modf, g_final, part, rows, first_row, after):
    d = xmid_all.shape[1]
    k_top, _, w = yg.shape

    def vec(col):
        return pl.BlockSpec((rows, d), lambda i: (part, col))

    return pl.pallas_call(
        _final_kernel,
        grid=(1,),
        in_specs=[vec(0),
                  pl.BlockSpec((k_top, rows, w), lambda i: (0, first_row // rows, 0)),
                  pl.BlockSpec((k_top, LANES), lambda i: (0, first_row // LANES)),
                  vec(5), vec(0), vec(1), pl.BlockSpec((1, d), lambda i: (0, 0)), pl.BlockSpec(memory_space=pl.ANY)],
        out_specs=pl.BlockSpec((rows, d), lambda i: (0, 0)),
        out_shape=jax.ShapeDtypeStruct((rows, d), F32),
        compiler_params=pltpu.CompilerParams(dimension_semantics=("arbitrary",),
                                             vmem_limit_bytes=VMEM_LIMIT),
        name="final_sample",
    )(xmid_all, yg, wts_t, mod, modf, modf, g_final, after)


def _block_diag_pairs(w_pool):
    g, c, _ = w_pool.shape
    eye = jnp.eye(g, dtype=w_pool.dtype)
    return (eye[:, None, :, None] * w_pool[:, :, None, :]).reshape(g * c, g * c)


def kernel(x_prompt, x_sample, state_pool, state_conv, c_prompt, c_sample, w_ada, b_ada, g_mix, w_in, w_pool, pool_scale, w_dw, b_dw, ln_g, ln_b, w_out, g_ffn, w_router, b_router, w_gate, w_up, w_down, ws_gate, ws_up, ws_down, w_ada_final, b_ada_final, g_final):
    bp, seq, d = x_prompt.shape
    bs = x_sample.shape[0]
    depth = w_ada.shape[0]
    assert depth == 1 and x_sample.shape[1] == 1
    conv_k = w_dw.shape[1]
    ne = w_router.shape[-1]
    per = ne // N_EXPERT_GROUPS

    row_expert = (jnp.arange(ne) % N_EXPERT_GROUPS) * per + jnp.arange(ne) // N_EXPERT_GROUPS
    wl = {
        "g_mix": g_mix[0][None, :],
        "in": w_in[0].astype(BF16),
        "pool": _block_diag_pairs(w_pool[0]).astype(BF16),
        "pool_scale": pool_scale[0][None, :],
        "dw": w_dw[0],
        "b_dw": b_dw[0][None, :],
        "ln_g": ln_g[0][None, :],
        "ln_b": ln_b[0][None, :],
        "out": w_out[0].astype(BF16),
        "g_ffn": g_ffn[0][None, :],
        "router_t": w_router[0].T[row_expert].astype(BF16),
        "s_gu": jnp.concatenate([ws_gate[0], ws_up[0]], axis=1).astype(BF16),
        "s_down": ws_down[0].astype(BF16),
    }
    b_perm = b_router[0][row_expert][:, None]

    assert bp % N_PARTS == 0 and bs % N_PARTS == 0
    bpp, bsp = bp // N_PARTS, bs // N_PARTS
    tpp = bpp * seq
    t_part = tpp + bsp
    grain = SC_WORKERS * SC_DISPATCH_CHUNK * SC_COMBINE_CHUNK // math.gcd(SC_DISPATCH_CHUNK, SC_COMBINE_CHUNK)
    t_pad = (t_part + grain - 1) // grain * grain
    assert t_pad % ROUTE_TILE == 0 and t_pad % DEST_TILE == 0 and tpp % bsp == 0

    c_all = jnp.concatenate([c_sample, c_prompt], axis=0)
    mod = _ada(c_all, w_ada[0], b_ada[0])
    modf = _ada(c_all, w_ada_final, b_ada_final)

    xmid_s, new_pool_t, new_conv_t, tails = _sample_mixer(
        x_sample.reshape(bs, d), mod, jnp.transpose(state_pool[0], (1, 0, 2)),
        jnp.transpose(state_conv[0], (1, 0, 2)), wl, tpp, t_pad, conv_k=conv_k)
    new_pool_s = jnp.transpose(new_pool_t, (1, 0, 2))
    new_conv_s = jnp.transpose(new_conv_t, (1, 0, 2))
    w_experts = tuple(a.reshape(a.shape[1:]) for a in (w_gate, w_up, w_down))

    n_blocks = (t_part * TOP_K + EXPERT_ROWS - 1) // EXPERT_ROWS + ne
    n_spare = -(-(t_pad - t_part) * TOP_K // EXPERT_ROWS)
    n_rows = (n_blocks + n_spare) * EXPERT_ROWS

    y_prompt, y_samples, npools, nconvs, w_bf = None, [], [], [], None
    mixed = []
    for p in range(N_PARTS):
        h2, lgt = tails[p]
        xmid_p, h2, lgt, npool_p, nconv_p, *w_bf = _prompt_mixer(
            x_prompt, mod, bs, wl, h2, lgt, p * bpp, bpp, w_experts, w_bf, conv_k=conv_k)
        mixed.append((xmid_p, h2, lgt))
        npools.append(npool_p)
        nconvs.append(nconv_p)
    for p in range(N_PARTS):
        xmid_p, h2, lgt = mixed[p]
        idx, wts, rank, counts_perm = _route(lgt, b_perm, t_part)
        counts = jnp.zeros((ne,), I32).at[row_expert].set(counts_perm[:, 0].astype(I32))
        nblk = (counts + EXPERT_ROWS - 1) // EXPERT_ROWS
        first_block = jnp.concatenate([jnp.zeros((1,), I32), jnp.cumsum(nblk).astype(I32)])
        dest = _dest_rows(first_block[:ne] * EXPERT_ROWS, idx, rank, t_part, n_blocks * EXPERT_ROWS)

        def chunked(ch):
            return dest.reshape(TOP_K, t_pad // ch, ch).transpose(1, 0, 2)

        xs = _sc_dispatch(h2, chunked(SC_DISPATCH_CHUNK), n_rows)
        y = _experts(xs, w_bf[0], w_bf[1], first_block, counts)
        yg = _sc_combine_gather(y, chunked(SC_COMBINE_CHUNK), t_pad)

        y_samples.append(_final_sample(xmid_s, yg, wts, mod, modf, g_final[None, :], p, bsp, tpp,
                                       g_final if y_prompt is None else y_prompt))
        y_prompt = _final_prompt(xmid_p, yg, wts, mod, modf, bs, g_final[None, :], p * bpp, bp, y_prompt)
    y_sample = jnp.concatenate(y_samples, axis=0)
    npool_p = jnp.concatenate(npools, axis=0)
    nconv_p = jnp.concatenate(nconvs, axis=0)

    return (y_prompt, y_sample[:, None, :], npool_p[None], nconv_p[None], new_pool_s[None], new_conv_s[None])
```

```python
import functools
import math

import jax
import jax.numpy as jnp
from jax import lax
from jax.experimental import pallas as pl
from jax.experimental.pallas import tpu as pltpu
from jax.experimental.pallas import tpu_sc as plsc

POOL_WINDOWS = (2, 4, 8, 16)
N_EXPERT_GROUPS = 8
TOPK_GROUPS = 4
TOP_K = 8
ROUTED_SCALE = 2.5
EPS = 1e-6

LANES = 128
SUBLANES = 8
VMEM_LIMIT = 52 * 1024 * 1024

ADA_TILE = 1024
SEQ_TILE = 512
ROW_CHUNK = 16
ROUTE_TILE = 1408
EXPERT_ROWS = 512
N_PARTS = 2
EXPERT_BUFFERS = 6
EXPERTS_PER_STEP = 4
FINAL_TILE = 512
DEST_TILE = 2816

SC_CORES = 2
SC_SUBCORES = 16
SC_WORKERS = SC_CORES * SC_SUBCORES
SC_DISPATCH_CHUNK = 88
SC_COMBINE_CHUNK = 24

F32 = jnp.float32
BF16 = jnp.bfloat16
I32 = jnp.int32
U32 = jnp.uint32
NEG_INF = float("-inf")
HI16 = 0xFFFF0000


def _sigmoid(x):
    return 1.0 / (1.0 + jnp.exp(-x))


def _silu(x):
    return x * _sigmoid(x)


def _rms_norm(x, g):
    return x * lax.rsqrt(jnp.mean(x * x, axis=-1, keepdims=True) + EPS) * g


def _dot(a, b):
    return jnp.dot(a, b, preferred_element_type=F32)


def _pack_bf16_pairs(x):
    w = x.shape[1] // 2
    bits = lax.bitcast_convert_type(x.astype(BF16).astype(F32), U32)
    return lax.bitcast_convert_type((bits[:, :w] >> 16) | (bits[:, w:] & U32(HI16)), I32)


def _unpack_bf16_pairs(p):
    bits = lax.bitcast_convert_type(p, U32)
    lo = lax.bitcast_convert_type(bits << 16, F32)
    hi = lax.bitcast_convert_type(bits & U32(HI16), F32)
    return jnp.concatenate([lo, hi], axis=1)


def _ada_kernel(c_ref, w_ref, b_ref, o_ref):
    s = _silu(c_ref[...])
    o_ref[...] = _dot(s.astype(BF16), w_ref[...].astype(BF16)) + b_ref[...]


def _ada(c, w, b):
    rows, d = c.shape
    n = w.shape[1]
    tn = ADA_TILE
    return pl.pallas_call(
        _ada_kernel,
        grid=(n // tn,),
        in_specs=[pl.BlockSpec((rows, d), lambda j: (0, 0)),
                  pl.BlockSpec((d, tn), lambda j: (0, j)),
                  pl.BlockSpec((1, tn), lambda j: (0, j))],
        out_specs=pl.BlockSpec((rows, tn), lambda j: (0, j)),
        out_shape=jax.ShapeDtypeStruct((rows, n), F32),
        compiler_params=pltpu.CompilerParams(dimension_semantics=("arbitrary",),
                                             vmem_limit_bytes=VMEM_LIMIT),
        name="ada",
    )(c, w, b.reshape(1, n))


def _mixer_tail(x, pool_d, conv_act, mod, w, xmid_ref, h2_ref, lgt_ref):
    sh2, sc2, g1, g2 = mod
    pw = pool_d.shape[1]
    pool_out = _dot(pool_d, w["pool"][...]) * w["pool_scale"][...]
    mix = _dot(pool_out.astype(BF16), w["out"][:pw, :]) + _dot(conv_act, w["out"][pw:, :])
    x1 = x + g1 * mix
    h2f = _rms_norm(x1, w["g_ffn"][...]) * (1.0 + sc2) + sh2
    h2_ref[...] = _pack_bf16_pairs(h2f)
    h2 = h2f.astype(BF16)
    lgt_ref[...] = lax.dot_general(w["router_t"][...], h2, (((1,), (1,)), ((), ())),
                                   preferred_element_type=F32)
    gu = _dot(h2, w["s_gu"][...])
    de = gu.shape[1] // 2
    hs = _silu(gu[:, :de]) * gu[:, de:]
    shared = _dot(hs.astype(BF16), w["s_down"][...])
    xmid_ref[...] = x1 + g2 * shared


_W_NAMES = ("g_mix", "in", "pool", "pool_scale", "dw", "b_dw", "ln_g", "ln_b", "out", "g_ffn",
            "router_t", "s_gu", "s_down")


def _layer_norm_silu(yc, g, b):
    mu = jnp.mean(yc, axis=-1, keepdims=True)
    yz = yc - mu
    var = jnp.mean(yz * yz, axis=-1, keepdims=True)
    return _silu(yz * lax.rsqrt(var + EPS) * g + b)


U_HALO = 32
A_HALO = 32


def _prompt_mixer_kernel(x_ref, mod_ref, *refs, tl, d, pw, cw, conv_k, mod_row0):
    nw = len(_W_NAMES)
    w = dict(zip(_W_NAMES, refs[:nw]))
    wg_ref, wu_ref, wd_ref = refs[nw + 2:nw + 5]
    xmid_ref, h2_ref, lgt_ref, npool_ref, nconv_ref, wgu_bf_ref, wd_bf_ref = refs[-16:-9]
    ubuf, s2buf, s4buf, s8buf, abuf, ashift, dbuf, cbuf, rowb = refs[-9:]
    de = wg_ref.shape[2]
    wgu_bf_ref[:, :, :de] = wg_ref[...].astype(BF16)
    wgu_bf_ref[:, :, de:] = wu_ref[...].astype(BF16)
    wd_bf_ref[...] = wd_ref[...].astype(BF16)
    l = pl.program_id(1)
    pool_buf = max(POOL_WINDOWS) - 1
    uh, ah = U_HALO, A_HALO
    pg = pw // len(POOL_WINDOWS)
    nu, na = uh + tl, ah + tl

    @pl.when(l == 0)
    def _():
        ubuf[0:uh, :] = jnp.zeros((uh, pw), F32)
        abuf[0:ah, :] = jnp.zeros((ah, cw), F32)

    @pl.when(l > 0)
    def _():
        ubuf[0:uh, :] = ubuf[tl:tl + uh, :]
        abuf[0:ah, :] = abuf[tl:tl + ah, :]

    x = x_ref[0]
    row = pl.ds((mod_row0 + pl.program_id(0)) % SUBLANES, 1)
    sh1, sc1, g1, sh2, sc2, g2 = [mod_ref[row, i * d:(i + 1) * d] for i in range(6)]

    h = _rms_norm(x, w["g_mix"][...]) * (1.0 + sc1) + sh1
    z = _dot(h.astype(BF16), w["in"][...])
    u = z[:, :pw]
    ubuf[uh:nu, :] = u
    abuf[ah:na, :] = z[:, pw:pw + cw] * _sigmoid(z[:, pw + cw:])

    s2buf[8:nu, :] = ubuf[8:nu, :] + ubuf[7:nu - 1, :]
    s4buf[16:nu, :] = s2buf[16:nu, pg:] + s2buf[14:nu - 2, pg:]
    s8buf[24:nu, :] = s4buf[24:nu, pg:] + s4buf[20:nu - 4, pg:]
    t = l * tl + lax.broadcasted_iota(I32, (tl, 1), 0)
    sums = (s2buf[uh:nu, 0:pg], s4buf[uh:nu, 0:pg], s8buf[uh:nu, 0:pg],
            s8buf[uh:nu, pg:2 * pg] + s8buf[uh - 8:nu - 8, pg:2 * pg])
    for g, win in enumerate(POOL_WINDOWS):
        cols = slice(g * pg, (g + 1) * pg)
        inv = 1.0 / jnp.minimum(win, t + 1).astype(F32)
        dbuf[:, cols] = (sums[g] * inv - u[:, cols]).astype(BF16)

    for r in range(1, SUBLANES):
        ashift[r - 1, 8:na, :] = abuf[8 - r:na - r, :]

    for k in range(conv_k):
        rowb[k] = jnp.broadcast_to(w["dw"][k:k + 1, :], (SUBLANES, cw))
    rowb[conv_k] = jnp.broadcast_to(w["b_dw"][...], (SUBLANES, cw))

    for c in range(tl // ROW_CHUNK):
        groups = []
        for o in range(ah + c * ROW_CHUNK, ah + (c + 1) * ROW_CHUNK, SUBLANES):
            yc = abuf[o:o + SUBLANES, :] * rowb[conv_k - 1]
            for back in range(1, conv_k):
                q, r = divmod(back, SUBLANES)
                src = abuf if r == 0 else ashift.at[r - 1]
                yc = yc + src[o - q * SUBLANES:o - (q - 1) * SUBLANES, :] * rowb[conv_k - 1 - back]
            groups.append(yc + rowb[conv_k])
        r0 = c * ROW_CHUNK
        cbuf[r0:r0 + ROW_CHUNK, :] = _layer_norm_silu(
            jnp.concatenate(groups, axis=0), w["ln_g"][...], w["ln_b"][...]).astype(BF16)

    @pl.when(l == pl.num_programs(1) - 1)
    def _():
        npool_ref[0] = ubuf[nu - pool_buf:nu, :]
        nconv_ref[0] = abuf[na - (conv_k - 1):na, :]

    _mixer_tail(x, dbuf[...], cbuf[...], (sh2, sc2, g1, g2), w, xmid_ref.at[0], h2_ref, lgt_ref)


def _sample_mixer_kernel(x_ref, mod_ref, sp_ref, sc_ref, *refs, d, pw, cw, conv_k):
    nw = len(_W_NAMES)
    w = dict(zip(_W_NAMES, refs[:nw]))
    xmid_ref, npool_ref, nconv_ref = refs[nw:nw + 3]
    tails = refs[nw + 3:-2]
    h2_tmp, lgt_tmp = refs[-2:]
    pool_buf = max(POOL_WINDOWS) - 1
    pg = pw // len(POOL_WINDOWS)
    x = x_ref[...]
    sh1, sc1, g1, sh2, sc2, g2 = [mod_ref[:, i * d:(i + 1) * d] for i in range(6)]
    h = _rms_norm(x, w["g_mix"][...]) * (1.0 + sc1) + sh1
    z = _dot(h.astype(BF16), w["in"][...])
    u = z[:, :pw]
    a = z[:, pw:pw + cw] * _sigmoid(z[:, pw + cw:])
    npool_ref[0:pool_buf - 1] = sp_ref[1:pool_buf]
    npool_ref[pool_buf - 1] = u
    nconv_ref[0:conv_k - 2] = sc_ref[1:conv_k - 1]
    nconv_ref[conv_k - 2] = a
    ds = []
    for g, win in enumerate(POOL_WINDOWS):
        cols = slice(g * pg, (g + 1) * pg)
        acc = u[:, cols]
        for j in range(1, win):
            acc = acc + sp_ref[pool_buf - j, :, cols]
        cnt = float(min(win, 1 + pool_buf))
        ds.append(acc / cnt - u[:, cols])
    pool_d = jnp.concatenate(ds, axis=-1).astype(BF16)
    yc = a * w["dw"][conv_k - 1:conv_k, :]
    for k in range(conv_k - 1):
        yc = yc + sc_ref[k] * w["dw"][k:k + 1, :]
    yc = yc + w["b_dw"][...]
    conv_act = _layer_norm_silu(yc, w["ln_g"][...], w["ln_b"][...]).astype(BF16)
    _mixer_tail(x, pool_d, conv_act, (sh2, sc2, g1, g2), w, xmid_ref, h2_tmp, lgt_tmp)
    n_parts = len(tails) // 2
    share = x.shape[0] // n_parts
    for p in range(n_parts):
        h2_ref, lgt_ref = tails[2 * p], tails[2 * p + 1]
        h2_ref[...] = jnp.zeros_like(h2_ref)
        lgt_ref[...] = jnp.zeros_like(lgt_ref)
        h2_ref[0:share, :] = h2_tmp[p * share:(p + 1) * share, :]
        lgt_ref[:, 0:share] = lgt_tmp[:, p * share:(p + 1) * share]


def _full_spec(a):
    nd = a.ndim
    return pl.BlockSpec(a.shape, lambda *_: (0,) * nd)


def _prompt_mixer(x, mod, mod_row0, wl, h2_all, lgt_all, b0, b, w_experts, w_bf_prev, *, conv_k):
    n_batch, seq, d = x.shape
    w_gate, w_up, w_down = w_experts
    n_exp, _, de = w_gate.shape
    pw = wl["pool"].shape[0]
    cw = wl["dw"].shape[1]
    ne = wl["router_t"].shape[0]
    tl = SEQ_TILE
    nl = seq // tl
    pg = pw // len(POOL_WINDOWS)
    assert POOL_WINDOWS == (2, 4, 8, 16) and conv_k - 1 <= A_HALO
    ws = [wl[n] for n in _W_NAMES]
    kern = functools.partial(_prompt_mixer_kernel, tl=tl, d=d, pw=pw, cw=cw, conv_k=conv_k, mod_row0=mod_row0 + b0)
    ecs = n_exp // (n_batch * nl)
    assert ecs * n_batch * nl == n_exp

    def w_map(i, j):
        return ((i + b0) * nl + j, 0, 0)

    ins = [x, mod, *ws, h2_all, lgt_all, w_gate, w_up, w_down]
    in_specs = ([pl.BlockSpec((1, tl, d), lambda i, j: (i + b0, j, 0)),
                 pl.BlockSpec((SUBLANES, mod.shape[-1]), lambda i, j: ((i + b0 + mod_row0) // SUBLANES, 0))]
                + [_full_spec(a) for a in ws]
                + [pl.BlockSpec(memory_space=pl.ANY), pl.BlockSpec(memory_space=pl.ANY),
                   pl.BlockSpec((ecs, d, de), w_map), pl.BlockSpec((ecs, d, de), w_map),
                   pl.BlockSpec((ecs, de, d), w_map)])
    aliases = {2 + len(ws): 1, 3 + len(ws): 2}
    if w_bf_prev is not None:
        aliases.update({len(ins): 5, len(ins) + 1: 6})
        ins += list(w_bf_prev)
        in_specs += [pl.BlockSpec(memory_space=pl.ANY), pl.BlockSpec(memory_space=pl.ANY)]
    return pl.pallas_call(
        kern,
        grid=(b, nl),
        in_specs=in_specs,
        out_specs=[pl.BlockSpec((1, tl, d), lambda i, j: (i, j, 0)),
                   pl.BlockSpec((tl, d // 2), lambda i, j: (i * nl + j, 0)),
                   pl.BlockSpec((ne, tl), lambda i, j: (0, i * nl + j)),
                   pl.BlockSpec((1, max(POOL_WINDOWS) - 1, pw), lambda i, j: (i, 0, 0)),
                   pl.BlockSpec((1, conv_k - 1, cw), lambda i, j: (i, 0, 0)),
                   pl.BlockSpec((ecs, d, 2 * de), w_map), pl.BlockSpec((ecs, de, d), w_map)],
        out_shape=[jax.ShapeDtypeStruct((b, seq, d), F32),
                   jax.ShapeDtypeStruct(h2_all.shape, I32),
                   jax.ShapeDtypeStruct(lgt_all.shape, F32),
                   jax.ShapeDtypeStruct((b, max(POOL_WINDOWS) - 1, pw), F32),
                   jax.ShapeDtypeStruct((b, conv_k - 1, cw), F32),
                   jax.ShapeDtypeStruct((n_exp, d, 2 * de), BF16),
                   jax.ShapeDtypeStruct((n_exp, de, d), BF16)],
        scratch_shapes=[pltpu.VMEM((U_HALO + tl, pw), F32), pltpu.VMEM((U_HALO + tl, pw), F32),
                        pltpu.VMEM((U_HALO + tl, pw - pg), F32), pltpu.VMEM((U_HALO + tl, pw - 2 * pg), F32),
                        pltpu.VMEM((A_HALO + tl, cw), F32), pltpu.VMEM((SUBLANES - 1, A_HALO + tl, cw), F32),
                        pltpu.VMEM((tl, pw), BF16), pltpu.VMEM((tl, cw), BF16),
                        pltpu.VMEM((conv_k + 1, SUBLANES, cw), F32)],
        compiler_params=pltpu.CompilerParams(dimension_semantics=("arbitrary", "arbitrary"),
                                             vmem_limit_bytes=VMEM_LIMIT),
        input_output_aliases=aliases,
        name="prompt_mixer",
    )(*ins)


def _sample_mixer(x, mod, sp, sc, wl, t_prompt, t_pad, *, conv_k):
    rows, d = x.shape
    tail = t_pad - t_prompt
    assert t_prompt % tail == 0 and rows % N_PARTS == 0 and rows // N_PARTS <= tail
    pw = wl["pool"].shape[0]
    cw = wl["dw"].shape[1]
    ne = wl["router_t"].shape[0]
    ws = [wl[n] for n in _W_NAMES]
    kern = functools.partial(_sample_mixer_kernel, d=d, pw=pw, cw=cw, conv_k=conv_k)
    ins = [x, mod, sp, sc] + ws
    tail_specs = [pl.BlockSpec((tail, d // 2), lambda i: (t_prompt // tail, 0)),
                  pl.BlockSpec((ne, tail), lambda i: (0, t_prompt // tail))] * N_PARTS
    tail_shapes = [jax.ShapeDtypeStruct((t_pad, d // 2), I32), jax.ShapeDtypeStruct((ne, t_pad), F32)] * N_PARTS
    outs = pl.pallas_call(
        kern,
        grid=(1,),
        in_specs=[_full_spec(x), pl.BlockSpec((rows, mod.shape[1]), lambda i: (0, 0))]
                 + [_full_spec(a) for a in ins[2:]],
        out_specs=[pl.BlockSpec((rows, d), lambda i: (0, 0)), _full_spec(sp), _full_spec(sc)] + tail_specs,
        out_shape=[jax.ShapeDtypeStruct((rows, d), F32),
                   jax.ShapeDtypeStruct(sp.shape, F32),
                   jax.ShapeDtypeStruct(sc.shape, F32)] + tail_shapes,
        scratch_shapes=[pltpu.VMEM((rows, d // 2), I32), pltpu.VMEM((ne, rows), F32)],
        compiler_params=pltpu.CompilerParams(dimension_semantics=("arbitrary",),
                                             vmem_limit_bytes=VMEM_LIMIT),
        name="sample_mixer",
    )(*ins)
    return outs[0], outs[1], outs[2], [(outs[3 + 2 * p], outs[4 + 2 * p]) for p in range(N_PARTS)]


def _sublane_max(x):
    return jnp.max(x, axis=0, keepdims=True)


def _route_kernel(lg_ref, b_ref, tri_ref, idx_ref, wt_ref, rank_ref, cnt_ref, carry, *, t_valid):
    i = pl.program_id(0)
    ne, tt = lg_ref.shape
    ng = N_EXPERT_GROUPS
    per = ne // ng

    @pl.when(i == 0)
    def _():
        carry[...] = jnp.zeros_like(carry)

    s = _sigmoid(lg_ref[...])
    sel = s + b_ref[...]
    s3 = [s[p * ng:(p + 1) * ng, :] for p in range(per)]
    sel3 = [sel[p * ng:(p + 1) * ng, :] for p in range(per)]
    m1 = sel3[0]
    m2 = jnp.full_like(m1, NEG_INF)
    for p in range(1, per):
        m2 = jnp.maximum(m2, jnp.minimum(m1, sel3[p]))
        m1 = jnp.maximum(m1, sel3[p])
    gs = m1 + m2
    gi = lax.broadcasted_iota(jnp.int32, (ng, tt), 0)
    beaten = jnp.zeros((ng, tt), jnp.int32)
    for g in range(ng):
        row = gs[g:g + 1, :]
        beats = (row > gs) | ((row == gs) & (gi > g))
        beaten = beaten + beats.astype(jnp.int32)
    keep = beaten < TOPK_GROUPS
    cur = [jnp.where(keep, sel3[p], NEG_INF) for p in range(per)]
    eid = [(gi * per + p).astype(F32) for p in range(per)]
    idxs, wts, hits = [], [], []
    for _ in range(TOP_K):
        m = cur[0]
        for p in range(1, per):
            m = jnp.maximum(m, cur[p])
        m = _sublane_max(m)
        cand = jnp.where(cur[0] == m, eid[0], float(ne))
        for p in range(1, per):
            cand = jnp.minimum(cand, jnp.where(cur[p] == m, eid[p], float(ne)))
        e_sel = jnp.min(cand, axis=0, keepdims=True)
        hit = [eid[p] == e_sel for p in range(per)]
        wk = jnp.where(hit[0], s3[0], 0.0)
        for p in range(1, per):
            wk = wk + jnp.where(hit[p], s3[p], 0.0)
        wts.append(jnp.sum(wk, axis=0, keepdims=True))
        cur = [jnp.where(hit[p], NEG_INF, cur[p]) for p in range(per)]
        idxs.append(e_sel)
        hits.append(hit)
    wsum = wts[0]
    for k in range(1, TOP_K):
        wsum = wsum + wts[k]
    chosen = [functools.reduce(lambda a, b: a | b, [hits[k][p] for k in range(TOP_K)]) for p in range(per)]
    real = (i * tt + lax.broadcasted_iota(I32, (1, tt), 1)) < t_valid
    onehot = jnp.concatenate([(c & real).astype(F32) for c in chosen], axis=0)
    before = _dot(onehot.astype(BF16), tri_ref[...]) + carry[...]
    for k in range(TOP_K):
        rk = jnp.where(hits[k][0], before[0:ng, :], 0.0)
        for p in range(1, per):
            rk = rk + jnp.where(hits[k][p], before[p * ng:(p + 1) * ng, :], 0.0)
        rank_ref[k:k + 1, :] = jnp.sum(rk, axis=0, keepdims=True).astype(jnp.int32)
        idx_ref[k:k + 1, :] = idxs[k].astype(jnp.int32)
        wt_ref[k:k + 1, :] = wts[k] / wsum * ROUTED_SCALE
    carry[...] = carry[...] + jnp.sum(onehot, axis=1, keepdims=True)
    cnt_ref[...] = carry[...]


def _route(lgt, b_perm, t_valid):
    ne, t = lgt.shape
    tt = ROUTE_TILE
    tri = (jnp.arange(tt)[:, None] < jnp.arange(tt)[None, :]).astype(BF16)
    return pl.pallas_call(
        functools.partial(_route_kernel, t_valid=t_valid),
        grid=(t // tt,),
        in_specs=[pl.BlockSpec((ne, tt), lambda i: (0, i)),
                  pl.BlockSpec((ne, 1), lambda i: (0, 0)),
                  pl.BlockSpec((tt, tt), lambda i: (0, 0))],
        out_specs=[pl.BlockSpec((TOP_K, tt), lambda i: (0, i)),
                   pl.BlockSpec((TOP_K, tt), lambda i: (0, i)),
                   pl.BlockSpec((TOP_K, tt), lambda i: (0, i)),
                   pl.BlockSpec((ne, 1), lambda i: (0, 0))],
        out_shape=[jax.ShapeDtypeStruct((TOP_K, t), jnp.int32),
                   jax.ShapeDtypeStruct((TOP_K, t), F32),
                   jax.ShapeDtypeStruct((TOP_K, t), jnp.int32),
                   jax.ShapeDtypeStruct((ne, 1), F32)],
        scratch_shapes=[pltpu.VMEM((ne, 1), F32)],
        compiler_params=pltpu.CompilerParams(dimension_semantics=("arbitrary",),
                                             vmem_limit_bytes=VMEM_LIMIT),
        name="route",
    )(lgt, b_perm, tri)


def _dest_kernel(pstart_ref, idx_ref, rank_ref, dest_ref, *, t_valid, spare_row):
    idx = idx_ref[...]
    k_top, tile = idx.shape

    def add_start(e, acc):
        return acc + jnp.where(idx == e, pstart_ref[e], 0)

    dest = lax.fori_loop(0, pstart_ref.shape[0], add_start, rank_ref[...])
    tok = pl.program_id(0) * tile + lax.broadcasted_iota(I32, idx.shape, 1)
    spare = spare_row + (tok - t_valid) * k_top + lax.broadcasted_iota(I32, idx.shape, 0)
    dest_ref[...] = jnp.where(tok < t_valid, dest, spare)


def _dest_rows(pstart, idx, rank, t_valid, spare_row):
    k_top, t = idx.shape
    tile = DEST_TILE
    spec = pl.BlockSpec((k_top, tile), lambda i, ps: (0, i))
    return pl.pallas_call(
        functools.partial(_dest_kernel, t_valid=t_valid, spare_row=spare_row),
        grid_spec=pltpu.PrefetchScalarGridSpec(num_scalar_prefetch=1, grid=(t // tile,),
                                               in_specs=[spec, spec], out_specs=spec),
        out_shape=jax.ShapeDtypeStruct((k_top, t), I32),
        compiler_params=pltpu.CompilerParams(dimension_semantics=("arbitrary",),
                                             vmem_limit_bytes=VMEM_LIMIT),
        name="dest_rows",
    )(pstart, idx, rank)


def _expert_kernel(first_ref, cnt_ref, xs_hbm, wgu_ref, wd_ref, y_hbm, xbuf, ybuf, in_sem, out_sem):
    step = pl.program_id(0)
    ne = first_ref.shape[0] - 1
    nbuf, rows = xbuf.shape[:2]
    per_step, de = wd_ref.shape[:2]
    n_total = first_ref[ne]

    def in_copy(g):
        return pltpu.make_async_copy(xs_hbm.at[pl.ds(g * rows, rows)], xbuf.at[g % nbuf], in_sem.at[g % nbuf])

    def out_copy(g):
        return pltpu.make_async_copy(ybuf.at[g % nbuf], y_hbm.at[pl.ds(g * rows, rows)], out_sem.at[g % nbuf])

    @pl.when(step == 0)
    def _():
        for g in range(nbuf - 1):
            @pl.when(g < n_total)
            def _():
                in_copy(g).start()

    def block(g, carry, *, j, e, b0):
        slot = g % nbuf

        @pl.when(g + nbuf - 1 < n_total)
        def _():
            in_copy(g + nbuf - 1).start()

        in_copy(g).wait()

        @pl.when(g >= nbuf)
        def _():
            out_copy(g - nbuf).wait()

        valid = cnt_ref[e] - (g - b0) * rows
        rid = lax.broadcasted_iota(I32, (rows, 1), 0)
        xs = jnp.where(rid < valid, _unpack_bf16_pairs(xbuf[slot]), 0.0).astype(BF16)
        gu = _dot(xs, wgu_ref[j])
        hb = _silu(gu[:, :de]) * gu[:, de:]
        ybuf[slot] = _pack_bf16_pairs(_dot(hb.astype(BF16), wd_ref[j]))
        out_copy(g).start()
        return carry

    for j in range(per_step):
        e = step * per_step + j
        b0 = first_ref[e]
        lax.fori_loop(b0, first_ref[e + 1], functools.partial(block, j=j, e=e, b0=b0), 0)

    @pl.when(step == pl.num_programs(0) - 1)
    def _():
        for back in range(nbuf, 0, -1):
            @pl.when(n_total >= back)
            def _():
                out_copy(n_total - back).wait()


def _experts(xs, w_gu, w_down, first_block, counts):
    p, dw = xs.shape
    ne, de, d = w_down.shape
    return pl.pallas_call(
        _expert_kernel,
        grid_spec=pltpu.PrefetchScalarGridSpec(
            num_scalar_prefetch=2,
            grid=(ne // EXPERTS_PER_STEP,),
            in_specs=[pl.BlockSpec(memory_space=pl.ANY),
                      pl.BlockSpec((EXPERTS_PER_STEP, d, 2 * de), lambda e, fb, cn: (e, 0, 0)),
                      pl.BlockSpec((EXPERTS_PER_STEP, de, d), lambda e, fb, cn: (e, 0, 0))],
            out_specs=pl.BlockSpec(memory_space=pl.ANY),
            scratch_shapes=[pltpu.VMEM((EXPERT_BUFFERS, EXPERT_ROWS, dw), I32),
                            pltpu.VMEM((EXPERT_BUFFERS, EXPERT_ROWS, dw), I32),
                            pltpu.SemaphoreType.DMA((EXPERT_BUFFERS,)), pltpu.SemaphoreType.DMA((EXPERT_BUFFERS,))]),
        out_shape=jax.ShapeDtypeStruct((p, dw), I32),
        compiler_params=pltpu.CompilerParams(dimension_semantics=("arbitrary",),
                                             vmem_limit_bytes=VMEM_LIMIT),
        name="experts",
    )(first_block, counts, xs, w_gu, w_down)


def _sc_first_chunk(n_chunks):
    per_worker = n_chunks // SC_WORKERS
    assert per_worker * SC_WORKERS == n_chunks
    return per_worker, (lax.axis_index("s") * SC_CORES + lax.axis_index("c")) * per_worker


def _sc_start(copies):
    for cp in copies:
        cp.start()


def _sc_wait(copies):
    for cp in copies:
        cp.wait()


def _sc_dispatch(h2, dest, n_rows):
    w = h2.shape[1]
    n_chunks, k_top, ch = dest.shape
    mesh = plsc.VectorSubcoreMesh(core_axis_name="c", subcore_axis_name="s")

    @functools.partial(
        pl.kernel, mesh=mesh, out_type=jax.ShapeDtypeStruct((n_rows, w), I32),
        scratch_types=[pltpu.VMEM((2, k_top, ch), I32), pltpu.VMEM((2, ch, w), I32),
                       pltpu.SemaphoreType.DMA((2,)), pltpu.SemaphoreType.DMA((2,))])
    def dispatch(h2_hbm, dest_hbm, xs_hbm, idx_v, rows_v, load_sem, store_sem):
        per_worker, first = _sc_first_chunk(n_chunks)

        def loads(i):
            slot = i % 2
            return (pltpu.make_async_copy(h2_hbm.at[pl.ds((first + i) * ch, ch)], rows_v.at[slot], load_sem.at[slot]),
                    pltpu.make_async_copy(dest_hbm.at[first + i], idx_v.at[slot], load_sem.at[slot]))

        def scatters(i):
            slot = i % 2
            return [pltpu.make_async_copy(rows_v.at[slot], xs_hbm.at[idx_v.at[slot, k]], store_sem.at[slot])
                    for k in range(k_top)]

        for i in range(min(2, per_worker)):
            _sc_start(loads(i))
        for i in range(per_worker):
            _sc_wait(loads(i))
            _sc_start(scatters(i))
            if 1 <= i < per_worker - 1:
                _sc_wait(scatters(i - 1))
                _sc_start(loads(i + 1))
        for i in range(max(per_worker - 2, 0), per_worker):
            _sc_wait(scatters(i))

    return dispatch(h2, dest)


def _sc_combine_gather(y, dest, t):
    w = y.shape[1]
    n_chunks, k_top, ch = dest.shape
    n_local = n_chunks // SC_WORKERS
    mesh = plsc.VectorSubcoreMesh(core_axis_name="c", subcore_axis_name="s")

    @functools.partial(
        pl.kernel, mesh=mesh, out_type=jax.ShapeDtypeStruct((k_top, t, w), I32),
        scratch_types=[pltpu.VMEM((n_local, k_top, ch), I32), pltpu.VMEM((k_top, ch, w), I32),
                       pltpu.SemaphoreType.DMA((k_top,)), pltpu.SemaphoreType.DMA((k_top,))])
    def gather(y_hbm, dest_hbm, out_hbm, idx_v, rows_v, load_sem, store_sem):
        per_worker, first = _sc_first_chunk(n_chunks)
        pltpu.sync_copy(dest_hbm.at[pl.ds(first, per_worker)], idx_v)

        def fetch(i, k):
            return pltpu.make_async_copy(y_hbm.at[idx_v.at[i, k]], rows_v.at[k], load_sem.at[k])

        def put(i, k):
            return pltpu.make_async_copy(rows_v.at[k], out_hbm.at[k, pl.ds((first + i) * ch, ch)], store_sem.at[k])

        @pl.loop(0, per_worker)
        def _(i):
            for k in range(k_top):
                @pl.when(i > 0)
                def _():
                    put(i - 1, k).wait()

                fetch(i, k).start()
            for k in range(k_top):
                fetch(i, k).wait()
                put(i, k).start()

        for k in range(k_top):
            put(per_worker - 1, k).wait()

    return gather(y, dest)


def _final_kernel(xmid_ref, yg_ref, wt_ref, g2_ref, shf_ref, scf_ref, gf_ref, *rest, vec_row0=None):
    o_ref = rest[-1]
    if vec_row0 is not None:
        row = pl.ds((vec_row0 + pl.program_id(0)) % SUBLANES, 1)
        g2_ref, shf_ref, scf_ref = g2_ref.at[row, :], shf_ref.at[row, :], scf_ref.at[row, :]
    wt = wt_ref[...].T[:xmid_ref.shape[0]]
    routed = wt[:, 0:1] * _unpack_bf16_pairs(yg_ref[0])
    for k in range(1, yg_ref.shape[0]):
        routed = routed + wt[:, k:k + 1] * _unpack_bf16_pairs(yg_ref[k])
    x2 = xmid_ref[...] + g2_ref[...] * routed
    o_ref[...] = _rms_norm(x2, gf_ref[...]) * (1.0 + scf_ref[...]) + shf_ref[...]


def _final_prompt(xmid, yg, wts_t, mod, modf, mod_row0, g_final, b0, n_batch, out_prev):
    b, seq, d = xmid.shape
    k_top, _, w = yg.shape
    tl = FINAL_TILE
    nl = seq // tl

    def vec(col):
        return pl.BlockSpec((SUBLANES, d), lambda i, j: ((i + b0 + mod_row0) // SUBLANES, col))

    in_specs = [pl.BlockSpec((None, tl, d), lambda i, j: (i, j, 0)),
                pl.BlockSpec((k_top, tl, w), lambda i, j: (0, i * nl + j, 0)),
                pl.BlockSpec((k_top, tl), lambda i, j: (0, i * nl + j)),
                vec(5), vec(0), vec(1), pl.BlockSpec((1, d), lambda i, j: (0, 0))]
    args = [xmid, yg, wts_t, mod, modf, modf, g_final]
    aliases = {}
    if out_prev is not None:
        in_specs.append(pl.BlockSpec(memory_space=pl.ANY))
        args.append(out_prev)
        aliases = {7: 0}
    return pl.pallas_call(
        functools.partial(_final_kernel, vec_row0=b0 + mod_row0),
        grid=(b, nl),
        in_specs=in_specs,
        out_specs=pl.BlockSpec((None, tl, d), lambda i, j: (i + b0, j, 0)),
        out_shape=jax.ShapeDtypeStruct((n_batch, seq, d), F32),
        compiler_params=pltpu.CompilerParams(dimension_semantics=("arbitrary", "arbitrary"),
                                             vmem_limit_bytes=VMEM_LIMIT),
        input_output_aliases=aliases,
        name="final_prompt",
    )(*args)


def _final_sample(xmid_all, yg, wts_t, mod, modf, g_final, part, rows, first_row, after):
    d = xmid_all.shape[1]
    k_top, _, w = yg.shape

    def vec(col):
        return pl.BlockSpec((rows, d), lambda i: (part, col))

    return pl.pallas_call(
        _final_kernel,
        grid=(1,),
        in_specs=[vec(0),
                  pl.BlockSpec((k_top, rows, w), lambda i: (0, first_row // rows, 0)),
                  pl.BlockSpec((k_top, LANES), lambda i: (0, first_row // LANES)),
                  vec(5), vec(0), vec(1), pl.BlockSpec((1, d), lambda i: (0, 0)), pl.BlockSpec(memory_space=pl.ANY)],
        out_specs=pl.BlockSpec((rows, d), lambda i: (0, 0)),
        out_shape=jax.ShapeDtypeStruct((rows, d), F32),
        compiler_params=pltpu.CompilerParams(dimension_semantics=("arbitrary",),
                                             vmem_limit_bytes=VMEM_LIMIT),
        name="final_sample",
    )(xmid_all, yg, wts_t, mod, modf, modf, g_final, after)


def _block_diag_pairs(w_pool):
    g, c, _ = w_pool.shape
    eye = jnp.eye(g, dtype=w_pool.dtype)
    return (eye[:, None, :, None] * w_pool[:, :, None, :]).reshape(g * c, g * c)


def kernel(x_prompt, x_sample, state_pool, state_conv, c_prompt, c_sample, w_ada, b_ada, g_mix, w_in, w_pool, pool_scale, w_dw, b_dw, ln_g, ln_b, w_out, g_ffn, w_router, b_router, w_gate, w_up, w_down, ws_gate, ws_up, ws_down, w_ada_final, b_ada_final, g_final):
    bp, seq, d = x_prompt.shape
    bs = x_sample.shape[0]
    depth = w_ada.shape[0]
    assert depth == 1 and x_sample.shape[1] == 1
    conv_k = w_dw.shape[1]
    ne = w_router.shape[-1]
    per = ne // N_EXPERT_GROUPS

    row_expert = (jnp.arange(ne) % N_EXPERT_GROUPS) * per + jnp.arange(ne) // N_EXPERT_GROUPS
    wl = {
        "g_mix": g_mix[0][None, :],
        "in": w_in[0].astype(BF16),
        "pool": _block_diag_pairs(w_pool[0]).astype(BF16),
        "pool_scale": pool_scale[0][None, :],
        "dw": w_dw[0],
        "b_dw": b_dw[0][None, :],
        "ln_g": ln_g[0][None, :],
        "ln_b": ln_b[0][None, :],
        "out": w_out[0].astype(BF16),
        "g_ffn": g_ffn[0][None, :],
        "router_t": w_router[0].T[row_expert].astype(BF16),
        "s_gu": jnp.concatenate([ws_gate[0], ws_up[0]], axis=1).astype(BF16),
        "s_down": ws_down[0].astype(BF16),
    }
    b_perm = b_router[0][row_expert][:, None]

    assert bp % N_PARTS == 0 and bs % N_PARTS == 0
    bpp, bsp = bp // N_PARTS, bs // N_PARTS
    tpp = bpp * seq
    t_part = tpp + bsp
    grain = SC_WORKERS * SC_DISPATCH_CHUNK * SC_COMBINE_CHUNK // math.gcd(SC_DISPATCH_CHUNK, SC_COMBINE_CHUNK)
    t_pad = (t_part + grain - 1) // grain * grain
    assert t_pad % ROUTE_TILE == 0 and t_pad % DEST_TILE == 0 and tpp % bsp == 0

    c_all = jnp.concatenate([c_sample, c_prompt], axis=0)
    mod = _ada(c_all, w_ada[0], b_ada[0])
    modf = _ada(c_all, w_ada_final, b_ada_final)

    xmid_s, new_pool_t, new_conv_t, tails = _sample_mixer(
        x_sample.reshape(bs, d), mod, jnp.transpose(state_pool[0], (1, 0, 2)),
        jnp.transpose(state_conv[0], (1, 0, 2)), wl, tpp, t_pad, conv_k=conv_k)
    new_pool_s = jnp.transpose(new_pool_t, (1, 0, 2))
    new_conv_s = jnp.transpose(new_conv_t, (1, 0, 2))
    w_experts = tuple(a.reshape(a.shape[1:]) for a in (w_gate, w_up, w_down))

    n_blocks = (t_part * TOP_K + EXPERT_ROWS - 1) // EXPERT_ROWS + ne
    n_spare = -(-(t_pad - t_part) * TOP_K // EXPERT_ROWS)
    n_rows = (n_blocks + n_spare) * EXPERT_ROWS

    y_prompt, y_samples, npools, nconvs, w_bf = None, [], [], [], None
    mixed = []
    for p in range(N_PARTS):
        h2, lgt = tails[p]
        xmid_p, h2, lgt, npool_p, nconv_p, *w_bf = _prompt_mixer(
            x_prompt, mod, bs, wl, h2, lgt, p * bpp, bpp, w_experts, w_bf, conv_k=conv_k)
        mixed.append((xmid_p, h2, lgt))
        npools.append(npool_p)
        nconvs.append(nconv_p)
    for p in range(N_PARTS):
        xmid_p, h2, lgt = mixed[p]
        idx, wts, rank, counts_perm = _route(lgt, b_perm, t_part)
        counts = jnp.zeros((ne,), I32).at[row_expert].set(counts_perm[:, 0].astype(I32))
        nblk = (counts + EXPERT_ROWS - 1) // EXPERT_ROWS
        first_block = jnp.concatenate([jnp.zeros((1,), I32), jnp.cumsum(nblk).astype(I32)])
        dest = _dest_rows(first_block[:ne] * EXPERT_ROWS, idx, rank, t_part, n_blocks * EXPERT_ROWS)

        def chunked(ch):
            return dest.reshape(TOP_K, t_pad // ch, ch).transpose(1, 0, 2)

        xs = _sc_dispatch(h2, chunked(SC_DISPATCH_CHUNK), n_rows)
        y = _experts(xs, w_bf[0], w_bf[1], first_block, counts)
        yg = _sc_combine_gather(y, chunked(SC_COMBINE_CHUNK), t_pad)

        y_samples.append(_final_sample(xmid_s, yg, wts, mod, modf, g_final[None, :], p, bsp, tpp,
                                       g_final if y_prompt is None else y_prompt))
        y_prompt = _final_prompt(xmid_p, yg, wts, mod, modf, bs, g_final[None, :], p * bpp, bp, y_prompt)
    y_sample = jnp.concatenate(y_samples, axis=0)
    npool_p = jnp.concatenate(npools, axis=0)
    nconv_p = jnp.concatenate(nconvs, axis=0)

    return (y_prompt, y_sample[:, None, :], npool_p[None], nconv_p[None], new_pool_s[None], new_conv_s[None])
```

```python
import functools
import math

import jax
import jax.numpy as jnp
from jax import lax
from jax.experimental import pallas as pl
from jax.experimental.pallas import tpu as pltpu
from jax.experimental.pallas import tpu_sc as plsc

POOL_WINDOWS = (2, 4, 8, 16)
N_EXPERT_GROUPS = 8
TOPK_GROUPS = 4
TOP_K = 8
ROUTED_SCALE = 2.5
EPS = 1e-6

LANES = 128
SUBLANES = 8
VMEM_LIMIT = 52 * 1024 * 1024

ADA_TILE = 1024
SEQ_TILE = 512
ROW_CHUNK = 16
EXPERT_ROWS = 512
N_PARTS = 2
EXPERT_BUFFERS = 6
EXPERTS_PER_STEP = 2
FINAL_TILE = 512
DEST_TILE = 2816

SC_CORES = 2
SC_SUBCORES = 16
SC_WORKERS = SC_CORES * SC_SUBCORES
SC_DISPATCH_CHUNK = 88
SC_COMBINE_CHUNK = 24

F32 = jnp.float32
BF16 = jnp.bfloat16
I32 = jnp.int32
U32 = jnp.uint32
NEG_INF = float("-inf")
HI16 = 0xFFFF0000


def _sigmoid(x):
    return 1.0 / (1.0 + jnp.exp(-x))


def _silu(x):
    return x * _sigmoid(x)


def _rms_norm(x, g):
    return x * lax.rsqrt(jnp.mean(x * x, axis=-1, keepdims=True) + EPS) * g


def _dot(a, b):
    return jnp.dot(a, b, preferred_element_type=F32)


def _pack_bf16_pairs(x):
    w = x.shape[1] // 2
    bits = lax.bitcast_convert_type(x.astype(BF16).astype(F32), U32)
    return lax.bitcast_convert_type((bits[:, :w] >> 16) | (bits[:, w:] & U32(HI16)), I32)


def _unpack_bf16_pairs(p):
    bits = lax.bitcast_convert_type(p, U32)
    lo = lax.bitcast_convert_type(bits << 16, F32)
    hi = lax.bitcast_convert_type(bits & U32(HI16), F32)
    return jnp.concatenate([lo, hi], axis=1)


def _ada_kernel(c_ref, w_ref, b_ref, o_ref):
    s = _silu(c_ref[...])
    o_ref[...] = _dot(s.astype(BF16), w_ref[...].astype(BF16)) + b_ref[...]


def _ada(c, w, b):
    rows, d = c.shape
    n = w.shape[1]
    tn = ADA_TILE
    return pl.pallas_call(
        _ada_kernel,
        grid=(n // tn,),
        in_specs=[pl.BlockSpec((rows, d), lambda j: (0, 0)),
                  pl.BlockSpec((d, tn), lambda j: (0, j)),
                  pl.BlockSpec((1, tn), lambda j: (0, j))],
        out_specs=pl.BlockSpec((rows, tn), lambda j: (0, j)),
        out_shape=jax.ShapeDtypeStruct((rows, n), F32),
        compiler_params=pltpu.CompilerParams(dimension_semantics=("arbitrary",),
                                             vmem_limit_bytes=VMEM_LIMIT),
        name="ada",
    )(c, w, b.reshape(1, n))


def _mixer_tail(x, pool_d, conv_act, mod, w, xmid_ref, h2_ref):
    sh2, sc2, g1, g2 = mod
    pw = pool_d.shape[1]
    pool_out = _dot(pool_d, w["pool"][...]) * w["pool_scale"][...]
    mix = _dot(pool_out.astype(BF16), w["out"][:pw, :]) + _dot(conv_act, w["out"][pw:, :])
    x1 = x + g1 * mix
    h2f = _rms_norm(x1, w["g_ffn"][...]) * (1.0 + sc2) + sh2
    h2_ref[...] = _pack_bf16_pairs(h2f)
    h2 = h2f.astype(BF16)
    lgt = lax.dot_general(w["router_t"][...], h2, (((1,), (1,)), ((), ())), preferred_element_type=F32)
    gu = _dot(h2, w["s_gu"][...])
    de = gu.shape[1] // 2
    hs = _silu(gu[:, :de]) * gu[:, de:]
    shared = _dot(hs.astype(BF16), w["s_down"][...])
    xmid_ref[...] = x1 + g2 * shared
    return lgt


_W_NAMES = ("g_mix", "in", "pool", "pool_scale", "dw", "b_dw", "ln_g", "ln_b", "out", "g_ffn",
            "router_t", "s_gu", "s_down")


def _layer_norm_silu(yc, g, b):
    mu = jnp.mean(yc, axis=-1, keepdims=True)
    yz = yc - mu
    var = jnp.mean(yz * yz, axis=-1, keepdims=True)
    return _silu(yz * lax.rsqrt(var + EPS) * g + b)


U_HALO = 32
A_HALO = 32


def _prompt_mixer_kernel(x_ref, mod_ref, *refs, tl, d, pw, cw, conv_k, mod_row0):
    nw = len(_W_NAMES)
    w = dict(zip(_W_NAMES, refs[:nw]))
    wg_ref, wu_ref, wd_ref = refs[nw + 4:nw + 7]
    bias_ref, tri_ref, cnt0_ref = refs[nw + 7:nw + 10]
    (xmid_ref, h2_ref, idx_ref, wt_ref, rank_ref, cnt_ref, npool_ref, nconv_ref,
     wgu_bf_ref, wd_bf_ref) = refs[-20:-10]
    ubuf, s2buf, s4buf, s8buf, abuf, ashift, dbuf, cbuf, rowb, carry = refs[-10:]
    de = wg_ref.shape[2]
    wgu_bf_ref[:, :, :de] = wg_ref[...].astype(BF16)
    wgu_bf_ref[:, :, de:] = wu_ref[...].astype(BF16)
    wd_bf_ref[...] = wd_ref[...].astype(BF16)
    l = pl.program_id(1)
    pool_buf = max(POOL_WINDOWS) - 1
    uh, ah = U_HALO, A_HALO
    pg = pw // len(POOL_WINDOWS)
    nu, na = uh + tl, ah + tl

    @pl.when((l == 0) & (pl.program_id(0) == 0))
    def _():
        carry[...] = cnt0_ref[...]

    @pl.when(l == 0)
    def _():
        ubuf[0:uh, :] = jnp.zeros((uh, pw), F32)
        abuf[0:ah, :] = jnp.zeros((ah, cw), F32)

    @pl.when(l > 0)
    def _():
        ubuf[0:uh, :] = ubuf[tl:tl + uh, :]
        abuf[0:ah, :] = abuf[tl:tl + ah, :]

    x = x_ref[0]
    row = pl.ds((mod_row0 + pl.program_id(0)) % SUBLANES, 1)
    sh1, sc1, g1, sh2, sc2, g2 = [mod_ref[row, i * d:(i + 1) * d] for i in range(6)]

    h = _rms_norm(x, w["g_mix"][...]) * (1.0 + sc1) + sh1
    z = _dot(h.astype(BF16), w["in"][...])
    u = z[:, :pw]
    ubuf[uh:nu, :] = u
    abuf[ah:na, :] = z[:, pw:pw + cw] * _sigmoid(z[:, pw + cw:])

    s2buf[8:nu, :] = ubuf[8:nu, :] + ubuf[7:nu - 1, :]
    s4buf[16:nu, :] = s2buf[16:nu, pg:] + s2buf[14:nu - 2, pg:]
    s8buf[24:nu, :] = s4buf[24:nu, pg:] + s4buf[20:nu - 4, pg:]
    t = l * tl + lax.broadcasted_iota(I32, (tl, 1), 0)
    sums = (s2buf[uh:nu, 0:pg], s4buf[uh:nu, 0:pg], s8buf[uh:nu, 0:pg],
            s8buf[uh:nu, pg:2 * pg] + s8buf[uh - 8:nu - 8, pg:2 * pg])
    for g, win in enumerate(POOL_WINDOWS):
        cols = slice(g * pg, (g + 1) * pg)
        inv = 1.0 / jnp.minimum(win, t + 1).astype(F32)
        dbuf[:, cols] = (sums[g] * inv - u[:, cols]).astype(BF16)

    for r in range(1, SUBLANES):
        ashift[r - 1, 8:na, :] = abuf[8 - r:na - r, :]

    for k in range(conv_k):
        rowb[k] = jnp.broadcast_to(w["dw"][k:k + 1, :], (SUBLANES, cw))
    rowb[conv_k] = jnp.broadcast_to(w["b_dw"][...], (SUBLANES, cw))

    for c in range(tl // ROW_CHUNK):
        groups = []
        for o in range(ah + c * ROW_CHUNK, ah + (c + 1) * ROW_CHUNK, SUBLANES):
            yc = abuf[o:o + SUBLANES, :] * rowb[conv_k - 1]
            for back in range(1, conv_k):
                q, r = divmod(back, SUBLANES)
                src = abuf if r == 0 else ashift.at[r - 1]
                yc = yc + src[o - q * SUBLANES:o - (q - 1) * SUBLANES, :] * rowb[conv_k - 1 - back]
            groups.append(yc + rowb[conv_k])
        r0 = c * ROW_CHUNK
        cbuf[r0:r0 + ROW_CHUNK, :] = _layer_norm_silu(
            jnp.concatenate(groups, axis=0), w["ln_g"][...], w["ln_b"][...]).astype(BF16)

    @pl.when(l == pl.num_programs(1) - 1)
    def _():
        npool_ref[0] = ubuf[nu - pool_buf:nu, :]
        nconv_ref[0] = abuf[na - (conv_k - 1):na, :]

    lgt = _mixer_tail(x, dbuf[...], cbuf[...], (sh2, sc2, g1, g2), w, xmid_ref.at[0], h2_ref)
    _route_tile(lgt, bias_ref[...], tri_ref[...], carry, jnp.full((1, tl), True), idx_ref, wt_ref, rank_ref)
    cnt_ref[...] = carry[...]


def _sample_mixer_kernel(x_ref, mod_ref, sp_ref, sc_ref, *refs, d, pw, cw, conv_k):
    nw = len(_W_NAMES)
    w = dict(zip(_W_NAMES, refs[:nw]))
    bias_ref, tri_ref = refs[nw:nw + 2]
    xmid_ref, npool_ref, nconv_ref = refs[nw + 2:nw + 5]
    tails = refs[nw + 5:-3]
    h2_tmp, lgt_buf, carry = refs[-3:]
    pool_buf = max(POOL_WINDOWS) - 1
    pg = pw // len(POOL_WINDOWS)
    x = x_ref[...]
    sh1, sc1, g1, sh2, sc2, g2 = [mod_ref[:, i * d:(i + 1) * d] for i in range(6)]
    h = _rms_norm(x, w["g_mix"][...]) * (1.0 + sc1) + sh1
    z = _dot(h.astype(BF16), w["in"][...])
    u = z[:, :pw]
    a = z[:, pw:pw + cw] * _sigmoid(z[:, pw + cw:])
    npool_ref[0:pool_buf - 1] = sp_ref[1:pool_buf]
    npool_ref[pool_buf - 1] = u
    nconv_ref[0:conv_k - 2] = sc_ref[1:conv_k - 1]
    nconv_ref[conv_k - 2] = a
    ds = []
    for g, win in enumerate(POOL_WINDOWS):
        cols = slice(g * pg, (g + 1) * pg)
        acc = u[:, cols]
        for j in range(1, win):
            acc = acc + sp_ref[pool_buf - j, :, cols]
        cnt = float(min(win, 1 + pool_buf))
        ds.append(acc / cnt - u[:, cols])
    pool_d = jnp.concatenate(ds, axis=-1).astype(BF16)
    yc = a * w["dw"][conv_k - 1:conv_k, :]
    for k in range(conv_k - 1):
        yc = yc + sc_ref[k] * w["dw"][k:k + 1, :]
    yc = yc + w["b_dw"][...]
    conv_act = _layer_norm_silu(yc, w["ln_g"][...], w["ln_b"][...]).astype(BF16)
    lgt = _mixer_tail(x, pool_d, conv_act, (sh2, sc2, g1, g2), w, xmid_ref, h2_tmp)
    n_parts = len(tails) // 5
    share = x.shape[0] // n_parts
    tail = lgt_buf.shape[1]
    real = lax.broadcasted_iota(I32, (1, tail), 1) < share
    for p in range(n_parts):
        h2_ref, idx_ref, wt_ref, rank_ref, cnt_ref = tails[5 * p:5 * p + 5]
        h2_ref[...] = jnp.zeros_like(h2_ref)
        h2_ref[0:share, :] = h2_tmp[p * share:(p + 1) * share, :]
        lgt_buf[...] = jnp.zeros_like(lgt_buf)
        lgt_buf[:, 0:share] = lgt[:, p * share:(p + 1) * share]
        carry[...] = jnp.zeros_like(carry)
        _route_tile(lgt_buf[...], bias_ref[...], tri_ref[...], carry, real, idx_ref, wt_ref, rank_ref)
        cnt_ref[...] = carry[...]


def _full_spec(a):
    nd = a.ndim
    return pl.BlockSpec(a.shape, lambda *_: (0,) * nd)


def _prompt_mixer(x, mod, mod_row0, wl, bufs, cnt0, b_perm, b0, b, w_experts, w_bf_prev, *, conv_k):
    n_batch, seq, d = x.shape
    w_gate, w_up, w_down = w_experts
    n_exp, _, de = w_gate.shape
    pw = wl["pool"].shape[0]
    cw = wl["dw"].shape[1]
    ne = wl["router_t"].shape[0]
    tl = SEQ_TILE
    nl = seq // tl
    pg = pw // len(POOL_WINDOWS)
    assert POOL_WINDOWS == (2, 4, 8, 16) and conv_k - 1 <= A_HALO
    ws = [wl[n] for n in _W_NAMES]
    kern = functools.partial(_prompt_mixer_kernel, tl=tl, d=d, pw=pw, cw=cw, conv_k=conv_k, mod_row0=mod_row0 + b0)
    ecs = n_exp // (n_batch * nl)
    assert ecs * n_batch * nl == n_exp

    def w_map(i, j):
        return ((i + b0) * nl + j, 0, 0)

    def tok_cols(i, j):
        return (0, i * nl + j)

    hbm = pl.BlockSpec(memory_space=pl.ANY)
    tri = _strict_upper(tl)
    ins = [x, mod, *ws, *bufs, w_gate, w_up, w_down, b_perm, tri, cnt0]
    in_specs = ([pl.BlockSpec((1, tl, d), lambda i, j: (i + b0, j, 0)),
                 pl.BlockSpec((SUBLANES, mod.shape[-1]), lambda i, j: ((i + b0 + mod_row0) // SUBLANES, 0))]
                + [_full_spec(a) for a in ws] + [hbm] * len(bufs)
                + [pl.BlockSpec((ecs, d, de), w_map), pl.BlockSpec((ecs, d, de), w_map),
                   pl.BlockSpec((ecs, de, d), w_map), _full_spec(b_perm), _full_spec(tri), _full_spec(cnt0)])
    aliases = {2 + len(ws) + k: 1 + k for k in range(len(bufs))}
    if w_bf_prev is not None:
        aliases.update({len(ins): 8, len(ins) + 1: 9})
        ins += list(w_bf_prev)
        in_specs += [hbm, hbm]
    k_top = bufs[1].shape[0]
    return pl.pallas_call(
        kern,
        grid=(b, nl),
        in_specs=in_specs,
        out_specs=[pl.BlockSpec((1, tl, d), lambda i, j: (i, j, 0)),
                   pl.BlockSpec((tl, d // 2), lambda i, j: (i * nl + j, 0)),
                   pl.BlockSpec((k_top, tl), tok_cols), pl.BlockSpec((k_top, tl), tok_cols),
                   pl.BlockSpec((k_top, tl), tok_cols),
                   pl.BlockSpec((ne, 1), lambda i, j: (0, 0)),
                   pl.BlockSpec((1, max(POOL_WINDOWS) - 1, pw), lambda i, j: (i, 0, 0)),
                   pl.BlockSpec((1, conv_k - 1, cw), lambda i, j: (i, 0, 0)),
                   pl.BlockSpec((ecs, d, 2 * de), w_map), pl.BlockSpec((ecs, de, d), w_map)],
        out_shape=[jax.ShapeDtypeStruct((b, seq, d), F32)]
                  + [jax.ShapeDtypeStruct(a.shape, a.dtype) for a in bufs]
                  + [jax.ShapeDtypeStruct((ne, 1), F32),
                     jax.ShapeDtypeStruct((b, max(POOL_WINDOWS) - 1, pw), F32),
                     jax.ShapeDtypeStruct((b, conv_k - 1, cw), F32),
                     jax.ShapeDtypeStruct((n_exp, d, 2 * de), BF16),
                     jax.ShapeDtypeStruct((n_exp, de, d), BF16)],
        scratch_shapes=[pltpu.VMEM((U_HALO + tl, pw), F32), pltpu.VMEM((U_HALO + tl, pw), F32),
                        pltpu.VMEM((U_HALO + tl, pw - pg), F32), pltpu.VMEM((U_HALO + tl, pw - 2 * pg), F32),
                        pltpu.VMEM((A_HALO + tl, cw), F32), pltpu.VMEM((SUBLANES - 1, A_HALO + tl, cw), F32),
                        pltpu.VMEM((tl, pw), BF16), pltpu.VMEM((tl, cw), BF16),
                        pltpu.VMEM((conv_k + 1, SUBLANES, cw), F32), pltpu.VMEM((ne, 1), F32)],
        compiler_params=pltpu.CompilerParams(dimension_semantics=("arbitrary", "arbitrary"),
                                             vmem_limit_bytes=VMEM_LIMIT),
        input_output_aliases=aliases,
        name="prompt_mixer",
    )(*ins)


def _sample_mixer(x, mod, sp, sc, wl, b_perm, t_prompt, t_pad, *, conv_k):
    rows, d = x.shape
    tail = t_pad - t_prompt
    assert t_prompt % tail == 0 and rows % N_PARTS == 0 and rows // N_PARTS <= tail
    pw = wl["pool"].shape[0]
    cw = wl["dw"].shape[1]
    ne = wl["router_t"].shape[0]
    ws = [wl[n] for n in _W_NAMES]
    kern = functools.partial(_sample_mixer_kernel, d=d, pw=pw, cw=cw, conv_k=conv_k)
    tri = _strict_upper(tail)
    ins = [x, mod, sp, sc] + ws + [b_perm, tri]
    row_tail = pl.BlockSpec((TOP_K, tail), lambda i: (0, t_prompt // tail))
    tail_specs = [pl.BlockSpec((tail, d // 2), lambda i: (t_prompt // tail, 0)), row_tail, row_tail, row_tail,
                  pl.BlockSpec((ne, 1), lambda i: (0, 0))] * N_PARTS
    tail_shapes = [jax.ShapeDtypeStruct((t_pad, d // 2), I32), jax.ShapeDtypeStruct((TOP_K, t_pad), I32),
                   jax.ShapeDtypeStruct((TOP_K, t_pad), F32), jax.ShapeDtypeStruct((TOP_K, t_pad), I32),
                   jax.ShapeDtypeStruct((ne, 1), F32)] * N_PARTS
    outs = pl.pallas_call(
        kern,
        grid=(1,),
        in_specs=[_full_spec(x), pl.BlockSpec((rows, mod.shape[1]), lambda i: (0, 0))]
                 + [_full_spec(a) for a in ins[2:]],
        out_specs=[pl.BlockSpec((rows, d), lambda i: (0, 0)), _full_spec(sp), _full_spec(sc)] + tail_specs,
        out_shape=[jax.ShapeDtypeStruct((rows, d), F32),
                   jax.ShapeDtypeStruct(sp.shape, F32),
                   jax.ShapeDtypeStruct(sc.shape, F32)] + tail_shapes,
        scratch_shapes=[pltpu.VMEM((rows, d // 2), I32), pltpu.VMEM((ne, tail), F32), pltpu.VMEM((ne, 1), F32)],
        compiler_params=pltpu.CompilerParams(dimension_semantics=("arbitrary",),
                                             vmem_limit_bytes=VMEM_LIMIT),
        name="sample_mixer",
    )(*ins)
    return outs[0], outs[1], outs[2], [tuple(outs[3 + 5 * p:8 + 5 * p]) for p in range(N_PARTS)]


def _sublane_max(x):
    return jnp.max(x, axis=0, keepdims=True)


def _route_tile(lg, bias, tri, carry, real, idx_ref, wt_ref, rank_ref):
    ne, tt = lg.shape
    ng = N_EXPERT_GROUPS
    per = ne // ng
    s = _sigmoid(lg)
    sel = s + bias
    s3 = [s[p * ng:(p + 1) * ng, :] for p in range(per)]
    sel3 = [sel[p * ng:(p + 1) * ng, :] for p in range(per)]
    m1 = sel3[0]
    m2 = jnp.full_like(m1, NEG_INF)
    for p in range(1, per):
        m2 = jnp.maximum(m2, jnp.minimum(m1, sel3[p]))
        m1 = jnp.maximum(m1, sel3[p])
    gs = m1 + m2
    gi = lax.broadcasted_iota(jnp.int32, (ng, tt), 0)
    beaten = jnp.zeros((ng, tt), jnp.int32)
    for g in range(ng):
        row = gs[g:g + 1, :]
        beats = (row > gs) | ((row == gs) & (gi > g))
        beaten = beaten + beats.astype(jnp.int32)
    keep = beaten < TOPK_GROUPS
    cur = [jnp.where(keep, sel3[p], NEG_INF) for p in range(per)]
    eid = [(gi * per + p).astype(F32) for p in range(per)]
    idxs, wts, hits = [], [], []
    for _ in range(TOP_K):
        m = cur[0]
        for p in range(1, per):
            m = jnp.maximum(m, cur[p])
        m = _sublane_max(m)
        cand = jnp.where(cur[0] == m, eid[0], float(ne))
        for p in range(1, per):
            cand = jnp.minimum(cand, jnp.where(cur[p] == m, eid[p], float(ne)))
        e_sel = jnp.min(cand, axis=0, keepdims=True)
        hit = [eid[p] == e_sel for p in range(per)]
        wk = jnp.where(hit[0], s3[0], 0.0)
        for p in range(1, per):
            wk = wk + jnp.where(hit[p], s3[p], 0.0)
        wts.append(jnp.sum(wk, axis=0, keepdims=True))
        cur = [jnp.where(hit[p], NEG_INF, cur[p]) for p in range(per)]
        idxs.append(e_sel)
        hits.append(hit)
    wsum = wts[0]
    for k in range(1, TOP_K):
        wsum = wsum + wts[k]
    chosen = [functools.reduce(lambda a, b: a | b, [hits[k][p] for k in range(TOP_K)]) for p in range(per)]
    onehot = jnp.concatenate([(c & real).astype(F32) for c in chosen], axis=0)
    before = _dot(onehot.astype(BF16), tri) + carry[...]
    for k in range(TOP_K):
        rk = jnp.where(hits[k][0], before[0:ng, :], 0.0)
        for p in range(1, per):
            rk = rk + jnp.where(hits[k][p], before[p * ng:(p + 1) * ng, :], 0.0)
        rank_ref[k:k + 1, :] = jnp.sum(rk, axis=0, keepdims=True).astype(jnp.int32)
        idx_ref[k:k + 1, :] = idxs[k].astype(jnp.int32)
        wt_ref[k:k + 1, :] = wts[k] / wsum * ROUTED_SCALE
    carry[...] = carry[...] + jnp.sum(onehot, axis=1, keepdims=True)


def _strict_upper(n):
    return (jnp.arange(n)[:, None] < jnp.arange(n)[None, :]).astype(BF16)


def _dest_kernel(pstart_ref, idx_ref, rank_ref, dest_ref, *, t_valid, spare_row):
    idx = idx_ref[...]
    k_top, tile = idx.shape

    def add_start(e, acc):
        return acc + jnp.where(idx == e, pstart_ref[e], 0)

    dest = lax.fori_loop(0, pstart_ref.shape[0], add_start, rank_ref[...])
    tok = pl.program_id(0) * tile + lax.broadcasted_iota(I32, idx.shape, 1)
    spare = spare_row + (tok - t_valid) * k_top + lax.broadcasted_iota(I32, idx.shape, 0)
    dest_ref[...] = jnp.where(tok < t_valid, dest, spare)


def _dest_rows(pstart, idx, rank, t_valid, spare_row):
    k_top, t = idx.shape
    tile = DEST_TILE
    spec = pl.BlockSpec((k_top, tile), lambda i, ps: (0, i))
    return pl.pallas_call(
        functools.partial(_dest_kernel, t_valid=t_valid, spare_row=spare_row),
        grid_spec=pltpu.PrefetchScalarGridSpec(num_scalar_prefetch=1, grid=(t // tile,),
                                               in_specs=[spec, spec], out_specs=spec),
        out_shape=jax.ShapeDtypeStruct((k_top, t), I32),
        compiler_params=pltpu.CompilerParams(dimension_semantics=("arbitrary",),
                                             vmem_limit_bytes=VMEM_LIMIT),
        name="dest_rows",
    )(pstart, idx, rank)


def _expert_kernel(first_ref, cnt_ref, xs_hbm, wgu_ref, wd_ref, y_hbm, xbuf, ybuf, in_sem, out_sem):
    step = pl.program_id(0)
    ne = first_ref.shape[0] - 1
    nbuf, rows = xbuf.shape[:2]
    per_step, de = wd_ref.shape[:2]
    n_total = first_ref[ne]

    def in_copy(g):
        return pltpu.make_async_copy(xs_hbm.at[pl.ds(g * rows, rows)], xbuf.at[g % nbuf], in_sem.at[g % nbuf])

    def out_copy(g):
        return pltpu.make_async_copy(ybuf.at[g % nbuf], y_hbm.at[pl.ds(g * rows, rows)], out_sem.at[g % nbuf])

    @pl.when(step == 0)
    def _():
        for g in range(nbuf - 1):
            @pl.when(g < n_total)
            def _():
                in_copy(g).start()

    def block(g, carry, *, j, e, b0):
        slot = g % nbuf

        @pl.when(g + nbuf - 1 < n_total)
        def _():
            in_copy(g + nbuf - 1).start()

        in_copy(g).wait()

        @pl.when(g >= nbuf)
        def _():
            out_copy(g - nbuf).wait()

        valid = cnt_ref[e] - (g - b0) * rows
        rid = lax.broadcasted_iota(I32, (rows, 1), 0)
        xs = jnp.where(rid < valid, _unpack_bf16_pairs(xbuf[slot]), 0.0).astype(BF16)
        gu = _dot(xs, wgu_ref[j])
        hb = _silu(gu[:, :de]) * gu[:, de:]
        ybuf[slot] = _pack_bf16_pairs(_dot(hb.astype(BF16), wd_ref[j]))
        out_copy(g).start()
        return carry

    for j in range(per_step):
        e = step * per_step + j
        b0 = first_ref[e]
        lax.fori_loop(b0, first_ref[e + 1], functools.partial(block, j=j, e=e, b0=b0), 0)

    @pl.when(step == pl.num_programs(0) - 1)
    def _():
        for back in range(nbuf, 0, -1):
            @pl.when(n_total >= back)
            def _():
                out_copy(n_total - back).wait()


def _experts(xs, w_gu, w_down, first_block, counts):
    p, dw = xs.shape
    ne, de, d = w_down.shape
    return pl.pallas_call(
        _expert_kernel,
        grid_spec=pltpu.PrefetchScalarGridSpec(
            num_scalar_prefetch=2,
            grid=(ne // EXPERTS_PER_STEP,),
            in_specs=[pl.BlockSpec(memory_space=pl.ANY),
                      pl.BlockSpec((EXPERTS_PER_STEP, d, 2 * de), lambda e, fb, cn: (e, 0, 0)),
                      pl.BlockSpec((EXPERTS_PER_STEP, de, d), lambda e, fb, cn: (e, 0, 0))],
            out_specs=pl.BlockSpec(memory_space=pl.ANY),
            scratch_shapes=[pltpu.VMEM((EXPERT_BUFFERS, EXPERT_ROWS, dw), I32),
                            pltpu.VMEM((EXPERT_BUFFERS, EXPERT_ROWS, dw), I32),
                            pltpu.SemaphoreType.DMA((EXPERT_BUFFERS,)), pltpu.SemaphoreType.DMA((EXPERT_BUFFERS,))]),
        out_shape=jax.ShapeDtypeStruct((p, dw), I32),
        compiler_params=pltpu.CompilerParams(dimension_semantics=("arbitrary",),
                                             vmem_limit_bytes=VMEM_LIMIT),
        name="experts",
    )(first_block, counts, xs, w_gu, w_down)


def _sc_first_chunk(n_chunks):
    per_worker = n_chunks // SC_WORKERS
    assert per_worker * SC_WORKERS == n_chunks
    return per_worker, (lax.axis_index("s") * SC_CORES + lax.axis_index("c")) * per_worker


def _sc_start(copies):
    for cp in copies:
        cp.start()


def _sc_wait(copies):
    for cp in copies:
        cp.wait()


def _sc_dispatch(h2, dest, n_rows):
    w = h2.shape[1]
    n_chunks, k_top, ch = dest.shape
    mesh = plsc.VectorSubcoreMesh(core_axis_name="c", subcore_axis_name="s")

    @functools.partial(
        pl.kernel, mesh=mesh, out_type=jax.ShapeDtypeStruct((n_rows, w), I32),
        scratch_types=[pltpu.VMEM((2, k_top, ch), I32), pltpu.VMEM((2, ch, w), I32),
                       pltpu.SemaphoreType.DMA((2,)), pltpu.SemaphoreType.DMA((2,))])
    def dispatch(h2_hbm, dest_hbm, xs_hbm, idx_v, rows_v, load_sem, store_sem):
        per_worker, first = _sc_first_chunk(n_chunks)

        def loads(i):
            slot = i % 2
            return (pltpu.make_async_copy(h2_hbm.at[pl.ds((first + i) * ch, ch)], rows_v.at[slot], load_sem.at[slot]),
                    pltpu.make_async_copy(dest_hbm.at[first + i], idx_v.at[slot], load_sem.at[slot]))

        def scatters(i):
            slot = i % 2
            return [pltpu.make_async_copy(rows_v.at[slot], xs_hbm.at[idx_v.at[slot, k]], store_sem.at[slot])
                    for k in range(k_top)]

        for i in range(min(2, per_worker)):
            _sc_start(loads(i))
        for i in range(per_worker):
            _sc_wait(loads(i))
            _sc_start(scatters(i))
            if 1 <= i < per_worker - 1:
                _sc_wait(scatters(i - 1))
                _sc_start(loads(i + 1))
        for i in range(max(per_worker - 2, 0), per_worker):
            _sc_wait(scatters(i))

    return dispatch(h2, dest)


def _sc_combine_gather(y, dest, t):
    w = y.shape[1]
    n_chunks, k_top, ch = dest.shape
    n_local = n_chunks // SC_WORKERS
    mesh = plsc.VectorSubcoreMesh(core_axis_name="c", subcore_axis_name="s")

    @functools.partial(
        pl.kernel, mesh=mesh, out_type=jax.ShapeDtypeStruct((k_top, t, w), I32),
        scratch_types=[pltpu.VMEM((n_local, k_top, ch), I32), pltpu.VMEM((k_top, ch, w), I32),
                       pltpu.SemaphoreType.DMA((k_top,)), pltpu.SemaphoreType.DMA((k_top,))])
    def gather(y_hbm, dest_hbm, out_hbm, idx_v, rows_v, load_sem, store_sem):
        per_worker, first = _sc_first_chunk(n_chunks)
        pltpu.sync_copy(dest_hbm.at[pl.ds(first, per_worker)], idx_v)

        def fetch(i, k):
            return pltpu.make_async_copy(y_hbm.at[idx_v.at[i, k]], rows_v.at[k], load_sem.at[k])

        def put(i, k):
            return pltpu.make_async_copy(rows_v.at[k], out_hbm.at[k, pl.ds((first + i) * ch, ch)], store_sem.at[k])

        @pl.loop(0, per_worker)
        def _(i):
            for k in range(k_top):
                @pl.when(i > 0)
                def _():
                    put(i - 1, k).wait()

                fetch(i, k).start()
            for k in range(k_top):
                fetch(i, k).wait()
                put(i, k).start()

        for k in range(k_top):
            put(per_worker - 1, k).wait()

    return gather(y, dest)


def _final_kernel(xmid_ref, yg_ref, wt_ref, g2_ref, shf_ref, scf_ref, gf_ref, *rest, vec_row0=None):
    o_ref = rest[-1]
    if vec_row0 is not None:
        row = pl.ds((vec_row0 + pl.program_id(0)) % SUBLANES, 1)
        g2_ref, shf_ref, scf_ref = g2_ref.at[row, :], shf_ref.at[row, :], scf_ref.at[row, :]
    wt = wt_ref[...].T[:xmid_ref.shape[0]]
    routed = wt[:, 0:1] * _unpack_bf16_pairs(yg_ref[0])
    for k in range(1, yg_ref.shape[0]):
        routed = routed + wt[:, k:k + 1] * _unpack_bf16_pairs(yg_ref[k])
    x2 = xmid_ref[...] + g2_ref[...] * routed
    o_ref[...] = _rms_norm(x2, gf_ref[...]) * (1.0 + scf_ref[...]) + shf_ref[...]


def _final_prompt(xmid, yg, wts_t, mod, modf, mod_row0, g_final, b0, n_batch, out_prev):
    b, seq, d = xmid.shape
    k_top, _, w = yg.shape
    tl = FINAL_TILE
    nl = seq // tl

    def vec(col):
        return pl.BlockSpec((SUBLANES, d), lambda i, j: ((i + b0 + mod_row0) // SUBLANES, col))

    in_specs = [pl.BlockSpec((None, tl, d), lambda i, j: (i, j, 0)),
                pl.BlockSpec((k_top, tl, w), lambda i, j: (0, i * nl + j, 0)),
                pl.BlockSpec((k_top, tl), lambda i, j: (0, i * nl + j)),
                vec(5), vec(0), vec(1), pl.BlockSpec((1, d), lambda i, j: (0, 0))]
    args = [xmid, yg, wts_t, mod, modf, modf, g_final]
    aliases = {}
    if out_prev is not None:
        in_specs.append(pl.BlockSpec(memory_space=pl.ANY))
        args.append(out_prev)
        aliases = {7: 0}
    return pl.pallas_call(
        functools.partial(_final_kernel, vec_row0=b0 + mod_row0),
        grid=(b, nl),
        in_specs=in_specs,
        out_specs=pl.BlockSpec((None, tl, d), lambda i, j: (i + b0, j, 0)),
        out_shape=jax.ShapeDtypeStruct((n_batch, seq, d), F32),
        compiler_params=pltpu.CompilerParams(dimension_semantics=("arbitrary", "arbitrary"),
                                             vmem_limit_bytes=VMEM_LIMIT),
        input_output_aliases=aliases,
        name="final_prompt",
    )(*args)


def _final_sample(xmid_all, yg, wts_t, mod, modf, g_final, part, rows, first_row, after):
    d = xmid_all.shape[1]
    k_top, _, w = yg.shape

    def vec(col):
        return pl.BlockSpec((rows, d), lambda i: (part, col))

    return pl.pallas_call(
        _final_kernel,
        grid=(1,),
        in_specs=[vec(0),
                  pl.BlockSpec((k_top, rows, w), lambda i: (0, first_row // rows, 0)),
                  pl.BlockSpec((k_top, LANES), lambda i: (0, first_row // LANES)),
                  vec(5), vec(0), vec(1), pl.BlockSpec((1, d), lambda i: (0, 0)), pl.BlockSpec(memory_space=pl.ANY)],
        out_specs=pl.BlockSpec((rows, d), lambda i: (0, 0)),
        out_shape=jax.ShapeDtypeStruct((rows, d), F32),
        compiler_params=pltpu.CompilerParams(dimension_semantics=("arbitrary",),
                                             vmem_limit_bytes=VMEM_LIMIT),
        name="final_sample",
    )(xmid_all, yg, wts_t, mod, modf, modf, g_final, after)


def _block_diag_pairs(w_pool):
    g, c, _ = w_pool.shape
    eye = jnp.eye(g, dtype=w_pool.dtype)
    return (eye[:, None, :, None] * w_pool[:, :, None, :]).reshape(g * c, g * c)


def kernel(x_prompt, x_sample, state_pool, state_conv, c_prompt, c_sample, w_ada, b_ada, g_mix, w_in, w_pool, pool_scale, w_dw, b_dw, ln_g, ln_b, w_out, g_ffn, w_router, b_router, w_gate, w_up, w_down, ws_gate, ws_up, ws_down, w_ada_final, b_ada_final, g_final):
    bp, seq, d = x_prompt.shape
    bs = x_sample.shape[0]
    depth = w_ada.shape[0]
    assert depth == 1 and x_sample.shape[1] == 1
    conv_k = w_dw.shape[1]
    ne = w_router.shape[-1]
    per = ne // N_EXPERT_GROUPS

    row_expert = (jnp.arange(ne) % N_EXPERT_GROUPS) * per + jnp.arange(ne) // N_EXPERT_GROUPS
    wl = {
        "g_mix": g_mix[0][None, :],
        "in": w_in[0].astype(BF16),
        "pool": _block_diag_pairs(w_pool[0]).astype(BF16),
        "pool_scale": pool_scale[0][None, :],
        "dw": w_dw[0],
        "b_dw": b_dw[0][None, :],
        "ln_g": ln_g[0][None, :],
        "ln_b": ln_b[0][None, :],
        "out": w_out[0].astype(BF16),
        "g_ffn": g_ffn[0][None, :],
        "router_t": w_router[0].T[row_expert].astype(BF16),
        "s_gu": jnp.concatenate([ws_gate[0], ws_up[0]], axis=1).astype(BF16),
        "s_down": ws_down[0].astype(BF16),
    }
    b_perm = b_router[0][row_expert][:, None]

    assert bp % N_PARTS == 0 and bs % N_PARTS == 0
    bpp, bsp = bp // N_PARTS, bs // N_PARTS
    tpp = bpp * seq
    t_part = tpp + bsp
    grain = SC_WORKERS * SC_DISPATCH_CHUNK * SC_COMBINE_CHUNK // math.gcd(SC_DISPATCH_CHUNK, SC_COMBINE_CHUNK)
    t_pad = (t_part + grain - 1) // grain * grain
    assert t_pad % DEST_TILE == 0 and tpp % bsp == 0

    c_all = jnp.concatenate([c_sample, c_prompt], axis=0)
    mod = _ada(c_all, w_ada[0], b_ada[0])
    modf = _ada(c_all, w_ada_final, b_ada_final)

    xmid_s, new_pool_t, new_conv_t, tails = _sample_mixer(
        x_sample.reshape(bs, d), mod, jnp.transpose(state_pool[0], (1, 0, 2)),
        jnp.transpose(state_conv[0], (1, 0, 2)), wl, b_perm, tpp, t_pad, conv_k=conv_k)
    new_pool_s = jnp.transpose(new_pool_t, (1, 0, 2))
    new_conv_s = jnp.transpose(new_conv_t, (1, 0, 2))
    w_experts = tuple(a.reshape(a.shape[1:]) for a in (w_gate, w_up, w_down))

    n_blocks = (t_part * TOP_K + EXPERT_ROWS - 1) // EXPERT_ROWS + ne
    n_spare = -(-(t_pad - t_part) * TOP_K // EXPERT_ROWS)
    n_rows = (n_blocks + n_spare) * EXPERT_ROWS

    y_prompt, y_samples, npools, nconvs, w_bf = None, [], [], [], None
    mixed = []
    for p in range(N_PARTS):
        xmid_p, h2, idx, wts, rank, counts_perm, npool_p, nconv_p, *w_bf = _prompt_mixer(
            x_prompt, mod, bs, wl, tails[p][:4], tails[p][4], b_perm, p * bpp, bpp, w_experts, w_bf,
            conv_k=conv_k)
        mixed.append((xmid_p, h2, idx, wts, rank, counts_perm))
        npools.append(npool_p)
        nconvs.append(nconv_p)
    for p in range(N_PARTS):
        xmid_p, h2, idx, wts, rank, counts_perm = mixed[p]
        counts = jnp.zeros((ne,), I32).at[row_expert].set(counts_perm[:, 0].astype(I32))
        nblk = (counts + EXPERT_ROWS - 1) // EXPERT_ROWS
        first_block = jnp.concatenate([jnp.zeros((1,), I32), jnp.cumsum(nblk).astype(I32)])
        dest = _dest_rows(first_block[:ne] * EXPERT_ROWS, idx, rank, t_part, n_blocks * EXPERT_ROWS)

        def chunked(ch):
            return dest.reshape(TOP_K, t_pad // ch, ch).transpose(1, 0, 2)

        xs = _sc_dispatch(h2, chunked(SC_DISPATCH_CHUNK), n_rows)
        y = _experts(xs, w_bf[0], w_bf[1], first_block, counts)
        yg = _sc_combine_gather(y, chunked(SC_COMBINE_CHUNK), t_pad)

        y_samples.append(_final_sample(xmid_s, yg, wts, mod, modf, g_final[None, :], p, bsp, tpp,
                                       g_final if y_prompt is None else y_prompt))
        y_prompt = _final_prompt(xmid_p, yg, wts, mod, modf, bs, g_final[None, :], p * bpp, bp, y_prompt)
    y_sample = jnp.concatenate(y_samples, axis=0)
    npool_p = jnp.concatenate(npools, axis=0)
    nconv_p = jnp.concatenate(nconvs, axis=0)

    return (y_prompt, y_sample[:, None, :], npool_p[None], nconv_p[None], new_pool_s[None], new_conv_s[None])
```

```python
import functools
import math

import jax
import jax.numpy as jnp
from jax import lax
from jax.experimental import pallas as pl
from jax.experimental.pallas import tpu as pltpu
from jax.experimental.pallas import tpu_sc as plsc

POOL_WINDOWS = (2, 4, 8, 16)
N_EXPERT_GROUPS = 8
TOPK_GROUPS = 4
TOP_K = 8
ROUTED_SCALE = 2.5
EPS = 1e-6

LANES = 128
SUBLANES = 8
VMEM_LIMIT = 52 * 1024 * 1024

ADA_TILE = 1024
SEQ_TILE = 512
ROW_CHUNK = 16
EXPERT_ROWS = 512
N_PARTS = 2
EXPERT_BUFFERS = 6
EXPERTS_PER_STEP = 2
FINAL_TILE = 512
DEST_TILE = 2816

SC_CORES = 2
SC_SUBCORES = 16
SC_WORKERS = SC_CORES * SC_SUBCORES
SC_DISPATCH_CHUNK = 88
SC_COMBINE_CHUNK = 24

F32 = jnp.float32
BF16 = jnp.bfloat16
I32 = jnp.int32
U32 = jnp.uint32
NEG_INF = float("-inf")
HI16 = 0xFFFF0000


def _sigmoid(x):
    return 1.0 / (1.0 + jnp.exp(-x))


def _silu(x):
    return x * _sigmoid(x)


def _rms_norm(x, g):
    return x * lax.rsqrt(jnp.mean(x * x, axis=-1, keepdims=True) + EPS) * g


def _rms_norm_mod(x, g, scale, shift):
    return x * lax.rsqrt(jnp.mean(x * x, axis=-1, keepdims=True) + EPS) * (g * (1.0 + scale)) + shift


def _dot(a, b):
    return jnp.dot(a, b, preferred_element_type=F32)


def _pack_bf16_pairs(x):
    w = x.shape[1] // 2
    bits = lax.bitcast_convert_type(x.astype(BF16).astype(F32), U32)
    return lax.bitcast_convert_type((bits[:, :w] >> 16) | (bits[:, w:] & U32(HI16)), I32)


def _unpack_bf16_pairs(p):
    bits = lax.bitcast_convert_type(p, U32)
    lo = lax.bitcast_convert_type(bits << 16, F32)
    hi = lax.bitcast_convert_type(bits & U32(HI16), F32)
    return jnp.concatenate([lo, hi], axis=1)


def _ada_kernel(c_ref, w_ref, b_ref, o_ref):
    s = _silu(c_ref[...])
    o_ref[...] = _dot(s.astype(BF16), w_ref[...].astype(BF16)) + b_ref[...]


def _ada(c, w, b):
    rows, d = c.shape
    n = w.shape[1]
    tn = ADA_TILE
    return pl.pallas_call(
        _ada_kernel,
        grid=(n // tn,),
        in_specs=[pl.BlockSpec((rows, d), lambda j: (0, 0)),
                  pl.BlockSpec((d, tn), lambda j: (0, j)),
                  pl.BlockSpec((1, tn), lambda j: (0, j))],
        out_specs=pl.BlockSpec((rows, tn), lambda j: (0, j)),
        out_shape=jax.ShapeDtypeStruct((rows, n), F32),
        compiler_params=pltpu.CompilerParams(dimension_semantics=("arbitrary",),
                                             vmem_limit_bytes=VMEM_LIMIT),
        name="ada",
    )(c, w, b.reshape(1, n))


def _mixer_tail(x, pool_d, conv_act, mod, w, xmid_ref, h2_ref):
    sh2, sc2, g1, g2 = mod
    pw = pool_d.shape[1]
    pool_out = _dot(pool_d, w["pool"][...]) * w["pool_scale"][...]
    mix = _dot(pool_out.astype(BF16), w["out"][:pw, :]) + _dot(conv_act, w["out"][pw:, :])
    x1 = x + g1 * mix
    h2f = _rms_norm_mod(x1, w["g_ffn"][...], sc2, sh2)
    h2_ref[...] = _pack_bf16_pairs(h2f)
    h2 = h2f.astype(BF16)
    lgt = lax.dot_general(w["router_t"][...], h2, (((1,), (1,)), ((), ())), preferred_element_type=F32)
    gu = _dot(h2, w["s_gu"][...])
    de = gu.shape[1] // 2
    hs = _silu(gu[:, :de]) * gu[:, de:]
    shared = _dot(hs.astype(BF16), w["s_down"][...])
    xmid_ref[...] = x1 + g2 * shared
    return lgt


_W_NAMES = ("g_mix", "in", "pool", "pool_scale", "dw", "b_dw", "ln_g", "ln_b", "out", "g_ffn",
            "router_t", "s_gu", "s_down")


def _layer_norm_silu(yc, g, b):
    mu = jnp.mean(yc, axis=-1, keepdims=True)
    yz = yc - mu
    var = jnp.mean(yz * yz, axis=-1, keepdims=True)
    return _silu(yz * lax.rsqrt(var + EPS) * g + b)


U_HALO = 32
A_HALO = 32


def _prompt_mixer_kernel(x_ref, mod_ref, *refs, tl, d, pw, cw, conv_k, mod_row0):
    nw = len(_W_NAMES)
    w = dict(zip(_W_NAMES, refs[:nw]))
    wg_ref, wu_ref, wd_ref = refs[nw + 4:nw + 7]
    bias_ref, tri_ref, cnt0_ref = refs[nw + 7:nw + 10]
    (xmid_ref, h2_ref, idx_ref, wt_ref, rank_ref, cnt_ref, npool_ref, nconv_ref,
     wgu_bf_ref, wd_bf_ref) = refs[-20:-10]
    ubuf, s2buf, s4buf, s8buf, abuf, ashift, dbuf, cbuf, rowb, carry = refs[-10:]
    de = wg_ref.shape[2]
    wgu_bf_ref[:, :, :de] = wg_ref[...].astype(BF16)
    wgu_bf_ref[:, :, de:] = wu_ref[...].astype(BF16)
    wd_bf_ref[...] = wd_ref[...].astype(BF16)
    l = pl.program_id(1)
    pool_buf = max(POOL_WINDOWS) - 1
    uh, ah = U_HALO, A_HALO
    pg = pw // len(POOL_WINDOWS)
    nu, na = uh + tl, ah + tl

    @pl.when((l == 0) & (pl.program_id(0) == 0))
    def _():
        carry[...] = cnt0_ref[...]

    @pl.when(l == 0)
    def _():
        ubuf[0:uh, :] = jnp.zeros((uh, pw), F32)
        abuf[0:ah, :] = jnp.zeros((ah, cw), F32)

    @pl.when(l > 0)
    def _():
        ubuf[0:uh, :] = ubuf[tl:tl + uh, :]
        abuf[0:ah, :] = abuf[tl:tl + ah, :]

    x = x_ref[0]
    row = pl.ds((mod_row0 + pl.program_id(0)) % SUBLANES, 1)
    sh1, sc1, g1, sh2, sc2, g2 = [mod_ref[row, i * d:(i + 1) * d] for i in range(6)]

    h = _rms_norm_mod(x, w["g_mix"][...], sc1, sh1)
    z = _dot(h.astype(BF16), w["in"][...])
    u = z[:, :pw]
    ubuf[uh:nu, :] = u
    abuf[ah:na, :] = z[:, pw:pw + cw] * _sigmoid(z[:, pw + cw:])

    s2buf[8:nu, :] = ubuf[8:nu, :] + ubuf[7:nu - 1, :]
    s4buf[16:nu, :] = s2buf[16:nu, pg:] + s2buf[14:nu - 2, pg:]
    s8buf[24:nu, :] = s4buf[24:nu, pg:] + s4buf[20:nu - 4, pg:]
    t = l * tl + lax.broadcasted_iota(I32, (tl, 1), 0)
    sums = (s2buf[uh:nu, 0:pg], s4buf[uh:nu, 0:pg], s8buf[uh:nu, 0:pg],
            s8buf[uh:nu, pg:2 * pg] + s8buf[uh - 8:nu - 8, pg:2 * pg])
    for g, win in enumerate(POOL_WINDOWS):
        cols = slice(g * pg, (g + 1) * pg)
        inv = 1.0 / jnp.minimum(win, t + 1).astype(F32)
        dbuf[:, cols] = (sums[g] * inv - u[:, cols]).astype(BF16)

    for r in range(1, SUBLANES):
        ashift[r - 1, 8:na, :] = abuf[8 - r:na - r, :]

    for k in range(conv_k):
        rowb[k] = jnp.broadcast_to(w["dw"][k:k + 1, :], (SUBLANES, cw))
    rowb[conv_k] = jnp.broadcast_to(w["b_dw"][...], (SUBLANES, cw))

    for c in range(tl // ROW_CHUNK):
        groups = []
        for o in range(ah + c * ROW_CHUNK, ah + (c + 1) * ROW_CHUNK, SUBLANES):
            yc = abuf[o:o + SUBLANES, :] * rowb[conv_k - 1]
            for back in range(1, conv_k):
                q, r = divmod(back, SUBLANES)
                src = abuf if r == 0 else ashift.at[r - 1]
                yc = yc + src[o - q * SUBLANES:o - (q - 1) * SUBLANES, :] * rowb[conv_k - 1 - back]
            groups.append(yc + rowb[conv_k])
        r0 = c * ROW_CHUNK
        cbuf[r0:r0 + ROW_CHUNK, :] = _layer_norm_silu(
            jnp.concatenate(groups, axis=0), w["ln_g"][...], w["ln_b"][...]).astype(BF16)

    @pl.when(l == pl.num_programs(1) - 1)
    def _():
        npool_ref[0] = ubuf[nu - pool_buf:nu, :]
        nconv_ref[0] = abuf[na - (conv_k - 1):na, :]

    lgt = _mixer_tail(x, dbuf[...], cbuf[...], (sh2, sc2, g1, g2), w, xmid_ref.at[0], h2_ref)
    _route_tile(lgt, bias_ref[...], tri_ref[...], carry, jnp.full((1, tl), True), idx_ref, wt_ref, rank_ref)
    cnt_ref[...] = carry[...]


def _sample_mixer_kernel(x_ref, mod_ref, sp_ref, sc_ref, *refs, d, pw, cw, conv_k):
    nw = len(_W_NAMES)
    w = dict(zip(_W_NAMES, refs[:nw]))
    bias_ref, tri_ref = refs[nw:nw + 2]
    xmid_ref, npool_ref, nconv_ref = refs[nw + 2:nw + 5]
    tails = refs[nw + 5:-3]
    h2_tmp, lgt_buf, carry = refs[-3:]
    pool_buf = max(POOL_WINDOWS) - 1
    pg = pw // len(POOL_WINDOWS)
    x = x_ref[...]
    sh1, sc1, g1, sh2, sc2, g2 = [mod_ref[:, i * d:(i + 1) * d] for i in range(6)]
    h = _rms_norm_mod(x, w["g_mix"][...], sc1, sh1)
    z = _dot(h.astype(BF16), w["in"][...])
    u = z[:, :pw]
    a = z[:, pw:pw + cw] * _sigmoid(z[:, pw + cw:])
    npool_ref[0:pool_buf - 1] = sp_ref[1:pool_buf]
    npool_ref[pool_buf - 1] = u
    nconv_ref[0:conv_k - 2] = sc_ref[1:conv_k - 1]
    nconv_ref[conv_k - 2] = a
    ds = []
    for g, win in enumerate(POOL_WINDOWS):
        cols = slice(g * pg, (g + 1) * pg)
        acc = u[:, cols]
        for j in range(1, win):
            acc = acc + sp_ref[pool_buf - j, :, cols]
        cnt = float(min(win, 1 + pool_buf))
        ds.append(acc / cnt - u[:, cols])
    pool_d = jnp.concatenate(ds, axis=-1).astype(BF16)
    yc = a * w["dw"][conv_k - 1:conv_k, :]
    for k in range(conv_k - 1):
        yc = yc + sc_ref[k] * w["dw"][k:k + 1, :]
    yc = yc + w["b_dw"][...]
    conv_act = _layer_norm_silu(yc, w["ln_g"][...], w["ln_b"][...]).astype(BF16)
    lgt = _mixer_tail(x, pool_d, conv_act, (sh2, sc2, g1, g2), w, xmid_ref, h2_tmp)
    n_parts = len(tails) // 5
    share = x.shape[0] // n_parts
    tail = lgt_buf.shape[1]
    real = lax.broadcasted_iota(I32, (1, tail), 1) < share
    for p in range(n_parts):
        h2_ref, idx_ref, wt_ref, rank_ref, cnt_ref = tails[5 * p:5 * p + 5]
        h2_ref[...] = jnp.zeros_like(h2_ref)
        h2_ref[0:share, :] = h2_tmp[p * share:(p + 1) * share, :]
        lgt_buf[...] = jnp.zeros_like(lgt_buf)
        lgt_buf[:, 0:share] = lgt[:, p * share:(p + 1) * share]
        carry[...] = jnp.zeros_like(carry)
        _route_tile(lgt_buf[...], bias_ref[...], tri_ref[...], carry, real, idx_ref, wt_ref, rank_ref)
        cnt_ref[...] = carry[...]


def _full_spec(a):
    nd = a.ndim
    return pl.BlockSpec(a.shape, lambda *_: (0,) * nd)


def _prompt_mixer(x, mod, mod_row0, wl, bufs, cnt0, b_perm, b0, b, w_experts, w_bf_prev, *, conv_k):
    n_batch, seq, d = x.shape
    w_gate, w_up, w_down = w_experts
    n_exp, _, de = w_gate.shape
    pw = wl["pool"].shape[0]
    cw = wl["dw"].shape[1]
    ne = wl["router_t"].shape[0]
    tl = SEQ_TILE
    nl = seq // tl
    pg = pw // len(POOL_WINDOWS)
    assert POOL_WINDOWS == (2, 4, 8, 16) and conv_k - 1 <= A_HALO
    ws = [wl[n] for n in _W_NAMES]
    kern = functools.partial(_prompt_mixer_kernel, tl=tl, d=d, pw=pw, cw=cw, conv_k=conv_k, mod_row0=mod_row0 + b0)
    ecs = n_exp // (n_batch * nl)
    assert ecs * n_batch * nl == n_exp

    def w_map(i, j):
        return ((i + b0) * nl + j, 0, 0)

    def tok_cols(i, j):
        return (0, i * nl + j)

    hbm = pl.BlockSpec(memory_space=pl.ANY)
    tri = _strict_upper(tl)
    ins = [x, mod, *ws, *bufs, w_gate, w_up, w_down, b_perm, tri, cnt0]
    in_specs = ([pl.BlockSpec((1, tl, d), lambda i, j: (i + b0, j, 0)),
                 pl.BlockSpec((SUBLANES, mod.shape[-1]), lambda i, j: ((i + b0 + mod_row0) // SUBLANES, 0))]
                + [_full_spec(a) for a in ws] + [hbm] * len(bufs)
                + [pl.BlockSpec((ecs, d, de), w_map), pl.BlockSpec((ecs, d, de), w_map),
                   pl.BlockSpec((ecs, de, d), w_map), _full_spec(b_perm), _full_spec(tri), _full_spec(cnt0)])
    aliases = {2 + len(ws) + k: 1 + k for k in range(len(bufs))}
    if w_bf_prev is not None:
        aliases.update({len(ins): 8, len(ins) + 1: 9})
        ins += list(w_bf_prev)
        in_specs += [hbm, hbm]
    k_top = bufs[1].shape[0]
    return pl.pallas_call(
        kern,
        grid=(b, nl),
        in_specs=in_specs,
        out_specs=[pl.BlockSpec((1, tl, d), lambda i, j: (i, j, 0)),
                   pl.BlockSpec((tl, d // 2), lambda i, j: (i * nl + j, 0)),
                   pl.BlockSpec((k_top, tl), tok_cols), pl.BlockSpec((k_top, tl), tok_cols),
                   pl.BlockSpec((k_top, tl), tok_cols),
                   pl.BlockSpec((ne, 1), lambda i, j: (0, 0)),
                   pl.BlockSpec((1, max(POOL_WINDOWS) - 1, pw), lambda i, j: (i, 0, 0)),
                   pl.BlockSpec((1, conv_k - 1, cw), lambda i, j: (i, 0, 0)),
                   pl.BlockSpec((ecs, d, 2 * de), w_map), pl.BlockSpec((ecs, de, d), w_map)],
        out_shape=[jax.ShapeDtypeStruct((b, seq, d), F32)]
                  + [jax.ShapeDtypeStruct(a.shape, a.dtype) for a in bufs]
                  + [jax.ShapeDtypeStruct((ne, 1), F32),
                     jax.ShapeDtypeStruct((b, max(POOL_WINDOWS) - 1, pw), F32),
                     jax.ShapeDtypeStruct((b, conv_k - 1, cw), F32),
                     jax.ShapeDtypeStruct((n_exp, d, 2 * de), BF16),
                     jax.ShapeDtypeStruct((n_exp, de, d), BF16)],
        scratch_shapes=[pltpu.VMEM((U_HALO + tl, pw), F32), pltpu.VMEM((U_HALO + tl, pw), F32),
                        pltpu.VMEM((U_HALO + tl, pw - pg), F32), pltpu.VMEM((U_HALO + tl, pw - 2 * pg), F32),
                        pltpu.VMEM((A_HALO + tl, cw), F32), pltpu.VMEM((SUBLANES - 1, A_HALO + tl, cw), F32),
                        pltpu.VMEM((tl, pw), BF16), pltpu.VMEM((tl, cw), BF16),
                        pltpu.VMEM((conv_k + 1, SUBLANES, cw), F32), pltpu.VMEM((ne, 1), F32)],
        compiler_params=pltpu.CompilerParams(dimension_semantics=("arbitrary", "arbitrary"),
                                             vmem_limit_bytes=VMEM_LIMIT),
        input_output_aliases=aliases,
        name="prompt_mixer",
    )(*ins)


def _sample_mixer(x, mod, sp, sc, wl, b_perm, t_prompt, t_pad, *, conv_k):
    rows, d = x.shape
    tail = t_pad - t_prompt
    assert t_prompt % tail == 0 and rows % N_PARTS == 0 and rows // N_PARTS <= tail
    pw = wl["pool"].shape[0]
    cw = wl["dw"].shape[1]
    ne = wl["router_t"].shape[0]
    ws = [wl[n] for n in _W_NAMES]
    kern = functools.partial(_sample_mixer_kernel, d=d, pw=pw, cw=cw, conv_k=conv_k)
    tri = _strict_upper(tail)
    ins = [x, mod, sp, sc] + ws + [b_perm, tri]
    row_tail = pl.BlockSpec((TOP_K, tail), lambda i: (0, t_prompt // tail))
    tail_specs = [pl.BlockSpec((tail, d // 2), lambda i: (t_prompt // tail, 0)), row_tail, row_tail, row_tail,
                  pl.BlockSpec((ne, 1), lambda i: (0, 0))] * N_PARTS
    tail_shapes = [jax.ShapeDtypeStruct((t_pad, d // 2), I32), jax.ShapeDtypeStruct((TOP_K, t_pad), I32),
                   jax.ShapeDtypeStruct((TOP_K, t_pad), F32), jax.ShapeDtypeStruct((TOP_K, t_pad), I32),
                   jax.ShapeDtypeStruct((ne, 1), F32)] * N_PARTS
    outs = pl.pallas_call(
        kern,
        grid=(1,),
        in_specs=[_full_spec(x), pl.BlockSpec((rows, mod.shape[1]), lambda i: (0, 0))]
                 + [_full_spec(a) for a in ins[2:]],
        out_specs=[pl.BlockSpec((rows, d), lambda i: (0, 0)), _full_spec(sp), _full_spec(sc)] + tail_specs,
        out_shape=[jax.ShapeDtypeStruct((rows, d), F32),
                   jax.ShapeDtypeStruct(sp.shape, F32),
                   jax.ShapeDtypeStruct(sc.shape, F32)] + tail_shapes,
        scratch_shapes=[pltpu.VMEM((rows, d // 2), I32), pltpu.VMEM((ne, tail), F32), pltpu.VMEM((ne, 1), F32)],
        compiler_params=pltpu.CompilerParams(dimension_semantics=("arbitrary",),
                                             vmem_limit_bytes=VMEM_LIMIT),
        name="sample_mixer",
    )(*ins)
    return outs[0], outs[1], outs[2], [tuple(outs[3 + 5 * p:8 + 5 * p]) for p in range(N_PARTS)]


def _sublane_max(x):
    return jnp.max(x, axis=0, keepdims=True)


def _route_tile(lg, bias, tri, carry, real, idx_ref, wt_ref, rank_ref):
    ne, tt = lg.shape
    ng = N_EXPERT_GROUPS
    per = ne // ng
    s = _sigmoid(lg)
    sel = s + bias
    s3 = [s[p * ng:(p + 1) * ng, :] for p in range(per)]
    sel3 = [sel[p * ng:(p + 1) * ng, :] for p in range(per)]
    m1 = sel3[0]
    m2 = jnp.full_like(m1, NEG_INF)
    for p in range(1, per):
        m2 = jnp.maximum(m2, jnp.minimum(m1, sel3[p]))
        m1 = jnp.maximum(m1, sel3[p])
    gs = m1 + m2
    gi = lax.broadcasted_iota(jnp.int32, (ng, tt), 0)
    beaten = jnp.zeros((ng, tt), jnp.int32)
    for g in range(ng):
        row = gs[g:g + 1, :]
        beats = (row > gs) | ((row == gs) & (gi > g))
        beaten = beaten + beats.astype(jnp.int32)
    keep = beaten < TOPK_GROUPS
    cur = [jnp.where(keep, sel3[p], NEG_INF) for p in range(per)]
    eid = [(gi * per + p).astype(F32) for p in range(per)]
    idxs, wts, hits = [], [], []
    for _ in range(TOP_K):
        m = cur[0]
        for p in range(1, per):
            m = jnp.maximum(m, cur[p])
        m = _sublane_max(m)
        cand = jnp.where(cur[0] == m, eid[0], float(ne))
        for p in range(1, per):
            cand = jnp.minimum(cand, jnp.where(cur[p] == m, eid[p], float(ne)))
        e_sel = jnp.min(cand, axis=0, keepdims=True)
        hit = [eid[p] == e_sel for p in range(per)]
        wk = jnp.where(hit[0], s3[0], 0.0)
        for p in range(1, per):
            wk = wk + jnp.where(hit[p], s3[p], 0.0)
        wts.append(jnp.sum(wk, axis=0, keepdims=True))
        cur = [jnp.where(hit[p], NEG_INF, cur[p]) for p in range(per)]
        idxs.append(e_sel)
        hits.append(hit)
    wsum = wts[0]
    for k in range(1, TOP_K):
        wsum = wsum + wts[k]
    chosen = [functools.reduce(lambda a, b: a | b, [hits[k][p] for k in range(TOP_K)]) for p in range(per)]
    onehot = jnp.concatenate([(c & real).astype(F32) for c in chosen], axis=0)
    before = _dot(onehot.astype(BF16), tri) + carry[...]
    for k in range(TOP_K):
        rk = jnp.where(hits[k][0], before[0:ng, :], 0.0)
        for p in range(1, per):
            rk = rk + jnp.where(hits[k][p], before[p * ng:(p + 1) * ng, :], 0.0)
        rank_ref[k:k + 1, :] = jnp.sum(rk, axis=0, keepdims=True).astype(jnp.int32)
        idx_ref[k:k + 1, :] = idxs[k].astype(jnp.int32)
        wt_ref[k:k + 1, :] = wts[k] / wsum * ROUTED_SCALE
    carry[...] = carry[...] + jnp.sum(onehot, axis=1, keepdims=True)


def _strict_upper(n):
    return (jnp.arange(n)[:, None] < jnp.arange(n)[None, :]).astype(BF16)


def _dest_kernel(cnt_ref, idx_ref, rank_ref, dest_ref, first_ref, counts_ref, pstart, *, t_valid, block_rows,
                 spare_row):
    ne = cnt_ref.shape[0]
    per = ne // N_EXPERT_GROUPS

    def table(e, blocks):
        c = cnt_ref[(e % per) * N_EXPERT_GROUPS + e // per]
        first_ref[e] = blocks
        counts_ref[e] = c
        pstart[e] = blocks * block_rows
        return blocks + (c + block_rows - 1) // block_rows

    first_ref[ne] = lax.fori_loop(0, ne, table, 0)

    idx = idx_ref[...]
    k_top, tile = idx.shape

    def add_start(e, acc):
        return acc + jnp.where(idx == e, pstart[e], 0)

    dest = lax.fori_loop(0, ne, add_start, rank_ref[...])
    tok = pl.program_id(0) * tile + lax.broadcasted_iota(I32, idx.shape, 1)
    spare = spare_row + (tok - t_valid) * k_top + lax.broadcasted_iota(I32, idx.shape, 0)
    dest_ref[...] = jnp.where(tok < t_valid, dest, spare)


def _dest_rows(counts_rows, idx, rank, t_valid, block_rows, spare_row):
    k_top, t = idx.shape
    ne = counts_rows.shape[0]
    tile = DEST_TILE
    spec = pl.BlockSpec((k_top, tile), lambda i, cn: (0, i))
    smem = pl.BlockSpec(memory_space=pltpu.SMEM)
    return pl.pallas_call(
        functools.partial(_dest_kernel, t_valid=t_valid, block_rows=block_rows, spare_row=spare_row),
        grid_spec=pltpu.PrefetchScalarGridSpec(num_scalar_prefetch=1, grid=(t // tile,),
                                               in_specs=[spec, spec], out_specs=[spec, smem, smem],
                                               scratch_shapes=[pltpu.SMEM((ne,), I32)]),
        out_shape=[jax.ShapeDtypeStruct((k_top, t), I32), jax.ShapeDtypeStruct((ne + 1,), I32),
                   jax.ShapeDtypeStruct((ne,), I32)],
        compiler_params=pltpu.CompilerParams(dimension_semantics=("arbitrary",),
                                             vmem_limit_bytes=VMEM_LIMIT),
        name="dest_rows",
    )(counts_rows, idx, rank)


def _expert_kernel(first_ref, cnt_ref, xs_hbm, wgu_ref, wd_ref, y_hbm, xbuf, ybuf, in_sem, out_sem):
    step = pl.program_id(0)
    ne = first_ref.shape[0] - 1
    nbuf, rows = xbuf.shape[:2]
    per_step, de = wd_ref.shape[:2]
    n_total = first_ref[ne]

    def in_copy(g):
        return pltpu.make_async_copy(xs_hbm.at[pl.ds(g * rows, rows)], xbuf.at[g % nbuf], in_sem.at[g % nbuf])

    def out_copy(g):
        return pltpu.make_async_copy(ybuf.at[g % nbuf], y_hbm.at[pl.ds(g * rows, rows)], out_sem.at[g % nbuf])

    @pl.when(step == 0)
    def _():
        for g in range(nbuf - 1):
            @pl.when(g < n_total)
            def _():
                in_copy(g).start()

    def block(g, carry, *, j, e, b0):
        slot = g % nbuf

        @pl.when(g + nbuf - 1 < n_total)
        def _():
            in_copy(g + nbuf - 1).start()

        in_copy(g).wait()

        @pl.when(g >= nbuf)
        def _():
            out_copy(g - nbuf).wait()

        valid = cnt_ref[e] - (g - b0) * rows
        rid = lax.broadcasted_iota(I32, (rows, 1), 0)
        xs = jnp.where(rid < valid, _unpack_bf16_pairs(xbuf[slot]), 0.0).astype(BF16)
        gu = _dot(xs, wgu_ref[j])
        hb = _silu(gu[:, :de]) * gu[:, de:]
        ybuf[slot] = _pack_bf16_pairs(_dot(hb.astype(BF16), wd_ref[j]))
        out_copy(g).start()
        return carry

    for j in range(per_step):
        e = step * per_step + j
        b0 = first_ref[e]
        lax.fori_loop(b0, first_ref[e + 1], functools.partial(block, j=j, e=e, b0=b0), 0)

    @pl.when(step == pl.num_programs(0) - 1)
    def _():
        for back in range(nbuf, 0, -1):
            @pl.when(n_total >= back)
            def _():
                out_copy(n_total - back).wait()


def _experts(xs, w_gu, w_down, first_block, counts):
    p, dw = xs.shape
    ne, de, d = w_down.shape
    return pl.pallas_call(
        _expert_kernel,
        grid_spec=pltpu.PrefetchScalarGridSpec(
            num_scalar_prefetch=2,
            grid=(ne // EXPERTS_PER_STEP,),
            in_specs=[pl.BlockSpec(memory_space=pl.ANY),
                      pl.BlockSpec((EXPERTS_PER_STEP, d, 2 * de), lambda e, fb, cn: (e, 0, 0)),
                      pl.BlockSpec((EXPERTS_PER_STEP, de, d), lambda e, fb, cn: (e, 0, 0))],
            out_specs=pl.BlockSpec(memory_space=pl.ANY),
            scratch_shapes=[pltpu.VMEM((EXPERT_BUFFERS, EXPERT_ROWS, dw), I32),
                            pltpu.VMEM((EXPERT_BUFFERS, EXPERT_ROWS, dw), I32),
                            pltpu.SemaphoreType.DMA((EXPERT_BUFFERS,)), pltpu.SemaphoreType.DMA((EXPERT_BUFFERS,))]),
        out_shape=jax.ShapeDtypeStruct((p, dw), I32),
        compiler_params=pltpu.CompilerParams(dimension_semantics=("arbitrary",),
                                             vmem_limit_bytes=VMEM_LIMIT),
        name="experts",
    )(first_block, counts, xs, w_gu, w_down)


def _sc_first_chunk(n_chunks):
    per_worker = n_chunks // SC_WORKERS
    assert per_worker * SC_WORKERS == n_chunks
    return per_worker, (lax.axis_index("s") * SC_CORES + lax.axis_index("c")) * per_worker


def _sc_start(copies):
    for cp in copies:
        cp.start()


def _sc_wait(copies):
    for cp in copies:
        cp.wait()


def _sc_dispatch(h2, dest, n_rows):
    w = h2.shape[1]
    n_chunks, k_top, ch = dest.shape
    mesh = plsc.VectorSubcoreMesh(core_axis_name="c", subcore_axis_name="s")

    @functools.partial(
        pl.kernel, mesh=mesh, out_type=jax.ShapeDtypeStruct((n_rows, w), I32),
        scratch_types=[pltpu.VMEM((2, k_top, ch), I32), pltpu.VMEM((2, ch, w), I32),
                       pltpu.SemaphoreType.DMA((2,)), pltpu.SemaphoreType.DMA((2,))])
    def dispatch(h2_hbm, dest_hbm, xs_hbm, idx_v, rows_v, load_sem, store_sem):
        per_worker, first = _sc_first_chunk(n_chunks)

        def loads(i):
            slot = i % 2
            return (pltpu.make_async_copy(h2_hbm.at[pl.ds((first + i) * ch, ch)], rows_v.at[slot], load_sem.at[slot]),
                    pltpu.make_async_copy(dest_hbm.at[first + i], idx_v.at[slot], load_sem.at[slot]))

        def scatters(i):
            slot = i % 2
            return [pltpu.make_async_copy(rows_v.at[slot], xs_hbm.at[idx_v.at[slot, k]], store_sem.at[slot])
                    for k in range(k_top)]

        for i in range(min(2, per_worker)):
            _sc_start(loads(i))
        for i in range(per_worker):
            _sc_wait(loads(i))
            _sc_start(scatters(i))
            if 1 <= i < per_worker - 1:
                _sc_wait(scatters(i - 1))
                _sc_start(loads(i + 1))
        for i in range(max(per_worker - 2, 0), per_worker):
            _sc_wait(scatters(i))

    return dispatch(h2, dest)


def _sc_combine_gather(y, dest, t):
    w = y.shape[1]
    n_chunks, k_top, ch = dest.shape
    n_local = n_chunks // SC_WORKERS
    mesh = plsc.VectorSubcoreMesh(core_axis_name="c", subcore_axis_name="s")

    @functools.partial(
        pl.kernel, mesh=mesh, out_type=jax.ShapeDtypeStruct((k_top, t, w), I32),
        scratch_types=[pltpu.VMEM((n_local, k_top, ch), I32), pltpu.VMEM((k_top, ch, w), I32),
                       pltpu.SemaphoreType.DMA((k_top,)), pltpu.SemaphoreType.DMA((k_top,))])
    def gather(y_hbm, dest_hbm, out_hbm, idx_v, rows_v, load_sem, store_sem):
        per_worker, first = _sc_first_chunk(n_chunks)
        pltpu.sync_copy(dest_hbm.at[pl.ds(first, per_worker)], idx_v)

        def fetch(i, k):
            return pltpu.make_async_copy(y_hbm.at[idx_v.at[i, k]], rows_v.at[k], load_sem.at[k])

        def put(i, k):
            return pltpu.make_async_copy(rows_v.at[k], out_hbm.at[k, pl.ds((first + i) * ch, ch)], store_sem.at[k])

        @pl.loop(0, per_worker)
        def _(i):
            for k in range(k_top):
                @pl.when(i > 0)
                def _():
                    put(i - 1, k).wait()

                fetch(i, k).start()
            for k in range(k_top):
                fetch(i, k).wait()
                put(i, k).start()

        for k in range(k_top):
            put(per_worker - 1, k).wait()

    return gather(y, dest)


def _final_kernel(xmid_ref, yg_ref, wt_ref, g2_ref, shf_ref, scf_ref, gf_ref, *rest, vec_row0=None):
    o_ref = rest[-1]
    if vec_row0 is not None:
        row = pl.ds((vec_row0 + pl.program_id(0)) % SUBLANES, 1)
        g2_ref, shf_ref, scf_ref = g2_ref.at[row, :], shf_ref.at[row, :], scf_ref.at[row, :]
    wt = wt_ref[...].T[:xmid_ref.shape[0]]
    routed = wt[:, 0:1] * _unpack_bf16_pairs(yg_ref[0])
    for k in range(1, yg_ref.shape[0]):
        routed = routed + wt[:, k:k + 1] * _unpack_bf16_pairs(yg_ref[k])
    x2 = xmid_ref[...] + g2_ref[...] * routed
    o_ref[...] = _rms_norm(x2, gf_ref[...]) * (1.0 + scf_ref[...]) + shf_ref[...]


def _final_prompt(xmid, yg, wts_t, mod, modf, mod_row0, g_final, b0, n_batch, out_prev):
    b, seq, d = xmid.shape
    k_top, _, w = yg.shape
    tl = FINAL_TILE
    nl = seq // tl

    def vec(col):
        return pl.BlockSpec((SUBLANES, d), lambda i, j: ((i + b0 + mod_row0) // SUBLANES, col))

    in_specs = [pl.BlockSpec((None, tl, d), lambda i, j: (i, j, 0)),
                pl.BlockSpec((k_top, tl, w), lambda i, j: (0, i * nl + j, 0)),
                pl.BlockSpec((k_top, tl), lambda i, j: (0, i * nl + j)),
                vec(5), vec(0), vec(1), pl.BlockSpec((1, d), lambda i, j: (0, 0))]
    args = [xmid, yg, wts_t, mod, modf, modf, g_final]
    aliases = {}
    if out_prev is not None:
        in_specs.append(pl.BlockSpec(memory_space=pl.ANY))
        args.append(out_prev)
        aliases = {7: 0}
    return pl.pallas_call(
        functools.partial(_final_kernel, vec_row0=b0 + mod_row0),
        grid=(b, nl),
        in_specs=in_specs,
        out_specs=pl.BlockSpec((None, tl, d), lambda i, j: (i + b0, j, 0)),
        out_shape=jax.ShapeDtypeStruct((n_batch, seq, d), F32),
        compiler_params=pltpu.CompilerParams(dimension_semantics=("arbitrary", "arbitrary"),
                                             vmem_limit_bytes=VMEM_LIMIT),
        input_output_aliases=aliases,
        name="final_prompt",
    )(*args)


def _final_sample(xmid_all, yg, wts_t, mod, modf, g_final, part, rows, first_row, after):
    d = xmid_all.shape[1]
    k_top, _, w = yg.shape

    def vec(col):
        return pl.BlockSpec((rows, d), lambda i: (part, col))

    return pl.pallas_call(
        _final_kernel,
        grid=(1,),
        in_specs=[vec(0),
                  pl.BlockSpec((k_top, rows, w), lambda i: (0, first_row // rows, 0)),
                  pl.BlockSpec((k_top, LANES), lambda i: (0, first_row // LANES)),
                  vec(5), vec(0), vec(1), pl.BlockSpec((1, d), lambda i: (0, 0)), pl.BlockSpec(memory_space=pl.ANY)],
        out_specs=pl.BlockSpec((rows, d), lambda i: (0, 0)),
        out_shape=jax.ShapeDtypeStruct((rows, d), F32),
        compiler_params=pltpu.CompilerParams(dimension_semantics=("arbitrary",),
                                             vmem_limit_bytes=VMEM_LIMIT),
        name="final_sample",
    )(xmid_all, yg, wts_t, mod, modf, modf, g_final, after)


def _block_diag_pairs(w_pool):
    g, c, _ = w_pool.shape
    eye = jnp.eye(g, dtype=w_pool.dtype)
    return (eye[:, None, :, None] * w_pool[:, :, None, :]).reshape(g * c, g * c)


def kernel(x_prompt, x_sample, state_pool, state_conv, c_prompt, c_sample, w_ada, b_ada, g_mix, w_in, w_pool, pool_scale, w_dw, b_dw, ln_g, ln_b, w_out, g_ffn, w_router, b_router, w_gate, w_up, w_down, ws_gate, ws_up, ws_down, w_ada_final, b_ada_final, g_final):
    bp, seq, d = x_prompt.shape
    bs = x_sample.shape[0]
    depth = w_ada.shape[0]
    assert depth == 1 and x_sample.shape[1] == 1
    conv_k = w_dw.shape[1]
    ne = w_router.shape[-1]
    per = ne // N_EXPERT_GROUPS

    row_expert = (jnp.arange(ne) % N_EXPERT_GROUPS) * per + jnp.arange(ne) // N_EXPERT_GROUPS
    wl = {
        "g_mix": g_mix[0][None, :],
        "in": w_in[0].astype(BF16),
        "pool": _block_diag_pairs(w_pool[0]).astype(BF16),
        "pool_scale": pool_scale[0][None, :],
        "dw": w_dw[0],
        "b_dw": b_dw[0][None, :],
        "ln_g": ln_g[0][None, :],
        "ln_b": ln_b[0][None, :],
        "out": w_out[0].astype(BF16),
        "g_ffn": g_ffn[0][None, :],
        "router_t": w_router[0].T[row_expert].astype(BF16),
        "s_gu": jnp.concatenate([ws_gate[0], ws_up[0]], axis=1).astype(BF16),
        "s_down": ws_down[0].astype(BF16),
    }
    b_perm = b_router[0][row_expert][:, None]

    assert bp % N_PARTS == 0 and bs % N_PARTS == 0
    bpp, bsp = bp // N_PARTS, bs // N_PARTS
    tpp = bpp * seq
    t_part = tpp + bsp
    grain = SC_WORKERS * SC_DISPATCH_CHUNK * SC_COMBINE_CHUNK // math.gcd(SC_DISPATCH_CHUNK, SC_COMBINE_CHUNK)
    t_pad = (t_part + grain - 1) // grain * grain
    assert t_pad % DEST_TILE == 0 and tpp % bsp == 0

    c_all = jnp.concatenate([c_sample, c_prompt], axis=0)
    mod = _ada(c_all, w_ada[0], b_ada[0])
    modf = _ada(c_all, w_ada_final, b_ada_final)

    xmid_s, new_pool_t, new_conv_t, tails = _sample_mixer(
        x_sample.reshape(bs, d), mod, jnp.transpose(state_pool[0], (1, 0, 2)),
        jnp.transpose(state_conv[0], (1, 0, 2)), wl, b_perm, tpp, t_pad, conv_k=conv_k)
    new_pool_s = jnp.transpose(new_pool_t, (1, 0, 2))
    new_conv_s = jnp.transpose(new_conv_t, (1, 0, 2))
    w_experts = tuple(a.reshape(a.shape[1:]) for a in (w_gate, w_up, w_down))

    n_blocks = (t_part * TOP_K + EXPERT_ROWS - 1) // EXPERT_ROWS + ne
    n_spare = -(-(t_pad - t_part) * TOP_K // EXPERT_ROWS)
    n_rows = (n_blocks + n_spare) * EXPERT_ROWS

    y_prompt, y_samples, npools, nconvs, w_bf = None, [], [], [], None
    mixed = []
    for p in range(N_PARTS):
        xmid_p, h2, idx, wts, rank, counts_perm, npool_p, nconv_p, *w_bf = _prompt_mixer(
            x_prompt, mod, bs, wl, tails[p][:4], tails[p][4], b_perm, p * bpp, bpp, w_experts, w_bf,
            conv_k=conv_k)
        mixed.append((xmid_p, h2, idx, wts, rank, counts_perm))
        npools.append(npool_p)
        nconvs.append(nconv_p)
    for p in range(N_PARTS):
        xmid_p, h2, idx, wts, rank, counts_perm = mixed[p]
        dest, first_block, counts = _dest_rows(counts_perm[:, 0].astype(I32), idx, rank, t_part, EXPERT_ROWS,
                                               n_blocks * EXPERT_ROWS)

        def chunked(ch):
            return dest.reshape(TOP_K, t_pad // ch, ch).transpose(1, 0, 2)

        xs = _sc_dispatch(h2, chunked(SC_DISPATCH_CHUNK), n_rows)
        y = _experts(xs, w_bf[0], w_bf[1], first_block, counts)
        yg = _sc_combine_gather(y, chunked(SC_COMBINE_CHUNK), t_pad)

        y_samples.append(_final_sample(xmid_s, yg, wts, mod, modf, g_final[None, :], p, bsp, tpp,
                                       g_final if y_prompt is None else y_prompt))
        y_prompt = _final_prompt(xmid_p, yg, wts, mod, modf, bs, g_final[None, :], p * bpp, bp, y_prompt)
    y_sample = jnp.concatenate(y_samples, axis=0)
    npool_p = jnp.concatenate(npools, axis=0)
    nconv_p = jnp.concatenate(nconvs, axis=0)

    return (y_prompt, y_sample[:, None, :], npool_p[None], nconv_p[None], new_pool_s[None], new_conv_s[None])
```

```python
import functools
import math

import jax
import jax.numpy as jnp
from jax import lax
from jax.experimental import pallas as pl
from jax.experimental.pallas import tpu as pltpu
from jax.experimental.pallas import tpu_sc as plsc

POOL_WINDOWS = (2, 4, 8, 16)
N_EXPERT_GROUPS = 8
TOPK_GROUPS = 4
TOP_K = 8
ROUTED_SCALE = 2.5
EPS = 1e-6

LANES = 128
SUBLANES = 8
VMEM_LIMIT = 52 * 1024 * 1024

ADA_TILE = 1024
SEQ_TILE = 512
ROW_CHUNK = 16
ROUTE_TILE = 1408
EXPERT_ROWS = 512
N_PARTS = 2
EXPERT_BUFFERS = 6
EXPERTS_PER_STEP = 2
FINAL_TILE = 512
DEST_TILE = 2816

SC_CORES = 2
SC_SUBCORES = 16
SC_WORKERS = SC_CORES * SC_SUBCORES
SC_DISPATCH_CHUNK = 88
SC_COMBINE_CHUNK = 24

F32 = jnp.float32
BF16 = jnp.bfloat16
I32 = jnp.int32
U32 = jnp.uint32
NEG_INF = float("-inf")
HI16 = 0xFFFF0000


def _sigmoid(x):
    return 1.0 / (1.0 + jnp.exp(-x))


def _silu(x):
    return x * _sigmoid(x)


def _rms_norm(x, g):
    return x * lax.rsqrt(jnp.mean(x * x, axis=-1, keepdims=True) + EPS) * g


def _dot(a, b):
    return jnp.dot(a, b, preferred_element_type=F32)


def _pack_bf16_pairs(x):
    w = x.shape[1] // 2
    bits = lax.bitcast_convert_type(x.astype(BF16).astype(F32), U32)
    return lax.bitcast_convert_type((bits[:, :w] >> 16) | (bits[:, w:] & U32(HI16)), I32)


def _unpack_bf16_pairs(p):
    bits = lax.bitcast_convert_type(p, U32)
    lo = lax.bitcast_convert_type(bits << 16, F32)
    hi = lax.bitcast_convert_type(bits & U32(HI16), F32)
    return jnp.concatenate([lo, hi], axis=1)


def _ada_kernel(c_ref, w_ref, b_ref, o_ref):
    s = _silu(c_ref[...])
    o_ref[...] = _dot(s.astype(BF16), w_ref[...].astype(BF16)) + b_ref[...]


def _ada(c, w, b):
    rows, d = c.shape
    n = w.shape[1]
    tn = ADA_TILE
    return pl.pallas_call(
        _ada_kernel,
        grid=(n // tn,),
        in_specs=[pl.BlockSpec((rows, d), lambda j: (0, 0)),
                  pl.BlockSpec((d, tn), lambda j: (0, j)),
                  pl.BlockSpec((1, tn), lambda j: (0, j))],
        out_specs=pl.BlockSpec((rows, tn), lambda j: (0, j)),
        out_shape=jax.ShapeDtypeStruct((rows, n), F32),
        compiler_params=pltpu.CompilerParams(dimension_semantics=("arbitrary",),
                                             vmem_limit_bytes=VMEM_LIMIT),
        name="ada",
    )(c, w, b.reshape(1, n))


def _mixer_tail(x, pool_d, conv_act, mod, w, xmid_ref, h2_ref, lgt_ref):
    sh2, sc2, g1, g2 = mod
    pw = pool_d.shape[1]
    pool_out = _dot(pool_d, w["pool"][...]) * w["pool_scale"][...]
    mix = _dot(pool_out.astype(BF16), w["out"][:pw, :]) + _dot(conv_act, w["out"][pw:, :])
    x1 = x + g1 * mix
    h2f = _rms_norm(x1, w["g_ffn"][...]) * (1.0 + sc2) + sh2
    h2_ref[...] = _pack_bf16_pairs(h2f)
    h2 = h2f.astype(BF16)
    lgt_ref[...] = lax.dot_general(w["router_t"][...], h2, (((1,), (1,)), ((), ())),
                                   preferred_element_type=F32)
    gu = _dot(h2, w["s_gu"][...])
    de = gu.shape[1] // 2
    hs = _silu(gu[:, :de]) * gu[:, de:]
    shared = _dot(hs.astype(BF16), w["s_down"][...])
    xmid_ref[...] = x1 + g2 * shared


_W_NAMES = ("g_mix", "in", "pool", "pool_scale", "dw", "b_dw", "ln_g", "ln_b", "out", "g_ffn",
            "router_t", "s_gu", "s_down")


def _layer_norm_silu(yc, g, b):
    mu = jnp.mean(yc, axis=-1, keepdims=True)
    yz = yc - mu
    var = jnp.mean(yz * yz, axis=-1, keepdims=True)
    return _silu(yz * lax.rsqrt(var + EPS) * g + b)


U_HALO = 32
A_HALO = 32


def _prompt_mixer_kernel(x_ref, mod_ref, *refs, tl, d, pw, cw, conv_k, mod_row0):
    nw = len(_W_NAMES)
    w = dict(zip(_W_NAMES, refs[:nw]))
    wg_ref, wu_ref, wd_ref = refs[nw + 2:nw + 5]
    xmid_ref, h2_ref, lgt_ref, npool_ref, nconv_ref, wgu_bf_ref, wd_bf_ref = refs[-16:-9]
    ubuf, s2buf, s4buf, s8buf, abuf, ashift, dbuf, cbuf, rowb = refs[-9:]
    de = wg_ref.shape[2]
    wgu_bf_ref[:, :, :de] = wg_ref[...].astype(BF16)
    wgu_bf_ref[:, :, de:] = wu_ref[...].astype(BF16)
    wd_bf_ref[...] = wd_ref[...].astype(BF16)
    l = pl.program_id(1)
    pool_buf = max(POOL_WINDOWS) - 1
    uh, ah = U_HALO, A_HALO
    pg = pw // len(POOL_WINDOWS)
    nu, na = uh + tl, ah + tl

    @pl.when(l == 0)
    def _():
        ubuf[0:uh, :] = jnp.zeros((uh, pw), F32)
        abuf[0:ah, :] = jnp.zeros((ah, cw), F32)

    @pl.when(l > 0)
    def _():
        ubuf[0:uh, :] = ubuf[tl:tl + uh, :]
        abuf[0:ah, :] = abuf[tl:tl + ah, :]

    x = x_ref[0]
    row = pl.ds((mod_row0 + pl.program_id(0)) % SUBLANES, 1)
    sh1, sc1, g1, sh2, sc2, g2 = [mod_ref[row, i * d:(i + 1) * d] for i in range(6)]

    h = _rms_norm(x, w["g_mix"][...]) * (1.0 + sc1) + sh1
    z = _dot(h.astype(BF16), w["in"][...])
    u = z[:, :pw]
    ubuf[uh:nu, :] = u
    abuf[ah:na, :] = z[:, pw:pw + cw] * _sigmoid(z[:, pw + cw:])

    s2buf[8:nu, :] = ubuf[8:nu, :] + ubuf[7:nu - 1, :]
    s4buf[16:nu, :] = s2buf[16:nu, pg:] + s2buf[14:nu - 2, pg:]
    s8buf[24:nu, :] = s4buf[24:nu, pg:] + s4buf[20:nu - 4, pg:]
    t = l * tl + lax.broadcasted_iota(I32, (tl, 1), 0)
    sums = (s2buf[uh:nu, 0:pg], s4buf[uh:nu, 0:pg], s8buf[uh:nu, 0:pg],
            s8buf[uh:nu, pg:2 * pg] + s8buf[uh - 8:nu - 8, pg:2 * pg])
    for g, win in enumerate(POOL_WINDOWS):
        cols = slice(g * pg, (g + 1) * pg)
        inv = 1.0 / jnp.minimum(win, t + 1).astype(F32)
        dbuf[:, cols] = (sums[g] * inv - u[:, cols]).astype(BF16)

    for r in range(1, SUBLANES):
        ashift[r - 1, 8:na, :] = abuf[8 - r:na - r, :]

    for k in range(conv_k):
        rowb[k] = jnp.broadcast_to(w["dw"][k:k + 1, :], (SUBLANES, cw))
    rowb[conv_k] = jnp.broadcast_to(w["b_dw"][...], (SUBLANES, cw))

    for c in range(tl // ROW_CHUNK):
        groups = []
        for o in range(ah + c * ROW_CHUNK, ah + (c + 1) * ROW_CHUNK, SUBLANES):
            yc = abuf[o:o + SUBLANES, :] * rowb[conv_k - 1]
            for back in range(1, conv_k):
                q, r = divmod(back, SUBLANES)
                src = abuf if r == 0 else ashift.at[r - 1]
                yc = yc + src[o - q * SUBLANES:o - (q - 1) * SUBLANES, :] * rowb[conv_k - 1 - back]
            groups.append(yc + rowb[conv_k])
        r0 = c * ROW_CHUNK
        cbuf[r0:r0 + ROW_CHUNK, :] = _layer_norm_silu(
            jnp.concatenate(groups, axis=0), w["ln_g"][...], w["ln_b"][...]).astype(BF16)

    @pl.when(l == pl.num_programs(1) - 1)
    def _():
        npool_ref[0] = ubuf[nu - pool_buf:nu, :]
        nconv_ref[0] = abuf[na - (conv_k - 1):na, :]

    _mixer_tail(x, dbuf[...], cbuf[...], (sh2, sc2, g1, g2), w, xmid_ref.at[0], h2_ref, lgt_ref)


def _sample_mixer_kernel(x_ref, mod_ref, sp_ref, sc_ref, *refs, d, pw, cw, conv_k):
    nw = len(_W_NAMES)
    w = dict(zip(_W_NAMES, refs[:nw]))
    xmid_ref, npool_ref, nconv_ref = refs[nw:nw + 3]
    tails = refs[nw + 3:-2]
    h2_tmp, lgt_tmp = refs[-2:]
    pool_buf = max(POOL_WINDOWS) - 1
    pg = pw // len(POOL_WINDOWS)
    x = x_ref[...]
    sh1, sc1, g1, sh2, sc2, g2 = [mod_ref[:, i * d:(i + 1) * d] for i in range(6)]
    h = _rms_norm(x, w["g_mix"][...]) * (1.0 + sc1) + sh1
    z = _dot(h.astype(BF16), w["in"][...])
    u = z[:, :pw]
    a = z[:, pw:pw + cw] * _sigmoid(z[:, pw + cw:])
    npool_ref[0:pool_buf - 1] = sp_ref[1:pool_buf]
    npool_ref[pool_buf - 1] = u
    nconv_ref[0:conv_k - 2] = sc_ref[1:conv_k - 1]
    nconv_ref[conv_k - 2] = a
    ds = []
    for g, win in enumerate(POOL_WINDOWS):
        cols = slice(g * pg, (g + 1) * pg)
        acc = u[:, cols]
        for j in range(1, win):
            acc = acc + sp_ref[pool_buf - j, :, cols]
        cnt = float(min(win, 1 + pool_buf))
        ds.append(acc / cnt - u[:, cols])
    pool_d = jnp.concatenate(ds, axis=-1).astype(BF16)
    yc = a * w["dw"][conv_k - 1:conv_k, :]
    for k in range(conv_k - 1):
        yc = yc + sc_ref[k] * w["dw"][k:k + 1, :]
    yc = yc + w["b_dw"][...]
    conv_act = _layer_norm_silu(yc, w["ln_g"][...], w["ln_b"][...]).astype(BF16)
    _mixer_tail(x, pool_d, conv_act, (sh2, sc2, g1, g2), w, xmid_ref, h2_tmp, lgt_tmp)
    n_parts = len(tails) // 2
    share = x.shape[0] // n_parts
    for p in range(n_parts):
        h2_ref, lgt_ref = tails[2 * p], tails[2 * p + 1]
        h2_ref[...] = jnp.zeros_like(h2_ref)
        lgt_ref[...] = jnp.zeros_like(lgt_ref)
        h2_ref[0:share, :] = h2_tmp[p * share:(p + 1) * share, :]
        lgt_ref[:, 0:share] = lgt_tmp[:, p * share:(p + 1) * share]


def _full_spec(a):
    nd = a.ndim
    return pl.BlockSpec(a.shape, lambda *_: (0,) * nd)


def _prompt_mixer(x, mod, mod_row0, wl, h2_all, lgt_all, b0, b, w_experts, w_bf_prev, *, conv_k):
    n_batch, seq, d = x.shape
    w_gate, w_up, w_down = w_experts
    n_exp, _, de = w_gate.shape
    pw = wl["pool"].shape[0]
    cw = wl["dw"].shape[1]
    ne = wl["router_t"].shape[0]
    tl = SEQ_TILE
    nl = seq // tl
    pg = pw // len(POOL_WINDOWS)
    assert POOL_WINDOWS == (2, 4, 8, 16) and conv_k - 1 <= A_HALO
    ws = [wl[n] for n in _W_NAMES]
    kern = functools.partial(_prompt_mixer_kernel, tl=tl, d=d, pw=pw, cw=cw, conv_k=conv_k, mod_row0=mod_row0 + b0)
    ecs = n_exp // (n_batch * nl)
    assert ecs * n_batch * nl == n_exp

    def w_map(i, j):
        return ((i + b0) * nl + j, 0, 0)

    ins = [x, mod, *ws, h2_all, lgt_all, w_gate, w_up, w_down]
    in_specs = ([pl.BlockSpec((1, tl, d), lambda i, j: (i + b0, j, 0)),
                 pl.BlockSpec((SUBLANES, mod.shape[-1]), lambda i, j: ((i + b0 + mod_row0) // SUBLANES, 0))]
                + [_full_spec(a) for a in ws]
                + [pl.BlockSpec(memory_space=pl.ANY), pl.BlockSpec(memory_space=pl.ANY),
                   pl.BlockSpec((ecs, d, de), w_map), pl.BlockSpec((ecs, d, de), w_map),
                   pl.BlockSpec((ecs, de, d), w_map)])
    aliases = {2 + len(ws): 1, 3 + len(ws): 2}
    if w_bf_prev is not None:
        aliases.update({len(ins): 5, len(ins) + 1: 6})
        ins += list(w_bf_prev)
        in_specs += [pl.BlockSpec(memory_space=pl.ANY), pl.BlockSpec(memory_space=pl.ANY)]
    return pl.pallas_call(
        kern,
        grid=(b, nl),
        in_specs=in_specs,
        out_specs=[pl.BlockSpec((1, tl, d), lambda i, j: (i, j, 0)),
                   pl.BlockSpec((tl, d // 2), lambda i, j: (i * nl + j, 0)),
                   pl.BlockSpec((ne, tl), lambda i, j: (0, i * nl + j)),
                   pl.BlockSpec((1, max(POOL_WINDOWS) - 1, pw), lambda i, j: (i, 0, 0)),
                   pl.BlockSpec((1, conv_k - 1, cw), lambda i, j: (i, 0, 0)),
                   pl.BlockSpec((ecs, d, 2 * de), w_map), pl.BlockSpec((ecs, de, d), w_map)],
        out_shape=[jax.ShapeDtypeStruct((b, seq, d), F32),
                   jax.ShapeDtypeStruct(h2_all.shape, I32),
                   jax.ShapeDtypeStruct(lgt_all.shape, F32),
                   jax.ShapeDtypeStruct((b, max(POOL_WINDOWS) - 1, pw), F32),
                   jax.ShapeDtypeStruct((b, conv_k - 1, cw), F32),
                   jax.ShapeDtypeStruct((n_exp, d, 2 * de), BF16),
                   jax.ShapeDtypeStruct((n_exp, de, d), BF16)],
        scratch_shapes=[pltpu.VMEM((U_HALO + tl, pw), F32), pltpu.VMEM((U_HALO + tl, pw), F32),
                        pltpu.VMEM((U_HALO + tl, pw - pg), F32), pltpu.VMEM((U_HALO + tl, pw - 2 * pg), F32),
                        pltpu.VMEM((A_HALO + tl, cw), F32), pltpu.VMEM((SUBLANES - 1, A_HALO + tl, cw), F32),
                        pltpu.VMEM((tl, pw), BF16), pltpu.VMEM((tl, cw), BF16),
                        pltpu.VMEM((conv_k + 1, SUBLANES, cw), F32)],
        compiler_params=pltpu.CompilerParams(dimension_semantics=("arbitrary", "arbitrary"),
                                             vmem_limit_bytes=VMEM_LIMIT),
        input_output_aliases=aliases,
        name="prompt_mixer",
    )(*ins)


def _sample_mixer(x, mod, sp, sc, wl, t_prompt, t_pad, *, conv_k):
    rows, d = x.shape
    tail = t_pad - t_prompt
    assert t_prompt % tail == 0 and rows % N_PARTS == 0 and rows // N_PARTS <= tail
    pw = wl["pool"].shape[0]
    cw = wl["dw"].shape[1]
    ne = wl["router_t"].shape[0]
    ws = [wl[n] for n in _W_NAMES]
    kern = functools.partial(_sample_mixer_kernel, d=d, pw=pw, cw=cw, conv_k=conv_k)
    ins = [x, mod, sp, sc] + ws
    tail_specs = [pl.BlockSpec((tail, d // 2), lambda i: (t_prompt // tail, 0)),
                  pl.BlockSpec((ne, tail), lambda i: (0, t_prompt // tail))] * N_PARTS
    tail_shapes = [jax.ShapeDtypeStruct((t_pad, d // 2), I32), jax.ShapeDtypeStruct((ne, t_pad), F32)] * N_PARTS
    outs = pl.pallas_call(
        kern,
        grid=(1,),
        in_specs=[_full_spec(x), pl.BlockSpec((rows, mod.shape[1]), lambda i: (0, 0))]
                 + [_full_spec(a) for a in ins[2:]],
        out_specs=[pl.BlockSpec((rows, d), lambda i: (0, 0)), _full_spec(sp), _full_spec(sc)] + tail_specs,
        out_shape=[jax.ShapeDtypeStruct((rows, d), F32),
                   jax.ShapeDtypeStruct(sp.shape, F32),
                   jax.ShapeDtypeStruct(sc.shape, F32)] + tail_shapes,
        scratch_shapes=[pltpu.VMEM((rows, d // 2), I32), pltpu.VMEM((ne, rows), F32)],
        compiler_params=pltpu.CompilerParams(dimension_semantics=("arbitrary",),
                                             vmem_limit_bytes=VMEM_LIMIT),
        name="sample_mixer",
    )(*ins)
    return outs[0], outs[1], outs[2], [(outs[3 + 2 * p], outs[4 + 2 * p]) for p in range(N_PARTS)]


def _sublane_max(x):
    return jnp.max(x, axis=0, keepdims=True)


def _route_kernel(lg_ref, b_ref, tri_ref, idx_ref, wt_ref, rank_ref, cnt_ref, carry, *, t_valid):
    i = pl.program_id(0)
    ne, tt = lg_ref.shape
    ng = N_EXPERT_GROUPS
    per = ne // ng

    @pl.when(i == 0)
    def _():
        carry[...] = jnp.zeros_like(carry)

    s = _sigmoid(lg_ref[...])
    sel = s + b_ref[...]
    s3 = [s[p * ng:(p + 1) * ng, :] for p in range(per)]
    sel3 = [sel[p * ng:(p + 1) * ng, :] for p in range(per)]
    m1 = sel3[0]
    m2 = jnp.full_like(m1, NEG_INF)
    for p in range(1, per):
        m2 = jnp.maximum(m2, jnp.minimum(m1, sel3[p]))
        m1 = jnp.maximum(m1, sel3[p])
    gs = m1 + m2
    gi = lax.broadcasted_iota(jnp.int32, (ng, tt), 0)
    beaten = jnp.zeros((ng, tt), jnp.int32)
    for g in range(ng):
        row = gs[g:g + 1, :]
        beats = (row > gs) | ((row == gs) & (gi > g))
        beaten = beaten + beats.astype(jnp.int32)
    keep = beaten < TOPK_GROUPS
    cur = [jnp.where(keep, sel3[p], NEG_INF) for p in range(per)]
    eid = [(gi * per + p).astype(F32) for p in range(per)]
    idxs, wts, hits = [], [], []
    for _ in range(TOP_K):
        m = cur[0]
        for p in range(1, per):
            m = jnp.maximum(m, cur[p])
        m = _sublane_max(m)
        cand = jnp.where(cur[0] == m, eid[0], float(ne))
        for p in range(1, per):
            cand = jnp.minimum(cand, jnp.where(cur[p] == m, eid[p], float(ne)))
        e_sel = jnp.min(cand, axis=0, keepdims=True)
        hit = [eid[p] == e_sel for p in range(per)]
        wk = jnp.where(hit[0], s3[0], 0.0)
        for p in range(1, per):
            wk = wk + jnp.where(hit[p], s3[p], 0.0)
        wts.append(jnp.sum(wk, axis=0, keepdims=True))
        cur = [jnp.where(hit[p], NEG_INF, cur[p]) for p in range(per)]
        idxs.append(e_sel)
        hits.append(hit)
    wsum = wts[0]
    for k in range(1, TOP_K):
        wsum = wsum + wts[k]
    chosen = [functools.reduce(lambda a, b: a | b, [hits[k][p] for k in range(TOP_K)]) for p in range(per)]
    real = (i * tt + lax.broadcasted_iota(I32, (1, tt), 1)) < t_valid
    onehot = jnp.concatenate([(c & real).astype(F32) for c in chosen], axis=0)
    before = _dot(onehot.astype(BF16), tri_ref[...]) + carry[...]
    for k in range(TOP_K):
        rk = jnp.where(hits[k][0], before[0:ng, :], 0.0)
        for p in range(1, per):
            rk = rk + jnp.where(hits[k][p], before[p * ng:(p + 1) * ng, :], 0.0)
        rank_ref[k:k + 1, :] = jnp.sum(rk, axis=0, keepdims=True).astype(jnp.int32)
        idx_ref[k:k + 1, :] = idxs[k].astype(jnp.int32)
        wt_ref[k:k + 1, :] = wts[k] / wsum * ROUTED_SCALE
    carry[...] = carry[...] + jnp.sum(onehot, axis=1, keepdims=True)
    cnt_ref[...] = carry[...]


def _route(lgt, b_perm, t_valid):
    ne, t = lgt.shape
    tt = ROUTE_TILE
    tri = (jnp.arange(tt)[:, None] < jnp.arange(tt)[None, :]).astype(BF16)
    return pl.pallas_call(
        functools.partial(_route_kernel, t_valid=t_valid),
        grid=(t // tt,),
        in_specs=[pl.BlockSpec((ne, tt), lambda i: (0, i)),
                  pl.BlockSpec((ne, 1), lambda i: (0, 0)),
                  pl.BlockSpec((tt, tt), lambda i: (0, 0))],
        out_specs=[pl.BlockSpec((TOP_K, tt), lambda i: (0, i)),
                   pl.BlockSpec((TOP_K, tt), lambda i: (0, i)),
                   pl.BlockSpec((TOP_K, tt), lambda i: (0, i)),
                   pl.BlockSpec((ne, 1), lambda i: (0, 0))],
        out_shape=[jax.ShapeDtypeStruct((TOP_K, t), jnp.int32),
                   jax.ShapeDtypeStruct((TOP_K, t), F32),
                   jax.ShapeDtypeStruct((TOP_K, t), jnp.int32),
                   jax.ShapeDtypeStruct((ne, 1), F32)],
        scratch_shapes=[pltpu.VMEM((ne, 1), F32)],
        compiler_params=pltpu.CompilerParams(dimension_semantics=("arbitrary",),
                                             vmem_limit_bytes=VMEM_LIMIT),
        name="route",
    )(lgt, b_perm, tri)


def _dest_kernel(pstart_ref, idx_ref, rank_ref, dest_ref, *, t_valid, spare_row):
    idx = idx_ref[...]
    k_top, tile = idx.shape

    def add_start(e, acc):
        return acc + jnp.where(idx == e, pstart_ref[e], 0)

    dest = lax.fori_loop(0, pstart_ref.shape[0], add_start, rank_ref[...])
    tok = pl.program_id(0) * tile + lax.broadcasted_iota(I32, idx.shape, 1)
    spare = spare_row + (tok - t_valid) * k_top + lax.broadcasted_iota(I32, idx.shape, 0)
    dest_ref[...] = jnp.where(tok < t_valid, dest, spare)


def _dest_rows(pstart, idx, rank, t_valid, spare_row):
    k_top, t = idx.shape
    tile = DEST_TILE
    spec = pl.BlockSpec((k_top, tile), lambda i, ps: (0, i))
    return pl.pallas_call(
        functools.partial(_dest_kernel, t_valid=t_valid, spare_row=spare_row),
        grid_spec=pltpu.PrefetchScalarGridSpec(num_scalar_prefetch=1, grid=(t // tile,),
                                               in_specs=[spec, spec], out_specs=spec),
        out_shape=jax.ShapeDtypeStruct((k_top, t), I32),
        compiler_params=pltpu.CompilerParams(dimension_semantics=("arbitrary",),
                                             vmem_limit_bytes=VMEM_LIMIT),
        name="dest_rows",
    )(pstart, idx, rank)


def _expert_kernel(first_ref, valid_ref, xs_hbm, wgu_ref, wd_ref, y_hbm, xbuf, ybuf, in_sem, out_sem):
    step = pl.program_id(0)
    ne = first_ref.shape[0] - 1
    nbuf, rows = xbuf.shape[:2]
    half = rows // 2
    per_step, de = wd_ref.shape[:2]
    n_total = first_ref[ne]

    def in_copy(g, h):
        return pltpu.make_async_copy(xs_hbm.at[pl.ds(g * rows + h * half, half)],
                                     xbuf.at[g % nbuf, pl.ds(h * half, half)], in_sem.at[g % nbuf])

    def out_copy(g, h):
        return pltpu.make_async_copy(ybuf.at[g % nbuf, pl.ds(h * half, half)],
                                     y_hbm.at[pl.ds(g * rows + h * half, half)], out_sem.at[g % nbuf])

    def both_halves(copy, g, act):
        getattr(copy(g, 0), act)()

        @pl.when(valid_ref[g] > half)
        def _():
            getattr(copy(g, 1), act)()

    @pl.when(step == 0)
    def _():
        for g in range(nbuf - 1):
            @pl.when(g < n_total)
            def _():
                both_halves(in_copy, g, "start")

    def block(g, carry, *, j):
        slot = g % nbuf

        @pl.when(g + nbuf - 1 < n_total)
        def _():
            both_halves(in_copy, g + nbuf - 1, "start")

        both_halves(in_copy, g, "wait")

        @pl.when(g >= nbuf)
        def _():
            both_halves(out_copy, g - nbuf, "wait")

        valid = valid_ref[g]

        def run(n):
            rid = lax.broadcasted_iota(I32, (n, 1), 0)
            xs = jnp.where(rid < valid, _unpack_bf16_pairs(xbuf[slot, 0:n]), 0.0).astype(BF16)
            gu = _dot(xs, wgu_ref[j])
            hb = _silu(gu[:, :de]) * gu[:, de:]
            ybuf[slot, 0:n] = _pack_bf16_pairs(_dot(hb.astype(BF16), wd_ref[j]))

        @pl.when(valid > half)
        def _():
            run(rows)

        @pl.when(valid <= half)
        def _():
            run(half)

        both_halves(out_copy, g, "start")
        return carry

    for j in range(per_step):
        e = step * per_step + j
        lax.fori_loop(first_ref[e], first_ref[e + 1], functools.partial(block, j=j), 0)

    @pl.when(step == pl.num_programs(0) - 1)
    def _():
        for back in range(nbuf, 0, -1):
            @pl.when(n_total >= back)
            def _():
                both_halves(out_copy, n_total - back, "wait")


def _experts(xs, w_gu, w_down, first_block, block_valid):
    p, dw = xs.shape
    ne, de, d = w_down.shape
    return pl.pallas_call(
        _expert_kernel,
        grid_spec=pltpu.PrefetchScalarGridSpec(
            num_scalar_prefetch=2,
            grid=(ne // EXPERTS_PER_STEP,),
            in_specs=[pl.BlockSpec(memory_space=pl.ANY),
                      pl.BlockSpec((EXPERTS_PER_STEP, d, 2 * de), lambda e, fb, cn: (e, 0, 0)),
                      pl.BlockSpec((EXPERTS_PER_STEP, de, d), lambda e, fb, cn: (e, 0, 0))],
            out_specs=pl.BlockSpec(memory_space=pl.ANY),
            scratch_shapes=[pltpu.VMEM((EXPERT_BUFFERS, EXPERT_ROWS, dw), I32),
                            pltpu.VMEM((EXPERT_BUFFERS, EXPERT_ROWS, dw), I32),
                            pltpu.SemaphoreType.DMA((EXPERT_BUFFERS,)), pltpu.SemaphoreType.DMA((EXPERT_BUFFERS,))]),
        out_shape=jax.ShapeDtypeStruct((p, dw), I32),
        compiler_params=pltpu.CompilerParams(dimension_semantics=("arbitrary",),
                                             vmem_limit_bytes=VMEM_LIMIT),
        name="experts",
    )(first_block, block_valid, xs, w_gu, w_down)


def _sc_first_chunk(n_chunks):
    per_worker = n_chunks // SC_WORKERS
    assert per_worker * SC_WORKERS == n_chunks
    return per_worker, (lax.axis_index("s") * SC_CORES + lax.axis_index("c")) * per_worker


def _sc_start(copies):
    for cp in copies:
        cp.start()


def _sc_wait(copies):
    for cp in copies:
        cp.wait()


def _sc_dispatch(h2, dest, n_rows):
    w = h2.shape[1]
    n_chunks, k_top, ch = dest.shape
    mesh = plsc.VectorSubcoreMesh(core_axis_name="c", subcore_axis_name="s")

    @functools.partial(
        pl.kernel, mesh=mesh, out_type=jax.ShapeDtypeStruct((n_rows, w), I32),
        scratch_types=[pltpu.VMEM((2, k_top, ch), I32), pltpu.VMEM((2, ch, w), I32),
                       pltpu.SemaphoreType.DMA((2,)), pltpu.SemaphoreType.DMA((2,))])
    def dispatch(h2_hbm, dest_hbm, xs_hbm, idx_v, rows_v, load_sem, store_sem):
        per_worker, first = _sc_first_chunk(n_chunks)

        def loads(i):
            slot = i % 2
            return (pltpu.make_async_copy(h2_hbm.at[pl.ds((first + i) * ch, ch)], rows_v.at[slot], load_sem.at[slot]),
                    pltpu.make_async_copy(dest_hbm.at[first + i], idx_v.at[slot], load_sem.at[slot]))

        def scatters(i):
            slot = i % 2
            return [pltpu.make_async_copy(rows_v.at[slot], xs_hbm.at[idx_v.at[slot, k]], store_sem.at[slot])
                    for k in range(k_top)]

        for i in range(min(2, per_worker)):
            _sc_start(loads(i))
        for i in range(per_worker):
            _sc_wait(loads(i))
            _sc_start(scatters(i))
            if 1 <= i < per_worker - 1:
                _sc_wait(scatters(i - 1))
                _sc_start(loads(i + 1))
        for i in range(max(per_worker - 2, 0), per_worker):
            _sc_wait(scatters(i))

    return dispatch(h2, dest)


def _sc_combine_gather(y, dest, t):
    w = y.shape[1]
    n_chunks, k_top, ch = dest.shape
    n_local = n_chunks // SC_WORKERS
    mesh = plsc.VectorSubcoreMesh(core_axis_name="c", subcore_axis_name="s")

    @functools.partial(
        pl.kernel, mesh=mesh, out_type=jax.ShapeDtypeStruct((k_top, t, w), I32),
        scratch_types=[pltpu.VMEM((n_local, k_top, ch), I32), pltpu.VMEM((k_top, ch, w), I32),
                       pltpu.SemaphoreType.DMA((k_top,)), pltpu.SemaphoreType.DMA((k_top,))])
    def gather(y_hbm, dest_hbm, out_hbm, idx_v, rows_v, load_sem, store_sem):
        per_worker, first = _sc_first_chunk(n_chunks)
        pltpu.sync_copy(dest_hbm.at[pl.ds(first, per_worker)], idx_v)

        def fetch(i, k):
            return pltpu.make_async_copy(y_hbm.at[idx_v.at[i, k]], rows_v.at[k], load_sem.at[k])

        def put(i, k):
            return pltpu.make_async_copy(rows_v.at[k], out_hbm.at[k, pl.ds((first + i) * ch, ch)], store_sem.at[k])

        @pl.loop(0, per_worker)
        def _(i):
            for k in range(k_top):
                @pl.when(i > 0)
                def _():
                    put(i - 1, k).wait()

                fetch(i, k).start()
            for k in range(k_top):
                fetch(i, k).wait()
                put(i, k).start()

        for k in range(k_top):
            put(per_worker - 1, k).wait()

    return gather(y, dest)


def _final_kernel(xmid_ref, yg_ref, wt_ref, g2_ref, shf_ref, scf_ref, gf_ref, *rest, vec_row0=None):
    o_ref = rest[-1]
    if vec_row0 is not None:
        row = pl.ds((vec_row0 + pl.program_id(0)) % SUBLANES, 1)
        g2_ref, shf_ref, scf_ref = g2_ref.at[row, :], shf_ref.at[row, :], scf_ref.at[row, :]
    wt = wt_ref[...].T[:xmid_ref.shape[0]]
    routed = wt[:, 0:1] * _unpack_bf16_pairs(yg_ref[0])
    for k in range(1, yg_ref.shape[0]):
        routed = routed + wt[:, k:k + 1] * _unpack_bf16_pairs(yg_ref[k])
    x2 = xmid_ref[...] + g2_ref[...] * routed
    o_ref[...] = _rms_norm(x2, gf_ref[...]) * (1.0 + scf_ref[...]) + shf_ref[...]


def _final_prompt(xmid, yg, wts_t, mod, modf, mod_row0, g_final, b0, n_batch, out_prev):
    b, seq, d = xmid.shape
    k_top, _, w = yg.shape
    tl = FINAL_TILE
    nl = seq // tl

    def vec(col):
        return pl.BlockSpec((SUBLANES, d), lambda i, j: ((i + b0 + mod_row0) // SUBLANES, col))

    in_specs = [pl.BlockSpec((None, tl, d), lambda i, j: (i, j, 0)),
                pl.BlockSpec((k_top, tl, w), lambda i, j: (0, i * nl + j, 0)),
                pl.BlockSpec((k_top, tl), lambda i, j: (0, i * nl + j)),
                vec(5), vec(0), vec(1), pl.BlockSpec((1, d), lambda i, j: (0, 0))]
    args = [xmid, yg, wts_t, mod, modf, modf, g_final]
    aliases = {}
    if out_prev is not None:
        in_specs.append(pl.BlockSpec(memory_space=pl.ANY))
        args.append(out_prev)
        aliases = {7: 0}
    return pl.pallas_call(
        functools.partial(_final_kernel, vec_row0=b0 + mod_row0),
        grid=(b, nl),
        in_specs=in_specs,
        out_specs=pl.BlockSpec((None, tl, d), lambda i, j: (i + b0, j, 0)),
        out_shape=jax.ShapeDtypeStruct((n_batch, seq, d), F32),
        compiler_params=pltpu.CompilerParams(dimension_semantics=("arbitrary", "arbitrary"),
                                             vmem_limit_bytes=VMEM_LIMIT),
        input_output_aliases=aliases,
        name="final_prompt",
    )(*args)


def _final_sample(xmid_all, yg, wts_t, mod, modf, g_final, part, rows, first_row, after):
    d = xmid_all.shape[1]
    k_top, _, w = yg.shape

    def vec(col):
        return pl.BlockSpec((rows, d), lambda i: (part, col))

    return pl.pallas_call(
        _final_kernel,
        grid=(1,),
        in_specs=[vec(0),
                  pl.BlockSpec((k_top, rows, w), lambda i: (0, first_row // rows, 0)),
                  pl.BlockSpec((k_top, LANES), lambda i: (0, first_row // LANES)),
                  vec(5), vec(0), vec(1), pl.BlockSpec((1, d), lambda i: (0, 0)), pl.BlockSpec(memory_space=pl.ANY)],
        out_specs=pl.BlockSpec((rows, d), lambda i: (0, 0)),
        out_shape=jax.ShapeDtypeStruct((rows, d), F32),
        compiler_params=pltpu.CompilerParams(dimension_semantics=("arbitrary",),
                                             vmem_limit_bytes=VMEM_LIMIT),
        name="final_sample",
    )(xmid_all, yg, wts_t, mod, modf, modf, g_final, after)


def _block_diag_pairs(w_pool):
    g, c, _ = w_pool.shape
    eye = jnp.eye(g, dtype=w_pool.dtype)
    return (eye[:, None, :, None] * w_pool[:, :, None, :]).reshape(g * c, g * c)


def kernel(x_prompt, x_sample, state_pool, state_conv, c_prompt, c_sample, w_ada, b_ada, g_mix, w_in, w_pool, pool_scale, w_dw, b_dw, ln_g, ln_b, w_out, g_ffn, w_router, b_router, w_gate, w_up, w_down, ws_gate, ws_up, ws_down, w_ada_final, b_ada_final, g_final):
    bp, seq, d = x_prompt.shape
    bs = x_sample.shape[0]
    depth = w_ada.shape[0]
    assert depth == 1 and x_sample.shape[1] == 1
    conv_k = w_dw.shape[1]
    ne = w_router.shape[-1]
    per = ne // N_EXPERT_GROUPS

    row_expert = (jnp.arange(ne) % N_EXPERT_GROUPS) * per + jnp.arange(ne) // N_EXPERT_GROUPS
    wl = {
        "g_mix": g_mix[0][None, :],
        "in": w_in[0].astype(BF16),
        "pool": _block_diag_pairs(w_pool[0]).astype(BF16),
        "pool_scale": pool_scale[0][None, :],
        "dw": w_dw[0],
        "b_dw": b_dw[0][None, :],
        "ln_g": ln_g[0][None, :],
        "ln_b": ln_b[0][None, :],
        "out": w_out[0].astype(BF16),
        "g_ffn": g_ffn[0][None, :],
        "router_t": w_router[0].T[row_expert].astype(BF16),
        "s_gu": jnp.concatenate([ws_gate[0], ws_up[0]], axis=1).astype(BF16),
        "s_down": ws_down[0].astype(BF16),
    }
    b_perm = b_router[0][row_expert][:, None]

    assert bp % N_PARTS == 0 and bs % N_PARTS == 0
    bpp, bsp = bp // N_PARTS, bs // N_PARTS
    tpp = bpp * seq
    t_part = tpp + bsp
    grain = SC_WORKERS * SC_DISPATCH_CHUNK * SC_COMBINE_CHUNK // math.gcd(SC_DISPATCH_CHUNK, SC_COMBINE_CHUNK)
    t_pad = (t_part + grain - 1) // grain * grain
    assert t_pad % ROUTE_TILE == 0 and t_pad % DEST_TILE == 0 and tpp % bsp == 0

    c_all = jnp.concatenate([c_sample, c_prompt], axis=0)
    mod = _ada(c_all, w_ada[0], b_ada[0])
    modf = _ada(c_all, w_ada_final, b_ada_final)

    xmid_s, new_pool_t, new_conv_t, tails = _sample_mixer(
        x_sample.reshape(bs, d), mod, jnp.transpose(state_pool[0], (1, 0, 2)),
        jnp.transpose(state_conv[0], (1, 0, 2)), wl, tpp, t_pad, conv_k=conv_k)
    new_pool_s = jnp.transpose(new_pool_t, (1, 0, 2))
    new_conv_s = jnp.transpose(new_conv_t, (1, 0, 2))
    w_experts = tuple(a.reshape(a.shape[1:]) for a in (w_gate, w_up, w_down))

    n_blocks = (t_part * TOP_K + EXPERT_ROWS - 1) // EXPERT_ROWS + ne
    n_spare = -(-(t_pad - t_part) * TOP_K // EXPERT_ROWS)
    n_rows = (n_blocks + n_spare) * EXPERT_ROWS

    y_prompt, y_samples, npools, nconvs, w_bf = None, [], [], [], None
    mixed = []
    for p in range(N_PARTS):
        h2, lgt = tails[p]
        xmid_p, h2, lgt, npool_p, nconv_p, *w_bf = _prompt_mixer(
            x_prompt, mod, bs, wl, h2, lgt, p * bpp, bpp, w_experts, w_bf, conv_k=conv_k)
        mixed.append((xmid_p, h2, lgt))
        npools.append(npool_p)
        nconvs.append(nconv_p)
    for p in range(N_PARTS):
        xmid_p, h2, lgt = mixed[p]
        idx, wts, rank, counts_perm = _route(lgt, b_perm, t_part)
        counts = jnp.zeros((ne,), I32).at[row_expert].set(counts_perm[:, 0].astype(I32))
        nblk = (counts + EXPERT_ROWS - 1) // EXPERT_ROWS
        first_block = jnp.concatenate([jnp.zeros((1,), I32), jnp.cumsum(nblk).astype(I32)])
        dest = _dest_rows(first_block[:ne] * EXPERT_ROWS, idx, rank, t_part, n_blocks * EXPERT_ROWS)

        def chunked(ch):
            return dest.reshape(TOP_K, t_pad // ch, ch).transpose(1, 0, 2)

        xs = _sc_dispatch(h2, chunked(SC_DISPATCH_CHUNK), n_rows)
        blk = jnp.arange(n_blocks, dtype=I32)[:, None]
        inside = (blk >= first_block[None, :ne]) & (blk < first_block[None, 1:])
        rows_left = counts[None, :] - (blk - first_block[None, :ne]) * EXPERT_ROWS
        block_valid = jnp.sum(jnp.where(inside, jnp.minimum(rows_left, EXPERT_ROWS), 0), axis=1)
        y = _experts(xs, w_bf[0], w_bf[1], first_block, block_valid)
        yg = _sc_combine_gather(y, chunked(SC_COMBINE_CHUNK), t_pad)

        y_samples.append(_final_sample(xmid_s, yg, wts, mod, modf, g_final[None, :], p, bsp, tpp,
                                       g_final if y_prompt is None else y_prompt))
        y_prompt = _final_prompt(xmid_p, yg, wts, mod, modf, bs, g_final[None, :], p * bpp, bp, y_prompt)
    y_sample = jnp.concatenate(y_samples, axis=0)
    npool_p = jnp.concatenate(npools, axis=0)
    nconv_p = jnp.concatenate(nconvs, axis=0)

    return (y_prompt, y_sample[:, None, :], npool_p[None], nconv_p[None], new_pool_s[None], new_conv_s[None])
```

```python
import functools
import math

import jax
import jax.numpy as jnp
from jax import lax
from jax.experimental import pallas as pl
from jax.experimental.pallas import tpu as pltpu
from jax.experimental.pallas import tpu_sc as plsc

POOL_WINDOWS = (2, 4, 8, 16)
N_EXPERT_GROUPS = 8
TOPK_GROUPS = 4
TOP_K = 8
ROUTED_SCALE = 2.5
EPS = 1e-6

LANES = 128
SUBLANES = 8
VMEM_LIMIT = 52 * 1024 * 1024

ADA_TILE = 1024
SEQ_TILE = 512
ROW_CHUNK = 16
ROUTE_TILE = 1408
EXPERT_ROWS = 512
N_PARTS = 2
EXPERT_BUFFERS = 6
EXPERTS_PER_STEP = 2
FINAL_TILE = 512
DEST_TILE = 2816

SC_CORES = 2
SC_SUBCORES = 16
SC_WORKERS = SC_CORES * SC_SUBCORES
SC_DISPATCH_CHUNK = 88
SC_COMBINE_CHUNK = 24

F32 = jnp.float32
BF16 = jnp.bfloat16
I32 = jnp.int32
U32 = jnp.uint32
NEG_INF = float("-inf")
HI16 = 0xFFFF0000


def _sigmoid(x):
    return 1.0 / (1.0 + jnp.exp(-x))


def _silu(x):
    return x * _sigmoid(x)


def _rms_norm(x, g):
    return x * lax.rsqrt(jnp.mean(x * x, axis=-1, keepdims=True) + EPS) * g


def _dot(a, b):
    return jnp.dot(a, b, preferred_element_type=F32)


def _pack_bf16_pairs(x):
    w = x.shape[1] // 2
    bits = lax.bitcast_convert_type(x.astype(BF16).astype(F32), U32)
    return lax.bitcast_convert_type((bits[:, :w] >> 16) | (bits[:, w:] & U32(HI16)), I32)


def _unpack_bf16_pairs(p):
    bits = lax.bitcast_convert_type(p, U32)
    lo = lax.bitcast_convert_type(bits << 16, F32)
    hi = lax.bitcast_convert_type(bits & U32(HI16), F32)
    return jnp.concatenate([lo, hi], axis=1)


def _ada_kernel(c_ref, w_ref, b_ref, o_ref):
    s = _silu(c_ref[...])
    o_ref[...] = _dot(s.astype(BF16), w_ref[...].astype(BF16)) + b_ref[...]


def _ada(c, w, b):
    rows, d = c.shape
    n = w.shape[1]
    tn = ADA_TILE
    return pl.pallas_call(
        _ada_kernel,
        grid=(n // tn,),
        in_specs=[pl.BlockSpec((rows, d), lambda j: (0, 0)),
                  pl.BlockSpec((d, tn), lambda j: (0, j)),
                  pl.BlockSpec((1, tn), lambda j: (0, j))],
        out_specs=pl.BlockSpec((rows, tn), lambda j: (0, j)),
        out_shape=jax.ShapeDtypeStruct((rows, n), F32),
        compiler_params=pltpu.CompilerParams(dimension_semantics=("arbitrary",),
                                             vmem_limit_bytes=VMEM_LIMIT),
        name="ada",
    )(c, w, b.reshape(1, n))


def _mixer_tail(x, pool_d, conv_act, mod, w, xmid_ref, h2_ref, lgt_ref):
    sh2, sc2, g1, g2 = mod
    pw = pool_d.shape[1]
    pool_out = _dot(pool_d, w["pool"][...]) * w["pool_scale"][...]
    mix = _dot(pool_out.astype(BF16), w["out"][:pw, :]) + _dot(conv_act, w["out"][pw:, :])
    x1 = x + g1 * mix
    h2f = _rms_norm(x1, w["g_ffn"][...]) * (1.0 + sc2) + sh2
    h2_ref[...] = _pack_bf16_pairs(h2f)
    h2 = h2f.astype(BF16)
    lgt_ref[...] = lax.dot_general(w["router_t"][...], h2, (((1,), (1,)), ((), ())),
                                   preferred_element_type=F32)
    gu = _dot(h2, w["s_gu"][...])
    de = gu.shape[1] // 2
    hs = _silu(gu[:, :de]) * gu[:, de:]
    shared = _dot(hs.astype(BF16), w["s_down"][...])
    xmid_ref[...] = x1 + g2 * shared


_W_NAMES = ("g_mix", "in", "pool", "pool_scale", "dw", "b_dw", "ln_g", "ln_b", "out", "g_ffn",
            "router_t", "s_gu", "s_down")


def _layer_norm_silu(yc, g, b):
    mu = jnp.mean(yc, axis=-1, keepdims=True)
    yz = yc - mu
    var = jnp.mean(yz * yz, axis=-1, keepdims=True)
    return _silu(yz * lax.rsqrt(var + EPS) * g + b)


U_HALO = 32
A_HALO = 32


def _prompt_mixer_kernel(x_ref, mod_ref, *refs, tl, d, pw, cw, conv_k, mod_row0):
    nw = len(_W_NAMES)
    w = dict(zip(_W_NAMES, refs[:nw]))
    wg_ref, wu_ref, wd_ref = refs[nw + 2:nw + 5]
    xmid_ref, h2_ref, lgt_ref, npool_ref, nconv_ref, wgu_bf_ref, wd_bf_ref = refs[-16:-9]
    ubuf, s2buf, s4buf, s8buf, abuf, ashift, dbuf, cbuf, rowb = refs[-9:]
    de = wg_ref.shape[2]
    wgu_bf_ref[:, :, :de] = wg_ref[...].astype(BF16)
    wgu_bf_ref[:, :, de:] = wu_ref[...].astype(BF16)
    wd_bf_ref[...] = wd_ref[...].astype(BF16)
    l = pl.program_id(1)
    pool_buf = max(POOL_WINDOWS) - 1
    uh, ah = U_HALO, A_HALO
    pg = pw // len(POOL_WINDOWS)
    nu, na = uh + tl, ah + tl

    @pl.when(l == 0)
    def _():
        ubuf[0:uh, :] = jnp.zeros((uh, pw), F32)
        abuf[0:ah, :] = jnp.zeros((ah, cw), F32)

    @pl.when(l > 0)
    def _():
        ubuf[0:uh, :] = ubuf[tl:tl + uh, :]
        abuf[0:ah, :] = abuf[tl:tl + ah, :]

    x = x_ref[0]
    row = pl.ds((mod_row0 + pl.program_id(0)) % SUBLANES, 1)
    sh1, sc1, g1, sh2, sc2, g2 = [mod_ref[row, i * d:(i + 1) * d] for i in range(6)]

    h = _rms_norm(x, w["g_mix"][...]) * (1.0 + sc1) + sh1
    z = _dot(h.astype(BF16), w["in"][...])
    u = z[:, :pw]
    ubuf[uh:nu, :] = u
    abuf[ah:na, :] = z[:, pw:pw + cw] * _sigmoid(z[:, pw + cw:])

    s2buf[8:nu, :] = ubuf[8:nu, :] + ubuf[7:nu - 1, :]
    s4buf[16:nu, :] = s2buf[16:nu, pg:] + s2buf[14:nu - 2, pg:]
    s8buf[24:nu, :] = s4buf[24:nu, pg:] + s4buf[20:nu - 4, pg:]
    t = l * tl + lax.broadcasted_iota(I32, (tl, 1), 0)
    sums = (s2buf[uh:nu, 0:pg], s4buf[uh:nu, 0:pg], s8buf[uh:nu, 0:pg],
            s8buf[uh:nu, pg:2 * pg] + s8buf[uh - 8:nu - 8, pg:2 * pg])
    for g, win in enumerate(POOL_WINDOWS):
        cols = slice(g * pg, (g + 1) * pg)
        inv = 1.0 / jnp.minimum(win, t + 1).astype(F32)
        dbuf[:, cols] = (sums[g] * inv - u[:, cols]).astype(BF16)

    for r in range(1, SUBLANES):
        ashift[r - 1, 8:na, :] = abuf[8 - r:na - r, :]

    for k in range(conv_k):
        rowb[k] = jnp.broadcast_to(w["dw"][k:k + 1, :], (SUBLANES, cw))
    rowb[conv_k] = jnp.broadcast_to(w["b_dw"][...], (SUBLANES, cw))

    for c in range(tl // ROW_CHUNK):
        groups = []
        for o in range(ah + c * ROW_CHUNK, ah + (c + 1) * ROW_CHUNK, SUBLANES):
            yc = abuf[o:o + SUBLANES, :] * rowb[conv_k - 1]
            for back in range(1, conv_k):
                q, r = divmod(back, SUBLANES)
                src = abuf if r == 0 else ashift.at[r - 1]
                yc = yc + src[o - q * SUBLANES:o - (q - 1) * SUBLANES, :] * rowb[conv_k - 1 - back]
            groups.append(yc + rowb[conv_k])
        r0 = c * ROW_CHUNK
        cbuf[r0:r0 + ROW_CHUNK, :] = _layer_norm_silu(
            jnp.concatenate(groups, axis=0), w["ln_g"][...], w["ln_b"][...]).astype(BF16)

    @pl.when(l == pl.num_programs(1) - 1)
    def _():
        npool_ref[0] = ubuf[nu - pool_buf:nu, :]
        nconv_ref[0] = abuf[na - (conv_k - 1):na, :]

    _mixer_tail(x, dbuf[...], cbuf[...], (sh2, sc2, g1, g2), w, xmid_ref.at[0], h2_ref, lgt_ref)


def _sample_mixer_kernel(x_ref, mod_ref, sp_ref, sc_ref, *refs, d, pw, cw, conv_k):
    nw = len(_W_NAMES)
    w = dict(zip(_W_NAMES, refs[:nw]))
    xmid_ref, npool_ref, nconv_ref = refs[nw:nw + 3]
    tails = refs[nw + 3:-2]
    h2_tmp, lgt_tmp = refs[-2:]
    pool_buf = max(POOL_WINDOWS) - 1
    pg = pw // len(POOL_WINDOWS)
    x = x_ref[...]
    sh1, sc1, g1, sh2, sc2, g2 = [mod_ref[:, i * d:(i + 1) * d] for i in range(6)]
    h = _rms_norm(x, w["g_mix"][...]) * (1.0 + sc1) + sh1
    z = _dot(h.astype(BF16), w["in"][...])
    u = z[:, :pw]
    a = z[:, pw:pw + cw] * _sigmoid(z[:, pw + cw:])
    npool_ref[0:pool_buf - 1] = sp_ref[1:pool_buf]
    npool_ref[pool_buf - 1] = u
    nconv_ref[0:conv_k - 2] = sc_ref[1:conv_k - 1]
    nconv_ref[conv_k - 2] = a
    ds = []
    for g, win in enumerate(POOL_WINDOWS):
        cols = slice(g * pg, (g + 1) * pg)
        acc = u[:, cols]
        for j in range(1, win):
            acc = acc + sp_ref[pool_buf - j, :, cols]
        cnt = float(min(win, 1 + pool_buf))
        ds.append(acc / cnt - u[:, cols])
    pool_d = jnp.concatenate(ds, axis=-1).astype(BF16)
    yc = a * w["dw"][conv_k - 1:conv_k, :]
    for k in range(conv_k - 1):
        yc = yc + sc_ref[k] * w["dw"][k:k + 1, :]
    yc = yc + w["b_dw"][...]
    conv_act = _layer_norm_silu(yc, w["ln_g"][...], w["ln_b"][...]).astype(BF16)
    _mixer_tail(x, pool_d, conv_act, (sh2, sc2, g1, g2), w, xmid_ref, h2_tmp, lgt_tmp)
    n_parts = len(tails) // 2
    share = x.shape[0] // n_parts
    for p in range(n_parts):
        h2_ref, lgt_ref = tails[2 * p], tails[2 * p + 1]
        h2_ref[...] = jnp.zeros_like(h2_ref)
        lgt_ref[...] = jnp.zeros_like(lgt_ref)
        h2_ref[0:share, :] = h2_tmp[p * share:(p + 1) * share, :]
        lgt_ref[:, 0:share] = lgt_tmp[:, p * share:(p + 1) * share]


def _full_spec(a):
    nd = a.ndim
    return pl.BlockSpec(a.shape, lambda *_: (0,) * nd)


def _prompt_mixer(x, mod, mod_row0, wl, h2_all, lgt_all, b0, b, w_experts, w_bf_prev, after, *, conv_k):
    n_batch, seq, d = x.shape
    w_gate, w_up, w_down = w_experts
    n_exp, _, de = w_gate.shape
    pw = wl["pool"].shape[0]
    cw = wl["dw"].shape[1]
    ne = wl["router_t"].shape[0]
    tl = SEQ_TILE
    nl = seq // tl
    pg = pw // len(POOL_WINDOWS)
    assert POOL_WINDOWS == (2, 4, 8, 16) and conv_k - 1 <= A_HALO
    ws = [wl[n] for n in _W_NAMES]
    kern = functools.partial(_prompt_mixer_kernel, tl=tl, d=d, pw=pw, cw=cw, conv_k=conv_k, mod_row0=mod_row0 + b0)
    ecs = n_exp // (n_batch * nl)
    assert ecs * n_batch * nl == n_exp

    def w_map(i, j):
        return ((i + b0) * nl + j, 0, 0)

    ins = [x, mod, *ws, h2_all, lgt_all, w_gate, w_up, w_down]
    in_specs = ([pl.BlockSpec((1, tl, d), lambda i, j: (i + b0, j, 0)),
                 pl.BlockSpec((SUBLANES, mod.shape[-1]), lambda i, j: ((i + b0 + mod_row0) // SUBLANES, 0))]
                + [_full_spec(a) for a in ws]
                + [pl.BlockSpec(memory_space=pl.ANY), pl.BlockSpec(memory_space=pl.ANY),
                   pl.BlockSpec((ecs, d, de), w_map), pl.BlockSpec((ecs, d, de), w_map),
                   pl.BlockSpec((ecs, de, d), w_map)])
    aliases = {2 + len(ws): 1, 3 + len(ws): 2}
    if w_bf_prev is not None:
        aliases.update({len(ins): 5, len(ins) + 1: 6})
        ins += list(w_bf_prev)
        in_specs += [pl.BlockSpec(memory_space=pl.ANY), pl.BlockSpec(memory_space=pl.ANY)]
    if after is not None:
        ins.append(after)
        in_specs.append(pl.BlockSpec(memory_space=pl.ANY))
    return pl.pallas_call(
        kern,
        grid=(b, nl),
        in_specs=in_specs,
        out_specs=[pl.BlockSpec((1, tl, d), lambda i, j: (i, j, 0)),
                   pl.BlockSpec((tl, d // 2), lambda i, j: (i * nl + j, 0)),
                   pl.BlockSpec((ne, tl), lambda i, j: (0, i * nl + j)),
                   pl.BlockSpec((1, max(POOL_WINDOWS) - 1, pw), lambda i, j: (i, 0, 0)),
                   pl.BlockSpec((1, conv_k - 1, cw), lambda i, j: (i, 0, 0)),
                   pl.BlockSpec((ecs, d, 2 * de), w_map), pl.BlockSpec((ecs, de, d), w_map)],
        out_shape=[jax.ShapeDtypeStruct((b, seq, d), F32),
                   jax.ShapeDtypeStruct(h2_all.shape, I32),
                   jax.ShapeDtypeStruct(lgt_all.shape, F32),
                   jax.ShapeDtypeStruct((b, max(POOL_WINDOWS) - 1, pw), F32),
                   jax.ShapeDtypeStruct((b, conv_k - 1, cw), F32),
                   jax.ShapeDtypeStruct((n_exp, d, 2 * de), BF16),
                   jax.ShapeDtypeStruct((n_exp, de, d), BF16)],
        scratch_shapes=[pltpu.VMEM((U_HALO + tl, pw), F32), pltpu.VMEM((U_HALO + tl, pw), F32),
                        pltpu.VMEM((U_HALO + tl, pw - pg), F32), pltpu.VMEM((U_HALO + tl, pw - 2 * pg), F32),
                        pltpu.VMEM((A_HALO + tl, cw), F32), pltpu.VMEM((SUBLANES - 1, A_HALO + tl, cw), F32),
                        pltpu.VMEM((tl, pw), BF16), pltpu.VMEM((tl, cw), BF16),
                        pltpu.VMEM((conv_k + 1, SUBLANES, cw), F32)],
        compiler_params=pltpu.CompilerParams(dimension_semantics=("arbitrary", "arbitrary"),
                                             vmem_limit_bytes=VMEM_LIMIT),
        input_output_aliases=aliases,
        name="prompt_mixer",
    )(*ins)


def _sample_mixer(x, mod, sp, sc, wl, t_prompt, t_pad, *, conv_k):
    rows, d = x.shape
    tail = t_pad - t_prompt
    assert t_prompt % tail == 0 and rows % N_PARTS == 0 and rows // N_PARTS <= tail
    pw = wl["pool"].shape[0]
    cw = wl["dw"].shape[1]
    ne = wl["router_t"].shape[0]
    ws = [wl[n] for n in _W_NAMES]
    kern = functools.partial(_sample_mixer_kernel, d=d, pw=pw, cw=cw, conv_k=conv_k)
    ins = [x, mod, sp, sc] + ws
    tail_specs = [pl.BlockSpec((tail, d // 2), lambda i: (t_prompt // tail, 0)),
                  pl.BlockSpec((ne, tail), lambda i: (0, t_prompt // tail))] * N_PARTS
    tail_shapes = [jax.ShapeDtypeStruct((t_pad, d // 2), I32), jax.ShapeDtypeStruct((ne, t_pad), F32)] * N_PARTS
    outs = pl.pallas_call(
        kern,
        grid=(1,),
        in_specs=[_full_spec(x), pl.BlockSpec((rows, mod.shape[1]), lambda i: (0, 0))]
                 + [_full_spec(a) for a in ins[2:]],
        out_specs=[pl.BlockSpec((rows, d), lambda i: (0, 0)), _full_spec(sp), _full_spec(sc)] + tail_specs,
        out_shape=[jax.ShapeDtypeStruct((rows, d), F32),
                   jax.ShapeDtypeStruct(sp.shape, F32),
                   jax.ShapeDtypeStruct(sc.shape, F32)] + tail_shapes,
        scratch_shapes=[pltpu.VMEM((rows, d // 2), I32), pltpu.VMEM((ne, rows), F32)],
        compiler_params=pltpu.CompilerParams(dimension_semantics=("arbitrary",),
                                             vmem_limit_bytes=VMEM_LIMIT),
        name="sample_mixer",
    )(*ins)
    return outs[0], outs[1], outs[2], [(outs[3 + 2 * p], outs[4 + 2 * p]) for p in range(N_PARTS)]


def _sublane_max(x):
    return jnp.max(x, axis=0, keepdims=True)


def _route_kernel(lg_ref, b_ref, tri_ref, idx_ref, wt_ref, rank_ref, cnt_ref, carry, *, t_valid):
    i = pl.program_id(0)
    ne, tt = lg_ref.shape
    ng = N_EXPERT_GROUPS
    per = ne // ng

    @pl.when(i == 0)
    def _():
        carry[...] = jnp.zeros_like(carry)

    s = _sigmoid(lg_ref[...])
    sel = s + b_ref[...]
    s3 = [s[p * ng:(p + 1) * ng, :] for p in range(per)]
    sel3 = [sel[p * ng:(p + 1) * ng, :] for p in range(per)]
    m1 = sel3[0]
    m2 = jnp.full_like(m1, NEG_INF)
    for p in range(1, per):
        m2 = jnp.maximum(m2, jnp.minimum(m1, sel3[p]))
        m1 = jnp.maximum(m1, sel3[p])
    gs = m1 + m2
    gi = lax.broadcasted_iota(jnp.int32, (ng, tt), 0)
    beaten = jnp.zeros((ng, tt), jnp.int32)
    for g in range(ng):
        row = gs[g:g + 1, :]
        beats = (row > gs) | ((row == gs) & (gi > g))
        beaten = beaten + beats.astype(jnp.int32)
    keep = beaten < TOPK_GROUPS
    cur = [jnp.where(keep, sel3[p], NEG_INF) for p in range(per)]
    eid = [(gi * per + p).astype(F32) for p in range(per)]
    idxs, wts, hits = [], [], []
    for _ in range(TOP_K):
        m = cur[0]
        for p in range(1, per):
            m = jnp.maximum(m, cur[p])
        m = _sublane_max(m)
        cand = jnp.where(cur[0] == m, eid[0], float(ne))
        for p in range(1, per):
            cand = jnp.minimum(cand, jnp.where(cur[p] == m, eid[p], float(ne)))
        e_sel = jnp.min(cand, axis=0, keepdims=True)
        hit = [eid[p] == e_sel for p in range(per)]
        wk = jnp.where(hit[0], s3[0], 0.0)
        for p in range(1, per):
            wk = wk + jnp.where(hit[p], s3[p], 0.0)
        wts.append(jnp.sum(wk, axis=0, keepdims=True))
        cur = [jnp.where(hit[p], NEG_INF, cur[p]) for p in range(per)]
        idxs.append(e_sel)
        hits.append(hit)
    wsum = wts[0]
    for k in range(1, TOP_K):
        wsum = wsum + wts[k]
    chosen = [functools.reduce(lambda a, b: a | b, [hits[k][p] for k in range(TOP_K)]) for p in range(per)]
    real = (i * tt + lax.broadcasted_iota(I32, (1, tt), 1)) < t_valid
    onehot = jnp.concatenate([(c & real).astype(F32) for c in chosen], axis=0)
    before = _dot(onehot.astype(BF16), tri_ref[...]) + carry[...]
    for k in range(TOP_K):
        rk = jnp.where(hits[k][0], before[0:ng, :], 0.0)
        for p in range(1, per):
            rk = rk + jnp.where(hits[k][p], before[p * ng:(p + 1) * ng, :], 0.0)
        rank_ref[k:k + 1, :] = jnp.sum(rk, axis=0, keepdims=True).astype(jnp.int32)
        idx_ref[k:k + 1, :] = idxs[k].astype(jnp.int32)
        wt_ref[k:k + 1, :] = wts[k] / wsum * ROUTED_SCALE
    carry[...] = carry[...] + jnp.sum(onehot, axis=1, keepdims=True)
    cnt_ref[...] = carry[...]


def _route(lgt, b_perm, t_valid):
    ne, t = lgt.shape
    tt = ROUTE_TILE
    tri = (jnp.arange(tt)[:, None] < jnp.arange(tt)[None, :]).astype(BF16)
    return pl.pallas_call(
        functools.partial(_route_kernel, t_valid=t_valid),
        grid=(t // tt,),
        in_specs=[pl.BlockSpec((ne, tt), lambda i: (0, i)),
                  pl.BlockSpec((ne, 1), lambda i: (0, 0)),
                  pl.BlockSpec((tt, tt), lambda i: (0, 0))],
        out_specs=[pl.BlockSpec((TOP_K, tt), lambda i: (0, i)),
                   pl.BlockSpec((TOP_K, tt), lambda i: (0, i)),
                   pl.BlockSpec((TOP_K, tt), lambda i: (0, i)),
                   pl.BlockSpec((ne, 1), lambda i: (0, 0))],
        out_shape=[jax.ShapeDtypeStruct((TOP_K, t), jnp.int32),
                   jax.ShapeDtypeStruct((TOP_K, t), F32),
                   jax.ShapeDtypeStruct((TOP_K, t), jnp.int32),
                   jax.ShapeDtypeStruct((ne, 1), F32)],
        scratch_shapes=[pltpu.VMEM((ne, 1), F32)],
        compiler_params=pltpu.CompilerParams(dimension_semantics=("arbitrary",),
                                             vmem_limit_bytes=VMEM_LIMIT),
        name="route",
    )(lgt, b_perm, tri)


def _dest_kernel(pstart_ref, idx_ref, rank_ref, dest_ref, *, t_valid, spare_row):
    idx = idx_ref[...]
    k_top, tile = idx.shape

    def add_start(e, acc):
        return acc + jnp.where(idx == e, pstart_ref[e], 0)

    dest = lax.fori_loop(0, pstart_ref.shape[0], add_start, rank_ref[...])
    tok = pl.program_id(0) * tile + lax.broadcasted_iota(I32, idx.shape, 1)
    spare = spare_row + (tok - t_valid) * k_top + lax.broadcasted_iota(I32, idx.shape, 0)
    dest_ref[...] = jnp.where(tok < t_valid, dest, spare)


def _dest_rows(pstart, idx, rank, t_valid, spare_row):
    k_top, t = idx.shape
    tile = DEST_TILE
    spec = pl.BlockSpec((k_top, tile), lambda i, ps: (0, i))
    return pl.pallas_call(
        functools.partial(_dest_kernel, t_valid=t_valid, spare_row=spare_row),
        grid_spec=pltpu.PrefetchScalarGridSpec(num_scalar_prefetch=1, grid=(t // tile,),
                                               in_specs=[spec, spec], out_specs=spec),
        out_shape=jax.ShapeDtypeStruct((k_top, t), I32),
        compiler_params=pltpu.CompilerParams(dimension_semantics=("arbitrary",),
                                             vmem_limit_bytes=VMEM_LIMIT),
        name="dest_rows",
    )(pstart, idx, rank)


def _expert_kernel(first_ref, valid_ref, xs_hbm, wgu_ref, wd_ref, y_hbm, xbuf, ybuf, in_sem, out_sem):
    step = pl.program_id(0)
    ne = first_ref.shape[0] - 1
    nbuf, rows = xbuf.shape[:2]
    half = rows // 2
    per_step, de = wd_ref.shape[:2]
    n_total = first_ref[ne]

    def in_copy(g, h):
        return pltpu.make_async_copy(xs_hbm.at[pl.ds(g * rows + h * half, half)],
                                     xbuf.at[g % nbuf, pl.ds(h * half, half)], in_sem.at[g % nbuf])

    def out_copy(g, h):
        return pltpu.make_async_copy(ybuf.at[g % nbuf, pl.ds(h * half, half)],
                                     y_hbm.at[pl.ds(g * rows + h * half, half)], out_sem.at[g % nbuf])

    def both_halves(copy, g, act):
        getattr(copy(g, 0), act)()

        @pl.when(valid_ref[g] > half)
        def _():
            getattr(copy(g, 1), act)()

    @pl.when(step == 0)
    def _():
        for g in range(nbuf - 1):
            @pl.when(g < n_total)
            def _():
                both_halves(in_copy, g, "start")

    def block(g, carry, *, j):
        slot = g % nbuf

        @pl.when(g + nbuf - 1 < n_total)
        def _():
            both_halves(in_copy, g + nbuf - 1, "start")

        both_halves(in_copy, g, "wait")

        @pl.when(g >= nbuf)
        def _():
            both_halves(out_copy, g - nbuf, "wait")

        valid = valid_ref[g]

        def run(n):
            rid = lax.broadcasted_iota(I32, (n, 1), 0)
            xs = jnp.where(rid < valid, _unpack_bf16_pairs(xbuf[slot, 0:n]), 0.0).astype(BF16)
            gu = _dot(xs, wgu_ref[j])
            hb = _silu(gu[:, :de]) * gu[:, de:]
            ybuf[slot, 0:n] = _pack_bf16_pairs(_dot(hb.astype(BF16), wd_ref[j]))

        @pl.when(valid > half)
        def _():
            run(rows)

        @pl.when(valid <= half)
        def _():
            run(half)

        both_halves(out_copy, g, "start")
        return carry

    for j in range(per_step):
        e = step * per_step + j
        lax.fori_loop(first_ref[e], first_ref[e + 1], functools.partial(block, j=j), 0)

    @pl.when(step == pl.num_programs(0) - 1)
    def _():
        for back in range(nbuf, 0, -1):
            @pl.when(n_total >= back)
            def _():
                both_halves(out_copy, n_total - back, "wait")


def _experts(xs, w_gu, w_down, first_block, block_valid):
    p, dw = xs.shape
    ne, de, d = w_down.shape
    return pl.pallas_call(
        _expert_kernel,
        grid_spec=pltpu.PrefetchScalarGridSpec(
            num_scalar_prefetch=2,
            grid=(ne // EXPERTS_PER_STEP,),
            in_specs=[pl.BlockSpec(memory_space=pl.ANY),
                      pl.BlockSpec((EXPERTS_PER_STEP, d, 2 * de), lambda e, fb, cn: (e, 0, 0)),
                      pl.BlockSpec((EXPERTS_PER_STEP, de, d), lambda e, fb, cn: (e, 0, 0))],
            out_specs=pl.BlockSpec(memory_space=pl.ANY),
            scratch_shapes=[pltpu.VMEM((EXPERT_BUFFERS, EXPERT_ROWS, dw), I32),
                            pltpu.VMEM((EXPERT_BUFFERS, EXPERT_ROWS, dw), I32),
                            pltpu.SemaphoreType.DMA((EXPERT_BUFFERS,)), pltpu.SemaphoreType.DMA((EXPERT_BUFFERS,))]),
        out_shape=jax.ShapeDtypeStruct((p, dw), I32),
        compiler_params=pltpu.CompilerParams(dimension_semantics=("arbitrary",),
                                             vmem_limit_bytes=VMEM_LIMIT),
        name="experts",
    )(first_block, block_valid, xs, w_gu, w_down)


def _sc_first_chunk(n_chunks):
    per_worker = n_chunks // SC_WORKERS
    assert per_worker * SC_WORKERS == n_chunks
    return per_worker, (lax.axis_index("s") * SC_CORES + lax.axis_index("c")) * per_worker


def _sc_start(copies):
    for cp in copies:
        cp.start()


def _sc_wait(copies):
    for cp in copies:
        cp.wait()


def _sc_dispatch(h2, dest, n_rows):
    w = h2.shape[1]
    n_chunks, k_top, ch = dest.shape
    mesh = plsc.VectorSubcoreMesh(core_axis_name="c", subcore_axis_name="s")

    @functools.partial(
        pl.kernel, mesh=mesh, out_type=jax.ShapeDtypeStruct((n_rows, w), I32),
        scratch_types=[pltpu.VMEM((2, k_top, ch), I32), pltpu.VMEM((2, ch, w), I32),
                       pltpu.SemaphoreType.DMA((2,)), pltpu.SemaphoreType.DMA((2,))])
    def dispatch(h2_hbm, dest_hbm, xs_hbm, idx_v, rows_v, load_sem, store_sem):
        per_worker, first = _sc_first_chunk(n_chunks)

        def loads(i):
            slot = i % 2
            return (pltpu.make_async_copy(h2_hbm.at[pl.ds((first + i) * ch, ch)], rows_v.at[slot], load_sem.at[slot]),
                    pltpu.make_async_copy(dest_hbm.at[first + i], idx_v.at[slot], load_sem.at[slot]))

        def scatters(i):
            slot = i % 2
            return [pltpu.make_async_copy(rows_v.at[slot], xs_hbm.at[idx_v.at[slot, k]], store_sem.at[slot])
                    for k in range(k_top)]

        for i in range(min(2, per_worker)):
            _sc_start(loads(i))
        for i in range(per_worker):
            _sc_wait(loads(i))
            _sc_start(scatters(i))
            if 1 <= i < per_worker - 1:
                _sc_wait(scatters(i - 1))
                _sc_start(loads(i + 1))
        for i in range(max(per_worker - 2, 0), per_worker):
            _sc_wait(scatters(i))

    return dispatch(h2, dest)


def _sc_combine_gather(y, dest, t):
    w = y.shape[1]
    n_chunks, k_top, ch = dest.shape
    n_local = n_chunks // SC_WORKERS
    mesh = plsc.VectorSubcoreMesh(core_axis_name="c", subcore_axis_name="s")

    @functools.partial(
        pl.kernel, mesh=mesh, out_type=jax.ShapeDtypeStruct((k_top, t, w), I32),
        scratch_types=[pltpu.VMEM((n_local, k_top, ch), I32), pltpu.VMEM((k_top, ch, w), I32),
                       pltpu.SemaphoreType.DMA((k_top,)), pltpu.SemaphoreType.DMA((k_top,))])
    def gather(y_hbm, dest_hbm, out_hbm, idx_v, rows_v, load_sem, store_sem):
        per_worker, first = _sc_first_chunk(n_chunks)
        pltpu.sync_copy(dest_hbm.at[pl.ds(first, per_worker)], idx_v)

        def fetch(i, k):
            return pltpu.make_async_copy(y_hbm.at[idx_v.at[i, k]], rows_v.at[k], load_sem.at[k])

        def put(i, k):
            return pltpu.make_async_copy(rows_v.at[k], out_hbm.at[k, pl.ds((first + i) * ch, ch)], store_sem.at[k])

        @pl.loop(0, per_worker)
        def _(i):
            for k in range(k_top):
                @pl.when(i > 0)
                def _():
                    put(i - 1, k).wait()

                fetch(i, k).start()
            for k in range(k_top):
                fetch(i, k).wait()
                put(i, k).start()

        for k in range(k_top):
            put(per_worker - 1, k).wait()

    return gather(y, dest)


def _final_kernel(xmid_ref, yg_ref, wt_ref, g2_ref, shf_ref, scf_ref, gf_ref, *rest, vec_row0=None):
    o_ref = rest[-1]
    if vec_row0 is not None:
        row = pl.ds((vec_row0 + pl.program_id(0)) % SUBLANES, 1)
        g2_ref, shf_ref, scf_ref = g2_ref.at[row, :], shf_ref.at[row, :], scf_ref.at[row, :]
    wt = wt_ref[...].T[:xmid_ref.shape[0]]
    routed = wt[:, 0:1] * _unpack_bf16_pairs(yg_ref[0])
    for k in range(1, yg_ref.shape[0]):
        routed = routed + wt[:, k:k + 1] * _unpack_bf16_pairs(yg_ref[k])
    x2 = xmid_ref[...] + g2_ref[...] * routed
    o_ref[...] = _rms_norm(x2, gf_ref[...]) * (1.0 + scf_ref[...]) + shf_ref[...]


def _final_prompt(xmid, yg, wts_t, mod, modf, mod_row0, g_final, b0, n_batch, out_prev):
    b, seq, d = xmid.shape
    k_top, _, w = yg.shape
    tl = FINAL_TILE
    nl = seq // tl

    def vec(col):
        return pl.BlockSpec((SUBLANES, d), lambda i, j: ((i + b0 + mod_row0) // SUBLANES, col))

    in_specs = [pl.BlockSpec((None, tl, d), lambda i, j: (i, j, 0)),
                pl.BlockSpec((k_top, tl, w), lambda i, j: (0, i * nl + j, 0)),
                pl.BlockSpec((k_top, tl), lambda i, j: (0, i * nl + j)),
                vec(5), vec(0), vec(1), pl.BlockSpec((1, d), lambda i, j: (0, 0))]
    args = [xmid, yg, wts_t, mod, modf, modf, g_final]
    aliases = {}
    if out_prev is not None:
        in_specs.append(pl.BlockSpec(memory_space=pl.ANY))
        args.append(out_prev)
        aliases = {7: 0}
    return pl.pallas_call(
        functools.partial(_final_kernel, vec_row0=b0 + mod_row0),
        grid=(b, nl),
        in_specs=in_specs,
        out_specs=pl.BlockSpec((None, tl, d), lambda i, j: (i + b0, j, 0)),
        out_shape=jax.ShapeDtypeStruct((n_batch, seq, d), F32),
        compiler_params=pltpu.CompilerParams(dimension_semantics=("arbitrary", "arbitrary"),
                                             vmem_limit_bytes=VMEM_LIMIT),
        input_output_aliases=aliases,
        name="final_prompt",
    )(*args)


def _final_sample(xmid_all, yg, wts_t, mod, modf, g_final, part, rows, first_row, after):
    d = xmid_all.shape[1]
    k_top, _, w = yg.shape

    def vec(col):
        return pl.BlockSpec((rows, d), lambda i: (part, col))

    return pl.pallas_call(
        _final_kernel,
        grid=(1,),
        in_specs=[vec(0),
                  pl.BlockSpec((k_top, rows, w), lambda i: (0, first_row // rows, 0)),
                  pl.BlockSpec((k_top, LANES), lambda i: (0, first_row // LANES)),
                  vec(5), vec(0), vec(1), pl.BlockSpec((1, d), lambda i: (0, 0)), pl.BlockSpec(memory_space=pl.ANY)],
        out_specs=pl.BlockSpec((rows, d), lambda i: (0, 0)),
        out_shape=jax.ShapeDtypeStruct((rows, d), F32),
        compiler_params=pltpu.CompilerParams(dimension_semantics=("arbitrary",),
                                             vmem_limit_bytes=VMEM_LIMIT),
        name="final_sample",
    )(xmid_all, yg, wts_t, mod, modf, modf, g_final, after)


def _block_diag_pairs(w_pool):
    g, c, _ = w_pool.shape
    eye = jnp.eye(g, dtype=w_pool.dtype)
    return (eye[:, None, :, None] * w_pool[:, :, None, :]).reshape(g * c, g * c)


def kernel(x_prompt, x_sample, state_pool, state_conv, c_prompt, c_sample, w_ada, b_ada, g_mix, w_in, w_pool, pool_scale, w_dw, b_dw, ln_g, ln_b, w_out, g_ffn, w_router, b_router, w_gate, w_up, w_down, ws_gate, ws_up, ws_down, w_ada_final, b_ada_final, g_final):
    bp, seq, d = x_prompt.shape
    bs = x_sample.shape[0]
    depth = w_ada.shape[0]
    assert depth == 1 and x_sample.shape[1] == 1
    conv_k = w_dw.shape[1]
    ne = w_router.shape[-1]
    per = ne // N_EXPERT_GROUPS

    row_expert = (jnp.arange(ne) % N_EXPERT_GROUPS) * per + jnp.arange(ne) // N_EXPERT_GROUPS
    wl = {
        "g_mix": g_mix[0][None, :],
        "in": w_in[0].astype(BF16),
        "pool": _block_diag_pairs(w_pool[0]).astype(BF16),
        "pool_scale": pool_scale[0][None, :],
        "dw": w_dw[0],
        "b_dw": b_dw[0][None, :],
        "ln_g": ln_g[0][None, :],
        "ln_b": ln_b[0][None, :],
        "out": w_out[0].astype(BF16),
        "g_ffn": g_ffn[0][None, :],
        "router_t": w_router[0].T[row_expert].astype(BF16),
        "s_gu": jnp.concatenate([ws_gate[0], ws_up[0]], axis=1).astype(BF16),
        "s_down": ws_down[0].astype(BF16),
    }
    b_perm = b_router[0][row_expert][:, None]

    assert bp % N_PARTS == 0 and bs % N_PARTS == 0
    bpp, bsp = bp // N_PARTS, bs // N_PARTS
    tpp = bpp * seq
    t_part = tpp + bsp
    grain = SC_WORKERS * SC_DISPATCH_CHUNK * SC_COMBINE_CHUNK // math.gcd(SC_DISPATCH_CHUNK, SC_COMBINE_CHUNK)
    t_pad = (t_part + grain - 1) // grain * grain
    assert t_pad % ROUTE_TILE == 0 and t_pad % DEST_TILE == 0 and tpp % bsp == 0

    c_all = jnp.concatenate([c_sample, c_prompt], axis=0)
    mod = _ada(c_all, w_ada[0], b_ada[0])
    modf = _ada(c_all, w_ada_final, b_ada_final)

    xmid_s, new_pool_t, new_conv_t, tails = _sample_mixer(
        x_sample.reshape(bs, d), mod, jnp.transpose(state_pool[0], (1, 0, 2)),
        jnp.transpose(state_conv[0], (1, 0, 2)), wl, tpp, t_pad, conv_k=conv_k)
    new_pool_s = jnp.transpose(new_pool_t, (1, 0, 2))
    new_conv_s = jnp.transpose(new_conv_t, (1, 0, 2))
    w_experts = tuple(a.reshape(a.shape[1:]) for a in (w_gate, w_up, w_down))

    n_blocks = (t_part * TOP_K + EXPERT_ROWS - 1) // EXPERT_ROWS + ne
    n_spare = -(-(t_pad - t_part) * TOP_K // EXPERT_ROWS)
    n_rows = (n_blocks + n_spare) * EXPERT_ROWS

    y_prompt, y_samples, npools, nconvs, w_bf = None, [], [], [], None
    mixed, dest = [], None
    for p in range(N_PARTS):
        h2, lgt = tails[p]
        xmid_p, h2, lgt, npool_p, nconv_p, *w_bf = _prompt_mixer(
            x_prompt, mod, bs, wl, h2, lgt, p * bpp, bpp, w_experts, w_bf, dest, conv_k=conv_k)
        npools.append(npool_p)
        nconvs.append(nconv_p)
        idx, wts, rank, counts_perm = _route(lgt, b_perm, t_part)
        counts = jnp.zeros((ne,), I32).at[row_expert].set(counts_perm[:, 0].astype(I32))
        nblk = (counts + EXPERT_ROWS - 1) // EXPERT_ROWS
        first_block = jnp.concatenate([jnp.zeros((1,), I32), jnp.cumsum(nblk).astype(I32)])
        dest = _dest_rows(first_block[:ne] * EXPERT_ROWS, idx, rank, t_part, n_blocks * EXPERT_ROWS)
        blk = jnp.arange(n_blocks, dtype=I32)[:, None]
        inside = (blk >= first_block[None, :ne]) & (blk < first_block[None, 1:])
        rows_left = counts[None, :] - (blk - first_block[None, :ne]) * EXPERT_ROWS
        block_valid = jnp.sum(jnp.where(inside, jnp.minimum(rows_left, EXPERT_ROWS), 0), axis=1)
        mixed.append((xmid_p, h2, wts, dest, first_block, block_valid))
    for p in range(N_PARTS):
        xmid_p, h2, wts, dest, first_block, block_valid = mixed[p]

        def chunked(ch):
            return dest.reshape(TOP_K, t_pad // ch, ch).transpose(1, 0, 2)

        xs = _sc_dispatch(h2, chunked(SC_DISPATCH_CHUNK), n_rows)
        y = _experts(xs, w_bf[0], w_bf[1], first_block, block_valid)
        yg = _sc_combine_gather(y, chunked(SC_COMBINE_CHUNK), t_pad)

        y_samples.append(_final_sample(xmid_s, yg, wts, mod, modf, g_final[None, :], p, bsp, tpp,
                                       g_final if y_prompt is None else y_prompt))
        y_prompt = _final_prompt(xmid_p, yg, wts, mod, modf, bs, g_final[None, :], p * bpp, bp, y_prompt)
    y_sample = jnp.concatenate(y_samples, axis=0)
    npool_p = jnp.concatenate(npools, axis=0)
    nconv_p = jnp.concatenate(nconvs, axis=0)

    return (y_prompt, y_sample[:, None, :], npool_p[None], nconv_p[None], new_pool_s[None], new_conv_s[None])
```

```python
import functools
import math

import jax
import jax.numpy as jnp
from jax import lax
from jax.experimental import pallas as pl
from jax.experimental.pallas import tpu as pltpu
from jax.experimental.pallas import tpu_sc as plsc

POOL_WINDOWS = (2, 4, 8, 16)
N_EXPERT_GROUPS = 8
TOPK_GROUPS = 4
TOP_K = 8
ROUTED_SCALE = 2.5
EPS = 1e-6

LANES = 128
SUBLANES = 8
VMEM_LIMIT = 52 * 1024 * 1024

ADA_TILE = 1024
SEQ_TILE = 512
ROW_CHUNK = 16
ROUTE_TILE = 1408
EXPERT_ROWS = 512
N_PARTS = 2
EXPERT_BUFFERS = 6
EXPERT_PIECES = 4
EXPERTS_PER_STEP = 2
FINAL_TILE = 512
DEST_TILE = 2816

SC_CORES = 2
SC_SUBCORES = 16
SC_WORKERS = SC_CORES * SC_SUBCORES
SC_DISPATCH_CHUNK = 88
SC_COMBINE_CHUNK = 24

F32 = jnp.float32
BF16 = jnp.bfloat16
I32 = jnp.int32
U32 = jnp.uint32
NEG_INF = float("-inf")
HI16 = 0xFFFF0000


def _sigmoid(x):
    return 1.0 / (1.0 + jnp.exp(-x))


def _silu(x):
    return x * _sigmoid(x)


def _rms_norm(x, g):
    return x * lax.rsqrt(jnp.mean(x * x, axis=-1, keepdims=True) + EPS) * g


def _dot(a, b):
    return jnp.dot(a, b, preferred_element_type=F32)


def _pack_bf16_pairs(x):
    w = x.shape[1] // 2
    bits = lax.bitcast_convert_type(x.astype(BF16).astype(F32), U32)
    return lax.bitcast_convert_type((bits[:, :w] >> 16) | (bits[:, w:] & U32(HI16)), I32)


def _unpack_bf16_pairs(p):
    bits = lax.bitcast_convert_type(p, U32)
    lo = lax.bitcast_convert_type(bits << 16, F32)
    hi = lax.bitcast_convert_type(bits & U32(HI16), F32)
    return jnp.concatenate([lo, hi], axis=1)


def _ada_kernel(c_ref, w_ref, b_ref, o_ref):
    s = _silu(c_ref[...])
    o_ref[...] = _dot(s.astype(BF16), w_ref[...].astype(BF16)) + b_ref[...]


def _ada(c, w, b):
    rows, d = c.shape
    n = w.shape[1]
    tn = ADA_TILE
    return pl.pallas_call(
        _ada_kernel,
        grid=(n // tn,),
        in_specs=[pl.BlockSpec((rows, d), lambda j: (0, 0)),
                  pl.BlockSpec((d, tn), lambda j: (0, j)),
                  pl.BlockSpec((1, tn), lambda j: (0, j))],
        out_specs=pl.BlockSpec((rows, tn), lambda j: (0, j)),
        out_shape=jax.ShapeDtypeStruct((rows, n), F32),
        compiler_params=pltpu.CompilerParams(dimension_semantics=("arbitrary",),
                                             vmem_limit_bytes=VMEM_LIMIT),
        name="ada",
    )(c, w, b.reshape(1, n))


def _mixer_tail(x, pool_d, conv_act, mod, w, xmid_ref, h2_ref, lgt_ref):
    sh2, sc2, g1, g2 = mod
    pw = pool_d.shape[1]
    pool_out = _dot(pool_d, w["pool"][...]) * w["pool_scale"][...]
    mix = _dot(pool_out.astype(BF16), w["out"][:pw, :]) + _dot(conv_act, w["out"][pw:, :])
    x1 = x + g1 * mix
    h2f = _rms_norm(x1, w["g_ffn"][...]) * (1.0 + sc2) + sh2
    h2_ref[...] = _pack_bf16_pairs(h2f)
    h2 = h2f.astype(BF16)
    lgt_ref[...] = lax.dot_general(w["router_t"][...], h2, (((1,), (1,)), ((), ())),
                                   preferred_element_type=F32)
    gu = _dot(h2, w["s_gu"][...])
    de = gu.shape[1] // 2
    hs = _silu(gu[:, :de]) * gu[:, de:]
    shared = _dot(hs.astype(BF16), w["s_down"][...])
    xmid_ref[...] = x1 + g2 * shared


_W_NAMES = ("g_mix", "in", "pool", "pool_scale", "dw", "b_dw", "ln_g", "ln_b", "out", "g_ffn",
            "router_t", "s_gu", "s_down")


def _layer_norm_silu(yc, g, b):
    mu = jnp.mean(yc, axis=-1, keepdims=True)
    yz = yc - mu
    var = jnp.mean(yz * yz, axis=-1, keepdims=True)
    return _silu(yz * lax.rsqrt(var + EPS) * g + b)


U_HALO = 32
A_HALO = 32


def _prompt_mixer_kernel(x_ref, mod_ref, *refs, tl, d, pw, cw, conv_k, mod_row0):
    nw = len(_W_NAMES)
    w = dict(zip(_W_NAMES, refs[:nw]))
    wg_ref, wu_ref, wd_ref = refs[nw + 2:nw + 5]
    xmid_ref, h2_ref, lgt_ref, npool_ref, nconv_ref, wgu_bf_ref, wd_bf_ref = refs[-16:-9]
    ubuf, s2buf, s4buf, s8buf, abuf, ashift, dbuf, cbuf, rowb = refs[-9:]
    de = wg_ref.shape[2]
    wgu_bf_ref[:, :, :de] = wg_ref[...].astype(BF16)
    wgu_bf_ref[:, :, de:] = wu_ref[...].astype(BF16)
    wd_bf_ref[...] = wd_ref[...].astype(BF16)
    l = pl.program_id(1)
    pool_buf = max(POOL_WINDOWS) - 1
    uh, ah = U_HALO, A_HALO
    pg = pw // len(POOL_WINDOWS)
    nu, na = uh + tl, ah + tl

    @pl.when(l == 0)
    def _():
        ubuf[0:uh, :] = jnp.zeros((uh, pw), F32)
        abuf[0:ah, :] = jnp.zeros((ah, cw), F32)

    @pl.when(l > 0)
    def _():
        ubuf[0:uh, :] = ubuf[tl:tl + uh, :]
        abuf[0:ah, :] = abuf[tl:tl + ah, :]

    x = x_ref[0]
    row = pl.ds((mod_row0 + pl.program_id(0)) % SUBLANES, 1)
    sh1, sc1, g1, sh2, sc2, g2 = [mod_ref[row, i * d:(i + 1) * d] for i in range(6)]

    h = _rms_norm(x, w["g_mix"][...]) * (1.0 + sc1) + sh1
    z = _dot(h.astype(BF16), w["in"][...])
    u = z[:, :pw]
    ubuf[uh:nu, :] = u
    abuf[ah:na, :] = z[:, pw:pw + cw] * _sigmoid(z[:, pw + cw:])

    s2buf[8:nu, :] = ubuf[8:nu, :] + ubuf[7:nu - 1, :]
    s4buf[16:nu, :] = s2buf[16:nu, pg:] + s2buf[14:nu - 2, pg:]
    s8buf[24:nu, :] = s4buf[24:nu, pg:] + s4buf[20:nu - 4, pg:]
    t = l * tl + lax.broadcasted_iota(I32, (tl, 1), 0)
    sums = (s2buf[uh:nu, 0:pg], s4buf[uh:nu, 0:pg], s8buf[uh:nu, 0:pg],
            s8buf[uh:nu, pg:2 * pg] + s8buf[uh - 8:nu - 8, pg:2 * pg])
    for g, win in enumerate(POOL_WINDOWS):
        cols = slice(g * pg, (g + 1) * pg)
        inv = 1.0 / jnp.minimum(win, t + 1).astype(F32)
        dbuf[:, cols] = (sums[g] * inv - u[:, cols]).astype(BF16)

    for r in range(1, SUBLANES):
        ashift[r - 1, 8:na, :] = abuf[8 - r:na - r, :]

    for k in range(conv_k):
        rowb[k] = jnp.broadcast_to(w["dw"][k:k + 1, :], (SUBLANES, cw))
    rowb[conv_k] = jnp.broadcast_to(w["b_dw"][...], (SUBLANES, cw))

    for c in range(tl // ROW_CHUNK):
        groups = []
        for o in range(ah + c * ROW_CHUNK, ah + (c + 1) * ROW_CHUNK, SUBLANES):
            yc = abuf[o:o + SUBLANES, :] * rowb[conv_k - 1]
            for back in range(1, conv_k):
                q, r = divmod(back, SUBLANES)
                src = abuf if r == 0 else ashift.at[r - 1]
                yc = yc + src[o - q * SUBLANES:o - (q - 1) * SUBLANES, :] * rowb[conv_k - 1 - back]
            groups.append(yc + rowb[conv_k])
        r0 = c * ROW_CHUNK
        cbuf[r0:r0 + ROW_CHUNK, :] = _layer_norm_silu(
            jnp.concatenate(groups, axis=0), w["ln_g"][...], w["ln_b"][...]).astype(BF16)

    @pl.when(l == pl.num_programs(1) - 1)
    def _():
        npool_ref[0] = ubuf[nu - pool_buf:nu, :]
        nconv_ref[0] = abuf[na - (conv_k - 1):na, :]

    _mixer_tail(x, dbuf[...], cbuf[...], (sh2, sc2, g1, g2), w, xmid_ref.at[0], h2_ref, lgt_ref)


def _sample_mixer_kernel(x_ref, mod_ref, sp_ref, sc_ref, *refs, d, pw, cw, conv_k):
    nw = len(_W_NAMES)
    w = dict(zip(_W_NAMES, refs[:nw]))
    xmid_ref, npool_ref, nconv_ref = refs[nw:nw + 3]
    tails = refs[nw + 3:-2]
    h2_tmp, lgt_tmp = refs[-2:]
    pool_buf = max(POOL_WINDOWS) - 1
    pg = pw // len(POOL_WINDOWS)
    x = x_ref[...]
    sh1, sc1, g1, sh2, sc2, g2 = [mod_ref[:, i * d:(i + 1) * d] for i in range(6)]
    h = _rms_norm(x, w["g_mix"][...]) * (1.0 + sc1) + sh1
    z = _dot(h.astype(BF16), w["in"][...])
    u = z[:, :pw]
    a = z[:, pw:pw + cw] * _sigmoid(z[:, pw + cw:])
    npool_ref[0:pool_buf - 1] = sp_ref[1:pool_buf]
    npool_ref[pool_buf - 1] = u
    nconv_ref[0:conv_k - 2] = sc_ref[1:conv_k - 1]
    nconv_ref[conv_k - 2] = a
    ds = []
    for g, win in enumerate(POOL_WINDOWS):
        cols = slice(g * pg, (g + 1) * pg)
        acc = u[:, cols]
        for j in range(1, win):
            acc = acc + sp_ref[pool_buf - j, :, cols]
        cnt = float(min(win, 1 + pool_buf))
        ds.append(acc / cnt - u[:, cols])
    pool_d = jnp.concatenate(ds, axis=-1).astype(BF16)
    yc = a * w["dw"][conv_k - 1:conv_k, :]
    for k in range(conv_k - 1):
        yc = yc + sc_ref[k] * w["dw"][k:k + 1, :]
    yc = yc + w["b_dw"][...]
    conv_act = _layer_norm_silu(yc, w["ln_g"][...], w["ln_b"][...]).astype(BF16)
    _mixer_tail(x, pool_d, conv_act, (sh2, sc2, g1, g2), w, xmid_ref, h2_tmp, lgt_tmp)
    n_parts = len(tails) // 2
    share = x.shape[0] // n_parts
    for p in range(n_parts):
        h2_ref, lgt_ref = tails[2 * p], tails[2 * p + 1]
        h2_ref[...] = jnp.zeros_like(h2_ref)
        lgt_ref[...] = jnp.zeros_like(lgt_ref)
        h2_ref[0:share, :] = h2_tmp[p * share:(p + 1) * share, :]
        lgt_ref[:, 0:share] = lgt_tmp[:, p * share:(p + 1) * share]


def _full_spec(a):
    nd = a.ndim
    return pl.BlockSpec(a.shape, lambda *_: (0,) * nd)


def _prompt_mixer(x, mod, mod_row0, wl, h2_all, lgt_all, b0, b, w_experts, w_bf_prev, after, *, conv_k):
    n_batch, seq, d = x.shape
    w_gate, w_up, w_down = w_experts
    n_exp, _, de = w_gate.shape
    pw = wl["pool"].shape[0]
    cw = wl["dw"].shape[1]
    ne = wl["router_t"].shape[0]
    tl = SEQ_TILE
    nl = seq // tl
    pg = pw // len(POOL_WINDOWS)
    assert POOL_WINDOWS == (2, 4, 8, 16) and conv_k - 1 <= A_HALO
    ws = [wl[n] for n in _W_NAMES]
    kern = functools.partial(_prompt_mixer_kernel, tl=tl, d=d, pw=pw, cw=cw, conv_k=conv_k, mod_row0=mod_row0 + b0)
    ecs = n_exp // (n_batch * nl)
    assert ecs * n_batch * nl == n_exp

    def w_map(i, j):
        return ((i + b0) * nl + j, 0, 0)

    ins = [x, mod, *ws, h2_all, lgt_all, w_gate, w_up, w_down]
    in_specs = ([pl.BlockSpec((1, tl, d), lambda i, j: (i + b0, j, 0)),
                 pl.BlockSpec((SUBLANES, mod.shape[-1]), lambda i, j: ((i + b0 + mod_row0) // SUBLANES, 0))]
                + [_full_spec(a) for a in ws]
                + [pl.BlockSpec(memory_space=pl.ANY), pl.BlockSpec(memory_space=pl.ANY),
                   pl.BlockSpec((ecs, d, de), w_map), pl.BlockSpec((ecs, d, de), w_map),
                   pl.BlockSpec((ecs, de, d), w_map)])
    aliases = {2 + len(ws): 1, 3 + len(ws): 2}
    if w_bf_prev is not None:
        aliases.update({len(ins): 5, len(ins) + 1: 6})
        ins += list(w_bf_prev)
        in_specs += [pl.BlockSpec(memory_space=pl.ANY), pl.BlockSpec(memory_space=pl.ANY)]
    if after is not None:
        ins.append(after)
        in_specs.append(pl.BlockSpec(memory_space=pl.ANY))
    return pl.pallas_call(
        kern,
        grid=(b, nl),
        in_specs=in_specs,
        out_specs=[pl.BlockSpec((1, tl, d), lambda i, j: (i, j, 0)),
                   pl.BlockSpec((tl, d // 2), lambda i, j: (i * nl + j, 0)),
                   pl.BlockSpec((ne, tl), lambda i, j: (0, i * nl + j)),
                   pl.BlockSpec((1, max(POOL_WINDOWS) - 1, pw), lambda i, j: (i, 0, 0)),
                   pl.BlockSpec((1, conv_k - 1, cw), lambda i, j: (i, 0, 0)),
                   pl.BlockSpec((ecs, d, 2 * de), w_map), pl.BlockSpec((ecs, de, d), w_map)],
        out_shape=[jax.ShapeDtypeStruct((b, seq, d), F32),
                   jax.ShapeDtypeStruct(h2_all.shape, I32),
                   jax.ShapeDtypeStruct(lgt_all.shape, F32),
                   jax.ShapeDtypeStruct((b, max(POOL_WINDOWS) - 1, pw), F32),
                   jax.ShapeDtypeStruct((b, conv_k - 1, cw), F32),
                   jax.ShapeDtypeStruct((n_exp, d, 2 * de), BF16),
                   jax.ShapeDtypeStruct((n_exp, de, d), BF16)],
        scratch_shapes=[pltpu.VMEM((U_HALO + tl, pw), F32), pltpu.VMEM((U_HALO + tl, pw), F32),
                        pltpu.VMEM((U_HALO + tl, pw - pg), F32), pltpu.VMEM((U_HALO + tl, pw - 2 * pg), F32),
                        pltpu.VMEM((A_HALO + tl, cw), F32), pltpu.VMEM((SUBLANES - 1, A_HALO + tl, cw), F32),
                        pltpu.VMEM((tl, pw), BF16), pltpu.VMEM((tl, cw), BF16),
                        pltpu.VMEM((conv_k + 1, SUBLANES, cw), F32)],
        compiler_params=pltpu.CompilerParams(dimension_semantics=("arbitrary", "arbitrary"),
                                             vmem_limit_bytes=VMEM_LIMIT),
        input_output_aliases=aliases,
        name="prompt_mixer",
    )(*ins)


def _sample_mixer(x, mod, sp, sc, wl, t_prompt, t_pad, *, conv_k):
    rows, d = x.shape
    tail = t_pad - t_prompt
    assert t_prompt % tail == 0 and rows % N_PARTS == 0 and rows // N_PARTS <= tail
    pw = wl["pool"].shape[0]
    cw = wl["dw"].shape[1]
    ne = wl["router_t"].shape[0]
    ws = [wl[n] for n in _W_NAMES]
    kern = functools.partial(_sample_mixer_kernel, d=d, pw=pw, cw=cw, conv_k=conv_k)
    ins = [x, mod, sp, sc] + ws
    tail_specs = [pl.BlockSpec((tail, d // 2), lambda i: (t_prompt // tail, 0)),
                  pl.BlockSpec((ne, tail), lambda i: (0, t_prompt // tail))] * N_PARTS
    tail_shapes = [jax.ShapeDtypeStruct((t_pad, d // 2), I32), jax.ShapeDtypeStruct((ne, t_pad), F32)] * N_PARTS
    outs = pl.pallas_call(
        kern,
        grid=(1,),
        in_specs=[_full_spec(x), pl.BlockSpec((rows, mod.shape[1]), lambda i: (0, 0))]
                 + [_full_spec(a) for a in ins[2:]],
        out_specs=[pl.BlockSpec((rows, d), lambda i: (0, 0)), _full_spec(sp), _full_spec(sc)] + tail_specs,
        out_shape=[jax.ShapeDtypeStruct((rows, d), F32),
                   jax.ShapeDtypeStruct(sp.shape, F32),
                   jax.ShapeDtypeStruct(sc.shape, F32)] + tail_shapes,
        scratch_shapes=[pltpu.VMEM((rows, d // 2), I32), pltpu.VMEM((ne, rows), F32)],
        compiler_params=pltpu.CompilerParams(dimension_semantics=("arbitrary",),
                                             vmem_limit_bytes=VMEM_LIMIT),
        name="sample_mixer",
    )(*ins)
    return outs[0], outs[1], outs[2], [(outs[3 + 2 * p], outs[4 + 2 * p]) for p in range(N_PARTS)]


def _sublane_max(x):
    return jnp.max(x, axis=0, keepdims=True)


def _route_kernel(lg_ref, b_ref, tri_ref, idx_ref, wt_ref, rank_ref, cnt_ref, carry, *, t_valid):
    i = pl.program_id(0)
    ne, tt = lg_ref.shape
    ng = N_EXPERT_GROUPS
    per = ne // ng

    @pl.when(i == 0)
    def _():
        carry[...] = jnp.zeros_like(carry)

    s = _sigmoid(lg_ref[...])
    sel = s + b_ref[...]
    s3 = [s[p * ng:(p + 1) * ng, :] for p in range(per)]
    sel3 = [sel[p * ng:(p + 1) * ng, :] for p in range(per)]
    m1 = sel3[0]
    m2 = jnp.full_like(m1, NEG_INF)
    for p in range(1, per):
        m2 = jnp.maximum(m2, jnp.minimum(m1, sel3[p]))
        m1 = jnp.maximum(m1, sel3[p])
    gs = m1 + m2
    gi = lax.broadcasted_iota(jnp.int32, (ng, tt), 0)
    beaten = jnp.zeros((ng, tt), jnp.int32)
    for g in range(ng):
        row = gs[g:g + 1, :]
        beats = (row > gs) | ((row == gs) & (gi > g))
        beaten = beaten + beats.astype(jnp.int32)
    keep = beaten < TOPK_GROUPS
    cur = [jnp.where(keep, sel3[p], NEG_INF) for p in range(per)]
    eid = [(gi * per + p).astype(F32) for p in range(per)]
    idxs, wts, hits = [], [], []
    for _ in range(TOP_K):
        m = cur[0]
        for p in range(1, per):
            m = jnp.maximum(m, cur[p])
        m = _sublane_max(m)
        cand = jnp.where(cur[0] == m, eid[0], float(ne))
        for p in range(1, per):
            cand = jnp.minimum(cand, jnp.where(cur[p] == m, eid[p], float(ne)))
        e_sel = jnp.min(cand, axis=0, keepdims=True)
        hit = [eid[p] == e_sel for p in range(per)]
        wk = jnp.where(hit[0], s3[0], 0.0)
        for p in range(1, per):
            wk = wk + jnp.where(hit[p], s3[p], 0.0)
        wts.append(jnp.sum(wk, axis=0, keepdims=True))
        cur = [jnp.where(hit[p], NEG_INF, cur[p]) for p in range(per)]
        idxs.append(e_sel)
        hits.append(hit)
    wsum = wts[0]
    for k in range(1, TOP_K):
        wsum = wsum + wts[k]
    chosen = [functools.reduce(lambda a, b: a | b, [hits[k][p] for k in range(TOP_K)]) for p in range(per)]
    real = (i * tt + lax.broadcasted_iota(I32, (1, tt), 1)) < t_valid
    onehot = jnp.concatenate([(c & real).astype(F32) for c in chosen], axis=0)
    before = _dot(onehot.astype(BF16), tri_ref[...]) + carry[...]
    for k in range(TOP_K):
        rk = jnp.where(hits[k][0], before[0:ng, :], 0.0)
        for p in range(1, per):
            rk = rk + jnp.where(hits[k][p], before[p * ng:(p + 1) * ng, :], 0.0)
        rank_ref[k:k + 1, :] = jnp.sum(rk, axis=0, keepdims=True).astype(jnp.int32)
        idx_ref[k:k + 1, :] = idxs[k].astype(jnp.int32)
        wt_ref[k:k + 1, :] = wts[k] / wsum * ROUTED_SCALE
    carry[...] = carry[...] + jnp.sum(onehot, axis=1, keepdims=True)
    cnt_ref[...] = carry[...]


def _route(lgt, b_perm, t_valid):
    ne, t = lgt.shape
    tt = ROUTE_TILE
    tri = (jnp.arange(tt)[:, None] < jnp.arange(tt)[None, :]).astype(BF16)
    return pl.pallas_call(
        functools.partial(_route_kernel, t_valid=t_valid),
        grid=(t // tt,),
        in_specs=[pl.BlockSpec((ne, tt), lambda i: (0, i)),
                  pl.BlockSpec((ne, 1), lambda i: (0, 0)),
                  pl.BlockSpec((tt, tt), lambda i: (0, 0))],
        out_specs=[pl.BlockSpec((TOP_K, tt), lambda i: (0, i)),
                   pl.BlockSpec((TOP_K, tt), lambda i: (0, i)),
                   pl.BlockSpec((TOP_K, tt), lambda i: (0, i)),
                   pl.BlockSpec((ne, 1), lambda i: (0, 0))],
        out_shape=[jax.ShapeDtypeStruct((TOP_K, t), jnp.int32),
                   jax.ShapeDtypeStruct((TOP_K, t), F32),
                   jax.ShapeDtypeStruct((TOP_K, t), jnp.int32),
                   jax.ShapeDtypeStruct((ne, 1), F32)],
        scratch_shapes=[pltpu.VMEM((ne, 1), F32)],
        compiler_params=pltpu.CompilerParams(dimension_semantics=("arbitrary",),
                                             vmem_limit_bytes=VMEM_LIMIT),
        name="route",
    )(lgt, b_perm, tri)


def _dest_kernel(pstart_ref, idx_ref, rank_ref, dest_ref, *, t_valid, spare_row):
    idx = idx_ref[...]
    k_top, tile = idx.shape

    def add_start(e, acc):
        return acc + jnp.where(idx == e, pstart_ref[e], 0)

    dest = lax.fori_loop(0, pstart_ref.shape[0], add_start, rank_ref[...])
    tok = pl.program_id(0) * tile + lax.broadcasted_iota(I32, idx.shape, 1)
    spare = spare_row + (tok - t_valid) * k_top + lax.broadcasted_iota(I32, idx.shape, 0)
    dest_ref[...] = jnp.where(tok < t_valid, dest, spare)


def _dest_rows(pstart, idx, rank, t_valid, spare_row):
    k_top, t = idx.shape
    tile = DEST_TILE
    spec = pl.BlockSpec((k_top, tile), lambda i, ps: (0, i))
    return pl.pallas_call(
        functools.partial(_dest_kernel, t_valid=t_valid, spare_row=spare_row),
        grid_spec=pltpu.PrefetchScalarGridSpec(num_scalar_prefetch=1, grid=(t // tile,),
                                               in_specs=[spec, spec], out_specs=spec),
        out_shape=jax.ShapeDtypeStruct((k_top, t), I32),
        compiler_params=pltpu.CompilerParams(dimension_semantics=("arbitrary",),
                                             vmem_limit_bytes=VMEM_LIMIT),
        name="dest_rows",
    )(pstart, idx, rank)


def _expert_kernel(first_ref, valid_ref, xs_hbm, wgu_ref, wd_ref, y_hbm, xbuf, ybuf, in_sem, out_sem):
    step = pl.program_id(0)
    ne = first_ref.shape[0] - 1
    nbuf, rows = xbuf.shape[:2]
    piece = rows // EXPERT_PIECES
    per_step, de = wd_ref.shape[:2]
    n_total = first_ref[ne]

    def in_copy(g, h):
        return pltpu.make_async_copy(xs_hbm.at[pl.ds(g * rows + h * piece, piece)],
                                     xbuf.at[g % nbuf, pl.ds(h * piece, piece)], in_sem.at[g % nbuf])

    def out_copy(g, h):
        return pltpu.make_async_copy(ybuf.at[g % nbuf, pl.ds(h * piece, piece)],
                                     y_hbm.at[pl.ds(g * rows + h * piece, piece)], out_sem.at[g % nbuf])

    def used_pieces(copy, g, act):
        getattr(copy(g, 0), act)()
        for h in range(1, EXPERT_PIECES):
            @pl.when(valid_ref[g] > h * piece)
            def _():
                getattr(copy(g, h), act)()

    @pl.when(step == 0)
    def _():
        for g in range(nbuf - 1):
            @pl.when(g < n_total)
            def _():
                used_pieces(in_copy, g, "start")

    def block(g, carry, *, j):
        slot = g % nbuf

        @pl.when(g + nbuf - 1 < n_total)
        def _():
            used_pieces(in_copy, g + nbuf - 1, "start")

        used_pieces(in_copy, g, "wait")

        @pl.when(g >= nbuf)
        def _():
            used_pieces(out_copy, g - nbuf, "wait")

        valid = valid_ref[g]

        def run(n):
            rid = lax.broadcasted_iota(I32, (n, 1), 0)
            xs = jnp.where(rid < valid, _unpack_bf16_pairs(xbuf[slot, 0:n]), 0.0).astype(BF16)
            gu = _dot(xs, wgu_ref[j])
            hb = _silu(gu[:, :de]) * gu[:, de:]
            ybuf[slot, 0:n] = _pack_bf16_pairs(_dot(hb.astype(BF16), wd_ref[j]))

        for h in range(EXPERT_PIECES):
            @pl.when((valid > h * piece) & (valid <= (h + 1) * piece))
            def _():
                run((h + 1) * piece)

        used_pieces(out_copy, g, "start")
        return carry

    for j in range(per_step):
        e = step * per_step + j
        lax.fori_loop(first_ref[e], first_ref[e + 1], functools.partial(block, j=j), 0)

    @pl.when(step == pl.num_programs(0) - 1)
    def _():
        for back in range(nbuf, 0, -1):
            @pl.when(n_total >= back)
            def _():
                used_pieces(out_copy, n_total - back, "wait")


def _experts(xs, w_gu, w_down, first_block, block_valid):
    p, dw = xs.shape
    ne, de, d = w_down.shape
    return pl.pallas_call(
        _expert_kernel,
        grid_spec=pltpu.PrefetchScalarGridSpec(
            num_scalar_prefetch=2,
            grid=(ne // EXPERTS_PER_STEP,),
            in_specs=[pl.BlockSpec(memory_space=pl.ANY),
                      pl.BlockSpec((EXPERTS_PER_STEP, d, 2 * de), lambda e, fb, cn: (e, 0, 0)),
                      pl.BlockSpec((EXPERTS_PER_STEP, de, d), lambda e, fb, cn: (e, 0, 0))],
            out_specs=pl.BlockSpec(memory_space=pl.ANY),
            scratch_shapes=[pltpu.VMEM((EXPERT_BUFFERS, EXPERT_ROWS, dw), I32),
                            pltpu.VMEM((EXPERT_BUFFERS, EXPERT_ROWS, dw), I32),
                            pltpu.SemaphoreType.DMA((EXPERT_BUFFERS,)), pltpu.SemaphoreType.DMA((EXPERT_BUFFERS,))]),
        out_shape=jax.ShapeDtypeStruct((p, dw), I32),
        compiler_params=pltpu.CompilerParams(dimension_semantics=("arbitrary",),
                                             vmem_limit_bytes=VMEM_LIMIT),
        name="experts",
    )(first_block, block_valid, xs, w_gu, w_down)


def _sc_first_chunk(n_chunks):
    per_worker = n_chunks // SC_WORKERS
    assert per_worker * SC_WORKERS == n_chunks
    return per_worker, (lax.axis_index("s") * SC_CORES + lax.axis_index("c")) * per_worker


def _sc_start(copies):
    for cp in copies:
        cp.start()


def _sc_wait(copies):
    for cp in copies:
        cp.wait()


def _sc_dispatch(h2, dest, n_rows):
    w = h2.shape[1]
    n_chunks, k_top, ch = dest.shape
    mesh = plsc.VectorSubcoreMesh(core_axis_name="c", subcore_axis_name="s")

    @functools.partial(
        pl.kernel, mesh=mesh, out_type=jax.ShapeDtypeStruct((n_rows, w), I32),
        scratch_types=[pltpu.VMEM((2, k_top, ch), I32), pltpu.VMEM((2, ch, w), I32),
                       pltpu.SemaphoreType.DMA((2,)), pltpu.SemaphoreType.DMA((2,))])
    def dispatch(h2_hbm, dest_hbm, xs_hbm, idx_v, rows_v, load_sem, store_sem):
        per_worker, first = _sc_first_chunk(n_chunks)

        def loads(i):
            slot = i % 2
            return (pltpu.make_async_copy(h2_hbm.at[pl.ds((first + i) * ch, ch)], rows_v.at[slot], load_sem.at[slot]),
                    pltpu.make_async_copy(dest_hbm.at[first + i], idx_v.at[slot], load_sem.at[slot]))

        def scatters(i):
            slot = i % 2
            return [pltpu.make_async_copy(rows_v.at[slot], xs_hbm.at[idx_v.at[slot, k]], store_sem.at[slot])
                    for k in range(k_top)]

        for i in range(min(2, per_worker)):
            _sc_start(loads(i))
        for i in range(per_worker):
            _sc_wait(loads(i))
            _sc_start(scatters(i))
            if 1 <= i < per_worker - 1:
                _sc_wait(scatters(i - 1))
                _sc_start(loads(i + 1))
        for i in range(max(per_worker - 2, 0), per_worker):
            _sc_wait(scatters(i))

    return dispatch(h2, dest)


def _sc_combine_gather(y, dest, t):
    w = y.shape[1]
    n_chunks, k_top, ch = dest.shape
    n_local = n_chunks // SC_WORKERS
    mesh = plsc.VectorSubcoreMesh(core_axis_name="c", subcore_axis_name="s")

    @functools.partial(
        pl.kernel, mesh=mesh, out_type=jax.ShapeDtypeStruct((k_top, t, w), I32),
        scratch_types=[pltpu.VMEM((n_local, k_top, ch), I32), pltpu.VMEM((k_top, ch, w), I32),
                       pltpu.SemaphoreType.DMA((k_top,)), pltpu.SemaphoreType.DMA((k_top,))])
    def gather(y_hbm, dest_hbm, out_hbm, idx_v, rows_v, load_sem, store_sem):
        per_worker, first = _sc_first_chunk(n_chunks)
        pltpu.sync_copy(dest_hbm.at[pl.ds(first, per_worker)], idx_v)

        def fetch(i, k):
            return pltpu.make_async_copy(y_hbm.at[idx_v.at[i, k]], rows_v.at[k], load_sem.at[k])

        def put(i, k):
            return pltpu.make_async_copy(rows_v.at[k], out_hbm.at[k, pl.ds((first + i) * ch, ch)], store_sem.at[k])

        @pl.loop(0, per_worker)
        def _(i):
            for k in range(k_top):
                @pl.when(i > 0)
                def _():
                    put(i - 1, k).wait()

                fetch(i, k).start()
            for k in range(k_top):
                fetch(i, k).wait()
                put(i, k).start()

        for k in range(k_top):
            put(per_worker - 1, k).wait()

    return gather(y, dest)


def _final_kernel(xmid_ref, yg_ref, wt_ref, g2_ref, shf_ref, scf_ref, gf_ref, *rest, vec_row0=None):
    o_ref = rest[-1]
    if vec_row0 is not None:
        row = pl.ds((vec_row0 + pl.program_id(0)) % SUBLANES, 1)
        g2_ref, shf_ref, scf_ref = g2_ref.at[row, :], shf_ref.at[row, :], scf_ref.at[row, :]
    wt = wt_ref[...].T[:xmid_ref.shape[0]]
    routed = wt[:, 0:1] * _unpack_bf16_pairs(yg_ref[0])
    for k in range(1, yg_ref.shape[0]):
        routed = routed + wt[:, k:k + 1] * _unpack_bf16_pairs(yg_ref[k])
    x2 = xmid_ref[...] + g2_ref[...] * routed
    o_ref[...] = _rms_norm(x2, gf_ref[...]) * (1.0 + scf_ref[...]) + shf_ref[...]


def _final_prompt(xmid, yg, wts_t, mod, modf, mod_row0, g_final, b0, n_batch, out_prev):
    b, seq, d = xmid.shape
    k_top, _, w = yg.shape
    tl = FINAL_TILE
    nl = seq // tl

    def vec(col):
        return pl.BlockSpec((SUBLANES, d), lambda i, j: ((i + b0 + mod_row0) // SUBLANES, col))

    in_specs = [pl.BlockSpec((None, tl, d), lambda i, j: (i, j, 0)),
                pl.BlockSpec((k_top, tl, w), lambda i, j: (0, i * nl + j, 0)),
                pl.BlockSpec((k_top, tl), lambda i, j: (0, i * nl + j)),
                vec(5), vec(0), vec(1), pl.BlockSpec((1, d), lambda i, j: (0, 0))]
    args = [xmid, yg, wts_t, mod, modf, modf, g_final]
    aliases = {}
    if out_prev is not None:
        in_specs.append(pl.BlockSpec(memory_space=pl.ANY))
        args.append(out_prev)
        aliases = {7: 0}
    return pl.pallas_call(
        functools.partial(_final_kernel, vec_row0=b0 + mod_row0),
        grid=(b, nl),
        in_specs=in_specs,
        out_specs=pl.BlockSpec((None, tl, d), lambda i, j: (i + b0, j, 0)),
        out_shape=jax.ShapeDtypeStruct((n_batch, seq, d), F32),
        compiler_params=pltpu.CompilerParams(dimension_semantics=("arbitrary", "arbitrary"),
                                             vmem_limit_bytes=VMEM_LIMIT),
        input_output_aliases=aliases,
        name="final_prompt",
    )(*args)


def _final_sample(xmid_all, yg, wts_t, mod, modf, g_final, part, rows, first_row, after):
    d = xmid_all.shape[1]
    k_top, _, w = yg.shape

    def vec(col):
        return pl.BlockSpec((rows, d), lambda i: (part, col))

    return pl.pallas_call(
        _final_kernel,
        grid=(1,),
        in_specs=[vec(0),
                  pl.BlockSpec((k_top, rows, w), lambda i: (0, first_row // rows, 0)),
                  pl.BlockSpec((k_top, LANES), lambda i: (0, first_row // LANES)),
                  vec(5), vec(0), vec(1), pl.BlockSpec((1, d), lambda i: (0, 0)), pl.BlockSpec(memory_space=pl.ANY)],
        out_specs=pl.BlockSpec((rows, d), lambda i: (0, 0)),
        out_shape=jax.ShapeDtypeStruct((rows, d), F32),
        compiler_params=pltpu.CompilerParams(dimension_semantics=("arbitrary",),
                                             vmem_limit_bytes=VMEM_LIMIT),
        name="final_sample",
    )(xmid_all, yg, wts_t, mod, modf, modf, g_final, after)


def _block_diag_pairs(w_pool):
    g, c, _ = w_pool.shape
    eye = jnp.eye(g, dtype=w_pool.dtype)
    return (eye[:, None, :, None] * w_pool[:, :, None, :]).reshape(g * c, g * c)


def kernel(x_prompt, x_sample, state_pool, state_conv, c_prompt, c_sample, w_ada, b_ada, g_mix, w_in, w_pool, pool_scale, w_dw, b_dw, ln_g, ln_b, w_out, g_ffn, w_router, b_router, w_gate, w_up, w_down, ws_gate, ws_up, ws_down, w_ada_final, b_ada_final, g_final):
    bp, seq, d = x_prompt.shape
    bs = x_sample.shape[0]
    depth = w_ada.shape[0]
    assert depth == 1 and x_sample.shape[1] == 1
    conv_k = w_dw.shape[1]
    ne = w_router.shape[-1]
    per = ne // N_EXPERT_GROUPS

    row_expert = (jnp.arange(ne) % N_EXPERT_GROUPS) * per + jnp.arange(ne) // N_EXPERT_GROUPS
    wl = {
        "g_mix": g_mix[0][None, :],
        "in": w_in[0].astype(BF16),
        "pool": _block_diag_pairs(w_pool[0]).astype(BF16),
        "pool_scale": pool_scale[0][None, :],
        "dw": w_dw[0],
        "b_dw": b_dw[0][None, :],
        "ln_g": ln_g[0][None, :],
        "ln_b": ln_b[0][None, :],
        "out": w_out[0].astype(BF16),
        "g_ffn": g_ffn[0][None, :],
        "router_t": w_router[0].T[row_expert].astype(BF16),
        "s_gu": jnp.concatenate([ws_gate[0], ws_up[0]], axis=1).astype(BF16),
        "s_down": ws_down[0].astype(BF16),
    }
    b_perm = b_router[0][row_expert][:, None]

    assert bp % N_PARTS == 0 and bs % N_PARTS == 0
    bpp, bsp = bp // N_PARTS, bs // N_PARTS
    tpp = bpp * seq
    t_part = tpp + bsp
    grain = SC_WORKERS * SC_DISPATCH_CHUNK * SC_COMBINE_CHUNK // math.gcd(SC_DISPATCH_CHUNK, SC_COMBINE_CHUNK)
    t_pad = (t_part + grain - 1) // grain * grain
    assert t_pad % ROUTE_TILE == 0 and t_pad % DEST_TILE == 0 and tpp % bsp == 0

    c_all = jnp.concatenate([c_sample, c_prompt], axis=0)
    mod = _ada(c_all, w_ada[0], b_ada[0])
    modf = _ada(c_all, w_ada_final, b_ada_final)

    xmid_s, new_pool_t, new_conv_t, tails = _sample_mixer(
        x_sample.reshape(bs, d), mod, jnp.transpose(state_pool[0], (1, 0, 2)),
        jnp.transpose(state_conv[0], (1, 0, 2)), wl, tpp, t_pad, conv_k=conv_k)
    new_pool_s = jnp.transpose(new_pool_t, (1, 0, 2))
    new_conv_s = jnp.transpose(new_conv_t, (1, 0, 2))
    w_experts = tuple(a.reshape(a.shape[1:]) for a in (w_gate, w_up, w_down))

    n_blocks = (t_part * TOP_K + EXPERT_ROWS - 1) // EXPERT_ROWS + ne
    n_spare = -(-(t_pad - t_part) * TOP_K // EXPERT_ROWS)
    n_rows = (n_blocks + n_spare) * EXPERT_ROWS

    y_prompt, y_samples, npools, nconvs, w_bf = None, [], [], [], None
    mixed, dest = [], None
    for p in range(N_PARTS):
        h2, lgt = tails[p]
        xmid_p, h2, lgt, npool_p, nconv_p, *w_bf = _prompt_mixer(
            x_prompt, mod, bs, wl, h2, lgt, p * bpp, bpp, w_experts, w_bf, dest, conv_k=conv_k)
        npools.append(npool_p)
        nconvs.append(nconv_p)
        idx, wts, rank, counts_perm = _route(lgt, b_perm, t_part)
        counts = jnp.zeros((ne,), I32).at[row_expert].set(counts_perm[:, 0].astype(I32))
        nblk = (counts + EXPERT_ROWS - 1) // EXPERT_ROWS
        first_block = jnp.concatenate([jnp.zeros((1,), I32), jnp.cumsum(nblk).astype(I32)])
        dest = _dest_rows(first_block[:ne] * EXPERT_ROWS, idx, rank, t_part, n_blocks * EXPERT_ROWS)
        blk = jnp.arange(n_blocks, dtype=I32)[:, None]
        inside = (blk >= first_block[None, :ne]) & (blk < first_block[None, 1:])
        rows_left = counts[None, :] - (blk - first_block[None, :ne]) * EXPERT_ROWS
        block_valid = jnp.sum(jnp.where(inside, jnp.minimum(rows_left, EXPERT_ROWS), 0), axis=1)
        mixed.append((xmid_p, h2, wts, dest, first_block, block_valid))
    for p in range(N_PARTS):
        xmid_p, h2, wts, dest, first_block, block_valid = mixed[p]

        def chunked(ch):
            return dest.reshape(TOP_K, t_pad // ch, ch).transpose(1, 0, 2)

        xs = _sc_dispatch(h2, chunked(SC_DISPATCH_CHUNK), n_rows)
        y = _experts(xs, w_bf[0], w_bf[1], first_block, block_valid)
        yg = _sc_combine_gather(y, chunked(SC_COMBINE_CHUNK), t_pad)

        y_samples.append(_final_sample(xmid_s, yg, wts, mod, modf, g_final[None, :], p, bsp, tpp,
                                       g_final if y_prompt is None else y_prompt))
        y_prompt = _final_prompt(xmid_p, yg, wts, mod, modf, bs, g_final[None, :], p * bpp, bp, y_prompt)
    y_sample = jnp.concatenate(y_samples, axis=0)
    npool_p = jnp.concatenate(npools, axis=0)
    nconv_p = jnp.concatenate(nconvs, axis=0)

    return (y_prompt, y_sample[:, None, :], npool_p[None], nconv_p[None], new_pool_s[None], new_conv_s[None])
```

```python
import functools
import math

import jax
import jax.numpy as jnp
from jax import lax
from jax.experimental import pallas as pl
from jax.experimental.pallas import tpu as pltpu
from jax.experimental.pallas import tpu_sc as plsc

POOL_WINDOWS = (2, 4, 8, 16)
N_EXPERT_GROUPS = 8
TOPK_GROUPS = 4
TOP_K = 8
ROUTED_SCALE = 2.5
EPS = 1e-6

LANES = 128
SUBLANES = 8
VMEM_LIMIT = 52 * 1024 * 1024

ADA_TILE = 1024
SEQ_TILE = 512
ROW_CHUNK = 16
ROUTE_TILE = 1408
EXPERT_ROWS = 1024
N_PARTS = 2
EXPERT_BUFFERS = 4
EXPERT_PIECES = 4
EXPERTS_PER_STEP = 2
FINAL_TILE = 512
DEST_TILE = 2816

SC_CORES = 2
SC_SUBCORES = 16
SC_WORKERS = SC_CORES * SC_SUBCORES
SC_DISPATCH_CHUNK = 88
SC_COMBINE_CHUNK = 24

F32 = jnp.float32
BF16 = jnp.bfloat16
I32 = jnp.int32
U32 = jnp.uint32
NEG_INF = float("-inf")
HI16 = 0xFFFF0000


def _sigmoid(x):
    return 1.0 / (1.0 + jnp.exp(-x))


def _silu(x):
    return x * _sigmoid(x)


def _rms_norm(x, g):
    return x * lax.rsqrt(jnp.mean(x * x, axis=-1, keepdims=True) + EPS) * g


def _dot(a, b):
    return jnp.dot(a, b, preferred_element_type=F32)


def _pack_bf16_pairs(x):
    w = x.shape[1] // 2
    bits = lax.bitcast_convert_type(x.astype(BF16).astype(F32), U32)
    return lax.bitcast_convert_type((bits[:, :w] >> 16) | (bits[:, w:] & U32(HI16)), I32)


def _unpack_bf16_pairs(p):
    bits = lax.bitcast_convert_type(p, U32)
    lo = lax.bitcast_convert_type(bits << 16, F32)
    hi = lax.bitcast_convert_type(bits & U32(HI16), F32)
    return jnp.concatenate([lo, hi], axis=1)


def _ada_kernel(c_ref, w_ref, b_ref, o_ref):
    s = _silu(c_ref[...])
    o_ref[...] = _dot(s.astype(BF16), w_ref[...].astype(BF16)) + b_ref[...]


def _ada(c, w, b):
    rows, d = c.shape
    n = w.shape[1]
    tn = ADA_TILE
    return pl.pallas_call(
        _ada_kernel,
        grid=(n // tn,),
        in_specs=[pl.BlockSpec((rows, d), lambda j: (0, 0)),
                  pl.BlockSpec((d, tn), lambda j: (0, j)),
                  pl.BlockSpec((1, tn), lambda j: (0, j))],
        out_specs=pl.BlockSpec((rows, tn), lambda j: (0, j)),
        out_shape=jax.ShapeDtypeStruct((rows, n), F32),
        compiler_params=pltpu.CompilerParams(dimension_semantics=("arbitrary",),
                                             vmem_limit_bytes=VMEM_LIMIT),
        name="ada",
    )(c, w, b.reshape(1, n))


def _mixer_tail(x, pool_d, conv_act, mod, w, xmid_ref, h2_ref, lgt_ref):
    sh2, sc2, g1, g2 = mod
    pw = pool_d.shape[1]
    pool_out = _dot(pool_d, w["pool"][...]) * w["pool_scale"][...]
    mix = _dot(pool_out.astype(BF16), w["out"][:pw, :]) + _dot(conv_act, w["out"][pw:, :])
    x1 = x + g1 * mix
    h2f = _rms_norm(x1, w["g_ffn"][...]) * (1.0 + sc2) + sh2
    h2_ref[...] = _pack_bf16_pairs(h2f)
    h2 = h2f.astype(BF16)
    lgt_ref[...] = lax.dot_general(w["router_t"][...], h2, (((1,), (1,)), ((), ())),
                                   preferred_element_type=F32)
    gu = _dot(h2, w["s_gu"][...])
    de = gu.shape[1] // 2
    hs = _silu(gu[:, :de]) * gu[:, de:]
    shared = _dot(hs.astype(BF16), w["s_down"][...])
    xmid_ref[...] = x1 + g2 * shared


_W_NAMES = ("g_mix", "in", "pool", "pool_scale", "dw", "b_dw", "ln_g", "ln_b", "out", "g_ffn",
            "router_t", "s_gu", "s_down")


def _layer_norm_silu(yc, g, b):
    mu = jnp.mean(yc, axis=-1, keepdims=True)
    yz = yc - mu
    var = jnp.mean(yz * yz, axis=-1, keepdims=True)
    return _silu(yz * lax.rsqrt(var + EPS) * g + b)


U_HALO = 32
A_HALO = 32


def _prompt_mixer_kernel(x_ref, mod_ref, *refs, tl, d, pw, cw, conv_k, mod_row0):
    nw = len(_W_NAMES)
    w = dict(zip(_W_NAMES, refs[:nw]))
    wg_ref, wu_ref, wd_ref = refs[nw + 2:nw + 5]
    xmid_ref, h2_ref, lgt_ref, npool_ref, nconv_ref, wgu_bf_ref, wd_bf_ref = refs[-16:-9]
    ubuf, s2buf, s4buf, s8buf, abuf, ashift, dbuf, cbuf, rowb = refs[-9:]
    de = wg_ref.shape[2]
    wgu_bf_ref[:, :, :de] = wg_ref[...].astype(BF16)
    wgu_bf_ref[:, :, de:] = wu_ref[...].astype(BF16)
    wd_bf_ref[...] = wd_ref[...].astype(BF16)
    l = pl.program_id(1)
    pool_buf = max(POOL_WINDOWS) - 1
    uh, ah = U_HALO, A_HALO
    pg = pw // len(POOL_WINDOWS)
    nu, na = uh + tl, ah + tl

    @pl.when(l == 0)
    def _():
        ubuf[0:uh, :] = jnp.zeros((uh, pw), F32)
        abuf[0:ah, :] = jnp.zeros((ah, cw), F32)

    @pl.when(l > 0)
    def _():
        ubuf[0:uh, :] = ubuf[tl:tl + uh, :]
        abuf[0:ah, :] = abuf[tl:tl + ah, :]

    x = x_ref[0]
    row = pl.ds((mod_row0 + pl.program_id(0)) % SUBLANES, 1)
    sh1, sc1, g1, sh2, sc2, g2 = [mod_ref[row, i * d:(i + 1) * d] for i in range(6)]

    h = _rms_norm(x, w["g_mix"][...]) * (1.0 + sc1) + sh1
    z = _dot(h.astype(BF16), w["in"][...])
    u = z[:, :pw]
    ubuf[uh:nu, :] = u
    abuf[ah:na, :] = z[:, pw:pw + cw] * _sigmoid(z[:, pw + cw:])

    s2buf[8:nu, :] = ubuf[8:nu, :] + ubuf[7:nu - 1, :]
    s4buf[16:nu, :] = s2buf[16:nu, pg:] + s2buf[14:nu - 2, pg:]
    s8buf[24:nu, :] = s4buf[24:nu, pg:] + s4buf[20:nu - 4, pg:]
    t = l * tl + lax.broadcasted_iota(I32, (tl, 1), 0)
    sums = (s2buf[uh:nu, 0:pg], s4buf[uh:nu, 0:pg], s8buf[uh:nu, 0:pg],
            s8buf[uh:nu, pg:2 * pg] + s8buf[uh - 8:nu - 8, pg:2 * pg])
    for g, win in enumerate(POOL_WINDOWS):
        cols = slice(g * pg, (g + 1) * pg)
        inv = 1.0 / jnp.minimum(win, t + 1).astype(F32)
        dbuf[:, cols] = (sums[g] * inv - u[:, cols]).astype(BF16)

    for r in range(1, SUBLANES):
        ashift[r - 1, 8:na, :] = abuf[8 - r:na - r, :]

    for k in range(conv_k):
        rowb[k] = jnp.broadcast_to(w["dw"][k:k + 1, :], (SUBLANES, cw))
    rowb[conv_k] = jnp.broadcast_to(w["b_dw"][...], (SUBLANES, cw))

    for c in range(tl // ROW_CHUNK):
        groups = []
        for o in range(ah + c * ROW_CHUNK, ah + (c + 1) * ROW_CHUNK, SUBLANES):
            yc = abuf[o:o + SUBLANES, :] * rowb[conv_k - 1]
            for back in range(1, conv_k):
                q, r = divmod(back, SUBLANES)
                src = abuf if r == 0 else ashift.at[r - 1]
                yc = yc + src[o - q * SUBLANES:o - (q - 1) * SUBLANES, :] * rowb[conv_k - 1 - back]
            groups.append(yc + rowb[conv_k])
        r0 = c * ROW_CHUNK
        cbuf[r0:r0 + ROW_CHUNK, :] = _layer_norm_silu(
            jnp.concatenate(groups, axis=0), w["ln_g"][...], w["ln_b"][...]).astype(BF16)

    @pl.when(l == pl.num_programs(1) - 1)
    def _():
        npool_ref[0] = ubuf[nu - pool_buf:nu, :]
        nconv_ref[0] = abuf[na - (conv_k - 1):na, :]

    _mixer_tail(x, dbuf[...], cbuf[...], (sh2, sc2, g1, g2), w, xmid_ref.at[0], h2_ref, lgt_ref)


def _sample_mixer_kernel(x_ref, mod_ref, sp_ref, sc_ref, *refs, d, pw, cw, conv_k):
    nw = len(_W_NAMES)
    w = dict(zip(_W_NAMES, refs[:nw]))
    xmid_ref, npool_ref, nconv_ref = refs[nw:nw + 3]
    tails = refs[nw + 3:-2]
    h2_tmp, lgt_tmp = refs[-2:]
    pool_buf = max(POOL_WINDOWS) - 1
    pg = pw // len(POOL_WINDOWS)
    x = x_ref[...]
    sh1, sc1, g1, sh2, sc2, g2 = [mod_ref[:, i * d:(i + 1) * d] for i in range(6)]
    h = _rms_norm(x, w["g_mix"][...]) * (1.0 + sc1) + sh1
    z = _dot(h.astype(BF16), w["in"][...])
    u = z[:, :pw]
    a = z[:, pw:pw + cw] * _sigmoid(z[:, pw + cw:])
    npool_ref[0:pool_buf - 1] = sp_ref[1:pool_buf]
    npool_ref[pool_buf - 1] = u
    nconv_ref[0:conv_k - 2] = sc_ref[1:conv_k - 1]
    nconv_ref[conv_k - 2] = a
    ds = []
    for g, win in enumerate(POOL_WINDOWS):
        cols = slice(g * pg, (g + 1) * pg)
        acc = u[:, cols]
        for j in range(1, win):
            acc = acc + sp_ref[pool_buf - j, :, cols]
        cnt = float(min(win, 1 + pool_buf))
        ds.append(acc / cnt - u[:, cols])
    pool_d = jnp.concatenate(ds, axis=-1).astype(BF16)
    yc = a * w["dw"][conv_k - 1:conv_k, :]
    for k in range(conv_k - 1):
        yc = yc + sc_ref[k] * w["dw"][k:k + 1, :]
    yc = yc + w["b_dw"][...]
    conv_act = _layer_norm_silu(yc, w["ln_g"][...], w["ln_b"][...]).astype(BF16)
    _mixer_tail(x, pool_d, conv_act, (sh2, sc2, g1, g2), w, xmid_ref, h2_tmp, lgt_tmp)
    n_parts = len(tails) // 2
    share = x.shape[0] // n_parts
    for p in range(n_parts):
        h2_ref, lgt_ref = tails[2 * p], tails[2 * p + 1]
        h2_ref[...] = jnp.zeros_like(h2_ref)
        lgt_ref[...] = jnp.zeros_like(lgt_ref)
        h2_ref[0:share, :] = h2_tmp[p * share:(p + 1) * share, :]
        lgt_ref[:, 0:share] = lgt_tmp[:, p * share:(p + 1) * share]


def _full_spec(a):
    nd = a.ndim
    return pl.BlockSpec(a.shape, lambda *_: (0,) * nd)


def _prompt_mixer(x, mod, mod_row0, wl, h2_all, lgt_all, b0, b, w_experts, w_bf_prev, after, *, conv_k):
    n_batch, seq, d = x.shape
    w_gate, w_up, w_down = w_experts
    n_exp, _, de = w_gate.shape
    pw = wl["pool"].shape[0]
    cw = wl["dw"].shape[1]
    ne = wl["router_t"].shape[0]
    tl = SEQ_TILE
    nl = seq // tl
    pg = pw // len(POOL_WINDOWS)
    assert POOL_WINDOWS == (2, 4, 8, 16) and conv_k - 1 <= A_HALO
    ws = [wl[n] for n in _W_NAMES]
    kern = functools.partial(_prompt_mixer_kernel, tl=tl, d=d, pw=pw, cw=cw, conv_k=conv_k, mod_row0=mod_row0 + b0)
    ecs = n_exp // (n_batch * nl)
    assert ecs * n_batch * nl == n_exp

    def w_map(i, j):
        return ((i + b0) * nl + j, 0, 0)

    ins = [x, mod, *ws, h2_all, lgt_all, w_gate, w_up, w_down]
    in_specs = ([pl.BlockSpec((1, tl, d), lambda i, j: (i + b0, j, 0)),
                 pl.BlockSpec((SUBLANES, mod.shape[-1]), lambda i, j: ((i + b0 + mod_row0) // SUBLANES, 0))]
                + [_full_spec(a) for a in ws]
                + [pl.BlockSpec(memory_space=pl.ANY), pl.BlockSpec(memory_space=pl.ANY),
                   pl.BlockSpec((ecs, d, de), w_map), pl.BlockSpec((ecs, d, de), w_map),
                   pl.BlockSpec((ecs, de, d), w_map)])
    aliases = {2 + len(ws): 1, 3 + len(ws): 2}
    if w_bf_prev is not None:
        aliases.update({len(ins): 5, len(ins) + 1: 6})
        ins += list(w_bf_prev)
        in_specs += [pl.BlockSpec(memory_space=pl.ANY), pl.BlockSpec(memory_space=pl.ANY)]
    if after is not None:
        ins.append(after)
        in_specs.append(pl.BlockSpec(memory_space=pl.ANY))
    return pl.pallas_call(
        kern,
        grid=(b, nl),
        in_specs=in_specs,
        out_specs=[pl.BlockSpec((1, tl, d), lambda i, j: (i, j, 0)),
                   pl.BlockSpec((tl, d // 2), lambda i, j: (i * nl + j, 0)),
                   pl.BlockSpec((ne, tl), lambda i, j: (0, i * nl + j)),
                   pl.BlockSpec((1, max(POOL_WINDOWS) - 1, pw), lambda i, j: (i, 0, 0)),
                   pl.BlockSpec((1, conv_k - 1, cw), lambda i, j: (i, 0, 0)),
                   pl.BlockSpec((ecs, d, 2 * de), w_map), pl.BlockSpec((ecs, de, d), w_map)],
        out_shape=[jax.ShapeDtypeStruct((b, seq, d), F32),
                   jax.ShapeDtypeStruct(h2_all.shape, I32),
                   jax.ShapeDtypeStruct(lgt_all.shape, F32),
                   jax.ShapeDtypeStruct((b, max(POOL_WINDOWS) - 1, pw), F32),
                   jax.ShapeDtypeStruct((b, conv_k - 1, cw), F32),
                   jax.ShapeDtypeStruct((n_exp, d, 2 * de), BF16),
                   jax.ShapeDtypeStruct((n_exp, de, d), BF16)],
        scratch_shapes=[pltpu.VMEM((U_HALO + tl, pw), F32), pltpu.VMEM((U_HALO + tl, pw), F32),
                        pltpu.VMEM((U_HALO + tl, pw - pg), F32), pltpu.VMEM((U_HALO + tl, pw - 2 * pg), F32),
                        pltpu.VMEM((A_HALO + tl, cw), F32), pltpu.VMEM((SUBLANES - 1, A_HALO + tl, cw), F32),
                        pltpu.VMEM((tl, pw), BF16), pltpu.VMEM((tl, cw), BF16),
                        pltpu.VMEM((conv_k + 1, SUBLANES, cw), F32)],
        compiler_params=pltpu.CompilerParams(dimension_semantics=("arbitrary", "arbitrary"),
                                             vmem_limit_bytes=VMEM_LIMIT),
        input_output_aliases=aliases,
        name="prompt_mixer",
    )(*ins)


def _sample_mixer(x, mod, sp, sc, wl, t_prompt, t_pad, *, conv_k):
    rows, d = x.shape
    tail = t_pad - t_prompt
    assert t_prompt % tail == 0 and rows % N_PARTS == 0 and rows // N_PARTS <= tail
    pw = wl["pool"].shape[0]
    cw = wl["dw"].shape[1]
    ne = wl["router_t"].shape[0]
    ws = [wl[n] for n in _W_NAMES]
    kern = functools.partial(_sample_mixer_kernel, d=d, pw=pw, cw=cw, conv_k=conv_k)
    ins = [x, mod, sp, sc] + ws
    tail_specs = [pl.BlockSpec((tail, d // 2), lambda i: (t_prompt // tail, 0)),
                  pl.BlockSpec((ne, tail), lambda i: (0, t_prompt // tail))] * N_PARTS
    tail_shapes = [jax.ShapeDtypeStruct((t_pad, d // 2), I32), jax.ShapeDtypeStruct((ne, t_pad), F32)] * N_PARTS
    outs = pl.pallas_call(
        kern,
        grid=(1,),
        in_specs=[_full_spec(x), pl.BlockSpec((rows, mod.shape[1]), lambda i: (0, 0))]
                 + [_full_spec(a) for a in ins[2:]],
        out_specs=[pl.BlockSpec((rows, d), lambda i: (0, 0)), _full_spec(sp), _full_spec(sc)] + tail_specs,
        out_shape=[jax.ShapeDtypeStruct((rows, d), F32),
                   jax.ShapeDtypeStruct(sp.shape, F32),
                   jax.ShapeDtypeStruct(sc.shape, F32)] + tail_shapes,
        scratch_shapes=[pltpu.VMEM((rows, d // 2), I32), pltpu.VMEM((ne, rows), F32)],
        compiler_params=pltpu.CompilerParams(dimension_semantics=("arbitrary",),
                                             vmem_limit_bytes=VMEM_LIMIT),
        name="sample_mixer",
    )(*ins)
    return outs[0], outs[1], outs[2], [(outs[3 + 2 * p], outs[4 + 2 * p]) for p in range(N_PARTS)]


def _sublane_max(x):
    return jnp.max(x, axis=0, keepdims=True)


def _route_kernel(lg_ref, b_ref, tri_ref, idx_ref, wt_ref, rank_ref, cnt_ref, carry, *, t_valid):
    i = pl.program_id(0)
    ne, tt = lg_ref.shape
    ng = N_EXPERT_GROUPS
    per = ne // ng

    @pl.when(i == 0)
    def _():
        carry[...] = jnp.zeros_like(carry)

    s = _sigmoid(lg_ref[...])
    sel = s + b_ref[...]
    s3 = [s[p * ng:(p + 1) * ng, :] for p in range(per)]
    sel3 = [sel[p * ng:(p + 1) * ng, :] for p in range(per)]
    m1 = sel3[0]
    m2 = jnp.full_like(m1, NEG_INF)
    for p in range(1, per):
        m2 = jnp.maximum(m2, jnp.minimum(m1, sel3[p]))
        m1 = jnp.maximum(m1, sel3[p])
    gs = m1 + m2
    gi = lax.broadcasted_iota(jnp.int32, (ng, tt), 0)
    beaten = jnp.zeros((ng, tt), jnp.int32)
    for g in range(ng):
        row = gs[g:g + 1, :]
        beats = (row > gs) | ((row == gs) & (gi > g))
        beaten = beaten + beats.astype(jnp.int32)
    keep = beaten < TOPK_GROUPS
    cur = [jnp.where(keep, sel3[p], NEG_INF) for p in range(per)]
    eid = [(gi * per + p).astype(F32) for p in range(per)]
    idxs, wts, hits = [], [], []
    for _ in range(TOP_K):
        m = cur[0]
        for p in range(1, per):
            m = jnp.maximum(m, cur[p])
        m = _sublane_max(m)
        cand = jnp.where(cur[0] == m, eid[0], float(ne))
        for p in range(1, per):
            cand = jnp.minimum(cand, jnp.where(cur[p] == m, eid[p], float(ne)))
        e_sel = jnp.min(cand, axis=0, keepdims=True)
        hit = [eid[p] == e_sel for p in range(per)]
        wk = jnp.where(hit[0], s3[0], 0.0)
        for p in range(1, per):
            wk = wk + jnp.where(hit[p], s3[p], 0.0)
        wts.append(jnp.sum(wk, axis=0, keepdims=True))
        cur = [jnp.where(hit[p], NEG_INF, cur[p]) for p in range(per)]
        idxs.append(e_sel)
        hits.append(hit)
    wsum = wts[0]
    for k in range(1, TOP_K):
        wsum = wsum + wts[k]
    chosen = [functools.reduce(lambda a, b: a | b, [hits[k][p] for k in range(TOP_K)]) for p in range(per)]
    real = (i * tt + lax.broadcasted_iota(I32, (1, tt), 1)) < t_valid
    onehot = jnp.concatenate([(c & real).astype(F32) for c in chosen], axis=0)
    before = _dot(onehot.astype(BF16), tri_ref[...]) + carry[...]
    for k in range(TOP_K):
        rk = jnp.where(hits[k][0], before[0:ng, :], 0.0)
        for p in range(1, per):
            rk = rk + jnp.where(hits[k][p], before[p * ng:(p + 1) * ng, :], 0.0)
        rank_ref[k:k + 1, :] = jnp.sum(rk, axis=0, keepdims=True).astype(jnp.int32)
        idx_ref[k:k + 1, :] = idxs[k].astype(jnp.int32)
        wt_ref[k:k + 1, :] = wts[k] / wsum * ROUTED_SCALE
    carry[...] = carry[...] + jnp.sum(onehot, axis=1, keepdims=True)
    cnt_ref[...] = carry[...]


def _route(lgt, b_perm, t_valid):
    ne, t = lgt.shape
    tt = ROUTE_TILE
    tri = (jnp.arange(tt)[:, None] < jnp.arange(tt)[None, :]).astype(BF16)
    return pl.pallas_call(
        functools.partial(_route_kernel, t_valid=t_valid),
        grid=(t // tt,),
        in_specs=[pl.BlockSpec((ne, tt), lambda i: (0, i)),
                  pl.BlockSpec((ne, 1), lambda i: (0, 0)),
                  pl.BlockSpec((tt, tt), lambda i: (0, 0))],
        out_specs=[pl.BlockSpec((TOP_K, tt), lambda i: (0, i)),
                   pl.BlockSpec((TOP_K, tt), lambda i: (0, i)),
                   pl.BlockSpec((TOP_K, tt), lambda i: (0, i)),
                   pl.BlockSpec((ne, 1), lambda i: (0, 0))],
        out_shape=[jax.ShapeDtypeStruct((TOP_K, t), jnp.int32),
                   jax.ShapeDtypeStruct((TOP_K, t), F32),
                   jax.ShapeDtypeStruct((TOP_K, t), jnp.int32),
                   jax.ShapeDtypeStruct((ne, 1), F32)],
        scratch_shapes=[pltpu.VMEM((ne, 1), F32)],
        compiler_params=pltpu.CompilerParams(dimension_semantics=("arbitrary",),
                                             vmem_limit_bytes=VMEM_LIMIT),
        name="route",
    )(lgt, b_perm, tri)


def _dest_kernel(pstart_ref, idx_ref, rank_ref, dest_ref, *, t_valid, spare_row):
    idx = idx_ref[...]
    k_top, tile = idx.shape

    def add_start(e, acc):
        return acc + jnp.where(idx == e, pstart_ref[e], 0)

    dest = lax.fori_loop(0, pstart_ref.shape[0], add_start, rank_ref[...])
    tok = pl.program_id(0) * tile + lax.broadcasted_iota(I32, idx.shape, 1)
    spare = spare_row + (tok - t_valid) * k_top + lax.broadcasted_iota(I32, idx.shape, 0)
    dest_ref[...] = jnp.where(tok < t_valid, dest, spare)


def _dest_rows(pstart, idx, rank, t_valid, spare_row):
    k_top, t = idx.shape
    tile = DEST_TILE
    spec = pl.BlockSpec((k_top, tile), lambda i, ps: (0, i))
    return pl.pallas_call(
        functools.partial(_dest_kernel, t_valid=t_valid, spare_row=spare_row),
        grid_spec=pltpu.PrefetchScalarGridSpec(num_scalar_prefetch=1, grid=(t // tile,),
                                               in_specs=[spec, spec], out_specs=spec),
        out_shape=jax.ShapeDtypeStruct((k_top, t), I32),
        compiler_params=pltpu.CompilerParams(dimension_semantics=("arbitrary",),
                                             vmem_limit_bytes=VMEM_LIMIT),
        name="dest_rows",
    )(pstart, idx, rank)


def _expert_kernel(first_ref, valid_ref, xs_hbm, wgu_ref, wd_ref, y_hbm, xbuf, ybuf, in_sem, out_sem):
    step = pl.program_id(0)
    ne = first_ref.shape[0] - 1
    nbuf, rows = xbuf.shape[:2]
    piece = rows // EXPERT_PIECES
    per_step, de = wd_ref.shape[:2]
    n_total = first_ref[ne]

    def in_copy(g, h):
        return pltpu.make_async_copy(xs_hbm.at[pl.ds(g * rows + h * piece, piece)],
                                     xbuf.at[g % nbuf, pl.ds(h * piece, piece)], in_sem.at[g % nbuf])

    def out_copy(g, h):
        return pltpu.make_async_copy(ybuf.at[g % nbuf, pl.ds(h * piece, piece)],
                                     y_hbm.at[pl.ds(g * rows + h * piece, piece)], out_sem.at[g % nbuf])

    def used_pieces(copy, g, act):
        getattr(copy(g, 0), act)()
        for h in range(1, EXPERT_PIECES):
            @pl.when(valid_ref[g] > h * piece)
            def _():
                getattr(copy(g, h), act)()

    @pl.when(step == 0)
    def _():
        for g in range(nbuf - 1):
            @pl.when(g < n_total)
            def _():
                used_pieces(in_copy, g, "start")

    def block(g, carry, *, j):
        slot = g % nbuf

        @pl.when(g + nbuf - 1 < n_total)
        def _():
            used_pieces(in_copy, g + nbuf - 1, "start")

        used_pieces(in_copy, g, "wait")

        @pl.when(g >= nbuf)
        def _():
            used_pieces(out_copy, g - nbuf, "wait")

        valid = valid_ref[g]

        def run(n):
            rid = lax.broadcasted_iota(I32, (n, 1), 0)
            xs = jnp.where(rid < valid, _unpack_bf16_pairs(xbuf[slot, 0:n]), 0.0).astype(BF16)
            gu = _dot(xs, wgu_ref[j])
            hb = _silu(gu[:, :de]) * gu[:, de:]
            ybuf[slot, 0:n] = _pack_bf16_pairs(_dot(hb.astype(BF16), wd_ref[j]))

        for h in range(EXPERT_PIECES):
            @pl.when((valid > h * piece) & (valid <= (h + 1) * piece))
            def _():
                run((h + 1) * piece)

        used_pieces(out_copy, g, "start")
        return carry

    for j in range(per_step):
        e = step * per_step + j
        lax.fori_loop(first_ref[e], first_ref[e + 1], functools.partial(block, j=j), 0)

    @pl.when(step == pl.num_programs(0) - 1)
    def _():
        for back in range(nbuf, 0, -1):
            @pl.when(n_total >= back)
            def _():
                used_pieces(out_copy, n_total - back, "wait")


def _experts(xs, w_gu, w_down, first_block, block_valid):
    p, dw = xs.shape
    ne, de, d = w_down.shape
    return pl.pallas_call(
        _expert_kernel,
        grid_spec=pltpu.PrefetchScalarGridSpec(
            num_scalar_prefetch=2,
            grid=(ne // EXPERTS_PER_STEP,),
            in_specs=[pl.BlockSpec(memory_space=pl.ANY),
                      pl.BlockSpec((EXPERTS_PER_STEP, d, 2 * de), lambda e, fb, cn: (e, 0, 0)),
                      pl.BlockSpec((EXPERTS_PER_STEP, de, d), lambda e, fb, cn: (e, 0, 0))],
            out_specs=pl.BlockSpec(memory_space=pl.ANY),
            scratch_shapes=[pltpu.VMEM((EXPERT_BUFFERS, EXPERT_ROWS, dw), I32),
                            pltpu.VMEM((EXPERT_BUFFERS, EXPERT_ROWS, dw), I32),
                            pltpu.SemaphoreType.DMA((EXPERT_BUFFERS,)), pltpu.SemaphoreType.DMA((EXPERT_BUFFERS,))]),
        out_shape=jax.ShapeDtypeStruct((p, dw), I32),
        compiler_params=pltpu.CompilerParams(dimension_semantics=("arbitrary",),
                                             vmem_limit_bytes=VMEM_LIMIT),
        name="experts",
    )(first_block, block_valid, xs, w_gu, w_down)


def _sc_first_chunk(n_chunks):
    per_worker = n_chunks // SC_WORKERS
    assert per_worker * SC_WORKERS == n_chunks
    return per_worker, (lax.axis_index("s") * SC_CORES + lax.axis_index("c")) * per_worker


def _sc_start(copies):
    for cp in copies:
        cp.start()


def _sc_wait(copies):
    for cp in copies:
        cp.wait()


def _sc_dispatch(h2, dest, n_rows):
    w = h2.shape[1]
    n_chunks, k_top, ch = dest.shape
    mesh = plsc.VectorSubcoreMesh(core_axis_name="c", subcore_axis_name="s")

    @functools.partial(
        pl.kernel, mesh=mesh, out_type=jax.ShapeDtypeStruct((n_rows, w), I32),
        scratch_types=[pltpu.VMEM((2, k_top, ch), I32), pltpu.VMEM((2, ch, w), I32),
                       pltpu.SemaphoreType.DMA((2,)), pltpu.SemaphoreType.DMA((2,))])
    def dispatch(h2_hbm, dest_hbm, xs_hbm, idx_v, rows_v, load_sem, store_sem):
        per_worker, first = _sc_first_chunk(n_chunks)

        def loads(i):
            slot = i % 2
            return (pltpu.make_async_copy(h2_hbm.at[pl.ds((first + i) * ch, ch)], rows_v.at[slot], load_sem.at[slot]),
                    pltpu.make_async_copy(dest_hbm.at[first + i], idx_v.at[slot], load_sem.at[slot]))

        def scatters(i):
            slot = i % 2
            return [pltpu.make_async_copy(rows_v.at[slot], xs_hbm.at[idx_v.at[slot, k]], store_sem.at[slot])
                    for k in range(k_top)]

        for i in range(min(2, per_worker)):
            _sc_start(loads(i))
        for i in range(per_worker):
            _sc_wait(loads(i))
            _sc_start(scatters(i))
            if 1 <= i < per_worker - 1:
                _sc_wait(scatters(i - 1))
                _sc_start(loads(i + 1))
        for i in range(max(per_worker - 2, 0), per_worker):
            _sc_wait(scatters(i))

    return dispatch(h2, dest)


def _sc_combine_gather(y, dest, t):
    w = y.shape[1]
    n_chunks, k_top, ch = dest.shape
    n_local = n_chunks // SC_WORKERS
    mesh = plsc.VectorSubcoreMesh(core_axis_name="c", subcore_axis_name="s")

    @functools.partial(
        pl.kernel, mesh=mesh, out_type=jax.ShapeDtypeStruct((k_top, t, w), I32),
        scratch_types=[pltpu.VMEM((n_local, k_top, ch), I32), pltpu.VMEM((k_top, ch, w), I32),
                       pltpu.SemaphoreType.DMA((k_top,)), pltpu.SemaphoreType.DMA((k_top,))])
    def gather(y_hbm, dest_hbm, out_hbm, idx_v, rows_v, load_sem, store_sem):
        per_worker, first = _sc_first_chunk(n_chunks)
        pltpu.sync_copy(dest_hbm.at[pl.ds(first, per_worker)], idx_v)

        def fetch(i, k):
            return pltpu.make_async_copy(y_hbm.at[idx_v.at[i, k]], rows_v.at[k], load_sem.at[k])

        def put(i, k):
            return pltpu.make_async_copy(rows_v.at[k], out_hbm.at[k, pl.ds((first + i) * ch, ch)], store_sem.at[k])

        @pl.loop(0, per_worker)
        def _(i):
            for k in range(k_top):
                @pl.when(i > 0)
                def _():
                    put(i - 1, k).wait()

                fetch(i, k).start()
            for k in range(k_top):
                fetch(i, k).wait()
                put(i, k).start()

        for k in range(k_top):
            put(per_worker - 1, k).wait()

    return gather(y, dest)


def _final_kernel(xmid_ref, yg_ref, wt_ref, g2_ref, shf_ref, scf_ref, gf_ref, *rest, vec_row0=None):
    o_ref = rest[-1]
    if vec_row0 is not None:
        row = pl.ds((vec_row0 + pl.program_id(0)) % SUBLANES, 1)
        g2_ref, shf_ref, scf_ref = g2_ref.at[row, :], shf_ref.at[row, :], scf_ref.at[row, :]
    wt = wt_ref[...].T[:xmid_ref.shape[0]]
    routed = wt[:, 0:1] * _unpack_bf16_pairs(yg_ref[0])
    for k in range(1, yg_ref.shape[0]):
        routed = routed + wt[:, k:k + 1] * _unpack_bf16_pairs(yg_ref[k])
    x2 = xmid_ref[...] + g2_ref[...] * routed
    o_ref[...] = _rms_norm(x2, gf_ref[...]) * (1.0 + scf_ref[...]) + shf_ref[...]


def _final_prompt(xmid, yg, wts_t, mod, modf, mod_row0, g_final, b0, n_batch, out_prev):
    b, seq, d = xmid.shape
    k_top, _, w = yg.shape
    tl = FINAL_TILE
    nl = seq // tl

    def vec(col):
        return pl.BlockSpec((SUBLANES, d), lambda i, j: ((i + b0 + mod_row0) // SUBLANES, col))

    in_specs = [pl.BlockSpec((None, tl, d), lambda i, j: (i, j, 0)),
                pl.BlockSpec((k_top, tl, w), lambda i, j: (0, i * nl + j, 0)),
                pl.BlockSpec((k_top, tl), lambda i, j: (0, i * nl + j)),
                vec(5), vec(0), vec(1), pl.BlockSpec((1, d), lambda i, j: (0, 0))]
    args = [xmid, yg, wts_t, mod, modf, modf, g_final]
    aliases = {}
    if out_prev is not None:
        in_specs.append(pl.BlockSpec(memory_space=pl.ANY))
        args.append(out_prev)
        aliases = {7: 0}
    return pl.pallas_call(
        functools.partial(_final_kernel, vec_row0=b0 + mod_row0),
        grid=(b, nl),
        in_specs=in_specs,
        out_specs=pl.BlockSpec((None, tl, d), lambda i, j: (i + b0, j, 0)),
        out_shape=jax.ShapeDtypeStruct((n_batch, seq, d), F32),
        compiler_params=pltpu.CompilerParams(dimension_semantics=("arbitrary", "arbitrary"),
                                             vmem_limit_bytes=VMEM_LIMIT),
        input_output_aliases=aliases,
        name="final_prompt",
    )(*args)


def _final_sample(xmid_all, yg, wts_t, mod, modf, g_final, part, rows, first_row, after):
    d = xmid_all.shape[1]
    k_top, _, w = yg.shape

    def vec(col):
        return pl.BlockSpec((rows, d), lambda i: (part, col))

    return pl.pallas_call(
        _final_kernel,
        grid=(1,),
        in_specs=[vec(0),
                  pl.BlockSpec((k_top, rows, w), lambda i: (0, first_row // rows, 0)),
                  pl.BlockSpec((k_top, LANES), lambda i: (0, first_row // LANES)),
                  vec(5), vec(0), vec(1), pl.BlockSpec((1, d), lambda i: (0, 0)), pl.BlockSpec(memory_space=pl.ANY)],
        out_specs=pl.BlockSpec((rows, d), lambda i: (0, 0)),
        out_shape=jax.ShapeDtypeStruct((rows, d), F32),
        compiler_params=pltpu.CompilerParams(dimension_semantics=("arbitrary",),
                                             vmem_limit_bytes=VMEM_LIMIT),
        name="final_sample",
    )(xmid_all, yg, wts_t, mod, modf, modf, g_final, after)


def _block_diag_pairs(w_pool):
    g, c, _ = w_pool.shape
    eye = jnp.eye(g, dtype=w_pool.dtype)
    return (eye[:, None, :, None] * w_pool[:, :, None, :]).reshape(g * c, g * c)


def kernel(x_prompt, x_sample, state_pool, state_conv, c_prompt, c_sample, w_ada, b_ada, g_mix, w_in, w_pool, pool_scale, w_dw, b_dw, ln_g, ln_b, w_out, g_ffn, w_router, b_router, w_gate, w_up, w_down, ws_gate, ws_up, ws_down, w_ada_final, b_ada_final, g_final):
    bp, seq, d = x_prompt.shape
    bs = x_sample.shape[0]
    depth = w_ada.shape[0]
    assert depth == 1 and x_sample.shape[1] == 1
    conv_k = w_dw.shape[1]
    ne = w_router.shape[-1]
    per = ne // N_EXPERT_GROUPS

    row_expert = (jnp.arange(ne) % N_EXPERT_GROUPS) * per + jnp.arange(ne) // N_EXPERT_GROUPS
    wl = {
        "g_mix": g_mix[0][None, :],
        "in": w_in[0].astype(BF16),
        "pool": _block_diag_pairs(w_pool[0]).astype(BF16),
        "pool_scale": pool_scale[0][None, :],
        "dw": w_dw[0],
        "b_dw": b_dw[0][None, :],
        "ln_g": ln_g[0][None, :],
        "ln_b": ln_b[0][None, :],
        "out": w_out[0].astype(BF16),
        "g_ffn": g_ffn[0][None, :],
        "router_t": w_router[0].T[row_expert].astype(BF16),
        "s_gu": jnp.concatenate([ws_gate[0], ws_up[0]], axis=1).astype(BF16),
        "s_down": ws_down[0].astype(BF16),
    }
    b_perm = b_router[0][row_expert][:, None]

    assert bp % N_PARTS == 0 and bs % N_PARTS == 0
    bpp, bsp = bp // N_PARTS, bs // N_PARTS
    tpp = bpp * seq
    t_part = tpp + bsp
    grain = SC_WORKERS * SC_DISPATCH_CHUNK * SC_COMBINE_CHUNK // math.gcd(SC_DISPATCH_CHUNK, SC_COMBINE_CHUNK)
    t_pad = (t_part + grain - 1) // grain * grain
    assert t_pad % ROUTE_TILE == 0 and t_pad % DEST_TILE == 0 and tpp % bsp == 0

    c_all = jnp.concatenate([c_sample, c_prompt], axis=0)
    mod = _ada(c_all, w_ada[0], b_ada[0])
    modf = _ada(c_all, w_ada_final, b_ada_final)

    xmid_s, new_pool_t, new_conv_t, tails = _sample_mixer(
        x_sample.reshape(bs, d), mod, jnp.transpose(state_pool[0], (1, 0, 2)),
        jnp.transpose(state_conv[0], (1, 0, 2)), wl, tpp, t_pad, conv_k=conv_k)
    new_pool_s = jnp.transpose(new_pool_t, (1, 0, 2))
    new_conv_s = jnp.transpose(new_conv_t, (1, 0, 2))
    w_experts = tuple(a.reshape(a.shape[1:]) for a in (w_gate, w_up, w_down))

    n_blocks = (t_part * TOP_K + EXPERT_ROWS - 1) // EXPERT_ROWS + ne
    n_spare = -(-(t_pad - t_part) * TOP_K // EXPERT_ROWS)
    n_rows = (n_blocks + n_spare) * EXPERT_ROWS

    y_prompt, y_samples, npools, nconvs, w_bf = None, [], [], [], None
    mixed, dest = [], None
    for p in range(N_PARTS):
        h2, lgt = tails[p]
        xmid_p, h2, lgt, npool_p, nconv_p, *w_bf = _prompt_mixer(
            x_prompt, mod, bs, wl, h2, lgt, p * bpp, bpp, w_experts, w_bf, dest, conv_k=conv_k)
        npools.append(npool_p)
        nconvs.append(nconv_p)
        idx, wts, rank, counts_perm = _route(lgt, b_perm, t_part)
        counts = jnp.zeros((ne,), I32).at[row_expert].set(counts_perm[:, 0].astype(I32))
        nblk = (counts + EXPERT_ROWS - 1) // EXPERT_ROWS
        first_block = jnp.concatenate([jnp.zeros((1,), I32), jnp.cumsum(nblk).astype(I32)])
        dest = _dest_rows(first_block[:ne] * EXPERT_ROWS, idx, rank, t_part, n_blocks * EXPERT_ROWS)
        blk = jnp.arange(n_blocks, dtype=I32)[:, None]
        inside = (blk >= first_block[None, :ne]) & (blk < first_block[None, 1:])
        rows_left = counts[None, :] - (blk - first_block[None, :ne]) * EXPERT_ROWS
        block_valid = jnp.sum(jnp.where(inside, jnp.minimum(rows_left, EXPERT_ROWS), 0), axis=1)
        mixed.append((xmid_p, h2, wts, dest, first_block, block_valid))
    for p in range(N_PARTS):
        xmid_p, h2, wts, dest, first_block, block_valid = mixed[p]

        def chunked(ch):
            return dest.reshape(TOP_K, t_pad // ch, ch).transpose(1, 0, 2)

        xs = _sc_dispatch(h2, chunked(SC_DISPATCH_CHUNK), n_rows)
        y = _experts(xs, w_bf[0], w_bf[1], first_block, block_valid)
        yg = _sc_combine_gather(y, chunked(SC_COMBINE_CHUNK), t_pad)

        y_samples.append(_final_sample(xmid_s, yg, wts, mod, modf, g_final[None, :], p, bsp, tpp,
                                       g_final if y_prompt is None else y_prompt))
        y_prompt = _final_prompt(xmid_p, yg, wts, mod, modf, bs, g_final[None, :], p * bpp, bp, y_prompt)
    y_sample = jnp.concatenate(y_samples, axis=0)
    npool_p = jnp.concatenate(npools, axis=0)
    nconv_p = jnp.concatenate(nconvs, axis=0)

    return (y_prompt, y_sample[:, None, :], npool_p[None], nconv_p[None], new_pool_s[None], new_conv_s[None])
```

```python
import functools
import math

import jax
import jax.numpy as jnp
from jax import lax
from jax.experimental import pallas as pl
from jax.experimental.pallas import tpu as pltpu
from jax.experimental.pallas import tpu_sc as plsc

POOL_WINDOWS = (2, 4, 8, 16)
N_EXPERT_GROUPS = 8
TOPK_GROUPS = 4
TOP_K = 8
ROUTED_SCALE = 2.5
EPS = 1e-6

LANES = 128
SUBLANES = 8
VMEM_LIMIT = 52 * 1024 * 1024

ADA_TILE = 1024
SEQ_TILE = 512
ROW_CHUNK = 16
ROUTE_TILE = 1408
EXPERT_ROWS = 1024
N_PARTS = 2
EXPERT_BUFFERS = 4
EXPERT_PIECES = 8
EXPERTS_PER_STEP = 2
FINAL_TILE = 512
DEST_TILE = 2816

SC_CORES = 2
SC_SUBCORES = 16
SC_WORKERS = SC_CORES * SC_SUBCORES
SC_DISPATCH_CHUNK = 88
SC_COMBINE_CHUNK = 24

F32 = jnp.float32
BF16 = jnp.bfloat16
I32 = jnp.int32
U32 = jnp.uint32
NEG_INF = float("-inf")
HI16 = 0xFFFF0000


def _sigmoid(x):
    return 1.0 / (1.0 + jnp.exp(-x))


def _silu(x):
    return x * _sigmoid(x)


def _rms_norm(x, g):
    return x * lax.rsqrt(jnp.mean(x * x, axis=-1, keepdims=True) + EPS) * g


def _dot(a, b):
    return jnp.dot(a, b, preferred_element_type=F32)


def _pack_bf16_pairs(x):
    w = x.shape[1] // 2
    bits = lax.bitcast_convert_type(x.astype(BF16).astype(F32), U32)
    return lax.bitcast_convert_type((bits[:, :w] >> 16) | (bits[:, w:] & U32(HI16)), I32)


def _unpack_bf16_pairs(p):
    bits = lax.bitcast_convert_type(p, U32)
    lo = lax.bitcast_convert_type(bits << 16, F32)
    hi = lax.bitcast_convert_type(bits & U32(HI16), F32)
    return jnp.concatenate([lo, hi], axis=1)


def _ada_kernel(c_ref, w_ref, b_ref, o_ref):
    s = _silu(c_ref[...])
    o_ref[...] = _dot(s.astype(BF16), w_ref[...].astype(BF16)) + b_ref[...]


def _ada(c, w, b):
    rows, d = c.shape
    n = w.shape[1]
    tn = ADA_TILE
    return pl.pallas_call(
        _ada_kernel,
        grid=(n // tn,),
        in_specs=[pl.BlockSpec((rows, d), lambda j: (0, 0)),
                  pl.BlockSpec((d, tn), lambda j: (0, j)),
                  pl.BlockSpec((1, tn), lambda j: (0, j))],
        out_specs=pl.BlockSpec((rows, tn), lambda j: (0, j)),
        out_shape=jax.ShapeDtypeStruct((rows, n), F32),
        compiler_params=pltpu.CompilerParams(dimension_semantics=("arbitrary",),
                                             vmem_limit_bytes=VMEM_LIMIT),
        name="ada",
    )(c, w, b.reshape(1, n))


def _mixer_tail(x, pool_d, conv_act, mod, w, xmid_ref, h2_ref, lgt_ref):
    sh2, sc2, g1, g2 = mod
    pw = pool_d.shape[1]
    pool_out = _dot(pool_d, w["pool"][...]) * w["pool_scale"][...]
    mix = _dot(pool_out.astype(BF16), w["out"][:pw, :]) + _dot(conv_act, w["out"][pw:, :])
    x1 = x + g1 * mix
    h2f = _rms_norm(x1, w["g_ffn"][...]) * (1.0 + sc2) + sh2
    h2_ref[...] = _pack_bf16_pairs(h2f)
    h2 = h2f.astype(BF16)
    lgt_ref[...] = lax.dot_general(w["router_t"][...], h2, (((1,), (1,)), ((), ())),
                                   preferred_element_type=F32)
    gu = _dot(h2, w["s_gu"][...])
    de = gu.shape[1] // 2
    hs = _silu(gu[:, :de]) * gu[:, de:]
    shared = _dot(hs.astype(BF16), w["s_down"][...])
    xmid_ref[...] = x1 + g2 * shared


_W_NAMES = ("g_mix", "in", "pool", "pool_scale", "dw", "b_dw", "ln_g", "ln_b", "out", "g_ffn",
            "router_t", "s_gu", "s_down")


def _layer_norm_silu(yc, g, b):
    mu = jnp.mean(yc, axis=-1, keepdims=True)
    yz = yc - mu
    var = jnp.mean(yz * yz, axis=-1, keepdims=True)
    return _silu(yz * lax.rsqrt(var + EPS) * g + b)


U_HALO = 32
A_HALO = 32


def _prompt_mixer_kernel(x_ref, mod_ref, *refs, tl, d, pw, cw, conv_k, mod_row0):
    nw = len(_W_NAMES)
    w = dict(zip(_W_NAMES, refs[:nw]))
    wg_ref, wu_ref, wd_ref = refs[nw + 2:nw + 5]
    xmid_ref, h2_ref, lgt_ref, npool_ref, nconv_ref, wgu_bf_ref, wd_bf_ref = refs[-16:-9]
    ubuf, s2buf, s4buf, s8buf, abuf, ashift, dbuf, cbuf, rowb = refs[-9:]
    de = wg_ref.shape[2]
    wgu_bf_ref[:, :, :de] = wg_ref[...].astype(BF16)
    wgu_bf_ref[:, :, de:] = wu_ref[...].astype(BF16)
    wd_bf_ref[...] = wd_ref[...].astype(BF16)
    l = pl.program_id(1)
    pool_buf = max(POOL_WINDOWS) - 1
    uh, ah = U_HALO, A_HALO
    pg = pw // len(POOL_WINDOWS)
    nu, na = uh + tl, ah + tl

    @pl.when(l == 0)
    def _():
        ubuf[0:uh, :] = jnp.zeros((uh, pw), F32)
        abuf[0:ah, :] = jnp.zeros((ah, cw), F32)

    @pl.when(l > 0)
    def _():
        ubuf[0:uh, :] = ubuf[tl:tl + uh, :]
        abuf[0:ah, :] = abuf[tl:tl + ah, :]

    x = x_ref[0]
    row = pl.ds((mod_row0 + pl.program_id(0)) % SUBLANES, 1)
    sh1, sc1, g1, sh2, sc2, g2 = [mod_ref[row, i * d:(i + 1) * d] for i in range(6)]

    h = _rms_norm(x, w["g_mix"][...]) * (1.0 + sc1) + sh1
    z = _dot(h.astype(BF16), w["in"][...])
    u = z[:, :pw]
    ubuf[uh:nu, :] = u
    abuf[ah:na, :] = z[:, pw:pw + cw] * _sigmoid(z[:, pw + cw:])

    s2buf[8:nu, :] = ubuf[8:nu, :] + ubuf[7:nu - 1, :]
    s4buf[16:nu, :] = s2buf[16:nu, pg:] + s2buf[14:nu - 2, pg:]
    s8buf[24:nu, :] = s4buf[24:nu, pg:] + s4buf[20:nu - 4, pg:]
    t = l * tl + lax.broadcasted_iota(I32, (tl, 1), 0)
    sums = (s2buf[uh:nu, 0:pg], s4buf[uh:nu, 0:pg], s8buf[uh:nu, 0:pg],
            s8buf[uh:nu, pg:2 * pg] + s8buf[uh - 8:nu - 8, pg:2 * pg])
    for g, win in enumerate(POOL_WINDOWS):
        cols = slice(g * pg, (g + 1) * pg)
        inv = 1.0 / jnp.minimum(win, t + 1).astype(F32)
        dbuf[:, cols] = (sums[g] * inv - u[:, cols]).astype(BF16)

    for r in range(1, SUBLANES):
        ashift[r - 1, 8:na, :] = abuf[8 - r:na - r, :]

    for k in range(conv_k):
        rowb[k] = jnp.broadcast_to(w["dw"][k:k + 1, :], (SUBLANES, cw))
    rowb[conv_k] = jnp.broadcast_to(w["b_dw"][...], (SUBLANES, cw))

    for c in range(tl // ROW_CHUNK):
        groups = []
        for o in range(ah + c * ROW_CHUNK, ah + (c + 1) * ROW_CHUNK, SUBLANES):
            yc = abuf[o:o + SUBLANES, :] * rowb[conv_k - 1]
            for back in range(1, conv_k):
                q, r = divmod(back, SUBLANES)
                src = abuf if r == 0 else ashift.at[r - 1]
                yc = yc + src[o - q * SUBLANES:o - (q - 1) * SUBLANES, :] * rowb[conv_k - 1 - back]
            groups.append(yc + rowb[conv_k])
        r0 = c * ROW_CHUNK
        cbuf[r0:r0 + ROW_CHUNK, :] = _layer_norm_silu(
            jnp.concatenate(groups, axis=0), w["ln_g"][...], w["ln_b"][...]).astype(BF16)

    @pl.when(l == pl.num_programs(1) - 1)
    def _():
        npool_ref[0] = ubuf[nu - pool_buf:nu, :]
        nconv_ref[0] = abuf[na - (conv_k - 1):na, :]

    _mixer_tail(x, dbuf[...], cbuf[...], (sh2, sc2, g1, g2), w, xmid_ref.at[0], h2_ref, lgt_ref)


def _sample_mixer_kernel(x_ref, mod_ref, sp_ref, sc_ref, *refs, d, pw, cw, conv_k):
    nw = len(_W_NAMES)
    w = dict(zip(_W_NAMES, refs[:nw]))
    xmid_ref, npool_ref, nconv_ref = refs[nw:nw + 3]
    tails = refs[nw + 3:-2]
    h2_tmp, lgt_tmp = refs[-2:]
    pool_buf = max(POOL_WINDOWS) - 1
    pg = pw // len(POOL_WINDOWS)
    x = x_ref[...]
    sh1, sc1, g1, sh2, sc2, g2 = [mod_ref[:, i * d:(i + 1) * d] for i in range(6)]
    h = _rms_norm(x, w["g_mix"][...]) * (1.0 + sc1) + sh1
    z = _dot(h.astype(BF16), w["in"][...])
    u = z[:, :pw]
    a = z[:, pw:pw + cw] * _sigmoid(z[:, pw + cw:])
    npool_ref[0:pool_buf - 1] = sp_ref[1:pool_buf]
    npool_ref[pool_buf - 1] = u
    nconv_ref[0:conv_k - 2] = sc_ref[1:conv_k - 1]
    nconv_ref[conv_k - 2] = a
    ds = []
    for g, win in enumerate(POOL_WINDOWS):
        cols = slice(g * pg, (g + 1) * pg)
        acc = u[:, cols]
        for j in range(1, win):
            acc = acc + sp_ref[pool_buf - j, :, cols]
        cnt = float(min(win, 1 + pool_buf))
        ds.append(acc / cnt - u[:, cols])
    pool_d = jnp.concatenate(ds, axis=-1).astype(BF16)
    yc = a * w["dw"][conv_k - 1:conv_k, :]
    for k in range(conv_k - 1):
        yc = yc + sc_ref[k] * w["dw"][k:k + 1, :]
    yc = yc + w["b_dw"][...]
    conv_act = _layer_norm_silu(yc, w["ln_g"][...], w["ln_b"][...]).astype(BF16)
    _mixer_tail(x, pool_d, conv_act, (sh2, sc2, g1, g2), w, xmid_ref, h2_tmp, lgt_tmp)
    n_parts = len(tails) // 2
    share = x.shape[0] // n_parts
    for p in range(n_parts):
        h2_ref, lgt_ref = tails[2 * p], tails[2 * p + 1]
        h2_ref[...] = jnp.zeros_like(h2_ref)
        lgt_ref[...] = jnp.zeros_like(lgt_ref)
        h2_ref[0:share, :] = h2_tmp[p * share:(p + 1) * share, :]
        lgt_ref[:, 0:share] = lgt_tmp[:, p * share:(p + 1) * share]


def _full_spec(a):
    nd = a.ndim
    return pl.BlockSpec(a.shape, lambda *_: (0,) * nd)


def _prompt_mixer(x, mod, mod_row0, wl, h2_all, lgt_all, b0, b, w_experts, w_bf_prev, after, *, conv_k):
    n_batch, seq, d = x.shape
    w_gate, w_up, w_down = w_experts
    n_exp, _, de = w_gate.shape
    pw = wl["pool"].shape[0]
    cw = wl["dw"].shape[1]
    ne = wl["router_t"].shape[0]
    tl = SEQ_TILE
    nl = seq // tl
    pg = pw // len(POOL_WINDOWS)
    assert POOL_WINDOWS == (2, 4, 8, 16) and conv_k - 1 <= A_HALO
    ws = [wl[n] for n in _W_NAMES]
    kern = functools.partial(_prompt_mixer_kernel, tl=tl, d=d, pw=pw, cw=cw, conv_k=conv_k, mod_row0=mod_row0 + b0)
    ecs = n_exp // (n_batch * nl)
    assert ecs * n_batch * nl == n_exp

    def w_map(i, j):
        return ((i + b0) * nl + j, 0, 0)

    ins = [x, mod, *ws, h2_all, lgt_all, w_gate, w_up, w_down]
    in_specs = ([pl.BlockSpec((1, tl, d), lambda i, j: (i + b0, j, 0)),
                 pl.BlockSpec((SUBLANES, mod.shape[-1]), lambda i, j: ((i + b0 + mod_row0) // SUBLANES, 0))]
                + [_full_spec(a) for a in ws]
                + [pl.BlockSpec(memory_space=pl.ANY), pl.BlockSpec(memory_space=pl.ANY),
                   pl.BlockSpec((ecs, d, de), w_map), pl.BlockSpec((ecs, d, de), w_map),
                   pl.BlockSpec((ecs, de, d), w_map)])
    aliases = {2 + len(ws): 1, 3 + len(ws): 2}
    if w_bf_prev is not None:
        aliases.update({len(ins): 5, len(ins) + 1: 6})
        ins += list(w_bf_prev)
        in_specs += [pl.BlockSpec(memory_space=pl.ANY), pl.BlockSpec(memory_space=pl.ANY)]
    if after is not None:
        ins.append(after)
        in_specs.append(pl.BlockSpec(memory_space=pl.ANY))
    return pl.pallas_call(
        kern,
        grid=(b, nl),
        in_specs=in_specs,
        out_specs=[pl.BlockSpec((1, tl, d), lambda i, j: (i, j, 0)),
                   pl.BlockSpec((tl, d // 2), lambda i, j: (i * nl + j, 0)),
                   pl.BlockSpec((ne, tl), lambda i, j: (0, i * nl + j)),
                   pl.BlockSpec((1, max(POOL_WINDOWS) - 1, pw), lambda i, j: (i, 0, 0)),
                   pl.BlockSpec((1, conv_k - 1, cw), lambda i, j: (i, 0, 0)),
                   pl.BlockSpec((ecs, d, 2 * de), w_map), pl.BlockSpec((ecs, de, d), w_map)],
        out_shape=[jax.ShapeDtypeStruct((b, seq, d), F32),
                   jax.ShapeDtypeStruct(h2_all.shape, I32),
                   jax.ShapeDtypeStruct(lgt_all.shape, F32),
                   jax.ShapeDtypeStruct((b, max(POOL_WINDOWS) - 1, pw), F32),
                   jax.ShapeDtypeStruct((b, conv_k - 1, cw), F32),
                   jax.ShapeDtypeStruct((n_exp, d, 2 * de), BF16),
                   jax.ShapeDtypeStruct((n_exp, de, d), BF16)],
        scratch_shapes=[pltpu.VMEM((U_HALO + tl, pw), F32), pltpu.VMEM((U_HALO + tl, pw), F32),
                        pltpu.VMEM((U_HALO + tl, pw - pg), F32), pltpu.VMEM((U_HALO + tl, pw - 2 * pg), F32),
                        pltpu.VMEM((A_HALO + tl, cw), F32), pltpu.VMEM((SUBLANES - 1, A_HALO + tl, cw), F32),
                        pltpu.VMEM((tl, pw), BF16), pltpu.VMEM((tl, cw), BF16),
                        pltpu.VMEM((conv_k + 1, SUBLANES, cw), F32)],
        compiler_params=pltpu.CompilerParams(dimension_semantics=("arbitrary", "arbitrary"),
                                             vmem_limit_bytes=VMEM_LIMIT),
        input_output_aliases=aliases,
        name="prompt_mixer",
    )(*ins)


def _sample_mixer(x, mod, sp, sc, wl, t_prompt, t_pad, *, conv_k):
    rows, d = x.shape
    tail = t_pad - t_prompt
    assert t_prompt % tail == 0 and rows % N_PARTS == 0 and rows // N_PARTS <= tail
    pw = wl["pool"].shape[0]
    cw = wl["dw"].shape[1]
    ne = wl["router_t"].shape[0]
    ws = [wl[n] for n in _W_NAMES]
    kern = functools.partial(_sample_mixer_kernel, d=d, pw=pw, cw=cw, conv_k=conv_k)
    ins = [x, mod, sp, sc] + ws
    tail_specs = [pl.BlockSpec((tail, d // 2), lambda i: (t_prompt // tail, 0)),
                  pl.BlockSpec((ne, tail), lambda i: (0, t_prompt // tail))] * N_PARTS
    tail_shapes = [jax.ShapeDtypeStruct((t_pad, d // 2), I32), jax.ShapeDtypeStruct((ne, t_pad), F32)] * N_PARTS
    outs = pl.pallas_call(
        kern,
        grid=(1,),
        in_specs=[_full_spec(x), pl.BlockSpec((rows, mod.shape[1]), lambda i: (0, 0))]
                 + [_full_spec(a) for a in ins[2:]],
        out_specs=[pl.BlockSpec((rows, d), lambda i: (0, 0)), _full_spec(sp), _full_spec(sc)] + tail_specs,
        out_shape=[jax.ShapeDtypeStruct((rows, d), F32),
                   jax.ShapeDtypeStruct(sp.shape, F32),
                   jax.ShapeDtypeStruct(sc.shape, F32)] + tail_shapes,
        scratch_shapes=[pltpu.VMEM((rows, d // 2), I32), pltpu.VMEM((ne, rows), F32)],
        compiler_params=pltpu.CompilerParams(dimension_semantics=("arbitrary",),
                                             vmem_limit_bytes=VMEM_LIMIT),
        name="sample_mixer",
    )(*ins)
    return outs[0], outs[1], outs[2], [(outs[3 + 2 * p], outs[4 + 2 * p]) for p in range(N_PARTS)]


def _sublane_max(x):
    return jnp.max(x, axis=0, keepdims=True)


def _route_kernel(lg_ref, b_ref, tri_ref, idx_ref, wt_ref, rank_ref, cnt_ref, carry, *, t_valid):
    i = pl.program_id(0)
    ne, tt = lg_ref.shape
    ng = N_EXPERT_GROUPS
    per = ne // ng

    @pl.when(i == 0)
    def _():
        carry[...] = jnp.zeros_like(carry)

    s = _sigmoid(lg_ref[...])
    sel = s + b_ref[...]
    s3 = [s[p * ng:(p + 1) * ng, :] for p in range(per)]
    sel3 = [sel[p * ng:(p + 1) * ng, :] for p in range(per)]
    m1 = sel3[0]
    m2 = jnp.full_like(m1, NEG_INF)
    for p in range(1, per):
        m2 = jnp.maximum(m2, jnp.minimum(m1, sel3[p]))
        m1 = jnp.maximum(m1, sel3[p])
    gs = m1 + m2
    gi = lax.broadcasted_iota(jnp.int32, (ng, tt), 0)
    beaten = jnp.zeros((ng, tt), jnp.int32)
    for g in range(ng):
        row = gs[g:g + 1, :]
        beats = (row > gs) | ((row == gs) & (gi > g))
        beaten = beaten + beats.astype(jnp.int32)
    keep = beaten < TOPK_GROUPS
    cur = [jnp.where(keep, sel3[p], NEG_INF) for p in range(per)]
    eid = [(gi * per + p).astype(F32) for p in range(per)]
    idxs, wts, hits = [], [], []
    for _ in range(TOP_K):
        m = cur[0]
        for p in range(1, per):
            m = jnp.maximum(m, cur[p])
        m = _sublane_max(m)
        cand = jnp.where(cur[0] == m, eid[0], float(ne))
        for p in range(1, per):
            cand = jnp.minimum(cand, jnp.where(cur[p] == m, eid[p], float(ne)))
        e_sel = jnp.min(cand, axis=0, keepdims=True)
        hit = [eid[p] == e_sel for p in range(per)]
        wk = jnp.where(hit[0], s3[0], 0.0)
        for p in range(1, per):
            wk = wk + jnp.where(hit[p], s3[p], 0.0)
        wts.append(jnp.sum(wk, axis=0, keepdims=True))
        cur = [jnp.where(hit[p], NEG_INF, cur[p]) for p in range(per)]
        idxs.append(e_sel)
        hits.append(hit)
    wsum = wts[0]
    for k in range(1, TOP_K):
        wsum = wsum + wts[k]
    chosen = [functools.reduce(lambda a, b: a | b, [hits[k][p] for k in range(TOP_K)]) for p in range(per)]
    real = (i * tt + lax.broadcasted_iota(I32, (1, tt), 1)) < t_valid
    onehot = jnp.concatenate([(c & real).astype(F32) for c in chosen], axis=0)
    before = _dot(onehot.astype(BF16), tri_ref[...]) + carry[...]
    for k in range(TOP_K):
        rk = jnp.where(hits[k][0], before[0:ng, :], 0.0)
        for p in range(1, per):
            rk = rk + jnp.where(hits[k][p], before[p * ng:(p + 1) * ng, :], 0.0)
        rank_ref[k:k + 1, :] = jnp.sum(rk, axis=0, keepdims=True).astype(jnp.int32)
        idx_ref[k:k + 1, :] = idxs[k].astype(jnp.int32)
        wt_ref[k:k + 1, :] = wts[k] / wsum * ROUTED_SCALE
    carry[...] = carry[...] + jnp.sum(onehot, axis=1, keepdims=True)
    cnt_ref[...] = carry[...]


def _route(lgt, b_perm, t_valid):
    ne, t = lgt.shape
    tt = ROUTE_TILE
    tri = (jnp.arange(tt)[:, None] < jnp.arange(tt)[None, :]).astype(BF16)
    return pl.pallas_call(
        functools.partial(_route_kernel, t_valid=t_valid),
        grid=(t // tt,),
        in_specs=[pl.BlockSpec((ne, tt), lambda i: (0, i)),
                  pl.BlockSpec((ne, 1), lambda i: (0, 0)),
                  pl.BlockSpec((tt, tt), lambda i: (0, 0))],
        out_specs=[pl.BlockSpec((TOP_K, tt), lambda i: (0, i)),
                   pl.BlockSpec((TOP_K, tt), lambda i: (0, i)),
                   pl.BlockSpec((TOP_K, tt), lambda i: (0, i)),
                   pl.BlockSpec((ne, 1), lambda i: (0, 0))],
        out_shape=[jax.ShapeDtypeStruct((TOP_K, t), jnp.int32),
                   jax.ShapeDtypeStruct((TOP_K, t), F32),
                   jax.ShapeDtypeStruct((TOP_K, t), jnp.int32),
                   jax.ShapeDtypeStruct((ne, 1), F32)],
        scratch_shapes=[pltpu.VMEM((ne, 1), F32)],
        compiler_params=pltpu.CompilerParams(dimension_semantics=("arbitrary",),
                                             vmem_limit_bytes=VMEM_LIMIT),
        name="route",
    )(lgt, b_perm, tri)


def _dest_kernel(pstart_ref, idx_ref, rank_ref, dest_ref, *, t_valid, spare_row):
    idx = idx_ref[...]
    k_top, tile = idx.shape

    def add_start(e, acc):
        return acc + jnp.where(idx == e, pstart_ref[e], 0)

    dest = lax.fori_loop(0, pstart_ref.shape[0], add_start, rank_ref[...])
    tok = pl.program_id(0) * tile + lax.broadcasted_iota(I32, idx.shape, 1)
    spare = spare_row + (tok - t_valid) * k_top + lax.broadcasted_iota(I32, idx.shape, 0)
    dest_ref[...] = jnp.where(tok < t_valid, dest, spare)


def _dest_rows(pstart, idx, rank, t_valid, spare_row):
    k_top, t = idx.shape
    tile = DEST_TILE
    spec = pl.BlockSpec((k_top, tile), lambda i, ps: (0, i))
    return pl.pallas_call(
        functools.partial(_dest_kernel, t_valid=t_valid, spare_row=spare_row),
        grid_spec=pltpu.PrefetchScalarGridSpec(num_scalar_prefetch=1, grid=(t // tile,),
                                               in_specs=[spec, spec], out_specs=spec),
        out_shape=jax.ShapeDtypeStruct((k_top, t), I32),
        compiler_params=pltpu.CompilerParams(dimension_semantics=("arbitrary",),
                                             vmem_limit_bytes=VMEM_LIMIT),
        name="dest_rows",
    )(pstart, idx, rank)


def _expert_kernel(first_ref, valid_ref, xs_hbm, wgu_ref, wd_ref, y_hbm, xbuf, ybuf, in_sem, out_sem):
    step = pl.program_id(0)
    ne = first_ref.shape[0] - 1
    nbuf, rows = xbuf.shape[:2]
    piece = rows // EXPERT_PIECES
    per_step, de = wd_ref.shape[:2]
    n_total = first_ref[ne]

    def in_copy(g, h):
        return pltpu.make_async_copy(xs_hbm.at[pl.ds(g * rows + h * piece, piece)],
                                     xbuf.at[g % nbuf, pl.ds(h * piece, piece)], in_sem.at[g % nbuf])

    def out_copy(g, h):
        return pltpu.make_async_copy(ybuf.at[g % nbuf, pl.ds(h * piece, piece)],
                                     y_hbm.at[pl.ds(g * rows + h * piece, piece)], out_sem.at[g % nbuf])

    def used_pieces(copy, g, act):
        getattr(copy(g, 0), act)()
        for h in range(1, EXPERT_PIECES):
            @pl.when(valid_ref[g] > h * piece)
            def _():
                getattr(copy(g, h), act)()

    @pl.when(step == 0)
    def _():
        for g in range(nbuf - 1):
            @pl.when(g < n_total)
            def _():
                used_pieces(in_copy, g, "start")

    def block(g, carry, *, j):
        slot = g % nbuf

        @pl.when(g + nbuf - 1 < n_total)
        def _():
            used_pieces(in_copy, g + nbuf - 1, "start")

        used_pieces(in_copy, g, "wait")

        @pl.when(g >= nbuf)
        def _():
            used_pieces(out_copy, g - nbuf, "wait")

        valid = valid_ref[g]

        def run(n):
            rid = lax.broadcasted_iota(I32, (n, 1), 0)
            xs = jnp.where(rid < valid, _unpack_bf16_pairs(xbuf[slot, 0:n]), 0.0).astype(BF16)
            gu = _dot(xs, wgu_ref[j])
            hb = _silu(gu[:, :de]) * gu[:, de:]
            ybuf[slot, 0:n] = _pack_bf16_pairs(_dot(hb.astype(BF16), wd_ref[j]))

        for h in range(EXPERT_PIECES):
            @pl.when((valid > h * piece) & (valid <= (h + 1) * piece))
            def _():
                run((h + 1) * piece)

        used_pieces(out_copy, g, "start")
        return carry

    for j in range(per_step):
        e = step * per_step + j
        lax.fori_loop(first_ref[e], first_ref[e + 1], functools.partial(block, j=j), 0)

    @pl.when(step == pl.num_programs(0) - 1)
    def _():
        for back in range(nbuf, 0, -1):
            @pl.when(n_total >= back)
            def _():
                used_pieces(out_copy, n_total - back, "wait")


def _experts(xs, w_gu, w_down, first_block, block_valid):
    p, dw = xs.shape
    ne, de, d = w_down.shape
    return pl.pallas_call(
        _expert_kernel,
        grid_spec=pltpu.PrefetchScalarGridSpec(
            num_scalar_prefetch=2,
            grid=(ne // EXPERTS_PER_STEP,),
            in_specs=[pl.BlockSpec(memory_space=pl.ANY),
                      pl.BlockSpec((EXPERTS_PER_STEP, d, 2 * de), lambda e, fb, cn: (e, 0, 0)),
                      pl.BlockSpec((EXPERTS_PER_STEP, de, d), lambda e, fb, cn: (e, 0, 0))],
            out_specs=pl.BlockSpec(memory_space=pl.ANY),
            scratch_shapes=[pltpu.VMEM((EXPERT_BUFFERS, EXPERT_ROWS, dw), I32),
                            pltpu.VMEM((EXPERT_BUFFERS, EXPERT_ROWS, dw), I32),
                            pltpu.SemaphoreType.DMA((EXPERT_BUFFERS,)), pltpu.SemaphoreType.DMA((EXPERT_BUFFERS,))]),
        out_shape=jax.ShapeDtypeStruct((p, dw), I32),
        compiler_params=pltpu.CompilerParams(dimension_semantics=("arbitrary",),
                                             vmem_limit_bytes=VMEM_LIMIT),
        name="experts",
    )(first_block, block_valid, xs, w_gu, w_down)


def _sc_first_chunk(n_chunks):
    per_worker = n_chunks // SC_WORKERS
    assert per_worker * SC_WORKERS == n_chunks
    return per_worker, (lax.axis_index("s") * SC_CORES + lax.axis_index("c")) * per_worker


def _sc_start(copies):
    for cp in copies:
        cp.start()


def _sc_wait(copies):
    for cp in copies:
        cp.wait()


def _sc_dispatch(h2, dest, n_rows):
    w = h2.shape[1]
    n_chunks, k_top, ch = dest.shape
    mesh = plsc.VectorSubcoreMesh(core_axis_name="c", subcore_axis_name="s")

    @functools.partial(
        pl.kernel, mesh=mesh, out_type=jax.ShapeDtypeStruct((n_rows, w), I32),
        scratch_types=[pltpu.VMEM((2, k_top, ch), I32), pltpu.VMEM((2, ch, w), I32),
                       pltpu.SemaphoreType.DMA((2,)), pltpu.SemaphoreType.DMA((2,))])
    def dispatch(h2_hbm, dest_hbm, xs_hbm, idx_v, rows_v, load_sem, store_sem):
        per_worker, first = _sc_first_chunk(n_chunks)

        def loads(i):
            slot = i % 2
            return (pltpu.make_async_copy(h2_hbm.at[pl.ds((first + i) * ch, ch)], rows_v.at[slot], load_sem.at[slot]),
                    pltpu.make_async_copy(dest_hbm.at[first + i], idx_v.at[slot], load_sem.at[slot]))

        def scatters(i):
            slot = i % 2
            return [pltpu.make_async_copy(rows_v.at[slot], xs_hbm.at[idx_v.at[slot, k]], store_sem.at[slot])
                    for k in range(k_top)]

        for i in range(min(2, per_worker)):
            _sc_start(loads(i))
        for i in range(per_worker):
            _sc_wait(loads(i))
            _sc_start(scatters(i))
            if 1 <= i < per_worker - 1:
                _sc_wait(scatters(i - 1))
                _sc_start(loads(i + 1))
        for i in range(max(per_worker - 2, 0), per_worker):
            _sc_wait(scatters(i))

    return dispatch(h2, dest)


def _sc_combine_gather(y, dest, t):
    w = y.shape[1]
    n_chunks, k_top, ch = dest.shape
    n_local = n_chunks // SC_WORKERS
    mesh = plsc.VectorSubcoreMesh(core_axis_name="c", subcore_axis_name="s")

    @functools.partial(
        pl.kernel, mesh=mesh, out_type=jax.ShapeDtypeStruct((k_top, t, w), I32),
        scratch_types=[pltpu.VMEM((n_local, k_top, ch), I32), pltpu.VMEM((k_top, ch, w), I32),
                       pltpu.SemaphoreType.DMA((k_top,)), pltpu.SemaphoreType.DMA((k_top,))])
    def gather(y_hbm, dest_hbm, out_hbm, idx_v, rows_v, load_sem, store_sem):
        per_worker, first = _sc_first_chunk(n_chunks)
        pltpu.sync_copy(dest_hbm.at[pl.ds(first, per_worker)], idx_v)

        def fetch(i, k):
            return pltpu.make_async_copy(y_hbm.at[idx_v.at[i, k]], rows_v.at[k], load_sem.at[k])

        def put(i, k):
            return pltpu.make_async_copy(rows_v.at[k], out_hbm.at[k, pl.ds((first + i) * ch, ch)], store_sem.at[k])

        @pl.loop(0, per_worker)
        def _(i):
            for k in range(k_top):
                @pl.when(i > 0)
                def _():
                    put(i - 1, k).wait()

                fetch(i, k).start()
            for k in range(k_top):
                fetch(i, k).wait()
                put(i, k).start()

        for k in range(k_top):
            put(per_worker - 1, k).wait()

    return gather(y, dest)


def _final_kernel(xmid_ref, yg_ref, wt_ref, g2_ref, shf_ref, scf_ref, gf_ref, *rest, vec_row0=None):
    o_ref = rest[-1]
    if vec_row0 is not None:
        row = pl.ds((vec_row0 + pl.program_id(0)) % SUBLANES, 1)
        g2_ref, shf_ref, scf_ref = g2_ref.at[row, :], shf_ref.at[row, :], scf_ref.at[row, :]
    wt = wt_ref[...].T[:xmid_ref.shape[0]]
    routed = wt[:, 0:1] * _unpack_bf16_pairs(yg_ref[0])
    for k in range(1, yg_ref.shape[0]):
        routed = routed + wt[:, k:k + 1] * _unpack_bf16_pairs(yg_ref[k])
    x2 = xmid_ref[...] + g2_ref[...] * routed
    o_ref[...] = _rms_norm(x2, gf_ref[...]) * (1.0 + scf_ref[...]) + shf_ref[...]


def _final_prompt(xmid, yg, wts_t, mod, modf, mod_row0, g_final, b0, n_batch, out_prev):
    b, seq, d = xmid.shape
    k_top, _, w = yg.shape
    tl = FINAL_TILE
    nl = seq // tl

    def vec(col):
        return pl.BlockSpec((SUBLANES, d), lambda i, j: ((i + b0 + mod_row0) // SUBLANES, col))

    in_specs = [pl.BlockSpec((None, tl, d), lambda i, j: (i, j, 0)),
                pl.BlockSpec((k_top, tl, w), lambda i, j: (0, i * nl + j, 0)),
                pl.BlockSpec((k_top, tl), lambda i, j: (0, i * nl + j)),
                vec(5), vec(0), vec(1), pl.BlockSpec((1, d), lambda i, j: (0, 0))]
    args = [xmid, yg, wts_t, mod, modf, modf, g_final]
    aliases = {}
    if out_prev is not None:
        in_specs.append(pl.BlockSpec(memory_space=pl.ANY))
        args.append(out_prev)
        aliases = {7: 0}
    return pl.pallas_call(
        functools.partial(_final_kernel, vec_row0=b0 + mod_row0),
        grid=(b, nl),
        in_specs=in_specs,
        out_specs=pl.BlockSpec((None, tl, d), lambda i, j: (i + b0, j, 0)),
        out_shape=jax.ShapeDtypeStruct((n_batch, seq, d), F32),
        compiler_params=pltpu.CompilerParams(dimension_semantics=("arbitrary", "arbitrary"),
                                             vmem_limit_bytes=VMEM_LIMIT),
        input_output_aliases=aliases,
        name="final_prompt",
    )(*args)


def _final_sample(xmid_all, yg, wts_t, mod, modf, g_final, part, rows, first_row, after):
    d = xmid_all.shape[1]
    k_top, _, w = yg.shape

    def vec(col):
        return pl.BlockSpec((rows, d), lambda i: (part, col))

    return pl.pallas_call(
        _final_kernel,
        grid=(1,),
        in_specs=[vec(0),
                  pl.BlockSpec((k_top, rows, w), lambda i: (0, first_row // rows, 0)),
                  pl.BlockSpec((k_top, LANES), lambda i: (0, first_row // LANES)),
                  vec(5), vec(0), vec(1), pl.BlockSpec((1, d), lambda i: (0, 0)), pl.BlockSpec(memory_space=pl.ANY)],
        out_specs=pl.BlockSpec((rows, d), lambda i: (0, 0)),
        out_shape=jax.ShapeDtypeStruct((rows, d), F32),
        compiler_params=pltpu.CompilerParams(dimension_semantics=("arbitrary",),
                                             vmem_limit_bytes=VMEM_LIMIT),
        name="final_sample",
    )(xmid_all, yg, wts_t, mod, modf, modf, g_final, after)


def _block_diag_pairs(w_pool):
    g, c, _ = w_pool.shape
    eye = jnp.eye(g, dtype=w_pool.dtype)
    return (eye[:, None, :, None] * w_pool[:, :, None, :]).reshape(g * c, g * c)


def kernel(x_prompt, x_sample, state_pool, state_conv, c_prompt, c_sample, w_ada, b_ada, g_mix, w_in, w_pool, pool_scale, w_dw, b_dw, ln_g, ln_b, w_out, g_ffn, w_router, b_router, w_gate, w_up, w_down, ws_gate, ws_up, ws_down, w_ada_final, b_ada_final, g_final):
    bp, seq, d = x_prompt.shape
    bs = x_sample.shape[0]
    depth = w_ada.shape[0]
    assert depth == 1 and x_sample.shape[1] == 1
    conv_k = w_dw.shape[1]
    ne = w_router.shape[-1]
    per = ne // N_EXPERT_GROUPS

    row_expert = (jnp.arange(ne) % N_EXPERT_GROUPS) * per + jnp.arange(ne) // N_EXPERT_GROUPS
    wl = {
        "g_mix": g_mix[0][None, :],
        "in": w_in[0].astype(BF16),
        "pool": _block_diag_pairs(w_pool[0]).astype(BF16),
        "pool_scale": pool_scale[0][None, :],
        "dw": w_dw[0],
        "b_dw": b_dw[0][None, :],
        "ln_g": ln_g[0][None, :],
        "ln_b": ln_b[0][None, :],
        "out": w_out[0].astype(BF16),
        "g_ffn": g_ffn[0][None, :],
        "router_t": w_router[0].T[row_expert].astype(BF16),
        "s_gu": jnp.concatenate([ws_gate[0], ws_up[0]], axis=1).astype(BF16),
        "s_down": ws_down[0].astype(BF16),
    }
    b_perm = b_router[0][row_expert][:, None]

    assert bp % N_PARTS == 0 and bs % N_PARTS == 0
    bpp, bsp = bp // N_PARTS, bs // N_PARTS
    tpp = bpp * seq
    t_part = tpp + bsp
    grain = SC_WORKERS * SC_DISPATCH_CHUNK * SC_COMBINE_CHUNK // math.gcd(SC_DISPATCH_CHUNK, SC_COMBINE_CHUNK)
    t_pad = (t_part + grain - 1) // grain * grain
    assert t_pad % ROUTE_TILE == 0 and t_pad % DEST_TILE == 0 and tpp % bsp == 0

    c_all = jnp.concatenate([c_sample, c_prompt], axis=0)
    mod = _ada(c_all, w_ada[0], b_ada[0])
    modf = _ada(c_all, w_ada_final, b_ada_final)

    xmid_s, new_pool_t, new_conv_t, tails = _sample_mixer(
        x_sample.reshape(bs, d), mod, jnp.transpose(state_pool[0], (1, 0, 2)),
        jnp.transpose(state_conv[0], (1, 0, 2)), wl, tpp, t_pad, conv_k=conv_k)
    new_pool_s = jnp.transpose(new_pool_t, (1, 0, 2))
    new_conv_s = jnp.transpose(new_conv_t, (1, 0, 2))
    w_experts = tuple(a.reshape(a.shape[1:]) for a in (w_gate, w_up, w_down))

    n_blocks = (t_part * TOP_K + EXPERT_ROWS - 1) // EXPERT_ROWS + ne
    n_spare = -(-(t_pad - t_part) * TOP_K // EXPERT_ROWS)
    n_rows = (n_blocks + n_spare) * EXPERT_ROWS

    y_prompt, y_samples, npools, nconvs, w_bf = None, [], [], [], None
    mixed, dest = [], None
    for p in range(N_PARTS):
        h2, lgt = tails[p]
        xmid_p, h2, lgt, npool_p, nconv_p, *w_bf = _prompt_mixer(
            x_prompt, mod, bs, wl, h2, lgt, p * bpp, bpp, w_experts, w_bf, dest, conv_k=conv_k)
        npools.append(npool_p)
        nconvs.append(nconv_p)
        idx, wts, rank, counts_perm = _route(lgt, b_perm, t_part)
        counts = jnp.zeros((ne,), I32).at[row_expert].set(counts_perm[:, 0].astype(I32))
        nblk = (counts + EXPERT_ROWS - 1) // EXPERT_ROWS
        first_block = jnp.concatenate([jnp.zeros((1,), I32), jnp.cumsum(nblk).astype(I32)])
        dest = _dest_rows(first_block[:ne] * EXPERT_ROWS, idx, rank, t_part, n_blocks * EXPERT_ROWS)
        blk = jnp.arange(n_blocks, dtype=I32)[:, None]
        inside = (blk >= first_block[None, :ne]) & (blk < first_block[None, 1:])
        rows_left = counts[None, :] - (blk - first_block[None, :ne]) * EXPERT_ROWS
        block_valid = jnp.sum(jnp.where(inside, jnp.minimum(rows_left, EXPERT_ROWS), 0), axis=1)
        mixed.append((xmid_p, h2, wts, dest, first_block, block_valid))
    for p in range(N_PARTS):
        xmid_p, h2, wts, dest, first_block, block_valid = mixed[p]

        def chunked(ch):
            return dest.reshape(TOP_K, t_pad // ch, ch).transpose(1, 0, 2)

        xs = _sc_dispatch(h2, chunked(SC_DISPATCH_CHUNK), n_rows)
        y = _experts(xs, w_bf[0], w_bf[1], first_block, block_valid)
        yg = _sc_combine_gather(y, chunked(SC_COMBINE_CHUNK), t_pad)

        y_samples.append(_final_sample(xmid_s, yg, wts, mod, modf, g_final[None, :], p, bsp, tpp,
                                       g_final if y_prompt is None else y_prompt))
        y_prompt = _final_prompt(xmid_p, yg, wts, mod, modf, bs, g_final[None, :], p * bpp, bp, y_prompt)
    y_sample = jnp.concatenate(y_samples, axis=0)
    npool_p = jnp.concatenate(npools, axis=0)
    nconv_p = jnp.concatenate(nconvs, axis=0)

    return (y_prompt, y_sample[:, None, :], npool_p[None], nconv_p[None], new_pool_s[None], new_conv_s[None])
```

```python
import functools
import math

import jax
import jax.numpy as jnp
from jax import lax
from jax.experimental import pallas as pl
from jax.experimental.pallas import tpu as pltpu
from jax.experimental.pallas import tpu_sc as plsc

POOL_WINDOWS = (2, 4, 8, 16)
N_EXPERT_GROUPS = 8
TOPK_GROUPS = 4
TOP_K = 8
ROUTED_SCALE = 2.5
EPS = 1e-6

LANES = 128
SUBLANES = 8
VMEM_LIMIT = 52 * 1024 * 1024

ADA_TILE = 1024
SEQ_TILE = 512
ROW_CHUNK = 16
ROUTE_TILE = 1408
EXPERT_ROWS = 1024
N_PARTS = 2
EXPERT_BUFFERS = 4
EXPERT_PIECES = 8
EXPERTS_PER_STEP = 2
FINAL_TILE = 512
DEST_TILE = 2816

SC_CORES = 2
SC_SUBCORES = 16
SC_WORKERS = SC_CORES * SC_SUBCORES
SC_DISPATCH_CHUNK = 88
SC_COMBINE_CHUNK = 24

F32 = jnp.float32
BF16 = jnp.bfloat16
I32 = jnp.int32
U32 = jnp.uint32
NEG_INF = float("-inf")
HI16 = 0xFFFF0000


def _sigmoid(x):
    return 1.0 / (1.0 + jnp.exp(-x))


def _silu(x):
    return x * _sigmoid(x)


def _rms_norm(x, g):
    return x * lax.rsqrt(jnp.mean(x * x, axis=-1, keepdims=True) + EPS) * g


def _dot(a, b):
    return jnp.dot(a, b, preferred_element_type=F32)


def _pack_bf16_pairs(x):
    w = x.shape[1] // 2
    bits = lax.bitcast_convert_type(x.astype(BF16).astype(F32), U32)
    return lax.bitcast_convert_type((bits[:, :w] >> 16) | (bits[:, w:] & U32(HI16)), I32)


def _unpack_bf16_pairs(p):
    bits = lax.bitcast_convert_type(p, U32)
    lo = lax.bitcast_convert_type(bits << 16, F32)
    hi = lax.bitcast_convert_type(bits & U32(HI16), F32)
    return jnp.concatenate([lo, hi], axis=1)


def _ada_kernel(c_ref, w_ref, b_ref, o_ref):
    s = _silu(c_ref[...])
    o_ref[...] = _dot(s.astype(BF16), w_ref[...].astype(BF16)) + b_ref[...]


def _ada(c, w, b):
    rows, d = c.shape
    n = w.shape[1]
    tn = ADA_TILE
    return pl.pallas_call(
        _ada_kernel,
        grid=(n // tn,),
        in_specs=[pl.BlockSpec((rows, d), lambda j: (0, 0)),
                  pl.BlockSpec((d, tn), lambda j: (0, j)),
                  pl.BlockSpec((1, tn), lambda j: (0, j))],
        out_specs=pl.BlockSpec((rows, tn), lambda j: (0, j)),
        out_shape=jax.ShapeDtypeStruct((rows, n), F32),
        compiler_params=pltpu.CompilerParams(dimension_semantics=("arbitrary",),
                                             vmem_limit_bytes=VMEM_LIMIT),
        name="ada",
    )(c, w, b.reshape(1, n))


def _mixer_tail(x, pool_d, conv_act, mod, w, xmid_ref, h2_ref, lgt_ref):
    sh2, sc2, g1, g2 = mod
    pw = pool_d.shape[1]
    pool_out = _dot(pool_d, w["pool"][...]) * w["pool_scale"][...]
    mix = _dot(pool_out.astype(BF16), w["out"][:pw, :]) + _dot(conv_act, w["out"][pw:, :])
    x1 = x + g1 * mix
    h2f = _rms_norm(x1, w["g_ffn"][...]) * (1.0 + sc2) + sh2
    h2_ref[...] = _pack_bf16_pairs(h2f)
    h2 = h2f.astype(BF16)
    lgt_ref[...] = lax.dot_general(w["router_t"][...], h2, (((1,), (1,)), ((), ())),
                                   preferred_element_type=F32)
    gu = _dot(h2, w["s_gu"][...])
    de = gu.shape[1] // 2
    hs = _silu(gu[:, :de]) * gu[:, de:]
    shared = _dot(hs.astype(BF16), w["s_down"][...])
    xmid_ref[...] = x1 + g2 * shared


_W_NAMES = ("g_mix", "in", "pool", "pool_scale", "dw", "b_dw", "ln_g", "ln_b", "out", "g_ffn",
            "router_t", "s_gu", "s_down")


def _layer_norm_silu(yc, g, b):
    mu = jnp.mean(yc, axis=-1, keepdims=True)
    yz = yc - mu
    var = jnp.mean(yz * yz, axis=-1, keepdims=True)
    return _silu(yz * lax.rsqrt(var + EPS) * g + b)


U_HALO = 32
A_HALO = 32


def _prompt_mixer_kernel(x_ref, mod_ref, *refs, tl, d, pw, cw, conv_k, mod_row0):
    nw = len(_W_NAMES)
    w = dict(zip(_W_NAMES, refs[:nw]))
    wg_ref, wu_ref, wd_ref = refs[nw + 2:nw + 5]
    xmid_ref, h2_ref, lgt_ref, npool_ref, nconv_ref, wgu_bf_ref, wd_bf_ref = refs[-16:-9]
    ubuf, s2buf, s4buf, s8buf, abuf, ashift, dbuf, cbuf, rowb = refs[-9:]
    de = wg_ref.shape[2]
    wgu_bf_ref[:, :, :de] = wg_ref[...].astype(BF16)
    wgu_bf_ref[:, :, de:] = wu_ref[...].astype(BF16)
    wd_bf_ref[...] = wd_ref[...].astype(BF16)
    l = pl.program_id(1)
    pool_buf = max(POOL_WINDOWS) - 1
    uh, ah = U_HALO, A_HALO
    pg = pw // len(POOL_WINDOWS)
    nu, na = uh + tl, ah + tl

    @pl.when(l == 0)
    def _():
        ubuf[0:uh, :] = jnp.zeros((uh, pw), F32)
        abuf[0:ah, :] = jnp.zeros((ah, cw), F32)

    @pl.when(l > 0)
    def _():
        ubuf[0:uh, :] = ubuf[tl:tl + uh, :]
        abuf[0:ah, :] = abuf[tl:tl + ah, :]

    x = x_ref[0]
    row = pl.ds((mod_row0 + pl.program_id(0)) % SUBLANES, 1)
    sh1, sc1, g1, sh2, sc2, g2 = [mod_ref[row, i * d:(i + 1) * d] for i in range(6)]

    h = _rms_norm(x, w["g_mix"][...]) * (1.0 + sc1) + sh1
    z = _dot(h.astype(BF16), w["in"][...])
    u = z[:, :pw]
    ubuf[uh:nu, :] = u
    abuf[ah:na, :] = z[:, pw:pw + cw] * _sigmoid(z[:, pw + cw:])

    s2buf[8:nu, :] = ubuf[8:nu, :] + ubuf[7:nu - 1, :]
    s4buf[16:nu, :] = s2buf[16:nu, pg:] + s2buf[14:nu - 2, pg:]
    s8buf[24:nu, :] = s4buf[24:nu, pg:] + s4buf[20:nu - 4, pg:]
    t = l * tl + lax.broadcasted_iota(I32, (tl, 1), 0)
    sums = (s2buf[uh:nu, 0:pg], s4buf[uh:nu, 0:pg], s8buf[uh:nu, 0:pg],
            s8buf[uh:nu, pg:2 * pg] + s8buf[uh - 8:nu - 8, pg:2 * pg])
    for g, win in enumerate(POOL_WINDOWS):
        cols = slice(g * pg, (g + 1) * pg)
        inv = 1.0 / jnp.minimum(win, t + 1).astype(F32)
        dbuf[:, cols] = (sums[g] * inv - u[:, cols]).astype(BF16)

    for r in range(1, SUBLANES):
        ashift[r - 1, 8:na, :] = abuf[8 - r:na - r, :]

    for k in range(conv_k):
        rowb[k] = jnp.broadcast_to(w["dw"][k:k + 1, :], (SUBLANES, cw))
    rowb[conv_k] = jnp.broadcast_to(w["b_dw"][...], (SUBLANES, cw))

    for c in range(tl // ROW_CHUNK):
        groups = []
        for o in range(ah + c * ROW_CHUNK, ah + (c + 1) * ROW_CHUNK, SUBLANES):
            yc = abuf[o:o + SUBLANES, :] * rowb[conv_k - 1]
            for back in range(1, conv_k):
                q, r = divmod(back, SUBLANES)
                src = abuf if r == 0 else ashift.at[r - 1]
                yc = yc + src[o - q * SUBLANES:o - (q - 1) * SUBLANES, :] * rowb[conv_k - 1 - back]
            groups.append(yc + rowb[conv_k])
        r0 = c * ROW_CHUNK
        cbuf[r0:r0 + ROW_CHUNK, :] = _layer_norm_silu(
            jnp.concatenate(groups, axis=0), w["ln_g"][...], w["ln_b"][...]).astype(BF16)

    @pl.when(l == pl.num_programs(1) - 1)
    def _():
        npool_ref[0] = ubuf[nu - pool_buf:nu, :]
        nconv_ref[0] = abuf[na - (conv_k - 1):na, :]

    _mixer_tail(x, dbuf[...], cbuf[...], (sh2, sc2, g1, g2), w, xmid_ref.at[0], h2_ref, lgt_ref)


def _sample_mixer_kernel(x_ref, mod_ref, sp_ref, sc_ref, *refs, d, pw, cw, conv_k):
    nw = len(_W_NAMES)
    w = dict(zip(_W_NAMES, refs[:nw]))
    xmid_ref, npool_ref, nconv_ref = refs[nw:nw + 3]
    tails = refs[nw + 3:-2]
    h2_tmp, lgt_tmp = refs[-2:]
    pool_buf = max(POOL_WINDOWS) - 1
    pg = pw // len(POOL_WINDOWS)
    x = x_ref[...]
    sh1, sc1, g1, sh2, sc2, g2 = [mod_ref[:, i * d:(i + 1) * d] for i in range(6)]
    h = _rms_norm(x, w["g_mix"][...]) * (1.0 + sc1) + sh1
    z = _dot(h.astype(BF16), w["in"][...])
    u = z[:, :pw]
    a = z[:, pw:pw + cw] * _sigmoid(z[:, pw + cw:])
    npool_ref[0:pool_buf - 1] = sp_ref[1:pool_buf]
    npool_ref[pool_buf - 1] = u
    nconv_ref[0:conv_k - 2] = sc_ref[1:conv_k - 1]
    nconv_ref[conv_k - 2] = a
    ds = []
    for g, win in enumerate(POOL_WINDOWS):
        cols = slice(g * pg, (g + 1) * pg)
        acc = u[:, cols]
        for j in range(1, win):
            acc = acc + sp_ref[pool_buf - j, :, cols]
        cnt = float(min(win, 1 + pool_buf))
        ds.append(acc / cnt - u[:, cols])
    pool_d = jnp.concatenate(ds, axis=-1).astype(BF16)
    yc = a * w["dw"][conv_k - 1:conv_k, :]
    for k in range(conv_k - 1):
        yc = yc + sc_ref[k] * w["dw"][k:k + 1, :]
    yc = yc + w["b_dw"][...]
    conv_act = _layer_norm_silu(yc, w["ln_g"][...], w["ln_b"][...]).astype(BF16)
    _mixer_tail(x, pool_d, conv_act, (sh2, sc2, g1, g2), w, xmid_ref, h2_tmp, lgt_tmp)
    n_parts = len(tails) // 2
    share = x.shape[0] // n_parts
    for p in range(n_parts):
        h2_ref, lgt_ref = tails[2 * p], tails[2 * p + 1]
        h2_ref[...] = jnp.zeros_like(h2_ref)
        lgt_ref[...] = jnp.zeros_like(lgt_ref)
        h2_ref[0:share, :] = h2_tmp[p * share:(p + 1) * share, :]
        lgt_ref[:, 0:share] = lgt_tmp[:, p * share:(p + 1) * share]


def _full_spec(a):
    nd = a.ndim
    return pl.BlockSpec(a.shape, lambda *_: (0,) * nd)


def _prompt_mixer(x, mod, mod_row0, wl, h2_all, lgt_all, b0, b, w_experts, w_bf_prev, after, *, conv_k):
    n_batch, seq, d = x.shape
    w_gate, w_up, w_down = w_experts
    n_exp, _, de = w_gate.shape
    pw = wl["pool"].shape[0]
    cw = wl["dw"].shape[1]
    ne = wl["router_t"].shape[0]
    tl = SEQ_TILE
    nl = seq // tl
    pg = pw // len(POOL_WINDOWS)
    assert POOL_WINDOWS == (2, 4, 8, 16) and conv_k - 1 <= A_HALO
    ws = [wl[n] for n in _W_NAMES]
    kern = functools.partial(_prompt_mixer_kernel, tl=tl, d=d, pw=pw, cw=cw, conv_k=conv_k, mod_row0=mod_row0 + b0)
    ecs = n_exp // (n_batch * nl)
    assert ecs * n_batch * nl == n_exp

    def w_map(i, j):
        return ((i + b0) * nl + j, 0, 0)

    ins = [x, mod, *ws, h2_all, lgt_all, w_gate, w_up, w_down]
    in_specs = ([pl.BlockSpec((1, tl, d), lambda i, j: (i + b0, j, 0)),
                 pl.BlockSpec((SUBLANES, mod.shape[-1]), lambda i, j: ((i + b0 + mod_row0) // SUBLANES, 0))]
                + [_full_spec(a) for a in ws]
                + [pl.BlockSpec(memory_space=pl.ANY), pl.BlockSpec(memory_space=pl.ANY),
                   pl.BlockSpec((ecs, d, de), w_map), pl.BlockSpec((ecs, d, de), w_map),
                   pl.BlockSpec((ecs, de, d), w_map)])
    aliases = {2 + len(ws): 1, 3 + len(ws): 2}
    if w_bf_prev is not None:
        aliases.update({len(ins): 5, len(ins) + 1: 6})
        ins += list(w_bf_prev)
        in_specs += [pl.BlockSpec(memory_space=pl.ANY), pl.BlockSpec(memory_space=pl.ANY)]
    if after is not None:
        ins.append(after)
        in_specs.append(pl.BlockSpec(memory_space=pl.ANY))
    return pl.pallas_call(
        kern,
        grid=(b, nl),
        in_specs=in_specs,
        out_specs=[pl.BlockSpec((1, tl, d), lambda i, j: (i, j, 0)),
                   pl.BlockSpec((tl, d // 2), lambda i, j: (i * nl + j, 0)),
                   pl.BlockSpec((ne, tl), lambda i, j: (0, i * nl + j)),
                   pl.BlockSpec((1, max(POOL_WINDOWS) - 1, pw), lambda i, j: (i, 0, 0)),
                   pl.BlockSpec((1, conv_k - 1, cw), lambda i, j: (i, 0, 0)),
                   pl.BlockSpec((ecs, d, 2 * de), w_map), pl.BlockSpec((ecs, de, d), w_map)],
        out_shape=[jax.ShapeDtypeStruct((b, seq, d), F32),
                   jax.ShapeDtypeStruct(h2_all.shape, I32),
                   jax.ShapeDtypeStruct(lgt_all.shape, F32),
                   jax.ShapeDtypeStruct((b, max(POOL_WINDOWS) - 1, pw), F32),
                   jax.ShapeDtypeStruct((b, conv_k - 1, cw), F32),
                   jax.ShapeDtypeStruct((n_exp, d, 2 * de), BF16),
                   jax.ShapeDtypeStruct((n_exp, de, d), BF16)],
        scratch_shapes=[pltpu.VMEM((U_HALO + tl, pw), F32), pltpu.VMEM((U_HALO + tl, pw), F32),
                        pltpu.VMEM((U_HALO + tl, pw - pg), F32), pltpu.VMEM((U_HALO + tl, pw - 2 * pg), F32),
                        pltpu.VMEM((A_HALO + tl, cw), F32), pltpu.VMEM((SUBLANES - 1, A_HALO + tl, cw), F32),
                        pltpu.VMEM((tl, pw), BF16), pltpu.VMEM((tl, cw), BF16),
                        pltpu.VMEM((conv_k + 1, SUBLANES, cw), F32)],
        compiler_params=pltpu.CompilerParams(dimension_semantics=("arbitrary", "arbitrary"),
                                             vmem_limit_bytes=VMEM_LIMIT),
        input_output_aliases=aliases,
        name="prompt_mixer",
    )(*ins)


def _sample_mixer(x, mod, sp, sc, wl, t_prompt, t_pad, *, conv_k):
    rows, d = x.shape
    tail = t_pad - t_prompt
    assert t_prompt % tail == 0 and rows % N_PARTS == 0 and rows // N_PARTS <= tail
    pw = wl["pool"].shape[0]
    cw = wl["dw"].shape[1]
    ne = wl["router_t"].shape[0]
    ws = [wl[n] for n in _W_NAMES]
    kern = functools.partial(_sample_mixer_kernel, d=d, pw=pw, cw=cw, conv_k=conv_k)
    ins = [x, mod, sp, sc] + ws
    tail_specs = [pl.BlockSpec((tail, d // 2), lambda i: (t_prompt // tail, 0)),
                  pl.BlockSpec((ne, tail), lambda i: (0, t_prompt // tail))] * N_PARTS
    tail_shapes = [jax.ShapeDtypeStruct((t_pad, d // 2), I32), jax.ShapeDtypeStruct((ne, t_pad), F32)] * N_PARTS
    outs = pl.pallas_call(
        kern,
        grid=(1,),
        in_specs=[_full_spec(x), pl.BlockSpec((rows, mod.shape[1]), lambda i: (0, 0))]
                 + [_full_spec(a) for a in ins[2:]],
        out_specs=[pl.BlockSpec((rows, d), lambda i: (0, 0)), _full_spec(sp), _full_spec(sc)] + tail_specs,
        out_shape=[jax.ShapeDtypeStruct((rows, d), F32),
                   jax.ShapeDtypeStruct(sp.shape, F32),
                   jax.ShapeDtypeStruct(sc.shape, F32)] + tail_shapes,
        scratch_shapes=[pltpu.VMEM((rows, d // 2), I32), pltpu.VMEM((ne, rows), F32)],
        compiler_params=pltpu.CompilerParams(dimension_semantics=("arbitrary",),
                                             vmem_limit_bytes=VMEM_LIMIT),
        name="sample_mixer",
    )(*ins)
    return outs[0], outs[1], outs[2], [(outs[3 + 2 * p], outs[4 + 2 * p]) for p in range(N_PARTS)]


def _sublane_max(x):
    return jnp.max(x, axis=0, keepdims=True)


def _route_kernel(lg_ref, b_ref, tri_ref, idx_ref, wt_ref, rank_ref, cnt_ref, carry, *, t_valid):
    i = pl.program_id(0)
    ne, tt = lg_ref.shape
    ng = N_EXPERT_GROUPS
    per = ne // ng

    @pl.when(i == 0)
    def _():
        carry[...] = jnp.zeros_like(carry)

    s = _sigmoid(lg_ref[...])
    sel = s + b_ref[...]
    s3 = [s[p * ng:(p + 1) * ng, :] for p in range(per)]
    sel3 = [sel[p * ng:(p + 1) * ng, :] for p in range(per)]
    m1 = sel3[0]
    m2 = jnp.full_like(m1, NEG_INF)
    for p in range(1, per):
        m2 = jnp.maximum(m2, jnp.minimum(m1, sel3[p]))
        m1 = jnp.maximum(m1, sel3[p])
    gs = m1 + m2
    gi = lax.broadcasted_iota(jnp.int32, (ng, tt), 0)
    beaten = jnp.zeros((ng, tt), jnp.int32)
    for g in range(ng):
        row = gs[g:g + 1, :]
        beats = (row > gs) | ((row == gs) & (gi > g))
        beaten = beaten + beats.astype(jnp.int32)
    keep = beaten < TOPK_GROUPS
    cur = [jnp.where(keep, sel3[p], NEG_INF) for p in range(per)]
    eid = [(gi * per + p).astype(F32) for p in range(per)]
    idxs, wts, hits = [], [], []
    for _ in range(TOP_K):
        m = cur[0]
        for p in range(1, per):
            m = jnp.maximum(m, cur[p])
        m = _sublane_max(m)
        cand = jnp.where(cur[0] == m, eid[0], float(ne))
        for p in range(1, per):
            cand = jnp.minimum(cand, jnp.where(cur[p] == m, eid[p], float(ne)))
        e_sel = jnp.min(cand, axis=0, keepdims=True)
        hit = [eid[p] == e_sel for p in range(per)]
        wk = jnp.where(hit[0], s3[0], 0.0)
        for p in range(1, per):
            wk = wk + jnp.where(hit[p], s3[p], 0.0)
        wts.append(jnp.sum(wk, axis=0, keepdims=True))
        cur = [jnp.where(hit[p], NEG_INF, cur[p]) for p in range(per)]
        idxs.append(e_sel)
        hits.append(hit)
    wsum = wts[0]
    for k in range(1, TOP_K):
        wsum = wsum + wts[k]
    chosen = [functools.reduce(lambda a, b: a | b, [hits[k][p] for k in range(TOP_K)]) for p in range(per)]
    real = (i * tt + lax.broadcasted_iota(I32, (1, tt), 1)) < t_valid
    onehot = jnp.concatenate([(c & real).astype(F32) for c in chosen], axis=0)
    before = _dot(onehot.astype(BF16), tri_ref[...]) + carry[...]
    for k in range(TOP_K):
        rk = jnp.where(hits[k][0], before[0:ng, :], 0.0)
        for p in range(1, per):
            rk = rk + jnp.where(hits[k][p], before[p * ng:(p + 1) * ng, :], 0.0)
        rank_ref[k:k + 1, :] = jnp.sum(rk, axis=0, keepdims=True).astype(jnp.int32)
        idx_ref[k:k + 1, :] = idxs[k].astype(jnp.int32)
        wt_ref[k:k + 1, :] = wts[k] / wsum * ROUTED_SCALE
    carry[...] = carry[...] + jnp.sum(onehot, axis=1, keepdims=True)
    cnt_ref[...] = carry[...]


def _route(lgt, b_perm, t_valid):
    ne, t = lgt.shape
    tt = ROUTE_TILE
    tri = (jnp.arange(tt)[:, None] < jnp.arange(tt)[None, :]).astype(BF16)
    return pl.pallas_call(
        functools.partial(_route_kernel, t_valid=t_valid),
        grid=(t // tt,),
        in_specs=[pl.BlockSpec((ne, tt), lambda i: (0, i)),
                  pl.BlockSpec((ne, 1), lambda i: (0, 0)),
                  pl.BlockSpec((tt, tt), lambda i: (0, 0))],
        out_specs=[pl.BlockSpec((TOP_K, tt), lambda i: (0, i)),
                   pl.BlockSpec((TOP_K, tt), lambda i: (0, i)),
                   pl.BlockSpec((TOP_K, tt), lambda i: (0, i)),
                   pl.BlockSpec((ne, 1), lambda i: (0, 0))],
        out_shape=[jax.ShapeDtypeStruct((TOP_K, t), jnp.int32),
                   jax.ShapeDtypeStruct((TOP_K, t), F32),
                   jax.ShapeDtypeStruct((TOP_K, t), jnp.int32),
                   jax.ShapeDtypeStruct((ne, 1), F32)],
        scratch_shapes=[pltpu.VMEM((ne, 1), F32)],
        compiler_params=pltpu.CompilerParams(dimension_semantics=("arbitrary",),
                                             vmem_limit_bytes=VMEM_LIMIT),
        name="route",
    )(lgt, b_perm, tri)


def _dest_kernel(pstart_ref, idx_ref, rank_ref, dest_ref, *, t_valid, spare_row):
    idx = idx_ref[...]
    k_top, tile = idx.shape

    def add_start(e, acc):
        return acc + jnp.where(idx == e, pstart_ref[e], 0)

    dest = lax.fori_loop(0, pstart_ref.shape[0], add_start, rank_ref[...])
    tok = pl.program_id(0) * tile + lax.broadcasted_iota(I32, idx.shape, 1)
    spare = spare_row + (tok - t_valid) * k_top + lax.broadcasted_iota(I32, idx.shape, 0)
    dest_ref[...] = jnp.where(tok < t_valid, dest, spare)


def _dest_rows(pstart, idx, rank, t_valid, spare_row):
    k_top, t = idx.shape
    tile = DEST_TILE
    spec = pl.BlockSpec((k_top, tile), lambda i, ps: (0, i))
    return pl.pallas_call(
        functools.partial(_dest_kernel, t_valid=t_valid, spare_row=spare_row),
        grid_spec=pltpu.PrefetchScalarGridSpec(num_scalar_prefetch=1, grid=(t // tile,),
                                               in_specs=[spec, spec], out_specs=spec),
        out_shape=jax.ShapeDtypeStruct((k_top, t), I32),
        compiler_params=pltpu.CompilerParams(dimension_semantics=("arbitrary",),
                                             vmem_limit_bytes=VMEM_LIMIT),
        name="dest_rows",
    )(pstart, idx, rank)


def _expert_kernel(first_ref, valid_ref, xs_hbm, wgu_ref, wd_ref, y_hbm, xbuf, ybuf, in_sem, out_sem):
    step = pl.program_id(0)
    ne = first_ref.shape[0] - 1
    nbuf, rows = xbuf.shape[:2]
    piece = rows // EXPERT_PIECES
    per_step, de = wd_ref.shape[:2]
    n_total = first_ref[ne]

    def in_copy(g, h):
        return pltpu.make_async_copy(xs_hbm.at[pl.ds(g * rows + h * piece, piece)],
                                     xbuf.at[g % nbuf, pl.ds(h * piece, piece)], in_sem.at[g % nbuf])

    def out_copy(g, h):
        return pltpu.make_async_copy(ybuf.at[g % nbuf, pl.ds(h * piece, piece)],
                                     y_hbm.at[pl.ds(g * rows + h * piece, piece)], out_sem.at[g % nbuf])

    def used_pieces(copy, g, act):
        getattr(copy(g, 0), act)()
        for h in range(1, EXPERT_PIECES):
            @pl.when(valid_ref[g] > h * piece)
            def _():
                getattr(copy(g, h), act)()

    @pl.when(step == 0)
    def _():
        for g in range(nbuf - 1):
            @pl.when(g < n_total)
            def _():
                used_pieces(in_copy, g, "start")

    def block(g, carry, *, j):
        slot = g % nbuf

        @pl.when(g + nbuf - 1 < n_total)
        def _():
            used_pieces(in_copy, g + nbuf - 1, "start")

        used_pieces(in_copy, g, "wait")

        @pl.when(g >= nbuf)
        def _():
            used_pieces(out_copy, g - nbuf, "wait")

        valid = valid_ref[g]

        def run(n):
            rid = lax.broadcasted_iota(I32, (n, 1), 0)
            xs = jnp.where(rid < valid, _unpack_bf16_pairs(xbuf[slot, 0:n]), 0.0).astype(BF16)
            gu = _dot(xs, wgu_ref[j])
            hb = _silu(gu[:, :de]) * gu[:, de:]
            ybuf[slot, 0:n] = _pack_bf16_pairs(_dot(hb.astype(BF16), wd_ref[j]))

        for h in range(EXPERT_PIECES):
            @pl.when((valid > h * piece) & (valid <= (h + 1) * piece))
            def _():
                run((h + 1) * piece)

        used_pieces(out_copy, g, "start")
        return carry

    for j in range(per_step):
        e = step * per_step + j
        lax.fori_loop(first_ref[e], first_ref[e + 1], functools.partial(block, j=j), 0)

    @pl.when(step == pl.num_programs(0) - 1)
    def _():
        for back in range(nbuf, 0, -1):
            @pl.when(n_total >= back)
            def _():
                used_pieces(out_copy, n_total - back, "wait")


def _experts(xs, w_gu, w_down, first_block, block_valid):
    p, dw = xs.shape
    ne, de, d = w_down.shape
    return pl.pallas_call(
        _expert_kernel,
        grid_spec=pltpu.PrefetchScalarGridSpec(
            num_scalar_prefetch=2,
            grid=(ne // EXPERTS_PER_STEP,),
            in_specs=[pl.BlockSpec(memory_space=pl.ANY),
                      pl.BlockSpec((EXPERTS_PER_STEP, d, 2 * de), lambda e, fb, cn: (e, 0, 0)),
                      pl.BlockSpec((EXPERTS_PER_STEP, de, d), lambda e, fb, cn: (e, 0, 0))],
            out_specs=pl.BlockSpec(memory_space=pl.ANY),
            scratch_shapes=[pltpu.VMEM((EXPERT_BUFFERS, EXPERT_ROWS, dw), I32),
                            pltpu.VMEM((EXPERT_BUFFERS, EXPERT_ROWS, dw), I32),
                            pltpu.SemaphoreType.DMA((EXPERT_BUFFERS,)), pltpu.SemaphoreType.DMA((EXPERT_BUFFERS,))]),
        out_shape=jax.ShapeDtypeStruct((p, dw), I32),
        compiler_params=pltpu.CompilerParams(dimension_semantics=("arbitrary",),
                                             vmem_limit_bytes=VMEM_LIMIT),
        name="experts",
    )(first_block, block_valid, xs, w_gu, w_down)


def _sc_first_chunk(n_chunks):
    per_worker = n_chunks // SC_WORKERS
    assert per_worker * SC_WORKERS == n_chunks
    return per_worker, (lax.axis_index("s") * SC_CORES + lax.axis_index("c")) * per_worker


def _sc_start(copies):
    for cp in copies:
        cp.start()


def _sc_wait(copies):
    for cp in copies:
        cp.wait()


def _sc_dispatch(h2, dest, n_rows):
    w = h2.shape[1]
    n_chunks, k_top, ch = dest.shape
    mesh = plsc.VectorSubcoreMesh(core_axis_name="c", subcore_axis_name="s")

    @functools.partial(
        pl.kernel, mesh=mesh, out_type=jax.ShapeDtypeStruct((n_rows, w), I32),
        scratch_types=[pltpu.VMEM((2, k_top, ch), I32), pltpu.VMEM((2, ch, w), I32),
                       pltpu.SemaphoreType.DMA((2,)), pltpu.SemaphoreType.DMA((2,))])
    def dispatch(h2_hbm, dest_hbm, xs_hbm, idx_v, rows_v, load_sem, store_sem):
        per_worker, first = _sc_first_chunk(n_chunks)

        def loads(i):
            slot = i % 2
            return (pltpu.make_async_copy(h2_hbm.at[pl.ds((first + i) * ch, ch)], rows_v.at[slot], load_sem.at[slot]),
                    pltpu.make_async_copy(dest_hbm.at[first + i], idx_v.at[slot], load_sem.at[slot]))

        def scatters(i):
            slot = i % 2
            return [pltpu.make_async_copy(rows_v.at[slot], xs_hbm.at[idx_v.at[slot, k]], store_sem.at[slot])
                    for k in range(k_top)]

        for i in range(min(2, per_worker)):
            _sc_start(loads(i))
        for i in range(per_worker):
            _sc_wait(loads(i))
            _sc_start(scatters(i))
            if 1 <= i < per_worker - 1:
                _sc_wait(scatters(i - 1))
                _sc_start(loads(i + 1))
        for i in range(max(per_worker - 2, 0), per_worker):
            _sc_wait(scatters(i))

    return dispatch(h2, dest)


def _sc_combine_gather(y, dest, t):
    w = y.shape[1]
    n_chunks, k_top, ch = dest.shape
    n_local = n_chunks // SC_WORKERS
    mesh = plsc.VectorSubcoreMesh(core_axis_name="c", subcore_axis_name="s")

    @functools.partial(
        pl.kernel, mesh=mesh, out_type=jax.ShapeDtypeStruct((k_top, t, w), I32),
        scratch_types=[pltpu.VMEM((n_local, k_top, ch), I32), pltpu.VMEM((k_top, ch, w), I32),
                       pltpu.SemaphoreType.DMA((k_top,)), pltpu.SemaphoreType.DMA((k_top,))])
    def gather(y_hbm, dest_hbm, out_hbm, idx_v, rows_v, load_sem, store_sem):
        per_worker, first = _sc_first_chunk(n_chunks)
        pltpu.sync_copy(dest_hbm.at[pl.ds(first, per_worker)], idx_v)

        def fetch(i, k):
            return pltpu.make_async_copy(y_hbm.at[idx_v.at[i, k]], rows_v.at[k], load_sem.at[k])

        def put(i, k):
            return pltpu.make_async_copy(rows_v.at[k], out_hbm.at[k, pl.ds((first + i) * ch, ch)], store_sem.at[k])

        @pl.loop(0, per_worker)
        def _(i):
            for k in range(k_top):
                @pl.when(i > 0)
                def _():
                    put(i - 1, k).wait()

                fetch(i, k).start()
            for k in range(k_top):
                fetch(i, k).wait()
                put(i, k).start()

        for k in range(k_top):
            put(per_worker - 1, k).wait()

    return gather(y, dest)


def _final_kernel(xmid_ref, yg_ref, wt_ref, g2_ref, shf_ref, scf_ref, gf_ref, *rest, vec_row0=None):
    o_ref = rest[-1]
    if vec_row0 is not None:
        row = pl.ds((vec_row0 + pl.program_id(0)) % SUBLANES, 1)
        g2_ref, shf_ref, scf_ref = g2_ref.at[row, :], shf_ref.at[row, :], scf_ref.at[row, :]
    wt = wt_ref[...].T[:xmid_ref.shape[0]]
    routed = wt[:, 0:1] * _unpack_bf16_pairs(yg_ref[0])
    for k in range(1, yg_ref.shape[0]):
        routed = routed + wt[:, k:k + 1] * _unpack_bf16_pairs(yg_ref[k])
    x2 = xmid_ref[...] + g2_ref[...] * routed
    o_ref[...] = _rms_norm(x2, gf_ref[...]) * (1.0 + scf_ref[...]) + shf_ref[...]


def _final_prompt(xmid, yg, wts_t, mod, modf, mod_row0, g_final, b0, n_batch, out_prev):
    b, seq, d = xmid.shape
    k_top, _, w = yg.shape
    tl = FINAL_TILE
    nl = seq // tl

    def vec(col):
        return pl.BlockSpec((SUBLANES, d), lambda i, j: ((i + b0 + mod_row0) // SUBLANES, col))

    in_specs = [pl.BlockSpec((None, tl, d), lambda i, j: (i, j, 0)),
                pl.BlockSpec((k_top, tl, w), lambda i, j: (0, i * nl + j, 0)),
                pl.BlockSpec((k_top, tl), lambda i, j: (0, i * nl + j)),
                vec(5), vec(0), vec(1), pl.BlockSpec((1, d), lambda i, j: (0, 0))]
    args = [xmid, yg, wts_t, mod, modf, modf, g_final]
    aliases = {}
    if out_prev is not None:
        in_specs.append(pl.BlockSpec(memory_space=pl.ANY))
        args.append(out_prev)
        aliases = {7: 0}
    return pl.pallas_call(
        functools.partial(_final_kernel, vec_row0=b0 + mod_row0),
        grid=(b, nl),
        in_specs=in_specs,
        out_specs=pl.BlockSpec((None, tl, d), lambda i, j: (i + b0, j, 0)),
        out_shape=jax.ShapeDtypeStruct((n_batch, seq, d), F32),
        compiler_params=pltpu.CompilerParams(dimension_semantics=("arbitrary", "arbitrary"),
                                             vmem_limit_bytes=VMEM_LIMIT),
        input_output_aliases=aliases,
        name="final_prompt",
    )(*args)


def _final_sample(xmid_all, yg, wts_t, mod, modf, g_final, part, rows, first_row, after):
    d = xmid_all.shape[1]
    k_top, _, w = yg.shape

    def vec(col):
        return pl.BlockSpec((rows, d), lambda i: (part, col))

    return pl.pallas_call(
        _final_kernel,
        grid=(1,),
        in_specs=[vec(0),
                  pl.BlockSpec((k_top, rows, w), lambda i: (0, first_row // rows, 0)),
                  pl.BlockSpec((k_top, LANES), lambda i: (0, first_row // LANES)),
                  vec(5), vec(0), vec(1), pl.BlockSpec((1, d), lambda i: (0, 0)), pl.BlockSpec(memory_space=pl.ANY)],
        out_specs=pl.BlockSpec((rows, d), lambda i: (0, 0)),
        out_shape=jax.ShapeDtypeStruct((rows, d), F32),
        compiler_params=pltpu.CompilerParams(dimension_semantics=("arbitrary",),
                                             vmem_limit_bytes=VMEM_LIMIT),
        name="final_sample",
    )(xmid_all, yg, wts_t, mod, modf, modf, g_final, after)


def _block_diag_pairs(w_pool):
    g, c, _ = w_pool.shape
    eye = jnp.eye(g, dtype=w_pool.dtype)
    return (eye[:, None, :, None] * w_pool[:, :, None, :]).reshape(g * c, g * c)


def kernel(x_prompt, x_sample, state_pool, state_conv, c_prompt, c_sample, w_ada, b_ada, g_mix, w_in, w_pool, pool_scale, w_dw, b_dw, ln_g, ln_b, w_out, g_ffn, w_router, b_router, w_gate, w_up, w_down, ws_gate, ws_up, ws_down, w_ada_final, b_ada_final, g_final):
    bp, seq, d = x_prompt.shape
    bs = x_sample.shape[0]
    depth = w_ada.shape[0]
    assert depth == 1 and x_sample.shape[1] == 1
    conv_k = w_dw.shape[1]
    ne = w_router.shape[-1]
    per = ne // N_EXPERT_GROUPS

    row_expert = (jnp.arange(ne) % N_EXPERT_GROUPS) * per + jnp.arange(ne) // N_EXPERT_GROUPS
    wl = {
        "g_mix": g_mix[0][None, :],
        "in": w_in[0].astype(BF16),
        "pool": _block_diag_pairs(w_pool[0]).astype(BF16),
        "pool_scale": pool_scale[0][None, :],
        "dw": w_dw[0],
        "b_dw": b_dw[0][None, :],
        "ln_g": ln_g[0][None, :],
        "ln_b": ln_b[0][None, :],
        "out": w_out[0].astype(BF16),
        "g_ffn": g_ffn[0][None, :],
        "router_t": w_router[0].T[row_expert].astype(BF16),
        "s_gu": jnp.concatenate([ws_gate[0], ws_up[0]], axis=1).astype(BF16),
        "s_down": ws_down[0].astype(BF16),
    }
    b_perm = b_router[0][row_expert][:, None]

    assert bp % N_PARTS == 0 and bs % N_PARTS == 0
    bpp, bsp = bp // N_PARTS, bs // N_PARTS
    tpp = bpp * seq
    t_part = tpp + bsp
    grain = SC_WORKERS * SC_DISPATCH_CHUNK * SC_COMBINE_CHUNK // math.gcd(SC_DISPATCH_CHUNK, SC_COMBINE_CHUNK)
    t_pad = (t_part + grain - 1) // grain * grain
    assert t_pad % ROUTE_TILE == 0 and t_pad % DEST_TILE == 0 and tpp % bsp == 0

    c_all = jnp.concatenate([c_sample, c_prompt], axis=0)
    mod = _ada(c_all, w_ada[0], b_ada[0])
    modf = _ada(c_all, w_ada_final, b_ada_final)

    xmid_s, new_pool_t, new_conv_t, tails = _sample_mixer(
        x_sample.reshape(bs, d), mod, jnp.transpose(state_pool[0], (1, 0, 2)),
        jnp.transpose(state_conv[0], (1, 0, 2)), wl, tpp, t_pad, conv_k=conv_k)
    new_pool_s = jnp.transpose(new_pool_t, (1, 0, 2))
    new_conv_s = jnp.transpose(new_conv_t, (1, 0, 2))
    w_experts = tuple(a.reshape(a.shape[1:]) for a in (w_gate, w_up, w_down))

    n_blocks = (t_part * TOP_K + EXPERT_ROWS - 1) // EXPERT_ROWS + ne
    n_spare = -(-(t_pad - t_part) * TOP_K // EXPERT_ROWS)
    n_rows = (n_blocks + n_spare) * EXPERT_ROWS

    y_prompt, y_samples, npools, nconvs, w_bf = None, [], [], [], None
    mixed, after = [], modf
    for p in range(N_PARTS):
        h2, lgt = tails[p]
        xmid_p, h2, lgt, npool_p, nconv_p, *w_bf = _prompt_mixer(
            x_prompt, mod, bs, wl, h2, lgt, p * bpp, bpp, w_experts, w_bf, after, conv_k=conv_k)
        npools.append(npool_p)
        nconvs.append(nconv_p)
        idx, wts, rank, counts_perm = _route(lgt, b_perm, t_part)
        counts = counts_perm.astype(I32).reshape(per, N_EXPERT_GROUPS).T.reshape(ne)
        nblk = (counts + EXPERT_ROWS - 1) // EXPERT_ROWS
        first_block = jnp.concatenate([jnp.zeros((1,), I32), jnp.cumsum(nblk).astype(I32)])
        dest = _dest_rows(first_block[:ne] * EXPERT_ROWS, idx, rank, t_part, n_blocks * EXPERT_ROWS)
        blk = jnp.arange(n_blocks, dtype=I32)[:, None]
        inside = (blk >= first_block[None, :ne]) & (blk < first_block[None, 1:])
        rows_left = counts[None, :] - (blk - first_block[None, :ne]) * EXPERT_ROWS
        block_valid = jnp.sum(jnp.where(inside, jnp.minimum(rows_left, EXPERT_ROWS), 0), axis=1)
        mixed.append((xmid_p, h2, wts, dest, first_block, block_valid))
        after = dest
    for p in range(N_PARTS):
        xmid_p, h2, wts, dest, first_block, block_valid = mixed[p]

        def chunked(ch):
            return dest.reshape(TOP_K, t_pad // ch, ch).transpose(1, 0, 2)

        xs = _sc_dispatch(h2, chunked(SC_DISPATCH_CHUNK), n_rows)
        y = _experts(xs, w_bf[0], w_bf[1], first_block, block_valid)
        yg = _sc_combine_gather(y, chunked(SC_COMBINE_CHUNK), t_pad)

        y_samples.append(_final_sample(xmid_s, yg, wts, mod, modf, g_final[None, :], p, bsp, tpp,
                                       g_final if y_prompt is None else y_prompt))
        y_prompt = _final_prompt(xmid_p, yg, wts, mod, modf, bs, g_final[None, :], p * bpp, bp, y_prompt)
    y_sample = jnp.concatenate(y_samples, axis=0)
    npool_p = jnp.concatenate(npools, axis=0)
    nconv_p = jnp.concatenate(nconvs, axis=0)

    return (y_prompt, y_sample[:, None, :], npool_p[None], nconv_p[None], new_pool_s[None], new_conv_s[None])
```

```python
import functools
import math

import jax
import jax.numpy as jnp
from jax import lax
from jax.experimental import pallas as pl
from jax.experimental.pallas import tpu as pltpu
from jax.experimental.pallas import tpu_sc as plsc

POOL_WINDOWS = (2, 4, 8, 16)
N_EXPERT_GROUPS = 8
TOPK_GROUPS = 4
TOP_K = 8
ROUTED_SCALE = 2.5
EPS = 1e-6

LANES = 128
SUBLANES = 8
VMEM_LIMIT = 52 * 1024 * 1024

ADA_TILE = 1024
SEQ_TILE = 512
ROW_CHUNK = 16
ROUTE_TILE = 1408
EXPERT_ROWS = 1024
N_PARTS = 2
EXPERT_BUFFERS = 4
EXPERT_PIECES = 8
EXPERTS_PER_STEP = 4
FINAL_TILE = 512
DEST_TILE = 2816

SC_CORES = 2
SC_SUBCORES = 16
SC_WORKERS = SC_CORES * SC_SUBCORES
SC_DISPATCH_CHUNK = 88
SC_COMBINE_CHUNK = 24

F32 = jnp.float32
BF16 = jnp.bfloat16
I32 = jnp.int32
U32 = jnp.uint32
NEG_INF = float("-inf")
HI16 = 0xFFFF0000


def _sigmoid(x):
    return 1.0 / (1.0 + jnp.exp(-x))


def _silu(x):
    return x * _sigmoid(x)


def _rms_norm(x, g):
    return x * lax.rsqrt(jnp.mean(x * x, axis=-1, keepdims=True) + EPS) * g


def _dot(a, b):
    return jnp.dot(a, b, preferred_element_type=F32)


def _pack_bf16_pairs(x):
    w = x.shape[1] // 2
    bits = lax.bitcast_convert_type(x.astype(BF16).astype(F32), U32)
    return lax.bitcast_convert_type((bits[:, :w] >> 16) | (bits[:, w:] & U32(HI16)), I32)


def _unpack_bf16_pairs(p):
    bits = lax.bitcast_convert_type(p, U32)
    lo = lax.bitcast_convert_type(bits << 16, F32)
    hi = lax.bitcast_convert_type(bits & U32(HI16), F32)
    return jnp.concatenate([lo, hi], axis=1)


def _ada_kernel(c_ref, w_ref, b_ref, o_ref):
    s = _silu(c_ref[...])
    o_ref[...] = _dot(s.astype(BF16), w_ref[...].astype(BF16)) + b_ref[...]


def _ada(c, w, b):
    rows, d = c.shape
    n = w.shape[1]
    tn = ADA_TILE
    return pl.pallas_call(
        _ada_kernel,
        grid=(n // tn,),
        in_specs=[pl.BlockSpec((rows, d), lambda j: (0, 0)),
                  pl.BlockSpec((d, tn), lambda j: (0, j)),
                  pl.BlockSpec((1, tn), lambda j: (0, j))],
        out_specs=pl.BlockSpec((rows, tn), lambda j: (0, j)),
        out_shape=jax.ShapeDtypeStruct((rows, n), F32),
        compiler_params=pltpu.CompilerParams(dimension_semantics=("arbitrary",),
                                             vmem_limit_bytes=VMEM_LIMIT),
        name="ada",
    )(c, w, b.reshape(1, n))


def _mixer_tail(x, pool_d, conv_act, mod, w, xmid_ref, h2_ref, lgt_ref):
    sh2, sc2, g1, g2 = mod
    pw = pool_d.shape[1]
    pool_out = _dot(pool_d, w["pool"][...]) * w["pool_scale"][...]
    mix = _dot(pool_out.astype(BF16), w["out"][:pw, :]) + _dot(conv_act, w["out"][pw:, :])
    x1 = x + g1 * mix
    h2f = _rms_norm(x1, w["g_ffn"][...]) * (1.0 + sc2) + sh2
    h2_ref[...] = _pack_bf16_pairs(h2f)
    h2 = h2f.astype(BF16)
    lgt_ref[...] = lax.dot_general(w["router_t"][...], h2, (((1,), (1,)), ((), ())),
                                   preferred_element_type=F32)
    gu = _dot(h2, w["s_gu"][...])
    de = gu.shape[1] // 2
    hs = _silu(gu[:, :de]) * gu[:, de:]
    shared = _dot(hs.astype(BF16), w["s_down"][...])
    xmid_ref[...] = x1 + g2 * shared


_W_NAMES = ("g_mix", "in", "pool", "pool_scale", "dw", "b_dw", "ln_g", "ln_b", "out", "g_ffn",
            "router_t", "s_gu", "s_down")


def _layer_norm_silu(yc, g, b):
    mu = jnp.mean(yc, axis=-1, keepdims=True)
    yz = yc - mu
    var = jnp.mean(yz * yz, axis=-1, keepdims=True)
    return _silu(yz * lax.rsqrt(var + EPS) * g + b)


U_HALO = 32
A_HALO = 32


def _prompt_mixer_kernel(x_ref, mod_ref, *refs, tl, d, pw, cw, conv_k, mod_row0):
    nw = len(_W_NAMES)
    w = dict(zip(_W_NAMES, refs[:nw]))
    wg_ref, wu_ref, wd_ref = refs[nw + 2:nw + 5]
    xmid_ref, h2_ref, lgt_ref, npool_ref, nconv_ref, wgu_bf_ref, wd_bf_ref = refs[-16:-9]
    ubuf, s2buf, s4buf, s8buf, abuf, ashift, dbuf, cbuf, rowb = refs[-9:]
    de = wg_ref.shape[2]
    wgu_bf_ref[:, :, :de] = wg_ref[...].astype(BF16)
    wgu_bf_ref[:, :, de:] = wu_ref[...].astype(BF16)
    wd_bf_ref[...] = wd_ref[...].astype(BF16)
    l = pl.program_id(1)
    pool_buf = max(POOL_WINDOWS) - 1
    uh, ah = U_HALO, A_HALO
    pg = pw // len(POOL_WINDOWS)
    nu, na = uh + tl, ah + tl

    @pl.when(l == 0)
    def _():
        ubuf[0:uh, :] = jnp.zeros((uh, pw), F32)
        abuf[0:ah, :] = jnp.zeros((ah, cw), F32)

    @pl.when(l > 0)
    def _():
        ubuf[0:uh, :] = ubuf[tl:tl + uh, :]
        abuf[0:ah, :] = abuf[tl:tl + ah, :]

    x = x_ref[0]
    row = pl.ds((mod_row0 + pl.program_id(0)) % SUBLANES, 1)
    sh1, sc1, g1, sh2, sc2, g2 = [mod_ref[row, i * d:(i + 1) * d] for i in range(6)]

    h = _rms_norm(x, w["g_mix"][...]) * (1.0 + sc1) + sh1
    z = _dot(h.astype(BF16), w["in"][...])
    u = z[:, :pw]
    ubuf[uh:nu, :] = u
    abuf[ah:na, :] = z[:, pw:pw + cw] * _sigmoid(z[:, pw + cw:])

    s2buf[8:nu, :] = ubuf[8:nu, :] + ubuf[7:nu - 1, :]
    s4buf[16:nu, :] = s2buf[16:nu, pg:] + s2buf[14:nu - 2, pg:]
    s8buf[24:nu, :] = s4buf[24:nu, pg:] + s4buf[20:nu - 4, pg:]
    t = l * tl + lax.broadcasted_iota(I32, (tl, 1), 0)
    sums = (s2buf[uh:nu, 0:pg], s4buf[uh:nu, 0:pg], s8buf[uh:nu, 0:pg],
            s8buf[uh:nu, pg:2 * pg] + s8buf[uh - 8:nu - 8, pg:2 * pg])
    for g, win in enumerate(POOL_WINDOWS):
        cols = slice(g * pg, (g + 1) * pg)
        inv = 1.0 / jnp.minimum(win, t + 1).astype(F32)
        dbuf[:, cols] = (sums[g] * inv - u[:, cols]).astype(BF16)

    for r in range(1, SUBLANES):
        ashift[r - 1, 8:na, :] = abuf[8 - r:na - r, :]

    for k in range(conv_k):
        rowb[k] = jnp.broadcast_to(w["dw"][k:k + 1, :], (SUBLANES, cw))
    rowb[conv_k] = jnp.broadcast_to(w["b_dw"][...], (SUBLANES, cw))

    for c in range(tl // ROW_CHUNK):
        groups = []
        for o in range(ah + c * ROW_CHUNK, ah + (c + 1) * ROW_CHUNK, SUBLANES):
            yc = abuf[o:o + SUBLANES, :] * rowb[conv_k - 1]
            for back in range(1, conv_k):
                q, r = divmod(back, SUBLANES)
                src = abuf if r == 0 else ashift.at[r - 1]
                yc = yc + src[o - q * SUBLANES:o - (q - 1) * SUBLANES, :] * rowb[conv_k - 1 - back]
            groups.append(yc + rowb[conv_k])
        r0 = c * ROW_CHUNK
        cbuf[r0:r0 + ROW_CHUNK, :] = _layer_norm_silu(
            jnp.concatenate(groups, axis=0), w["ln_g"][...], w["ln_b"][...]).astype(BF16)

    @pl.when(l == pl.num_programs(1) - 1)
    def _():
        npool_ref[0] = ubuf[nu - pool_buf:nu, :]
        nconv_ref[0] = abuf[na - (conv_k - 1):na, :]

    _mixer_tail(x, dbuf[...], cbuf[...], (sh2, sc2, g1, g2), w, xmid_ref.at[0], h2_ref, lgt_ref)


def _sample_mixer_kernel(x_ref, mod_ref, sp_ref, sc_ref, *refs, d, pw, cw, conv_k):
    nw = len(_W_NAMES)
    w = dict(zip(_W_NAMES, refs[:nw]))
    xmid_ref, npool_ref, nconv_ref = refs[nw:nw + 3]
    tails = refs[nw + 3:-2]
    h2_tmp, lgt_tmp = refs[-2:]
    pool_buf = max(POOL_WINDOWS) - 1
    pg = pw // len(POOL_WINDOWS)
    x = x_ref[...]
    sh1, sc1, g1, sh2, sc2, g2 = [mod_ref[:, i * d:(i + 1) * d] for i in range(6)]
    h = _rms_norm(x, w["g_mix"][...]) * (1.0 + sc1) + sh1
    z = _dot(h.astype(BF16), w["in"][...])
    u = z[:, :pw]
    a = z[:, pw:pw + cw] * _sigmoid(z[:, pw + cw:])
    npool_ref[0:pool_buf - 1] = sp_ref[1:pool_buf]
    npool_ref[pool_buf - 1] = u
    nconv_ref[0:conv_k - 2] = sc_ref[1:conv_k - 1]
    nconv_ref[conv_k - 2] = a
    ds = []
    for g, win in enumerate(POOL_WINDOWS):
        cols = slice(g * pg, (g + 1) * pg)
        acc = u[:, cols]
        for j in range(1, win):
            acc = acc + sp_ref[pool_buf - j, :, cols]
        cnt = float(min(win, 1 + pool_buf))
        ds.append(acc / cnt - u[:, cols])
    pool_d = jnp.concatenate(ds, axis=-1).astype(BF16)
    yc = a * w["dw"][conv_k - 1:conv_k, :]
    for k in range(conv_k - 1):
        yc = yc + sc_ref[k] * w["dw"][k:k + 1, :]
    yc = yc + w["b_dw"][...]
    conv_act = _layer_norm_silu(yc, w["ln_g"][...], w["ln_b"][...]).astype(BF16)
    _mixer_tail(x, pool_d, conv_act, (sh2, sc2, g1, g2), w, xmid_ref, h2_tmp, lgt_tmp)
    n_parts = len(tails) // 2
    share = x.shape[0] // n_parts
    for p in range(n_parts):
        h2_ref, lgt_ref = tails[2 * p], tails[2 * p + 1]
        h2_ref[...] = jnp.zeros_like(h2_ref)
        lgt_ref[...] = jnp.zeros_like(lgt_ref)
        h2_ref[0:share, :] = h2_tmp[p * share:(p + 1) * share, :]
        lgt_ref[:, 0:share] = lgt_tmp[:, p * share:(p + 1) * share]


def _full_spec(a):
    nd = a.ndim
    return pl.BlockSpec(a.shape, lambda *_: (0,) * nd)


def _prompt_mixer(x, mod, mod_row0, wl, h2_all, lgt_all, b0, b, w_experts, w_bf_prev, after, *, conv_k):
    n_batch, seq, d = x.shape
    w_gate, w_up, w_down = w_experts
    n_exp, _, de = w_gate.shape
    pw = wl["pool"].shape[0]
    cw = wl["dw"].shape[1]
    ne = wl["router_t"].shape[0]
    tl = SEQ_TILE
    nl = seq // tl
    pg = pw // len(POOL_WINDOWS)
    assert POOL_WINDOWS == (2, 4, 8, 16) and conv_k - 1 <= A_HALO
    ws = [wl[n] for n in _W_NAMES]
    kern = functools.partial(_prompt_mixer_kernel, tl=tl, d=d, pw=pw, cw=cw, conv_k=conv_k, mod_row0=mod_row0 + b0)
    ecs = n_exp // (n_batch * nl)
    assert ecs * n_batch * nl == n_exp

    def w_map(i, j):
        return ((i + b0) * nl + j, 0, 0)

    ins = [x, mod, *ws, h2_all, lgt_all, w_gate, w_up, w_down]
    in_specs = ([pl.BlockSpec((1, tl, d), lambda i, j: (i + b0, j, 0)),
                 pl.BlockSpec((SUBLANES, mod.shape[-1]), lambda i, j: ((i + b0 + mod_row0) // SUBLANES, 0))]
                + [_full_spec(a) for a in ws]
                + [pl.BlockSpec(memory_space=pl.ANY), pl.BlockSpec(memory_space=pl.ANY),
                   pl.BlockSpec((ecs, d, de), w_map), pl.BlockSpec((ecs, d, de), w_map),
                   pl.BlockSpec((ecs, de, d), w_map)])
    aliases = {2 + len(ws): 1, 3 + len(ws): 2}
    if w_bf_prev is not None:
        aliases.update({len(ins): 5, len(ins) + 1: 6})
        ins += list(w_bf_prev)
        in_specs += [pl.BlockSpec(memory_space=pl.ANY), pl.BlockSpec(memory_space=pl.ANY)]
    if after is not None:
        ins.append(after)
        in_specs.append(pl.BlockSpec(memory_space=pl.ANY))
    return pl.pallas_call(
        kern,
        grid=(b, nl),
        in_specs=in_specs,
        out_specs=[pl.BlockSpec((1, tl, d), lambda i, j: (i, j, 0)),
                   pl.BlockSpec((tl, d // 2), lambda i, j: (i * nl + j, 0)),
                   pl.BlockSpec((ne, tl), lambda i, j: (0, i * nl + j)),
                   pl.BlockSpec((1, max(POOL_WINDOWS) - 1, pw), lambda i, j: (i, 0, 0)),
                   pl.BlockSpec((1, conv_k - 1, cw), lambda i, j: (i, 0, 0)),
                   pl.BlockSpec((ecs, d, 2 * de), w_map), pl.BlockSpec((ecs, de, d), w_map)],
        out_shape=[jax.ShapeDtypeStruct((b, seq, d), F32),
                   jax.ShapeDtypeStruct(h2_all.shape, I32),
                   jax.ShapeDtypeStruct(lgt_all.shape, F32),
                   jax.ShapeDtypeStruct((b, max(POOL_WINDOWS) - 1, pw), F32),
                   jax.ShapeDtypeStruct((b, conv_k - 1, cw), F32),
                   jax.ShapeDtypeStruct((n_exp, d, 2 * de), BF16),
                   jax.ShapeDtypeStruct((n_exp, de, d), BF16)],
        scratch_shapes=[pltpu.VMEM((U_HALO + tl, pw), F32), pltpu.VMEM((U_HALO + tl, pw), F32),
                        pltpu.VMEM((U_HALO + tl, pw - pg), F32), pltpu.VMEM((U_HALO + tl, pw - 2 * pg), F32),
                        pltpu.VMEM((A_HALO + tl, cw), F32), pltpu.VMEM((SUBLANES - 1, A_HALO + tl, cw), F32),
                        pltpu.VMEM((tl, pw), BF16), pltpu.VMEM((tl, cw), BF16),
                        pltpu.VMEM((conv_k + 1, SUBLANES, cw), F32)],
        compiler_params=pltpu.CompilerParams(dimension_semantics=("arbitrary", "arbitrary"),
                                             vmem_limit_bytes=VMEM_LIMIT),
        input_output_aliases=aliases,
        name="prompt_mixer",
    )(*ins)


def _sample_mixer(x, mod, sp, sc, wl, t_prompt, t_pad, *, conv_k):
    rows, d = x.shape
    tail = t_pad - t_prompt
    assert t_prompt % tail == 0 and rows % N_PARTS == 0 and rows // N_PARTS <= tail
    pw = wl["pool"].shape[0]
    cw = wl["dw"].shape[1]
    ne = wl["router_t"].shape[0]
    ws = [wl[n] for n in _W_NAMES]
    kern = functools.partial(_sample_mixer_kernel, d=d, pw=pw, cw=cw, conv_k=conv_k)
    ins = [x, mod, sp, sc] + ws
    tail_specs = [pl.BlockSpec((tail, d // 2), lambda i: (t_prompt // tail, 0)),
                  pl.BlockSpec((ne, tail), lambda i: (0, t_prompt // tail))] * N_PARTS
    tail_shapes = [jax.ShapeDtypeStruct((t_pad, d // 2), I32), jax.ShapeDtypeStruct((ne, t_pad), F32)] * N_PARTS
    outs = pl.pallas_call(
        kern,
        grid=(1,),
        in_specs=[_full_spec(x), pl.BlockSpec((rows, mod.shape[1]), lambda i: (0, 0))]
                 + [_full_spec(a) for a in ins[2:]],
        out_specs=[pl.BlockSpec((rows, d), lambda i: (0, 0)), _full_spec(sp), _full_spec(sc)] + tail_specs,
        out_shape=[jax.ShapeDtypeStruct((rows, d), F32),
                   jax.ShapeDtypeStruct(sp.shape, F32),
                   jax.ShapeDtypeStruct(sc.shape, F32)] + tail_shapes,
        scratch_shapes=[pltpu.VMEM((rows, d // 2), I32), pltpu.VMEM((ne, rows), F32)],
        compiler_params=pltpu.CompilerParams(dimension_semantics=("arbitrary",),
                                             vmem_limit_bytes=VMEM_LIMIT),
        name="sample_mixer",
    )(*ins)
    return outs[0], outs[1], outs[2], [(outs[3 + 2 * p], outs[4 + 2 * p]) for p in range(N_PARTS)]


def _sublane_max(x):
    return jnp.max(x, axis=0, keepdims=True)


def _route_kernel(lg_ref, b_ref, tri_ref, idx_ref, wt_ref, rank_ref, cnt_ref, carry, *, t_valid):
    i = pl.program_id(0)
    ne, tt = lg_ref.shape
    ng = N_EXPERT_GROUPS
    per = ne // ng

    @pl.when(i == 0)
    def _():
        carry[...] = jnp.zeros_like(carry)

    s = _sigmoid(lg_ref[...])
    sel = s + b_ref[...]
    s3 = [s[p * ng:(p + 1) * ng, :] for p in range(per)]
    sel3 = [sel[p * ng:(p + 1) * ng, :] for p in range(per)]
    m1 = sel3[0]
    m2 = jnp.full_like(m1, NEG_INF)
    for p in range(1, per):
        m2 = jnp.maximum(m2, jnp.minimum(m1, sel3[p]))
        m1 = jnp.maximum(m1, sel3[p])
    gs = m1 + m2
    gi = lax.broadcasted_iota(jnp.int32, (ng, tt), 0)
    beaten = jnp.zeros((ng, tt), jnp.int32)
    for g in range(ng):
        row = gs[g:g + 1, :]
        beats = (row > gs) | ((row == gs) & (gi > g))
        beaten = beaten + beats.astype(jnp.int32)
    keep = beaten < TOPK_GROUPS
    cur = [jnp.where(keep, sel3[p], NEG_INF) for p in range(per)]
    eid = [(gi * per + p).astype(F32) for p in range(per)]
    idxs, wts, hits = [], [], []
    for _ in range(TOP_K):
        m = cur[0]
        for p in range(1, per):
            m = jnp.maximum(m, cur[p])
        m = _sublane_max(m)
        cand = jnp.where(cur[0] == m, eid[0], float(ne))
        for p in range(1, per):
            cand = jnp.minimum(cand, jnp.where(cur[p] == m, eid[p], float(ne)))
        e_sel = jnp.min(cand, axis=0, keepdims=True)
        hit = [eid[p] == e_sel for p in range(per)]
        wk = jnp.where(hit[0], s3[0], 0.0)
        for p in range(1, per):
            wk = wk + jnp.where(hit[p], s3[p], 0.0)
        wts.append(jnp.sum(wk, axis=0, keepdims=True))
        cur = [jnp.where(hit[p], NEG_INF, cur[p]) for p in range(per)]
        idxs.append(e_sel)
        hits.append(hit)
    wsum = wts[0]
    for k in range(1, TOP_K):
        wsum = wsum + wts[k]
    chosen = [functools.reduce(lambda a, b: a | b, [hits[k][p] for k in range(TOP_K)]) for p in range(per)]
    real = (i * tt + lax.broadcasted_iota(I32, (1, tt), 1)) < t_valid
    onehot = jnp.concatenate([(c & real).astype(F32) for c in chosen], axis=0)
    before = _dot(onehot.astype(BF16), tri_ref[...]) + carry[...]
    for k in range(TOP_K):
        rk = jnp.where(hits[k][0], before[0:ng, :], 0.0)
        for p in range(1, per):
            rk = rk + jnp.where(hits[k][p], before[p * ng:(p + 1) * ng, :], 0.0)
        rank_ref[k:k + 1, :] = jnp.sum(rk, axis=0, keepdims=True).astype(jnp.int32)
        idx_ref[k:k + 1, :] = idxs[k].astype(jnp.int32)
        wt_ref[k:k + 1, :] = wts[k] / wsum * ROUTED_SCALE
    carry[...] = carry[...] + jnp.sum(onehot, axis=1, keepdims=True)
    cnt_ref[...] = carry[...]


def _route(lgt, b_perm, t_valid):
    ne, t = lgt.shape
    tt = ROUTE_TILE
    tri = (jnp.arange(tt)[:, None] < jnp.arange(tt)[None, :]).astype(BF16)
    return pl.pallas_call(
        functools.partial(_route_kernel, t_valid=t_valid),
        grid=(t // tt,),
        in_specs=[pl.BlockSpec((ne, tt), lambda i: (0, i)),
                  pl.BlockSpec((ne, 1), lambda i: (0, 0)),
                  pl.BlockSpec((tt, tt), lambda i: (0, 0))],
        out_specs=[pl.BlockSpec((TOP_K, tt), lambda i: (0, i)),
                   pl.BlockSpec((TOP_K, tt), lambda i: (0, i)),
                   pl.BlockSpec((TOP_K, tt), lambda i: (0, i)),
                   pl.BlockSpec((ne, 1), lambda i: (0, 0))],
        out_shape=[jax.ShapeDtypeStruct((TOP_K, t), jnp.int32),
                   jax.ShapeDtypeStruct((TOP_K, t), F32),
                   jax.ShapeDtypeStruct((TOP_K, t), jnp.int32),
                   jax.ShapeDtypeStruct((ne, 1), F32)],
        scratch_shapes=[pltpu.VMEM((ne, 1), F32)],
        compiler_params=pltpu.CompilerParams(dimension_semantics=("arbitrary",),
                                             vmem_limit_bytes=VMEM_LIMIT),
        name="route",
    )(lgt, b_perm, tri)


def _dest_kernel(pstart_ref, idx_ref, rank_ref, dest_ref, *, t_valid, spare_row):
    idx = idx_ref[...]
    k_top, tile = idx.shape

    def add_start(e, acc):
        return acc + jnp.where(idx == e, pstart_ref[e], 0)

    dest = lax.fori_loop(0, pstart_ref.shape[0], add_start, rank_ref[...])
    tok = pl.program_id(0) * tile + lax.broadcasted_iota(I32, idx.shape, 1)
    spare = spare_row + (tok - t_valid) * k_top + lax.broadcasted_iota(I32, idx.shape, 0)
    dest_ref[...] = jnp.where(tok < t_valid, dest, spare)


def _dest_rows(pstart, idx, rank, t_valid, spare_row):
    k_top, t = idx.shape
    tile = DEST_TILE
    spec = pl.BlockSpec((k_top, tile), lambda i, ps: (0, i))
    return pl.pallas_call(
        functools.partial(_dest_kernel, t_valid=t_valid, spare_row=spare_row),
        grid_spec=pltpu.PrefetchScalarGridSpec(num_scalar_prefetch=1, grid=(t // tile,),
                                               in_specs=[spec, spec], out_specs=spec),
        out_shape=jax.ShapeDtypeStruct((k_top, t), I32),
        compiler_params=pltpu.CompilerParams(dimension_semantics=("arbitrary",),
                                             vmem_limit_bytes=VMEM_LIMIT),
        name="dest_rows",
    )(pstart, idx, rank)


def _expert_kernel(first_ref, valid_ref, xs_hbm, wgu_ref, wd_ref, y_hbm, xbuf, ybuf, in_sem, out_sem):
    step = pl.program_id(0)
    ne = first_ref.shape[0] - 1
    nbuf, rows = xbuf.shape[:2]
    piece = rows // EXPERT_PIECES
    per_step, de = wd_ref.shape[:2]
    n_total = first_ref[ne]

    def in_copy(g, h):
        return pltpu.make_async_copy(xs_hbm.at[pl.ds(g * rows + h * piece, piece)],
                                     xbuf.at[g % nbuf, pl.ds(h * piece, piece)], in_sem.at[g % nbuf])

    def out_copy(g, h):
        return pltpu.make_async_copy(ybuf.at[g % nbuf, pl.ds(h * piece, piece)],
                                     y_hbm.at[pl.ds(g * rows + h * piece, piece)], out_sem.at[g % nbuf])

    def used_pieces(copy, g, act):
        getattr(copy(g, 0), act)()
        for h in range(1, EXPERT_PIECES):
            @pl.when(valid_ref[g] > h * piece)
            def _():
                getattr(copy(g, h), act)()

    @pl.when(step == 0)
    def _():
        for g in range(nbuf - 1):
            @pl.when(g < n_total)
            def _():
                used_pieces(in_copy, g, "start")

    def block(g, carry, *, j):
        slot = g % nbuf

        @pl.when(g + nbuf - 1 < n_total)
        def _():
            used_pieces(in_copy, g + nbuf - 1, "start")

        used_pieces(in_copy, g, "wait")

        @pl.when(g >= nbuf)
        def _():
            used_pieces(out_copy, g - nbuf, "wait")

        valid = valid_ref[g]

        def run(n):
            rid = lax.broadcasted_iota(I32, (n, 1), 0)
            xs = jnp.where(rid < valid, _unpack_bf16_pairs(xbuf[slot, 0:n]), 0.0).astype(BF16)
            gu = _dot(xs, wgu_ref[j])
            hb = _silu(gu[:, :de]) * gu[:, de:]
            ybuf[slot, 0:n] = _pack_bf16_pairs(_dot(hb.astype(BF16), wd_ref[j]))

        for h in range(EXPERT_PIECES):
            @pl.when((valid > h * piece) & (valid <= (h + 1) * piece))
            def _():
                run((h + 1) * piece)

        used_pieces(out_copy, g, "start")
        return carry

    for j in range(per_step):
        e = step * per_step + j
        lax.fori_loop(first_ref[e], first_ref[e + 1], functools.partial(block, j=j), 0)

    @pl.when(step == pl.num_programs(0) - 1)
    def _():
        for back in range(nbuf, 0, -1):
            @pl.when(n_total >= back)
            def _():
                used_pieces(out_copy, n_total - back, "wait")


def _experts(xs, w_gu, w_down, first_block, block_valid):
    p, dw = xs.shape
    ne, de, d = w_down.shape
    return pl.pallas_call(
        _expert_kernel,
        grid_spec=pltpu.PrefetchScalarGridSpec(
            num_scalar_prefetch=2,
            grid=(ne // EXPERTS_PER_STEP,),
            in_specs=[pl.BlockSpec(memory_space=pl.ANY),
                      pl.BlockSpec((EXPERTS_PER_STEP, d, 2 * de), lambda e, fb, cn: (e, 0, 0)),
                      pl.BlockSpec((EXPERTS_PER_STEP, de, d), lambda e, fb, cn: (e, 0, 0))],
            out_specs=pl.BlockSpec(memory_space=pl.ANY),
            scratch_shapes=[pltpu.VMEM((EXPERT_BUFFERS, EXPERT_ROWS, dw), I32),
                            pltpu.VMEM((EXPERT_BUFFERS, EXPERT_ROWS, dw), I32),
                            pltpu.SemaphoreType.DMA((EXPERT_BUFFERS,)), pltpu.SemaphoreType.DMA((EXPERT_BUFFERS,))]),
        out_shape=jax.ShapeDtypeStruct((p, dw), I32),
        compiler_params=pltpu.CompilerParams(dimension_semantics=("arbitrary",),
                                             vmem_limit_bytes=VMEM_LIMIT),
        name="experts",
    )(first_block, block_valid, xs, w_gu, w_down)


def _sc_first_chunk(n_chunks):
    per_worker = n_chunks // SC_WORKERS
    assert per_worker * SC_WORKERS == n_chunks
    return per_worker, (lax.axis_index("s") * SC_CORES + lax.axis_index("c")) * per_worker


def _sc_start(copies):
    for cp in copies:
        cp.start()


def _sc_wait(copies):
    for cp in copies:
        cp.wait()


def _sc_dispatch(h2, dest, n_rows):
    w = h2.shape[1]
    n_chunks, k_top, ch = dest.shape
    mesh = plsc.VectorSubcoreMesh(core_axis_name="c", subcore_axis_name="s")

    @functools.partial(
        pl.kernel, mesh=mesh, out_type=jax.ShapeDtypeStruct((n_rows, w), I32),
        scratch_types=[pltpu.VMEM((2, k_top, ch), I32), pltpu.VMEM((2, ch, w), I32),
                       pltpu.SemaphoreType.DMA((2,)), pltpu.SemaphoreType.DMA((2,))])
    def dispatch(h2_hbm, dest_hbm, xs_hbm, idx_v, rows_v, load_sem, store_sem):
        per_worker, first = _sc_first_chunk(n_chunks)

        def loads(i):
            slot = i % 2
            return (pltpu.make_async_copy(h2_hbm.at[pl.ds((first + i) * ch, ch)], rows_v.at[slot], load_sem.at[slot]),
                    pltpu.make_async_copy(dest_hbm.at[first + i], idx_v.at[slot], load_sem.at[slot]))

        def scatters(i):
            slot = i % 2
            return [pltpu.make_async_copy(rows_v.at[slot], xs_hbm.at[idx_v.at[slot, k]], store_sem.at[slot])
                    for k in range(k_top)]

        for i in range(min(2, per_worker)):
            _sc_start(loads(i))
        for i in range(per_worker):
            _sc_wait(loads(i))
            _sc_start(scatters(i))
            if 1 <= i < per_worker - 1:
                _sc_wait(scatters(i - 1))
                _sc_start(loads(i + 1))
        for i in range(max(per_worker - 2, 0), per_worker):
            _sc_wait(scatters(i))

    return dispatch(h2, dest)


def _sc_combine_gather(y, dest, t):
    w = y.shape[1]
    n_chunks, k_top, ch = dest.shape
    n_local = n_chunks // SC_WORKERS
    mesh = plsc.VectorSubcoreMesh(core_axis_name="c", subcore_axis_name="s")

    @functools.partial(
        pl.kernel, mesh=mesh, out_type=jax.ShapeDtypeStruct((k_top, t, w), I32),
        scratch_types=[pltpu.VMEM((n_local, k_top, ch), I32), pltpu.VMEM((k_top, ch, w), I32),
                       pltpu.SemaphoreType.DMA((k_top,)), pltpu.SemaphoreType.DMA((k_top,))])
    def gather(y_hbm, dest_hbm, out_hbm, idx_v, rows_v, load_sem, store_sem):
        per_worker, first = _sc_first_chunk(n_chunks)
        pltpu.sync_copy(dest_hbm.at[pl.ds(first, per_worker)], idx_v)

        def fetch(i, k):
            return pltpu.make_async_copy(y_hbm.at[idx_v.at[i, k]], rows_v.at[k], load_sem.at[k])

        def put(i, k):
            return pltpu.make_async_copy(rows_v.at[k], out_hbm.at[k, pl.ds((first + i) * ch, ch)], store_sem.at[k])

        @pl.loop(0, per_worker)
        def _(i):
            for k in range(k_top):
                @pl.when(i > 0)
                def _():
                    put(i - 1, k).wait()

                fetch(i, k).start()
            for k in range(k_top):
                fetch(i, k).wait()
                put(i, k).start()

        for k in range(k_top):
            put(per_worker - 1, k).wait()

    return gather(y, dest)


def _final_kernel(xmid_ref, yg_ref, wt_ref, g2_ref, shf_ref, scf_ref, gf_ref, *rest, vec_row0=None):
    o_ref = rest[-1]
    if vec_row0 is not None:
        row = pl.ds((vec_row0 + pl.program_id(0)) % SUBLANES, 1)
        g2_ref, shf_ref, scf_ref = g2_ref.at[row, :], shf_ref.at[row, :], scf_ref.at[row, :]
    wt = wt_ref[...].T[:xmid_ref.shape[0]]
    routed = wt[:, 0:1] * _unpack_bf16_pairs(yg_ref[0])
    for k in range(1, yg_ref.shape[0]):
        routed = routed + wt[:, k:k + 1] * _unpack_bf16_pairs(yg_ref[k])
    x2 = xmid_ref[...] + g2_ref[...] * routed
    o_ref[...] = _rms_norm(x2, gf_ref[...]) * (1.0 + scf_ref[...]) + shf_ref[...]


def _final_prompt(xmid, yg, wts_t, mod, modf, mod_row0, g_final, b0, n_batch, out_prev):
    b, seq, d = xmid.shape
    k_top, _, w = yg.shape
    tl = FINAL_TILE
    nl = seq // tl

    def vec(col):
        return pl.BlockSpec((SUBLANES, d), lambda i, j: ((i + b0 + mod_row0) // SUBLANES, col))

    in_specs = [pl.BlockSpec((None, tl, d), lambda i, j: (i, j, 0)),
                pl.BlockSpec((k_top, tl, w), lambda i, j: (0, i * nl + j, 0)),
                pl.BlockSpec((k_top, tl), lambda i, j: (0, i * nl + j)),
                vec(5), vec(0), vec(1), pl.BlockSpec((1, d), lambda i, j: (0, 0))]
    args = [xmid, yg, wts_t, mod, modf, modf, g_final]
    aliases = {}
    if out_prev is not None:
        in_specs.append(pl.BlockSpec(memory_space=pl.ANY))
        args.append(out_prev)
        aliases = {7: 0}
    return pl.pallas_call(
        functools.partial(_final_kernel, vec_row0=b0 + mod_row0),
        grid=(b, nl),
        in_specs=in_specs,
        out_specs=pl.BlockSpec((None, tl, d), lambda i, j: (i + b0, j, 0)),
        out_shape=jax.ShapeDtypeStruct((n_batch, seq, d), F32),
        compiler_params=pltpu.CompilerParams(dimension_semantics=("arbitrary", "arbitrary"),
                                             vmem_limit_bytes=VMEM_LIMIT),
        input_output_aliases=aliases,
        name="final_prompt",
    )(*args)


def _final_sample(xmid_all, yg, wts_t, mod, modf, g_final, part, rows, first_row, after):
    d = xmid_all.shape[1]
    k_top, _, w = yg.shape

    def vec(col):
        return pl.BlockSpec((rows, d), lambda i: (part, col))

    return pl.pallas_call(
        _final_kernel,
        grid=(1,),
        in_specs=[vec(0),
                  pl.BlockSpec((k_top, rows, w), lambda i: (0, first_row // rows, 0)),
                  pl.BlockSpec((k_top, LANES), lambda i: (0, first_row // LANES)),
                  vec(5), vec(0), vec(1), pl.BlockSpec((1, d), lambda i: (0, 0)), pl.BlockSpec(memory_space=pl.ANY)],
        out_specs=pl.BlockSpec((rows, d), lambda i: (0, 0)),
        out_shape=jax.ShapeDtypeStruct((rows, d), F32),
        compiler_params=pltpu.CompilerParams(dimension_semantics=("arbitrary",),
                                             vmem_limit_bytes=VMEM_LIMIT),
        name="final_sample",
    )(xmid_all, yg, wts_t, mod, modf, modf, g_final, after)


def _block_diag_pairs(w_pool):
    g, c, _ = w_pool.shape
    eye = jnp.eye(g, dtype=w_pool.dtype)
    return (eye[:, None, :, None] * w_pool[:, :, None, :]).reshape(g * c, g * c)


def kernel(x_prompt, x_sample, state_pool, state_conv, c_prompt, c_sample, w_ada, b_ada, g_mix, w_in, w_pool, pool_scale, w_dw, b_dw, ln_g, ln_b, w_out, g_ffn, w_router, b_router, w_gate, w_up, w_down, ws_gate, ws_up, ws_down, w_ada_final, b_ada_final, g_final):
    bp, seq, d = x_prompt.shape
    bs = x_sample.shape[0]
    depth = w_ada.shape[0]
    assert depth == 1 and x_sample.shape[1] == 1
    conv_k = w_dw.shape[1]
    ne = w_router.shape[-1]
    per = ne // N_EXPERT_GROUPS

    row_expert = (jnp.arange(ne) % N_EXPERT_GROUPS) * per + jnp.arange(ne) // N_EXPERT_GROUPS
    wl = {
        "g_mix": g_mix[0][None, :],
        "in": w_in[0].astype(BF16),
        "pool": _block_diag_pairs(w_pool[0]).astype(BF16),
        "pool_scale": pool_scale[0][None, :],
        "dw": w_dw[0],
        "b_dw": b_dw[0][None, :],
        "ln_g": ln_g[0][None, :],
        "ln_b": ln_b[0][None, :],
        "out": w_out[0].astype(BF16),
        "g_ffn": g_ffn[0][None, :],
        "router_t": w_router[0].T[row_expert].astype(BF16),
        "s_gu": jnp.concatenate([ws_gate[0], ws_up[0]], axis=1).astype(BF16),
        "s_down": ws_down[0].astype(BF16),
    }
    b_perm = b_router[0][row_expert][:, None]

    assert bp % N_PARTS == 0 and bs % N_PARTS == 0
    bpp, bsp = bp // N_PARTS, bs // N_PARTS
    tpp = bpp * seq
    t_part = tpp + bsp
    grain = SC_WORKERS * SC_DISPATCH_CHUNK * SC_COMBINE_CHUNK // math.gcd(SC_DISPATCH_CHUNK, SC_COMBINE_CHUNK)
    t_pad = (t_part + grain - 1) // grain * grain
    assert t_pad % ROUTE_TILE == 0 and t_pad % DEST_TILE == 0 and tpp % bsp == 0

    c_all = jnp.concatenate([c_sample, c_prompt], axis=0)
    mod = _ada(c_all, w_ada[0], b_ada[0])
    modf = _ada(c_all, w_ada_final, b_ada_final)

    xmid_s, new_pool_t, new_conv_t, tails = _sample_mixer(
        x_sample.reshape(bs, d), mod, jnp.transpose(state_pool[0], (1, 0, 2)),
        jnp.transpose(state_conv[0], (1, 0, 2)), wl, tpp, t_pad, conv_k=conv_k)
    new_pool_s = jnp.transpose(new_pool_t, (1, 0, 2))
    new_conv_s = jnp.transpose(new_conv_t, (1, 0, 2))
    w_experts = tuple(a.reshape(a.shape[1:]) for a in (w_gate, w_up, w_down))

    n_blocks = (t_part * TOP_K + EXPERT_ROWS - 1) // EXPERT_ROWS + ne
    n_spare = -(-(t_pad - t_part) * TOP_K // EXPERT_ROWS)
    n_rows = (n_blocks + n_spare) * EXPERT_ROWS

    y_prompt, y_samples, npools, nconvs, w_bf = None, [], [], [], None
    mixed, after = [], modf
    for p in range(N_PARTS):
        h2, lgt = tails[p]
        xmid_p, h2, lgt, npool_p, nconv_p, *w_bf = _prompt_mixer(
            x_prompt, mod, bs, wl, h2, lgt, p * bpp, bpp, w_experts, w_bf, after, conv_k=conv_k)
        npools.append(npool_p)
        nconvs.append(nconv_p)
        idx, wts, rank, counts_perm = _route(lgt, b_perm, t_part)
        counts = counts_perm.astype(I32).reshape(per, N_EXPERT_GROUPS).T.reshape(ne)
        nblk = (counts + EXPERT_ROWS - 1) // EXPERT_ROWS
        first_block = jnp.concatenate([jnp.zeros((1,), I32), jnp.cumsum(nblk).astype(I32)])
        dest = _dest_rows(first_block[:ne] * EXPERT_ROWS, idx, rank, t_part, n_blocks * EXPERT_ROWS)
        blk = jnp.arange(n_blocks, dtype=I32)[:, None]
        inside = (blk >= first_block[None, :ne]) & (blk < first_block[None, 1:])
        rows_left = counts[None, :] - (blk - first_block[None, :ne]) * EXPERT_ROWS
        block_valid = jnp.sum(jnp.where(inside, jnp.minimum(rows_left, EXPERT_ROWS), 0), axis=1)
        mixed.append((xmid_p, h2, wts, dest, first_block, block_valid))
        after = dest
    for p in range(N_PARTS):
        xmid_p, h2, wts, dest, first_block, block_valid = mixed[p]

        def chunked(ch):
            return dest.reshape(TOP_K, t_pad // ch, ch).transpose(1, 0, 2)

        xs = _sc_dispatch(h2, chunked(SC_DISPATCH_CHUNK), n_rows)
        y = _experts(xs, w_bf[0], w_bf[1], first_block, block_valid)
        yg = _sc_combine_gather(y, chunked(SC_COMBINE_CHUNK), t_pad)

        y_samples.append(_final_sample(xmid_s, yg, wts, mod, modf, g_final[None, :], p, bsp, tpp,
                                       g_final if y_prompt is None else y_prompt))
        y_prompt = _final_prompt(xmid_p, yg, wts, mod, modf, bs, g_final[None, :], p * bpp, bp, y_prompt)
    y_sample = jnp.concatenate(y_samples, axis=0)
    npool_p = jnp.concatenate(npools, axis=0)
    nconv_p = jnp.concatenate(nconvs, axis=0)

    return (y_prompt, y_sample[:, None, :], npool_p[None], nconv_p[None], new_pool_s[None], new_conv_s[None])
```

```python
import functools
import math

import jax
import jax.numpy as jnp
from jax import lax
from jax.experimental import pallas as pl
from jax.experimental.pallas import tpu as pltpu
from jax.experimental.pallas import tpu_sc as plsc

POOL_WINDOWS = (2, 4, 8, 16)
N_EXPERT_GROUPS = 8
TOPK_GROUPS = 4
TOP_K = 8
ROUTED_SCALE = 2.5
EPS = 1e-6

LANES = 128
SUBLANES = 8
VMEM_LIMIT = 52 * 1024 * 1024

ADA_TILE = 1024
SEQ_TILE = 512
ROW_CHUNK = 16
ROUTE_TILE = 1408
EXPERT_ROWS = 1024
N_PARTS = 2
EXPERT_BUFFERS = 6
EXPERT_PIECES = 8
EXPERTS_PER_STEP = 2
FINAL_TILE = 512
DEST_TILE = 2816

SC_CORES = 2
SC_SUBCORES = 16
SC_WORKERS = SC_CORES * SC_SUBCORES
SC_DISPATCH_CHUNK = 88
SC_COMBINE_CHUNK = 24

F32 = jnp.float32
BF16 = jnp.bfloat16
I32 = jnp.int32
U32 = jnp.uint32
NEG_INF = float("-inf")
HI16 = 0xFFFF0000


def _sigmoid(x):
    return 1.0 / (1.0 + jnp.exp(-x))


def _silu(x):
    return x * _sigmoid(x)


def _rms_norm(x, g):
    return x * lax.rsqrt(jnp.mean(x * x, axis=-1, keepdims=True) + EPS) * g


def _dot(a, b):
    return jnp.dot(a, b, preferred_element_type=F32)


def _pack_bf16_pairs(x):
    w = x.shape[1] // 2
    bits = lax.bitcast_convert_type(x.astype(BF16).astype(F32), U32)
    return lax.bitcast_convert_type((bits[:, :w] >> 16) | (bits[:, w:] & U32(HI16)), I32)


def _unpack_bf16_pairs(p):
    bits = lax.bitcast_convert_type(p, U32)
    lo = lax.bitcast_convert_type(bits << 16, F32)
    hi = lax.bitcast_convert_type(bits & U32(HI16), F32)
    return jnp.concatenate([lo, hi], axis=1)


def _ada_kernel(c_ref, w_ref, b_ref, o_ref):
    s = _silu(c_ref[...])
    o_ref[...] = _dot(s.astype(BF16), w_ref[...].astype(BF16)) + b_ref[...]


def _ada(c, w, b):
    rows, d = c.shape
    n = w.shape[1]
    tn = ADA_TILE
    return pl.pallas_call(
        _ada_kernel,
        grid=(n // tn,),
        in_specs=[pl.BlockSpec((rows, d), lambda j: (0, 0)),
                  pl.BlockSpec((d, tn), lambda j: (0, j)),
                  pl.BlockSpec((1, tn), lambda j: (0, j))],
        out_specs=pl.BlockSpec((rows, tn), lambda j: (0, j)),
        out_shape=jax.ShapeDtypeStruct((rows, n), F32),
        compiler_params=pltpu.CompilerParams(dimension_semantics=("arbitrary",),
                                             vmem_limit_bytes=VMEM_LIMIT),
        name="ada",
    )(c, w, b.reshape(1, n))


def _mixer_tail(x, pool_d, conv_act, mod, w, xmid_ref, h2_ref, lgt_ref):
    sh2, sc2, g1, g2 = mod
    pw = pool_d.shape[1]
    pool_out = _dot(pool_d, w["pool"][...]) * w["pool_scale"][...]
    mix = _dot(pool_out.astype(BF16), w["out"][:pw, :]) + _dot(conv_act, w["out"][pw:, :])
    x1 = x + g1 * mix
    h2f = _rms_norm(x1, w["g_ffn"][...]) * (1.0 + sc2) + sh2
    h2_ref[...] = _pack_bf16_pairs(h2f)
    h2 = h2f.astype(BF16)
    lgt_ref[...] = lax.dot_general(w["router_t"][...], h2, (((1,), (1,)), ((), ())),
                                   preferred_element_type=F32)
    gu = _dot(h2, w["s_gu"][...])
    de = gu.shape[1] // 2
    hs = _silu(gu[:, :de]) * gu[:, de:]
    shared = _dot(hs.astype(BF16), w["s_down"][...])
    xmid_ref[...] = x1 + g2 * shared


_W_NAMES = ("g_mix", "in", "pool", "pool_scale", "dw", "b_dw", "ln_g", "ln_b", "out", "g_ffn",
            "router_t", "s_gu", "s_down")


def _layer_norm_silu(yc, g, b):
    mu = jnp.mean(yc, axis=-1, keepdims=True)
    yz = yc - mu
    var = jnp.mean(yz * yz, axis=-1, keepdims=True)
    return _silu(yz * lax.rsqrt(var + EPS) * g + b)


U_HALO = 32
A_HALO = 32


def _prompt_mixer_kernel(x_ref, mod_ref, *refs, tl, d, pw, cw, conv_k, mod_row0):
    nw = len(_W_NAMES)
    w = dict(zip(_W_NAMES, refs[:nw]))
    wg_ref, wu_ref, wd_ref = refs[nw + 2:nw + 5]
    xmid_ref, h2_ref, lgt_ref, npool_ref, nconv_ref, wgu_bf_ref, wd_bf_ref = refs[-16:-9]
    ubuf, s2buf, s4buf, s8buf, abuf, ashift, dbuf, cbuf, rowb = refs[-9:]
    de = wg_ref.shape[2]
    wgu_bf_ref[:, :, :de] = wg_ref[...].astype(BF16)
    wgu_bf_ref[:, :, de:] = wu_ref[...].astype(BF16)
    wd_bf_ref[...] = wd_ref[...].astype(BF16)
    l = pl.program_id(1)
    pool_buf = max(POOL_WINDOWS) - 1
    uh, ah = U_HALO, A_HALO
    pg = pw // len(POOL_WINDOWS)
    nu, na = uh + tl, ah + tl

    @pl.when(l == 0)
    def _():
        ubuf[0:uh, :] = jnp.zeros((uh, pw), F32)
        abuf[0:ah, :] = jnp.zeros((ah, cw), F32)

    @pl.when(l > 0)
    def _():
        ubuf[0:uh, :] = ubuf[tl:tl + uh, :]
        abuf[0:ah, :] = abuf[tl:tl + ah, :]

    x = x_ref[0]
    row = pl.ds((mod_row0 + pl.program_id(0)) % SUBLANES, 1)
    sh1, sc1, g1, sh2, sc2, g2 = [mod_ref[row, i * d:(i + 1) * d] for i in range(6)]

    h = _rms_norm(x, w["g_mix"][...]) * (1.0 + sc1) + sh1
    z = _dot(h.astype(BF16), w["in"][...])
    u = z[:, :pw]
    ubuf[uh:nu, :] = u
    abuf[ah:na, :] = z[:, pw:pw + cw] * _sigmoid(z[:, pw + cw:])

    s2buf[8:nu, :] = ubuf[8:nu, :] + ubuf[7:nu - 1, :]
    s4buf[16:nu, :] = s2buf[16:nu, pg:] + s2buf[14:nu - 2, pg:]
    s8buf[24:nu, :] = s4buf[24:nu, pg:] + s4buf[20:nu - 4, pg:]
    t = l * tl + lax.broadcasted_iota(I32, (tl, 1), 0)
    sums = (s2buf[uh:nu, 0:pg], s4buf[uh:nu, 0:pg], s8buf[uh:nu, 0:pg],
            s8buf[uh:nu, pg:2 * pg] + s8buf[uh - 8:nu - 8, pg:2 * pg])
    for g, win in enumerate(POOL_WINDOWS):
        cols = slice(g * pg, (g + 1) * pg)
        inv = 1.0 / jnp.minimum(win, t + 1).astype(F32)
        dbuf[:, cols] = (sums[g] * inv - u[:, cols]).astype(BF16)

    for r in range(1, SUBLANES):
        ashift[r - 1, 8:na, :] = abuf[8 - r:na - r, :]

    for k in range(conv_k):
        rowb[k] = jnp.broadcast_to(w["dw"][k:k + 1, :], (SUBLANES, cw))
    rowb[conv_k] = jnp.broadcast_to(w["b_dw"][...], (SUBLANES, cw))

    for c in range(tl // ROW_CHUNK):
        groups = []
        for o in range(ah + c * ROW_CHUNK, ah + (c + 1) * ROW_CHUNK, SUBLANES):
            yc = abuf[o:o + SUBLANES, :] * rowb[conv_k - 1]
            for back in range(1, conv_k):
                q, r = divmod(back, SUBLANES)
                src = abuf if r == 0 else ashift.at[r - 1]
                yc = yc + src[o - q * SUBLANES:o - (q - 1) * SUBLANES, :] * rowb[conv_k - 1 - back]
            groups.append(yc + rowb[conv_k])
        r0 = c * ROW_CHUNK
        cbuf[r0:r0 + ROW_CHUNK, :] = _layer_norm_silu(
            jnp.concatenate(groups, axis=0), w["ln_g"][...], w["ln_b"][...]).astype(BF16)

    @pl.when(l == pl.num_programs(1) - 1)
    def _():
        npool_ref[0] = ubuf[nu - pool_buf:nu, :]
        nconv_ref[0] = abuf[na - (conv_k - 1):na, :]

    _mixer_tail(x, dbuf[...], cbuf[...], (sh2, sc2, g1, g2), w, xmid_ref.at[0], h2_ref, lgt_ref)


def _sample_mixer_kernel(x_ref, mod_ref, sp_ref, sc_ref, *refs, d, pw, cw, conv_k):
    nw = len(_W_NAMES)
    w = dict(zip(_W_NAMES, refs[:nw]))
    xmid_ref, npool_ref, nconv_ref = refs[nw:nw + 3]
    tails = refs[nw + 3:-2]
    h2_tmp, lgt_tmp = refs[-2:]
    pool_buf = max(POOL_WINDOWS) - 1
    pg = pw // len(POOL_WINDOWS)
    x = x_ref[...]
    sh1, sc1, g1, sh2, sc2, g2 = [mod_ref[:, i * d:(i + 1) * d] for i in range(6)]
    h = _rms_norm(x, w["g_mix"][...]) * (1.0 + sc1) + sh1
    z = _dot(h.astype(BF16), w["in"][...])
    u = z[:, :pw]
    a = z[:, pw:pw + cw] * _sigmoid(z[:, pw + cw:])
    npool_ref[0:pool_buf - 1] = sp_ref[1:pool_buf]
    npool_ref[pool_buf - 1] = u
    nconv_ref[0:conv_k - 2] = sc_ref[1:conv_k - 1]
    nconv_ref[conv_k - 2] = a
    ds = []
    for g, win in enumerate(POOL_WINDOWS):
        cols = slice(g * pg, (g + 1) * pg)
        acc = u[:, cols]
        for j in range(1, win):
            acc = acc + sp_ref[pool_buf - j, :, cols]
        cnt = float(min(win, 1 + pool_buf))
        ds.append(acc / cnt - u[:, cols])
    pool_d = jnp.concatenate(ds, axis=-1).astype(BF16)
    yc = a * w["dw"][conv_k - 1:conv_k, :]
    for k in range(conv_k - 1):
        yc = yc + sc_ref[k] * w["dw"][k:k + 1, :]
    yc = yc + w["b_dw"][...]
    conv_act = _layer_norm_silu(yc, w["ln_g"][...], w["ln_b"][...]).astype(BF16)
    _mixer_tail(x, pool_d, conv_act, (sh2, sc2, g1, g2), w, xmid_ref, h2_tmp, lgt_tmp)
    n_parts = len(tails) // 2
    share = x.shape[0] // n_parts
    for p in range(n_parts):
        h2_ref, lgt_ref = tails[2 * p], tails[2 * p + 1]
        h2_ref[...] = jnp.zeros_like(h2_ref)
        lgt_ref[...] = jnp.zeros_like(lgt_ref)
        h2_ref[0:share, :] = h2_tmp[p * share:(p + 1) * share, :]
        lgt_ref[:, 0:share] = lgt_tmp[:, p * share:(p + 1) * share]


def _full_spec(a):
    nd = a.ndim
    return pl.BlockSpec(a.shape, lambda *_: (0,) * nd)


def _prompt_mixer(x, mod, mod_row0, wl, h2_all, lgt_all, b0, b, w_experts, w_bf_prev, after, *, conv_k):
    n_batch, seq, d = x.shape
    w_gate, w_up, w_down = w_experts
    n_exp, _, de = w_gate.shape
    pw = wl["pool"].shape[0]
    cw = wl["dw"].shape[1]
    ne = wl["router_t"].shape[0]
    tl = SEQ_TILE
    nl = seq // tl
    pg = pw // len(POOL_WINDOWS)
    assert POOL_WINDOWS == (2, 4, 8, 16) and conv_k - 1 <= A_HALO
    ws = [wl[n] for n in _W_NAMES]
    kern = functools.partial(_prompt_mixer_kernel, tl=tl, d=d, pw=pw, cw=cw, conv_k=conv_k, mod_row0=mod_row0 + b0)
    ecs = n_exp // (n_batch * nl)
    assert ecs * n_batch * nl == n_exp

    def w_map(i, j):
        return ((i + b0) * nl + j, 0, 0)

    ins = [x, mod, *ws, h2_all, lgt_all, w_gate, w_up, w_down]
    in_specs = ([pl.BlockSpec((1, tl, d), lambda i, j: (i + b0, j, 0)),
                 pl.BlockSpec((SUBLANES, mod.shape[-1]), lambda i, j: ((i + b0 + mod_row0) // SUBLANES, 0))]
                + [_full_spec(a) for a in ws]
                + [pl.BlockSpec(memory_space=pl.ANY), pl.BlockSpec(memory_space=pl.ANY),
                   pl.BlockSpec((ecs, d, de), w_map), pl.BlockSpec((ecs, d, de), w_map),
                   pl.BlockSpec((ecs, de, d), w_map)])
    aliases = {2 + len(ws): 1, 3 + len(ws): 2}
    if w_bf_prev is not None:
        aliases.update({len(ins): 5, len(ins) + 1: 6})
        ins += list(w_bf_prev)
        in_specs += [pl.BlockSpec(memory_space=pl.ANY), pl.BlockSpec(memory_space=pl.ANY)]
    if after is not None:
        ins.append(after)
        in_specs.append(pl.BlockSpec(memory_space=pl.ANY))
    return pl.pallas_call(
        kern,
        grid=(b, nl),
        in_specs=in_specs,
        out_specs=[pl.BlockSpec((1, tl, d), lambda i, j: (i, j, 0)),
                   pl.BlockSpec((tl, d // 2), lambda i, j: (i * nl + j, 0)),
                   pl.BlockSpec((ne, tl), lambda i, j: (0, i * nl + j)),
                   pl.BlockSpec((1, max(POOL_WINDOWS) - 1, pw), lambda i, j: (i, 0, 0)),
                   pl.BlockSpec((1, conv_k - 1, cw), lambda i, j: (i, 0, 0)),
                   pl.BlockSpec((ecs, d, 2 * de), w_map), pl.BlockSpec((ecs, de, d), w_map)],
        out_shape=[jax.ShapeDtypeStruct((b, seq, d), F32),
                   jax.ShapeDtypeStruct(h2_all.shape, I32),
                   jax.ShapeDtypeStruct(lgt_all.shape, F32),
                   jax.ShapeDtypeStruct((b, max(POOL_WINDOWS) - 1, pw), F32),
                   jax.ShapeDtypeStruct((b, conv_k - 1, cw), F32),
                   jax.ShapeDtypeStruct((n_exp, d, 2 * de), BF16),
                   jax.ShapeDtypeStruct((n_exp, de, d), BF16)],
        scratch_shapes=[pltpu.VMEM((U_HALO + tl, pw), F32), pltpu.VMEM((U_HALO + tl, pw), F32),
                        pltpu.VMEM((U_HALO + tl, pw - pg), F32), pltpu.VMEM((U_HALO + tl, pw - 2 * pg), F32),
                        pltpu.VMEM((A_HALO + tl, cw), F32), pltpu.VMEM((SUBLANES - 1, A_HALO + tl, cw), F32),
                        pltpu.VMEM((tl, pw), BF16), pltpu.VMEM((tl, cw), BF16),
                        pltpu.VMEM((conv_k + 1, SUBLANES, cw), F32)],
        compiler_params=pltpu.CompilerParams(dimension_semantics=("arbitrary", "arbitrary"),
                                             vmem_limit_bytes=VMEM_LIMIT),
        input_output_aliases=aliases,
        name="prompt_mixer",
    )(*ins)


def _sample_mixer(x, mod, sp, sc, wl, t_prompt, t_pad, *, conv_k):
    rows, d = x.shape
    tail = t_pad - t_prompt
    assert t_prompt % tail == 0 and rows % N_PARTS == 0 and rows // N_PARTS <= tail
    pw = wl["pool"].shape[0]
    cw = wl["dw"].shape[1]
    ne = wl["router_t"].shape[0]
    ws = [wl[n] for n in _W_NAMES]
    kern = functools.partial(_sample_mixer_kernel, d=d, pw=pw, cw=cw, conv_k=conv_k)
    ins = [x, mod, sp, sc] + ws
    tail_specs = [pl.BlockSpec((tail, d // 2), lambda i: (t_prompt // tail, 0)),
                  pl.BlockSpec((ne, tail), lambda i: (0, t_prompt // tail))] * N_PARTS
    tail_shapes = [jax.ShapeDtypeStruct((t_pad, d // 2), I32), jax.ShapeDtypeStruct((ne, t_pad), F32)] * N_PARTS
    outs = pl.pallas_call(
        kern,
        grid=(1,),
        in_specs=[_full_spec(x), pl.BlockSpec((rows, mod.shape[1]), lambda i: (0, 0))]
                 + [_full_spec(a) for a in ins[2:]],
        out_specs=[pl.BlockSpec((rows, d), lambda i: (0, 0)), _full_spec(sp), _full_spec(sc)] + tail_specs,
        out_shape=[jax.ShapeDtypeStruct((rows, d), F32),
                   jax.ShapeDtypeStruct(sp.shape, F32),
                   jax.ShapeDtypeStruct(sc.shape, F32)] + tail_shapes,
        scratch_shapes=[pltpu.VMEM((rows, d // 2), I32), pltpu.VMEM((ne, rows), F32)],
        compiler_params=pltpu.CompilerParams(dimension_semantics=("arbitrary",),
                                             vmem_limit_bytes=VMEM_LIMIT),
        name="sample_mixer",
    )(*ins)
    return outs[0], outs[1], outs[2], [(outs[3 + 2 * p], outs[4 + 2 * p]) for p in range(N_PARTS)]


def _sublane_max(x):
    return jnp.max(x, axis=0, keepdims=True)


def _route_kernel(lg_ref, b_ref, tri_ref, idx_ref, wt_ref, rank_ref, cnt_ref, carry, *, t_valid):
    i = pl.program_id(0)
    ne, tt = lg_ref.shape
    ng = N_EXPERT_GROUPS
    per = ne // ng

    @pl.when(i == 0)
    def _():
        carry[...] = jnp.zeros_like(carry)

    s = _sigmoid(lg_ref[...])
    sel = s + b_ref[...]
    s3 = [s[p * ng:(p + 1) * ng, :] for p in range(per)]
    sel3 = [sel[p * ng:(p + 1) * ng, :] for p in range(per)]
    m1 = sel3[0]
    m2 = jnp.full_like(m1, NEG_INF)
    for p in range(1, per):
        m2 = jnp.maximum(m2, jnp.minimum(m1, sel3[p]))
        m1 = jnp.maximum(m1, sel3[p])
    gs = m1 + m2
    gi = lax.broadcasted_iota(jnp.int32, (ng, tt), 0)
    beaten = jnp.zeros((ng, tt), jnp.int32)
    for g in range(ng):
        row = gs[g:g + 1, :]
        beats = (row > gs) | ((row == gs) & (gi > g))
        beaten = beaten + beats.astype(jnp.int32)
    keep = beaten < TOPK_GROUPS
    cur = [jnp.where(keep, sel3[p], NEG_INF) for p in range(per)]
    eid = [(gi * per + p).astype(F32) for p in range(per)]
    idxs, wts, hits = [], [], []
    for _ in range(TOP_K):
        m = cur[0]
        for p in range(1, per):
            m = jnp.maximum(m, cur[p])
        m = _sublane_max(m)
        cand = jnp.where(cur[0] == m, eid[0], float(ne))
        for p in range(1, per):
            cand = jnp.minimum(cand, jnp.where(cur[p] == m, eid[p], float(ne)))
        e_sel = jnp.min(cand, axis=0, keepdims=True)
        hit = [eid[p] == e_sel for p in range(per)]
        wk = jnp.where(hit[0], s3[0], 0.0)
        for p in range(1, per):
            wk = wk + jnp.where(hit[p], s3[p], 0.0)
        wts.append(jnp.sum(wk, axis=0, keepdims=True))
        cur = [jnp.where(hit[p], NEG_INF, cur[p]) for p in range(per)]
        idxs.append(e_sel)
        hits.append(hit)
    wsum = wts[0]
    for k in range(1, TOP_K):
        wsum = wsum + wts[k]
    chosen = [functools.reduce(lambda a, b: a | b, [hits[k][p] for k in range(TOP_K)]) for p in range(per)]
    real = (i * tt + lax.broadcasted_iota(I32, (1, tt), 1)) < t_valid
    onehot = jnp.concatenate([(c & real).astype(F32) for c in chosen], axis=0)
    before = _dot(onehot.astype(BF16), tri_ref[...]) + carry[...]
    for k in range(TOP_K):
        rk = jnp.where(hits[k][0], before[0:ng, :], 0.0)
        for p in range(1, per):
            rk = rk + jnp.where(hits[k][p], before[p * ng:(p + 1) * ng, :], 0.0)
        rank_ref[k:k + 1, :] = jnp.sum(rk, axis=0, keepdims=True).astype(jnp.int32)
        idx_ref[k:k + 1, :] = idxs[k].astype(jnp.int32)
        wt_ref[k:k + 1, :] = wts[k] / wsum * ROUTED_SCALE
    carry[...] = carry[...] + jnp.sum(onehot, axis=1, keepdims=True)
    cnt_ref[...] = carry[...]


def _route(lgt, b_perm, t_valid):
    ne, t = lgt.shape
    tt = ROUTE_TILE
    tri = (jnp.arange(tt)[:, None] < jnp.arange(tt)[None, :]).astype(BF16)
    return pl.pallas_call(
        functools.partial(_route_kernel, t_valid=t_valid),
        grid=(t // tt,),
        in_specs=[pl.BlockSpec((ne, tt), lambda i: (0, i)),
                  pl.BlockSpec((ne, 1), lambda i: (0, 0)),
                  pl.BlockSpec((tt, tt), lambda i: (0, 0))],
        out_specs=[pl.BlockSpec((TOP_K, tt), lambda i: (0, i)),
                   pl.BlockSpec((TOP_K, tt), lambda i: (0, i)),
                   pl.BlockSpec((TOP_K, tt), lambda i: (0, i)),
                   pl.BlockSpec((ne, 1), lambda i: (0, 0))],
        out_shape=[jax.ShapeDtypeStruct((TOP_K, t), jnp.int32),
                   jax.ShapeDtypeStruct((TOP_K, t), F32),
                   jax.ShapeDtypeStruct((TOP_K, t), jnp.int32),
                   jax.ShapeDtypeStruct((ne, 1), F32)],
        scratch_shapes=[pltpu.VMEM((ne, 1), F32)],
        compiler_params=pltpu.CompilerParams(dimension_semantics=("arbitrary",),
                                             vmem_limit_bytes=VMEM_LIMIT),
        name="route",
    )(lgt, b_perm, tri)


def _dest_kernel(pstart_ref, idx_ref, rank_ref, dest_ref, *, t_valid, spare_row):
    idx = idx_ref[...]
    k_top, tile = idx.shape

    def add_start(e, acc):
        return acc + jnp.where(idx == e, pstart_ref[e], 0)

    dest = lax.fori_loop(0, pstart_ref.shape[0], add_start, rank_ref[...])
    tok = pl.program_id(0) * tile + lax.broadcasted_iota(I32, idx.shape, 1)
    spare = spare_row + (tok - t_valid) * k_top + lax.broadcasted_iota(I32, idx.shape, 0)
    dest_ref[...] = jnp.where(tok < t_valid, dest, spare)


def _dest_rows(pstart, idx, rank, t_valid, spare_row):
    k_top, t = idx.shape
    tile = DEST_TILE
    spec = pl.BlockSpec((k_top, tile), lambda i, ps: (0, i))
    return pl.pallas_call(
        functools.partial(_dest_kernel, t_valid=t_valid, spare_row=spare_row),
        grid_spec=pltpu.PrefetchScalarGridSpec(num_scalar_prefetch=1, grid=(t // tile,),
                                               in_specs=[spec, spec], out_specs=spec),
        out_shape=jax.ShapeDtypeStruct((k_top, t), I32),
        compiler_params=pltpu.CompilerParams(dimension_semantics=("arbitrary",),
                                             vmem_limit_bytes=VMEM_LIMIT),
        name="dest_rows",
    )(pstart, idx, rank)


def _expert_kernel(first_ref, valid_ref, xs_hbm, wgu_ref, wd_ref, y_hbm, xbuf, ybuf, in_sem, out_sem):
    step = pl.program_id(0)
    ne = first_ref.shape[0] - 1
    nbuf, rows = xbuf.shape[:2]
    piece = rows // EXPERT_PIECES
    per_step, de = wd_ref.shape[:2]
    n_total = first_ref[ne]

    def in_copy(g, h):
        return pltpu.make_async_copy(xs_hbm.at[pl.ds(g * rows + h * piece, piece)],
                                     xbuf.at[g % nbuf, pl.ds(h * piece, piece)], in_sem.at[g % nbuf])

    def out_copy(g, h):
        return pltpu.make_async_copy(ybuf.at[g % nbuf, pl.ds(h * piece, piece)],
                                     y_hbm.at[pl.ds(g * rows + h * piece, piece)], out_sem.at[g % nbuf])

    def used_pieces(copy, g, act):
        getattr(copy(g, 0), act)()
        for h in range(1, EXPERT_PIECES):
            @pl.when(valid_ref[g] > h * piece)
            def _():
                getattr(copy(g, h), act)()

    @pl.when(step == 0)
    def _():
        for g in range(nbuf - 1):
            @pl.when(g < n_total)
            def _():
                used_pieces(in_copy, g, "start")

    def block(g, carry, *, j):
        slot = g % nbuf

        @pl.when(g + nbuf - 1 < n_total)
        def _():
            used_pieces(in_copy, g + nbuf - 1, "start")

        used_pieces(in_copy, g, "wait")

        @pl.when(g >= nbuf)
        def _():
            used_pieces(out_copy, g - nbuf, "wait")

        valid = valid_ref[g]

        def run(n):
            rid = lax.broadcasted_iota(I32, (n, 1), 0)
            xs = jnp.where(rid < valid, _unpack_bf16_pairs(xbuf[slot, 0:n]), 0.0).astype(BF16)
            gu = _dot(xs, wgu_ref[j])
            hb = _silu(gu[:, :de]) * gu[:, de:]
            ybuf[slot, 0:n] = _pack_bf16_pairs(_dot(hb.astype(BF16), wd_ref[j]))

        for h in range(EXPERT_PIECES):
            @pl.when((valid > h * piece) & (valid <= (h + 1) * piece))
            def _():
                run((h + 1) * piece)

        used_pieces(out_copy, g, "start")
        return carry

    for j in range(per_step):
        e = step * per_step + j
        lax.fori_loop(first_ref[e], first_ref[e + 1], functools.partial(block, j=j), 0)

    @pl.when(step == pl.num_programs(0) - 1)
    def _():
        for back in range(nbuf, 0, -1):
            @pl.when(n_total >= back)
            def _():
                used_pieces(out_copy, n_total - back, "wait")


def _experts(xs, w_gu, w_down, first_block, block_valid):
    p, dw = xs.shape
    ne, de, d = w_down.shape
    return pl.pallas_call(
        _expert_kernel,
        grid_spec=pltpu.PrefetchScalarGridSpec(
            num_scalar_prefetch=2,
            grid=(ne // EXPERTS_PER_STEP,),
            in_specs=[pl.BlockSpec(memory_space=pl.ANY),
                      pl.BlockSpec((EXPERTS_PER_STEP, d, 2 * de), lambda e, fb, cn: (e, 0, 0)),
                      pl.BlockSpec((EXPERTS_PER_STEP, de, d), lambda e, fb, cn: (e, 0, 0))],
            out_specs=pl.BlockSpec(memory_space=pl.ANY),
            scratch_shapes=[pltpu.VMEM((EXPERT_BUFFERS, EXPERT_ROWS, dw), I32),
                            pltpu.VMEM((EXPERT_BUFFERS, EXPERT_ROWS, dw), I32),
                            pltpu.SemaphoreType.DMA((EXPERT_BUFFERS,)), pltpu.SemaphoreType.DMA((EXPERT_BUFFERS,))]),
        out_shape=jax.ShapeDtypeStruct((p, dw), I32),
        compiler_params=pltpu.CompilerParams(dimension_semantics=("arbitrary",),
                                             vmem_limit_bytes=VMEM_LIMIT),
        name="experts",
    )(first_block, block_valid, xs, w_gu, w_down)


def _sc_first_chunk(n_chunks):
    per_worker = n_chunks // SC_WORKERS
    assert per_worker * SC_WORKERS == n_chunks
    return per_worker, (lax.axis_index("s") * SC_CORES + lax.axis_index("c")) * per_worker


def _sc_start(copies):
    for cp in copies:
        cp.start()


def _sc_wait(copies):
    for cp in copies:
        cp.wait()


def _sc_dispatch(h2, dest, n_rows):
    w = h2.shape[1]
    n_chunks, k_top, ch = dest.shape
    mesh = plsc.VectorSubcoreMesh(core_axis_name="c", subcore_axis_name="s")

    @functools.partial(
        pl.kernel, mesh=mesh, out_type=jax.ShapeDtypeStruct((n_rows, w), I32),
        scratch_types=[pltpu.VMEM((2, k_top, ch), I32), pltpu.VMEM((2, ch, w), I32),
                       pltpu.SemaphoreType.DMA((2,)), pltpu.SemaphoreType.DMA((2,))])
    def dispatch(h2_hbm, dest_hbm, xs_hbm, idx_v, rows_v, load_sem, store_sem):
        per_worker, first = _sc_first_chunk(n_chunks)

        def loads(i):
            slot = i % 2
            return (pltpu.make_async_copy(h2_hbm.at[pl.ds((first + i) * ch, ch)], rows_v.at[slot], load_sem.at[slot]),
                    pltpu.make_async_copy(dest_hbm.at[first + i], idx_v.at[slot], load_sem.at[slot]))

        def scatters(i):
            slot = i % 2
            return [pltpu.make_async_copy(rows_v.at[slot], xs_hbm.at[idx_v.at[slot, k]], store_sem.at[slot])
                    for k in range(k_top)]

        for i in range(min(2, per_worker)):
            _sc_start(loads(i))
        for i in range(per_worker):
            _sc_wait(loads(i))
            _sc_start(scatters(i))
            if 1 <= i < per_worker - 1:
                _sc_wait(scatters(i - 1))
                _sc_start(loads(i + 1))
        for i in range(max(per_worker - 2, 0), per_worker):
            _sc_wait(scatters(i))

    return dispatch(h2, dest)


def _sc_combine_gather(y, dest, t):
    w = y.shape[1]
    n_chunks, k_top, ch = dest.shape
    n_local = n_chunks // SC_WORKERS
    mesh = plsc.VectorSubcoreMesh(core_axis_name="c", subcore_axis_name="s")

    @functools.partial(
        pl.kernel, mesh=mesh, out_type=jax.ShapeDtypeStruct((k_top, t, w), I32),
        scratch_types=[pltpu.VMEM((n_local, k_top, ch), I32), pltpu.VMEM((k_top, ch, w), I32),
                       pltpu.SemaphoreType.DMA((k_top,)), pltpu.SemaphoreType.DMA((k_top,))])
    def gather(y_hbm, dest_hbm, out_hbm, idx_v, rows_v, load_sem, store_sem):
        per_worker, first = _sc_first_chunk(n_chunks)
        pltpu.sync_copy(dest_hbm.at[pl.ds(first, per_worker)], idx_v)

        def fetch(i, k):
            return pltpu.make_async_copy(y_hbm.at[idx_v.at[i, k]], rows_v.at[k], load_sem.at[k])

        def put(i, k):
            return pltpu.make_async_copy(rows_v.at[k], out_hbm.at[k, pl.ds((first + i) * ch, ch)], store_sem.at[k])

        @pl.loop(0, per_worker)
        def _(i):
            for k in range(k_top):
                @pl.when(i > 0)
                def _():
                    put(i - 1, k).wait()

                fetch(i, k).start()
            for k in range(k_top):
                fetch(i, k).wait()
                put(i, k).start()

        for k in range(k_top):
            put(per_worker - 1, k).wait()

    return gather(y, dest)


def _final_kernel(xmid_ref, yg_ref, wt_ref, g2_ref, shf_ref, scf_ref, gf_ref, *rest, vec_row0=None):
    o_ref = rest[-1]
    if vec_row0 is not None:
        row = pl.ds((vec_row0 + pl.program_id(0)) % SUBLANES, 1)
        g2_ref, shf_ref, scf_ref = g2_ref.at[row, :], shf_ref.at[row, :], scf_ref.at[row, :]
    wt = wt_ref[...].T[:xmid_ref.shape[0]]
    routed = wt[:, 0:1] * _unpack_bf16_pairs(yg_ref[0])
    for k in range(1, yg_ref.shape[0]):
        routed = routed + wt[:, k:k + 1] * _unpack_bf16_pairs(yg_ref[k])
    x2 = xmid_ref[...] + g2_ref[...] * routed
    o_ref[...] = _rms_norm(x2, gf_ref[...]) * (1.0 + scf_ref[...]) + shf_ref[...]


def _final_prompt(xmid, yg, wts_t, mod, modf, mod_row0, g_final, b0, n_batch, out_prev):
    b, seq, d = xmid.shape
    k_top, _, w = yg.shape
    tl = FINAL_TILE
    nl = seq // tl

    def vec(col):
        return pl.BlockSpec((SUBLANES, d), lambda i, j: ((i + b0 + mod_row0) // SUBLANES, col))

    in_specs = [pl.BlockSpec((None, tl, d), lambda i, j: (i, j, 0)),
                pl.BlockSpec((k_top, tl, w), lambda i, j: (0, i * nl + j, 0)),
                pl.BlockSpec((k_top, tl), lambda i, j: (0, i * nl + j)),
                vec(5), vec(0), vec(1), pl.BlockSpec((1, d), lambda i, j: (0, 0))]
    args = [xmid, yg, wts_t, mod, modf, modf, g_final]
    aliases = {}
    if out_prev is not None:
        in_specs.append(pl.BlockSpec(memory_space=pl.ANY))
        args.append(out_prev)
        aliases = {7: 0}
    return pl.pallas_call(
        functools.partial(_final_kernel, vec_row0=b0 + mod_row0),
        grid=(b, nl),
        in_specs=in_specs,
        out_specs=pl.BlockSpec((None, tl, d), lambda i, j: (i + b0, j, 0)),
        out_shape=jax.ShapeDtypeStruct((n_batch, seq, d), F32),
        compiler_params=pltpu.CompilerParams(dimension_semantics=("arbitrary", "arbitrary"),
                                             vmem_limit_bytes=VMEM_LIMIT),
        input_output_aliases=aliases,
        name="final_prompt",
    )(*args)


def _final_sample(xmid_all, yg, wts_t, mod, modf, g_final, part, rows, first_row, after):
    d = xmid_all.shape[1]
    k_top, _, w = yg.shape

    def vec(col):
        return pl.BlockSpec((rows, d), lambda i: (part, col))

    return pl.pallas_call(
        _final_kernel,
        grid=(1,),
        in_specs=[vec(0),
                  pl.BlockSpec((k_top, rows, w), lambda i: (0, first_row // rows, 0)),
                  pl.BlockSpec((k_top, LANES), lambda i: (0, first_row // LANES)),
                  vec(5), vec(0), vec(1), pl.BlockSpec((1, d), lambda i: (0, 0)), pl.BlockSpec(memory_space=pl.ANY)],
        out_specs=pl.BlockSpec((rows, d), lambda i: (0, 0)),
        out_shape=jax.ShapeDtypeStruct((rows, d), F32),
        compiler_params=pltpu.CompilerParams(dimension_semantics=("arbitrary",),
                                             vmem_limit_bytes=VMEM_LIMIT),
        name="final_sample",
    )(xmid_all, yg, wts_t, mod, modf, modf, g_final, after)


def _block_diag_pairs(w_pool):
    g, c, _ = w_pool.shape
    eye = jnp.eye(g, dtype=w_pool.dtype)
    return (eye[:, None, :, None] * w_pool[:, :, None, :]).reshape(g * c, g * c)


def kernel(x_prompt, x_sample, state_pool, state_conv, c_prompt, c_sample, w_ada, b_ada, g_mix, w_in, w_pool, pool_scale, w_dw, b_dw, ln_g, ln_b, w_out, g_ffn, w_router, b_router, w_gate, w_up, w_down, ws_gate, ws_up, ws_down, w_ada_final, b_ada_final, g_final):
    bp, seq, d = x_prompt.shape
    bs = x_sample.shape[0]
    depth = w_ada.shape[0]
    assert depth == 1 and x_sample.shape[1] == 1
    conv_k = w_dw.shape[1]
    ne = w_router.shape[-1]
    per = ne // N_EXPERT_GROUPS

    row_expert = (jnp.arange(ne) % N_EXPERT_GROUPS) * per + jnp.arange(ne) // N_EXPERT_GROUPS
    wl = {
        "g_mix": g_mix[0][None, :],
        "in": w_in[0].astype(BF16),
        "pool": _block_diag_pairs(w_pool[0]).astype(BF16),
        "pool_scale": pool_scale[0][None, :],
        "dw": w_dw[0],
        "b_dw": b_dw[0][None, :],
        "ln_g": ln_g[0][None, :],
        "ln_b": ln_b[0][None, :],
        "out": w_out[0].astype(BF16),
        "g_ffn": g_ffn[0][None, :],
        "router_t": w_router[0].T[row_expert].astype(BF16),
        "s_gu": jnp.concatenate([ws_gate[0], ws_up[0]], axis=1).astype(BF16),
        "s_down": ws_down[0].astype(BF16),
    }
    b_perm = b_router[0][row_expert][:, None]

    assert bp % N_PARTS == 0 and bs % N_PARTS == 0
    bpp, bsp = bp // N_PARTS, bs // N_PARTS
    tpp = bpp * seq
    t_part = tpp + bsp
    grain = SC_WORKERS * SC_DISPATCH_CHUNK * SC_COMBINE_CHUNK // math.gcd(SC_DISPATCH_CHUNK, SC_COMBINE_CHUNK)
    t_pad = (t_part + grain - 1) // grain * grain
    assert t_pad % ROUTE_TILE == 0 and t_pad % DEST_TILE == 0 and tpp % bsp == 0

    c_all = jnp.concatenate([c_sample, c_prompt], axis=0)
    mod = _ada(c_all, w_ada[0], b_ada[0])
    modf = _ada(c_all, w_ada_final, b_ada_final)

    xmid_s, new_pool_t, new_conv_t, tails = _sample_mixer(
        x_sample.reshape(bs, d), mod, jnp.transpose(state_pool[0], (1, 0, 2)),
        jnp.transpose(state_conv[0], (1, 0, 2)), wl, tpp, t_pad, conv_k=conv_k)
    new_pool_s = jnp.transpose(new_pool_t, (1, 0, 2))
    new_conv_s = jnp.transpose(new_conv_t, (1, 0, 2))
    w_experts = tuple(a.reshape(a.shape[1:]) for a in (w_gate, w_up, w_down))

    n_blocks = (t_part * TOP_K + EXPERT_ROWS - 1) // EXPERT_ROWS + ne
    n_spare = -(-(t_pad - t_part) * TOP_K // EXPERT_ROWS)
    n_rows = (n_blocks + n_spare) * EXPERT_ROWS

    y_prompt, y_samples, npools, nconvs, w_bf = None, [], [], [], None
    mixed, after = [], modf
    for p in range(N_PARTS):
        h2, lgt = tails[p]
        xmid_p, h2, lgt, npool_p, nconv_p, *w_bf = _prompt_mixer(
            x_prompt, mod, bs, wl, h2, lgt, p * bpp, bpp, w_experts, w_bf, after, conv_k=conv_k)
        npools.append(npool_p)
        nconvs.append(nconv_p)
        idx, wts, rank, counts_perm = _route(lgt, b_perm, t_part)
        counts = counts_perm.astype(I32).reshape(per, N_EXPERT_GROUPS).T.reshape(ne)
        nblk = (counts + EXPERT_ROWS - 1) // EXPERT_ROWS
        first_block = jnp.concatenate([jnp.zeros((1,), I32), jnp.cumsum(nblk).astype(I32)])
        dest = _dest_rows(first_block[:ne] * EXPERT_ROWS, idx, rank, t_part, n_blocks * EXPERT_ROWS)
        blk = jnp.arange(n_blocks, dtype=I32)[:, None]
        inside = (blk >= first_block[None, :ne]) & (blk < first_block[None, 1:])
        rows_left = counts[None, :] - (blk - first_block[None, :ne]) * EXPERT_ROWS
        block_valid = jnp.sum(jnp.where(inside, jnp.minimum(rows_left, EXPERT_ROWS), 0), axis=1)
        mixed.append((xmid_p, h2, wts, dest, first_block, block_valid))
        after = dest
    for p in range(N_PARTS):
        xmid_p, h2, wts, dest, first_block, block_valid = mixed[p]

        def chunked(ch):
            return dest.reshape(TOP_K, t_pad // ch, ch).transpose(1, 0, 2)

        xs = _sc_dispatch(h2, chunked(SC_DISPATCH_CHUNK), n_rows)
        y = _experts(xs, w_bf[0], w_bf[1], first_block, block_valid)
        yg = _sc_combine_gather(y, chunked(SC_COMBINE_CHUNK), t_pad)

        y_samples.append(_final_sample(xmid_s, yg, wts, mod, modf, g_final[None, :], p, bsp, tpp,
                                       g_final if y_prompt is None else y_prompt))
        y_prompt = _final_prompt(xmid_p, yg, wts, mod, modf, bs, g_final[None, :], p * bpp, bp, y_prompt)
    y_sample = jnp.concatenate(y_samples, axis=0)
    npool_p = jnp.concatenate(npools, axis=0)
    nconv_p = jnp.concatenate(nconvs, axis=0)

    return (y_prompt, y_sample[:, None, :], npool_p[None], nconv_p[None], new_pool_s[None], new_conv_s[None])
```

```python
import functools
import math

import jax
import jax.numpy as jnp
from jax import lax
from jax.experimental import pallas as pl
from jax.experimental.pallas import tpu as pltpu
from jax.experimental.pallas import tpu_sc as plsc

POOL_WINDOWS = (2, 4, 8, 16)
N_EXPERT_GROUPS = 8
TOPK_GROUPS = 4
TOP_K = 8
ROUTED_SCALE = 2.5
EPS = 1e-6

LANES = 128
SUBLANES = 8
VMEM_LIMIT = 52 * 1024 * 1024

ADA_TILE = 1024
SEQ_TILE = 512
ROW_CHUNK = 16
ROUTE_TILE = 1408
EXPERT_ROWS = 1024
N_PARTS = 2
EXPERT_BUFFERS = 8
EXPERT_PIECES = 8
EXPERTS_PER_STEP = 2
FINAL_TILE = 512
DEST_TILE = 2816

SC_CORES = 2
SC_SUBCORES = 16
SC_WORKERS = SC_CORES * SC_SUBCORES
SC_DISPATCH_CHUNK = 88
SC_COMBINE_CHUNK = 24

F32 = jnp.float32
BF16 = jnp.bfloat16
I32 = jnp.int32
U32 = jnp.uint32
NEG_INF = float("-inf")
HI16 = 0xFFFF0000


def _sigmoid(x):
    return 1.0 / (1.0 + jnp.exp(-x))


def _silu(x):
    return x * _sigmoid(x)


def _rms_norm(x, g):
    return x * lax.rsqrt(jnp.mean(x * x, axis=-1, keepdims=True) + EPS) * g


def _dot(a, b):
    return jnp.dot(a, b, preferred_element_type=F32)


def _pack_bf16_pairs(x):
    w = x.shape[1] // 2
    bits = lax.bitcast_convert_type(x.astype(BF16).astype(F32), U32)
    return lax.bitcast_convert_type((bits[:, :w] >> 16) | (bits[:, w:] & U32(HI16)), I32)


def _unpack_bf16_pairs(p):
    bits = lax.bitcast_convert_type(p, U32)
    lo = lax.bitcast_convert_type(bits << 16, F32)
    hi = lax.bitcast_convert_type(bits & U32(HI16), F32)
    return jnp.concatenate([lo, hi], axis=1)


def _ada_kernel(c_ref, w_ref, b_ref, o_ref):
    s = _silu(c_ref[...])
    o_ref[...] = _dot(s.astype(BF16), w_ref[...].astype(BF16)) + b_ref[...]


def _ada(c, w, b):
    rows, d = c.shape
    n = w.shape[1]
    tn = ADA_TILE
    return pl.pallas_call(
        _ada_kernel,
        grid=(n // tn,),
        in_specs=[pl.BlockSpec((rows, d), lambda j: (0, 0)),
                  pl.BlockSpec((d, tn), lambda j: (0, j)),
                  pl.BlockSpec((1, tn), lambda j: (0, j))],
        out_specs=pl.BlockSpec((rows, tn), lambda j: (0, j)),
        out_shape=jax.ShapeDtypeStruct((rows, n), F32),
        compiler_params=pltpu.CompilerParams(dimension_semantics=("arbitrary",),
                                             vmem_limit_bytes=VMEM_LIMIT),
        name="ada",
    )(c, w, b.reshape(1, n))


def _mixer_tail(x, pool_d, conv_act, mod, w, xmid_ref, h2_ref, lgt_ref):
    sh2, sc2, g1, g2 = mod
    pw = pool_d.shape[1]
    pool_out = _dot(pool_d, w["pool"][...]) * w["pool_scale"][...]
    mix = _dot(pool_out.astype(BF16), w["out"][:pw, :]) + _dot(conv_act, w["out"][pw:, :])
    x1 = x + g1 * mix
    h2f = _rms_norm(x1, w["g_ffn"][...]) * (1.0 + sc2) + sh2
    h2_ref[...] = _pack_bf16_pairs(h2f)
    h2 = h2f.astype(BF16)
    lgt_ref[...] = lax.dot_general(w["router_t"][...], h2, (((1,), (1,)), ((), ())),
                                   preferred_element_type=F32)
    gu = _dot(h2, w["s_gu"][...])
    de = gu.shape[1] // 2
    hs = _silu(gu[:, :de]) * gu[:, de:]
    shared = _dot(hs.astype(BF16), w["s_down"][...])
    xmid_ref[...] = x1 + g2 * shared


_W_NAMES = ("g_mix", "in", "pool", "pool_scale", "dw", "b_dw", "ln_g", "ln_b", "out", "g_ffn",
            "router_t", "s_gu", "s_down")


def _layer_norm_silu(yc, g, b):
    mu = jnp.mean(yc, axis=-1, keepdims=True)
    yz = yc - mu
    var = jnp.mean(yz * yz, axis=-1, keepdims=True)
    return _silu(yz * lax.rsqrt(var + EPS) * g + b)


U_HALO = 32
A_HALO = 32


def _prompt_mixer_kernel(x_ref, mod_ref, *refs, tl, d, pw, cw, conv_k, mod_row0):
    nw = len(_W_NAMES)
    w = dict(zip(_W_NAMES, refs[:nw]))
    wg_ref, wu_ref, wd_ref = refs[nw + 2:nw + 5]
    xmid_ref, h2_ref, lgt_ref, npool_ref, nconv_ref, wgu_bf_ref, wd_bf_ref = refs[-16:-9]
    ubuf, s2buf, s4buf, s8buf, abuf, ashift, dbuf, cbuf, rowb = refs[-9:]
    de = wg_ref.shape[2]
    wgu_bf_ref[:, :, :de] = wg_ref[...].astype(BF16)
    wgu_bf_ref[:, :, de:] = wu_ref[...].astype(BF16)
    wd_bf_ref[...] = wd_ref[...].astype(BF16)
    l = pl.program_id(1)
    pool_buf = max(POOL_WINDOWS) - 1
    uh, ah = U_HALO, A_HALO
    pg = pw // len(POOL_WINDOWS)
    nu, na = uh + tl, ah + tl

    @pl.when(l == 0)
    def _():
        ubuf[0:uh, :] = jnp.zeros((uh, pw), F32)
        abuf[0:ah, :] = jnp.zeros((ah, cw), F32)

    @pl.when(l > 0)
    def _():
        ubuf[0:uh, :] = ubuf[tl:tl + uh, :]
        abuf[0:ah, :] = abuf[tl:tl + ah, :]

    x = x_ref[0]
    row = pl.ds((mod_row0 + pl.program_id(0)) % SUBLANES, 1)
    sh1, sc1, g1, sh2, sc2, g2 = [mod_ref[row, i * d:(i + 1) * d] for i in range(6)]

    h = _rms_norm(x, w["g_mix"][...]) * (1.0 + sc1) + sh1
    z = _dot(h.astype(BF16), w["in"][...])
    u = z[:, :pw]
    ubuf[uh:nu, :] = u
    abuf[ah:na, :] = z[:, pw:pw + cw] * _sigmoid(z[:, pw + cw:])

    s2buf[8:nu, :] = ubuf[8:nu, :] + ubuf[7:nu - 1, :]
    s4buf[16:nu, :] = s2buf[16:nu, pg:] + s2buf[14:nu - 2, pg:]
    s8buf[24:nu, :] = s4buf[24:nu, pg:] + s4buf[20:nu - 4, pg:]
    t = l * tl + lax.broadcasted_iota(I32, (tl, 1), 0)
    sums = (s2buf[uh:nu, 0:pg], s4buf[uh:nu, 0:pg], s8buf[uh:nu, 0:pg],
            s8buf[uh:nu, pg:2 * pg] + s8buf[uh - 8:nu - 8, pg:2 * pg])
    for g, win in enumerate(POOL_WINDOWS):
        cols = slice(g * pg, (g + 1) * pg)
        inv = 1.0 / jnp.minimum(win, t + 1).astype(F32)
        dbuf[:, cols] = (sums[g] * inv - u[:, cols]).astype(BF16)

    for r in range(1, SUBLANES):
        ashift[r - 1, 8:na, :] = abuf[8 - r:na - r, :]

    for k in range(conv_k):
        rowb[k] = jnp.broadcast_to(w["dw"][k:k + 1, :], (SUBLANES, cw))
    rowb[conv_k] = jnp.broadcast_to(w["b_dw"][...], (SUBLANES, cw))

    for c in range(tl // ROW_CHUNK):
        groups = []
        for o in range(ah + c * ROW_CHUNK, ah + (c + 1) * ROW_CHUNK, SUBLANES):
            yc = abuf[o:o + SUBLANES, :] * rowb[conv_k - 1]
            for back in range(1, conv_k):
                q, r = divmod(back, SUBLANES)
                src = abuf if r == 0 else ashift.at[r - 1]
                yc = yc + src[o - q * SUBLANES:o - (q - 1) * SUBLANES, :] * rowb[conv_k - 1 - back]
            groups.append(yc + rowb[conv_k])
        r0 = c * ROW_CHUNK
        cbuf[r0:r0 + ROW_CHUNK, :] = _layer_norm_silu(
            jnp.concatenate(groups, axis=0), w["ln_g"][...], w["ln_b"][...]).astype(BF16)

    @pl.when(l == pl.num_programs(1) - 1)
    def _():
        npool_ref[0] = ubuf[nu - pool_buf:nu, :]
        nconv_ref[0] = abuf[na - (conv_k - 1):na, :]

    _mixer_tail(x, dbuf[...], cbuf[...], (sh2, sc2, g1, g2), w, xmid_ref.at[0], h2_ref, lgt_ref)


def _sample_mixer_kernel(x_ref, mod_ref, sp_ref, sc_ref, *refs, d, pw, cw, conv_k):
    nw = len(_W_NAMES)
    w = dict(zip(_W_NAMES, refs[:nw]))
    xmid_ref, npool_ref, nconv_ref = refs[nw:nw + 3]
    tails = refs[nw + 3:-2]
    h2_tmp, lgt_tmp = refs[-2:]
    pool_buf = max(POOL_WINDOWS) - 1
    pg = pw // len(POOL_WINDOWS)
    x = x_ref[...]
    sh1, sc1, g1, sh2, sc2, g2 = [mod_ref[:, i * d:(i + 1) * d] for i in range(6)]
    h = _rms_norm(x, w["g_mix"][...]) * (1.0 + sc1) + sh1
    z = _dot(h.astype(BF16), w["in"][...])
    u = z[:, :pw]
    a = z[:, pw:pw + cw] * _sigmoid(z[:, pw + cw:])
    npool_ref[0:pool_buf - 1] = sp_ref[1:pool_buf]
    npool_ref[pool_buf - 1] = u
    nconv_ref[0:conv_k - 2] = sc_ref[1:conv_k - 1]
    nconv_ref[conv_k - 2] = a
    ds = []
    for g, win in enumerate(POOL_WINDOWS):
        cols = slice(g * pg, (g + 1) * pg)
        acc = u[:, cols]
        for j in range(1, win):
            acc = acc + sp_ref[pool_buf - j, :, cols]
        cnt = float(min(win, 1 + pool_buf))
        ds.append(acc / cnt - u[:, cols])
    pool_d = jnp.concatenate(ds, axis=-1).astype(BF16)
    yc = a * w["dw"][conv_k - 1:conv_k, :]
    for k in range(conv_k - 1):
        yc = yc + sc_ref[k] * w["dw"][k:k + 1, :]
    yc = yc + w["b_dw"][...]
    conv_act = _layer_norm_silu(yc, w["ln_g"][...], w["ln_b"][...]).astype(BF16)
    _mixer_tail(x, pool_d, conv_act, (sh2, sc2, g1, g2), w, xmid_ref, h2_tmp, lgt_tmp)
    n_parts = len(tails) // 2
    share = x.shape[0] // n_parts
    for p in range(n_parts):
        h2_ref, lgt_ref = tails[2 * p], tails[2 * p + 1]
        h2_ref[...] = jnp.zeros_like(h2_ref)
        lgt_ref[...] = jnp.zeros_like(lgt_ref)
        h2_ref[0:share, :] = h2_tmp[p * share:(p + 1) * share, :]
        lgt_ref[:, 0:share] = lgt_tmp[:, p * share:(p + 1) * share]


def _full_spec(a):
    nd = a.ndim
    return pl.BlockSpec(a.shape, lambda *_: (0,) * nd)


def _prompt_mixer(x, mod, mod_row0, wl, h2_all, lgt_all, b0, b, w_experts, w_bf_prev, after, *, conv_k):
    n_batch, seq, d = x.shape
    w_gate, w_up, w_down = w_experts
    n_exp, _, de = w_gate.shape
    pw = wl["pool"].shape[0]
    cw = wl["dw"].shape[1]
    ne = wl["router_t"].shape[0]
    tl = SEQ_TILE
    nl = seq // tl
    pg = pw // len(POOL_WINDOWS)
    assert POOL_WINDOWS == (2, 4, 8, 16) and conv_k - 1 <= A_HALO
    ws = [wl[n] for n in _W_NAMES]
    kern = functools.partial(_prompt_mixer_kernel, tl=tl, d=d, pw=pw, cw=cw, conv_k=conv_k, mod_row0=mod_row0 + b0)
    ecs = n_exp // (n_batch * nl)
    assert ecs * n_batch * nl == n_exp

    def w_map(i, j):
        return ((i + b0) * nl + j, 0, 0)

    ins = [x, mod, *ws, h2_all, lgt_all, w_gate, w_up, w_down]
    in_specs = ([pl.BlockSpec((1, tl, d), lambda i, j: (i + b0, j, 0)),
                 pl.BlockSpec((SUBLANES, mod.shape[-1]), lambda i, j: ((i + b0 + mod_row0) // SUBLANES, 0))]
                + [_full_spec(a) for a in ws]
                + [pl.BlockSpec(memory_space=pl.ANY), pl.BlockSpec(memory_space=pl.ANY),
                   pl.BlockSpec((ecs, d, de), w_map), pl.BlockSpec((ecs, d, de), w_map),
                   pl.BlockSpec((ecs, de, d), w_map)])
    aliases = {2 + len(ws): 1, 3 + len(ws): 2}
    if w_bf_prev is not None:
        aliases.update({len(ins): 5, len(ins) + 1: 6})
        ins += list(w_bf_prev)
        in_specs += [pl.BlockSpec(memory_space=pl.ANY), pl.BlockSpec(memory_space=pl.ANY)]
    if after is not None:
        ins.append(after)
        in_specs.append(pl.BlockSpec(memory_space=pl.ANY))
    return pl.pallas_call(
        kern,
        grid=(b, nl),
        in_specs=in_specs,
        out_specs=[pl.BlockSpec((1, tl, d), lambda i, j: (i, j, 0)),
                   pl.BlockSpec((tl, d // 2), lambda i, j: (i * nl + j, 0)),
                   pl.BlockSpec((ne, tl), lambda i, j: (0, i * nl + j)),
                   pl.BlockSpec((1, max(POOL_WINDOWS) - 1, pw), lambda i, j: (i, 0, 0)),
                   pl.BlockSpec((1, conv_k - 1, cw), lambda i, j: (i, 0, 0)),
                   pl.BlockSpec((ecs, d, 2 * de), w_map), pl.BlockSpec((ecs, de, d), w_map)],
        out_shape=[jax.ShapeDtypeStruct((b, seq, d), F32),
                   jax.ShapeDtypeStruct(h2_all.shape, I32),
                   jax.ShapeDtypeStruct(lgt_all.shape, F32),
                   jax.ShapeDtypeStruct((b, max(POOL_WINDOWS) - 1, pw), F32),
                   jax.ShapeDtypeStruct((b, conv_k - 1, cw), F32),
                   jax.ShapeDtypeStruct((n_exp, d, 2 * de), BF16),
                   jax.ShapeDtypeStruct((n_exp, de, d), BF16)],
        scratch_shapes=[pltpu.VMEM((U_HALO + tl, pw), F32), pltpu.VMEM((U_HALO + tl, pw), F32),
                        pltpu.VMEM((U_HALO + tl, pw - pg), F32), pltpu.VMEM((U_HALO + tl, pw - 2 * pg), F32),
                        pltpu.VMEM((A_HALO + tl, cw), F32), pltpu.VMEM((SUBLANES - 1, A_HALO + tl, cw), F32),
                        pltpu.VMEM((tl, pw), BF16), pltpu.VMEM((tl, cw), BF16),
                        pltpu.VMEM((conv_k + 1, SUBLANES, cw), F32)],
        compiler_params=pltpu.CompilerParams(dimension_semantics=("arbitrary", "arbitrary"),
                                             vmem_limit_bytes=VMEM_LIMIT),
        input_output_aliases=aliases,
        name="prompt_mixer",
    )(*ins)


def _sample_mixer(x, mod, sp, sc, wl, t_prompt, t_pad, *, conv_k):
    rows, d = x.shape
    tail = t_pad - t_prompt
    assert t_prompt % tail == 0 and rows % N_PARTS == 0 and rows // N_PARTS <= tail
    pw = wl["pool"].shape[0]
    cw = wl["dw"].shape[1]
    ne = wl["router_t"].shape[0]
    ws = [wl[n] for n in _W_NAMES]
    kern = functools.partial(_sample_mixer_kernel, d=d, pw=pw, cw=cw, conv_k=conv_k)
    ins = [x, mod, sp, sc] + ws
    tail_specs = [pl.BlockSpec((tail, d // 2), lambda i: (t_prompt // tail, 0)),
                  pl.BlockSpec((ne, tail), lambda i: (0, t_prompt // tail))] * N_PARTS
    tail_shapes = [jax.ShapeDtypeStruct((t_pad, d // 2), I32), jax.ShapeDtypeStruct((ne, t_pad), F32)] * N_PARTS
    outs = pl.pallas_call(
        kern,
        grid=(1,),
        in_specs=[_full_spec(x), pl.BlockSpec((rows, mod.shape[1]), lambda i: (0, 0))]
                 + [_full_spec(a) for a in ins[2:]],
        out_specs=[pl.BlockSpec((rows, d), lambda i: (0, 0)), _full_spec(sp), _full_spec(sc)] + tail_specs,
        out_shape=[jax.ShapeDtypeStruct((rows, d), F32),
                   jax.ShapeDtypeStruct(sp.shape, F32),
                   jax.ShapeDtypeStruct(sc.shape, F32)] + tail_shapes,
        scratch_shapes=[pltpu.VMEM((rows, d // 2), I32), pltpu.VMEM((ne, rows), F32)],
        compiler_params=pltpu.CompilerParams(dimension_semantics=("arbitrary",),
                                             vmem_limit_bytes=VMEM_LIMIT),
        name="sample_mixer",
    )(*ins)
    return outs[0], outs[1], outs[2], [(outs[3 + 2 * p], outs[4 + 2 * p]) for p in range(N_PARTS)]


def _sublane_max(x):
    return jnp.max(x, axis=0, keepdims=True)


def _route_kernel(lg_ref, b_ref, tri_ref, idx_ref, wt_ref, rank_ref, cnt_ref, carry, *, t_valid):
    i = pl.program_id(0)
    ne, tt = lg_ref.shape
    ng = N_EXPERT_GROUPS
    per = ne // ng

    @pl.when(i == 0)
    def _():
        carry[...] = jnp.zeros_like(carry)

    s = _sigmoid(lg_ref[...])
    sel = s + b_ref[...]
    s3 = [s[p * ng:(p + 1) * ng, :] for p in range(per)]
    sel3 = [sel[p * ng:(p + 1) * ng, :] for p in range(per)]
    m1 = sel3[0]
    m2 = jnp.full_like(m1, NEG_INF)
    for p in range(1, per):
        m2 = jnp.maximum(m2, jnp.minimum(m1, sel3[p]))
        m1 = jnp.maximum(m1, sel3[p])
    gs = m1 + m2
    gi = lax.broadcasted_iota(jnp.int32, (ng, tt), 0)
    beaten = jnp.zeros((ng, tt), jnp.int32)
    for g in range(ng):
        row = gs[g:g + 1, :]
        beats = (row > gs) | ((row == gs) & (gi > g))
        beaten = beaten + beats.astype(jnp.int32)
    keep = beaten < TOPK_GROUPS
    cur = [jnp.where(keep, sel3[p], NEG_INF) for p in range(per)]
    eid = [(gi * per + p).astype(F32) for p in range(per)]
    idxs, wts, hits = [], [], []
    for _ in range(TOP_K):
        m = cur[0]
        for p in range(1, per):
            m = jnp.maximum(m, cur[p])
        m = _sublane_max(m)
        cand = jnp.where(cur[0] == m, eid[0], float(ne))
        for p in range(1, per):
            cand = jnp.minimum(cand, jnp.where(cur[p] == m, eid[p], float(ne)))
        e_sel = jnp.min(cand, axis=0, keepdims=True)
        hit = [eid[p] == e_sel for p in range(per)]
        wk = jnp.where(hit[0], s3[0], 0.0)
        for p in range(1, per):
            wk = wk + jnp.where(hit[p], s3[p], 0.0)
        wts.append(jnp.sum(wk, axis=0, keepdims=True))
        cur = [jnp.where(hit[p], NEG_INF, cur[p]) for p in range(per)]
        idxs.append(e_sel)
        hits.append(hit)
    wsum = wts[0]
    for k in range(1, TOP_K):
        wsum = wsum + wts[k]
    chosen = [functools.reduce(lambda a, b: a | b, [hits[k][p] for k in range(TOP_K)]) for p in range(per)]
    real = (i * tt + lax.broadcasted_iota(I32, (1, tt), 1)) < t_valid
    onehot = jnp.concatenate([(c & real).astype(F32) for c in chosen], axis=0)
    before = _dot(onehot.astype(BF16), tri_ref[...]) + carry[...]
    for k in range(TOP_K):
        rk = jnp.where(hits[k][0], before[0:ng, :], 0.0)
        for p in range(1, per):
            rk = rk + jnp.where(hits[k][p], before[p * ng:(p + 1) * ng, :], 0.0)
        rank_ref[k:k + 1, :] = jnp.sum(rk, axis=0, keepdims=True).astype(jnp.int32)
        idx_ref[k:k + 1, :] = idxs[k].astype(jnp.int32)
        wt_ref[k:k + 1, :] = wts[k] / wsum * ROUTED_SCALE
    carry[...] = carry[...] + jnp.sum(onehot, axis=1, keepdims=True)
    cnt_ref[...] = carry[...]


def _route(lgt, b_perm, t_valid):
    ne, t = lgt.shape
    tt = ROUTE_TILE
    tri = (jnp.arange(tt)[:, None] < jnp.arange(tt)[None, :]).astype(BF16)
    return pl.pallas_call(
        functools.partial(_route_kernel, t_valid=t_valid),
        grid=(t // tt,),
        in_specs=[pl.BlockSpec((ne, tt), lambda i: (0, i)),
                  pl.BlockSpec((ne, 1), lambda i: (0, 0)),
                  pl.BlockSpec((tt, tt), lambda i: (0, 0))],
        out_specs=[pl.BlockSpec((TOP_K, tt), lambda i: (0, i)),
                   pl.BlockSpec((TOP_K, tt), lambda i: (0, i)),
                   pl.BlockSpec((TOP_K, tt), lambda i: (0, i)),
                   pl.BlockSpec((ne, 1), lambda i: (0, 0))],
        out_shape=[jax.ShapeDtypeStruct((TOP_K, t), jnp.int32),
                   jax.ShapeDtypeStruct((TOP_K, t), F32),
                   jax.ShapeDtypeStruct((TOP_K, t), jnp.int32),
                   jax.ShapeDtypeStruct((ne, 1), F32)],
        scratch_shapes=[pltpu.VMEM((ne, 1), F32)],
        compiler_params=pltpu.CompilerParams(dimension_semantics=("arbitrary",),
                                             vmem_limit_bytes=VMEM_LIMIT),
        name="route",
    )(lgt, b_perm, tri)


def _dest_kernel(pstart_ref, idx_ref, rank_ref, dest_ref, *, t_valid, spare_row):
    idx = idx_ref[...]
    k_top, tile = idx.shape

    def add_start(e, acc):
        return acc + jnp.where(idx == e, pstart_ref[e], 0)

    dest = lax.fori_loop(0, pstart_ref.shape[0], add_start, rank_ref[...])
    tok = pl.program_id(0) * tile + lax.broadcasted_iota(I32, idx.shape, 1)
    spare = spare_row + (tok - t_valid) * k_top + lax.broadcasted_iota(I32, idx.shape, 0)
    dest_ref[...] = jnp.where(tok < t_valid, dest, spare)


def _dest_rows(pstart, idx, rank, t_valid, spare_row):
    k_top, t = idx.shape
    tile = DEST_TILE
    spec = pl.BlockSpec((k_top, tile), lambda i, ps: (0, i))
    return pl.pallas_call(
        functools.partial(_dest_kernel, t_valid=t_valid, spare_row=spare_row),
        grid_spec=pltpu.PrefetchScalarGridSpec(num_scalar_prefetch=1, grid=(t // tile,),
                                               in_specs=[spec, spec], out_specs=spec),
        out_shape=jax.ShapeDtypeStruct((k_top, t), I32),
        compiler_params=pltpu.CompilerParams(dimension_semantics=("arbitrary",),
                                             vmem_limit_bytes=VMEM_LIMIT),
        name="dest_rows",
    )(pstart, idx, rank)


def _expert_kernel(first_ref, valid_ref, xs_hbm, wgu_ref, wd_ref, y_hbm, xbuf, ybuf, in_sem, out_sem):
    step = pl.program_id(0)
    ne = first_ref.shape[0] - 1
    nbuf, rows = xbuf.shape[:2]
    piece = rows // EXPERT_PIECES
    per_step, de = wd_ref.shape[:2]
    n_total = first_ref[ne]

    def in_copy(g, h):
        return pltpu.make_async_copy(xs_hbm.at[pl.ds(g * rows + h * piece, piece)],
                                     xbuf.at[g % nbuf, pl.ds(h * piece, piece)], in_sem.at[g % nbuf])

    def out_copy(g, h):
        return pltpu.make_async_copy(ybuf.at[g % nbuf, pl.ds(h * piece, piece)],
                                     y_hbm.at[pl.ds(g * rows + h * piece, piece)], out_sem.at[g % nbuf])

    def used_pieces(copy, g, act):
        getattr(copy(g, 0), act)()
        for h in range(1, EXPERT_PIECES):
            @pl.when(valid_ref[g] > h * piece)
            def _():
                getattr(copy(g, h), act)()

    @pl.when(step == 0)
    def _():
        for g in range(nbuf - 1):
            @pl.when(g < n_total)
            def _():
                used_pieces(in_copy, g, "start")

    def block(g, carry, *, j):
        slot = g % nbuf

        @pl.when(g + nbuf - 1 < n_total)
        def _():
            used_pieces(in_copy, g + nbuf - 1, "start")

        used_pieces(in_copy, g, "wait")

        @pl.when(g >= nbuf)
        def _():
            used_pieces(out_copy, g - nbuf, "wait")

        valid = valid_ref[g]

        def run(n):
            rid = lax.broadcasted_iota(I32, (n, 1), 0)
            xs = jnp.where(rid < valid, _unpack_bf16_pairs(xbuf[slot, 0:n]), 0.0).astype(BF16)
            gu = _dot(xs, wgu_ref[j])
            hb = _silu(gu[:, :de]) * gu[:, de:]
            ybuf[slot, 0:n] = _pack_bf16_pairs(_dot(hb.astype(BF16), wd_ref[j]))

        for h in range(EXPERT_PIECES):
            @pl.when((valid > h * piece) & (valid <= (h + 1) * piece))
            def _():
                run((h + 1) * piece)

        used_pieces(out_copy, g, "start")
        return carry

    for j in range(per_step):
        e = step * per_step + j
        lax.fori_loop(first_ref[e], first_ref[e + 1], functools.partial(block, j=j), 0)

    @pl.when(step == pl.num_programs(0) - 1)
    def _():
        for back in range(nbuf, 0, -1):
            @pl.when(n_total >= back)
            def _():
                used_pieces(out_copy, n_total - back, "wait")


def _experts(xs, w_gu, w_down, first_block, block_valid):
    p, dw = xs.shape
    ne, de, d = w_down.shape
    return pl.pallas_call(
        _expert_kernel,
        grid_spec=pltpu.PrefetchScalarGridSpec(
            num_scalar_prefetch=2,
            grid=(ne // EXPERTS_PER_STEP,),
            in_specs=[pl.BlockSpec(memory_space=pl.ANY),
                      pl.BlockSpec((EXPERTS_PER_STEP, d, 2 * de), lambda e, fb, cn: (e, 0, 0)),
                      pl.BlockSpec((EXPERTS_PER_STEP, de, d), lambda e, fb, cn: (e, 0, 0))],
            out_specs=pl.BlockSpec(memory_space=pl.ANY),
            scratch_shapes=[pltpu.VMEM((EXPERT_BUFFERS, EXPERT_ROWS, dw), I32),
                            pltpu.VMEM((EXPERT_BUFFERS, EXPERT_ROWS, dw), I32),
                            pltpu.SemaphoreType.DMA((EXPERT_BUFFERS,)), pltpu.SemaphoreType.DMA((EXPERT_BUFFERS,))]),
        out_shape=jax.ShapeDtypeStruct((p, dw), I32),
        compiler_params=pltpu.CompilerParams(dimension_semantics=("arbitrary",),
                                             vmem_limit_bytes=VMEM_LIMIT),
        name="experts",
    )(first_block, block_valid, xs, w_gu, w_down)


def _sc_first_chunk(n_chunks):
    per_worker = n_chunks // SC_WORKERS
    assert per_worker * SC_WORKERS == n_chunks
    return per_worker, (lax.axis_index("s") * SC_CORES + lax.axis_index("c")) * per_worker


def _sc_start(copies):
    for cp in copies:
        cp.start()


def _sc_wait(copies):
    for cp in copies:
        cp.wait()


def _sc_dispatch(h2, dest, n_rows):
    w = h2.shape[1]
    n_chunks, k_top, ch = dest.shape
    mesh = plsc.VectorSubcoreMesh(core_axis_name="c", subcore_axis_name="s")

    @functools.partial(
        pl.kernel, mesh=mesh, out_type=jax.ShapeDtypeStruct((n_rows, w), I32),
        scratch_types=[pltpu.VMEM((2, k_top, ch), I32), pltpu.VMEM((2, ch, w), I32),
                       pltpu.SemaphoreType.DMA((2,)), pltpu.SemaphoreType.DMA((2,))])
    def dispatch(h2_hbm, dest_hbm, xs_hbm, idx_v, rows_v, load_sem, store_sem):
        per_worker, first = _sc_first_chunk(n_chunks)

        def loads(i):
            slot = i % 2
            return (pltpu.make_async_copy(h2_hbm.at[pl.ds((first + i) * ch, ch)], rows_v.at[slot], load_sem.at[slot]),
                    pltpu.make_async_copy(dest_hbm.at[first + i], idx_v.at[slot], load_sem.at[slot]))

        def scatters(i):
            slot = i % 2
            return [pltpu.make_async_copy(rows_v.at[slot], xs_hbm.at[idx_v.at[slot, k]], store_sem.at[slot])
                    for k in range(k_top)]

        for i in range(min(2, per_worker)):
            _sc_start(loads(i))
        for i in range(per_worker):
            _sc_wait(loads(i))
            _sc_start(scatters(i))
            if 1 <= i < per_worker - 1:
                _sc_wait(scatters(i - 1))
                _sc_start(loads(i + 1))
        for i in range(max(per_worker - 2, 0), per_worker):
            _sc_wait(scatters(i))

    return dispatch(h2, dest)


def _sc_combine_gather(y, dest, t):
    w = y.shape[1]
    n_chunks, k_top, ch = dest.shape
    n_local = n_chunks // SC_WORKERS
    mesh = plsc.VectorSubcoreMesh(core_axis_name="c", subcore_axis_name="s")

    @functools.partial(
        pl.kernel, mesh=mesh, out_type=jax.ShapeDtypeStruct((k_top, t, w), I32),
        scratch_types=[pltpu.VMEM((n_local, k_top, ch), I32), pltpu.VMEM((k_top, ch, w), I32),
                       pltpu.SemaphoreType.DMA((k_top,)), pltpu.SemaphoreType.DMA((k_top,))])
    def gather(y_hbm, dest_hbm, out_hbm, idx_v, rows_v, load_sem, store_sem):
        per_worker, first = _sc_first_chunk(n_chunks)
        pltpu.sync_copy(dest_hbm.at[pl.ds(first, per_worker)], idx_v)

        def fetch(i, k):
            return pltpu.make_async_copy(y_hbm.at[idx_v.at[i, k]], rows_v.at[k], load_sem.at[k])

        def put(i, k):
            return pltpu.make_async_copy(rows_v.at[k], out_hbm.at[k, pl.ds((first + i) * ch, ch)], store_sem.at[k])

        @pl.loop(0, per_worker)
        def _(i):
            for k in range(k_top):
                @pl.when(i > 0)
                def _():
                    put(i - 1, k).wait()

                fetch(i, k).start()
            for k in range(k_top):
                fetch(i, k).wait()
                put(i, k).start()

        for k in range(k_top):
            put(per_worker - 1, k).wait()

    return gather(y, dest)


def _final_kernel(xmid_ref, yg_ref, wt_ref, g2_ref, shf_ref, scf_ref, gf_ref, *rest, vec_row0=None):
    o_ref = rest[-1]
    if vec_row0 is not None:
        row = pl.ds((vec_row0 + pl.program_id(0)) % SUBLANES, 1)
        g2_ref, shf_ref, scf_ref = g2_ref.at[row, :], shf_ref.at[row, :], scf_ref.at[row, :]
    wt = wt_ref[...].T[:xmid_ref.shape[0]]
    routed = wt[:, 0:1] * _unpack_bf16_pairs(yg_ref[0])
    for k in range(1, yg_ref.shape[0]):
        routed = routed + wt[:, k:k + 1] * _unpack_bf16_pairs(yg_ref[k])
    x2 = xmid_ref[...] + g2_ref[...] * routed
    o_ref[...] = _rms_norm(x2, gf_ref[...]) * (1.0 + scf_ref[...]) + shf_ref[...]


def _final_prompt(xmid, yg, wts_t, mod, modf, mod_row0, g_final, b0, n_batch, out_prev):
    b, seq, d = xmid.shape
    k_top, _, w = yg.shape
    tl = FINAL_TILE
    nl = seq // tl

    def vec(col):
        return pl.BlockSpec((SUBLANES, d), lambda i, j: ((i + b0 + mod_row0) // SUBLANES, col))

    in_specs = [pl.BlockSpec((None, tl, d), lambda i, j: (i, j, 0)),
                pl.BlockSpec((k_top, tl, w), lambda i, j: (0, i * nl + j, 0)),
                pl.BlockSpec((k_top, tl), lambda i, j: (0, i * nl + j)),
                vec(5), vec(0), vec(1), pl.BlockSpec((1, d), lambda i, j: (0, 0))]
    args = [xmid, yg, wts_t, mod, modf, modf, g_final]
    aliases = {}
    if out_prev is not None:
        in_specs.append(pl.BlockSpec(memory_space=pl.ANY))
        args.append(out_prev)
        aliases = {7: 0}
    return pl.pallas_call(
        functools.partial(_final_kernel, vec_row0=b0 + mod_row0),
        grid=(b, nl),
        in_specs=in_specs,
        out_specs=pl.BlockSpec((None, tl, d), lambda i, j: (i + b0, j, 0)),
        out_shape=jax.ShapeDtypeStruct((n_batch, seq, d), F32),
        compiler_params=pltpu.CompilerParams(dimension_semantics=("arbitrary", "arbitrary"),
                                             vmem_limit_bytes=VMEM_LIMIT),
        input_output_aliases=aliases,
        name="final_prompt",
    )(*args)


def _final_sample(xmid_all, yg, wts_t, mod, modf, g_final, part, rows, first_row, after):
    d = xmid_all.shape[1]
    k_top, _, w = yg.shape

    def vec(col):
        return pl.BlockSpec((rows, d), lambda i: (part, col))

    return pl.pallas_call(
        _final_kernel,
        grid=(1,),
        in_specs=[vec(0),
                  pl.BlockSpec((k_top, rows, w), lambda i: (0, first_row // rows, 0)),
                  pl.BlockSpec((k_top, LANES), lambda i: (0, first_row // LANES)),
                  vec(5), vec(0), vec(1), pl.BlockSpec((1, d), lambda i: (0, 0)), pl.BlockSpec(memory_space=pl.ANY)],
        out_specs=pl.BlockSpec((rows, d), lambda i: (0, 0)),
        out_shape=jax.ShapeDtypeStruct((rows, d), F32),
        compiler_params=pltpu.CompilerParams(dimension_semantics=("arbitrary",),
                                             vmem_limit_bytes=VMEM_LIMIT),
        name="final_sample",
    )(xmid_all, yg, wts_t, mod, modf, modf, g_final, after)


def _block_diag_pairs(w_pool):
    g, c, _ = w_pool.shape
    eye = jnp.eye(g, dtype=w_pool.dtype)
    return (eye[:, None, :, None] * w_pool[:, :, None, :]).reshape(g * c, g * c)


def kernel(x_prompt, x_sample, state_pool, state_conv, c_prompt, c_sample, w_ada, b_ada, g_mix, w_in, w_pool, pool_scale, w_dw, b_dw, ln_g, ln_b, w_out, g_ffn, w_router, b_router, w_gate, w_up, w_down, ws_gate, ws_up, ws_down, w_ada_final, b_ada_final, g_final):
    bp, seq, d = x_prompt.shape
    bs = x_sample.shape[0]
    depth = w_ada.shape[0]
    assert depth == 1 and x_sample.shape[1] == 1
    conv_k = w_dw.shape[1]
    ne = w_router.shape[-1]
    per = ne // N_EXPERT_GROUPS

    row_expert = (jnp.arange(ne) % N_EXPERT_GROUPS) * per + jnp.arange(ne) // N_EXPERT_GROUPS
    wl = {
        "g_mix": g_mix[0][None, :],
        "in": w_in[0].astype(BF16),
        "pool": _block_diag_pairs(w_pool[0]).astype(BF16),
        "pool_scale": pool_scale[0][None, :],
        "dw": w_dw[0],
        "b_dw": b_dw[0][None, :],
        "ln_g": ln_g[0][None, :],
        "ln_b": ln_b[0][None, :],
        "out": w_out[0].astype(BF16),
        "g_ffn": g_ffn[0][None, :],
        "router_t": w_router[0].T[row_expert].astype(BF16),
        "s_gu": jnp.concatenate([ws_gate[0], ws_up[0]], axis=1).astype(BF16),
        "s_down": ws_down[0].astype(BF16),
    }
    b_perm = b_router[0][row_expert][:, None]

    assert bp % N_PARTS == 0 and bs % N_PARTS == 0
    bpp, bsp = bp // N_PARTS, bs // N_PARTS
    tpp = bpp * seq
    t_part = tpp + bsp
    grain = SC_WORKERS * SC_DISPATCH_CHUNK * SC_COMBINE_CHUNK // math.gcd(SC_DISPATCH_CHUNK, SC_COMBINE_CHUNK)
    t_pad = (t_part + grain - 1) // grain * grain
    assert t_pad % ROUTE_TILE == 0 and t_pad % DEST_TILE == 0 and tpp % bsp == 0

    c_all = jnp.concatenate([c_sample, c_prompt], axis=0)
    mod = _ada(c_all, w_ada[0], b_ada[0])
    modf = _ada(c_all, w_ada_final, b_ada_final)

    xmid_s, new_pool_t, new_conv_t, tails = _sample_mixer(
        x_sample.reshape(bs, d), mod, jnp.transpose(state_pool[0], (1, 0, 2)),
        jnp.transpose(state_conv[0], (1, 0, 2)), wl, tpp, t_pad, conv_k=conv_k)
    new_pool_s = jnp.transpose(new_pool_t, (1, 0, 2))
    new_conv_s = jnp.transpose(new_conv_t, (1, 0, 2))
    w_experts = tuple(a.reshape(a.shape[1:]) for a in (w_gate, w_up, w_down))

    n_blocks = (t_part * TOP_K + EXPERT_ROWS - 1) // EXPERT_ROWS + ne
    n_spare = -(-(t_pad - t_part) * TOP_K // EXPERT_ROWS)
    n_rows = (n_blocks + n_spare) * EXPERT_ROWS

    y_prompt, y_samples, npools, nconvs, w_bf = None, [], [], [], None
    mixed, after = [], modf
    for p in range(N_PARTS):
        h2, lgt = tails[p]
        xmid_p, h2, lgt, npool_p, nconv_p, *w_bf = _prompt_mixer(
            x_prompt, mod, bs, wl, h2, lgt, p * bpp, bpp, w_experts, w_bf, after, conv_k=conv_k)
        npools.append(npool_p)
        nconvs.append(nconv_p)
        idx, wts, rank, counts_perm = _route(lgt, b_perm, t_part)
        counts = counts_perm.astype(I32).reshape(per, N_EXPERT_GROUPS).T.reshape(ne)
        nblk = (counts + EXPERT_ROWS - 1) // EXPERT_ROWS
        first_block = jnp.concatenate([jnp.zeros((1,), I32), jnp.cumsum(nblk).astype(I32)])
        dest = _dest_rows(first_block[:ne] * EXPERT_ROWS, idx, rank, t_part, n_blocks * EXPERT_ROWS)
        blk = jnp.arange(n_blocks, dtype=I32)[:, None]
        inside = (blk >= first_block[None, :ne]) & (blk < first_block[None, 1:])
        rows_left = counts[None, :] - (blk - first_block[None, :ne]) * EXPERT_ROWS
        block_valid = jnp.sum(jnp.where(inside, jnp.minimum(rows_left, EXPERT_ROWS), 0), axis=1)
        mixed.append((xmid_p, h2, wts, dest, first_block, block_valid))
        after = dest
    for p in range(N_PARTS):
        xmid_p, h2, wts, dest, first_block, block_valid = mixed[p]

        def chunked(ch):
            return dest.reshape(TOP_K, t_pad // ch, ch).transpose(1, 0, 2)

        xs = _sc_dispatch(h2, chunked(SC_DISPATCH_CHUNK), n_rows)
        y = _experts(xs, w_bf[0], w_bf[1], first_block, block_valid)
        yg = _sc_combine_gather(y, chunked(SC_COMBINE_CHUNK), t_pad)

        y_samples.append(_final_sample(xmid_s, yg, wts, mod, modf, g_final[None, :], p, bsp, tpp,
                                       g_final if y_prompt is None else y_prompt))
        y_prompt = _final_prompt(xmid_p, yg, wts, mod, modf, bs, g_final[None, :], p * bpp, bp, y_prompt)
    y_sample = jnp.concatenate(y_samples, axis=0)
    npool_p = jnp.concatenate(npools, axis=0)
    nconv_p = jnp.concatenate(nconvs, axis=0)

    return (y_prompt, y_sample[:, None, :], npool_p[None], nconv_p[None], new_pool_s[None], new_conv_s[None])
```

```python
import functools
import math

import jax
import jax.numpy as jnp
from jax import lax
from jax.experimental import pallas as pl
from jax.experimental.pallas import tpu as pltpu
from jax.experimental.pallas import tpu_sc as plsc

POOL_WINDOWS = (2, 4, 8, 16)
N_EXPERT_GROUPS = 8
TOPK_GROUPS = 4
TOP_K = 8
ROUTED_SCALE = 2.5
EPS = 1e-6

LANES = 128
SUBLANES = 8
VMEM_LIMIT = 52 * 1024 * 1024

ADA_TILE = 1024
SEQ_TILE = 512
ROW_CHUNK = 16
ROUTE_TILE = 1408
EXPERT_ROWS = 1024
N_PARTS = 2
EXPERT_BUFFERS = 8
EXPERT_PIECES = 8
EXPERTS_PER_STEP = 2
FINAL_TILE = 512
DEST_TILE = 2816

SC_CORES = 2
SC_SUBCORES = 16
SC_WORKERS = SC_CORES * SC_SUBCORES
SC_DISPATCH_CHUNK = 88
SC_COMBINE_CHUNK = 24

F32 = jnp.float32
BF16 = jnp.bfloat16
I32 = jnp.int32
U32 = jnp.uint32
NEG_INF = float("-inf")
HI16 = 0xFFFF0000


def _sigmoid(x):
    return 1.0 / (1.0 + jnp.exp(-x))


def _silu(x):
    return x * _sigmoid(x)


def _rms_norm(x, g):
    return x * lax.rsqrt(jnp.mean(x * x, axis=-1, keepdims=True) + EPS) * g


def _dot(a, b):
    return jnp.dot(a, b, preferred_element_type=F32)


def _pack_bf16_pairs(x):
    w = x.shape[1] // 2
    bits = lax.bitcast_convert_type(x.astype(BF16).astype(F32), U32)
    return lax.bitcast_convert_type((bits[:, :w] >> 16) | (bits[:, w:] & U32(HI16)), I32)


def _unpack_bf16_pairs(p):
    bits = lax.bitcast_convert_type(p, U32)
    lo = lax.bitcast_convert_type(bits << 16, F32)
    hi = lax.bitcast_convert_type(bits & U32(HI16), F32)
    return jnp.concatenate([lo, hi], axis=1)


def _ada_kernel(c_ref, w_ref, b_ref, o_ref):
    s = _silu(c_ref[...])
    o_ref[...] = _dot(s.astype(BF16), w_ref[...].astype(BF16)) + b_ref[...]


def _ada(c, w, b):
    rows, d = c.shape
    n = w.shape[1]
    tn = ADA_TILE
    return pl.pallas_call(
        _ada_kernel,
        grid=(n // tn,),
        in_specs=[pl.BlockSpec((rows, d), lambda j: (0, 0)),
                  pl.BlockSpec((d, tn), lambda j: (0, j)),
                  pl.BlockSpec((1, tn), lambda j: (0, j))],
        out_specs=pl.BlockSpec((rows, tn), lambda j: (0, j)),
        out_shape=jax.ShapeDtypeStruct((rows, n), F32),
        compiler_params=pltpu.CompilerParams(dimension_semantics=("arbitrary",),
                                             vmem_limit_bytes=VMEM_LIMIT),
        name="ada",
    )(c, w, b.reshape(1, n))


def _mixer_tail(x, pool_d, conv_act, mod, w, xmid_ref, h2_ref, lgt_ref):
    sh2, sc2, g1, g2 = mod
    pw = pool_d.shape[1]
    pool_out = _dot(pool_d, w["pool"][...]) * w["pool_scale"][...]
    mix = _dot(pool_out.astype(BF16), w["out"][:pw, :]) + _dot(conv_act, w["out"][pw:, :])
    x1 = x + g1 * mix
    h2f = _rms_norm(x1, w["g_ffn"][...]) * (1.0 + sc2) + sh2
    h2_ref[...] = _pack_bf16_pairs(h2f)
    h2 = h2f.astype(BF16)
    lgt_ref[...] = lax.dot_general(w["router_t"][...], h2, (((1,), (1,)), ((), ())),
                                   preferred_element_type=F32)
    gu = _dot(h2, w["s_gu"][...])
    de = gu.shape[1] // 2
    hs = _silu(gu[:, :de]) * gu[:, de:]
    shared = _dot(hs.astype(BF16), w["s_down"][...])
    xmid_ref[...] = x1 + g2 * shared


_W_NAMES = ("g_mix", "in", "pool", "pool_scale", "dw", "b_dw", "ln_g", "ln_b", "out", "g_ffn",
            "router_t", "s_gu", "s_down")


def _layer_norm_silu(yc, g, b):
    mu = jnp.mean(yc, axis=-1, keepdims=True)
    yz = yc - mu
    var = jnp.mean(yz * yz, axis=-1, keepdims=True)
    return _silu(yz * lax.rsqrt(var + EPS) * g + b)


U_HALO = 32
A_HALO = 32


def _prompt_mixer_kernel(x_ref, mod_ref, *refs, tl, d, pw, cw, conv_k, mod_row0):
    nw = len(_W_NAMES)
    w = dict(zip(_W_NAMES, refs[:nw]))
    wg_ref, wu_ref, wd_ref = refs[nw + 2:nw + 5]
    xmid_ref, h2_ref, lgt_ref, npool_ref, nconv_ref, wgu_bf_ref, wd_bf_ref = refs[-16:-9]
    ubuf, s2buf, s4buf, s8buf, abuf, ashift, dbuf, cbuf, rowb = refs[-9:]
    de = wg_ref.shape[2]
    wgu_bf_ref[:, :, :de] = wg_ref[...].astype(BF16)
    wgu_bf_ref[:, :, de:] = wu_ref[...].astype(BF16)
    wd_bf_ref[...] = wd_ref[...].astype(BF16)
    l = pl.program_id(1)
    pool_buf = max(POOL_WINDOWS) - 1
    uh, ah = U_HALO, A_HALO
    pg = pw // len(POOL_WINDOWS)
    nu, na = uh + tl, ah + tl

    @pl.when(l == 0)
    def _():
        ubuf[0:uh, :] = jnp.zeros((uh, pw), F32)
        abuf[0:ah, :] = jnp.zeros((ah, cw), F32)

    @pl.when(l > 0)
    def _():
        ubuf[0:uh, :] = ubuf[tl:tl + uh, :]
        abuf[0:ah, :] = abuf[tl:tl + ah, :]

    x = x_ref[0]
    row = pl.ds((mod_row0 + pl.program_id(0)) % SUBLANES, 1)
    sh1, sc1, g1, sh2, sc2, g2 = [mod_ref[row, i * d:(i + 1) * d] for i in range(6)]

    h = _rms_norm(x, w["g_mix"][...]) * (1.0 + sc1) + sh1
    z = _dot(h.astype(BF16), w["in"][...])
    u = z[:, :pw]
    ubuf[uh:nu, :] = u
    abuf[ah:na, :] = z[:, pw:pw + cw] * _sigmoid(z[:, pw + cw:])

    s2buf[8:nu, :] = ubuf[8:nu, :] + ubuf[7:nu - 1, :]
    s4buf[16:nu, :] = s2buf[16:nu, pg:] + s2buf[14:nu - 2, pg:]
    s8buf[24:nu, :] = s4buf[24:nu, pg:] + s4buf[20:nu - 4, pg:]
    t = l * tl + lax.broadcasted_iota(I32, (tl, 1), 0)
    sums = (s2buf[uh:nu, 0:pg], s4buf[uh:nu, 0:pg], s8buf[uh:nu, 0:pg],
            s8buf[uh:nu, pg:2 * pg] + s8buf[uh - 8:nu - 8, pg:2 * pg])
    for g, win in enumerate(POOL_WINDOWS):
        cols = slice(g * pg, (g + 1) * pg)
        inv = 1.0 / jnp.minimum(win, t + 1).astype(F32)
        dbuf[:, cols] = (sums[g] * inv - u[:, cols]).astype(BF16)

    for r in range(1, SUBLANES):
        ashift[r - 1, 8:na, :] = abuf[8 - r:na - r, :]

    for k in range(conv_k):
        rowb[k] = jnp.broadcast_to(w["dw"][k:k + 1, :], (SUBLANES, cw))
    rowb[conv_k] = jnp.broadcast_to(w["b_dw"][...], (SUBLANES, cw))

    for c in range(tl // ROW_CHUNK):
        groups = []
        for o in range(ah + c * ROW_CHUNK, ah + (c + 1) * ROW_CHUNK, SUBLANES):
            yc = abuf[o:o + SUBLANES, :] * rowb[conv_k - 1]
            for back in range(1, conv_k):
                q, r = divmod(back, SUBLANES)
                src = abuf if r == 0 else ashift.at[r - 1]
                yc = yc + src[o - q * SUBLANES:o - (q - 1) * SUBLANES, :] * rowb[conv_k - 1 - back]
            groups.append(yc + rowb[conv_k])
        r0 = c * ROW_CHUNK
        cbuf[r0:r0 + ROW_CHUNK, :] = _layer_norm_silu(
            jnp.concatenate(groups, axis=0), w["ln_g"][...], w["ln_b"][...]).astype(BF16)

    @pl.when(l == pl.num_programs(1) - 1)
    def _():
        npool_ref[0] = ubuf[nu - pool_buf:nu, :]
        nconv_ref[0] = abuf[na - (conv_k - 1):na, :]

    _mixer_tail(x, dbuf[...], cbuf[...], (sh2, sc2, g1, g2), w, xmid_ref.at[0], h2_ref, lgt_ref)


def _sample_mixer_kernel(x_ref, mod_ref, sp_ref, sc_ref, *refs, d, pw, cw, conv_k):
    nw = len(_W_NAMES)
    w = dict(zip(_W_NAMES, refs[:nw]))
    xmid_ref, npool_ref, nconv_ref = refs[nw:nw + 3]
    tails = refs[nw + 3:-2]
    h2_tmp, lgt_tmp = refs[-2:]
    pool_buf = max(POOL_WINDOWS) - 1
    pg = pw // len(POOL_WINDOWS)
    x = x_ref[...]
    sh1, sc1, g1, sh2, sc2, g2 = [mod_ref[:, i * d:(i + 1) * d] for i in range(6)]
    h = _rms_norm(x, w["g_mix"][...]) * (1.0 + sc1) + sh1
    z = _dot(h.astype(BF16), w["in"][...])
    u = z[:, :pw]
    a = z[:, pw:pw + cw] * _sigmoid(z[:, pw + cw:])
    npool_ref[0:pool_buf - 1] = sp_ref[1:pool_buf]
    npool_ref[pool_buf - 1] = u
    nconv_ref[0:conv_k - 2] = sc_ref[1:conv_k - 1]
    nconv_ref[conv_k - 2] = a
    ds = []
    for g, win in enumerate(POOL_WINDOWS):
        cols = slice(g * pg, (g + 1) * pg)
        acc = u[:, cols]
        for j in range(1, win):
            acc = acc + sp_ref[pool_buf - j, :, cols]
        cnt = float(min(win, 1 + pool_buf))
        ds.append(acc / cnt - u[:, cols])
    pool_d = jnp.concatenate(ds, axis=-1).astype(BF16)
    yc = a * w["dw"][conv_k - 1:conv_k, :]
    for k in range(conv_k - 1):
        yc = yc + sc_ref[k] * w["dw"][k:k + 1, :]
    yc = yc + w["b_dw"][...]
    conv_act = _layer_norm_silu(yc, w["ln_g"][...], w["ln_b"][...]).astype(BF16)
    _mixer_tail(x, pool_d, conv_act, (sh2, sc2, g1, g2), w, xmid_ref, h2_tmp, lgt_tmp)
    n_parts = len(tails) // 2
    share = x.shape[0] // n_parts
    for p in range(n_parts):
        h2_ref, lgt_ref = tails[2 * p], tails[2 * p + 1]
        h2_ref[...] = jnp.zeros_like(h2_ref)
        lgt_ref[...] = jnp.zeros_like(lgt_ref)
        h2_ref[0:share, :] = h2_tmp[p * share:(p + 1) * share, :]
        lgt_ref[:, 0:share] = lgt_tmp[:, p * share:(p + 1) * share]


def _full_spec(a):
    nd = a.ndim
    return pl.BlockSpec(a.shape, lambda *_: (0,) * nd)


def _prompt_mixer(x, mod, mod_row0, wl, h2_all, lgt_all, b0, b, w_experts, w_bf_prev, after, *, conv_k):
    n_batch, seq, d = x.shape
    w_gate, w_up, w_down = w_experts
    n_exp, _, de = w_gate.shape
    pw = wl["pool"].shape[0]
    cw = wl["dw"].shape[1]
    ne = wl["router_t"].shape[0]
    tl = SEQ_TILE
    nl = seq // tl
    pg = pw // len(POOL_WINDOWS)
    assert POOL_WINDOWS == (2, 4, 8, 16) and conv_k - 1 <= A_HALO
    ws = [wl[n] for n in _W_NAMES]
    kern = functools.partial(_prompt_mixer_kernel, tl=tl, d=d, pw=pw, cw=cw, conv_k=conv_k, mod_row0=mod_row0 + b0)
    ecs = n_exp // (n_batch * nl)
    assert ecs * n_batch * nl == n_exp

    def w_map(i, j):
        return ((i + b0) * nl + j, 0, 0)

    ins = [x, mod, *ws, h2_all, lgt_all, w_gate, w_up, w_down]
    in_specs = ([pl.BlockSpec((1, tl, d), lambda i, j: (i + b0, j, 0)),
                 pl.BlockSpec((SUBLANES, mod.shape[-1]), lambda i, j: ((i + b0 + mod_row0) // SUBLANES, 0))]
                + [_full_spec(a) for a in ws]
                + [pl.BlockSpec(memory_space=pl.ANY), pl.BlockSpec(memory_space=pl.ANY),
                   pl.BlockSpec((ecs, d, de), w_map), pl.BlockSpec((ecs, d, de), w_map),
                   pl.BlockSpec((ecs, de, d), w_map)])
    aliases = {2 + len(ws): 1, 3 + len(ws): 2}
    if w_bf_prev is not None:
        aliases.update({len(ins): 5, len(ins) + 1: 6})
        ins += list(w_bf_prev)
        in_specs += [pl.BlockSpec(memory_space=pl.ANY), pl.BlockSpec(memory_space=pl.ANY)]
    if after is not None:
        ins.append(after)
        in_specs.append(pl.BlockSpec(memory_space=pl.ANY))
    return pl.pallas_call(
        kern,
        grid=(b, nl),
        in_specs=in_specs,
        out_specs=[pl.BlockSpec((1, tl, d), lambda i, j: (i, j, 0)),
                   pl.BlockSpec((tl, d // 2), lambda i, j: (i * nl + j, 0)),
                   pl.BlockSpec((ne, tl), lambda i, j: (0, i * nl + j)),
                   pl.BlockSpec((1, max(POOL_WINDOWS) - 1, pw), lambda i, j: (i, 0, 0)),
                   pl.BlockSpec((1, conv_k - 1, cw), lambda i, j: (i, 0, 0)),
                   pl.BlockSpec((ecs, d, 2 * de), w_map), pl.BlockSpec((ecs, de, d), w_map)],
        out_shape=[jax.ShapeDtypeStruct((b, seq, d), F32),
                   jax.ShapeDtypeStruct(h2_all.shape, I32),
                   jax.ShapeDtypeStruct(lgt_all.shape, F32),
                   jax.ShapeDtypeStruct((b, max(POOL_WINDOWS) - 1, pw), F32),
                   jax.ShapeDtypeStruct((b, conv_k - 1, cw), F32),
                   jax.ShapeDtypeStruct((n_exp, d, 2 * de), BF16),
                   jax.ShapeDtypeStruct((n_exp, de, d), BF16)],
        scratch_shapes=[pltpu.VMEM((U_HALO + tl, pw), F32), pltpu.VMEM((U_HALO + tl, pw), F32),
                        pltpu.VMEM((U_HALO + tl, pw - pg), F32), pltpu.VMEM((U_HALO + tl, pw - 2 * pg), F32),
                        pltpu.VMEM((A_HALO + tl, cw), F32), pltpu.VMEM((SUBLANES - 1, A_HALO + tl, cw), F32),
                        pltpu.VMEM((tl, pw), BF16), pltpu.VMEM((tl, cw), BF16),
                        pltpu.VMEM((conv_k + 1, SUBLANES, cw), F32)],
        compiler_params=pltpu.CompilerParams(dimension_semantics=("arbitrary", "arbitrary"),
                                             vmem_limit_bytes=VMEM_LIMIT),
        input_output_aliases=aliases,
        name="prompt_mixer",
    )(*ins)


def _sample_mixer(x, mod, sp, sc, wl, t_prompt, t_pad, *, conv_k):
    rows, d = x.shape
    tail = t_pad - t_prompt
    assert t_prompt % tail == 0 and rows % N_PARTS == 0 and rows // N_PARTS <= tail
    pw = wl["pool"].shape[0]
    cw = wl["dw"].shape[1]
    ne = wl["router_t"].shape[0]
    ws = [wl[n] for n in _W_NAMES]
    kern = functools.partial(_sample_mixer_kernel, d=d, pw=pw, cw=cw, conv_k=conv_k)
    ins = [x, mod, sp, sc] + ws
    tail_specs = [pl.BlockSpec((tail, d // 2), lambda i: (t_prompt // tail, 0)),
                  pl.BlockSpec((ne, tail), lambda i: (0, t_prompt // tail))] * N_PARTS
    tail_shapes = [jax.ShapeDtypeStruct((t_pad, d // 2), I32), jax.ShapeDtypeStruct((ne, t_pad), F32)] * N_PARTS
    outs = pl.pallas_call(
        kern,
        grid=(1,),
        in_specs=[_full_spec(x), pl.BlockSpec((rows, mod.shape[1]), lambda i: (0, 0))]
                 + [_full_spec(a) for a in ins[2:]],
        out_specs=[pl.BlockSpec((rows, d), lambda i: (0, 0)), _full_spec(sp), _full_spec(sc)] + tail_specs,
        out_shape=[jax.ShapeDtypeStruct((rows, d), F32),
                   jax.ShapeDtypeStruct(sp.shape, F32),
                   jax.ShapeDtypeStruct(sc.shape, F32)] + tail_shapes,
        scratch_shapes=[pltpu.VMEM((rows, d // 2), I32), pltpu.VMEM((ne, rows), F32)],
        compiler_params=pltpu.CompilerParams(dimension_semantics=("arbitrary",),
                                             vmem_limit_bytes=VMEM_LIMIT),
        name="sample_mixer",
    )(*ins)
    return outs[0], outs[1], outs[2], [(outs[3 + 2 * p], outs[4 + 2 * p]) for p in range(N_PARTS)]


def _sublane_max(x):
    return jnp.max(x, axis=0, keepdims=True)


def _route_kernel(lg_ref, b_ref, tri_ref, idx_ref, wt_ref, rank_ref, cnt_ref, carry, *, t_valid):
    i = pl.program_id(0)
    ne, tt = lg_ref.shape
    ng = N_EXPERT_GROUPS
    per = ne // ng

    @pl.when(i == 0)
    def _():
        carry[...] = jnp.zeros_like(carry)

    s = _sigmoid(lg_ref[...])
    sel = s + b_ref[...]
    s3 = [s[p * ng:(p + 1) * ng, :] for p in range(per)]
    sel3 = [sel[p * ng:(p + 1) * ng, :] for p in range(per)]
    m1 = sel3[0]
    m2 = jnp.full_like(m1, NEG_INF)
    for p in range(1, per):
        m2 = jnp.maximum(m2, jnp.minimum(m1, sel3[p]))
        m1 = jnp.maximum(m1, sel3[p])
    gs = m1 + m2
    gi = lax.broadcasted_iota(jnp.int32, (ng, tt), 0)
    beaten = jnp.zeros((ng, tt), jnp.int32)
    for g in range(ng):
        row = gs[g:g + 1, :]
        beats = (row > gs) | ((row == gs) & (gi > g))
        beaten = beaten + beats.astype(jnp.int32)
    keep = beaten < TOPK_GROUPS
    cur = [jnp.where(keep, sel3[p], NEG_INF) for p in range(per)]
    eid = [(gi * per + p).astype(F32) for p in range(per)]
    idxs, wts, hits = [], [], []
    for _ in range(TOP_K):
        m = cur[0]
        for p in range(1, per):
            m = jnp.maximum(m, cur[p])
        m = _sublane_max(m)
        cand = jnp.where(cur[0] == m, eid[0], float(ne))
        for p in range(1, per):
            cand = jnp.minimum(cand, jnp.where(cur[p] == m, eid[p], float(ne)))
        e_sel = jnp.min(cand, axis=0, keepdims=True)
        hit = [eid[p] == e_sel for p in range(per)]
        wk = jnp.where(hit[0], s3[0], 0.0)
        for p in range(1, per):
            wk = wk + jnp.where(hit[p], s3[p], 0.0)
        wts.append(jnp.sum(wk, axis=0, keepdims=True))
        cur = [jnp.where(hit[p], NEG_INF, cur[p]) for p in range(per)]
        idxs.append(e_sel)
        hits.append(hit)
    wsum = wts[0]
    for k in range(1, TOP_K):
        wsum = wsum + wts[k]
    chosen = [functools.reduce(lambda a, b: a | b, [hits[k][p] for k in range(TOP_K)]) for p in range(per)]
    real = (i * tt + lax.broadcasted_iota(I32, (1, tt), 1)) < t_valid
    onehot = jnp.concatenate([(c & real).astype(F32) for c in chosen], axis=0)
    before = _dot(onehot.astype(BF16), tri_ref[...]) + carry[...]
    for k in range(TOP_K):
        rk = jnp.where(hits[k][0], before[0:ng, :], 0.0)
        for p in range(1, per):
            rk = rk + jnp.where(hits[k][p], before[p * ng:(p + 1) * ng, :], 0.0)
        rank_ref[k:k + 1, :] = jnp.sum(rk, axis=0, keepdims=True).astype(jnp.int32)
        idx_ref[k:k + 1, :] = idxs[k].astype(jnp.int32)
        wt_ref[k:k + 1, :] = wts[k] / wsum * ROUTED_SCALE
    carry[...] = carry[...] + jnp.sum(onehot, axis=1, keepdims=True)
    cnt_ref[...] = carry[...]


def _route(lgt, b_perm, t_valid):
    ne, t = lgt.shape
    tt = ROUTE_TILE
    tri = (jnp.arange(tt)[:, None] < jnp.arange(tt)[None, :]).astype(BF16)
    return pl.pallas_call(
        functools.partial(_route_kernel, t_valid=t_valid),
        grid=(t // tt,),
        in_specs=[pl.BlockSpec((ne, tt), lambda i: (0, i)),
                  pl.BlockSpec((ne, 1), lambda i: (0, 0)),
                  pl.BlockSpec((tt, tt), lambda i: (0, 0))],
        out_specs=[pl.BlockSpec((TOP_K, tt), lambda i: (0, i)),
                   pl.BlockSpec((TOP_K, tt), lambda i: (0, i)),
                   pl.BlockSpec((TOP_K, tt), lambda i: (0, i)),
                   pl.BlockSpec((ne, 1), lambda i: (0, 0))],
        out_shape=[jax.ShapeDtypeStruct((TOP_K, t), jnp.int32),
                   jax.ShapeDtypeStruct((TOP_K, t), F32),
                   jax.ShapeDtypeStruct((TOP_K, t), jnp.int32),
                   jax.ShapeDtypeStruct((ne, 1), F32)],
        scratch_shapes=[pltpu.VMEM((ne, 1), F32)],
        compiler_params=pltpu.CompilerParams(dimension_semantics=("arbitrary",),
                                             vmem_limit_bytes=VMEM_LIMIT),
        name="route",
    )(lgt, b_perm, tri)


def _dest_kernel(pstart_ref, idx_ref, rank_ref, dest_ref, *, t_valid, spare_row):
    idx = idx_ref[...]
    k_top, tile = idx.shape

    def add_start(e, acc):
        return acc + jnp.where(idx == e, pstart_ref[e], 0)

    dest = lax.fori_loop(0, pstart_ref.shape[0], add_start, rank_ref[...])
    tok = pl.program_id(0) * tile + lax.broadcasted_iota(I32, idx.shape, 1)
    spare = spare_row + (tok - t_valid) * k_top + lax.broadcasted_iota(I32, idx.shape, 0)
    dest_ref[...] = jnp.where(tok < t_valid, dest, spare)


def _dest_rows(pstart, idx, rank, t_valid, spare_row):
    k_top, t = idx.shape
    tile = DEST_TILE
    spec = pl.BlockSpec((k_top, tile), lambda i, ps: (0, i))
    return pl.pallas_call(
        functools.partial(_dest_kernel, t_valid=t_valid, spare_row=spare_row),
        grid_spec=pltpu.PrefetchScalarGridSpec(num_scalar_prefetch=1, grid=(t // tile,),
                                               in_specs=[spec, spec], out_specs=spec),
        out_shape=jax.ShapeDtypeStruct((k_top, t), I32),
        compiler_params=pltpu.CompilerParams(dimension_semantics=("arbitrary",),
                                             vmem_limit_bytes=VMEM_LIMIT),
        name="dest_rows",
    )(pstart, idx, rank)


def _expert_kernel(first_ref, valid_ref, xs_hbm, wgu_ref, wd_ref, y_hbm, xbuf, ybuf, in_sem, out_sem):
    step = pl.program_id(0)
    ne = first_ref.shape[0] - 1
    nbuf, rows = xbuf.shape[:2]
    piece = rows // EXPERT_PIECES
    per_step, de = wd_ref.shape[:2]
    n_total = first_ref[ne]

    def in_copy(g, h):
        return pltpu.make_async_copy(xs_hbm.at[pl.ds(g * rows + h * piece, piece)],
                                     xbuf.at[g % nbuf, pl.ds(h * piece, piece)], in_sem.at[g % nbuf])

    def out_copy(g, h):
        return pltpu.make_async_copy(ybuf.at[g % nbuf, pl.ds(h * piece, piece)],
                                     y_hbm.at[pl.ds(g * rows + h * piece, piece)], out_sem.at[g % nbuf])

    def used_pieces(copy, g, act):
        def go(h):
            if act == "start":
                copy(g, h).start(priority=h % 2)
            else:
                copy(g, h).wait()

        go(0)
        for h in range(1, EXPERT_PIECES):
            @pl.when(valid_ref[g] > h * piece)
            def _():
                go(h)

    @pl.when(step == 0)
    def _():
        for g in range(nbuf - 1):
            @pl.when(g < n_total)
            def _():
                used_pieces(in_copy, g, "start")

    def block(g, carry, *, j):
        slot = g % nbuf

        @pl.when(g + nbuf - 1 < n_total)
        def _():
            used_pieces(in_copy, g + nbuf - 1, "start")

        used_pieces(in_copy, g, "wait")

        @pl.when(g >= nbuf)
        def _():
            used_pieces(out_copy, g - nbuf, "wait")

        valid = valid_ref[g]

        def run(n):
            rid = lax.broadcasted_iota(I32, (n, 1), 0)
            xs = jnp.where(rid < valid, _unpack_bf16_pairs(xbuf[slot, 0:n]), 0.0).astype(BF16)
            gu = _dot(xs, wgu_ref[j])
            hb = _silu(gu[:, :de]) * gu[:, de:]
            ybuf[slot, 0:n] = _pack_bf16_pairs(_dot(hb.astype(BF16), wd_ref[j]))

        for h in range(EXPERT_PIECES):
            @pl.when((valid > h * piece) & (valid <= (h + 1) * piece))
            def _():
                run((h + 1) * piece)

        used_pieces(out_copy, g, "start")
        return carry

    for j in range(per_step):
        e = step * per_step + j
        lax.fori_loop(first_ref[e], first_ref[e + 1], functools.partial(block, j=j), 0)

    @pl.when(step == pl.num_programs(0) - 1)
    def _():
        for back in range(nbuf, 0, -1):
            @pl.when(n_total >= back)
            def _():
                used_pieces(out_copy, n_total - back, "wait")


def _experts(xs, w_gu, w_down, first_block, block_valid):
    p, dw = xs.shape
    ne, de, d = w_down.shape
    return pl.pallas_call(
        _expert_kernel,
        grid_spec=pltpu.PrefetchScalarGridSpec(
            num_scalar_prefetch=2,
            grid=(ne // EXPERTS_PER_STEP,),
            in_specs=[pl.BlockSpec(memory_space=pl.ANY),
                      pl.BlockSpec((EXPERTS_PER_STEP, d, 2 * de), lambda e, fb, cn: (e, 0, 0)),
                      pl.BlockSpec((EXPERTS_PER_STEP, de, d), lambda e, fb, cn: (e, 0, 0))],
            out_specs=pl.BlockSpec(memory_space=pl.ANY),
            scratch_shapes=[pltpu.VMEM((EXPERT_BUFFERS, EXPERT_ROWS, dw), I32),
                            pltpu.VMEM((EXPERT_BUFFERS, EXPERT_ROWS, dw), I32),
                            pltpu.SemaphoreType.DMA((EXPERT_BUFFERS,)), pltpu.SemaphoreType.DMA((EXPERT_BUFFERS,))]),
        out_shape=jax.ShapeDtypeStruct((p, dw), I32),
        compiler_params=pltpu.CompilerParams(dimension_semantics=("arbitrary",),
                                             vmem_limit_bytes=VMEM_LIMIT),
        name="experts",
    )(first_block, block_valid, xs, w_gu, w_down)


def _sc_first_chunk(n_chunks):
    per_worker = n_chunks // SC_WORKERS
    assert per_worker * SC_WORKERS == n_chunks
    return per_worker, (lax.axis_index("s") * SC_CORES + lax.axis_index("c")) * per_worker


def _sc_start(copies):
    for cp in copies:
        cp.start()


def _sc_wait(copies):
    for cp in copies:
        cp.wait()


def _sc_dispatch(h2, dest, n_rows):
    w = h2.shape[1]
    n_chunks, k_top, ch = dest.shape
    mesh = plsc.VectorSubcoreMesh(core_axis_name="c", subcore_axis_name="s")

    @functools.partial(
        pl.kernel, mesh=mesh, out_type=jax.ShapeDtypeStruct((n_rows, w), I32),
        scratch_types=[pltpu.VMEM((2, k_top, ch), I32), pltpu.VMEM((2, ch, w), I32),
                       pltpu.SemaphoreType.DMA((2,)), pltpu.SemaphoreType.DMA((2,))])
    def dispatch(h2_hbm, dest_hbm, xs_hbm, idx_v, rows_v, load_sem, store_sem):
        per_worker, first = _sc_first_chunk(n_chunks)

        def loads(i):
            slot = i % 2
            return (pltpu.make_async_copy(h2_hbm.at[pl.ds((first + i) * ch, ch)], rows_v.at[slot], load_sem.at[slot]),
                    pltpu.make_async_copy(dest_hbm.at[first + i], idx_v.at[slot], load_sem.at[slot]))

        def scatters(i):
            slot = i % 2
            return [pltpu.make_async_copy(rows_v.at[slot], xs_hbm.at[idx_v.at[slot, k]], store_sem.at[slot])
                    for k in range(k_top)]

        for i in range(min(2, per_worker)):
            _sc_start(loads(i))
        for i in range(per_worker):
            _sc_wait(loads(i))
            _sc_start(scatters(i))
            if 1 <= i < per_worker - 1:
                _sc_wait(scatters(i - 1))
                _sc_start(loads(i + 1))
        for i in range(max(per_worker - 2, 0), per_worker):
            _sc_wait(scatters(i))

    return dispatch(h2, dest)


def _sc_combine_gather(y, dest, t):
    w = y.shape[1]
    n_chunks, k_top, ch = dest.shape
    n_local = n_chunks // SC_WORKERS
    mesh = plsc.VectorSubcoreMesh(core_axis_name="c", subcore_axis_name="s")

    @functools.partial(
        pl.kernel, mesh=mesh, out_type=jax.ShapeDtypeStruct((k_top, t, w), I32),
        scratch_types=[pltpu.VMEM((n_local, k_top, ch), I32), pltpu.VMEM((k_top, ch, w), I32),
                       pltpu.SemaphoreType.DMA((k_top,)), pltpu.SemaphoreType.DMA((k_top,))])
    def gather(y_hbm, dest_hbm, out_hbm, idx_v, rows_v, load_sem, store_sem):
        per_worker, first = _sc_first_chunk(n_chunks)
        pltpu.sync_copy(dest_hbm.at[pl.ds(first, per_worker)], idx_v)

        def fetch(i, k):
            return pltpu.make_async_copy(y_hbm.at[idx_v.at[i, k]], rows_v.at[k], load_sem.at[k])

        def put(i, k):
            return pltpu.make_async_copy(rows_v.at[k], out_hbm.at[k, pl.ds((first + i) * ch, ch)], store_sem.at[k])

        @pl.loop(0, per_worker)
        def _(i):
            for k in range(k_top):
                @pl.when(i > 0)
                def _():
                    put(i - 1, k).wait()

                fetch(i, k).start()
            for k in range(k_top):
                fetch(i, k).wait()
                put(i, k).start()

        for k in range(k_top):
            put(per_worker - 1, k).wait()

    return gather(y, dest)


def _final_kernel(xmid_ref, yg_ref, wt_ref, g2_ref, shf_ref, scf_ref, gf_ref, *rest, vec_row0=None):
    o_ref = rest[-1]
    if vec_row0 is not None:
        row = pl.ds((vec_row0 + pl.program_id(0)) % SUBLANES, 1)
        g2_ref, shf_ref, scf_ref = g2_ref.at[row, :], shf_ref.at[row, :], scf_ref.at[row, :]
    wt = wt_ref[...].T[:xmid_ref.shape[0]]
    routed = wt[:, 0:1] * _unpack_bf16_pairs(yg_ref[0])
    for k in range(1, yg_ref.shape[0]):
        routed = routed + wt[:, k:k + 1] * _unpack_bf16_pairs(yg_ref[k])
    x2 = xmid_ref[...] + g2_ref[...] * routed
    o_ref[...] = _rms_norm(x2, gf_ref[...]) * (1.0 + scf_ref[...]) + shf_ref[...]


def _final_prompt(xmid, yg, wts_t, mod, modf, mod_row0, g_final, b0, n_batch, out_prev):
    b, seq, d = xmid.shape
    k_top, _, w = yg.shape
    tl = FINAL_TILE
    nl = seq // tl

    def vec(col):
        return pl.BlockSpec((SUBLANES, d), lambda i, j: ((i + b0 + mod_row0) // SUBLANES, col))

    in_specs = [pl.BlockSpec((None, tl, d), lambda i, j: (i, j, 0)),
                pl.BlockSpec((k_top, tl, w), lambda i, j: (0, i * nl + j, 0)),
                pl.BlockSpec((k_top, tl), lambda i, j: (0, i * nl + j)),
                vec(5), vec(0), vec(1), pl.BlockSpec((1, d), lambda i, j: (0, 0))]
    args = [xmid, yg, wts_t, mod, modf, modf, g_final]
    aliases = {}
    if out_prev is not None:
        in_specs.append(pl.BlockSpec(memory_space=pl.ANY))
        args.append(out_prev)
        aliases = {7: 0}
    return pl.pallas_call(
        functools.partial(_final_kernel, vec_row0=b0 + mod_row0),
        grid=(b, nl),
        in_specs=in_specs,
        out_specs=pl.BlockSpec((None, tl, d), lambda i, j: (i + b0, j, 0)),
        out_shape=jax.ShapeDtypeStruct((n_batch, seq, d), F32),
        compiler_params=pltpu.CompilerParams(dimension_semantics=("arbitrary", "arbitrary"),
                                             vmem_limit_bytes=VMEM_LIMIT),
        input_output_aliases=aliases,
        name="final_prompt",
    )(*args)


def _final_sample(xmid_all, yg, wts_t, mod, modf, g_final, part, rows, first_row, after):
    d = xmid_all.shape[1]
    k_top, _, w = yg.shape

    def vec(col):
        return pl.BlockSpec((rows, d), lambda i: (part, col))

    return pl.pallas_call(
        _final_kernel,
        grid=(1,),
        in_specs=[vec(0),
                  pl.BlockSpec((k_top, rows, w), lambda i: (0, first_row // rows, 0)),
                  pl.BlockSpec((k_top, LANES), lambda i: (0, first_row // LANES)),
                  vec(5), vec(0), vec(1), pl.BlockSpec((1, d), lambda i: (0, 0)), pl.BlockSpec(memory_space=pl.ANY)],
        out_specs=pl.BlockSpec((rows, d), lambda i: (0, 0)),
        out_shape=jax.ShapeDtypeStruct((rows, d), F32),
        compiler_params=pltpu.CompilerParams(dimension_semantics=("arbitrary",),
                                             vmem_limit_bytes=VMEM_LIMIT),
        name="final_sample",
    )(xmid_all, yg, wts_t, mod, modf, modf, g_final, after)


def _block_diag_pairs(w_pool):
    g, c, _ = w_pool.shape
    eye = jnp.eye(g, dtype=w_pool.dtype)
    return (eye[:, None, :, None] * w_pool[:, :, None, :]).reshape(g * c, g * c)


def kernel(x_prompt, x_sample, state_pool, state_conv, c_prompt, c_sample, w_ada, b_ada, g_mix, w_in, w_pool, pool_scale, w_dw, b_dw, ln_g, ln_b, w_out, g_ffn, w_router, b_router, w_gate, w_up, w_down, ws_gate, ws_up, ws_down, w_ada_final, b_ada_final, g_final):
    bp, seq, d = x_prompt.shape
    bs = x_sample.shape[0]
    depth = w_ada.shape[0]
    assert depth == 1 and x_sample.shape[1] == 1
    conv_k = w_dw.shape[1]
    ne = w_router.shape[-1]
    per = ne // N_EXPERT_GROUPS

    row_expert = (jnp.arange(ne) % N_EXPERT_GROUPS) * per + jnp.arange(ne) // N_EXPERT_GROUPS
    wl = {
        "g_mix": g_mix[0][None, :],
        "in": w_in[0].astype(BF16),
        "pool": _block_diag_pairs(w_pool[0]).astype(BF16),
        "pool_scale": pool_scale[0][None, :],
        "dw": w_dw[0],
        "b_dw": b_dw[0][None, :],
        "ln_g": ln_g[0][None, :],
        "ln_b": ln_b[0][None, :],
        "out": w_out[0].astype(BF16),
        "g_ffn": g_ffn[0][None, :],
        "router_t": w_router[0].T[row_expert].astype(BF16),
        "s_gu": jnp.concatenate([ws_gate[0], ws_up[0]], axis=1).astype(BF16),
        "s_down": ws_down[0].astype(BF16),
    }
    b_perm = b_router[0][row_expert][:, None]

    assert bp % N_PARTS == 0 and bs % N_PARTS == 0
    bpp, bsp = bp // N_PARTS, bs // N_PARTS
    tpp = bpp * seq
    t_part = tpp + bsp
    grain = SC_WORKERS * SC_DISPATCH_CHUNK * SC_COMBINE_CHUNK // math.gcd(SC_DISPATCH_CHUNK, SC_COMBINE_CHUNK)
    t_pad = (t_part + grain - 1) // grain * grain
    assert t_pad % ROUTE_TILE == 0 and t_pad % DEST_TILE == 0 and tpp % bsp == 0

    c_all = jnp.concatenate([c_sample, c_prompt], axis=0)
    mod = _ada(c_all, w_ada[0], b_ada[0])
    modf = _ada(c_all, w_ada_final, b_ada_final)

    xmid_s, new_pool_t, new_conv_t, tails = _sample_mixer(
        x_sample.reshape(bs, d), mod, jnp.transpose(state_pool[0], (1, 0, 2)),
        jnp.transpose(state_conv[0], (1, 0, 2)), wl, tpp, t_pad, conv_k=conv_k)
    new_pool_s = jnp.transpose(new_pool_t, (1, 0, 2))
    new_conv_s = jnp.transpose(new_conv_t, (1, 0, 2))
    w_experts = tuple(a.reshape(a.shape[1:]) for a in (w_gate, w_up, w_down))

    n_blocks = (t_part * TOP_K + EXPERT_ROWS - 1) // EXPERT_ROWS + ne
    n_spare = -(-(t_pad - t_part) * TOP_K // EXPERT_ROWS)
    n_rows = (n_blocks + n_spare) * EXPERT_ROWS

    y_prompt, y_samples, npools, nconvs, w_bf = None, [], [], [], None
    mixed, after = [], modf
    for p in range(N_PARTS):
        h2, lgt = tails[p]
        xmid_p, h2, lgt, npool_p, nconv_p, *w_bf = _prompt_mixer(
            x_prompt, mod, bs, wl, h2, lgt, p * bpp, bpp, w_experts, w_bf, after, conv_k=conv_k)
        npools.append(npool_p)
        nconvs.append(nconv_p)
        idx, wts, rank, counts_perm = _route(lgt, b_perm, t_part)
        counts = counts_perm.astype(I32).reshape(per, N_EXPERT_GROUPS).T.reshape(ne)
        nblk = (counts + EXPERT_ROWS - 1) // EXPERT_ROWS
        first_block = jnp.concatenate([jnp.zeros((1,), I32), jnp.cumsum(nblk).astype(I32)])
        dest = _dest_rows(first_block[:ne] * EXPERT_ROWS, idx, rank, t_part, n_blocks * EXPERT_ROWS)
        blk = jnp.arange(n_blocks, dtype=I32)[:, None]
        inside = (blk >= first_block[None, :ne]) & (blk < first_block[None, 1:])
        rows_left = counts[None, :] - (blk - first_block[None, :ne]) * EXPERT_ROWS
        block_valid = jnp.sum(jnp.where(inside, jnp.minimum(rows_left, EXPERT_ROWS), 0), axis=1)
        mixed.append((xmid_p, h2, wts, dest, first_block, block_valid))
        after = dest
    for p in range(N_PARTS):
        xmid_p, h2, wts, dest, first_block, block_valid = mixed[p]

        def chunked(ch):
            return dest.reshape(TOP_K, t_pad // ch, ch).transpose(1, 0, 2)

        xs = _sc_dispatch(h2, chunked(SC_DISPATCH_CHUNK), n_rows)
        y = _experts(xs, w_bf[0], w_bf[1], first_block, block_valid)
        yg = _sc_combine_gather(y, chunked(SC_COMBINE_CHUNK), t_pad)

        y_samples.append(_final_sample(xmid_s, yg, wts, mod, modf, g_final[None, :], p, bsp, tpp,
                                       g_final if y_prompt is None else y_prompt))
        y_prompt = _final_prompt(xmid_p, yg, wts, mod, modf, bs, g_final[None, :], p * bpp, bp, y_prompt)
    y_sample = jnp.concatenate(y_samples, axis=0)
    npool_p = jnp.concatenate(npools, axis=0)
    nconv_p = jnp.concatenate(nconvs, axis=0)

    return (y_prompt, y_sample[:, None, :], npool_p[None], nconv_p[None], new_pool_s[None], new_conv_s[None])
```
